```python
import jax, jax.numpy as jnp
from jax import lax
import numpy as np

D_MODEL = 1024
BATCH = 8
SEQ = 2048
DEPTH = 1
DEC_BATCH = 128
DEC_SEQ = 4
PAST_LEN = 16384
PAGE_SIZE = 128

CHUNK = 128
D_A = D_MODEL
G_A = 8
C_A = D_A // G_A
HEAD_B = 64
D_B = D_MODEL // 2
H_B = D_B // HEAD_B
R_W = 64
R_A = 64
R_G = 128
SHIFT_W = 3 * D_B + R_W + R_A + R_G
IN_COLS = SHIFT_W + 2 * D_A + 2 * D_MODEL
N_EXPERTS = 32
TOP_K = 4
D_FF = D_MODEL
SWIGLU_LIMIT = 7.0
SWIGLU_ALPHA = 1.702
EPS = 1e-5
GN_EPS = HEAD_B * 1e-5

kernel_name = "gmlp_rwkv7_gated_moe_step"

F32 = jnp.float32


def rms_norm(x, g):
    xf = x.astype(F32)
    y = xf * lax.rsqrt(jnp.mean(xf * xf, axis=-1, keepdims=True) + EPS)
    return (y * g.astype(F32)).astype(x.dtype)


def layer_norm(x, g, b):
    xf = x.astype(F32)
    mu = jnp.mean(xf, axis=-1, keepdims=True)
    var = jnp.mean(jnp.square(xf - mu), axis=-1, keepdims=True)
    return ((xf - mu) * lax.rsqrt(var + EPS) * g.astype(F32) + b.astype(F32)).astype(x.dtype)


def spatial_gating(u, vn, w_spatial, b_spatial):
    bsz, t, _ = vn.shape
    n_chunks = -(-t // CHUNK)
    pad = n_chunks * CHUNK - t
    vp = jnp.pad(vn, ((0, 0), (0, pad), (0, 0))).reshape(bsz, n_chunks, CHUNK, G_A, C_A)
    causal = jnp.tril(jnp.ones((CHUNK, CHUNK), dtype=bool))
    ws = jnp.where(causal[None], w_spatial, 0).astype(vn.dtype)
    mixed = jnp.einsum('gts,bcsgd->bctgd', ws, vp) + b_spatial.T[:, :, None].astype(vn.dtype)
    mixed = mixed.reshape(bsz, n_chunks * CHUNK, D_A)[:, :t]
    return u * mixed


def rwkv7_branch(cur, prev, mu_shift, wkv0, w0, w2, a0, a2, g2, k_k, k_a, r_k, lnx_g, lnx_b):
    bsz, t, _ = cur.shape
    xs = cur + (prev - cur) * mu_shift
    r, k, v, xw, xa, xg = jnp.split(
        xs, [D_B, 2 * D_B, 3 * D_B, 3 * D_B + R_W, 3 * D_B + R_W + R_A], axis=-1)
    w = -jax.nn.softplus(-(w0 + jnp.tanh(xw) @ w2).astype(F32)) - 0.5
    decay = jnp.exp(-jnp.exp(w))
    a = jax.nn.sigmoid(a0 + xa @ a2)
    g = jax.nn.sigmoid(xg) @ g2
    heads = lambda z: z.reshape(bsz, t, H_B, HEAD_B).astype(F32)
    kk = heads(k * k_k)
    kk = kk / jnp.maximum(jnp.sqrt(jnp.sum(kk * kk, axis=-1, keepdims=True)), 1e-12)
    k = k * (1 + (a - 1) * k_a)
    r_h, k_h, v_h, a_h, w_h = heads(r), heads(k), heads(v), heads(a), heads(decay)

    def step(S, inp):
        r_t, w_t, k_t, v_t, kk_t, a_t = inp
        sa = jnp.einsum('bhij,bhj->bhi', S, -kk_t)
        S = (S * w_t[:, :, None, :] + sa[..., None] * (kk_t * a_t)[:, :, None, :]
             + v_t[..., None] * k_t[:, :, None, :])
        return S, jnp.einsum('bhij,bhj->bhi', S, r_t)

    seq_first = lambda z: jnp.moveaxis(z, 1, 0)
    xs_scan = (seq_first(r_h), seq_first(w_h), seq_first(k_h), seq_first(v_h), seq_first(kk), seq_first(a_h))
    S_fin, ys = lax.scan(step, wkv0.astype(F32), xs_scan)
    y = jnp.moveaxis(ys, 0, 1)
    mu = jnp.mean(y, axis=-1, keepdims=True)
    var = jnp.mean(jnp.square(y - mu), axis=-1, keepdims=True)
    y = ((y - mu) * lax.rsqrt(var + GN_EPS)).reshape(bsz, t, D_B) * lnx_g.astype(F32) + lnx_b.astype(F32)
    bonus = jnp.sum(r_h * k_h * r_k.astype(F32), axis=-1, keepdims=True) * v_h
    y = (y + bonus.reshape(bsz, t, D_B)) * g.astype(F32)
    return y.astype(cur.dtype), S_fin.astype(wkv0.dtype)


def moe(x, router_w, router_b, exp_w_gu, exp_b_gu, exp_w_down, exp_b_down):
    bsz, t, d = x.shape
    xt = x.reshape(-1, d)
    logits = (xt @ router_w + router_b).astype(F32)
    top_v, top_i = lax.top_k(logits, TOP_K)
    probs = jax.nn.softmax(top_v, axis=-1)
    combine = jnp.sum(jax.nn.one_hot(top_i, N_EXPERTS, dtype=F32) * probs[..., None], axis=1)
    out = jnp.zeros((xt.shape[0], d), F32)
    for e in range(N_EXPERTS):
        gu = xt @ exp_w_gu[e] + exp_b_gu[e]
        gate = jnp.minimum(gu[:, :D_FF], SWIGLU_LIMIT)
        up = jnp.clip(gu[:, D_FF:], -SWIGLU_LIMIT, SWIGLU_LIMIT)
        h = (up + 1) * gate * jax.nn.sigmoid(gate * SWIGLU_ALPHA)
        out = out + combine[:, e:e + 1] * (h @ exp_w_down[e] + exp_b_down[e]).astype(F32)
    return out.reshape(bsz, t, d).astype(x.dtype)


def block(x, shift0, wkv0, norm1_g, w_in, mu_shift, vnorm_g, vnorm_b, w_spatial, b_spatial,
          w_a_out, w0, w2, a0, a2, g2, k_k, k_a, r_k, lnx_g, lnx_b, w_b_out, w_out,
          norm2_g, router_w, router_b, exp_w_gu, exp_b_gu, exp_w_down, exp_b_down):
    xn = rms_norm(x, norm1_g)
    proj = xn @ w_in
    cur, u, v, gate_a, gate_b = jnp.split(
        proj, [SHIFT_W, SHIFT_W + D_A, SHIFT_W + 2 * D_A, SHIFT_W + 2 * D_A + D_MODEL], axis=-1)
    prev = jnp.concatenate([shift0[:, None, :].astype(cur.dtype), cur[:, :-1]], axis=1)
    vn = layer_norm(v, vnorm_g, vnorm_b)
    y_a = spatial_gating(u, vn, w_spatial, b_spatial)
    y_b, wkv_new = rwkv7_branch(cur, prev, mu_shift, wkv0, w0, w2, a0, a2, g2, k_k, k_a, r_k, lnx_g, lnx_b)
    merged = jax.nn.sigmoid(gate_a) * (y_a @ w_a_out) + jax.nn.sigmoid(gate_b) * (y_b @ w_b_out)
    h = x + merged @ w_out
    h = h + moe(rms_norm(h, norm2_g), router_w, router_b, exp_w_gu, exp_b_gu, exp_w_down, exp_b_down)
    return h, vn, cur[:, -1], wkv_new


def setup_inputs(seed: int = 0) -> dict:
    key = jax.random.key(seed)
    ks = iter(jax.random.split(key, 40))
    nrm = lambda shape, s: jax.random.normal(next(ks), shape, F32) * s
    gain = lambda shape: 1.0 + nrm(shape, 0.02)
    L = DEPTH
    return {
        "x_prompt": nrm((BATCH, SEQ, D_MODEL), 1.0),
        "x_sample": nrm((DEC_BATCH, DEC_SEQ, D_MODEL), 1.0),
        "state_shift": nrm((L, DEC_BATCH, SHIFT_W), 1.0),
        "state_wkv": nrm((L, DEC_BATCH, H_B, HEAD_B, HEAD_B), 0.3),
        "norm1_g": gain((L, D_MODEL)),
        "w_in": nrm((L, D_MODEL, IN_COLS), D_MODEL ** -0.5),
        "mu_shift": jax.random.uniform(next(ks), (L, SHIFT_W), F32),
        "vnorm_g": gain((L, D_A)),
        "vnorm_b": nrm((L, D_A), 0.02),
        "w_spatial": nrm((L, G_A, CHUNK, CHUNK), CHUNK ** -0.5),
        "b_spatial": gain((L, G_A, CHUNK)),
        "w_a_out": nrm((L, D_A, D_MODEL), D_A ** -0.5),
        "w0": jax.random.uniform(next(ks), (L, D_B), F32, -5.0, -0.5),
        "w2": nrm((L, R_W, D_B), 0.5 * R_W ** -0.5),
        "a0": nrm((L, D_B), 0.1),
        "a2": nrm((L, R_A, D_B), 0.5 * R_A ** -0.5),
        "g2": nrm((L, R_G, D_B), R_G ** -0.5),
        "k_k": 0.85 + nrm((L, D_B), 0.05),
        "k_a": 1.0 + nrm((L, D_B), 0.05),
        "r_k": nrm((L, H_B, HEAD_B), 0.1),
        "lnx_g": gain((L, D_B)),
        "lnx_b": nrm((L, D_B), 0.02),
        "w_b_out": nrm((L, D_B, D_MODEL), D_B ** -0.5),
        "w_out": nrm((L, D_MODEL, D_MODEL), D_MODEL ** -0.5),
        "norm2_g": gain((L, D_MODEL)),
        "router_w": nrm((L, D_MODEL, N_EXPERTS), D_MODEL ** -0.5),
        "router_b": nrm((L, N_EXPERTS), 0.01),
        "exp_w_gu": nrm((L, N_EXPERTS, D_MODEL, 2 * D_FF), D_MODEL ** -0.5),
        "exp_b_gu": nrm((L, N_EXPERTS, 2 * D_FF), 0.02),
        "exp_w_down": nrm((L, N_EXPERTS, D_FF, D_MODEL), D_FF ** -0.5),
        "exp_b_down": nrm((L, N_EXPERTS, D_MODEL), 0.02),
        "normf_g": gain((D_MODEL,)),
    }


def reference(x_prompt, x_sample, state_shift, state_wkv, norm1_g, w_in, mu_shift, vnorm_g, vnorm_b,
              w_spatial, b_spatial, w_a_out, w0, w2, a0, a2, g2, k_k, k_a, r_k, lnx_g, lnx_b,
              w_b_out, w_out, norm2_g, router_w, router_b, exp_w_gu, exp_b_gu, exp_w_down,
              exp_b_down, normf_g):
    layer_params = (norm1_g, w_in, mu_shift, vnorm_g, vnorm_b, w_spatial, b_spatial, w_a_out,
                    w0, w2, a0, a2, g2, k_k, k_a, r_k, lnx_g, lnx_b, w_b_out, w_out,
                    norm2_g, router_w, router_b, exp_w_gu, exp_b_gu, exp_w_down, exp_b_down)

    def trunk(x, shift_in, wkv_in):
        h = x
        vrows, shifts, wkvs = [], [], []
        for layer in range(DEPTH):
            h, vn, sh, s = block(h, shift_in[layer], wkv_in[layer], *[p[layer] for p in layer_params])
            vrows.append(vn)
            shifts.append(sh)
            wkvs.append(s)
        return rms_norm(h, normf_g), jnp.stack(vrows), jnp.stack(shifts), jnp.stack(wkvs)

    bp = x_prompt.shape[0]
    y_prompt, _, shift_prompt, wkv_prompt = trunk(
        x_prompt,
        jnp.zeros((DEPTH, bp, SHIFT_W), state_shift.dtype),
        jnp.zeros((DEPTH, bp, H_B, HEAD_B, HEAD_B), state_wkv.dtype))
    y_sample, chunkv_sample, shift_sample, wkv_sample = trunk(x_sample, state_shift, state_wkv)
    return (y_prompt, y_sample, shift_prompt, wkv_prompt, shift_sample, wkv_sample, chunkv_sample)
```

```python
import functools

import jax
import jax.numpy as jnp
from jax import lax
from jax.experimental import pallas as pl
from jax.experimental.pallas import tpu as pltpu

F32 = jnp.float32
BF16 = jnp.bfloat16

CHUNK = 128
G_A = 8
HEAD_B = 64
R_W, R_A, R_G = 64, 64, 128
TOP_K = 4
SWIGLU_LIMIT = 7.0
SWIGLU_ALPHA = 1.702
EPS = 1e-5
GN_EPS = HEAD_B * 1e-5

LANES = 128
PAIR = 2 * HEAD_B
VMEM_LIMIT = 56 * 1024 * 1024


def _dot(a, b):
    return jnp.dot(a.astype(BF16), b.astype(BF16), preferred_element_type=F32)


def _split(x):
    hi = x.astype(BF16)
    lo = (x - hi.astype(F32)).astype(BF16)
    return hi, lo


def _dot3(a, b):
    ah, al = _split(a)
    bh, bl = _split(b)
    return (jnp.dot(ah, bh, preferred_element_type=F32)
            + jnp.dot(ah, bl, preferred_element_type=F32)
            + jnp.dot(al, bh, preferred_element_type=F32))


def _dot_exact_rhs(a, b_bf16):
    ah, al = _split(a)
    return (jnp.dot(ah, b_bf16, preferred_element_type=F32)
            + jnp.dot(al, b_bf16, preferred_element_type=F32))


def _dot_exact_lhs(a_bf16, b):
    bh, bl = _split(b)
    bm = b - bh.astype(F32) - bl.astype(F32)
    return (jnp.dot(a_bf16, bh, preferred_element_type=F32)
            + jnp.dot(a_bf16, bl, preferred_element_type=F32)
            + jnp.dot(a_bf16, bm.astype(BF16), preferred_element_type=F32))


def _rms(x, g):
    return x * lax.rsqrt(jnp.mean(x * x, axis=-1, keepdims=True) + EPS) * g


def _sigmoid(x):
    return 1.0 / (1.0 + jnp.exp(-x))


def _softplus(z):
    return jnp.maximum(z, 0.0) + jnp.log(1.0 + jnp.exp(-jnp.abs(z)))


def _full_spec(shape):
    nd = len(shape)
    return pl.BlockSpec(shape, lambda *_: (0,) * nd, pipeline_mode=pl.Buffered(1))


def _branch_a_kernel(x_ref, g1_ref, wu_ref, wv_ref, wga_ref, vg_ref, vb_ref, wmix_ref,
                     bias_ref, wao_ref, apart_ref, vn_ref):
    xn = _rms(x_ref[...], g1_ref[...]).astype(BF16)
    v = jnp.dot(xn, wv_ref[...], preferred_element_type=F32)
    mu = jnp.mean(v, axis=-1, keepdims=True)
    d = v - mu
    var = jnp.mean(d * d, axis=-1, keepdims=True)
    vn = d * lax.rsqrt(var + EPS) * vg_ref[...] + vb_ref[...]
    vn_ref[...] = vn
    vnb = vn.astype(BF16)
    c_a = vnb.shape[1] // G_A
    mixed = jnp.concatenate(
        [jnp.dot(wmix_ref[g], vnb[:, g * c_a:(g + 1) * c_a], preferred_element_type=F32)
         for g in range(G_A)], axis=1) + bias_ref[...]
    u = jnp.dot(xn, wu_ref[...], preferred_element_type=F32)
    ya = (u * mixed).astype(BF16)
    ga = jnp.dot(xn, wga_ref[...], preferred_element_type=F32)
    apart_ref[...] = _sigmoid(ga) * jnp.dot(ya, wao_ref[...], preferred_element_type=F32)


def _branch_a(x2d, tm, g1, wu, wv, wga, vg, vb, wmix, bias_full, wao):
    n, d = x2d.shape
    d_a = wu.shape[1]
    tok = lambda w: pl.BlockSpec((tm, w), lambda i: (i, 0))
    return pl.pallas_call(
        _branch_a_kernel,
        grid=(n // tm,),
        in_specs=[tok(d), _full_spec(g1.shape), _full_spec(wu.shape), _full_spec(wv.shape),
                  _full_spec(wga.shape), _full_spec(vg.shape), _full_spec(vb.shape),
                  _full_spec(wmix.shape), _full_spec(bias_full.shape), _full_spec(wao.shape)],
        out_specs=[tok(d), tok(d_a)],
        out_shape=[jax.ShapeDtypeStruct((n, d), F32), jax.ShapeDtypeStruct((n, d_a), F32)],
        compiler_params=pltpu.CompilerParams(dimension_semantics=("parallel",),
                                             vmem_limit_bytes=VMEM_LIMIT),
        name="branch_a",
    )(x2d, g1, wu, wv, wga, vg, vb, wmix, bias_full, wao)


def _branch_b_kernel(x_ref, g1_ref, wcur_ref, wgb_ref, mu_ref, ext_ref, w0_ref, w2_ref,
                     a0_ref, a2_ref, g2_ref, kk_ref, ka_ref, rk_ref, bd_ref,
                     r_out, k_out, v_out, kk_out, kka_out, lw_out, g_out, bonus_out,
                     sgb_out, cur_out, carry_scr, *, seq_rows, d_b):
    tm = x_ref.shape[0]
    xn = _rms(x_ref[...], g1_ref[...]).astype(BF16)
    cur = jnp.dot(xn, wcur_ref[...], preferred_element_type=F32)
    rolled = pltpu.roll(cur, 1, axis=0)
    row = lax.broadcasted_iota(jnp.int32, (tm, 1), 0)
    if seq_rows is None:
        first_tile = pl.program_id(1) == 0
        carry = jnp.where(first_tile, 0.0, carry_scr[...])
        prev = jnp.where(row == 0, carry, rolled)
        carry_scr[...] = cur[tm - 1:tm, :]
        cur_out[...] = cur[tm - 8:tm, :]
    else:
        prev = jnp.where(row % seq_rows == 0, ext_ref[...], rolled)
        cur_out[...] = cur
    xs = cur + (prev - cur) * mu_ref[...]
    r = xs[:, 0:d_b]
    k = xs[:, d_b:2 * d_b]
    v = xs[:, 2 * d_b:3 * d_b]
    o = 3 * d_b
    xw = xs[:, o:o + R_W]
    xa = xs[:, o + R_W:o + R_W + R_A]
    xg = xs[:, o + R_W + R_A:o + R_W + R_A + R_G]
    wl = w0_ref[...] + _dot3(jnp.tanh(xw), w2_ref[...])
    lw = -jnp.exp(-_softplus(-wl) - 0.5)
    a = _sigmoid(a0_ref[...] + _dot3(xa, a2_ref[...]))
    g = _dot3(_sigmoid(xg), g2_ref[...])
    bd = bd_ref[...]
    kkr = k * kk_ref[...]
    n2 = _dot_exact_rhs(kkr * kkr, bd)
    kk = kkr / jnp.maximum(jnp.sqrt(n2), 1e-12)
    k2 = k * (1.0 + (a - 1.0) * ka_ref[...])
    bonus = _dot_exact_rhs(r * k2 * rk_ref[...], bd) * v
    r_out[...] = r
    k_out[...] = k2
    v_out[...] = v
    kk_out[...] = kk
    kka_out[...] = kk * a
    lw_out[...] = lw
    g_out[...] = g
    bonus_out[...] = bonus
    sgb_out[...] = _sigmoid(jnp.dot(xn, wgb_ref[...], preferred_element_type=F32))


def _branch_b(x3d, tm, seq_rows, ext, g1, wcur, wgb, mu, w0, w2, a0, a2, g2, k_k, k_a, r_k, bd):
    nb, t, d = x3d.shape
    d_b = w0.shape[1]
    shift_w = wcur.shape[1]
    nt = t // tm
    tok = lambda w: pl.BlockSpec((None, tm, w), lambda b, i: (b, i, 0))
    cur_rows = 8 if seq_rows is None else tm
    outs = [jax.ShapeDtypeStruct((nb, t, d_b), F32)] * 8 + [
        jax.ShapeDtypeStruct((nb, t, d), F32),
        jax.ShapeDtypeStruct((nb, nt * cur_rows, shift_w), F32)]
    out_specs = [tok(d_b)] * 8 + [tok(d), pl.BlockSpec((None, cur_rows, shift_w),
                                                       lambda b, i: (b, i, 0))]
    weights = (g1, wcur, wgb, mu)
    small = (w0, w2, a0, a2, g2, k_k, k_a, r_k, bd)
    return pl.pallas_call(
        functools.partial(_branch_b_kernel, seq_rows=seq_rows, d_b=d_b),
        grid=(nb, nt),
        in_specs=[tok(d)] + [_full_spec(w.shape) for w in weights]
        + [tok(shift_w) if seq_rows is not None else _full_spec(ext.shape)]
        + [_full_spec(w.shape) for w in small],
        out_specs=out_specs,
        out_shape=outs,
        scratch_shapes=[pltpu.VMEM((1, shift_w), F32)],
        compiler_params=pltpu.CompilerParams(dimension_semantics=("parallel", "arbitrary"),
                                             vmem_limit_bytes=VMEM_LIMIT),
        name="branch_b",
    )(x3d, *weights, ext, *small)


def _scan_kernel(r_ref, k_ref, v_ref, kk_ref, kka_ref, lw_ref, s0_ref, y_ref, sout_ref, s_scr,
                 *, chunk):
    L = chunk
    c = pl.program_id(1)

    @pl.when(c == 0)
    def _():
        s_scr[...] = s0_ref[...]

    n_pairs = s_scr.shape[0]
    row2 = lax.broadcasted_iota(jnp.int32, (2 * L, 1), 0)
    lane = lax.broadcasted_iota(jnp.int32, (1, PAIR), 1)
    stack_mask = (row2 >= L) == (lane >= HEAD_B)
    ri = lax.broadcasted_iota(jnp.int32, (2 * L, 2 * L), 0)
    ci = lax.broadcasted_iota(jnp.int32, (2 * L, 2 * L), 1)
    same = (ri >= L) == (ci >= L)
    strict = same & (ci < ri)
    incl = same & (ci <= ri)
    eye2 = (ri == ci).astype(F32)
    ti = lax.broadcasted_iota(jnp.int32, (L, L), 0)
    tj = lax.broadcasted_iota(jnp.int32, (L, L), 1)
    tri = (tj <= ti).astype(BF16)
    di = lax.broadcasted_iota(jnp.int32, (PAIR, PAIR), 0)
    dj = lax.broadcasted_iota(jnp.int32, (PAIR, PAIR), 1)
    diag = di == dj

    lw_all = lw_ref[...]
    cum_all = _dot_exact_lhs(tri, lw_all)

    def stack(x):
        return jnp.where(stack_mask, jnp.concatenate([x, x], axis=0), 0.0)

    n_double = max(L.bit_length() - 2, 0)
    for p in range(n_pairs):
        sl = slice(p * PAIR, (p + 1) * PAIR)
        lw = lw_all[:, sl]
        cum = cum_all[:, sl]
        cum_last = cum[L - 1:L, :]
        e_cum = jnp.exp(cum)
        e_cumx = jnp.exp(cum - lw)
        e_neg = jnp.exp(-cum)
        e_rel = jnp.exp(cum_last - cum)
        kk = kk_ref[:, sl]
        kka = kka_ref[:, sl]
        kx = k_ref[:, sl]
        ab = stack(-kk * e_cumx)
        rb = stack(r_ref[:, sl] * e_cum)
        bb = stack(kka * e_neg)
        kb = stack(kx * e_neg)
        bt = stack(kka * e_rel)
        kt = stack(kx * e_rel)
        vs = stack(v_ref[:, sl])
        ar = jnp.concatenate([ab, rb], axis=0).astype(BF16)
        bk = jnp.concatenate([bb, kb], axis=0).astype(BF16)
        a_all = lax.dot_general(ar, bk, (((1,), (1,)), ((), ())), preferred_element_type=F32)
        a_ab = jnp.where(strict, a_all[:2 * L, :2 * L], 0.0)
        a_ak = jnp.where(strict, a_all[:2 * L, 2 * L:], 0.0)
        a_rb = jnp.where(incl, a_all[2 * L:, :2 * L], 0.0)
        a_rk = jnp.where(incl, a_all[2 * L:, 2 * L:], 0.0)
        nk = a_ab
        tm = eye2 + nk
        for _ in range(n_double):
            nk = _dot(nk, nk)
            tm = tm + _dot(tm, nk)
        akv = _dot(a_ak, vs)
        tx = _dot(tm, jnp.concatenate([ab, akv], axis=1))
        ta = tx[:, :PAIR]
        u0 = tx[:, PAIR:]
        ry = _dot(a_rb, tx)
        rt = rb + ry[:, :PAIR]
        y0 = ry[:, PAIR:] + _dot(a_rk, vs)
        btk = jnp.concatenate([bt.T, kt.T], axis=1)
        mc = _dot(btk, jnp.concatenate(
            [jnp.concatenate([ta, u0], axis=1),
             jnp.concatenate([jnp.zeros_like(vs), vs], axis=1)], axis=0))
        m = mc[:, :PAIR] + jnp.where(diag, jnp.exp(cum_last), 0.0)
        cc = mc[:, PAIR:]
        rt_p = rt[:L] + rt[L:]
        y0_p = y0[:L] + y0[L:]
        s = s_scr[p]
        y_ref[:, sl] = _dot(rt_p, s) + y0_p
        s_scr[p] = _dot(m, s) + cc

    @pl.when(c == pl.num_programs(1) - 1)
    def _():
        sout_ref[...] = s_scr[...]


def _scan(r, k, v, kk, kka, lw, s0, chunk):
    nb, t, d_b = r.shape
    n_pairs = d_b // PAIR
    tok = pl.BlockSpec((None, chunk, d_b), lambda b, c: (b, c, 0))
    st = pl.BlockSpec((None, n_pairs, PAIR, PAIR), lambda b, c: (b, 0, 0, 0))
    return pl.pallas_call(
        functools.partial(_scan_kernel, chunk=chunk),
        grid=(nb, t // chunk),
        in_specs=[tok] * 6 + [st],
        out_specs=[tok, st],
        out_shape=[jax.ShapeDtypeStruct((nb, t, d_b), F32),
                   jax.ShapeDtypeStruct((nb, n_pairs, PAIR, PAIR), F32)],
        scratch_shapes=[pltpu.VMEM((n_pairs, PAIR, PAIR), F32)],
        compiler_params=pltpu.CompilerParams(dimension_semantics=("parallel", "arbitrary"),
                                             vmem_limit_bytes=VMEM_LIMIT),
        name="rwkv_scan",
    )(r, k, v, kk, kka, lw, s0)


def _state_to_blockdiag(wkv):
    b, h, n, _ = wkv.shape
    st = jnp.swapaxes(wkv, -1, -2).reshape(b, h // 2, 2, n, n)
    z = jnp.zeros_like(st[:, :, 0])
    top = jnp.concatenate([st[:, :, 0], z], axis=-1)
    bot = jnp.concatenate([z, st[:, :, 1]], axis=-1)
    return jnp.concatenate([top, bot], axis=-2)


def _blockdiag_to_state(s):
    b, hp, _, _ = s.shape
    n = HEAD_B
    blocks = jnp.stack([s[:, :, :n, :n], s[:, :, n:, n:]], axis=2)
    return jnp.swapaxes(blocks, -1, -2).reshape(b, hp * 2, n, n)


def _merge_kernel(y_ref, g_ref, bonus_ref, sgb_ref, apart_ref, x_ref, lg_ref, lb_ref, bd_ref,
                  wbo_ref, wo_ref, g2_ref, rw_ref, rb_ref, h_out, xn2_out, comb_out):
    bd = bd_ref[...]
    y = y_ref[...]
    inv_n = 1.0 / HEAD_B
    mu = _dot_exact_rhs(y, bd) * inv_n
    d = y - mu
    var = _dot_exact_rhs(d * d, bd) * inv_n
    yn = d * lax.rsqrt(var + GN_EPS) * lg_ref[...] + lb_ref[...]
    yb = ((yn + bonus_ref[...]) * g_ref[...]).astype(BF16)
    merged = apart_ref[...] + sgb_ref[...] * jnp.dot(yb, wbo_ref[...], preferred_element_type=F32)
    h = x_ref[...] + jnp.dot(merged.astype(BF16), wo_ref[...], preferred_element_type=F32)
    h_out[...] = h
    xn2 = _rms(h, g2_ref[...])
    xn2_out[...] = xn2.astype(BF16)
    logits = _dot3(xn2, rw_ref[...]) + rb_ref[...]
    n_e = logits.shape[1]
    idx = lax.broadcasted_iota(jnp.int32, logits.shape, 1).astype(F32)
    work = logits
    tops, hots = [], []
    for _ in range(TOP_K):
        m = jnp.max(work, axis=-1, keepdims=True)
        sel = jnp.min(jnp.where(work == m, idx, float(n_e)), axis=-1, keepdims=True)
        hot = idx == sel
        tops.append(m)
        hots.append(hot)
        work = jnp.where(hot, -jnp.inf, work)
    es = [jnp.exp(t - tops[0]) for t in tops]
    denom = es[0] + es[1] + es[2] + es[3]
    comb = jnp.zeros_like(logits)
    for e, hot in zip(es, hots):
        comb = comb + jnp.where(hot, e / denom, 0.0)
    comb_out[...] = comb


def _merge(y, g, bonus, sgb, apart, x2d, tm, lnx_g, lnx_b, bd, wbo, wo, g2n, rw, rb):
    n, d = x2d.shape
    d_b = y.shape[1]
    n_e = rw.shape[1]
    tok = lambda w: pl.BlockSpec((tm, w), lambda i: (i, 0))
    weights = (lnx_g, lnx_b, bd, wbo, wo, g2n, rw, rb)
    return pl.pallas_call(
        _merge_kernel,
        grid=(n // tm,),
        in_specs=[tok(d_b), tok(d_b), tok(d_b), tok(d), tok(d), tok(d)]
        + [_full_spec(w.shape) for w in weights],
        out_specs=[tok(d), tok(d), tok(n_e)],
        out_shape=[jax.ShapeDtypeStruct((n, d), F32), jax.ShapeDtypeStruct((n, d), BF16),
                   jax.ShapeDtypeStruct((n, n_e), F32)],
        compiler_params=pltpu.CompilerParams(dimension_semantics=("parallel",),
                                             vmem_limit_bytes=VMEM_LIMIT),
        name="merge_router",
    )(y, g, bonus, sgb, apart, x2d, *weights)


def _moe_kernel(x_ref, comb_ref, h_ref, wgu_ref, bgu_ref, wd_ref, bdn_ref, gf_ref, y_ref, acc_scr,
                *, ff_chunk):
    e = pl.program_id(1)

    @pl.when(e == 0)
    def _():
        acc_scr[...] = jnp.zeros_like(acc_scr)

    x = x_ref[...]
    comb = comb_ref[...]
    lane = lax.broadcasted_iota(jnp.int32, comb.shape, 1)
    cw = jnp.sum(jnp.where(lane == e, comb, 0.0), axis=-1, keepdims=True)
    d_ff = wd_ref.shape[0]
    part = jnp.zeros(acc_scr.shape, F32)
    for f in range(d_ff // ff_chunk):
        lo = f * ff_chunk
        gate = jnp.dot(x, wgu_ref[:, lo:lo + ff_chunk], preferred_element_type=F32) \
            + bgu_ref[:, lo:lo + ff_chunk]
        up = jnp.dot(x, wgu_ref[:, d_ff + lo:d_ff + lo + ff_chunk], preferred_element_type=F32) \
            + bgu_ref[:, d_ff + lo:d_ff + lo + ff_chunk]
        gate = jnp.minimum(gate, SWIGLU_LIMIT)
        up = jnp.clip(up, -SWIGLU_LIMIT, SWIGLU_LIMIT)
        hh = (up + 1.0) * gate * _sigmoid(gate * SWIGLU_ALPHA)
        part = part + jnp.dot(hh.astype(BF16), wd_ref[lo:lo + ff_chunk, :],
                              preferred_element_type=F32)
    acc_scr[...] += cw * (part + bdn_ref[...])

    @pl.when(e == pl.num_programs(1) - 1)
    def _():
        y_ref[...] = _rms(h_ref[...] + acc_scr[...], gf_ref[...])


def _moe(xn2, comb, h, tm, wgu, bgu, wd, bdn, gf):
    n, d = h.shape
    n_e, _, gu = wgu.shape
    d_ff = wd.shape[1]
    tok = lambda w: pl.BlockSpec((tm, w), lambda i, e: (i, 0))
    return pl.pallas_call(
        functools.partial(_moe_kernel, ff_chunk=512),
        grid=(n // tm, n_e),
        in_specs=[tok(d), tok(n_e), tok(d),
                  pl.BlockSpec((None, d, gu), lambda i, e: (e, 0, 0)),
                  pl.BlockSpec((None, 1, gu), lambda i, e: (e, 0, 0)),
                  pl.BlockSpec((None, d_ff, d), lambda i, e: (e, 0, 0)),
                  pl.BlockSpec((None, 1, d), lambda i, e: (e, 0, 0)),
                  pl.BlockSpec((1, d), lambda i, e: (0, 0))],
        out_specs=tok(d),
        out_shape=jax.ShapeDtypeStruct((n, d), F32),
        scratch_shapes=[pltpu.VMEM((tm, d), F32)],
        compiler_params=pltpu.CompilerParams(dimension_semantics=("parallel", "arbitrary"),
                                             vmem_limit_bytes=VMEM_LIMIT),
        name="moe_dense",
    )(xn2, comb, h, wgu, bgu, wd, bdn, gf)


def _mix_matrix(ws, tm, seq_rows):
    causal = jnp.tril(jnp.ones((CHUNK, CHUNK), dtype=bool))
    w = jnp.where(causal[None], ws, 0.0)[:, :seq_rows, :seq_rows]
    eye = jnp.eye(tm // seq_rows, dtype=ws.dtype)
    return jnp.einsum("ab,gts->gatbs", eye, w).reshape(ws.shape[0], tm, tm).astype(BF16)


def _stream(x, shift_in, wkv_in, p, tm, seq_rows, scan_chunk):
    nb, t, d = x.shape
    n = nb * t
    x2d = x.reshape(n, d)
    rows = CHUNK if seq_rows is None else seq_rows
    wmix = _mix_matrix(p["w_spatial"], tm, rows)
    pos = jnp.arange(tm) % rows
    d_a = p["wu"].shape[1]
    bias_full = jnp.repeat(p["b_spatial"].T[pos], d_a // G_A, axis=1)
    apart, vn = _branch_a(x2d, tm, p["norm1_g"], p["wu"], p["wv"], p["wga"], p["vnorm_g"],
                          p["vnorm_b"], wmix, bias_full, p["w_a_out"])
    if seq_rows is None:
        xb, ext = x, jnp.zeros((1, p["wcur"].shape[1]), F32)
    else:
        xb = x2d.reshape(n // tm, tm, d)
        ext = jnp.repeat(shift_in, seq_rows, axis=0).reshape(n // tm, tm, -1)
    outs = _branch_b(xb, tm, seq_rows, ext, p["norm1_g"], p["wcur"], p["wgb"], p["mu_shift"],
                     p["w0"], p["w2"], p["a0"], p["a2"], p["g2"], p["k_k"], p["k_a"], p["r_k"],
                     p["bd"])
    r, k2, v, kk, kka, lw, g, bonus, sgb, cur = outs
    d_b = r.shape[-1]
    if seq_rows is None:
        shift_out = cur[:, -1, :]
        scan_in = [a.reshape(nb, t, d_b) for a in (r, k2, v, kk, kka, lw)]
    else:
        shift_out = cur.reshape(nb, t, -1)[:, -1, :]
        pad = scan_chunk - t
        scan_in = [jnp.pad(a.reshape(nb, t, d_b), ((0, 0), (0, pad), (0, 0)))
                   for a in (r, k2, v, kk, kka, lw)]
    y, s_out = _scan(*scan_in, _state_to_blockdiag(wkv_in), scan_chunk)
    y2d = y[:, :t].reshape(n, d_b)
    flat = lambda a: a.reshape(n, a.shape[-1])
    h, xn2, comb = _merge(y2d, flat(g), flat(bonus), flat(sgb), apart, x2d, tm, p["lnx_g"],
                          p["lnx_b"], p["bd"], p["w_b_out"], p["w_out"], p["norm2_g"],
                          p["router_w"], p["router_b"])
    return h, xn2, comb, vn, shift_out, _blockdiag_to_state(s_out)


def kernel(x_prompt, x_sample, state_shift, state_wkv, norm1_g, w_in, mu_shift, vnorm_g, vnorm_b, w_spatial, b_spatial, w_a_out, w0, w2, a0, a2, g2, k_k, k_a, r_k, lnx_g, lnx_b, w_b_out, w_out, norm2_g, router_w, router_b, exp_w_gu, exp_b_gu, exp_w_down, exp_b_down, normf_g):
    depth = w_in.shape[0]
    assert depth == 1, "the final norm is fused into the single layer's MoE call"
    d_model = x_prompt.shape[-1]
    d_b = w0.shape[-1]
    shift_w = mu_shift.shape[-1]
    d_a = vnorm_g.shape[-1]
    bp, tp, _ = x_prompt.shape
    bs, ts, _ = x_sample.shape
    head_id = jnp.arange(d_b) // HEAD_B
    bd = (head_id[:, None] == head_id[None, :]).astype(BF16)
    row = lambda a: a.reshape(1, -1)

    hp, hs = x_prompt, x_sample
    shift_p = jnp.zeros((depth, bp, shift_w), state_shift.dtype)
    wkv_p = jnp.zeros((depth, bp) + state_wkv.shape[2:], state_wkv.dtype)
    vrows, shifts_p, wkvs_p, shifts_s, wkvs_s = [], [], [], [], []
    for l in range(depth):
        wi = w_in[l].astype(BF16)
        o = shift_w
        p = dict(
            norm1_g=row(norm1_g[l]), wcur=wi[:, :o], wu=wi[:, o:o + d_a],
            wv=wi[:, o + d_a:o + 2 * d_a], wga=wi[:, o + 2 * d_a:o + 2 * d_a + d_model],
            wgb=wi[:, o + 2 * d_a + d_model:], mu_shift=row(mu_shift[l]),
            vnorm_g=row(vnorm_g[l]), vnorm_b=row(vnorm_b[l]), w_spatial=w_spatial[l],
            b_spatial=b_spatial[l], w_a_out=w_a_out[l].astype(BF16), w0=row(w0[l]), w2=w2[l],
            a0=row(a0[l]), a2=a2[l], g2=g2[l], k_k=row(k_k[l]), k_a=row(k_a[l]),
            r_k=row(r_k[l]), lnx_g=row(lnx_g[l]), lnx_b=row(lnx_b[l]),
            w_b_out=w_b_out[l].astype(BF16), w_out=w_out[l].astype(BF16),
            norm2_g=row(norm2_g[l]), router_w=router_w[l], router_b=row(router_b[l]), bd=bd)
        h_p, xn_p, comb_p, _, sh_p, s_p = _stream(hp, shift_p[l], wkv_p[l], p, 512, None, 64)
        h_s, xn_s, comb_s, vn_s, sh_s, s_s = _stream(hs, state_shift[l], state_wkv[l], p,
                                                     512, ts, 8)
        wgu = exp_w_gu[l].astype(BF16)
        wd = exp_w_down[l].astype(BF16)
        bgu = exp_b_gu[l][:, None, :]
        bdn = exp_b_down[l][:, None, :]
        gf = row(normf_g)
        hp = _moe(xn_p, comb_p, h_p, 1024, wgu, bgu, wd, bdn, gf).reshape(bp, tp, d_model)
        hs = _moe(xn_s, comb_s, h_s, 512, wgu, bgu, wd, bdn, gf).reshape(bs, ts, d_model)
        vrows.append(vn_s.reshape(bs, ts, d_a))
        shifts_p.append(sh_p)
        wkvs_p.append(s_p)
        shifts_s.append(sh_s)
        wkvs_s.append(s_s)
    return (hp, hs, jnp.stack(shifts_p), jnp.stack(wkvs_p), jnp.stack(shifts_s),
            jnp.stack(wkvs_s), jnp.stack(vrows))
```

```python
import functools

import jax
import jax.numpy as jnp
from jax import lax
from jax.experimental import pallas as pl
from jax.experimental.pallas import tpu as pltpu
from jax.experimental.pallas import tpu_sc as plsc

F32 = jnp.float32
BF16 = jnp.bfloat16

CHUNK = 128
G_A = 8
HEAD_B = 64
R_W, R_A, R_G = 64, 64, 128
TOP_K = 4
SWIGLU_LIMIT = 7.0
SWIGLU_ALPHA = 1.702
EPS = 1e-5
GN_EPS = HEAD_B * 1e-5

LANES = 128
PAIR = 2 * HEAD_B
VMEM_LIMIT = 56 * 1024 * 1024


def _dot(a, b):
    return jnp.dot(a.astype(BF16), b.astype(BF16), preferred_element_type=F32)


def _split(x):
    hi = x.astype(BF16)
    lo = (x - hi.astype(F32)).astype(BF16)
    return hi, lo


def _dot3(a, b):
    ah, al = _split(a)
    bh, bl = _split(b)
    return (jnp.dot(ah, bh, preferred_element_type=F32)
            + jnp.dot(ah, bl, preferred_element_type=F32)
            + jnp.dot(al, bh, preferred_element_type=F32))


def _dot_exact_rhs(a, b_bf16):
    ah, al = _split(a)
    return (jnp.dot(ah, b_bf16, preferred_element_type=F32)
            + jnp.dot(al, b_bf16, preferred_element_type=F32))


def _dot_exact_lhs(a_bf16, b):
    bh, bl = _split(b)
    bm = b - bh.astype(F32) - bl.astype(F32)
    return (jnp.dot(a_bf16, bh, preferred_element_type=F32)
            + jnp.dot(a_bf16, bl, preferred_element_type=F32)
            + jnp.dot(a_bf16, bm.astype(BF16), preferred_element_type=F32))


def _rms(x, g):
    return x * lax.rsqrt(jnp.mean(x * x, axis=-1, keepdims=True) + EPS) * g


def _sigmoid(x):
    return 1.0 / (1.0 + jnp.exp(-x))


def _softplus(z):
    return jnp.maximum(z, 0.0) + jnp.log(1.0 + jnp.exp(-jnp.abs(z)))


def _full_spec(shape):
    nd = len(shape)
    return pl.BlockSpec(shape, lambda *_: (0,) * nd, pipeline_mode=pl.Buffered(1))


def _branch_a_kernel(x_ref, g1_ref, wu_ref, wv_ref, wga_ref, vg_ref, vb_ref, wmix_ref,
                     bias_ref, wao_ref, apart_ref, vn_ref):
    xn = _rms(x_ref[...], g1_ref[...]).astype(BF16)
    v = jnp.dot(xn, wv_ref[...], preferred_element_type=F32)
    mu = jnp.mean(v, axis=-1, keepdims=True)
    d = v - mu
    var = jnp.mean(d * d, axis=-1, keepdims=True)
    vn = d * lax.rsqrt(var + EPS) * vg_ref[...] + vb_ref[...]
    vn_ref[...] = vn
    vnb = vn.astype(BF16)
    c_a = vnb.shape[1] // G_A
    mixed = jnp.concatenate(
        [jnp.dot(wmix_ref[g], vnb[:, g * c_a:(g + 1) * c_a], preferred_element_type=F32)
         for g in range(G_A)], axis=1) + bias_ref[...]
    u = jnp.dot(xn, wu_ref[...], preferred_element_type=F32)
    ya = (u * mixed).astype(BF16)
    ga = jnp.dot(xn, wga_ref[...], preferred_element_type=F32)
    apart_ref[...] = _sigmoid(ga) * jnp.dot(ya, wao_ref[...], preferred_element_type=F32)


def _branch_a(x2d, tm, g1, wu, wv, wga, vg, vb, wmix, bias_full, wao):
    n, d = x2d.shape
    d_a = wu.shape[1]
    tok = lambda w: pl.BlockSpec((tm, w), lambda i: (i, 0))
    return pl.pallas_call(
        _branch_a_kernel,
        grid=(n // tm,),
        in_specs=[tok(d), _full_spec(g1.shape), _full_spec(wu.shape), _full_spec(wv.shape),
                  _full_spec(wga.shape), _full_spec(vg.shape), _full_spec(vb.shape),
                  _full_spec(wmix.shape), _full_spec(bias_full.shape), _full_spec(wao.shape)],
        out_specs=[tok(d), tok(d_a)],
        out_shape=[jax.ShapeDtypeStruct((n, d), F32), jax.ShapeDtypeStruct((n, d_a), F32)],
        compiler_params=pltpu.CompilerParams(dimension_semantics=("parallel",),
                                             vmem_limit_bytes=VMEM_LIMIT),
        name="branch_a",
    )(x2d, g1, wu, wv, wga, vg, vb, wmix, bias_full, wao)


def _branch_b_kernel(x_ref, g1_ref, wcur_ref, wgb_ref, mu_ref, ext_ref, w0_ref, w2_ref,
                     a0_ref, a2_ref, g2_ref, kk_ref, ka_ref, rk_ref, bd_ref,
                     r_out, k_out, v_out, kk_out, kka_out, lw_out, g_out, bonus_out,
                     sgb_out, cur_out, carry_scr, *, seq_rows, d_b):
    tm = x_ref.shape[0]
    xn = _rms(x_ref[...], g1_ref[...]).astype(BF16)
    cur = jnp.dot(xn, wcur_ref[...], preferred_element_type=F32)
    rolled = pltpu.roll(cur, 1, axis=0)
    row = lax.broadcasted_iota(jnp.int32, (tm, 1), 0)
    if seq_rows is None:
        first_tile = pl.program_id(1) == 0
        carry = jnp.where(first_tile, 0.0, carry_scr[...])
        prev = jnp.where(row == 0, carry, rolled)
        carry_scr[...] = cur[tm - 1:tm, :]
        cur_out[...] = cur[tm - 8:tm, :]
    else:
        prev = jnp.where(row % seq_rows == 0, ext_ref[...], rolled)
        cur_out[...] = cur
    xs = cur + (prev - cur) * mu_ref[...]
    r = xs[:, 0:d_b]
    k = xs[:, d_b:2 * d_b]
    v = xs[:, 2 * d_b:3 * d_b]
    o = 3 * d_b
    xw = xs[:, o:o + R_W]
    xa = xs[:, o + R_W:o + R_W + R_A]
    xg = xs[:, o + R_W + R_A:o + R_W + R_A + R_G]
    wl = w0_ref[...] + _dot3(jnp.tanh(xw), w2_ref[...])
    lw = -jnp.exp(-_softplus(-wl) - 0.5)
    a = _sigmoid(a0_ref[...] + _dot3(xa, a2_ref[...]))
    g = _dot3(_sigmoid(xg), g2_ref[...])
    bd = bd_ref[...]
    kkr = k * kk_ref[...]
    n2 = _dot_exact_rhs(kkr * kkr, bd)
    kk = kkr / jnp.maximum(jnp.sqrt(n2), 1e-12)
    k2 = k * (1.0 + (a - 1.0) * ka_ref[...])
    bonus = _dot_exact_rhs(r * k2 * rk_ref[...], bd) * v
    r_out[...] = r
    k_out[...] = k2
    v_out[...] = v
    kk_out[...] = kk
    kka_out[...] = kk * a
    lw_out[...] = lw
    g_out[...] = g
    bonus_out[...] = bonus
    sgb_out[...] = _sigmoid(jnp.dot(xn, wgb_ref[...], preferred_element_type=F32))


def _branch_b(x3d, tm, seq_rows, ext, g1, wcur, wgb, mu, w0, w2, a0, a2, g2, k_k, k_a, r_k, bd):
    nb, t, d = x3d.shape
    d_b = w0.shape[1]
    shift_w = wcur.shape[1]
    nt = t // tm
    tok = lambda w: pl.BlockSpec((None, tm, w), lambda b, i: (b, i, 0))
    cur_rows = 8 if seq_rows is None else tm
    outs = [jax.ShapeDtypeStruct((nb, t, d_b), F32)] * 8 + [
        jax.ShapeDtypeStruct((nb, t, d), F32),
        jax.ShapeDtypeStruct((nb, nt * cur_rows, shift_w), F32)]
    out_specs = [tok(d_b)] * 8 + [tok(d), pl.BlockSpec((None, cur_rows, shift_w),
                                                       lambda b, i: (b, i, 0))]
    weights = (g1, wcur, wgb, mu)
    small = (w0, w2, a0, a2, g2, k_k, k_a, r_k, bd)
    return pl.pallas_call(
        functools.partial(_branch_b_kernel, seq_rows=seq_rows, d_b=d_b),
        grid=(nb, nt),
        in_specs=[tok(d)] + [_full_spec(w.shape) for w in weights]
        + [tok(shift_w) if seq_rows is not None else _full_spec(ext.shape)]
        + [_full_spec(w.shape) for w in small],
        out_specs=out_specs,
        out_shape=outs,
        scratch_shapes=[pltpu.VMEM((1, shift_w), F32)],
        compiler_params=pltpu.CompilerParams(dimension_semantics=("parallel", "arbitrary"),
                                             vmem_limit_bytes=VMEM_LIMIT),
        name="branch_b",
    )(x3d, *weights, ext, *small)


def _scan_kernel(r_ref, k_ref, v_ref, kk_ref, kka_ref, lw_ref, s0_ref, y_ref, sout_ref, s_scr,
                 *, chunk):
    L = chunk
    c = pl.program_id(1)

    @pl.when(c == 0)
    def _():
        s_scr[...] = s0_ref[...]

    n_pairs = s_scr.shape[0]
    row2 = lax.broadcasted_iota(jnp.int32, (2 * L, 1), 0)
    lane = lax.broadcasted_iota(jnp.int32, (1, PAIR), 1)
    stack_mask = (row2 >= L) == (lane >= HEAD_B)
    ri = lax.broadcasted_iota(jnp.int32, (2 * L, 2 * L), 0)
    ci = lax.broadcasted_iota(jnp.int32, (2 * L, 2 * L), 1)
    same = (ri >= L) == (ci >= L)
    strict = same & (ci < ri)
    incl = same & (ci <= ri)
    eye2 = (ri == ci).astype(F32)
    ti = lax.broadcasted_iota(jnp.int32, (L, L), 0)
    tj = lax.broadcasted_iota(jnp.int32, (L, L), 1)
    tri = (tj <= ti).astype(BF16)
    di = lax.broadcasted_iota(jnp.int32, (PAIR, PAIR), 0)
    dj = lax.broadcasted_iota(jnp.int32, (PAIR, PAIR), 1)
    diag = di == dj

    lw_all = lw_ref[...]
    cum_all = _dot_exact_lhs(tri, lw_all)

    def stack(x):
        return jnp.where(stack_mask, jnp.concatenate([x, x], axis=0), 0.0)

    n_double = max(L.bit_length() - 2, 0)
    for p in range(n_pairs):
        sl = slice(p * PAIR, (p + 1) * PAIR)
        lw = lw_all[:, sl]
        cum = cum_all[:, sl]
        cum_last = cum[L - 1:L, :]
        e_cum = jnp.exp(cum)
        e_cumx = jnp.exp(cum - lw)
        e_neg = jnp.exp(-cum)
        e_rel = jnp.exp(cum_last - cum)
        kk = kk_ref[:, sl]
        kka = kka_ref[:, sl]
        kx = k_ref[:, sl]
        ab = stack(-kk * e_cumx)
        rb = stack(r_ref[:, sl] * e_cum)
        bb = stack(kka * e_neg)
        kb = stack(kx * e_neg)
        bt = stack(kka * e_rel)
        kt = stack(kx * e_rel)
        vs = stack(v_ref[:, sl])
        ar = jnp.concatenate([ab, rb], axis=0).astype(BF16)
        bk = jnp.concatenate([bb, kb], axis=0).astype(BF16)
        a_all = lax.dot_general(ar, bk, (((1,), (1,)), ((), ())), preferred_element_type=F32)
        a_ab = jnp.where(strict, a_all[:2 * L, :2 * L], 0.0)
        a_ak = jnp.where(strict, a_all[:2 * L, 2 * L:], 0.0)
        a_rb = jnp.where(incl, a_all[2 * L:, :2 * L], 0.0)
        a_rk = jnp.where(incl, a_all[2 * L:, 2 * L:], 0.0)
        nk = a_ab
        tm = eye2 + nk
        for _ in range(n_double):
            nk = _dot(nk, nk)
            tm = tm + _dot(tm, nk)
        akv = _dot(a_ak, vs)
        tx = _dot(tm, jnp.concatenate([ab, akv], axis=1))
        ta = tx[:, :PAIR]
        u0 = tx[:, PAIR:]
        ry = _dot(a_rb, tx)
        rt = rb + ry[:, :PAIR]
        y0 = ry[:, PAIR:] + _dot(a_rk, vs)
        btk = jnp.concatenate([bt.T, kt.T], axis=1)
        mc = _dot(btk, jnp.concatenate(
            [jnp.concatenate([ta, u0], axis=1),
             jnp.concatenate([jnp.zeros_like(vs), vs], axis=1)], axis=0))
        m = mc[:, :PAIR] + jnp.where(diag, jnp.exp(cum_last), 0.0)
        cc = mc[:, PAIR:]
        rt_p = rt[:L] + rt[L:]
        y0_p = y0[:L] + y0[L:]
        s = s_scr[p]
        y_ref[:, sl] = _dot(rt_p, s) + y0_p
        s_scr[p] = _dot(m, s) + cc

    @pl.when(c == pl.num_programs(1) - 1)
    def _():
        sout_ref[...] = s_scr[...]


def _scan(r, k, v, kk, kka, lw, s0, chunk):
    nb, t, d_b = r.shape
    n_pairs = d_b // PAIR
    tok = pl.BlockSpec((None, chunk, d_b), lambda b, c: (b, c, 0))
    st = pl.BlockSpec((None, n_pairs, PAIR, PAIR), lambda b, c: (b, 0, 0, 0))
    return pl.pallas_call(
        functools.partial(_scan_kernel, chunk=chunk),
        grid=(nb, t // chunk),
        in_specs=[tok] * 6 + [st],
        out_specs=[tok, st],
        out_shape=[jax.ShapeDtypeStruct((nb, t, d_b), F32),
                   jax.ShapeDtypeStruct((nb, n_pairs, PAIR, PAIR), F32)],
        scratch_shapes=[pltpu.VMEM((n_pairs, PAIR, PAIR), F32)],
        compiler_params=pltpu.CompilerParams(dimension_semantics=("parallel", "arbitrary"),
                                             vmem_limit_bytes=VMEM_LIMIT),
        name="rwkv_scan",
    )(r, k, v, kk, kka, lw, s0)


def _state_to_blockdiag(wkv):
    b, h, n, _ = wkv.shape
    st = jnp.swapaxes(wkv, -1, -2).reshape(b, h // 2, 2, n, n)
    z = jnp.zeros_like(st[:, :, 0])
    top = jnp.concatenate([st[:, :, 0], z], axis=-1)
    bot = jnp.concatenate([z, st[:, :, 1]], axis=-1)
    return jnp.concatenate([top, bot], axis=-2)


def _blockdiag_to_state(s):
    b, hp, _, _ = s.shape
    n = HEAD_B
    blocks = jnp.stack([s[:, :, :n, :n], s[:, :, n:, n:]], axis=2)
    return jnp.swapaxes(blocks, -1, -2).reshape(b, hp * 2, n, n)


def _pack_bf16_pairs(x):
    w = x.shape[1] // 2
    bits = lambda v: lax.bitcast_convert_type(v.astype(BF16).astype(F32), jnp.uint32)
    return (bits(x[:, :w]) >> 16) | (bits(x[:, w:]) & jnp.uint32(0xFFFF0000))


def _unpack_bf16_pairs(u):
    lo = lax.bitcast_convert_type(u << 16, F32)
    hi = lax.bitcast_convert_type(u & jnp.uint32(0xFFFF0000), F32)
    return lo, hi


def _merge_kernel(y_ref, g_ref, bonus_ref, sgb_ref, apart_ref, x_ref, lg_ref, lb_ref, bd_ref,
                  wbo_ref, wo_ref, g2_ref, rw_ref, rb_ref,
                  h_out, xp_out, topi_out, prob_out, rank_out, cnt_out, cnt_scr):
    step = pl.program_id(0)

    @pl.when(step == 0)
    def _():
        cnt_scr[...] = jnp.zeros_like(cnt_scr)

    bd = bd_ref[...]
    y = y_ref[...]
    inv_n = 1.0 / HEAD_B
    mu = _dot_exact_rhs(y, bd) * inv_n
    d = y - mu
    var = _dot_exact_rhs(d * d, bd) * inv_n
    yn = d * lax.rsqrt(var + GN_EPS) * lg_ref[...] + lb_ref[...]
    yb = ((yn + bonus_ref[...]) * g_ref[...]).astype(BF16)
    merged = apart_ref[...] + sgb_ref[...] * jnp.dot(yb, wbo_ref[...], preferred_element_type=F32)
    h = x_ref[...] + jnp.dot(merged.astype(BF16), wo_ref[...], preferred_element_type=F32)
    h_out[...] = h
    xn2 = _rms(h, g2_ref[...])
    xp_out[...] = _pack_bf16_pairs(xn2)
    logits = _dot3(xn2, rw_ref[...]) + rb_ref[...]
    tm, n_e = logits.shape
    idx = lax.broadcasted_iota(jnp.int32, logits.shape, 1).astype(F32)
    work = logits
    tops, hots, sels = [], [], []
    for _ in range(TOP_K):
        m = jnp.max(work, axis=-1, keepdims=True)
        sel = jnp.min(jnp.where(work == m, idx, float(n_e)), axis=-1, keepdims=True)
        hot = idx == sel
        tops.append(m)
        hots.append(hot)
        sels.append(sel)
        work = jnp.where(hot, -jnp.inf, work)
    es = [jnp.exp(t - tops[0]) for t in tops]
    denom = es[0] + es[1] + es[2] + es[3]
    topi_out[...] = jnp.concatenate(sels, axis=1).astype(jnp.int32)
    prob_out[...] = jnp.concatenate([e / denom for e in es], axis=1)
    hot_any = jnp.zeros_like(logits)
    for hot in hots:
        hot_any = hot_any + hot.astype(F32)
    ri = lax.broadcasted_iota(jnp.int32, (tm, tm), 0)
    ci = lax.broadcasted_iota(jnp.int32, (tm, tm), 1)
    before = _dot((ci < ri).astype(BF16), hot_any) + cnt_scr[...]
    rank_out[...] = jnp.concatenate(
        [jnp.sum(jnp.where(hot, before, 0.0), axis=-1, keepdims=True) for hot in hots],
        axis=1).astype(jnp.int32)
    total = cnt_scr[...] + jnp.sum(hot_any, axis=0, keepdims=True)
    cnt_scr[...] = total
    cnt_out[...] = total.astype(jnp.int32)


def _merge(y, g, bonus, sgb, apart, x2d, tm, lnx_g, lnx_b, bd, wbo, wo, g2n, rw, rb):
    n, d = x2d.shape
    d_b = y.shape[1]
    n_e = rw.shape[1]
    tok = lambda w: pl.BlockSpec((tm, w), lambda i: (i, 0))
    weights = (lnx_g, lnx_b, bd, wbo, wo, g2n, rw, rb)
    return pl.pallas_call(
        _merge_kernel,
        grid=(n // tm,),
        in_specs=[tok(d_b), tok(d_b), tok(d_b), tok(d), tok(d), tok(d)]
        + [_full_spec(w.shape) for w in weights],
        out_specs=[tok(d), tok(d // 2), tok(TOP_K), tok(TOP_K), tok(TOP_K),
                   pl.BlockSpec((1, n_e), lambda i: (0, 0))],
        out_shape=[jax.ShapeDtypeStruct((n, d), F32),
                   jax.ShapeDtypeStruct((n, d // 2), jnp.uint32),
                   jax.ShapeDtypeStruct((n, TOP_K), jnp.int32),
                   jax.ShapeDtypeStruct((n, TOP_K), F32),
                   jax.ShapeDtypeStruct((n, TOP_K), jnp.int32),
                   jax.ShapeDtypeStruct((1, n_e), jnp.int32)],
        scratch_shapes=[pltpu.VMEM((1, n_e), F32)],
        compiler_params=pltpu.CompilerParams(dimension_semantics=("arbitrary",),
                                             vmem_limit_bytes=VMEM_LIMIT),
        name="merge_router",
    )(y, g, bonus, sgb, apart, x2d, *weights)


MOE_TM = 256
MOE_FF_CHUNK = 512
CAST_ROWS = 256


def _route(topi_p, rank_p, cnt_p, topi_s, rank_s, cnt_s, n_tiles):
    n_e = cnt_p.shape[-1]
    cnt_p, cnt_s = cnt_p.reshape(n_e), cnt_s.reshape(n_e)
    cnt = cnt_p + cnt_s
    padded = (cnt + MOE_TM - 1) // MOE_TM * MOE_TM
    ends = jnp.cumsum(padded)
    base = ends - padded
    lookup = lambda table, idx: jnp.sum(
        jnp.where(idx[..., None] == jnp.arange(n_e), table, 0), axis=-1)
    pos_p = lookup(base, topi_p) + rank_p
    pos_s = lookup(base + cnt_p, topi_s) + rank_s
    start = jnp.arange(n_tiles, dtype=jnp.int32) * MOE_TM
    n_used = (ends[-1] // MOE_TM).astype(jnp.int32)
    tile_e = jnp.minimum(jnp.sum(start[:, None] >= ends[None, :], axis=1), n_e - 1)
    tile_rows = jnp.clip(lookup(base + cnt, tile_e) - start, 0, MOE_TM)
    tile_rows = jnp.where(start < ends[-1], tile_rows, 0)
    return (pos_p.astype(jnp.int32), pos_s.astype(jnp.int32), tile_e.astype(jnp.int32),
            tile_rows.astype(jnp.int32), n_used.reshape(1))


def _moe_kernel(te_ref, rows_ref, nused_ref, xs_ref, wgu_ref, bgu_ref, wd_ref, bdn_ref, ys_ref,
                wgu_b, wd_b):
    i = pl.program_id(0)
    e = te_ref[i]
    rows = rows_ref[i]
    first_of_expert = jnp.logical_or(i == 0, e != te_ref[jnp.maximum(i - 1, 0)])
    d, gu = wgu_ref.shape
    d_ff = wd_ref.shape[0]

    @pl.when(jnp.logical_and(first_of_expert, rows > 0))
    def _():
        for r0 in range(0, d, CAST_ROWS):
            wgu_b[r0:r0 + CAST_ROWS, :] = wgu_ref[r0:r0 + CAST_ROWS, :].astype(BF16)
        for r0 in range(0, d_ff, CAST_ROWS):
            wd_b[r0:r0 + CAST_ROWS, :] = wd_ref[r0:r0 + CAST_ROWS, :].astype(BF16)

    @pl.when(rows > 0)
    def _():
        lo, hi = _unpack_bf16_pairs(xs_ref[...])
        valid = lax.broadcasted_iota(jnp.int32, lo.shape, 0) < rows
        x_lo = jnp.where(valid, lo, 0.0).astype(BF16)
        x_hi = jnp.where(valid, hi, 0.0).astype(BF16)
        half = d // 2

        def proj(c0):
            cols = slice(c0, c0 + MOE_FF_CHUNK)
            return (jnp.dot(x_lo, wgu_b[:half, cols], preferred_element_type=F32)
                    + jnp.dot(x_hi, wgu_b[half:, cols], preferred_element_type=F32)
                    + bgu_ref[:, cols])

        y = jnp.zeros((x_lo.shape[0], d), F32)
        for f in range(d_ff // MOE_FF_CHUNK):
            c0 = f * MOE_FF_CHUNK
            gate = jnp.minimum(proj(c0), SWIGLU_LIMIT)
            up = jnp.clip(proj(d_ff + c0), -SWIGLU_LIMIT, SWIGLU_LIMIT)
            hh = (up + 1.0) * gate * _sigmoid(gate * SWIGLU_ALPHA)
            y = y + jnp.dot(hh.astype(BF16), wd_b[c0:c0 + MOE_FF_CHUNK, :],
                            preferred_element_type=F32)
        ys_ref[...] = _pack_bf16_pairs(y + bdn_ref[...])


def _moe(xs, tile_e, tile_rows, n_used, wgu, bgu, wd, bdn):
    p_rows, half = xs.shape
    n_e, d, gu = wgu.shape
    d_ff = wd.shape[1]
    n_tiles = p_rows // MOE_TM
    row_blk = lambda i, te, rows, nu: (jnp.minimum(i, nu[0] - 1), 0)
    w_blk = lambda i, te, rows, nu: (te[i], 0, 0)
    grid_spec = pltpu.PrefetchScalarGridSpec(
        num_scalar_prefetch=3,
        grid=(n_tiles,),
        in_specs=[pl.BlockSpec((MOE_TM, half), row_blk),
                  pl.BlockSpec((None, d, gu), w_blk),
                  pl.BlockSpec((None, 1, gu), w_blk),
                  pl.BlockSpec((None, d_ff, d), w_blk),
                  pl.BlockSpec((None, 1, d), w_blk)],
        out_specs=pl.BlockSpec((MOE_TM, half), row_blk),
        scratch_shapes=[pltpu.VMEM((d, gu), BF16), pltpu.VMEM((d_ff, d), BF16)])
    return pl.pallas_call(
        _moe_kernel,
        grid_spec=grid_spec,
        out_shape=jax.ShapeDtypeStruct((p_rows, half), jnp.uint32),
        compiler_params=pltpu.CompilerParams(dimension_semantics=("arbitrary",),
                                             vmem_limit_bytes=VMEM_LIMIT),
        name="moe_experts",
    )(tile_e, tile_rows, n_used, xs, wgu, bgu, wd, bdn)


def _combine_kernel(h_ref, yg_ref, prob_ref, gf_ref, out_ref):
    half = h_ref.shape[1] // 2
    prob = prob_ref[...]
    acc_lo = jnp.zeros((h_ref.shape[0], half), F32)
    acc_hi = jnp.zeros((h_ref.shape[0], half), F32)
    for k in range(TOP_K):
        lo, hi = _unpack_bf16_pairs(yg_ref[:, k * half:(k + 1) * half])
        pk = prob[:, k:k + 1]
        acc_lo = acc_lo + pk * lo
        acc_hi = acc_hi + pk * hi
    z = h_ref[...] + jnp.concatenate([acc_lo, acc_hi], axis=1)
    out_ref[...] = _rms(z, gf_ref[...])


def _combine(h, yg, prob, gf, tm):
    n, d = h.shape
    tok = lambda w: pl.BlockSpec((tm, w), lambda i: (i, 0))
    return pl.pallas_call(
        _combine_kernel,
        grid=(n // tm,),
        in_specs=[tok(d), tok(yg.shape[1]), tok(TOP_K), _full_spec(gf.shape)],
        out_specs=tok(d),
        out_shape=jax.ShapeDtypeStruct((n, d), F32),
        compiler_params=pltpu.CompilerParams(dimension_semantics=("parallel",),
                                             vmem_limit_bytes=VMEM_LIMIT),
        name="moe_combine",
    )(h, yg, prob, gf)


SC_CORES = 2
SC_SUBCORES = 16
SC_WORKERS = SC_CORES * SC_SUBCORES
SC_MAX_INDEX = 128


def _sc_chunk(rows_per_worker):
    for c in range(SC_MAX_INDEX, 7, -8):
        if rows_per_worker % c == 0:
            return c
    raise ValueError(f"no 8-aligned chunk divides {rows_per_worker} rows")


def _sc_mesh():
    return plsc.VectorSubcoreMesh(core_axis_name="c", subcore_axis_name="s")


def _sc_worker():
    return lax.axis_index("s") * SC_CORES + lax.axis_index("c")


def _scatter_rows(x, pos_t, p_rows):
    n, w = x.shape
    n_k = pos_t.shape[0]
    per = n // SC_WORKERS
    chunk = _sc_chunk(per)
    pos_flat = pos_t.reshape(n_k * n)

    def body(x_hbm, pos_hbm, xs_hbm, idx_v, rows_v, sem):
        base = _sc_worker() * per

        @pl.loop(0, per // chunk)
        def _(c):
            off = pl.multiple_of(base + c * chunk, 8)
            pltpu.sync_copy(x_hbm.at[pl.ds(off, chunk)], rows_v)
            for k in range(n_k):
                pltpu.sync_copy(pos_hbm.at[pl.ds(pl.multiple_of(k * n + off, 8), chunk)], idx_v)
                pltpu.async_copy(rows_v, xs_hbm.at[idx_v], sem).wait()

    return pl.kernel(
        body, out_type=jax.ShapeDtypeStruct((p_rows, w), x.dtype), mesh=_sc_mesh(),
        scratch_types=[pltpu.VMEM((chunk,), jnp.int32), pltpu.VMEM((chunk, w), x.dtype),
                       pltpu.SemaphoreType.DMA],
        name="sc_scatter_rows")(x, pos_flat)


def _gather_rows(table, idx):
    n = idx.shape[0]
    w = table.shape[1]
    per = n // SC_WORKERS
    chunk = _sc_chunk(per)

    def body(table_hbm, idx_hbm, out_hbm, idx_v, rows_v, sem):
        base = _sc_worker() * per

        @pl.loop(0, per // chunk)
        def _(c):
            off = pl.multiple_of(base + c * chunk, 8)
            pltpu.sync_copy(idx_hbm.at[pl.ds(off, chunk)], idx_v)
            pltpu.async_copy(table_hbm.at[idx_v], rows_v, sem).wait()
            pltpu.sync_copy(rows_v, out_hbm.at[pl.ds(off, chunk)])

    return pl.kernel(
        body, out_type=jax.ShapeDtypeStruct((n, w), table.dtype), mesh=_sc_mesh(),
        scratch_types=[pltpu.VMEM((chunk,), jnp.int32), pltpu.VMEM((chunk, w), table.dtype),
                       pltpu.SemaphoreType.DMA],
        name="sc_gather_rows")(table, idx)


def _mix_matrix(ws, tm, seq_rows):
    causal = jnp.tril(jnp.ones((CHUNK, CHUNK), dtype=bool))
    w = jnp.where(causal[None], ws, 0.0)[:, :seq_rows, :seq_rows]
    eye = jnp.eye(tm // seq_rows, dtype=ws.dtype)
    return jnp.einsum("ab,gts->gatbs", eye, w).reshape(ws.shape[0], tm, tm).astype(BF16)


def _stream(x, shift_in, wkv_in, p, tm, seq_rows, scan_chunk):
    nb, t, d = x.shape
    n = nb * t
    x2d = x.reshape(n, d)
    rows = CHUNK if seq_rows is None else seq_rows
    wmix = _mix_matrix(p["w_spatial"], tm, rows)
    pos = jnp.arange(tm) % rows
    d_a = p["wu"].shape[1]
    bias_full = jnp.repeat(p["b_spatial"].T[pos], d_a // G_A, axis=1)
    apart, vn = _branch_a(x2d, tm, p["norm1_g"], p["wu"], p["wv"], p["wga"], p["vnorm_g"],
                          p["vnorm_b"], wmix, bias_full, p["w_a_out"])
    if seq_rows is None:
        xb, ext = x, jnp.zeros((1, p["wcur"].shape[1]), F32)
    else:
        xb = x2d.reshape(n // tm, tm, d)
        ext = jnp.repeat(shift_in, seq_rows, axis=0).reshape(n // tm, tm, -1)
    outs = _branch_b(xb, tm, seq_rows, ext, p["norm1_g"], p["wcur"], p["wgb"], p["mu_shift"],
                     p["w0"], p["w2"], p["a0"], p["a2"], p["g2"], p["k_k"], p["k_a"], p["r_k"],
                     p["bd"])
    r, k2, v, kk, kka, lw, g, bonus, sgb, cur = outs
    d_b = r.shape[-1]
    if seq_rows is None:
        shift_out = cur[:, -1, :]
        scan_in = [a.reshape(nb, t, d_b) for a in (r, k2, v, kk, kka, lw)]
    else:
        shift_out = cur.reshape(nb, t, -1)[:, -1, :]
        pad = scan_chunk - t
        scan_in = [jnp.pad(a.reshape(nb, t, d_b), ((0, 0), (0, pad), (0, 0)))
                   for a in (r, k2, v, kk, kka, lw)]
    y, s_out = _scan(*scan_in, _state_to_blockdiag(wkv_in), scan_chunk)
    y2d = y[:, :t].reshape(n, d_b)
    flat = lambda a: a.reshape(n, a.shape[-1])
    routed = _merge(y2d, flat(g), flat(bonus), flat(sgb), apart, x2d, tm, p["lnx_g"],
                    p["lnx_b"], p["bd"], p["w_b_out"], p["w_out"], p["norm2_g"],
                    p["router_w"], p["router_b"])
    return routed, vn, shift_out, _blockdiag_to_state(s_out)


def kernel(x_prompt, x_sample, state_shift, state_wkv, norm1_g, w_in, mu_shift, vnorm_g, vnorm_b, w_spatial, b_spatial, w_a_out, w0, w2, a0, a2, g2, k_k, k_a, r_k, lnx_g, lnx_b, w_b_out, w_out, norm2_g, router_w, router_b, exp_w_gu, exp_b_gu, exp_w_down, exp_b_down, normf_g):
    depth = w_in.shape[0]
    assert depth == 1, "the final norm is fused into the single layer's MoE call"
    d_model = x_prompt.shape[-1]
    d_b = w0.shape[-1]
    shift_w = mu_shift.shape[-1]
    d_a = vnorm_g.shape[-1]
    bp, tp, _ = x_prompt.shape
    bs, ts, _ = x_sample.shape
    head_id = jnp.arange(d_b) // HEAD_B
    bd = (head_id[:, None] == head_id[None, :]).astype(BF16)
    row = lambda a: a.reshape(1, -1)

    hp, hs = x_prompt, x_sample
    shift_p = jnp.zeros((depth, bp, shift_w), state_shift.dtype)
    wkv_p = jnp.zeros((depth, bp) + state_wkv.shape[2:], state_wkv.dtype)
    vrows, shifts_p, wkvs_p, shifts_s, wkvs_s = [], [], [], [], []
    for l in range(depth):
        wi = w_in[l].astype(BF16)
        o = shift_w
        p = dict(
            norm1_g=row(norm1_g[l]), wcur=wi[:, :o], wu=wi[:, o:o + d_a],
            wv=wi[:, o + d_a:o + 2 * d_a], wga=wi[:, o + 2 * d_a:o + 2 * d_a + d_model],
            wgb=wi[:, o + 2 * d_a + d_model:], mu_shift=row(mu_shift[l]),
            vnorm_g=row(vnorm_g[l]), vnorm_b=row(vnorm_b[l]), w_spatial=w_spatial[l],
            b_spatial=b_spatial[l], w_a_out=w_a_out[l].astype(BF16), w0=row(w0[l]), w2=w2[l],
            a0=row(a0[l]), a2=a2[l], g2=g2[l], k_k=row(k_k[l]), k_a=row(k_a[l]),
            r_k=row(r_k[l]), lnx_g=row(lnx_g[l]), lnx_b=row(lnx_b[l]),
            w_b_out=w_b_out[l].astype(BF16), w_out=w_out[l].astype(BF16),
            norm2_g=row(norm2_g[l]), router_w=router_w[l], router_b=row(router_b[l]), bd=bd)
        routed_p, _, sh_p, s_p = _stream(hp, shift_p[l], wkv_p[l], p, 512, None, 64)
        routed_s, vn_s, sh_s, s_s = _stream(hs, state_shift[l], state_wkv[l], p, 512, ts, 8)
        h_p, xp_p, topi_p, prob_p, rank_p, cnt_p = routed_p
        h_s, xp_s, topi_s, prob_s, rank_s, cnt_s = routed_s
        n_p, n_s = h_p.shape[0], h_s.shape[0]
        n_e = router_w.shape[-1]
        n_tiles = (n_p + n_s) * TOP_K // MOE_TM + n_e
        pos_p, pos_s, tile_e, tile_rows, n_used = _route(topi_p, rank_p, cnt_p, topi_s, rank_s,
                                                         cnt_s, n_tiles)
        pos = jnp.concatenate([pos_p, pos_s], axis=0)
        xs = _scatter_rows(jnp.concatenate([xp_p, xp_s], axis=0), pos.T, n_tiles * MOE_TM)
        ys = _moe(xs, tile_e, tile_rows, n_used, exp_w_gu[l], exp_b_gu[l][:, None, :],
                  exp_w_down[l], exp_b_down[l][:, None, :])
        yg = _gather_rows(ys, pos.reshape(-1)).reshape(n_p + n_s, TOP_K * (d_model // 2))
        gf = row(normf_g)
        hp = _combine(h_p, yg[:n_p], prob_p, gf, 512).reshape(bp, tp, d_model)
        hs = _combine(h_s, yg[n_p:], prob_s, gf, 512).reshape(bs, ts, d_model)
        vrows.append(vn_s.reshape(bs, ts, d_a))
        shifts_p.append(sh_p)
        wkvs_p.append(s_p)
        shifts_s.append(sh_s)
        wkvs_s.append(s_s)
    return (hp, hs, jnp.stack(shifts_p), jnp.stack(wkvs_p), jnp.stack(shifts_s),
            jnp.stack(wkvs_s), jnp.stack(vrows))
```

```python
import functools

import jax
import jax.numpy as jnp
from jax import lax
from jax.experimental import pallas as pl
from jax.experimental.pallas import tpu as pltpu
from jax.experimental.pallas import tpu_sc as plsc

F32 = jnp.float32
BF16 = jnp.bfloat16

CHUNK = 128
G_A = 8
HEAD_B = 64
R_W, R_A, R_G = 64, 64, 128
TOP_K = 4
SWIGLU_LIMIT = 7.0
SWIGLU_ALPHA = 1.702
EPS = 1e-5
GN_EPS = HEAD_B * 1e-5

LANES = 128
PAIR = 2 * HEAD_B
VMEM_LIMIT = 56 * 1024 * 1024


def _dot(a, b):
    return jnp.dot(a.astype(BF16), b.astype(BF16), preferred_element_type=F32)


def _split(x):
    hi = x.astype(BF16)
    lo = (x - hi.astype(F32)).astype(BF16)
    return hi, lo


def _dot3(a, b):
    ah, al = _split(a)
    bh, bl = _split(b)
    return (jnp.dot(ah, bh, preferred_element_type=F32)
            + jnp.dot(ah, bl, preferred_element_type=F32)
            + jnp.dot(al, bh, preferred_element_type=F32))


def _dot_exact_rhs(a, b_bf16):
    ah, al = _split(a)
    return (jnp.dot(ah, b_bf16, preferred_element_type=F32)
            + jnp.dot(al, b_bf16, preferred_element_type=F32))


def _dot_exact_lhs(a_bf16, b):
    bh, bl = _split(b)
    bm = b - bh.astype(F32) - bl.astype(F32)
    return (jnp.dot(a_bf16, bh, preferred_element_type=F32)
            + jnp.dot(a_bf16, bl, preferred_element_type=F32)
            + jnp.dot(a_bf16, bm.astype(BF16), preferred_element_type=F32))


def _rms(x, g):
    return x * lax.rsqrt(jnp.mean(x * x, axis=-1, keepdims=True) + EPS) * g


def _sigmoid(x):
    return 1.0 / (1.0 + jnp.exp(-x))


def _softplus(z):
    return jnp.maximum(z, 0.0) + jnp.log(1.0 + jnp.exp(-jnp.abs(z)))


def _full_spec(shape):
    nd = len(shape)
    return pl.BlockSpec(shape, lambda *_: (0,) * nd, pipeline_mode=pl.Buffered(1))


def _branch_a_kernel(x_ref, g1_ref, wu_ref, wv_ref, wga_ref, vg_ref, vb_ref, wmix_ref,
                     bias_ref, wao_ref, apart_ref, vn_ref):
    xn = _rms(x_ref[...], g1_ref[...]).astype(BF16)
    v = jnp.dot(xn, wv_ref[...], preferred_element_type=F32)
    mu = jnp.mean(v, axis=-1, keepdims=True)
    d = v - mu
    var = jnp.mean(d * d, axis=-1, keepdims=True)
    vn = d * lax.rsqrt(var + EPS) * vg_ref[...] + vb_ref[...]
    vn_ref[...] = vn
    vnb = vn.astype(BF16)
    c_a = vnb.shape[1] // G_A
    mixed = jnp.concatenate(
        [jnp.dot(wmix_ref[g], vnb[:, g * c_a:(g + 1) * c_a], preferred_element_type=F32)
         for g in range(G_A)], axis=1) + bias_ref[...]
    u = jnp.dot(xn, wu_ref[...], preferred_element_type=F32)
    ya = (u * mixed).astype(BF16)
    ga = jnp.dot(xn, wga_ref[...], preferred_element_type=F32)
    apart_ref[...] = _sigmoid(ga) * jnp.dot(ya, wao_ref[...], preferred_element_type=F32)


def _branch_a(x2d, tm, g1, wu, wv, wga, vg, vb, wmix, bias_full, wao):
    n, d = x2d.shape
    d_a = wu.shape[1]
    tok = lambda w: pl.BlockSpec((tm, w), lambda i: (i, 0))
    return pl.pallas_call(
        _branch_a_kernel,
        grid=(n // tm,),
        in_specs=[tok(d), _full_spec(g1.shape), _full_spec(wu.shape), _full_spec(wv.shape),
                  _full_spec(wga.shape), _full_spec(vg.shape), _full_spec(vb.shape),
                  _full_spec(wmix.shape), _full_spec(bias_full.shape), _full_spec(wao.shape)],
        out_specs=[tok(d), tok(d_a)],
        out_shape=[jax.ShapeDtypeStruct((n, d), F32), jax.ShapeDtypeStruct((n, d_a), F32)],
        compiler_params=pltpu.CompilerParams(dimension_semantics=("parallel",),
                                             vmem_limit_bytes=VMEM_LIMIT),
        name="branch_a",
    )(x2d, g1, wu, wv, wga, vg, vb, wmix, bias_full, wao)


def _branch_b_kernel(x_ref, g1_ref, wcur_ref, wgb_ref, mu_ref, ext_ref, w0_ref, w2_ref,
                     a0_ref, a2_ref, g2_ref, kk_ref, ka_ref, rk_ref, bd_ref,
                     r_out, k_out, v_out, kk_out, kka_out, lw_out, g_out, bonus_out,
                     sgb_out, cur_out, carry_scr, *, seq_rows, d_b):
    tm = x_ref.shape[0]
    xn = _rms(x_ref[...], g1_ref[...]).astype(BF16)
    cur = jnp.dot(xn, wcur_ref[...], preferred_element_type=F32)
    rolled = pltpu.roll(cur, 1, axis=0)
    row = lax.broadcasted_iota(jnp.int32, (tm, 1), 0)
    if seq_rows is None:
        first_tile = pl.program_id(1) == 0
        carry = jnp.where(first_tile, 0.0, carry_scr[...])
        prev = jnp.where(row == 0, carry, rolled)
        carry_scr[...] = cur[tm - 1:tm, :]
        cur_out[...] = cur[tm - 8:tm, :]
    else:
        prev = jnp.where(row % seq_rows == 0, ext_ref[...], rolled)
        cur_out[...] = cur
    xs = cur + (prev - cur) * mu_ref[...]
    r = xs[:, 0:d_b]
    k = xs[:, d_b:2 * d_b]
    v = xs[:, 2 * d_b:3 * d_b]
    o = 3 * d_b
    xw = xs[:, o:o + R_W]
    xa = xs[:, o + R_W:o + R_W + R_A]
    xg = xs[:, o + R_W + R_A:o + R_W + R_A + R_G]
    wl = w0_ref[...] + _dot3(jnp.tanh(xw), w2_ref[...])
    lw = -jnp.exp(-_softplus(-wl) - 0.5)
    a = _sigmoid(a0_ref[...] + _dot3(xa, a2_ref[...]))
    g = _dot3(_sigmoid(xg), g2_ref[...])
    bd = bd_ref[...]
    kkr = k * kk_ref[...]
    n2 = _dot_exact_rhs(kkr * kkr, bd)
    kk = kkr / jnp.maximum(jnp.sqrt(n2), 1e-12)
    k2 = k * (1.0 + (a - 1.0) * ka_ref[...])
    bonus = _dot_exact_rhs(r * k2 * rk_ref[...], bd) * v
    r_out[...] = r
    k_out[...] = k2
    v_out[...] = v
    kk_out[...] = kk
    kka_out[...] = kk * a
    lw_out[...] = lw
    g_out[...] = g
    bonus_out[...] = bonus
    sgb_out[...] = _sigmoid(jnp.dot(xn, wgb_ref[...], preferred_element_type=F32))


def _branch_b(x3d, tm, seq_rows, ext, g1, wcur, wgb, mu, w0, w2, a0, a2, g2, k_k, k_a, r_k, bd):
    nb, t, d = x3d.shape
    d_b = w0.shape[1]
    shift_w = wcur.shape[1]
    nt = t // tm
    tok = lambda w: pl.BlockSpec((None, tm, w), lambda b, i: (b, i, 0))
    cur_rows = 8 if seq_rows is None else tm
    outs = [jax.ShapeDtypeStruct((nb, t, d_b), F32)] * 8 + [
        jax.ShapeDtypeStruct((nb, t, d), F32),
        jax.ShapeDtypeStruct((nb, nt * cur_rows, shift_w), F32)]
    out_specs = [tok(d_b)] * 8 + [tok(d), pl.BlockSpec((None, cur_rows, shift_w),
                                                       lambda b, i: (b, i, 0))]
    weights = (g1, wcur, wgb, mu)
    small = (w0, w2, a0, a2, g2, k_k, k_a, r_k, bd)
    return pl.pallas_call(
        functools.partial(_branch_b_kernel, seq_rows=seq_rows, d_b=d_b),
        grid=(nb, nt),
        in_specs=[tok(d)] + [_full_spec(w.shape) for w in weights]
        + [tok(shift_w) if seq_rows is not None else _full_spec(ext.shape)]
        + [_full_spec(w.shape) for w in small],
        out_specs=out_specs,
        out_shape=outs,
        scratch_shapes=[pltpu.VMEM((1, shift_w), F32)],
        compiler_params=pltpu.CompilerParams(dimension_semantics=("parallel", "arbitrary"),
                                             vmem_limit_bytes=VMEM_LIMIT),
        name="branch_b",
    )(x3d, *weights, ext, *small)


def _scan_kernel(r_ref, k_ref, v_ref, kk_ref, kka_ref, lw_ref, s0_ref, y_ref, sout_ref, s_scr,
                 *, chunk):
    L = chunk
    c = pl.program_id(1)

    @pl.when(c == 0)
    def _():
        s_scr[...] = s0_ref[...]

    n_pairs = s_scr.shape[0]
    row2 = lax.broadcasted_iota(jnp.int32, (2 * L, 1), 0)
    lane = lax.broadcasted_iota(jnp.int32, (1, PAIR), 1)
    stack_mask = (row2 >= L) == (lane >= HEAD_B)
    ri = lax.broadcasted_iota(jnp.int32, (2 * L, 2 * L), 0)
    ci = lax.broadcasted_iota(jnp.int32, (2 * L, 2 * L), 1)
    same = (ri >= L) == (ci >= L)
    strict = same & (ci < ri)
    incl = same & (ci <= ri)
    eye2 = (ri == ci).astype(F32)
    ti = lax.broadcasted_iota(jnp.int32, (L, L), 0)
    tj = lax.broadcasted_iota(jnp.int32, (L, L), 1)
    tri = (tj <= ti).astype(BF16)
    di = lax.broadcasted_iota(jnp.int32, (PAIR, PAIR), 0)
    dj = lax.broadcasted_iota(jnp.int32, (PAIR, PAIR), 1)
    diag = di == dj

    lw_all = lw_ref[...]
    cum_all = _dot_exact_lhs(tri, lw_all)

    def stack(x):
        return jnp.where(stack_mask, jnp.concatenate([x, x], axis=0), 0.0)

    n_double = max(L.bit_length() - 2, 0)
    pairs = range(n_pairs)
    prep = []
    for p in pairs:
        sl = slice(p * PAIR, (p + 1) * PAIR)
        lw = lw_all[:, sl]
        cum = cum_all[:, sl]
        cum_last = cum[L - 1:L, :]
        e_neg = jnp.exp(-cum)
        e_rel = jnp.exp(cum_last - cum)
        kk = kk_ref[:, sl]
        kka = kka_ref[:, sl]
        kx = k_ref[:, sl]
        prep.append(dict(
            ab=stack(-kk * jnp.exp(cum - lw)), rb=stack(r_ref[:, sl] * jnp.exp(cum)),
            bb=stack(kka * e_neg), kb=stack(kx * e_neg), bt=stack(kka * e_rel),
            kt=stack(kx * e_rel), vs=stack(v_ref[:, sl]), decay=jnp.exp(cum_last)))
    a_all = [lax.dot_general(jnp.concatenate([q["ab"], q["rb"]], axis=0).astype(BF16),
                             jnp.concatenate([q["bb"], q["kb"]], axis=0).astype(BF16),
                             (((1,), (1,)), ((), ())), preferred_element_type=F32)
             for q in prep]
    nk = [jnp.where(strict, a[:2 * L, :2 * L], 0.0) for a in a_all]
    tm = [eye2 + n for n in nk]
    for _ in range(n_double):
        nk = [_dot(n, n) for n in nk]
        tm = [t + _dot(t, n) for t, n in zip(tm, nk)]
    akv = [_dot(jnp.where(strict, a[:2 * L, 2 * L:], 0.0), q["vs"]) for a, q in zip(a_all, prep)]
    tx = [_dot(t, jnp.concatenate([q["ab"], kv], axis=1))
          for t, q, kv in zip(tm, prep, akv)]
    ry = [_dot(jnp.where(incl, a[2 * L:, :2 * L], 0.0), x) for a, x in zip(a_all, tx)]
    rkv = [_dot(jnp.where(incl, a[2 * L:, 2 * L:], 0.0), q["vs"]) for a, q in zip(a_all, prep)]
    mc = [_dot(jnp.concatenate([q["bt"].T, q["kt"].T], axis=1),
               jnp.concatenate([x, jnp.concatenate([jnp.zeros_like(q["vs"]), q["vs"]], axis=1)],
                               axis=0))
          for q, x in zip(prep, tx)]
    for p in pairs:
        sl = slice(p * PAIR, (p + 1) * PAIR)
        rt = prep[p]["rb"] + ry[p][:, :PAIR]
        y0 = ry[p][:, PAIR:] + rkv[p]
        m = mc[p][:, :PAIR] + jnp.where(diag, prep[p]["decay"], 0.0)
        s = s_scr[p]
        y_ref[:, sl] = _dot(rt[:L] + rt[L:], s) + (y0[:L] + y0[L:])
        s_scr[p] = _dot(m, s) + mc[p][:, PAIR:]

    @pl.when(c == pl.num_programs(1) - 1)
    def _():
        sout_ref[...] = s_scr[...]


def _scan(r, k, v, kk, kka, lw, s0, chunk):
    nb, t, d_b = r.shape
    n_pairs = d_b // PAIR
    tok = pl.BlockSpec((None, chunk, d_b), lambda b, c: (b, c, 0))
    st = pl.BlockSpec((None, n_pairs, PAIR, PAIR), lambda b, c: (b, 0, 0, 0))
    return pl.pallas_call(
        functools.partial(_scan_kernel, chunk=chunk),
        grid=(nb, t // chunk),
        in_specs=[tok] * 6 + [st],
        out_specs=[tok, st],
        out_shape=[jax.ShapeDtypeStruct((nb, t, d_b), F32),
                   jax.ShapeDtypeStruct((nb, n_pairs, PAIR, PAIR), F32)],
        scratch_shapes=[pltpu.VMEM((n_pairs, PAIR, PAIR), F32)],
        compiler_params=pltpu.CompilerParams(dimension_semantics=("parallel", "arbitrary"),
                                             vmem_limit_bytes=VMEM_LIMIT),
        name="rwkv_scan",
    )(r, k, v, kk, kka, lw, s0)


def _state_to_blockdiag(wkv):
    b, h, n, _ = wkv.shape
    st = jnp.swapaxes(wkv, -1, -2).reshape(b, h // 2, 2, n, n)
    z = jnp.zeros_like(st[:, :, 0])
    top = jnp.concatenate([st[:, :, 0], z], axis=-1)
    bot = jnp.concatenate([z, st[:, :, 1]], axis=-1)
    return jnp.concatenate([top, bot], axis=-2)


def _blockdiag_to_state(s):
    b, hp, _, _ = s.shape
    n = HEAD_B
    blocks = jnp.stack([s[:, :, :n, :n], s[:, :, n:, n:]], axis=2)
    return jnp.swapaxes(blocks, -1, -2).reshape(b, hp * 2, n, n)


def _pack_bf16_pairs(x):
    w = x.shape[1] // 2
    bits = lambda v: lax.bitcast_convert_type(v.astype(BF16).astype(F32), jnp.uint32)
    return (bits(x[:, :w]) >> 16) | (bits(x[:, w:]) & jnp.uint32(0xFFFF0000))


def _unpack_bf16_pairs(u):
    lo = lax.bitcast_convert_type(u << 16, F32)
    hi = lax.bitcast_convert_type(u & jnp.uint32(0xFFFF0000), F32)
    return lo, hi


def _merge_kernel(y_ref, g_ref, bonus_ref, sgb_ref, apart_ref, x_ref, lg_ref, lb_ref, bd_ref,
                  wbo_ref, wo_ref, g2_ref, rw_ref, rb_ref,
                  h_out, xp_out, topi_out, prob_out, rank_out, cnt_out, cnt_scr):
    step = pl.program_id(0)

    @pl.when(step == 0)
    def _():
        cnt_scr[...] = jnp.zeros_like(cnt_scr)

    bd = bd_ref[...]
    y = y_ref[...]
    inv_n = 1.0 / HEAD_B
    mu = _dot_exact_rhs(y, bd) * inv_n
    d = y - mu
    var = _dot_exact_rhs(d * d, bd) * inv_n
    yn = d * lax.rsqrt(var + GN_EPS) * lg_ref[...] + lb_ref[...]
    yb = ((yn + bonus_ref[...]) * g_ref[...]).astype(BF16)
    merged = apart_ref[...] + sgb_ref[...] * jnp.dot(yb, wbo_ref[...], preferred_element_type=F32)
    h = x_ref[...] + jnp.dot(merged.astype(BF16), wo_ref[...], preferred_element_type=F32)
    h_out[...] = h
    xn2 = _rms(h, g2_ref[...])
    xp_out[...] = _pack_bf16_pairs(xn2)
    logits = _dot3(xn2, rw_ref[...]) + rb_ref[...]
    tm, n_e = logits.shape
    idx = lax.broadcasted_iota(jnp.int32, logits.shape, 1).astype(F32)
    work = logits
    tops, hots, sels = [], [], []
    for _ in range(TOP_K):
        m = jnp.max(work, axis=-1, keepdims=True)
        sel = jnp.min(jnp.where(work == m, idx, float(n_e)), axis=-1, keepdims=True)
        hot = idx == sel
        tops.append(m)
        hots.append(hot)
        sels.append(sel)
        work = jnp.where(hot, -jnp.inf, work)
    es = [jnp.exp(t - tops[0]) for t in tops]
    denom = es[0] + es[1] + es[2] + es[3]
    topi_out[...] = jnp.concatenate(sels, axis=1).astype(jnp.int32)
    prob_out[...] = jnp.concatenate([e / denom for e in es], axis=1)
    hot_any = jnp.zeros_like(logits)
    for hot in hots:
        hot_any = hot_any + hot.astype(F32)
    ri = lax.broadcasted_iota(jnp.int32, (tm, tm), 0)
    ci = lax.broadcasted_iota(jnp.int32, (tm, tm), 1)
    before = _dot((ci < ri).astype(BF16), hot_any) + cnt_scr[...]
    rank_out[...] = jnp.concatenate(
        [jnp.sum(jnp.where(hot, before, 0.0), axis=-1, keepdims=True) for hot in hots],
        axis=1).astype(jnp.int32)
    total = cnt_scr[...] + jnp.sum(hot_any, axis=0, keepdims=True)
    cnt_scr[...] = total
    cnt_out[...] = total.astype(jnp.int32)


def _merge(y, g, bonus, sgb, apart, x2d, tm, lnx_g, lnx_b, bd, wbo, wo, g2n, rw, rb):
    n, d = x2d.shape
    d_b = y.shape[1]
    n_e = rw.shape[1]
    tok = lambda w: pl.BlockSpec((tm, w), lambda i: (i, 0))
    weights = (lnx_g, lnx_b, bd, wbo, wo, g2n, rw, rb)
    return pl.pallas_call(
        _merge_kernel,
        grid=(n // tm,),
        in_specs=[tok(d_b), tok(d_b), tok(d_b), tok(d), tok(d), tok(d)]
        + [_full_spec(w.shape) for w in weights],
        out_specs=[tok(d), tok(d // 2), tok(TOP_K), tok(TOP_K), tok(TOP_K),
                   pl.BlockSpec((1, n_e), lambda i: (0, 0))],
        out_shape=[jax.ShapeDtypeStruct((n, d), F32),
                   jax.ShapeDtypeStruct((n, d // 2), jnp.uint32),
                   jax.ShapeDtypeStruct((n, TOP_K), jnp.int32),
                   jax.ShapeDtypeStruct((n, TOP_K), F32),
                   jax.ShapeDtypeStruct((n, TOP_K), jnp.int32),
                   jax.ShapeDtypeStruct((1, n_e), jnp.int32)],
        scratch_shapes=[pltpu.VMEM((1, n_e), F32)],
        compiler_params=pltpu.CompilerParams(dimension_semantics=("arbitrary",),
                                             vmem_limit_bytes=VMEM_LIMIT),
        name="merge_router",
    )(y, g, bonus, sgb, apart, x2d, *weights)


MOE_TM = 256
MOE_FF_CHUNK = 512
CAST_ROWS = 256


def _route(topi_p, rank_p, cnt_p, topi_s, rank_s, cnt_s, n_tiles):
    n_e = cnt_p.shape[-1]
    cnt_p, cnt_s = cnt_p.reshape(n_e), cnt_s.reshape(n_e)
    cnt = cnt_p + cnt_s
    padded = (cnt + MOE_TM - 1) // MOE_TM * MOE_TM
    ends = jnp.cumsum(padded)
    base = ends - padded
    lookup = lambda table, idx: jnp.sum(
        jnp.where(idx[..., None] == jnp.arange(n_e), table, 0), axis=-1)
    pos_p = lookup(base, topi_p) + rank_p
    pos_s = lookup(base + cnt_p, topi_s) + rank_s
    start = jnp.arange(n_tiles, dtype=jnp.int32) * MOE_TM
    n_used = (ends[-1] // MOE_TM).astype(jnp.int32)
    tile_e = jnp.minimum(jnp.sum(start[:, None] >= ends[None, :], axis=1), n_e - 1)
    tile_rows = jnp.clip(lookup(base + cnt, tile_e) - start, 0, MOE_TM)
    tile_rows = jnp.where(start < ends[-1], tile_rows, 0)
    return (pos_p.astype(jnp.int32), pos_s.astype(jnp.int32), tile_e.astype(jnp.int32),
            tile_rows.astype(jnp.int32), n_used.reshape(1))


def _moe_kernel(te_ref, rows_ref, nused_ref, xs_ref, wgu_ref, bgu_ref, wd_ref, bdn_ref, ys_ref,
                wgu_b, wd_b):
    i = pl.program_id(0)
    e = te_ref[i]
    rows = rows_ref[i]
    first_of_expert = jnp.logical_or(i == 0, e != te_ref[jnp.maximum(i - 1, 0)])
    d, gu = wgu_ref.shape
    d_ff = wd_ref.shape[0]

    @pl.when(jnp.logical_and(first_of_expert, rows > 0))
    def _():
        for r0 in range(0, d, CAST_ROWS):
            wgu_b[r0:r0 + CAST_ROWS, :] = wgu_ref[r0:r0 + CAST_ROWS, :].astype(BF16)
        for r0 in range(0, d_ff, CAST_ROWS):
            wd_b[r0:r0 + CAST_ROWS, :] = wd_ref[r0:r0 + CAST_ROWS, :].astype(BF16)

    @pl.when(rows > 0)
    def _():
        lo, hi = _unpack_bf16_pairs(xs_ref[...])
        valid = lax.broadcasted_iota(jnp.int32, lo.shape, 0) < rows
        x_lo = jnp.where(valid, lo, 0.0).astype(BF16)
        x_hi = jnp.where(valid, hi, 0.0).astype(BF16)
        half = d // 2

        def proj(c0):
            cols = slice(c0, c0 + MOE_FF_CHUNK)
            return (jnp.dot(x_lo, wgu_b[:half, cols], preferred_element_type=F32)
                    + jnp.dot(x_hi, wgu_b[half:, cols], preferred_element_type=F32)
                    + bgu_ref[:, cols])

        y = jnp.zeros((x_lo.shape[0], d), F32)
        for f in range(d_ff // MOE_FF_CHUNK):
            c0 = f * MOE_FF_CHUNK
            gate = jnp.minimum(proj(c0), SWIGLU_LIMIT)
            up = jnp.clip(proj(d_ff + c0), -SWIGLU_LIMIT, SWIGLU_LIMIT)
            hh = (up + 1.0) * gate * _sigmoid(gate * SWIGLU_ALPHA)
            y = y + jnp.dot(hh.astype(BF16), wd_b[c0:c0 + MOE_FF_CHUNK, :],
                            preferred_element_type=F32)
        ys_ref[...] = _pack_bf16_pairs(y + bdn_ref[...])


def _moe(xs, tile_e, tile_rows, n_used, wgu, bgu, wd, bdn):
    p_rows, half = xs.shape
    n_e, d, gu = wgu.shape
    d_ff = wd.shape[1]
    n_tiles = p_rows // MOE_TM
    row_blk = lambda i, te, rows, nu: (jnp.minimum(i, nu[0] - 1), 0)
    w_blk = lambda i, te, rows, nu: (te[i], 0, 0)
    grid_spec = pltpu.PrefetchScalarGridSpec(
        num_scalar_prefetch=3,
        grid=(n_tiles,),
        in_specs=[pl.BlockSpec((MOE_TM, half), row_blk),
                  pl.BlockSpec((None, d, gu), w_blk),
                  pl.BlockSpec((None, 1, gu), w_blk),
                  pl.BlockSpec((None, d_ff, d), w_blk),
                  pl.BlockSpec((None, 1, d), w_blk)],
        out_specs=pl.BlockSpec((MOE_TM, half), row_blk),
        scratch_shapes=[pltpu.VMEM((d, gu), BF16), pltpu.VMEM((d_ff, d), BF16)])
    return pl.pallas_call(
        _moe_kernel,
        grid_spec=grid_spec,
        out_shape=jax.ShapeDtypeStruct((p_rows, half), jnp.uint32),
        compiler_params=pltpu.CompilerParams(dimension_semantics=("arbitrary",),
                                             vmem_limit_bytes=VMEM_LIMIT),
        name="moe_experts",
    )(tile_e, tile_rows, n_used, xs, wgu, bgu, wd, bdn)


def _combine_kernel(h_ref, yg_ref, prob_ref, gf_ref, out_ref):
    half = h_ref.shape[1] // 2
    prob = prob_ref[...]
    acc_lo = jnp.zeros((h_ref.shape[0], half), F32)
    acc_hi = jnp.zeros((h_ref.shape[0], half), F32)
    for k in range(TOP_K):
        lo, hi = _unpack_bf16_pairs(yg_ref[k])
        pk = prob[:, k:k + 1]
        acc_lo = acc_lo + pk * lo
        acc_hi = acc_hi + pk * hi
    z = h_ref[...] + jnp.concatenate([acc_lo, acc_hi], axis=1)
    out_ref[...] = _rms(z, gf_ref[...])


def _combine(h, yg, row0, prob, gf, tm):
    n, d = h.shape
    tok = lambda w: pl.BlockSpec((tm, w), lambda i: (i, 0))
    t0 = row0 // tm
    return pl.pallas_call(
        _combine_kernel,
        grid=(n // tm,),
        in_specs=[tok(d), pl.BlockSpec((TOP_K, tm, d // 2), lambda i: (0, i + t0, 0)),
                  tok(TOP_K), _full_spec(gf.shape)],
        out_specs=tok(d),
        out_shape=jax.ShapeDtypeStruct((n, d), F32),
        compiler_params=pltpu.CompilerParams(dimension_semantics=("parallel",),
                                             vmem_limit_bytes=VMEM_LIMIT),
        name="moe_combine",
    )(h, yg, prob, gf)


SC_CORES = 2
SC_SUBCORES = 16
SC_WORKERS = SC_CORES * SC_SUBCORES
SC_MAX_INDEX = 128


def _sc_chunk(rows_per_worker):
    for c in range(SC_MAX_INDEX, 7, -8):
        if rows_per_worker % c == 0:
            return c
    raise ValueError(f"no 8-aligned chunk divides {rows_per_worker} rows")


def _sc_mesh():
    return plsc.VectorSubcoreMesh(core_axis_name="c", subcore_axis_name="s")


def _sc_worker():
    return lax.axis_index("s") * SC_CORES + lax.axis_index("c")


def _scatter_rows(x, pos_t, p_rows):
    n, w = x.shape
    n_k = pos_t.shape[0]
    per = n // SC_WORKERS
    chunk = _sc_chunk(per)
    pos_flat = pos_t.reshape(n_k * n)

    def body(x_hbm, pos_hbm, xs_hbm, idx_v, rows_v, sem):
        base = _sc_worker() * per

        @pl.loop(0, per // chunk)
        def _(c):
            off = pl.multiple_of(base + c * chunk, 8)
            pltpu.sync_copy(x_hbm.at[pl.ds(off, chunk)], rows_v)
            for k in range(n_k):
                pltpu.sync_copy(pos_hbm.at[pl.ds(pl.multiple_of(k * n + off, 8), chunk)], idx_v)
                pltpu.async_copy(rows_v, xs_hbm.at[idx_v], sem).wait()

    return pl.kernel(
        body, out_type=jax.ShapeDtypeStruct((p_rows, w), x.dtype), mesh=_sc_mesh(),
        scratch_types=[pltpu.VMEM((chunk,), jnp.int32), pltpu.VMEM((chunk, w), x.dtype),
                       pltpu.SemaphoreType.DMA],
        name="sc_scatter_rows")(x, pos_flat)


def _gather_rows(table, idx):
    n = idx.shape[0]
    w = table.shape[1]
    per = n // SC_WORKERS
    chunk = _sc_chunk(per)

    def body(table_hbm, idx_hbm, out_hbm, idx_v, rows_v, sem):
        base = _sc_worker() * per

        @pl.loop(0, per // chunk)
        def _(c):
            off = pl.multiple_of(base + c * chunk, 8)
            pltpu.sync_copy(idx_hbm.at[pl.ds(off, chunk)], idx_v)
            pltpu.async_copy(table_hbm.at[idx_v], rows_v, sem).wait()
            pltpu.sync_copy(rows_v, out_hbm.at[pl.ds(off, chunk)])

    return pl.kernel(
        body, out_type=jax.ShapeDtypeStruct((n, w), table.dtype), mesh=_sc_mesh(),
        scratch_types=[pltpu.VMEM((chunk,), jnp.int32), pltpu.VMEM((chunk, w), table.dtype),
                       pltpu.SemaphoreType.DMA],
        name="sc_gather_rows")(table, idx)


def _mix_matrix(ws, tm, seq_rows):
    causal = jnp.tril(jnp.ones((CHUNK, CHUNK), dtype=bool))
    w = jnp.where(causal[None], ws, 0.0)[:, :seq_rows, :seq_rows]
    eye = jnp.eye(tm // seq_rows, dtype=ws.dtype)
    return jnp.einsum("ab,gts->gatbs", eye, w).reshape(ws.shape[0], tm, tm).astype(BF16)


def _stream(x, shift_in, wkv_in, p, tm, seq_rows, scan_chunk):
    nb, t, d = x.shape
    n = nb * t
    x2d = x.reshape(n, d)
    rows = CHUNK if seq_rows is None else seq_rows
    wmix = _mix_matrix(p["w_spatial"], tm, rows)
    pos = jnp.arange(tm) % rows
    d_a = p["wu"].shape[1]
    bias_full = jnp.repeat(p["b_spatial"].T[pos], d_a // G_A, axis=1)
    apart, vn = _branch_a(x2d, tm, p["norm1_g"], p["wu"], p["wv"], p["wga"], p["vnorm_g"],
                          p["vnorm_b"], wmix, bias_full, p["w_a_out"])
    if seq_rows is None:
        xb, ext = x, jnp.zeros((1, p["wcur"].shape[1]), F32)
    else:
        xb = x2d.reshape(n // tm, tm, d)
        ext = jnp.repeat(shift_in, seq_rows, axis=0).reshape(n // tm, tm, -1)
    outs = _branch_b(xb, tm, seq_rows, ext, p["norm1_g"], p["wcur"], p["wgb"], p["mu_shift"],
                     p["w0"], p["w2"], p["a0"], p["a2"], p["g2"], p["k_k"], p["k_a"], p["r_k"],
                     p["bd"])
    r, k2, v, kk, kka, lw, g, bonus, sgb, cur = outs
    d_b = r.shape[-1]
    if seq_rows is None:
        shift_out = cur[:, -1, :]
        scan_in = [a.reshape(nb, t, d_b) for a in (r, k2, v, kk, kka, lw)]
    else:
        shift_out = cur.reshape(nb, t, -1)[:, -1, :]
        pad = scan_chunk - t
        scan_in = [jnp.pad(a.reshape(nb, t, d_b), ((0, 0), (0, pad), (0, 0)))
                   for a in (r, k2, v, kk, kka, lw)]
    y, s_out = _scan(*scan_in, _state_to_blockdiag(wkv_in), scan_chunk)
    y2d = y[:, :t].reshape(n, d_b)
    flat = lambda a: a.reshape(n, a.shape[-1])
    routed = _merge(y2d, flat(g), flat(bonus), flat(sgb), apart, x2d, tm, p["lnx_g"],
                    p["lnx_b"], p["bd"], p["w_b_out"], p["w_out"], p["norm2_g"],
                    p["router_w"], p["router_b"])
    return routed, vn, shift_out, _blockdiag_to_state(s_out)


def kernel(x_prompt, x_sample, state_shift, state_wkv, norm1_g, w_in, mu_shift, vnorm_g, vnorm_b, w_spatial, b_spatial, w_a_out, w0, w2, a0, a2, g2, k_k, k_a, r_k, lnx_g, lnx_b, w_b_out, w_out, norm2_g, router_w, router_b, exp_w_gu, exp_b_gu, exp_w_down, exp_b_down, normf_g):
    depth = w_in.shape[0]
    assert depth == 1, "the final norm is fused into the single layer's MoE call"
    d_model = x_prompt.shape[-1]
    d_b = w0.shape[-1]
    shift_w = mu_shift.shape[-1]
    d_a = vnorm_g.shape[-1]
    bp, tp, _ = x_prompt.shape
    bs, ts, _ = x_sample.shape
    head_id = jnp.arange(d_b) // HEAD_B
    bd = (head_id[:, None] == head_id[None, :]).astype(BF16)
    row = lambda a: a.reshape(1, -1)

    hp, hs = x_prompt, x_sample
    shift_p = jnp.zeros((depth, bp, shift_w), state_shift.dtype)
    wkv_p = jnp.zeros((depth, bp) + state_wkv.shape[2:], state_wkv.dtype)
    vrows, shifts_p, wkvs_p, shifts_s, wkvs_s = [], [], [], [], []
    for l in range(depth):
        wi = w_in[l].astype(BF16)
        o = shift_w
        p = dict(
            norm1_g=row(norm1_g[l]), wcur=wi[:, :o], wu=wi[:, o:o + d_a],
            wv=wi[:, o + d_a:o + 2 * d_a], wga=wi[:, o + 2 * d_a:o + 2 * d_a + d_model],
            wgb=wi[:, o + 2 * d_a + d_model:], mu_shift=row(mu_shift[l]),
            vnorm_g=row(vnorm_g[l]), vnorm_b=row(vnorm_b[l]), w_spatial=w_spatial[l],
            b_spatial=b_spatial[l], w_a_out=w_a_out[l].astype(BF16), w0=row(w0[l]), w2=w2[l],
            a0=row(a0[l]), a2=a2[l], g2=g2[l], k_k=row(k_k[l]), k_a=row(k_a[l]),
            r_k=row(r_k[l]), lnx_g=row(lnx_g[l]), lnx_b=row(lnx_b[l]),
            w_b_out=w_b_out[l].astype(BF16), w_out=w_out[l].astype(BF16),
            norm2_g=row(norm2_g[l]), router_w=router_w[l], router_b=row(router_b[l]), bd=bd)
        routed_p, _, sh_p, s_p = _stream(hp, shift_p[l], wkv_p[l], p, 512, None, 64)
        routed_s, vn_s, sh_s, s_s = _stream(hs, state_shift[l], state_wkv[l], p, 512, ts, 8)
        h_p, xp_p, topi_p, prob_p, rank_p, cnt_p = routed_p
        h_s, xp_s, topi_s, prob_s, rank_s, cnt_s = routed_s
        n_p, n_s = h_p.shape[0], h_s.shape[0]
        n_e = router_w.shape[-1]
        n_tiles = (n_p + n_s) * TOP_K // MOE_TM + n_e
        pos_p, pos_s, tile_e, tile_rows, n_used = _route(topi_p, rank_p, cnt_p, topi_s, rank_s,
                                                         cnt_s, n_tiles)
        pos_t = jnp.concatenate([pos_p, pos_s], axis=0).T
        xs = _scatter_rows(jnp.concatenate([xp_p, xp_s], axis=0), pos_t, n_tiles * MOE_TM)
        ys = _moe(xs, tile_e, tile_rows, n_used, exp_w_gu[l], exp_b_gu[l][:, None, :],
                  exp_w_down[l], exp_b_down[l][:, None, :])
        yg = _gather_rows(ys, pos_t.reshape(-1)).reshape(TOP_K, n_p + n_s, d_model // 2)
        gf = row(normf_g)
        hp = _combine(h_p, yg, 0, prob_p, gf, 512).reshape(bp, tp, d_model)
        hs = _combine(h_s, yg, n_p, prob_s, gf, 512).reshape(bs, ts, d_model)
        vrows.append(vn_s.reshape(bs, ts, d_a))
        shifts_p.append(sh_p)
        wkvs_p.append(s_p)
        shifts_s.append(sh_s)
        wkvs_s.append(s_s)
    return (hp, hs, jnp.stack(shifts_p), jnp.stack(wkvs_p), jnp.stack(shifts_s),
            jnp.stack(wkvs_s), jnp.stack(vrows))
```

```python
import functools

import jax
import jax.numpy as jnp
from jax import lax
from jax.experimental import pallas as pl
from jax.experimental.pallas import tpu as pltpu
from jax.experimental.pallas import tpu_sc as plsc

F32 = jnp.float32
BF16 = jnp.bfloat16

CHUNK = 128
G_A = 8
HEAD_B = 64
R_W, R_A, R_G = 64, 64, 128
TOP_K = 4
SWIGLU_LIMIT = 7.0
SWIGLU_ALPHA = 1.702
EPS = 1e-5
GN_EPS = HEAD_B * 1e-5

LANES = 128
PAIR = 2 * HEAD_B
VMEM_LIMIT = 56 * 1024 * 1024


def _dot(a, b):
    return jnp.dot(a.astype(BF16), b.astype(BF16), preferred_element_type=F32)


def _split(x):
    hi = x.astype(BF16)
    lo = (x - hi.astype(F32)).astype(BF16)
    return hi, lo


def _dot3(a, b):
    ah, al = _split(a)
    bh, bl = _split(b)
    return (jnp.dot(ah, bh, preferred_element_type=F32)
            + jnp.dot(ah, bl, preferred_element_type=F32)
            + jnp.dot(al, bh, preferred_element_type=F32))


def _dot_exact_rhs(a, b_bf16):
    ah, al = _split(a)
    return (jnp.dot(ah, b_bf16, preferred_element_type=F32)
            + jnp.dot(al, b_bf16, preferred_element_type=F32))


def _dot_exact_lhs(a_bf16, b):
    bh, bl = _split(b)
    bm = b - bh.astype(F32) - bl.astype(F32)
    return (jnp.dot(a_bf16, bh, preferred_element_type=F32)
            + jnp.dot(a_bf16, bl, preferred_element_type=F32)
            + jnp.dot(a_bf16, bm.astype(BF16), preferred_element_type=F32))


def _rms(x, g):
    return x * lax.rsqrt(jnp.mean(x * x, axis=-1, keepdims=True) + EPS) * g


def _sigmoid(x):
    return 1.0 / (1.0 + jnp.exp(-x))


def _softplus(z):
    return jnp.maximum(z, 0.0) + jnp.log(1.0 + jnp.exp(-jnp.abs(z)))


def _full_spec(shape):
    nd = len(shape)
    return pl.BlockSpec(shape, lambda *_: (0,) * nd, pipeline_mode=pl.Buffered(1))


def _branch_a_kernel(x_ref, g1_ref, wu_ref, wv_ref, wga_ref, vg_ref, vb_ref, wmix_ref,
                     bias_ref, wao_ref, apart_ref, vn_ref):
    xn = _rms(x_ref[...], g1_ref[...]).astype(BF16)
    v = jnp.dot(xn, wv_ref[...], preferred_element_type=F32)
    mu = jnp.mean(v, axis=-1, keepdims=True)
    d = v - mu
    var = jnp.mean(d * d, axis=-1, keepdims=True)
    vn = d * lax.rsqrt(var + EPS) * vg_ref[...] + vb_ref[...]
    vn_ref[...] = vn
    vnb = vn.astype(BF16)
    c_a = vnb.shape[1] // G_A
    mixed = jnp.concatenate(
        [jnp.dot(wmix_ref[g], vnb[:, g * c_a:(g + 1) * c_a], preferred_element_type=F32)
         for g in range(G_A)], axis=1) + bias_ref[...]
    u = jnp.dot(xn, wu_ref[...], preferred_element_type=F32)
    ya = (u * mixed).astype(BF16)
    ga = jnp.dot(xn, wga_ref[...], preferred_element_type=F32)
    apart_ref[...] = _sigmoid(ga) * jnp.dot(ya, wao_ref[...], preferred_element_type=F32)


def _branch_a(x2d, tm, g1, wu, wv, wga, vg, vb, wmix, bias_full, wao):
    n, d = x2d.shape
    d_a = wu.shape[1]
    tok = lambda w: pl.BlockSpec((tm, w), lambda i: (i, 0))
    return pl.pallas_call(
        _branch_a_kernel,
        grid=(n // tm,),
        in_specs=[tok(d), _full_spec(g1.shape), _full_spec(wu.shape), _full_spec(wv.shape),
                  _full_spec(wga.shape), _full_spec(vg.shape), _full_spec(vb.shape),
                  _full_spec(wmix.shape), _full_spec(bias_full.shape), _full_spec(wao.shape)],
        out_specs=[tok(d), tok(d_a)],
        out_shape=[jax.ShapeDtypeStruct((n, d), F32), jax.ShapeDtypeStruct((n, d_a), F32)],
        compiler_params=pltpu.CompilerParams(dimension_semantics=("parallel",),
                                             vmem_limit_bytes=VMEM_LIMIT),
        name="branch_a",
    )(x2d, g1, wu, wv, wga, vg, vb, wmix, bias_full, wao)


def _branch_b_kernel(x_ref, g1_ref, wcur_ref, wgb_ref, mu_ref, ext_ref, w0_ref, w2_ref,
                     a0_ref, a2_ref, g2_ref, kk_ref, ka_ref, rk_ref, bd_ref,
                     r_out, k_out, v_out, kk_out, kka_out, lw_out, g_out, bonus_out,
                     sgb_out, cur_out, carry_scr, *, seq_rows, d_b):
    tm = x_ref.shape[0]
    xn = _rms(x_ref[...], g1_ref[...]).astype(BF16)
    cur = jnp.dot(xn, wcur_ref[...], preferred_element_type=F32)
    rolled = pltpu.roll(cur, 1, axis=0)
    row = lax.broadcasted_iota(jnp.int32, (tm, 1), 0)
    if seq_rows is None:
        first_tile = pl.program_id(1) == 0
        carry = jnp.where(first_tile, 0.0, carry_scr[...])
        prev = jnp.where(row == 0, carry, rolled)
        carry_scr[...] = cur[tm - 1:tm, :]
        cur_out[...] = cur[tm - 8:tm, :]
    else:
        prev = jnp.where(row % seq_rows == 0, ext_ref[...], rolled)
        cur_out[...] = cur
    xs = cur + (prev - cur) * mu_ref[...]
    r = xs[:, 0:d_b]
    k = xs[:, d_b:2 * d_b]
    v = xs[:, 2 * d_b:3 * d_b]
    o = 3 * d_b
    xw = xs[:, o:o + R_W]
    xa = xs[:, o + R_W:o + R_W + R_A]
    xg = xs[:, o + R_W + R_A:o + R_W + R_A + R_G]
    wl = w0_ref[...] + _dot3(jnp.tanh(xw), w2_ref[...])
    lw = -jnp.exp(-_softplus(-wl) - 0.5)
    a = _sigmoid(a0_ref[...] + _dot3(xa, a2_ref[...]))
    g = _dot3(_sigmoid(xg), g2_ref[...])
    bd = bd_ref[...]
    kkr = k * kk_ref[...]
    n2 = _dot_exact_rhs(kkr * kkr, bd)
    kk = kkr / jnp.maximum(jnp.sqrt(n2), 1e-12)
    k2 = k * (1.0 + (a - 1.0) * ka_ref[...])
    bonus = _dot_exact_rhs(r * k2 * rk_ref[...], bd) * v
    r_out[...] = r
    k_out[...] = k2
    v_out[...] = v
    kk_out[...] = kk
    kka_out[...] = kk * a
    lw_out[...] = lw
    g_out[...] = g
    bonus_out[...] = bonus
    sgb_out[...] = _sigmoid(jnp.dot(xn, wgb_ref[...], preferred_element_type=F32))


def _branch_b(x3d, tm, seq_rows, ext, g1, wcur, wgb, mu, w0, w2, a0, a2, g2, k_k, k_a, r_k, bd):
    nb, t, d = x3d.shape
    d_b = w0.shape[1]
    shift_w = wcur.shape[1]
    nt = t // tm
    tok = lambda w: pl.BlockSpec((None, tm, w), lambda b, i: (b, i, 0))
    cur_rows = 8 if seq_rows is None else tm
    outs = [jax.ShapeDtypeStruct((nb, t, d_b), F32)] * 8 + [
        jax.ShapeDtypeStruct((nb, t, d), F32),
        jax.ShapeDtypeStruct((nb, nt * cur_rows, shift_w), F32)]
    out_specs = [tok(d_b)] * 8 + [tok(d), pl.BlockSpec((None, cur_rows, shift_w),
                                                       lambda b, i: (b, i, 0))]
    weights = (g1, wcur, wgb, mu)
    small = (w0, w2, a0, a2, g2, k_k, k_a, r_k, bd)
    return pl.pallas_call(
        functools.partial(_branch_b_kernel, seq_rows=seq_rows, d_b=d_b),
        grid=(nb, nt),
        in_specs=[tok(d)] + [_full_spec(w.shape) for w in weights]
        + [tok(shift_w) if seq_rows is not None else _full_spec(ext.shape)]
        + [_full_spec(w.shape) for w in small],
        out_specs=out_specs,
        out_shape=outs,
        scratch_shapes=[pltpu.VMEM((1, shift_w), F32)],
        compiler_params=pltpu.CompilerParams(dimension_semantics=("parallel", "arbitrary"),
                                             vmem_limit_bytes=VMEM_LIMIT),
        name="branch_b",
    )(x3d, *weights, ext, *small)


def _scan_kernel(r_ref, k_ref, v_ref, kk_ref, kka_ref, lw_ref, s0_ref, y_ref, sout_ref, s_scr,
                 *, chunk):
    L = chunk
    W2 = 2 * L
    c = pl.program_id(1)

    @pl.when(c == 0)
    def _():
        s_scr[...] = s0_ref[...]

    n_seq, n_pairs = s_scr.shape[:2]
    iota = lambda shape, dim: lax.broadcasted_iota(jnp.int32, shape, dim)
    stack_mask = (iota((W2, 1), 0) >= L) == (iota((1, PAIR), 1) >= HEAD_B)
    bd_mask = (iota((W2, 1), 0) >= L) == (iota((1, W2), 1) >= L)
    assert L & (L - 1) == 0, "chunk length must be a power of two"
    rw = iota((L, W2), 0)
    cw = iota((L, W2), 1) & (L - 1)
    strict_w = cw < rw
    eye_w = (cw == rw).astype(F32)
    incl_w2 = (iota((L, 2 * W2), 1) & (L - 1)) <= iota((L, 2 * W2), 0)
    tri = (iota((L, L), 1) <= iota((L, L), 0)).astype(BF16)
    di = iota((PAIR, PAIR), 0)
    dj = iota((PAIR, PAIR), 1)
    diag = di == dj
    head_diag = (di >= HEAD_B) == (dj >= HEAD_B)

    def stack(x):
        return jnp.where(stack_mask, jnp.concatenate([x, x], axis=0), 0.0)

    def bd(xw):
        return jnp.where(bd_mask, jnp.concatenate([xw, xw], axis=0), 0.0)

    n_double = max(L.bit_length() - 2, 0)
    prep = []
    for s in range(n_seq):
        lw_all = lw_ref[s]
        cum_all = _dot_exact_lhs(tri, lw_all)
        for p in range(n_pairs):
            sl = slice(p * PAIR, (p + 1) * PAIR)
            cum = cum_all[:, sl]
            cum_last = cum[L - 1:L, :]
            e_neg = jnp.exp(-cum)
            e_rel = jnp.exp(cum_last - cum)
            kk = kk_ref[s, :, sl]
            kka = kka_ref[s, :, sl]
            kx = k_ref[s, :, sl]
            prep.append(dict(
                s=s, p=p, sl=sl, ab=-kk * jnp.exp(cum - lw_all[:, sl]),
                rb=r_ref[s, :, sl] * jnp.exp(cum), bb=kka * e_neg, kb=kx * e_neg,
                bt=kka * e_rel, kt=kx * e_rel, v=v_ref[s, :, sl], decay=jnp.exp(cum_last)))
    a_w = [lax.dot_general(
        jnp.concatenate([q["ab"], q["rb"]], axis=0).astype(BF16),
        jnp.concatenate([stack(q["bb"]), stack(q["kb"])], axis=0).astype(BF16),
        (((1,), (1,)), ((), ())), preferred_element_type=F32) for q in prep]
    nw = [jnp.where(strict_w, a[:L, :W2], 0.0) for a in a_w]
    tw = [eye_w + n for n in nw]
    nbd = [bd(n) for n in nw]
    for _ in range(n_double):
        nw = [_dot(n, b) for n, b in zip(nw, nbd)]
        nbd = [bd(n) for n in nw]
        tw = [t + _dot(t, b) for t, b in zip(tw, nbd)]
    akv = [_dot(jnp.where(strict_w, a[:L, W2:], 0.0), stack(q["v"])) for a, q in zip(a_w, prep)]
    tx = [_dot(t, jnp.concatenate([stack(q["ab"]), stack(kv)], axis=1))
          for t, q, kv in zip(tw, prep, akv)]
    ry = [_dot(jnp.where(incl_w2, a[L:], 0.0),
               jnp.concatenate(
                   [jnp.concatenate([stack(x[:, :PAIR]), stack(x[:, PAIR:])], axis=1),
                    jnp.concatenate([jnp.zeros((W2, PAIR), F32), stack(q["v"])], axis=1)], axis=0))
          for a, x, q in zip(a_w, tx, prep)]
    for q, x, y in zip(prep, tx, ry):
        st = s_scr[q["s"], q["p"]]
        us = _dot(jnp.concatenate([x[:, :PAIR], q["rb"] + y[:, :PAIR]], axis=0), st)
        y_ref[q["s"], :, q["sl"]] = us[L:] + y[:, PAIR:]
        u = us[:L] + x[:, PAIR:]
        lhs = jnp.concatenate([jnp.where(diag, q["decay"], 0.0),
                               jnp.concatenate([q["bt"], q["kt"]], axis=0).T], axis=1)
        g = _dot(lhs, jnp.concatenate([st, u, q["v"]], axis=0))
        s_scr[q["s"], q["p"]] = jnp.where(head_diag, g, 0.0)

    @pl.when(c == pl.num_programs(1) - 1)
    def _():
        sout_ref[...] = s_scr[...]


def _scan(r, k, v, kk, kka, lw, s0, chunk, n_seq):
    nb, t, d_b = r.shape
    n_pairs = d_b // PAIR
    tok = pl.BlockSpec((n_seq, chunk, d_b), lambda b, c: (b, c, 0))
    st = pl.BlockSpec((n_seq, n_pairs, PAIR, PAIR), lambda b, c: (b, 0, 0, 0))
    return pl.pallas_call(
        functools.partial(_scan_kernel, chunk=chunk),
        grid=(nb // n_seq, t // chunk),
        in_specs=[tok] * 6 + [st],
        out_specs=[tok, st],
        out_shape=[jax.ShapeDtypeStruct((nb, t, d_b), F32),
                   jax.ShapeDtypeStruct((nb, n_pairs, PAIR, PAIR), F32)],
        scratch_shapes=[pltpu.VMEM((n_seq, n_pairs, PAIR, PAIR), F32)],
        compiler_params=pltpu.CompilerParams(dimension_semantics=("parallel", "arbitrary"),
                                             vmem_limit_bytes=VMEM_LIMIT),
        name="rwkv_scan",
    )(r, k, v, kk, kka, lw, s0)


def _state_to_blockdiag(wkv):
    b, h, n, _ = wkv.shape
    st = jnp.swapaxes(wkv, -1, -2).reshape(b, h // 2, 2, n, n)
    z = jnp.zeros_like(st[:, :, 0])
    top = jnp.concatenate([st[:, :, 0], z], axis=-1)
    bot = jnp.concatenate([z, st[:, :, 1]], axis=-1)
    return jnp.concatenate([top, bot], axis=-2)


def _blockdiag_to_state(s):
    b, hp, _, _ = s.shape
    n = HEAD_B
    blocks = jnp.stack([s[:, :, :n, :n], s[:, :, n:, n:]], axis=2)
    return jnp.swapaxes(blocks, -1, -2).reshape(b, hp * 2, n, n)


def _pack_bf16_pairs(x):
    w = x.shape[1] // 2
    bits = lambda v: lax.bitcast_convert_type(v.astype(BF16).astype(F32), jnp.uint32)
    return (bits(x[:, :w]) >> 16) | (bits(x[:, w:]) & jnp.uint32(0xFFFF0000))


def _unpack_bf16_pairs(u):
    lo = lax.bitcast_convert_type(u << 16, F32)
    hi = lax.bitcast_convert_type(u & jnp.uint32(0xFFFF0000), F32)
    return lo, hi


def _merge_kernel(y_ref, g_ref, bonus_ref, sgb_ref, apart_ref, x_ref, lg_ref, lb_ref, bd_ref,
                  wbo_ref, wo_ref, g2_ref, rw_ref, rb_ref,
                  h_out, xp_out, topi_out, prob_out, rank_out, cnt_out, cnt_scr):
    step = pl.program_id(0)

    @pl.when(step == 0)
    def _():
        cnt_scr[...] = jnp.zeros_like(cnt_scr)

    bd = bd_ref[...]
    y = y_ref[...]
    inv_n = 1.0 / HEAD_B
    mu = _dot_exact_rhs(y, bd) * inv_n
    d = y - mu
    var = _dot_exact_rhs(d * d, bd) * inv_n
    yn = d * lax.rsqrt(var + GN_EPS) * lg_ref[...] + lb_ref[...]
    yb = ((yn + bonus_ref[...]) * g_ref[...]).astype(BF16)
    merged = apart_ref[...] + sgb_ref[...] * jnp.dot(yb, wbo_ref[...], preferred_element_type=F32)
    h = x_ref[...] + jnp.dot(merged.astype(BF16), wo_ref[...], preferred_element_type=F32)
    h_out[...] = h
    xn2 = _rms(h, g2_ref[...])
    xp_out[...] = _pack_bf16_pairs(xn2)
    logits = _dot3(xn2, rw_ref[...]) + rb_ref[...]
    tm, n_e = logits.shape
    idx = lax.broadcasted_iota(jnp.int32, logits.shape, 1).astype(F32)
    work = logits
    tops, hots, sels = [], [], []
    for _ in range(TOP_K):
        m = jnp.max(work, axis=-1, keepdims=True)
        sel = jnp.min(jnp.where(work == m, idx, float(n_e)), axis=-1, keepdims=True)
        hot = idx == sel
        tops.append(m)
        hots.append(hot)
        sels.append(sel)
        work = jnp.where(hot, -jnp.inf, work)
    es = [jnp.exp(t - tops[0]) for t in tops]
    denom = es[0] + es[1] + es[2] + es[3]
    topi_out[...] = jnp.concatenate(sels, axis=1).astype(jnp.int32)
    prob_out[...] = jnp.concatenate([e / denom for e in es], axis=1)
    hot_any = jnp.zeros_like(logits)
    for hot in hots:
        hot_any = hot_any + hot.astype(F32)
    ri = lax.broadcasted_iota(jnp.int32, (tm, tm), 0)
    ci = lax.broadcasted_iota(jnp.int32, (tm, tm), 1)
    before = _dot((ci < ri).astype(BF16), hot_any) + cnt_scr[...]
    rank_out[...] = jnp.concatenate(
        [jnp.sum(jnp.where(hot, before, 0.0), axis=-1, keepdims=True) for hot in hots],
        axis=1).astype(jnp.int32)
    total = cnt_scr[...] + jnp.sum(hot_any, axis=0, keepdims=True)
    cnt_scr[...] = total
    cnt_out[...] = total.astype(jnp.int32)


def _merge(y, g, bonus, sgb, apart, x2d, tm, lnx_g, lnx_b, bd, wbo, wo, g2n, rw, rb):
    n, d = x2d.shape
    d_b = y.shape[1]
    n_e = rw.shape[1]
    tok = lambda w: pl.BlockSpec((tm, w), lambda i: (i, 0))
    weights = (lnx_g, lnx_b, bd, wbo, wo, g2n, rw, rb)
    return pl.pallas_call(
        _merge_kernel,
        grid=(n // tm,),
        in_specs=[tok(d_b), tok(d_b), tok(d_b), tok(d), tok(d), tok(d)]
        + [_full_spec(w.shape) for w in weights],
        out_specs=[tok(d), tok(d // 2), tok(TOP_K), tok(TOP_K), tok(TOP_K),
                   pl.BlockSpec((1, n_e), lambda i: (0, 0))],
        out_shape=[jax.ShapeDtypeStruct((n, d), F32),
                   jax.ShapeDtypeStruct((n, d // 2), jnp.uint32),
                   jax.ShapeDtypeStruct((n, TOP_K), jnp.int32),
                   jax.ShapeDtypeStruct((n, TOP_K), F32),
                   jax.ShapeDtypeStruct((n, TOP_K), jnp.int32),
                   jax.ShapeDtypeStruct((1, n_e), jnp.int32)],
        scratch_shapes=[pltpu.VMEM((1, n_e), F32)],
        compiler_params=pltpu.CompilerParams(dimension_semantics=("arbitrary",),
                                             vmem_limit_bytes=VMEM_LIMIT),
        name="merge_router",
    )(y, g, bonus, sgb, apart, x2d, *weights)


MOE_TM = 256
MOE_FF_CHUNK = 512
CAST_ROWS = 256


def _route(topi_p, rank_p, cnt_p, topi_s, rank_s, cnt_s, n_tiles):
    n_e = cnt_p.shape[-1]
    cnt_p, cnt_s = cnt_p.reshape(n_e), cnt_s.reshape(n_e)
    cnt = cnt_p + cnt_s
    padded = (cnt + MOE_TM - 1) // MOE_TM * MOE_TM
    ends = jnp.cumsum(padded)
    base = ends - padded
    lookup = lambda table, idx: jnp.sum(
        jnp.where(idx[..., None] == jnp.arange(n_e), table, 0), axis=-1)
    pos_p = lookup(base, topi_p) + rank_p
    pos_s = lookup(base + cnt_p, topi_s) + rank_s
    start = jnp.arange(n_tiles, dtype=jnp.int32) * MOE_TM
    n_used = (ends[-1] // MOE_TM).astype(jnp.int32)
    tile_e = jnp.minimum(jnp.sum(start[:, None] >= ends[None, :], axis=1), n_e - 1)
    tile_rows = jnp.clip(lookup(base + cnt, tile_e) - start, 0, MOE_TM)
    tile_rows = jnp.where(start < ends[-1], tile_rows, 0)
    return (pos_p.astype(jnp.int32), pos_s.astype(jnp.int32), tile_e.astype(jnp.int32),
            tile_rows.astype(jnp.int32), n_used.reshape(1))


def _moe_kernel(te_ref, rows_ref, nused_ref, xs_ref, wgu_ref, bgu_ref, wd_ref, bdn_ref, ys_ref,
                wgu_b, wd_b):
    i = pl.program_id(0)
    e = te_ref[i]
    rows = rows_ref[i]
    first_of_expert = jnp.logical_or(i == 0, e != te_ref[jnp.maximum(i - 1, 0)])
    d, gu = wgu_ref.shape
    d_ff = wd_ref.shape[0]

    @pl.when(jnp.logical_and(first_of_expert, rows > 0))
    def _():
        for r0 in range(0, d, CAST_ROWS):
            wgu_b[r0:r0 + CAST_ROWS, :] = wgu_ref[r0:r0 + CAST_ROWS, :].astype(BF16)
        for r0 in range(0, d_ff, CAST_ROWS):
            wd_b[r0:r0 + CAST_ROWS, :] = wd_ref[r0:r0 + CAST_ROWS, :].astype(BF16)

    @pl.when(rows > 0)
    def _():
        lo, hi = _unpack_bf16_pairs(xs_ref[...])
        valid = lax.broadcasted_iota(jnp.int32, lo.shape, 0) < rows
        x_lo = jnp.where(valid, lo, 0.0).astype(BF16)
        x_hi = jnp.where(valid, hi, 0.0).astype(BF16)
        half = d // 2

        def proj(c0):
            cols = slice(c0, c0 + MOE_FF_CHUNK)
            return (jnp.dot(x_lo, wgu_b[:half, cols], preferred_element_type=F32)
                    + jnp.dot(x_hi, wgu_b[half:, cols], preferred_element_type=F32)
                    + bgu_ref[:, cols])

        y = jnp.zeros((x_lo.shape[0], d), F32)
        for f in range(d_ff // MOE_FF_CHUNK):
            c0 = f * MOE_FF_CHUNK
            gate = jnp.minimum(proj(c0), SWIGLU_LIMIT)
            up = jnp.clip(proj(d_ff + c0), -SWIGLU_LIMIT, SWIGLU_LIMIT)
            hh = (up + 1.0) * gate * _sigmoid(gate * SWIGLU_ALPHA)
            y = y + jnp.dot(hh.astype(BF16), wd_b[c0:c0 + MOE_FF_CHUNK, :],
                            preferred_element_type=F32)
        ys_ref[...] = _pack_bf16_pairs(y + bdn_ref[...])


def _moe(xs, tile_e, tile_rows, n_used, wgu, bgu, wd, bdn):
    p_rows, half = xs.shape
    n_e, d, gu = wgu.shape
    d_ff = wd.shape[1]
    n_tiles = p_rows // MOE_TM
    row_blk = lambda i, te, rows, nu: (jnp.minimum(i, nu[0] - 1), 0)
    w_blk = lambda i, te, rows, nu: (te[i], 0, 0)
    grid_spec = pltpu.PrefetchScalarGridSpec(
        num_scalar_prefetch=3,
        grid=(n_tiles,),
        in_specs=[pl.BlockSpec((MOE_TM, half), row_blk),
                  pl.BlockSpec((None, d, gu), w_blk),
                  pl.BlockSpec((None, 1, gu), w_blk),
                  pl.BlockSpec((None, d_ff, d), w_blk),
                  pl.BlockSpec((None, 1, d), w_blk)],
        out_specs=pl.BlockSpec((MOE_TM, half), row_blk),
        scratch_shapes=[pltpu.VMEM((d, gu), BF16), pltpu.VMEM((d_ff, d), BF16)])
    return pl.pallas_call(
        _moe_kernel,
        grid_spec=grid_spec,
        out_shape=jax.ShapeDtypeStruct((p_rows, half), jnp.uint32),
        compiler_params=pltpu.CompilerParams(dimension_semantics=("arbitrary",),
                                             vmem_limit_bytes=VMEM_LIMIT),
        name="moe_experts",
    )(tile_e, tile_rows, n_used, xs, wgu, bgu, wd, bdn)


def _combine_kernel(h_ref, yg_ref, prob_ref, gf_ref, out_ref):
    half = h_ref.shape[1] // 2
    prob = prob_ref[...]
    acc_lo = jnp.zeros((h_ref.shape[0], half), F32)
    acc_hi = jnp.zeros((h_ref.shape[0], half), F32)
    for k in range(TOP_K):
        lo, hi = _unpack_bf16_pairs(yg_ref[k])
        pk = prob[:, k:k + 1]
        acc_lo = acc_lo + pk * lo
        acc_hi = acc_hi + pk * hi
    z = h_ref[...] + jnp.concatenate([acc_lo, acc_hi], axis=1)
    out_ref[...] = _rms(z, gf_ref[...])


def _combine(h, yg, row0, prob, gf, tm):
    n, d = h.shape
    tok = lambda w: pl.BlockSpec((tm, w), lambda i: (i, 0))
    t0 = row0 // tm
    return pl.pallas_call(
        _combine_kernel,
        grid=(n // tm,),
        in_specs=[tok(d), pl.BlockSpec((TOP_K, tm, d // 2), lambda i: (0, i + t0, 0)),
                  tok(TOP_K), _full_spec(gf.shape)],
        out_specs=tok(d),
        out_shape=jax.ShapeDtypeStruct((n, d), F32),
        compiler_params=pltpu.CompilerParams(dimension_semantics=("parallel",),
                                             vmem_limit_bytes=VMEM_LIMIT),
        name="moe_combine",
    )(h, yg, prob, gf)


SC_CORES = 2
SC_SUBCORES = 16
SC_WORKERS = SC_CORES * SC_SUBCORES
SC_MAX_INDEX = 128


def _sc_chunk(rows_per_worker):
    for c in range(SC_MAX_INDEX, 7, -8):
        if rows_per_worker % c == 0:
            return c
    raise ValueError(f"no 8-aligned chunk divides {rows_per_worker} rows")


def _sc_mesh():
    return plsc.VectorSubcoreMesh(core_axis_name="c", subcore_axis_name="s")


def _sc_worker():
    return lax.axis_index("s") * SC_CORES + lax.axis_index("c")


def _scatter_rows(x, pos_t, p_rows):
    n, w = x.shape
    n_k = pos_t.shape[0]
    per = n // SC_WORKERS
    chunk = _sc_chunk(per)
    pos_flat = pos_t.reshape(n_k * n)

    def body(x_hbm, pos_hbm, xs_hbm, idx_v, rows_v, sem):
        base = _sc_worker() * per

        @pl.loop(0, per // chunk)
        def _(c):
            off = pl.multiple_of(base + c * chunk, 8)
            pltpu.sync_copy(x_hbm.at[pl.ds(off, chunk)], rows_v)
            for k in range(n_k):
                pltpu.sync_copy(pos_hbm.at[pl.ds(pl.multiple_of(k * n + off, 8), chunk)], idx_v)
                pltpu.async_copy(rows_v, xs_hbm.at[idx_v], sem).wait()

    return pl.kernel(
        body, out_type=jax.ShapeDtypeStruct((p_rows, w), x.dtype), mesh=_sc_mesh(),
        scratch_types=[pltpu.VMEM((chunk,), jnp.int32), pltpu.VMEM((chunk, w), x.dtype),
                       pltpu.SemaphoreType.DMA],
        name="sc_scatter_rows")(x, pos_flat)


def _gather_rows(table, idx):
    n = idx.shape[0]
    w = table.shape[1]
    per = n // SC_WORKERS
    chunk = _sc_chunk(per)

    def body(table_hbm, idx_hbm, out_hbm, idx_v, rows_v, sem):
        base = _sc_worker() * per

        @pl.loop(0, per // chunk)
        def _(c):
            off = pl.multiple_of(base + c * chunk, 8)
            pltpu.sync_copy(idx_hbm.at[pl.ds(off, chunk)], idx_v)
            pltpu.async_copy(table_hbm.at[idx_v], rows_v, sem).wait()
            pltpu.sync_copy(rows_v, out_hbm.at[pl.ds(off, chunk)])

    return pl.kernel(
        body, out_type=jax.ShapeDtypeStruct((n, w), table.dtype), mesh=_sc_mesh(),
        scratch_types=[pltpu.VMEM((chunk,), jnp.int32), pltpu.VMEM((chunk, w), table.dtype),
                       pltpu.SemaphoreType.DMA],
        name="sc_gather_rows")(table, idx)


def _mix_matrix(ws, tm, seq_rows):
    causal = jnp.tril(jnp.ones((CHUNK, CHUNK), dtype=bool))
    w = jnp.where(causal[None], ws, 0.0)[:, :seq_rows, :seq_rows]
    eye = jnp.eye(tm // seq_rows, dtype=ws.dtype)
    return jnp.einsum("ab,gts->gatbs", eye, w).reshape(ws.shape[0], tm, tm).astype(BF16)


def _stream(x, shift_in, wkv_in, p, tm, seq_rows, scan_chunk, scan_seqs):
    nb, t, d = x.shape
    n = nb * t
    x2d = x.reshape(n, d)
    rows = CHUNK if seq_rows is None else seq_rows
    wmix = _mix_matrix(p["w_spatial"], tm, rows)
    pos = jnp.arange(tm) % rows
    d_a = p["wu"].shape[1]
    bias_full = jnp.repeat(p["b_spatial"].T[pos], d_a // G_A, axis=1)
    apart, vn = _branch_a(x2d, tm, p["norm1_g"], p["wu"], p["wv"], p["wga"], p["vnorm_g"],
                          p["vnorm_b"], wmix, bias_full, p["w_a_out"])
    if seq_rows is None:
        xb, ext = x, jnp.zeros((1, p["wcur"].shape[1]), F32)
    else:
        xb = x2d.reshape(n // tm, tm, d)
        ext = jnp.repeat(shift_in, seq_rows, axis=0).reshape(n // tm, tm, -1)
    outs = _branch_b(xb, tm, seq_rows, ext, p["norm1_g"], p["wcur"], p["wgb"], p["mu_shift"],
                     p["w0"], p["w2"], p["a0"], p["a2"], p["g2"], p["k_k"], p["k_a"], p["r_k"],
                     p["bd"])
    r, k2, v, kk, kka, lw, g, bonus, sgb, cur = outs
    d_b = r.shape[-1]
    if seq_rows is None:
        shift_out = cur[:, -1, :]
        scan_in = [a.reshape(nb, t, d_b) for a in (r, k2, v, kk, kka, lw)]
    else:
        shift_out = cur.reshape(nb, t, -1)[:, -1, :]
        pad = scan_chunk - t
        scan_in = [jnp.pad(a.reshape(nb, t, d_b), ((0, 0), (0, pad), (0, 0)))
                   for a in (r, k2, v, kk, kka, lw)]
    y, s_out = _scan(*scan_in, _state_to_blockdiag(wkv_in), scan_chunk, scan_seqs)
    y2d = y[:, :t].reshape(n, d_b)
    flat = lambda a: a.reshape(n, a.shape[-1])
    routed = _merge(y2d, flat(g), flat(bonus), flat(sgb), apart, x2d, tm, p["lnx_g"],
                    p["lnx_b"], p["bd"], p["w_b_out"], p["w_out"], p["norm2_g"],
                    p["router_w"], p["router_b"])
    return routed, vn, shift_out, _blockdiag_to_state(s_out)


def kernel(x_prompt, x_sample, state_shift, state_wkv, norm1_g, w_in, mu_shift, vnorm_g, vnorm_b, w_spatial, b_spatial, w_a_out, w0, w2, a0, a2, g2, k_k, k_a, r_k, lnx_g, lnx_b, w_b_out, w_out, norm2_g, router_w, router_b, exp_w_gu, exp_b_gu, exp_w_down, exp_b_down, normf_g):
    depth = w_in.shape[0]
    assert depth == 1, "the final norm is fused into the single layer's MoE call"
    d_model = x_prompt.shape[-1]
    d_b = w0.shape[-1]
    shift_w = mu_shift.shape[-1]
    d_a = vnorm_g.shape[-1]
    bp, tp, _ = x_prompt.shape
    bs, ts, _ = x_sample.shape
    head_id = jnp.arange(d_b) // HEAD_B
    bd = (head_id[:, None] == head_id[None, :]).astype(BF16)
    row = lambda a: a.reshape(1, -1)

    hp, hs = x_prompt, x_sample
    shift_p = jnp.zeros((depth, bp, shift_w), state_shift.dtype)
    wkv_p = jnp.zeros((depth, bp) + state_wkv.shape[2:], state_wkv.dtype)
    vrows, shifts_p, wkvs_p, shifts_s, wkvs_s = [], [], [], [], []
    for l in range(depth):
        wi = w_in[l].astype(BF16)
        o = shift_w
        p = dict(
            norm1_g=row(norm1_g[l]), wcur=wi[:, :o], wu=wi[:, o:o + d_a],
            wv=wi[:, o + d_a:o + 2 * d_a], wga=wi[:, o + 2 * d_a:o + 2 * d_a + d_model],
            wgb=wi[:, o + 2 * d_a + d_model:], mu_shift=row(mu_shift[l]),
            vnorm_g=row(vnorm_g[l]), vnorm_b=row(vnorm_b[l]), w_spatial=w_spatial[l],
            b_spatial=b_spatial[l], w_a_out=w_a_out[l].astype(BF16), w0=row(w0[l]), w2=w2[l],
            a0=row(a0[l]), a2=a2[l], g2=g2[l], k_k=row(k_k[l]), k_a=row(k_a[l]),
            r_k=row(r_k[l]), lnx_g=row(lnx_g[l]), lnx_b=row(lnx_b[l]),
            w_b_out=w_b_out[l].astype(BF16), w_out=w_out[l].astype(BF16),
            norm2_g=row(norm2_g[l]), router_w=router_w[l], router_b=row(router_b[l]), bd=bd)
        routed_p, _, sh_p, s_p = _stream(hp, shift_p[l], wkv_p[l], p, 512, None, 64, 4)
        routed_s, vn_s, sh_s, s_s = _stream(hs, state_shift[l], state_wkv[l], p, 512, ts, 8, 8)
        h_p, xp_p, topi_p, prob_p, rank_p, cnt_p = routed_p
        h_s, xp_s, topi_s, prob_s, rank_s, cnt_s = routed_s
        n_p, n_s = h_p.shape[0], h_s.shape[0]
        n_e = router_w.shape[-1]
        n_tiles = (n_p + n_s) * TOP_K // MOE_TM + n_e
        pos_p, pos_s, tile_e, tile_rows, n_used = _route(topi_p, rank_p, cnt_p, topi_s, rank_s,
                                                         cnt_s, n_tiles)
        pos_t = jnp.concatenate([pos_p, pos_s], axis=0).T
        xs = _scatter_rows(jnp.concatenate([xp_p, xp_s], axis=0), pos_t, n_tiles * MOE_TM)
        ys = _moe(xs, tile_e, tile_rows, n_used, exp_w_gu[l], exp_b_gu[l][:, None, :],
                  exp_w_down[l], exp_b_down[l][:, None, :])
        yg = _gather_rows(ys, pos_t.reshape(-1)).reshape(TOP_K, n_p + n_s, d_model // 2)
        gf = row(normf_g)
        hp = _combine(h_p, yg, 0, prob_p, gf, 512).reshape(bp, tp, d_model)
        hs = _combine(h_s, yg, n_p, prob_s, gf, 512).reshape(bs, ts, d_model)
        vrows.append(vn_s.reshape(bs, ts, d_a))
        shifts_p.append(sh_p)
        wkvs_p.append(s_p)
        shifts_s.append(sh_s)
        wkvs_s.append(s_s)
    return (hp, hs, jnp.stack(shifts_p), jnp.stack(wkvs_p), jnp.stack(shifts_s),
            jnp.stack(wkvs_s), jnp.stack(vrows))
```

```python
import functools

import jax
import jax.numpy as jnp
from jax import lax
from jax.experimental import pallas as pl
from jax.experimental.pallas import tpu as pltpu
from jax.experimental.pallas import tpu_sc as plsc

F32 = jnp.float32
BF16 = jnp.bfloat16

CHUNK = 128
G_A = 8
HEAD_B = 64
R_W, R_A, R_G = 64, 64, 128
TOP_K = 4
SWIGLU_LIMIT = 7.0
SWIGLU_ALPHA = 1.702
EPS = 1e-5
GN_EPS = HEAD_B * 1e-5

LANES = 128
PAIR = 2 * HEAD_B
VMEM_LIMIT = 56 * 1024 * 1024


def _dot(a, b):
    return jnp.dot(a.astype(BF16), b.astype(BF16), preferred_element_type=F32)


def _split(x):
    hi = x.astype(BF16)
    lo = (x - hi.astype(F32)).astype(BF16)
    return hi, lo


def _dot3(a, b):
    ah, al = _split(a)
    bh, bl = _split(b)
    return (jnp.dot(ah, bh, preferred_element_type=F32)
            + jnp.dot(ah, bl, preferred_element_type=F32)
            + jnp.dot(al, bh, preferred_element_type=F32))


def _dot_exact_rhs(a, b_bf16):
    ah, al = _split(a)
    return (jnp.dot(ah, b_bf16, preferred_element_type=F32)
            + jnp.dot(al, b_bf16, preferred_element_type=F32))


def _dot_exact_lhs(a_bf16, b):
    bh, bl = _split(b)
    bm = b - bh.astype(F32) - bl.astype(F32)
    return (jnp.dot(a_bf16, bh, preferred_element_type=F32)
            + jnp.dot(a_bf16, bl, preferred_element_type=F32)
            + jnp.dot(a_bf16, bm.astype(BF16), preferred_element_type=F32))


def _rms(x, g):
    return x * lax.rsqrt(jnp.mean(x * x, axis=-1, keepdims=True) + EPS) * g


def _sigmoid(x):
    return 1.0 / (1.0 + jnp.exp(-x))


def _softplus(z):
    return jnp.maximum(z, 0.0) + jnp.log(1.0 + jnp.exp(-jnp.abs(z)))


def _full_spec(shape):
    nd = len(shape)
    return pl.BlockSpec(shape, lambda *_: (0,) * nd, pipeline_mode=pl.Buffered(1))


def _branch_a_kernel(x_ref, g1_ref, wu_ref, wv_ref, wga_ref, vg_ref, vb_ref, wmix_ref,
                     bias_ref, wao_ref, apart_ref, vn_ref):
    xn = _rms(x_ref[...], g1_ref[...]).astype(BF16)
    v = jnp.dot(xn, wv_ref[...], preferred_element_type=F32)
    mu = jnp.mean(v, axis=-1, keepdims=True)
    d = v - mu
    var = jnp.mean(d * d, axis=-1, keepdims=True)
    vn = d * lax.rsqrt(var + EPS) * vg_ref[...] + vb_ref[...]
    vn_ref[...] = vn
    vnb = vn.astype(BF16)
    c_a = vnb.shape[1] // G_A
    mixed = jnp.concatenate(
        [jnp.dot(wmix_ref[g], vnb[:, g * c_a:(g + 1) * c_a], preferred_element_type=F32)
         for g in range(G_A)], axis=1) + bias_ref[...]
    u = jnp.dot(xn, wu_ref[...], preferred_element_type=F32)
    ya = (u * mixed).astype(BF16)
    ga = jnp.dot(xn, wga_ref[...], preferred_element_type=F32)
    apart_ref[...] = _sigmoid(ga) * jnp.dot(ya, wao_ref[...], preferred_element_type=F32)


def _branch_a(x2d, tm, g1, wu, wv, wga, vg, vb, wmix, bias_full, wao):
    n, d = x2d.shape
    d_a = wu.shape[1]
    tok = lambda w: pl.BlockSpec((tm, w), lambda i: (i, 0))
    return pl.pallas_call(
        _branch_a_kernel,
        grid=(n // tm,),
        in_specs=[tok(d), _full_spec(g1.shape), _full_spec(wu.shape), _full_spec(wv.shape),
                  _full_spec(wga.shape), _full_spec(vg.shape), _full_spec(vb.shape),
                  _full_spec(wmix.shape), _full_spec(bias_full.shape), _full_spec(wao.shape)],
        out_specs=[tok(d), tok(d_a)],
        out_shape=[jax.ShapeDtypeStruct((n, d), F32), jax.ShapeDtypeStruct((n, d_a), F32)],
        compiler_params=pltpu.CompilerParams(dimension_semantics=("parallel",),
                                             vmem_limit_bytes=VMEM_LIMIT),
        name="branch_a",
    )(x2d, g1, wu, wv, wga, vg, vb, wmix, bias_full, wao)


def _branch_b_kernel(x_ref, g1_ref, wcur_ref, wgb_ref, mu_ref, ext_ref, w0_ref, w2_ref,
                     a0_ref, a2_ref, g2_ref, kk_ref, ka_ref, rk_ref, bd_ref,
                     r_out, k_out, v_out, kk_out, kka_out, lw_out, g_out, bonus_out,
                     sgb_out, cur_out, carry_scr, *, seq_rows, d_b):
    tm = x_ref.shape[0]
    xn = _rms(x_ref[...], g1_ref[...]).astype(BF16)
    cur = jnp.dot(xn, wcur_ref[...], preferred_element_type=F32)
    rolled = pltpu.roll(cur, 1, axis=0)
    row = lax.broadcasted_iota(jnp.int32, (tm, 1), 0)
    if seq_rows is None:
        first_tile = pl.program_id(1) == 0
        carry = jnp.where(first_tile, 0.0, carry_scr[...])
        prev = jnp.where(row == 0, carry, rolled)
        carry_scr[...] = cur[tm - 1:tm, :]
        cur_out[...] = cur[tm - 8:tm, :]
    else:
        prev = jnp.where(row % seq_rows == 0, ext_ref[...], rolled)
        cur_out[...] = cur
    xs = cur + (prev - cur) * mu_ref[...]
    r = xs[:, 0:d_b]
    k = xs[:, d_b:2 * d_b]
    v = xs[:, 2 * d_b:3 * d_b]
    o = 3 * d_b
    xw = xs[:, o:o + R_W]
    xa = xs[:, o + R_W:o + R_W + R_A]
    xg = xs[:, o + R_W + R_A:o + R_W + R_A + R_G]
    wl = w0_ref[...] + _dot3(jnp.tanh(xw), w2_ref[...])
    lw = -jnp.exp(-_softplus(-wl) - 0.5)
    a = _sigmoid(a0_ref[...] + _dot3(xa, a2_ref[...]))
    g = _dot3(_sigmoid(xg), g2_ref[...])
    bd = bd_ref[...]
    kkr = k * kk_ref[...]
    n2 = _dot_exact_rhs(kkr * kkr, bd)
    kk = kkr / jnp.maximum(jnp.sqrt(n2), 1e-12)
    k2 = k * (1.0 + (a - 1.0) * ka_ref[...])
    bonus = _dot_exact_rhs(r * k2 * rk_ref[...], bd) * v
    r_out[...] = r
    k_out[...] = k2
    v_out[...] = v
    kk_out[...] = kk
    kka_out[...] = kk * a
    lw_out[...] = lw
    g_out[...] = g
    bonus_out[...] = bonus
    sgb_out[...] = _sigmoid(jnp.dot(xn, wgb_ref[...], preferred_element_type=F32))


def _branch_b(x3d, tm, seq_rows, ext, g1, wcur, wgb, mu, w0, w2, a0, a2, g2, k_k, k_a, r_k, bd):
    nb, t, d = x3d.shape
    d_b = w0.shape[1]
    shift_w = wcur.shape[1]
    nt = t // tm
    tok = lambda w: pl.BlockSpec((None, tm, w), lambda b, i: (b, i, 0))
    cur_rows = 8 if seq_rows is None else tm
    outs = [jax.ShapeDtypeStruct((nb, t, d_b), F32)] * 8 + [
        jax.ShapeDtypeStruct((nb, t, d), F32),
        jax.ShapeDtypeStruct((nb, nt * cur_rows, shift_w), F32)]
    out_specs = [tok(d_b)] * 8 + [tok(d), pl.BlockSpec((None, cur_rows, shift_w),
                                                       lambda b, i: (b, i, 0))]
    weights = (g1, wcur, wgb, mu)
    small = (w0, w2, a0, a2, g2, k_k, k_a, r_k, bd)
    return pl.pallas_call(
        functools.partial(_branch_b_kernel, seq_rows=seq_rows, d_b=d_b),
        grid=(nb, nt),
        in_specs=[tok(d)] + [_full_spec(w.shape) for w in weights]
        + [tok(shift_w) if seq_rows is not None else _full_spec(ext.shape)]
        + [_full_spec(w.shape) for w in small],
        out_specs=out_specs,
        out_shape=outs,
        scratch_shapes=[pltpu.VMEM((1, shift_w), F32)],
        compiler_params=pltpu.CompilerParams(dimension_semantics=("parallel", "arbitrary"),
                                             vmem_limit_bytes=VMEM_LIMIT),
        name="branch_b",
    )(x3d, *weights, ext, *small)


def _scan_kernel(r_ref, k_ref, v_ref, kk_ref, kka_ref, lw_ref, s0_ref, y_ref, sout_ref, s_scr,
                 *, chunk):
    L = chunk
    W2 = 2 * L
    c = pl.program_id(1)

    @pl.when(c == 0)
    def _():
        s_scr[...] = s0_ref[...]

    n_seq, n_pairs = s_scr.shape[:2]
    iota = lambda shape, dim: lax.broadcasted_iota(jnp.int32, shape, dim)
    stack_mask = (iota((W2, 1), 0) >= L) == (iota((1, PAIR), 1) >= HEAD_B)
    bd_mask = (iota((W2, 1), 0) >= L) == (iota((1, W2), 1) >= L)
    assert L & (L - 1) == 0, "chunk length must be a power of two"
    rw = iota((L, W2), 0)
    cw = iota((L, W2), 1) & (L - 1)
    strict_w = cw < rw
    eye_w = (cw == rw).astype(F32)
    incl_w2 = (iota((L, 2 * W2), 1) & (L - 1)) <= iota((L, 2 * W2), 0)
    tri = (iota((L, L), 1) <= iota((L, L), 0)).astype(BF16)
    di = iota((PAIR, PAIR), 0)
    dj = iota((PAIR, PAIR), 1)
    diag = di == dj
    head_diag = (di >= HEAD_B) == (dj >= HEAD_B)

    def stack(x):
        return jnp.where(stack_mask, jnp.concatenate([x, x], axis=0), 0.0)

    def bd(xw):
        return jnp.where(bd_mask, jnp.concatenate([xw, xw], axis=0), 0.0)

    n_double = max(L.bit_length() - 2, 0)
    prep = []
    for s in range(n_seq):
        lw_all = lw_ref[s]
        cum_all = _dot_exact_lhs(tri, lw_all)
        for p in range(n_pairs):
            sl = slice(p * PAIR, (p + 1) * PAIR)
            cum = cum_all[:, sl]
            cum_last = cum[L - 1:L, :]
            e_neg = jnp.exp(-cum)
            e_rel = jnp.exp(cum_last - cum)
            kk = kk_ref[s, :, sl]
            kka = kka_ref[s, :, sl]
            kx = k_ref[s, :, sl]
            prep.append(dict(
                s=s, p=p, sl=sl, ab=-kk * jnp.exp(cum - lw_all[:, sl]),
                rb=r_ref[s, :, sl] * jnp.exp(cum), bb=kka * e_neg, kb=kx * e_neg,
                bt=kka * e_rel, kt=kx * e_rel, v=v_ref[s, :, sl], decay=jnp.exp(cum_last)))
    a_w = [lax.dot_general(
        jnp.concatenate([q["ab"], q["rb"]], axis=0).astype(BF16),
        jnp.concatenate([stack(q["bb"]), stack(q["kb"])], axis=0).astype(BF16),
        (((1,), (1,)), ((), ())), preferred_element_type=F32) for q in prep]
    nw = [jnp.where(strict_w, a[:L, :W2], 0.0) for a in a_w]
    tw = [eye_w + n for n in nw]
    nbd = [bd(n) for n in nw]
    for _ in range(n_double):
        nw = [_dot(n, b) for n, b in zip(nw, nbd)]
        nbd = [bd(n) for n in nw]
        tw = [t + _dot(t, b) for t, b in zip(tw, nbd)]
    akv = [_dot(jnp.where(strict_w, a[:L, W2:], 0.0), stack(q["v"])) for a, q in zip(a_w, prep)]
    tx = [_dot(t, jnp.concatenate([stack(q["ab"]), stack(kv)], axis=1))
          for t, q, kv in zip(tw, prep, akv)]
    ry = [_dot(jnp.where(incl_w2, a[L:], 0.0),
               jnp.concatenate(
                   [jnp.concatenate([stack(x[:, :PAIR]), stack(x[:, PAIR:])], axis=1),
                    jnp.concatenate([jnp.zeros((W2, PAIR), F32), stack(q["v"])], axis=1)], axis=0))
          for a, x, q in zip(a_w, tx, prep)]
    for q, x, y in zip(prep, tx, ry):
        st = s_scr[q["s"], q["p"]]
        us = _dot(jnp.concatenate([x[:, :PAIR], q["rb"] + y[:, :PAIR]], axis=0), st)
        y_ref[q["s"], :, q["sl"]] = us[L:] + y[:, PAIR:]
        u = us[:L] + x[:, PAIR:]
        lhs = jnp.concatenate([jnp.where(diag, q["decay"], 0.0),
                               jnp.concatenate([q["bt"], q["kt"]], axis=0).T], axis=1)
        g = _dot(lhs, jnp.concatenate([st, u, q["v"]], axis=0))
        s_scr[q["s"], q["p"]] = jnp.where(head_diag, g, 0.0)

    @pl.when(c == pl.num_programs(1) - 1)
    def _():
        sout_ref[...] = s_scr[...]


def _scan(r, k, v, kk, kka, lw, s0, chunk, n_seq):
    nb, t, d_b = r.shape
    n_pairs = d_b // PAIR
    tok = pl.BlockSpec((n_seq, chunk, d_b), lambda b, c: (b, c, 0))
    st = pl.BlockSpec((n_seq, n_pairs, PAIR, PAIR), lambda b, c: (b, 0, 0, 0))
    return pl.pallas_call(
        functools.partial(_scan_kernel, chunk=chunk),
        grid=(nb // n_seq, t // chunk),
        in_specs=[tok] * 6 + [st],
        out_specs=[tok, st],
        out_shape=[jax.ShapeDtypeStruct((nb, t, d_b), F32),
                   jax.ShapeDtypeStruct((nb, n_pairs, PAIR, PAIR), F32)],
        scratch_shapes=[pltpu.VMEM((n_seq, n_pairs, PAIR, PAIR), F32)],
        compiler_params=pltpu.CompilerParams(dimension_semantics=("parallel", "arbitrary"),
                                             vmem_limit_bytes=VMEM_LIMIT),
        name="rwkv_scan",
    )(r, k, v, kk, kka, lw, s0)


def _state_to_blockdiag(wkv):
    b, h, n, _ = wkv.shape
    st = jnp.swapaxes(wkv, -1, -2).reshape(b, h // 2, 2, n, n)
    z = jnp.zeros_like(st[:, :, 0])
    top = jnp.concatenate([st[:, :, 0], z], axis=-1)
    bot = jnp.concatenate([z, st[:, :, 1]], axis=-1)
    return jnp.concatenate([top, bot], axis=-2)


def _blockdiag_to_state(s):
    b, hp, _, _ = s.shape
    n = HEAD_B
    blocks = jnp.stack([s[:, :, :n, :n], s[:, :, n:, n:]], axis=2)
    return jnp.swapaxes(blocks, -1, -2).reshape(b, hp * 2, n, n)


def _pack_bf16_pairs(x):
    w = x.shape[1] // 2
    bits = lambda v: lax.bitcast_convert_type(v.astype(BF16).astype(F32), jnp.uint32)
    return (bits(x[:, :w]) >> 16) | (bits(x[:, w:]) & jnp.uint32(0xFFFF0000))


def _unpack_bf16_pairs(u):
    lo = lax.bitcast_convert_type(u << 16, F32)
    hi = lax.bitcast_convert_type(u & jnp.uint32(0xFFFF0000), F32)
    return lo, hi


def _merge_kernel(y_ref, g_ref, bonus_ref, sgb_ref, apart_ref, x_ref, lg_ref, lb_ref, bd_ref,
                  wbo_ref, wo_ref, g2_ref, rw_ref, rb_ref,
                  h_out, xp_out, topi_out, prob_out, rank_out, cnt_out, cnt_scr):
    step = pl.program_id(0)

    @pl.when(step == 0)
    def _():
        cnt_scr[...] = jnp.zeros_like(cnt_scr)

    bd = bd_ref[...]
    y = y_ref[...]
    inv_n = 1.0 / HEAD_B
    mu = _dot_exact_rhs(y, bd) * inv_n
    d = y - mu
    var = _dot_exact_rhs(d * d, bd) * inv_n
    yn = d * lax.rsqrt(var + GN_EPS) * lg_ref[...] + lb_ref[...]
    yb = ((yn + bonus_ref[...]) * g_ref[...]).astype(BF16)
    merged = apart_ref[...] + sgb_ref[...] * jnp.dot(yb, wbo_ref[...], preferred_element_type=F32)
    h = x_ref[...] + jnp.dot(merged.astype(BF16), wo_ref[...], preferred_element_type=F32)
    h_out[...] = h
    xn2 = _rms(h, g2_ref[...])
    xp_out[...] = _pack_bf16_pairs(xn2)
    logits = _dot3(xn2, rw_ref[...]) + rb_ref[...]
    tm, n_e = logits.shape
    idx = lax.broadcasted_iota(jnp.int32, logits.shape, 1).astype(F32)
    work = logits
    tops, hots, sels = [], [], []
    for _ in range(TOP_K):
        m = jnp.max(work, axis=-1, keepdims=True)
        sel = jnp.min(jnp.where(work == m, idx, float(n_e)), axis=-1, keepdims=True)
        hot = idx == sel
        tops.append(m)
        hots.append(hot)
        sels.append(sel)
        work = jnp.where(hot, -jnp.inf, work)
    es = [jnp.exp(t - tops[0]) for t in tops]
    denom = es[0] + es[1] + es[2] + es[3]
    topi_out[...] = jnp.concatenate(sels, axis=1).astype(jnp.int32)
    prob_out[...] = jnp.concatenate([e / denom for e in es], axis=1)
    hot_any = jnp.zeros_like(logits)
    for hot in hots:
        hot_any = hot_any + hot.astype(F32)
    ri = lax.broadcasted_iota(jnp.int32, (tm, tm), 0)
    ci = lax.broadcasted_iota(jnp.int32, (tm, tm), 1)
    before = _dot((ci < ri).astype(BF16), hot_any) + cnt_scr[...]
    rank_out[...] = jnp.concatenate(
        [jnp.sum(jnp.where(hot, before, 0.0), axis=-1, keepdims=True) for hot in hots],
        axis=1).astype(jnp.int32)
    total = cnt_scr[...] + jnp.sum(hot_any, axis=0, keepdims=True)
    cnt_scr[...] = total
    cnt_out[...] = total.astype(jnp.int32)


def _merge(y, g, bonus, sgb, apart, x2d, tm, lnx_g, lnx_b, bd, wbo, wo, g2n, rw, rb):
    n, d = x2d.shape
    d_b = y.shape[1]
    n_e = rw.shape[1]
    tok = lambda w: pl.BlockSpec((tm, w), lambda i: (i, 0))
    weights = (lnx_g, lnx_b, bd, wbo, wo, g2n, rw, rb)
    return pl.pallas_call(
        _merge_kernel,
        grid=(n // tm,),
        in_specs=[tok(d_b), tok(d_b), tok(d_b), tok(d), tok(d), tok(d)]
        + [_full_spec(w.shape) for w in weights],
        out_specs=[tok(d), tok(d // 2), tok(TOP_K), tok(TOP_K), tok(TOP_K),
                   pl.BlockSpec((1, n_e), lambda i: (0, 0))],
        out_shape=[jax.ShapeDtypeStruct((n, d), F32),
                   jax.ShapeDtypeStruct((n, d // 2), jnp.uint32),
                   jax.ShapeDtypeStruct((n, TOP_K), jnp.int32),
                   jax.ShapeDtypeStruct((n, TOP_K), F32),
                   jax.ShapeDtypeStruct((n, TOP_K), jnp.int32),
                   jax.ShapeDtypeStruct((1, n_e), jnp.int32)],
        scratch_shapes=[pltpu.VMEM((1, n_e), F32)],
        compiler_params=pltpu.CompilerParams(dimension_semantics=("arbitrary",),
                                             vmem_limit_bytes=VMEM_LIMIT),
        name="merge_router",
    )(y, g, bonus, sgb, apart, x2d, *weights)


MOE_TM = 256
MOE_FF_CHUNK = 512
CAST_ROWS = 256


def _route(topi_p, rank_p, cnt_p, topi_s, rank_s, cnt_s):
    n_e = cnt_p.shape[-1]
    cnt_p, cnt_s = cnt_p.reshape(n_e), cnt_s.reshape(n_e)
    cnt = cnt_p + cnt_s
    padded = (cnt + MOE_TM - 1) // MOE_TM * MOE_TM
    ends = jnp.cumsum(padded)
    base = ends - padded
    lookup = lambda table, idx: jnp.sum(
        jnp.where(idx[..., None] == jnp.arange(n_e), table, 0), axis=-1)
    pos_p = lookup(base, topi_p) + rank_p
    pos_s = lookup(base + cnt_p, topi_s) + rank_s
    return (pos_p.astype(jnp.int32), pos_s.astype(jnp.int32),
            (base // MOE_TM).astype(jnp.int32), cnt.astype(jnp.int32))


def _moe_kernel(tile0_ref, cnt_ref, xs_hbm, wgu_ref, bgu_ref, wd_ref, bdn_ref, ys_hbm,
                wgu_b, wd_b, xbuf, ybuf, sem_in, sem_out):
    e = pl.program_id(0)
    tile0 = tile0_ref[e]
    cnt = cnt_ref[e]
    n_t = (cnt + MOE_TM - 1) // MOE_TM
    d, gu = wgu_ref.shape
    d_ff = wd_ref.shape[0]
    half = d // 2

    def in_copy(tile, slot):
        return pltpu.make_async_copy(xs_hbm.at[pl.ds(tile * MOE_TM, MOE_TM)], xbuf.at[slot],
                                     sem_in.at[slot])

    def out_copy(tile, slot):
        return pltpu.make_async_copy(ybuf.at[slot], ys_hbm.at[pl.ds(tile * MOE_TM, MOE_TM)],
                                     sem_out.at[slot])

    @pl.when(n_t > 0)
    def _():
        in_copy(tile0, 0).start()
        for r0 in range(0, d, CAST_ROWS):
            wgu_b[r0:r0 + CAST_ROWS, :] = wgu_ref[r0:r0 + CAST_ROWS, :].astype(BF16)
        for r0 in range(0, d_ff, CAST_ROWS):
            wd_b[r0:r0 + CAST_ROWS, :] = wd_ref[r0:r0 + CAST_ROWS, :].astype(BF16)

    def tile_step(j, carry):
        slot = j % 2
        in_copy(tile0 + j, slot).wait()

        @pl.when(j + 1 < n_t)
        def _():
            in_copy(tile0 + j + 1, 1 - slot).start()

        @pl.when(j >= 2)
        def _():
            out_copy(tile0 + j - 2, slot).wait()

        lo, hi = _unpack_bf16_pairs(xbuf[slot])
        valid = lax.broadcasted_iota(jnp.int32, lo.shape, 0) < cnt - j * MOE_TM
        x_lo = jnp.where(valid, lo, 0.0).astype(BF16)
        x_hi = jnp.where(valid, hi, 0.0).astype(BF16)

        def proj(c0):
            cols = slice(c0, c0 + MOE_FF_CHUNK)
            return (jnp.dot(x_lo, wgu_b[:half, cols], preferred_element_type=F32)
                    + jnp.dot(x_hi, wgu_b[half:, cols], preferred_element_type=F32)
                    + bgu_ref[:, cols])

        y = jnp.zeros((MOE_TM, d), F32)
        for f in range(d_ff // MOE_FF_CHUNK):
            c0 = f * MOE_FF_CHUNK
            gate = jnp.minimum(proj(c0), SWIGLU_LIMIT)
            up = jnp.clip(proj(d_ff + c0), -SWIGLU_LIMIT, SWIGLU_LIMIT)
            hh = (up + 1.0) * gate * _sigmoid(gate * SWIGLU_ALPHA)
            y = y + jnp.dot(hh.astype(BF16), wd_b[c0:c0 + MOE_FF_CHUNK, :],
                            preferred_element_type=F32)
        ybuf[slot] = _pack_bf16_pairs(y + bdn_ref[...])
        out_copy(tile0 + j, slot).start()
        return carry

    lax.fori_loop(0, n_t, tile_step, 0)

    @pl.when(n_t >= 2)
    def _():
        out_copy(tile0 + n_t - 2, n_t % 2).wait()

    @pl.when(n_t >= 1)
    def _():
        out_copy(tile0 + n_t - 1, (n_t - 1) % 2).wait()


def _moe(xs, tile0, cnt, wgu, bgu, wd, bdn):
    p_rows, half = xs.shape
    n_e, d, gu = wgu.shape
    d_ff = wd.shape[1]
    w_blk = lambda e, tile0, cnt: (e, 0, 0)
    grid_spec = pltpu.PrefetchScalarGridSpec(
        num_scalar_prefetch=2,
        grid=(n_e,),
        in_specs=[pl.BlockSpec(memory_space=pl.ANY),
                  pl.BlockSpec((None, d, gu), w_blk),
                  pl.BlockSpec((None, 1, gu), w_blk),
                  pl.BlockSpec((None, d_ff, d), w_blk),
                  pl.BlockSpec((None, 1, d), w_blk)],
        out_specs=pl.BlockSpec(memory_space=pl.ANY),
        scratch_shapes=[pltpu.VMEM((d, gu), BF16), pltpu.VMEM((d_ff, d), BF16),
                        pltpu.VMEM((2, MOE_TM, half), jnp.uint32),
                        pltpu.VMEM((2, MOE_TM, half), jnp.uint32),
                        pltpu.SemaphoreType.DMA((2,)), pltpu.SemaphoreType.DMA((2,))])
    return pl.pallas_call(
        _moe_kernel,
        grid_spec=grid_spec,
        out_shape=jax.ShapeDtypeStruct((p_rows, half), jnp.uint32),
        compiler_params=pltpu.CompilerParams(dimension_semantics=("arbitrary",),
                                             vmem_limit_bytes=VMEM_LIMIT),
        name="moe_experts",
    )(tile0, cnt, xs, wgu, bgu, wd, bdn)


def _combine_kernel(h_ref, yg_ref, prob_ref, gf_ref, out_ref):
    half = h_ref.shape[1] // 2
    prob = prob_ref[...]
    acc_lo = jnp.zeros((h_ref.shape[0], half), F32)
    acc_hi = jnp.zeros((h_ref.shape[0], half), F32)
    for k in range(TOP_K):
        lo, hi = _unpack_bf16_pairs(yg_ref[k])
        pk = prob[:, k:k + 1]
        acc_lo = acc_lo + pk * lo
        acc_hi = acc_hi + pk * hi
    z = h_ref[...] + jnp.concatenate([acc_lo, acc_hi], axis=1)
    out_ref[...] = _rms(z, gf_ref[...])


def _combine(h, yg, row0, prob, gf, tm):
    n, d = h.shape
    tok = lambda w: pl.BlockSpec((tm, w), lambda i: (i, 0))
    t0 = row0 // tm
    return pl.pallas_call(
        _combine_kernel,
        grid=(n // tm,),
        in_specs=[tok(d), pl.BlockSpec((TOP_K, tm, d // 2), lambda i: (0, i + t0, 0)),
                  tok(TOP_K), _full_spec(gf.shape)],
        out_specs=tok(d),
        out_shape=jax.ShapeDtypeStruct((n, d), F32),
        compiler_params=pltpu.CompilerParams(dimension_semantics=("parallel",),
                                             vmem_limit_bytes=VMEM_LIMIT),
        name="moe_combine",
    )(h, yg, prob, gf)


SC_CORES = 2
SC_SUBCORES = 16
SC_WORKERS = SC_CORES * SC_SUBCORES
SC_MAX_INDEX = 128


def _sc_chunk(rows_per_worker):
    for c in range(SC_MAX_INDEX, 7, -8):
        if rows_per_worker % c == 0:
            return c
    raise ValueError(f"no 8-aligned chunk divides {rows_per_worker} rows")


def _sc_mesh():
    return plsc.VectorSubcoreMesh(core_axis_name="c", subcore_axis_name="s")


def _sc_worker():
    return lax.axis_index("s") * SC_CORES + lax.axis_index("c")


def _scatter_rows(x, pos_t, p_rows):
    n, w = x.shape
    n_k = pos_t.shape[0]
    per = n // SC_WORKERS
    chunk = _sc_chunk(per)
    pos_flat = pos_t.reshape(n_k * n)

    def body(x_hbm, pos_hbm, xs_hbm, idx_v, rows_v, sem):
        base = _sc_worker() * per

        @pl.loop(0, per // chunk)
        def _(c):
            off = pl.multiple_of(base + c * chunk, 8)
            pltpu.sync_copy(x_hbm.at[pl.ds(off, chunk)], rows_v)
            for k in range(n_k):
                pltpu.sync_copy(pos_hbm.at[pl.ds(pl.multiple_of(k * n + off, 8), chunk)], idx_v)
                pltpu.async_copy(rows_v, xs_hbm.at[idx_v], sem).wait()

    return pl.kernel(
        body, out_type=jax.ShapeDtypeStruct((p_rows, w), x.dtype), mesh=_sc_mesh(),
        scratch_types=[pltpu.VMEM((chunk,), jnp.int32), pltpu.VMEM((chunk, w), x.dtype),
                       pltpu.SemaphoreType.DMA],
        name="sc_scatter_rows")(x, pos_flat)


def _gather_rows(table, idx):
    n = idx.shape[0]
    w = table.shape[1]
    per = n // SC_WORKERS
    chunk = _sc_chunk(per)

    def body(table_hbm, idx_hbm, out_hbm, idx_v, rows_v, sem):
        base = _sc_worker() * per

        @pl.loop(0, per // chunk)
        def _(c):
            off = pl.multiple_of(base + c * chunk, 8)
            pltpu.sync_copy(idx_hbm.at[pl.ds(off, chunk)], idx_v)
            pltpu.async_copy(table_hbm.at[idx_v], rows_v, sem).wait()
            pltpu.sync_copy(rows_v, out_hbm.at[pl.ds(off, chunk)])

    return pl.kernel(
        body, out_type=jax.ShapeDtypeStruct((n, w), table.dtype), mesh=_sc_mesh(),
        scratch_types=[pltpu.VMEM((chunk,), jnp.int32), pltpu.VMEM((chunk, w), table.dtype),
                       pltpu.SemaphoreType.DMA],
        name="sc_gather_rows")(table, idx)


def _mix_matrix(ws, tm, seq_rows):
    causal = jnp.tril(jnp.ones((CHUNK, CHUNK), dtype=bool))
    w = jnp.where(causal[None], ws, 0.0)[:, :seq_rows, :seq_rows]
    eye = jnp.eye(tm // seq_rows, dtype=ws.dtype)
    return jnp.einsum("ab,gts->gatbs", eye, w).reshape(ws.shape[0], tm, tm).astype(BF16)


def _stream(x, shift_in, wkv_in, p, tm, seq_rows, scan_chunk, scan_seqs):
    nb, t, d = x.shape
    n = nb * t
    x2d = x.reshape(n, d)
    rows = CHUNK if seq_rows is None else seq_rows
    wmix = _mix_matrix(p["w_spatial"], tm, rows)
    pos = jnp.arange(tm) % rows
    d_a = p["wu"].shape[1]
    bias_full = jnp.repeat(p["b_spatial"].T[pos], d_a // G_A, axis=1)
    apart, vn = _branch_a(x2d, tm, p["norm1_g"], p["wu"], p["wv"], p["wga"], p["vnorm_g"],
                          p["vnorm_b"], wmix, bias_full, p["w_a_out"])
    if seq_rows is None:
        xb, ext = x, jnp.zeros((1, p["wcur"].shape[1]), F32)
    else:
        xb = x2d.reshape(n // tm, tm, d)
        ext = jnp.repeat(shift_in, seq_rows, axis=0).reshape(n // tm, tm, -1)
    outs = _branch_b(xb, tm, seq_rows, ext, p["norm1_g"], p["wcur"], p["wgb"], p["mu_shift"],
                     p["w0"], p["w2"], p["a0"], p["a2"], p["g2"], p["k_k"], p["k_a"], p["r_k"],
                     p["bd"])
    r, k2, v, kk, kka, lw, g, bonus, sgb, cur = outs
    d_b = r.shape[-1]
    if seq_rows is None:
        shift_out = cur[:, -1, :]
        scan_in = [a.reshape(nb, t, d_b) for a in (r, k2, v, kk, kka, lw)]
    else:
        shift_out = cur.reshape(nb, t, -1)[:, -1, :]
        pad = scan_chunk - t
        scan_in = [jnp.pad(a.reshape(nb, t, d_b), ((0, 0), (0, pad), (0, 0)))
                   for a in (r, k2, v, kk, kka, lw)]
    y, s_out = _scan(*scan_in, _state_to_blockdiag(wkv_in), scan_chunk, scan_seqs)
    y2d = y[:, :t].reshape(n, d_b)
    flat = lambda a: a.reshape(n, a.shape[-1])
    routed = _merge(y2d, flat(g), flat(bonus), flat(sgb), apart, x2d, tm, p["lnx_g"],
                    p["lnx_b"], p["bd"], p["w_b_out"], p["w_out"], p["norm2_g"],
                    p["router_w"], p["router_b"])
    return routed, vn, shift_out, _blockdiag_to_state(s_out)


def kernel(x_prompt, x_sample, state_shift, state_wkv, norm1_g, w_in, mu_shift, vnorm_g, vnorm_b, w_spatial, b_spatial, w_a_out, w0, w2, a0, a2, g2, k_k, k_a, r_k, lnx_g, lnx_b, w_b_out, w_out, norm2_g, router_w, router_b, exp_w_gu, exp_b_gu, exp_w_down, exp_b_down, normf_g):
    depth = w_in.shape[0]
    assert depth == 1, "the final norm is fused into the single layer's MoE call"
    d_model = x_prompt.shape[-1]
    d_b = w0.shape[-1]
    shift_w = mu_shift.shape[-1]
    d_a = vnorm_g.shape[-1]
    bp, tp, _ = x_prompt.shape
    bs, ts, _ = x_sample.shape
    head_id = jnp.arange(d_b) // HEAD_B
    bd = (head_id[:, None] == head_id[None, :]).astype(BF16)
    row = lambda a: a.reshape(1, -1)

    hp, hs = x_prompt, x_sample
    shift_p = jnp.zeros((depth, bp, shift_w), state_shift.dtype)
    wkv_p = jnp.zeros((depth, bp) + state_wkv.shape[2:], state_wkv.dtype)
    vrows, shifts_p, wkvs_p, shifts_s, wkvs_s = [], [], [], [], []
    for l in range(depth):
        wi = w_in[l].astype(BF16)
        o = shift_w
        p = dict(
            norm1_g=row(norm1_g[l]), wcur=wi[:, :o], wu=wi[:, o:o + d_a],
            wv=wi[:, o + d_a:o + 2 * d_a], wga=wi[:, o + 2 * d_a:o + 2 * d_a + d_model],
            wgb=wi[:, o + 2 * d_a + d_model:], mu_shift=row(mu_shift[l]),
            vnorm_g=row(vnorm_g[l]), vnorm_b=row(vnorm_b[l]), w_spatial=w_spatial[l],
            b_spatial=b_spatial[l], w_a_out=w_a_out[l].astype(BF16), w0=row(w0[l]), w2=w2[l],
            a0=row(a0[l]), a2=a2[l], g2=g2[l], k_k=row(k_k[l]), k_a=row(k_a[l]),
            r_k=row(r_k[l]), lnx_g=row(lnx_g[l]), lnx_b=row(lnx_b[l]),
            w_b_out=w_b_out[l].astype(BF16), w_out=w_out[l].astype(BF16),
            norm2_g=row(norm2_g[l]), router_w=router_w[l], router_b=row(router_b[l]), bd=bd)
        routed_p, _, sh_p, s_p = _stream(hp, shift_p[l], wkv_p[l], p, 512, None, 64, 4)
        routed_s, vn_s, sh_s, s_s = _stream(hs, state_shift[l], state_wkv[l], p, 512, ts, 8, 8)
        h_p, xp_p, topi_p, prob_p, rank_p, cnt_p = routed_p
        h_s, xp_s, topi_s, prob_s, rank_s, cnt_s = routed_s
        n_p, n_s = h_p.shape[0], h_s.shape[0]
        n_e = router_w.shape[-1]
        n_tiles = (n_p + n_s) * TOP_K // MOE_TM + n_e
        pos_p, pos_s, tile0, cnt = _route(topi_p, rank_p, cnt_p, topi_s, rank_s, cnt_s)
        pos_t = jnp.concatenate([pos_p, pos_s], axis=0).T
        xs = _scatter_rows(jnp.concatenate([xp_p, xp_s], axis=0), pos_t, n_tiles * MOE_TM)
        ys = _moe(xs, tile0, cnt, exp_w_gu[l], exp_b_gu[l][:, None, :],
                  exp_w_down[l], exp_b_down[l][:, None, :])
        yg = _gather_rows(ys, pos_t.reshape(-1)).reshape(TOP_K, n_p + n_s, d_model // 2)
        gf = row(normf_g)
        hp = _combine(h_p, yg, 0, prob_p, gf, 512).reshape(bp, tp, d_model)
        hs = _combine(h_s, yg, n_p, prob_s, gf, 512).reshape(bs, ts, d_model)
        vrows.append(vn_s.reshape(bs, ts, d_a))
        shifts_p.append(sh_p)
        wkvs_p.append(s_p)
        shifts_s.append(sh_s)
        wkvs_s.append(s_s)
    return (hp, hs, jnp.stack(shifts_p), jnp.stack(wkvs_p), jnp.stack(shifts_s),
            jnp.stack(wkvs_s), jnp.stack(vrows))
```

```python
import functools

import jax
import jax.numpy as jnp
from jax import lax
from jax.experimental import pallas as pl
from jax.experimental.pallas import tpu as pltpu
from jax.experimental.pallas import tpu_sc as plsc

F32 = jnp.float32
BF16 = jnp.bfloat16

CHUNK = 128
G_A = 8
HEAD_B = 64
R_W, R_A, R_G = 64, 64, 128
TOP_K = 4
SWIGLU_LIMIT = 7.0
SWIGLU_ALPHA = 1.702
EPS = 1e-5
GN_EPS = HEAD_B * 1e-5

LANES = 128
PAIR = 2 * HEAD_B
VMEM_LIMIT = 56 * 1024 * 1024


def _dot(a, b):
    return jnp.dot(a.astype(BF16), b.astype(BF16), preferred_element_type=F32)


def _split(x):
    hi = x.astype(BF16)
    lo = (x - hi.astype(F32)).astype(BF16)
    return hi, lo


def _dot3(a, b):
    ah, al = _split(a)
    bh, bl = _split(b)
    return (jnp.dot(ah, bh, preferred_element_type=F32)
            + jnp.dot(ah, bl, preferred_element_type=F32)
            + jnp.dot(al, bh, preferred_element_type=F32))


def _dot_exact_rhs(a, b_bf16):
    ah, al = _split(a)
    return (jnp.dot(ah, b_bf16, preferred_element_type=F32)
            + jnp.dot(al, b_bf16, preferred_element_type=F32))


def _stack_rhs3(b):
    bh, bl = _split(b)
    return jnp.concatenate([bh, bh, bl], axis=0)


def _dot3_stacked(a, b_stacked):
    ah, al = _split(a)
    return jnp.dot(jnp.concatenate([ah, al, ah], axis=1), b_stacked, preferred_element_type=F32)


def _head_sums(x, ones_bd):
    w = ones_bd.shape[0]
    return jnp.concatenate([_dot_exact_rhs(x[:, c:c + w], ones_bd)
                            for c in range(0, x.shape[1], w)], axis=1)


def _dot_exact_lhs(a_bf16, b):
    bh, bl = _split(b)
    bm = b - bh.astype(F32) - bl.astype(F32)
    return (jnp.dot(a_bf16, bh, preferred_element_type=F32)
            + jnp.dot(a_bf16, bl, preferred_element_type=F32)
            + jnp.dot(a_bf16, bm.astype(BF16), preferred_element_type=F32))


def _rms(x, g):
    return x * lax.rsqrt(jnp.mean(x * x, axis=-1, keepdims=True) + EPS) * g


def _sigmoid(x):
    return 1.0 / (1.0 + jnp.exp(-x))


def _softplus(z):
    return jnp.maximum(z, 0.0) + jnp.log(1.0 + jnp.exp(-jnp.abs(z)))


def _full_spec(shape):
    nd = len(shape)
    return pl.BlockSpec(shape, lambda *_: (0,) * nd, pipeline_mode=pl.Buffered(1))


def _branch_a_kernel(x_ref, g1_ref, wu_ref, wv_ref, wga_ref, vg_ref, vb_ref, wmix_ref,
                     bias_ref, wao_ref, apart_ref, vn_ref):
    xn = _rms(x_ref[...], g1_ref[...]).astype(BF16)
    v = jnp.dot(xn, wv_ref[...], preferred_element_type=F32)
    mu = jnp.mean(v, axis=-1, keepdims=True)
    d = v - mu
    var = jnp.mean(d * d, axis=-1, keepdims=True)
    vn = d * lax.rsqrt(var + EPS) * vg_ref[...] + vb_ref[...]
    vn_ref[...] = vn
    vnb = vn.astype(BF16)
    c_a = vnb.shape[1] // G_A
    rb = wmix_ref.shape[1]
    mixed = jnp.concatenate(
        [jnp.concatenate(
            [jnp.dot(wmix_ref[g], vnb[r0:r0 + rb, g * c_a:(g + 1) * c_a],
                     preferred_element_type=F32) for g in range(G_A)], axis=1)
         for r0 in range(0, vnb.shape[0], rb)], axis=0) + bias_ref[...]
    u = jnp.dot(xn, wu_ref[...], preferred_element_type=F32)
    ya = (u * mixed).astype(BF16)
    ga = jnp.dot(xn, wga_ref[...], preferred_element_type=F32)
    apart_ref[...] = _sigmoid(ga) * jnp.dot(ya, wao_ref[...], preferred_element_type=F32)


def _branch_a(x2d, tm, g1, wu, wv, wga, vg, vb, wmix, bias_full, wao):
    n, d = x2d.shape
    d_a = wu.shape[1]
    tok = lambda w: pl.BlockSpec((tm, w), lambda i: (i, 0))
    return pl.pallas_call(
        _branch_a_kernel,
        grid=(n // tm,),
        in_specs=[tok(d), _full_spec(g1.shape), _full_spec(wu.shape), _full_spec(wv.shape),
                  _full_spec(wga.shape), _full_spec(vg.shape), _full_spec(vb.shape),
                  _full_spec(wmix.shape), _full_spec(bias_full.shape), _full_spec(wao.shape)],
        out_specs=[tok(d), tok(d_a)],
        out_shape=[jax.ShapeDtypeStruct((n, d), F32), jax.ShapeDtypeStruct((n, d_a), F32)],
        compiler_params=pltpu.CompilerParams(dimension_semantics=("parallel",),
                                             vmem_limit_bytes=VMEM_LIMIT),
        name="branch_a",
    )(x2d, g1, wu, wv, wga, vg, vb, wmix, bias_full, wao)


def _branch_b_kernel(x_ref, g1_ref, wcur_ref, wgb_ref, mu_ref, ext_ref, w0_ref, w2_ref,
                     a0_ref, a2_ref, g2_ref, kk_ref, ka_ref, rk_ref, bd_ref,
                     r_out, k_out, v_out, kk_out, kka_out, lw_out, g_out, bonus_out,
                     sgb_out, cur_out, carry_scr, *, seq_rows, d_b):
    tm = x_ref.shape[0]
    xn = _rms(x_ref[...], g1_ref[...]).astype(BF16)
    cur = jnp.dot(xn, wcur_ref[...], preferred_element_type=F32)
    rolled = pltpu.roll(cur, 1, axis=0)
    row = lax.broadcasted_iota(jnp.int32, (tm, 1), 0)
    if seq_rows is None:
        first_tile = pl.program_id(1) == 0
        carry = jnp.where(first_tile, 0.0, carry_scr[...])
        prev = jnp.where(row == 0, carry, rolled)
        carry_scr[...] = cur[tm - 1:tm, :]
        cur_out[...] = cur[tm - 8:tm, :]
    else:
        prev = jnp.where(row % seq_rows == 0, ext_ref[...], rolled)
        cur_out[...] = cur
    xs = cur + (prev - cur) * mu_ref[...]
    r = xs[:, 0:d_b]
    k = xs[:, d_b:2 * d_b]
    v = xs[:, 2 * d_b:3 * d_b]
    o = 3 * d_b
    xw = xs[:, o:o + R_W]
    xa = xs[:, o + R_W:o + R_W + R_A]
    xg = xs[:, o + R_W + R_A:o + R_W + R_A + R_G]
    wl = w0_ref[...] + _dot3_stacked(jnp.tanh(xw), w2_ref[...])
    lw = -jnp.exp(-_softplus(-wl) - 0.5)
    a = _sigmoid(a0_ref[...] + _dot3_stacked(xa, a2_ref[...]))
    g = _dot3_stacked(_sigmoid(xg), g2_ref[...])
    bd = bd_ref[...]
    kkr = k * kk_ref[...]
    n2 = _head_sums(kkr * kkr, bd)
    kk = kkr / jnp.maximum(jnp.sqrt(n2), 1e-12)
    k2 = k * (1.0 + (a - 1.0) * ka_ref[...])
    bonus = _head_sums(r * k2 * rk_ref[...], bd) * v
    r_out[...] = r
    k_out[...] = k2
    v_out[...] = v
    kk_out[...] = kk
    kka_out[...] = kk * a
    lw_out[...] = lw
    g_out[...] = g
    bonus_out[...] = bonus
    sgb_out[...] = _sigmoid(jnp.dot(xn, wgb_ref[...], preferred_element_type=F32))


def _branch_b(x3d, tm, seq_rows, ext, g1, wcur, wgb, mu, w0, w2, a0, a2, g2, k_k, k_a, r_k, bd):
    nb, t, d = x3d.shape
    d_b = w0.shape[1]
    shift_w = wcur.shape[1]
    nt = t // tm
    tok = lambda w: pl.BlockSpec((None, tm, w), lambda b, i: (b, i, 0))
    cur_rows = 8 if seq_rows is None else tm
    outs = [jax.ShapeDtypeStruct((nb, t, d_b), F32)] * 8 + [
        jax.ShapeDtypeStruct((nb, t, d), F32),
        jax.ShapeDtypeStruct((nb, nt * cur_rows, shift_w), F32)]
    out_specs = [tok(d_b)] * 8 + [tok(d), pl.BlockSpec((None, cur_rows, shift_w),
                                                       lambda b, i: (b, i, 0))]
    weights = (g1, wcur, wgb, mu)
    small = (w0, w2, a0, a2, g2, k_k, k_a, r_k, bd)
    return pl.pallas_call(
        functools.partial(_branch_b_kernel, seq_rows=seq_rows, d_b=d_b),
        grid=(nb, nt),
        in_specs=[tok(d)] + [_full_spec(w.shape) for w in weights]
        + [tok(shift_w) if seq_rows is not None else _full_spec(ext.shape)]
        + [_full_spec(w.shape) for w in small],
        out_specs=out_specs,
        out_shape=outs,
        scratch_shapes=[pltpu.VMEM((1, shift_w), F32)],
        compiler_params=pltpu.CompilerParams(dimension_semantics=("parallel", "arbitrary"),
                                             vmem_limit_bytes=VMEM_LIMIT),
        name="branch_b",
    )(x3d, *weights, ext, *small)


def _scan_kernel(r_ref, k_ref, v_ref, kk_ref, kka_ref, lw_ref, s0_ref, y_ref, sout_ref, s_scr,
                 *, chunk):
    L = chunk
    W2 = 2 * L
    c = pl.program_id(1)

    @pl.when(c == 0)
    def _():
        s_scr[...] = s0_ref[...]

    n_seq, n_pairs = s_scr.shape[:2]
    iota = lambda shape, dim: lax.broadcasted_iota(jnp.int32, shape, dim)
    stack_mask = (iota((W2, 1), 0) >= L) == (iota((1, PAIR), 1) >= HEAD_B)
    bd_mask = (iota((W2, 1), 0) >= L) == (iota((1, W2), 1) >= L)
    assert L & (L - 1) == 0, "chunk length must be a power of two"
    rw = iota((L, W2), 0)
    cw = iota((L, W2), 1) & (L - 1)
    strict_w = cw < rw
    eye_w = (cw == rw).astype(F32)
    incl_w2 = (iota((L, 2 * W2), 1) & (L - 1)) <= iota((L, 2 * W2), 0)
    tri = (iota((L, L), 1) <= iota((L, L), 0)).astype(BF16)
    di = iota((PAIR, PAIR), 0)
    dj = iota((PAIR, PAIR), 1)
    diag = di == dj
    head_diag = (di >= HEAD_B) == (dj >= HEAD_B)

    def stack(x):
        return jnp.where(stack_mask, jnp.concatenate([x, x], axis=0), 0.0)

    def bd(xw):
        return jnp.where(bd_mask, jnp.concatenate([xw, xw], axis=0), 0.0)

    n_double = max(L.bit_length() - 2, 0)
    prep = []
    for s in range(n_seq):
        lw_all = lw_ref[s]
        cum_all = _dot_exact_lhs(tri, lw_all)
        for p in range(n_pairs):
            sl = slice(p * PAIR, (p + 1) * PAIR)
            cum = cum_all[:, sl]
            cum_last = cum[L - 1:L, :]
            e_neg = jnp.exp(-cum)
            e_rel = jnp.exp(cum_last - cum)
            kk = kk_ref[s, :, sl]
            kka = kka_ref[s, :, sl]
            kx = k_ref[s, :, sl]
            prep.append(dict(
                s=s, p=p, sl=sl, ab=-kk * jnp.exp(cum - lw_all[:, sl]),
                rb=r_ref[s, :, sl] * jnp.exp(cum), bb=kka * e_neg, kb=kx * e_neg,
                bt=kka * e_rel, kt=kx * e_rel, v=v_ref[s, :, sl], decay=jnp.exp(cum_last)))
    a_w = [lax.dot_general(
        jnp.concatenate([q["ab"], q["rb"]], axis=0).astype(BF16),
        jnp.concatenate([stack(q["bb"]), stack(q["kb"])], axis=0).astype(BF16),
        (((1,), (1,)), ((), ())), preferred_element_type=F32) for q in prep]
    nw = [jnp.where(strict_w, a[:L, :W2], 0.0) for a in a_w]
    tw = [eye_w + n for n in nw]
    nbd = [bd(n) for n in nw]
    for _ in range(n_double):
        nw = [_dot(n, b) for n, b in zip(nw, nbd)]
        nbd = [bd(n) for n in nw]
        tw = [t + _dot(t, b) for t, b in zip(tw, nbd)]
    akv = [_dot(jnp.where(strict_w, a[:L, W2:], 0.0), stack(q["v"])) for a, q in zip(a_w, prep)]
    tx = [_dot(t, jnp.concatenate([stack(q["ab"]), stack(kv)], axis=1))
          for t, q, kv in zip(tw, prep, akv)]
    ry = [_dot(jnp.where(incl_w2, a[L:], 0.0),
               jnp.concatenate(
                   [jnp.concatenate([stack(x[:, :PAIR]), stack(x[:, PAIR:])], axis=1),
                    jnp.concatenate([jnp.zeros((W2, PAIR), F32), stack(q["v"])], axis=1)], axis=0))
          for a, x, q in zip(a_w, tx, prep)]
    for q, x, y in zip(prep, tx, ry):
        st = s_scr[q["s"], q["p"]]
        us = _dot(jnp.concatenate([x[:, :PAIR], q["rb"] + y[:, :PAIR]], axis=0), st)
        y_ref[q["s"], :, q["sl"]] = us[L:] + y[:, PAIR:]
        u = us[:L] + x[:, PAIR:]
        lhs = jnp.concatenate([jnp.where(diag, q["decay"], 0.0),
                               jnp.concatenate([q["bt"], q["kt"]], axis=0).T], axis=1)
        g = _dot(lhs, jnp.concatenate([st, u, q["v"]], axis=0))
        s_scr[q["s"], q["p"]] = jnp.where(head_diag, g, 0.0)

    @pl.when(c == pl.num_programs(1) - 1)
    def _():
        sout_ref[...] = s_scr[...]


def _scan(r, k, v, kk, kka, lw, s0, chunk, n_seq):
    nb, t, d_b = r.shape
    n_pairs = d_b // PAIR
    tok = pl.BlockSpec((n_seq, chunk, d_b), lambda b, c: (b, c, 0))
    st = pl.BlockSpec((n_seq, n_pairs, PAIR, PAIR), lambda b, c: (b, 0, 0, 0))
    return pl.pallas_call(
        functools.partial(_scan_kernel, chunk=chunk),
        grid=(nb // n_seq, t // chunk),
        in_specs=[tok] * 6 + [st],
        out_specs=[tok, st],
        out_shape=[jax.ShapeDtypeStruct((nb, t, d_b), F32),
                   jax.ShapeDtypeStruct((nb, n_pairs, PAIR, PAIR), F32)],
        scratch_shapes=[pltpu.VMEM((n_seq, n_pairs, PAIR, PAIR), F32)],
        compiler_params=pltpu.CompilerParams(dimension_semantics=("parallel", "arbitrary"),
                                             vmem_limit_bytes=VMEM_LIMIT),
        name="rwkv_scan",
    )(r, k, v, kk, kka, lw, s0)


def _state_to_blockdiag(wkv):
    b, h, n, _ = wkv.shape
    st = jnp.swapaxes(wkv, -1, -2).reshape(b, h // 2, 2, n, n)
    z = jnp.zeros_like(st[:, :, 0])
    top = jnp.concatenate([st[:, :, 0], z], axis=-1)
    bot = jnp.concatenate([z, st[:, :, 1]], axis=-1)
    return jnp.concatenate([top, bot], axis=-2)


def _blockdiag_to_state(s):
    b, hp, _, _ = s.shape
    n = HEAD_B
    blocks = jnp.stack([s[:, :, :n, :n], s[:, :, n:, n:]], axis=2)
    return jnp.swapaxes(blocks, -1, -2).reshape(b, hp * 2, n, n)


def _pack_bf16_pairs(x):
    w = x.shape[1] // 2
    bits = lambda v: lax.bitcast_convert_type(v.astype(BF16).astype(F32), jnp.uint32)
    return (bits(x[:, :w]) >> 16) | (bits(x[:, w:]) & jnp.uint32(0xFFFF0000))


def _unpack_bf16_pairs(u):
    lo = lax.bitcast_convert_type(u << 16, F32)
    hi = lax.bitcast_convert_type(u & jnp.uint32(0xFFFF0000), F32)
    return lo, hi


def _merge_kernel(y_ref, g_ref, bonus_ref, sgb_ref, apart_ref, x_ref, lg_ref, lb_ref, bd_ref,
                  wbo_ref, wo_ref, g2_ref, rw_ref, rb_ref,
                  h_out, xp_out, topi_out, prob_out, rank_out, cnt_out, cnt_scr):
    step = pl.program_id(0)

    @pl.when(step == 0)
    def _():
        cnt_scr[...] = jnp.zeros_like(cnt_scr)

    bd = bd_ref[...]
    y = y_ref[...]
    inv_n = 1.0 / HEAD_B
    mu = _head_sums(y, bd) * inv_n
    d = y - mu
    var = _head_sums(d * d, bd) * inv_n
    yn = d * lax.rsqrt(var + GN_EPS) * lg_ref[...] + lb_ref[...]
    yb = ((yn + bonus_ref[...]) * g_ref[...]).astype(BF16)
    merged = apart_ref[...] + sgb_ref[...] * jnp.dot(yb, wbo_ref[...], preferred_element_type=F32)
    h = x_ref[...] + jnp.dot(merged.astype(BF16), wo_ref[...], preferred_element_type=F32)
    h_out[...] = h
    xn2 = _rms(h, g2_ref[...])
    xp_out[...] = _pack_bf16_pairs(xn2)
    n_e = rb_ref.shape[1]
    xh, xl = _split(xn2)
    rw = rw_ref[...]
    t = jnp.dot(xh, rw, preferred_element_type=F32)
    logits = (t[:, :n_e] + t[:, n_e:] + jnp.dot(xl, rw[:, :n_e], preferred_element_type=F32)
              + rb_ref[...])
    tm = logits.shape[0]
    idx = lax.broadcasted_iota(jnp.int32, logits.shape, 1).astype(F32)
    work = logits
    tops, hots, sels = [], [], []
    for _ in range(TOP_K):
        m = jnp.max(work, axis=-1, keepdims=True)
        sel = jnp.min(jnp.where(work == m, idx, float(n_e)), axis=-1, keepdims=True)
        hot = idx == sel
        tops.append(m)
        hots.append(hot)
        sels.append(sel)
        work = jnp.where(hot, -jnp.inf, work)
    es = [jnp.exp(t - tops[0]) for t in tops]
    denom = es[0] + es[1] + es[2] + es[3]
    topi_out[...] = jnp.concatenate(sels, axis=1).astype(jnp.int32)
    prob_out[...] = jnp.concatenate([e / denom for e in es], axis=1)
    hot_any = jnp.zeros_like(logits)
    for hot in hots:
        hot_any = hot_any + hot.astype(F32)
    ri = lax.broadcasted_iota(jnp.int32, (tm, tm), 0)
    ci = lax.broadcasted_iota(jnp.int32, (tm, tm), 1)
    before = _dot((ci < ri).astype(BF16), hot_any) + cnt_scr[...]
    rank_out[...] = jnp.concatenate(
        [jnp.sum(jnp.where(hot, before, 0.0), axis=-1, keepdims=True) for hot in hots],
        axis=1).astype(jnp.int32)
    total = cnt_scr[...] + jnp.sum(hot_any, axis=0, keepdims=True)
    cnt_scr[...] = total
    cnt_out[...] = total.astype(jnp.int32)


def _merge(y, g, bonus, sgb, apart, x2d, tm, lnx_g, lnx_b, bd, wbo, wo, g2n, rw, rb):
    n, d = x2d.shape
    d_b = y.shape[1]
    n_e = rb.shape[1]
    tok = lambda w: pl.BlockSpec((tm, w), lambda i: (i, 0))
    weights = (lnx_g, lnx_b, bd, wbo, wo, g2n, rw, rb)
    return pl.pallas_call(
        _merge_kernel,
        grid=(n // tm,),
        in_specs=[tok(d_b), tok(d_b), tok(d_b), tok(d), tok(d), tok(d)]
        + [_full_spec(w.shape) for w in weights],
        out_specs=[tok(d), tok(d // 2), tok(TOP_K), tok(TOP_K), tok(TOP_K),
                   pl.BlockSpec((1, n_e), lambda i: (0, 0))],
        out_shape=[jax.ShapeDtypeStruct((n, d), F32),
                   jax.ShapeDtypeStruct((n, d // 2), jnp.uint32),
                   jax.ShapeDtypeStruct((n, TOP_K), jnp.int32),
                   jax.ShapeDtypeStruct((n, TOP_K), F32),
                   jax.ShapeDtypeStruct((n, TOP_K), jnp.int32),
                   jax.ShapeDtypeStruct((1, n_e), jnp.int32)],
        scratch_shapes=[pltpu.VMEM((1, n_e), F32)],
        compiler_params=pltpu.CompilerParams(dimension_semantics=("arbitrary",),
                                             vmem_limit_bytes=VMEM_LIMIT),
        name="merge_router",
    )(y, g, bonus, sgb, apart, x2d, *weights)


MOE_TM = 512
MOE_FF_CHUNK = 512
CAST_ROWS = 256


def _route(topi_p, rank_p, cnt_p, topi_s, rank_s, cnt_s):
    n_e = cnt_p.shape[-1]
    cnt_p, cnt_s = cnt_p.reshape(n_e), cnt_s.reshape(n_e)
    cnt = cnt_p + cnt_s
    padded = (cnt + MOE_TM - 1) // MOE_TM * MOE_TM
    ends = jnp.cumsum(padded)
    base = ends - padded
    lookup = lambda table, idx: jnp.sum(
        jnp.where(idx[..., None] == jnp.arange(n_e), table, 0), axis=-1)
    pos_p = lookup(base, topi_p) + rank_p
    pos_s = lookup(base + cnt_p, topi_s) + rank_s
    return (pos_p.astype(jnp.int32), pos_s.astype(jnp.int32),
            (base // MOE_TM).astype(jnp.int32), cnt.astype(jnp.int32))


def _moe_kernel(tile0_ref, cnt_ref, xs_hbm, wgu_ref, bgu_ref, wd_ref, bdn_ref, ys_hbm,
                wgu_b, wd_b, xbuf, ybuf, sem_in, sem_out):
    e = pl.program_id(0)
    tile0 = tile0_ref[e]
    cnt = cnt_ref[e]
    n_t = (cnt + MOE_TM - 1) // MOE_TM
    d, gu = wgu_ref.shape
    d_ff = wd_ref.shape[0]
    half = d // 2

    def in_copy(tile, slot):
        return pltpu.make_async_copy(xs_hbm.at[pl.ds(tile * MOE_TM, MOE_TM)], xbuf.at[slot],
                                     sem_in.at[slot])

    def out_copy(tile, slot):
        return pltpu.make_async_copy(ybuf.at[slot], ys_hbm.at[pl.ds(tile * MOE_TM, MOE_TM)],
                                     sem_out.at[slot])

    @pl.when(n_t > 0)
    def _():
        in_copy(tile0, 0).start()
        for r0 in range(0, d, CAST_ROWS):
            wgu_b[r0:r0 + CAST_ROWS, :] = wgu_ref[r0:r0 + CAST_ROWS, :].astype(BF16)
        for r0 in range(0, d_ff, CAST_ROWS):
            wd_b[r0:r0 + CAST_ROWS, :] = wd_ref[r0:r0 + CAST_ROWS, :].astype(BF16)

    def tile_step(j, carry):
        slot = j % 2
        in_copy(tile0 + j, slot).wait()

        @pl.when(j + 1 < n_t)
        def _():
            in_copy(tile0 + j + 1, 1 - slot).start()

        @pl.when(j >= 2)
        def _():
            out_copy(tile0 + j - 2, slot).wait()

        lo, hi = _unpack_bf16_pairs(xbuf[slot])
        valid = lax.broadcasted_iota(jnp.int32, lo.shape, 0) < cnt - j * MOE_TM
        x_lo = jnp.where(valid, lo, 0.0).astype(BF16)
        x_hi = jnp.where(valid, hi, 0.0).astype(BF16)

        def proj(c0):
            cols = slice(c0, c0 + MOE_FF_CHUNK)
            return (jnp.dot(x_lo, wgu_b[:half, cols], preferred_element_type=F32)
                    + jnp.dot(x_hi, wgu_b[half:, cols], preferred_element_type=F32)
                    + bgu_ref[:, cols])

        y = jnp.zeros((MOE_TM, d), F32)
        for f in range(d_ff // MOE_FF_CHUNK):
            c0 = f * MOE_FF_CHUNK
            gate = jnp.minimum(proj(c0), SWIGLU_LIMIT)
            up = jnp.clip(proj(d_ff + c0), -SWIGLU_LIMIT, SWIGLU_LIMIT)
            hh = (up + 1.0) * gate * _sigmoid(gate * SWIGLU_ALPHA)
            y = y + jnp.dot(hh.astype(BF16), wd_b[c0:c0 + MOE_FF_CHUNK, :],
                            preferred_element_type=F32)
        ybuf[slot] = _pack_bf16_pairs(y + bdn_ref[...])
        out_copy(tile0 + j, slot).start()
        return carry

    lax.fori_loop(0, n_t, tile_step, 0)

    @pl.when(n_t >= 2)
    def _():
        out_copy(tile0 + n_t - 2, n_t % 2).wait()

    @pl.when(n_t >= 1)
    def _():
        out_copy(tile0 + n_t - 1, (n_t - 1) % 2).wait()


def _moe(xs, tile0, cnt, wgu, bgu, wd, bdn):
    p_rows, half = xs.shape
    n_e, d, gu = wgu.shape
    d_ff = wd.shape[1]
    w_blk = lambda e, tile0, cnt: (e, 0, 0)
    grid_spec = pltpu.PrefetchScalarGridSpec(
        num_scalar_prefetch=2,
        grid=(n_e,),
        in_specs=[pl.BlockSpec(memory_space=pl.ANY),
                  pl.BlockSpec((None, d, gu), w_blk),
                  pl.BlockSpec((None, 1, gu), w_blk),
                  pl.BlockSpec((None, d_ff, d), w_blk),
                  pl.BlockSpec((None, 1, d), w_blk)],
        out_specs=pl.BlockSpec(memory_space=pl.ANY),
        scratch_shapes=[pltpu.VMEM((d, gu), BF16), pltpu.VMEM((d_ff, d), BF16),
                        pltpu.VMEM((2, MOE_TM, half), jnp.uint32),
                        pltpu.VMEM((2, MOE_TM, half), jnp.uint32),
                        pltpu.SemaphoreType.DMA((2,)), pltpu.SemaphoreType.DMA((2,))])
    return pl.pallas_call(
        _moe_kernel,
        grid_spec=grid_spec,
        out_shape=jax.ShapeDtypeStruct((p_rows, half), jnp.uint32),
        compiler_params=pltpu.CompilerParams(dimension_semantics=("arbitrary",),
                                             vmem_limit_bytes=VMEM_LIMIT),
        name="moe_experts",
    )(tile0, cnt, xs, wgu, bgu, wd, bdn)


def _combine_kernel(h_ref, yg_ref, prob_ref, gf_ref, out_ref):
    half = h_ref.shape[1] // 2
    prob = prob_ref[...]
    acc_lo = jnp.zeros((h_ref.shape[0], half), F32)
    acc_hi = jnp.zeros((h_ref.shape[0], half), F32)
    for k in range(TOP_K):
        lo, hi = _unpack_bf16_pairs(yg_ref[k])
        pk = prob[:, k:k + 1]
        acc_lo = acc_lo + pk * lo
        acc_hi = acc_hi + pk * hi
    z = h_ref[...] + jnp.concatenate([acc_lo, acc_hi], axis=1)
    out_ref[...] = _rms(z, gf_ref[...])


def _combine(h, yg, row0, prob, gf, tm):
    n, d = h.shape
    tok = lambda w: pl.BlockSpec((tm, w), lambda i: (i, 0))
    t0 = row0 // tm
    return pl.pallas_call(
        _combine_kernel,
        grid=(n // tm,),
        in_specs=[tok(d), pl.BlockSpec((TOP_K, tm, d // 2), lambda i: (0, i + t0, 0)),
                  tok(TOP_K), _full_spec(gf.shape)],
        out_specs=tok(d),
        out_shape=jax.ShapeDtypeStruct((n, d), F32),
        compiler_params=pltpu.CompilerParams(dimension_semantics=("parallel",),
                                             vmem_limit_bytes=VMEM_LIMIT),
        name="moe_combine",
    )(h, yg, prob, gf)


SC_CORES = 2
SC_SUBCORES = 16
SC_WORKERS = SC_CORES * SC_SUBCORES
SC_MAX_INDEX = 128


def _sc_chunk(rows_per_worker):
    for c in range(SC_MAX_INDEX, 7, -8):
        if rows_per_worker % c == 0:
            return c
    raise ValueError(f"no 8-aligned chunk divides {rows_per_worker} rows")


def _sc_mesh():
    return plsc.VectorSubcoreMesh(core_axis_name="c", subcore_axis_name="s")


def _sc_worker():
    return lax.axis_index("s") * SC_CORES + lax.axis_index("c")


def _scatter_rows(x, pos_t, p_rows):
    n, w = x.shape
    n_k = pos_t.shape[0]
    per = n // SC_WORKERS
    chunk = _sc_chunk(per)
    pos_flat = pos_t.reshape(n_k * n)

    def body(x_hbm, pos_hbm, xs_hbm, idx_v, rows_v, sem):
        base = _sc_worker() * per

        @pl.loop(0, per // chunk)
        def _(c):
            off = pl.multiple_of(base + c * chunk, 8)
            pltpu.sync_copy(x_hbm.at[pl.ds(off, chunk)], rows_v)
            for k in range(n_k):
                pltpu.sync_copy(pos_hbm.at[pl.ds(pl.multiple_of(k * n + off, 8), chunk)], idx_v)
                pltpu.async_copy(rows_v, xs_hbm.at[idx_v], sem).wait()

    return pl.kernel(
        body, out_type=jax.ShapeDtypeStruct((p_rows, w), x.dtype), mesh=_sc_mesh(),
        scratch_types=[pltpu.VMEM((chunk,), jnp.int32), pltpu.VMEM((chunk, w), x.dtype),
                       pltpu.SemaphoreType.DMA],
        name="sc_scatter_rows")(x, pos_flat)


def _gather_rows(table, idx):
    n = idx.shape[0]
    w = table.shape[1]
    per = n // SC_WORKERS
    chunk = _sc_chunk(per)

    def body(table_hbm, idx_hbm, out_hbm, idx_v, rows_v, sem):
        base = _sc_worker() * per

        @pl.loop(0, per // chunk)
        def _(c):
            off = pl.multiple_of(base + c * chunk, 8)
            pltpu.sync_copy(idx_hbm.at[pl.ds(off, chunk)], idx_v)
            pltpu.async_copy(table_hbm.at[idx_v], rows_v, sem).wait()
            pltpu.sync_copy(rows_v, out_hbm.at[pl.ds(off, chunk)])

    return pl.kernel(
        body, out_type=jax.ShapeDtypeStruct((n, w), table.dtype), mesh=_sc_mesh(),
        scratch_types=[pltpu.VMEM((chunk,), jnp.int32), pltpu.VMEM((chunk, w), table.dtype),
                       pltpu.SemaphoreType.DMA],
        name="sc_gather_rows")(table, idx)


def _mix_matrix(ws, block, seq_rows):
    causal = jnp.tril(jnp.ones((CHUNK, CHUNK), dtype=bool))
    w = jnp.where(causal[None], ws, 0.0)[:, :seq_rows, :seq_rows]
    eye = jnp.eye(block // seq_rows, dtype=ws.dtype)
    return jnp.einsum("ab,gts->gatbs", eye, w).reshape(ws.shape[0], block, block).astype(BF16)


def _stream(x, shift_in, wkv_in, p, tm, seq_rows, scan_chunk, scan_seqs):
    nb, t, d = x.shape
    n = nb * t
    x2d = x.reshape(n, d)
    rows = CHUNK if seq_rows is None else seq_rows
    wmix = _mix_matrix(p["w_spatial"], max(rows, LANES), rows)
    pos = jnp.arange(tm) % rows
    d_a = p["wu"].shape[1]
    bias_full = jnp.repeat(p["b_spatial"].T[pos], d_a // G_A, axis=1)
    apart, vn = _branch_a(x2d, tm, p["norm1_g"], p["wu"], p["wv"], p["wga"], p["vnorm_g"],
                          p["vnorm_b"], wmix, bias_full, p["w_a_out"])
    if seq_rows is None:
        xb, ext = x, jnp.zeros((1, p["wcur"].shape[1]), F32)
    else:
        xb = x2d.reshape(n // tm, tm, d)
        ext = jnp.repeat(shift_in, seq_rows, axis=0).reshape(n // tm, tm, -1)
    outs = _branch_b(xb, tm, seq_rows, ext, p["norm1_g"], p["wcur"], p["wgb"], p["mu_shift"],
                     p["w0"], p["w2"], p["a0"], p["a2"], p["g2"], p["k_k"], p["k_a"], p["r_k"],
                     p["bd"])
    r, k2, v, kk, kka, lw, g, bonus, sgb, cur = outs
    d_b = r.shape[-1]
    if seq_rows is None:
        shift_out = cur[:, -1, :]
        scan_in = [a.reshape(nb, t, d_b) for a in (r, k2, v, kk, kka, lw)]
    else:
        shift_out = cur.reshape(nb, t, -1)[:, -1, :]
        pad = scan_chunk - t
        scan_in = [jnp.pad(a.reshape(nb, t, d_b), ((0, 0), (0, pad), (0, 0)))
                   for a in (r, k2, v, kk, kka, lw)]
    y, s_out = _scan(*scan_in, _state_to_blockdiag(wkv_in), scan_chunk, scan_seqs)
    y2d = y[:, :t].reshape(n, d_b)
    flat = lambda a: a.reshape(n, a.shape[-1])
    routed = _merge(y2d, flat(g), flat(bonus), flat(sgb), apart, x2d, tm, p["lnx_g"],
                    p["lnx_b"], p["bd"], p["w_b_out"], p["w_out"], p["norm2_g"],
                    p["router_w"], p["router_b"])
    return routed, vn, shift_out, _blockdiag_to_state(s_out)


def kernel(x_prompt, x_sample, state_shift, state_wkv, norm1_g, w_in, mu_shift, vnorm_g, vnorm_b, w_spatial, b_spatial, w_a_out, w0, w2, a0, a2, g2, k_k, k_a, r_k, lnx_g, lnx_b, w_b_out, w_out, norm2_g, router_w, router_b, exp_w_gu, exp_b_gu, exp_w_down, exp_b_down, normf_g):
    depth = w_in.shape[0]
    assert depth == 1, "the final norm is fused into the single layer's MoE call"
    d_model = x_prompt.shape[-1]
    d_b = w0.shape[-1]
    shift_w = mu_shift.shape[-1]
    d_a = vnorm_g.shape[-1]
    bp, tp, _ = x_prompt.shape
    bs, ts, _ = x_sample.shape
    head_id = jnp.arange(2 * LANES) // HEAD_B
    bd = (head_id[:, None] == head_id[None, :]).astype(BF16)
    row = lambda a: a.reshape(1, -1)

    hp, hs = x_prompt, x_sample
    shift_p = jnp.zeros((depth, bp, shift_w), state_shift.dtype)
    wkv_p = jnp.zeros((depth, bp) + state_wkv.shape[2:], state_wkv.dtype)
    vrows, shifts_p, wkvs_p, shifts_s, wkvs_s = [], [], [], [], []
    for l in range(depth):
        wi = w_in[l].astype(BF16)
        o = shift_w
        p = dict(
            norm1_g=row(norm1_g[l]), wcur=wi[:, :o], wu=wi[:, o:o + d_a],
            wv=wi[:, o + d_a:o + 2 * d_a], wga=wi[:, o + 2 * d_a:o + 2 * d_a + d_model],
            wgb=wi[:, o + 2 * d_a + d_model:], mu_shift=row(mu_shift[l]),
            vnorm_g=row(vnorm_g[l]), vnorm_b=row(vnorm_b[l]), w_spatial=w_spatial[l],
            b_spatial=b_spatial[l], w_a_out=w_a_out[l].astype(BF16), w0=row(w0[l]),
            w2=_stack_rhs3(w2[l]), a0=row(a0[l]), a2=_stack_rhs3(a2[l]), g2=_stack_rhs3(g2[l]),
            k_k=row(k_k[l]), k_a=row(k_a[l]),
            r_k=row(r_k[l]), lnx_g=row(lnx_g[l]), lnx_b=row(lnx_b[l]),
            w_b_out=w_b_out[l].astype(BF16), w_out=w_out[l].astype(BF16),
            norm2_g=row(norm2_g[l]), router_w=jnp.concatenate(_split(router_w[l]), axis=1),
            router_b=row(router_b[l]), bd=bd)
        routed_p, _, sh_p, s_p = _stream(hp, shift_p[l], wkv_p[l], p, 512, None, 64, 4)
        routed_s, vn_s, sh_s, s_s = _stream(hs, state_shift[l], state_wkv[l], p, 512, ts, 8, 8)
        h_p, xp_p, topi_p, prob_p, rank_p, cnt_p = routed_p
        h_s, xp_s, topi_s, prob_s, rank_s, cnt_s = routed_s
        n_p, n_s = h_p.shape[0], h_s.shape[0]
        n_e = router_w.shape[-1]
        n_tiles = (n_p + n_s) * TOP_K // MOE_TM + n_e
        pos_p, pos_s, tile0, cnt = _route(topi_p, rank_p, cnt_p, topi_s, rank_s, cnt_s)
        pos_t = jnp.concatenate([pos_p, pos_s], axis=0).T
        xs = _scatter_rows(jnp.concatenate([xp_p, xp_s], axis=0), pos_t, n_tiles * MOE_TM)
        ys = _moe(xs, tile0, cnt, exp_w_gu[l], exp_b_gu[l][:, None, :],
                  exp_w_down[l], exp_b_down[l][:, None, :])
        yg = _gather_rows(ys, pos_t.reshape(-1)).reshape(TOP_K, n_p + n_s, d_model // 2)
        gf = row(normf_g)
        hp = _combine(h_p, yg, 0, prob_p, gf, 512).reshape(bp, tp, d_model)
        hs = _combine(h_s, yg, n_p, prob_s, gf, 512).reshape(bs, ts, d_model)
        vrows.append(vn_s.reshape(bs, ts, d_a))
        shifts_p.append(sh_p)
        wkvs_p.append(s_p)
        shifts_s.append(sh_s)
        wkvs_s.append(s_s)
    return (hp, hs, jnp.stack(shifts_p), jnp.stack(wkvs_p), jnp.stack(shifts_s),
            jnp.stack(wkvs_s), jnp.stack(vrows))
```

```python
import functools

import jax
import jax.numpy as jnp
from jax import lax
from jax.experimental import pallas as pl
from jax.experimental.pallas import tpu as pltpu
from jax.experimental.pallas import tpu_sc as plsc

F32 = jnp.float32
BF16 = jnp.bfloat16

CHUNK = 128
G_A = 8
HEAD_B = 64
R_W, R_A, R_G = 64, 64, 128
TOP_K = 4
SWIGLU_LIMIT = 7.0
SWIGLU_ALPHA = 1.702
EPS = 1e-5
GN_EPS = HEAD_B * 1e-5

LANES = 128
PAIR = 2 * HEAD_B
VMEM_LIMIT = 56 * 1024 * 1024


def _dot(a, b):
    return jnp.dot(a.astype(BF16), b.astype(BF16), preferred_element_type=F32)


def _split(x):
    hi = x.astype(BF16)
    lo = (x - hi.astype(F32)).astype(BF16)
    return hi, lo


def _dot3(a, b):
    ah, al = _split(a)
    bh, bl = _split(b)
    return (jnp.dot(ah, bh, preferred_element_type=F32)
            + jnp.dot(ah, bl, preferred_element_type=F32)
            + jnp.dot(al, bh, preferred_element_type=F32))


def _dot_exact_rhs(a, b_bf16):
    ah, al = _split(a)
    return (jnp.dot(ah, b_bf16, preferred_element_type=F32)
            + jnp.dot(al, b_bf16, preferred_element_type=F32))


def _stack_rhs3(b):
    bh, bl = _split(b)
    return jnp.concatenate([bh, bh, bl], axis=0)


def _dot3_stacked(a, b_stacked):
    ah, al = _split(a)
    return jnp.dot(jnp.concatenate([ah, al, ah], axis=1), b_stacked, preferred_element_type=F32)


def _head_sums(x, ones_bd):
    w = ones_bd.shape[0]
    return jnp.concatenate([_dot_exact_rhs(x[:, c:c + w], ones_bd)
                            for c in range(0, x.shape[1], w)], axis=1)


def _dot_exact_lhs(a_bf16, b):
    bh, bl = _split(b)
    bm = b - bh.astype(F32) - bl.astype(F32)
    return (jnp.dot(a_bf16, bh, preferred_element_type=F32)
            + jnp.dot(a_bf16, bl, preferred_element_type=F32)
            + jnp.dot(a_bf16, bm.astype(BF16), preferred_element_type=F32))


def _rms(x, g):
    return x * lax.rsqrt(jnp.mean(x * x, axis=-1, keepdims=True) + EPS) * g


def _sigmoid(x):
    return 1.0 / (1.0 + jnp.exp(-x))


def _softplus(z):
    return jnp.maximum(z, 0.0) + jnp.log(1.0 + jnp.exp(-jnp.abs(z)))


def _full_spec(shape):
    nd = len(shape)
    return pl.BlockSpec(shape, lambda *_: (0,) * nd, pipeline_mode=pl.Buffered(1))


def _branch_a_kernel(x_ref, g1_ref, wu_ref, wv_ref, wga_ref, vg_ref, vb_ref, wmix_ref,
                     bias_ref, wao_ref, apart_ref, vn_ref):
    xn = _rms(x_ref[...], g1_ref[...]).astype(BF16)
    v = jnp.dot(xn, wv_ref[...], preferred_element_type=F32)
    mu = jnp.mean(v, axis=-1, keepdims=True)
    d = v - mu
    var = jnp.mean(d * d, axis=-1, keepdims=True)
    vn = d * lax.rsqrt(var + EPS) * vg_ref[...] + vb_ref[...]
    vn_ref[...] = vn
    vnb = vn.astype(BF16)
    c_a = vnb.shape[1] // G_A
    rb = wmix_ref.shape[1]
    mixed = jnp.concatenate(
        [jnp.concatenate(
            [jnp.dot(wmix_ref[g], vnb[r0:r0 + rb, g * c_a:(g + 1) * c_a],
                     preferred_element_type=F32) for g in range(G_A)], axis=1)
         for r0 in range(0, vnb.shape[0], rb)], axis=0) + bias_ref[...]
    u = jnp.dot(xn, wu_ref[...], preferred_element_type=F32)
    ya = (u * mixed).astype(BF16)
    ga = jnp.dot(xn, wga_ref[...], preferred_element_type=F32)
    apart_ref[...] = _sigmoid(ga) * jnp.dot(ya, wao_ref[...], preferred_element_type=F32)


def _branch_a(x2d, tm, g1, wu, wv, wga, vg, vb, wmix, bias_full, wao):
    n, d = x2d.shape
    d_a = wu.shape[1]
    tok = lambda w: pl.BlockSpec((tm, w), lambda i: (i, 0))
    return pl.pallas_call(
        _branch_a_kernel,
        grid=(n // tm,),
        in_specs=[tok(d), _full_spec(g1.shape), _full_spec(wu.shape), _full_spec(wv.shape),
                  _full_spec(wga.shape), _full_spec(vg.shape), _full_spec(vb.shape),
                  _full_spec(wmix.shape), _full_spec(bias_full.shape), _full_spec(wao.shape)],
        out_specs=[tok(d), tok(d_a)],
        out_shape=[jax.ShapeDtypeStruct((n, d), F32), jax.ShapeDtypeStruct((n, d_a), F32)],
        compiler_params=pltpu.CompilerParams(dimension_semantics=("parallel",),
                                             vmem_limit_bytes=VMEM_LIMIT),
        name="branch_a",
    )(x2d, g1, wu, wv, wga, vg, vb, wmix, bias_full, wao)


def _branch_b_kernel(x_ref, g1_ref, wcur_ref, wgb_ref, mu_ref, ext_ref, w0_ref, w2_ref,
                     a0_ref, a2_ref, g2_ref, kk_ref, ka_ref, rk_ref, bd_ref,
                     r_out, k_out, v_out, kk_out, kka_out, lw_out, g_out, bonus_out,
                     sgb_out, cur_out, carry_scr, *, seq_rows, d_b):
    tm = x_ref.shape[0]
    xn = _rms(x_ref[...], g1_ref[...]).astype(BF16)
    cur = jnp.dot(xn, wcur_ref[...], preferred_element_type=F32)
    rolled = pltpu.roll(cur, 1, axis=0)
    row = lax.broadcasted_iota(jnp.int32, (tm, 1), 0)
    if seq_rows is None:
        first_tile = pl.program_id(1) == 0
        carry = jnp.where(first_tile, 0.0, carry_scr[...])
        prev = jnp.where(row == 0, carry, rolled)
        carry_scr[...] = cur[tm - 1:tm, :]
        cur_out[...] = cur[tm - 8:tm, :]
    else:
        prev = jnp.where(row % seq_rows == 0, ext_ref[...], rolled)
        cur_out[...] = cur
    xs = cur + (prev - cur) * mu_ref[...]
    r = xs[:, 0:d_b]
    k = xs[:, d_b:2 * d_b]
    v = xs[:, 2 * d_b:3 * d_b]
    o = 3 * d_b
    xw = xs[:, o:o + R_W]
    xa = xs[:, o + R_W:o + R_W + R_A]
    xg = xs[:, o + R_W + R_A:o + R_W + R_A + R_G]
    wl = w0_ref[...] + _dot3_stacked(jnp.tanh(xw), w2_ref[...])
    lw = -jnp.exp(-_softplus(-wl) - 0.5)
    a = _sigmoid(a0_ref[...] + _dot3_stacked(xa, a2_ref[...]))
    g = _dot3_stacked(_sigmoid(xg), g2_ref[...])
    bd = bd_ref[...]
    kkr = k * kk_ref[...]
    n2 = _head_sums(kkr * kkr, bd)
    kk = kkr / jnp.maximum(jnp.sqrt(n2), 1e-12)
    k2 = k * (1.0 + (a - 1.0) * ka_ref[...])
    bonus = _head_sums(r * k2 * rk_ref[...], bd) * v
    r_out[...] = r
    k_out[...] = k2
    v_out[...] = v
    kk_out[...] = kk
    kka_out[...] = kk * a
    lw_out[...] = lw
    g_out[...] = g
    bonus_out[...] = bonus
    sgb_out[...] = _sigmoid(jnp.dot(xn, wgb_ref[...], preferred_element_type=F32))


def _branch_b(x3d, tm, seq_rows, ext, g1, wcur, wgb, mu, w0, w2, a0, a2, g2, k_k, k_a, r_k, bd):
    nb, t, d = x3d.shape
    d_b = w0.shape[1]
    shift_w = wcur.shape[1]
    nt = t // tm
    tok = lambda w: pl.BlockSpec((None, tm, w), lambda b, i: (b, i, 0))
    cur_rows = 8 if seq_rows is None else tm
    outs = [jax.ShapeDtypeStruct((nb, t, d_b), F32)] * 8 + [
        jax.ShapeDtypeStruct((nb, t, d), F32),
        jax.ShapeDtypeStruct((nb, nt * cur_rows, shift_w), F32)]
    out_specs = [tok(d_b)] * 8 + [tok(d), pl.BlockSpec((None, cur_rows, shift_w),
                                                       lambda b, i: (b, i, 0))]
    weights = (g1, wcur, wgb, mu)
    small = (w0, w2, a0, a2, g2, k_k, k_a, r_k, bd)
    return pl.pallas_call(
        functools.partial(_branch_b_kernel, seq_rows=seq_rows, d_b=d_b),
        grid=(nb, nt),
        in_specs=[tok(d)] + [_full_spec(w.shape) for w in weights]
        + [tok(shift_w) if seq_rows is not None else _full_spec(ext.shape)]
        + [_full_spec(w.shape) for w in small],
        out_specs=out_specs,
        out_shape=outs,
        scratch_shapes=[pltpu.VMEM((1, shift_w), F32)],
        compiler_params=pltpu.CompilerParams(dimension_semantics=("parallel", "arbitrary"),
                                             vmem_limit_bytes=VMEM_LIMIT),
        name="branch_b",
    )(x3d, *weights, ext, *small)


def _scan_kernel(r_ref, k_ref, v_ref, kk_ref, kka_ref, lw_ref, s0_ref, y_ref, sout_ref, s_scr,
                 *, chunk):
    L = chunk
    W2 = 2 * L
    c = pl.program_id(1)

    @pl.when(c == 0)
    def _():
        s_scr[...] = s0_ref[...]

    n_seq, n_pairs = s_scr.shape[:2]
    iota = lambda shape, dim: lax.broadcasted_iota(jnp.int32, shape, dim)
    stack_mask = (iota((W2, 1), 0) >= L) == (iota((1, PAIR), 1) >= HEAD_B)
    bd_mask = (iota((W2, 1), 0) >= L) == (iota((1, W2), 1) >= L)
    assert L & (L - 1) == 0, "chunk length must be a power of two"
    rw = iota((L, W2), 0)
    cw = iota((L, W2), 1) & (L - 1)
    strict_w = cw < rw
    eye_w = (cw == rw).astype(F32)
    incl_w2 = (iota((L, 2 * W2), 1) & (L - 1)) <= iota((L, 2 * W2), 0)
    tri = (iota((L, L), 1) <= iota((L, L), 0)).astype(BF16)
    di = iota((PAIR, PAIR), 0)
    dj = iota((PAIR, PAIR), 1)
    diag = di == dj
    head_diag = (di >= HEAD_B) == (dj >= HEAD_B)

    def stack(x):
        return jnp.where(stack_mask, jnp.concatenate([x, x], axis=0), 0.0)

    def bd(xw):
        return jnp.where(bd_mask, jnp.concatenate([xw, xw], axis=0), 0.0)

    n_double = max(L.bit_length() - 2, 0)
    prep = []
    for s in range(n_seq):
        lw_all = lw_ref[s]
        cum_all = _dot_exact_lhs(tri, lw_all)
        for p in range(n_pairs):
            sl = slice(p * PAIR, (p + 1) * PAIR)
            cum = cum_all[:, sl]
            cum_last = cum[L - 1:L, :]
            e_neg = jnp.exp(-cum)
            e_rel = jnp.exp(cum_last - cum)
            kk = kk_ref[s, :, sl]
            kka = kka_ref[s, :, sl]
            kx = k_ref[s, :, sl]
            prep.append(dict(
                s=s, p=p, sl=sl, ab=-kk * jnp.exp(cum - lw_all[:, sl]),
                rb=r_ref[s, :, sl] * jnp.exp(cum), bb=kka * e_neg, kb=kx * e_neg,
                bt=kka * e_rel, kt=kx * e_rel, v=v_ref[s, :, sl], decay=jnp.exp(cum_last)))
    a_w = [lax.dot_general(
        jnp.concatenate([q["ab"], q["rb"]], axis=0).astype(BF16),
        jnp.concatenate([stack(q["bb"]), stack(q["kb"])], axis=0).astype(BF16),
        (((1,), (1,)), ((), ())), preferred_element_type=F32) for q in prep]
    nw = [jnp.where(strict_w, a[:L, :W2], 0.0) for a in a_w]
    tw = [eye_w + n for n in nw]
    nbd = [bd(n) for n in nw]
    for _ in range(n_double):
        nw = [_dot(n, b) for n, b in zip(nw, nbd)]
        nbd = [bd(n) for n in nw]
        tw = [t + _dot(t, b) for t, b in zip(tw, nbd)]
    akv = [_dot(jnp.where(strict_w, a[:L, W2:], 0.0), stack(q["v"])) for a, q in zip(a_w, prep)]
    tx = [_dot(t, jnp.concatenate([stack(q["ab"]), stack(kv)], axis=1))
          for t, q, kv in zip(tw, prep, akv)]
    ry = [_dot(jnp.where(incl_w2, a[L:], 0.0),
               jnp.concatenate(
                   [jnp.concatenate([stack(x[:, :PAIR]), stack(x[:, PAIR:])], axis=1),
                    jnp.concatenate([jnp.zeros((W2, PAIR), F32), stack(q["v"])], axis=1)], axis=0))
          for a, x, q in zip(a_w, tx, prep)]
    for q, x, y in zip(prep, tx, ry):
        st = s_scr[q["s"], q["p"]]
        us = _dot(jnp.concatenate([x[:, :PAIR], q["rb"] + y[:, :PAIR]], axis=0), st)
        y_ref[q["s"], :, q["sl"]] = us[L:] + y[:, PAIR:]
        u = us[:L] + x[:, PAIR:]
        lhs = jnp.concatenate([jnp.where(diag, q["decay"], 0.0),
                               jnp.concatenate([q["bt"], q["kt"]], axis=0).T], axis=1)
        g = _dot(lhs, jnp.concatenate([st, u, q["v"]], axis=0))
        s_scr[q["s"], q["p"]] = jnp.where(head_diag, g, 0.0)

    @pl.when(c == pl.num_programs(1) - 1)
    def _():
        sout_ref[...] = s_scr[...]


def _scan(r, k, v, kk, kka, lw, s0, chunk, n_seq):
    nb, t, d_b = r.shape
    n_pairs = d_b // PAIR
    tok = pl.BlockSpec((n_seq, chunk, d_b), lambda b, c: (b, c, 0))
    st = pl.BlockSpec((n_seq, n_pairs, PAIR, PAIR), lambda b, c: (b, 0, 0, 0))
    return pl.pallas_call(
        functools.partial(_scan_kernel, chunk=chunk),
        grid=(nb // n_seq, t // chunk),
        in_specs=[tok] * 6 + [st],
        out_specs=[tok, st],
        out_shape=[jax.ShapeDtypeStruct((nb, t, d_b), F32),
                   jax.ShapeDtypeStruct((nb, n_pairs, PAIR, PAIR), F32)],
        scratch_shapes=[pltpu.VMEM((n_seq, n_pairs, PAIR, PAIR), F32)],
        compiler_params=pltpu.CompilerParams(dimension_semantics=("parallel", "arbitrary"),
                                             vmem_limit_bytes=VMEM_LIMIT),
        name="rwkv_scan",
    )(r, k, v, kk, kka, lw, s0)


def _state_to_blockdiag(wkv):
    b, h, n, _ = wkv.shape
    st = jnp.swapaxes(wkv, -1, -2).reshape(b, h // 2, 2, n, n)
    z = jnp.zeros_like(st[:, :, 0])
    top = jnp.concatenate([st[:, :, 0], z], axis=-1)
    bot = jnp.concatenate([z, st[:, :, 1]], axis=-1)
    return jnp.concatenate([top, bot], axis=-2)


def _blockdiag_to_state(s):
    b, hp, _, _ = s.shape
    n = HEAD_B
    blocks = jnp.stack([s[:, :, :n, :n], s[:, :, n:, n:]], axis=2)
    return jnp.swapaxes(blocks, -1, -2).reshape(b, hp * 2, n, n)


def _pack_bf16_pairs(x):
    w = x.shape[1] // 2
    bits = lambda v: lax.bitcast_convert_type(v.astype(BF16).astype(F32), jnp.uint32)
    return (bits(x[:, :w]) >> 16) | (bits(x[:, w:]) & jnp.uint32(0xFFFF0000))


def _unpack_bf16_pairs(u):
    lo = lax.bitcast_convert_type(u << 16, F32)
    hi = lax.bitcast_convert_type(u & jnp.uint32(0xFFFF0000), F32)
    return lo, hi


def _merge_kernel(y_ref, g_ref, bonus_ref, sgb_ref, apart_ref, x_ref, lg_ref, lb_ref, bd_ref,
                  wbo_ref, wo_ref, g2_ref, rw_ref, rb_ref,
                  h_out, xp_out, topi_out, prob_out, rank_out, cnt_out, cnt_scr):
    step = pl.program_id(0)

    @pl.when(step == 0)
    def _():
        cnt_scr[...] = jnp.zeros_like(cnt_scr)

    bd = bd_ref[...]
    y = y_ref[...]
    inv_n = 1.0 / HEAD_B
    mu = _head_sums(y, bd) * inv_n
    d = y - mu
    var = _head_sums(d * d, bd) * inv_n
    yn = d * lax.rsqrt(var + GN_EPS) * lg_ref[...] + lb_ref[...]
    yb = ((yn + bonus_ref[...]) * g_ref[...]).astype(BF16)
    merged = apart_ref[...] + sgb_ref[...] * jnp.dot(yb, wbo_ref[...], preferred_element_type=F32)
    h = x_ref[...] + jnp.dot(merged.astype(BF16), wo_ref[...], preferred_element_type=F32)
    h_out[...] = h
    xn2 = _rms(h, g2_ref[...])
    xp_out[...] = _pack_bf16_pairs(xn2)
    n_e = rb_ref.shape[1]
    xh, xl = _split(xn2)
    rw = rw_ref[...]
    t = jnp.dot(xh, rw, preferred_element_type=F32)
    logits = (t[:, :n_e] + t[:, n_e:] + jnp.dot(xl, rw[:, :n_e], preferred_element_type=F32)
              + rb_ref[...])
    tm = logits.shape[0]
    idx = lax.broadcasted_iota(jnp.int32, logits.shape, 1).astype(F32)
    work = logits
    tops, hots, sels = [], [], []
    for _ in range(TOP_K):
        m = jnp.max(work, axis=-1, keepdims=True)
        sel = jnp.min(jnp.where(work == m, idx, float(n_e)), axis=-1, keepdims=True)
        hot = idx == sel
        tops.append(m)
        hots.append(hot)
        sels.append(sel)
        work = jnp.where(hot, -jnp.inf, work)
    es = [jnp.exp(t - tops[0]) for t in tops]
    denom = es[0] + es[1] + es[2] + es[3]
    topi_out[...] = jnp.concatenate(sels, axis=1).astype(jnp.int32)
    prob_out[...] = jnp.concatenate([e / denom for e in es], axis=1)
    hot_any = jnp.zeros_like(logits)
    for hot in hots:
        hot_any = hot_any + hot.astype(F32)
    ri = lax.broadcasted_iota(jnp.int32, (tm, tm), 0)
    ci = lax.broadcasted_iota(jnp.int32, (tm, tm), 1)
    before = _dot((ci < ri).astype(BF16), hot_any) + cnt_scr[...]
    rank_out[...] = jnp.concatenate(
        [jnp.sum(jnp.where(hot, before, 0.0), axis=-1, keepdims=True) for hot in hots],
        axis=1).astype(jnp.int32)
    total = cnt_scr[...] + jnp.sum(hot_any, axis=0, keepdims=True)
    cnt_scr[...] = total
    cnt_out[...] = total.astype(jnp.int32)


def _merge(y, g, bonus, sgb, apart, x2d, tm, lnx_g, lnx_b, bd, wbo, wo, g2n, rw, rb):
    n, d = x2d.shape
    d_b = y.shape[1]
    n_e = rb.shape[1]
    tok = lambda w: pl.BlockSpec((tm, w), lambda i: (i, 0))
    weights = (lnx_g, lnx_b, bd, wbo, wo, g2n, rw, rb)
    return pl.pallas_call(
        _merge_kernel,
        grid=(n // tm,),
        in_specs=[tok(d_b), tok(d_b), tok(d_b), tok(d), tok(d), tok(d)]
        + [_full_spec(w.shape) for w in weights],
        out_specs=[tok(d), tok(d // 2), tok(TOP_K), tok(TOP_K), tok(TOP_K),
                   pl.BlockSpec((1, n_e), lambda i: (0, 0))],
        out_shape=[jax.ShapeDtypeStruct((n, d), F32),
                   jax.ShapeDtypeStruct((n, d // 2), jnp.uint32),
                   jax.ShapeDtypeStruct((n, TOP_K), jnp.int32),
                   jax.ShapeDtypeStruct((n, TOP_K), F32),
                   jax.ShapeDtypeStruct((n, TOP_K), jnp.int32),
                   jax.ShapeDtypeStruct((1, n_e), jnp.int32)],
        scratch_shapes=[pltpu.VMEM((1, n_e), F32)],
        compiler_params=pltpu.CompilerParams(dimension_semantics=("arbitrary",),
                                             vmem_limit_bytes=VMEM_LIMIT),
        name="merge_router",
    )(y, g, bonus, sgb, apart, x2d, *weights)


MOE_TM = 1024
MOE_SUB = 256
MOE_FF_CHUNK = 512
CAST_ROWS = 256


def _route(topi_p, rank_p, cnt_p, topi_s, rank_s, cnt_s):
    n_e = cnt_p.shape[-1]
    cnt_p, cnt_s = cnt_p.reshape(n_e), cnt_s.reshape(n_e)
    cnt = cnt_p + cnt_s
    padded = (cnt + MOE_TM - 1) // MOE_TM * MOE_TM
    ends = jnp.cumsum(padded)
    base = ends - padded
    lookup = lambda table, idx: jnp.sum(
        jnp.where(idx[..., None] == jnp.arange(n_e), table, 0), axis=-1)
    pos_p = lookup(base, topi_p) + rank_p
    pos_s = lookup(base + cnt_p, topi_s) + rank_s
    return (pos_p.astype(jnp.int32), pos_s.astype(jnp.int32),
            (base // MOE_TM).astype(jnp.int32), cnt.astype(jnp.int32))


def _moe_kernel(tile0_ref, cnt_ref, xs_hbm, wgu_ref, bgu_ref, wd_ref, bdn_ref, ys_hbm,
                wgu_b, wd_b, xbuf, ybuf, sem_in, sem_out):
    e = pl.program_id(0)
    tile0 = tile0_ref[e]
    cnt = cnt_ref[e]
    n_t = (cnt + MOE_TM - 1) // MOE_TM
    d, gu = wgu_ref.shape
    d_ff = wd_ref.shape[0]
    half = d // 2

    def in_copy(tile, slot):
        return pltpu.make_async_copy(xs_hbm.at[pl.ds(tile * MOE_TM, MOE_TM)], xbuf.at[slot],
                                     sem_in.at[slot])

    def out_copy(tile, slot):
        return pltpu.make_async_copy(ybuf.at[slot], ys_hbm.at[pl.ds(tile * MOE_TM, MOE_TM)],
                                     sem_out.at[slot])

    @pl.when(n_t > 0)
    def _():
        in_copy(tile0, 0).start()
        for r0 in range(0, d, CAST_ROWS):
            wgu_b[r0:r0 + CAST_ROWS, :] = wgu_ref[r0:r0 + CAST_ROWS, :].astype(BF16)
        for r0 in range(0, d_ff, CAST_ROWS):
            wd_b[r0:r0 + CAST_ROWS, :] = wd_ref[r0:r0 + CAST_ROWS, :].astype(BF16)

    def tile_step(j, carry):
        slot = j % 2
        in_copy(tile0 + j, slot).wait()

        @pl.when(j + 1 < n_t)
        def _():
            in_copy(tile0 + j + 1, 1 - slot).start()

        @pl.when(j >= 2)
        def _():
            out_copy(tile0 + j - 2, slot).wait()

        left = cnt - j * MOE_TM

        def ffn(row0, rows):
            lo, hi = _unpack_bf16_pairs(xbuf[slot, pl.ds(row0, rows)])
            valid = lax.broadcasted_iota(jnp.int32, lo.shape, 0) < left - row0
            x_lo = jnp.where(valid, lo, 0.0).astype(BF16)
            x_hi = jnp.where(valid, hi, 0.0).astype(BF16)

            def proj(c0):
                cols = slice(c0, c0 + MOE_FF_CHUNK)
                return (jnp.dot(x_lo, wgu_b[:half, cols], preferred_element_type=F32)
                        + jnp.dot(x_hi, wgu_b[half:, cols], preferred_element_type=F32)
                        + bgu_ref[:, cols])

            y = jnp.zeros((rows, d), F32)
            for f in range(d_ff // MOE_FF_CHUNK):
                c0 = f * MOE_FF_CHUNK
                gate = jnp.minimum(proj(c0), SWIGLU_LIMIT)
                up = jnp.clip(proj(d_ff + c0), -SWIGLU_LIMIT, SWIGLU_LIMIT)
                hh = (up + 1.0) * gate * _sigmoid(gate * SWIGLU_ALPHA)
                y = y + jnp.dot(hh.astype(BF16), wd_b[c0:c0 + MOE_FF_CHUNK, :],
                                preferred_element_type=F32)
            ybuf[slot, pl.ds(row0, rows)] = _pack_bf16_pairs(y + bdn_ref[...])

        n_sub = (jnp.minimum(left, MOE_TM) + MOE_SUB - 1) // MOE_SUB

        @pl.when(n_sub == MOE_TM // MOE_SUB)
        def _():
            ffn(0, MOE_TM)

        @pl.when(n_sub < MOE_TM // MOE_SUB)
        def _():
            def sub_step(i, c):
                ffn(pl.multiple_of(i * MOE_SUB, MOE_SUB), MOE_SUB)
                return c
            lax.fori_loop(0, n_sub, sub_step, 0)

        out_copy(tile0 + j, slot).start()
        return carry

    lax.fori_loop(0, n_t, tile_step, 0)

    @pl.when(n_t >= 2)
    def _():
        out_copy(tile0 + n_t - 2, n_t % 2).wait()

    @pl.when(n_t >= 1)
    def _():
        out_copy(tile0 + n_t - 1, (n_t - 1) % 2).wait()


def _moe(xs, tile0, cnt, wgu, bgu, wd, bdn):
    p_rows, half = xs.shape
    n_e, d, gu = wgu.shape
    d_ff = wd.shape[1]
    w_blk = lambda e, tile0, cnt: (e, 0, 0)
    grid_spec = pltpu.PrefetchScalarGridSpec(
        num_scalar_prefetch=2,
        grid=(n_e,),
        in_specs=[pl.BlockSpec(memory_space=pl.ANY),
                  pl.BlockSpec((None, d, gu), w_blk),
                  pl.BlockSpec((None, 1, gu), w_blk),
                  pl.BlockSpec((None, d_ff, d), w_blk),
                  pl.BlockSpec((None, 1, d), w_blk)],
        out_specs=pl.BlockSpec(memory_space=pl.ANY),
        scratch_shapes=[pltpu.VMEM((d, gu), BF16), pltpu.VMEM((d_ff, d), BF16),
                        pltpu.VMEM((2, MOE_TM, half), jnp.uint32),
                        pltpu.VMEM((2, MOE_TM, half), jnp.uint32),
                        pltpu.SemaphoreType.DMA((2,)), pltpu.SemaphoreType.DMA((2,))])
    return pl.pallas_call(
        _moe_kernel,
        grid_spec=grid_spec,
        out_shape=jax.ShapeDtypeStruct((p_rows, half), jnp.uint32),
        compiler_params=pltpu.CompilerParams(dimension_semantics=("arbitrary",),
                                             vmem_limit_bytes=VMEM_LIMIT),
        name="moe_experts",
    )(tile0, cnt, xs, wgu, bgu, wd, bdn)


def _combine_kernel(h_ref, yg_ref, prob_ref, gf_ref, out_ref):
    half = h_ref.shape[1] // 2
    prob = prob_ref[...]
    acc_lo = jnp.zeros((h_ref.shape[0], half), F32)
    acc_hi = jnp.zeros((h_ref.shape[0], half), F32)
    for k in range(TOP_K):
        lo, hi = _unpack_bf16_pairs(yg_ref[k])
        pk = prob[:, k:k + 1]
        acc_lo = acc_lo + pk * lo
        acc_hi = acc_hi + pk * hi
    z = h_ref[...] + jnp.concatenate([acc_lo, acc_hi], axis=1)
    out_ref[...] = _rms(z, gf_ref[...])


def _combine(h, yg, row0, prob, gf, tm):
    n, d = h.shape
    tok = lambda w: pl.BlockSpec((tm, w), lambda i: (i, 0))
    t0 = row0 // tm
    return pl.pallas_call(
        _combine_kernel,
        grid=(n // tm,),
        in_specs=[tok(d), pl.BlockSpec((TOP_K, tm, d // 2), lambda i: (0, i + t0, 0)),
                  tok(TOP_K), _full_spec(gf.shape)],
        out_specs=tok(d),
        out_shape=jax.ShapeDtypeStruct((n, d), F32),
        compiler_params=pltpu.CompilerParams(dimension_semantics=("parallel",),
                                             vmem_limit_bytes=VMEM_LIMIT),
        name="moe_combine",
    )(h, yg, prob, gf)


SC_CORES = 2
SC_SUBCORES = 16
SC_WORKERS = SC_CORES * SC_SUBCORES
SC_MAX_INDEX = 128


def _sc_chunk(rows_per_worker):
    for c in range(SC_MAX_INDEX, 7, -8):
        if rows_per_worker % c == 0:
            return c
    raise ValueError(f"no 8-aligned chunk divides {rows_per_worker} rows")


def _sc_mesh():
    return plsc.VectorSubcoreMesh(core_axis_name="c", subcore_axis_name="s")


def _sc_worker():
    return lax.axis_index("s") * SC_CORES + lax.axis_index("c")


def _scatter_rows(x, pos_t, p_rows):
    n, w = x.shape
    n_k = pos_t.shape[0]
    per = n // SC_WORKERS
    chunk = _sc_chunk(per)
    pos_flat = pos_t.reshape(n_k * n)

    def body(x_hbm, pos_hbm, xs_hbm, idx_v, rows_v, sem):
        base = _sc_worker() * per

        @pl.loop(0, per // chunk)
        def _(c):
            off = pl.multiple_of(base + c * chunk, 8)
            pltpu.sync_copy(x_hbm.at[pl.ds(off, chunk)], rows_v)
            for k in range(n_k):
                pltpu.sync_copy(pos_hbm.at[pl.ds(pl.multiple_of(k * n + off, 8), chunk)], idx_v)
                pltpu.async_copy(rows_v, xs_hbm.at[idx_v], sem).wait()

    return pl.kernel(
        body, out_type=jax.ShapeDtypeStruct((p_rows, w), x.dtype), mesh=_sc_mesh(),
        scratch_types=[pltpu.VMEM((chunk,), jnp.int32), pltpu.VMEM((chunk, w), x.dtype),
                       pltpu.SemaphoreType.DMA],
        name="sc_scatter_rows")(x, pos_flat)


def _gather_rows(table, idx):
    n = idx.shape[0]
    w = table.shape[1]
    per = n // SC_WORKERS
    chunk = _sc_chunk(per)

    def body(table_hbm, idx_hbm, out_hbm, idx_v, rows_v, sem):
        base = _sc_worker() * per

        @pl.loop(0, per // chunk)
        def _(c):
            off = pl.multiple_of(base + c * chunk, 8)
            pltpu.sync_copy(idx_hbm.at[pl.ds(off, chunk)], idx_v)
            pltpu.async_copy(table_hbm.at[idx_v], rows_v, sem).wait()
            pltpu.sync_copy(rows_v, out_hbm.at[pl.ds(off, chunk)])

    return pl.kernel(
        body, out_type=jax.ShapeDtypeStruct((n, w), table.dtype), mesh=_sc_mesh(),
        scratch_types=[pltpu.VMEM((chunk,), jnp.int32), pltpu.VMEM((chunk, w), table.dtype),
                       pltpu.SemaphoreType.DMA],
        name="sc_gather_rows")(table, idx)


def _mix_matrix(ws, block, seq_rows):
    causal = jnp.tril(jnp.ones((CHUNK, CHUNK), dtype=bool))
    w = jnp.where(causal[None], ws, 0.0)[:, :seq_rows, :seq_rows]
    eye = jnp.eye(block // seq_rows, dtype=ws.dtype)
    return jnp.einsum("ab,gts->gatbs", eye, w).reshape(ws.shape[0], block, block).astype(BF16)


def _stream(x, shift_in, wkv_in, p, tm, seq_rows, scan_chunk, scan_seqs):
    nb, t, d = x.shape
    n = nb * t
    x2d = x.reshape(n, d)
    rows = CHUNK if seq_rows is None else seq_rows
    wmix = _mix_matrix(p["w_spatial"], max(rows, LANES), rows)
    pos = jnp.arange(tm) % rows
    d_a = p["wu"].shape[1]
    bias_full = jnp.repeat(p["b_spatial"].T[pos], d_a // G_A, axis=1)
    apart, vn = _branch_a(x2d, tm, p["norm1_g"], p["wu"], p["wv"], p["wga"], p["vnorm_g"],
                          p["vnorm_b"], wmix, bias_full, p["w_a_out"])
    if seq_rows is None:
        xb, ext = x, jnp.zeros((1, p["wcur"].shape[1]), F32)
    else:
        xb = x2d.reshape(n // tm, tm, d)
        ext = jnp.repeat(shift_in, seq_rows, axis=0).reshape(n // tm, tm, -1)
    outs = _branch_b(xb, tm, seq_rows, ext, p["norm1_g"], p["wcur"], p["wgb"], p["mu_shift"],
                     p["w0"], p["w2"], p["a0"], p["a2"], p["g2"], p["k_k"], p["k_a"], p["r_k"],
                     p["bd"])
    r, k2, v, kk, kka, lw, g, bonus, sgb, cur = outs
    d_b = r.shape[-1]
    if seq_rows is None:
        shift_out = cur[:, -1, :]
        scan_in = [a.reshape(nb, t, d_b) for a in (r, k2, v, kk, kka, lw)]
    else:
        shift_out = cur.reshape(nb, t, -1)[:, -1, :]
        pad = scan_chunk - t
        scan_in = [jnp.pad(a.reshape(nb, t, d_b), ((0, 0), (0, pad), (0, 0)))
                   for a in (r, k2, v, kk, kka, lw)]
    y, s_out = _scan(*scan_in, _state_to_blockdiag(wkv_in), scan_chunk, scan_seqs)
    y2d = y[:, :t].reshape(n, d_b)
    flat = lambda a: a.reshape(n, a.shape[-1])
    routed = _merge(y2d, flat(g), flat(bonus), flat(sgb), apart, x2d, tm, p["lnx_g"],
                    p["lnx_b"], p["bd"], p["w_b_out"], p["w_out"], p["norm2_g"],
                    p["router_w"], p["router_b"])
    return routed, vn, shift_out, _blockdiag_to_state(s_out)


def kernel(x_prompt, x_sample, state_shift, state_wkv, norm1_g, w_in, mu_shift, vnorm_g, vnorm_b, w_spatial, b_spatial, w_a_out, w0, w2, a0, a2, g2, k_k, k_a, r_k, lnx_g, lnx_b, w_b_out, w_out, norm2_g, router_w, router_b, exp_w_gu, exp_b_gu, exp_w_down, exp_b_down, normf_g):
    depth = w_in.shape[0]
    assert depth == 1, "the final norm is fused into the single layer's MoE call"
    d_model = x_prompt.shape[-1]
    d_b = w0.shape[-1]
    shift_w = mu_shift.shape[-1]
    d_a = vnorm_g.shape[-1]
    bp, tp, _ = x_prompt.shape
    bs, ts, _ = x_sample.shape
    head_id = jnp.arange(2 * LANES) // HEAD_B
    bd = (head_id[:, None] == head_id[None, :]).astype(BF16)
    row = lambda a: a.reshape(1, -1)

    hp, hs = x_prompt, x_sample
    shift_p = jnp.zeros((depth, bp, shift_w), state_shift.dtype)
    wkv_p = jnp.zeros((depth, bp) + state_wkv.shape[2:], state_wkv.dtype)
    vrows, shifts_p, wkvs_p, shifts_s, wkvs_s = [], [], [], [], []
    for l in range(depth):
        wi = w_in[l].astype(BF16)
        o = shift_w
        p = dict(
            norm1_g=row(norm1_g[l]), wcur=wi[:, :o], wu=wi[:, o:o + d_a],
            wv=wi[:, o + d_a:o + 2 * d_a], wga=wi[:, o + 2 * d_a:o + 2 * d_a + d_model],
            wgb=wi[:, o + 2 * d_a + d_model:], mu_shift=row(mu_shift[l]),
            vnorm_g=row(vnorm_g[l]), vnorm_b=row(vnorm_b[l]), w_spatial=w_spatial[l],
            b_spatial=b_spatial[l], w_a_out=w_a_out[l].astype(BF16), w0=row(w0[l]),
            w2=_stack_rhs3(w2[l]), a0=row(a0[l]), a2=_stack_rhs3(a2[l]), g2=_stack_rhs3(g2[l]),
            k_k=row(k_k[l]), k_a=row(k_a[l]),
            r_k=row(r_k[l]), lnx_g=row(lnx_g[l]), lnx_b=row(lnx_b[l]),
            w_b_out=w_b_out[l].astype(BF16), w_out=w_out[l].astype(BF16),
            norm2_g=row(norm2_g[l]), router_w=jnp.concatenate(_split(router_w[l]), axis=1),
            router_b=row(router_b[l]), bd=bd)
        routed_p, _, sh_p, s_p = _stream(hp, shift_p[l], wkv_p[l], p, 512, None, 64, 4)
        routed_s, vn_s, sh_s, s_s = _stream(hs, state_shift[l], state_wkv[l], p, 512, ts, 8, 8)
        h_p, xp_p, topi_p, prob_p, rank_p, cnt_p = routed_p
        h_s, xp_s, topi_s, prob_s, rank_s, cnt_s = routed_s
        n_p, n_s = h_p.shape[0], h_s.shape[0]
        n_e = router_w.shape[-1]
        n_tiles = (n_p + n_s) * TOP_K // MOE_TM + n_e
        pos_p, pos_s, tile0, cnt = _route(topi_p, rank_p, cnt_p, topi_s, rank_s, cnt_s)
        pos_t = jnp.concatenate([pos_p, pos_s], axis=0).T
        xs = _scatter_rows(jnp.concatenate([xp_p, xp_s], axis=0), pos_t, n_tiles * MOE_TM)
        ys = _moe(xs, tile0, cnt, exp_w_gu[l], exp_b_gu[l][:, None, :],
                  exp_w_down[l], exp_b_down[l][:, None, :])
        yg = _gather_rows(ys, pos_t.reshape(-1)).reshape(TOP_K, n_p + n_s, d_model // 2)
        gf = row(normf_g)
        hp = _combine(h_p, yg, 0, prob_p, gf, 512).reshape(bp, tp, d_model)
        hs = _combine(h_s, yg, n_p, prob_s, gf, 512).reshape(bs, ts, d_model)
        vrows.append(vn_s.reshape(bs, ts, d_a))
        shifts_p.append(sh_p)
        wkvs_p.append(s_p)
        shifts_s.append(sh_s)
        wkvs_s.append(s_s)
    return (hp, hs, jnp.stack(shifts_p), jnp.stack(wkvs_p), jnp.stack(shifts_s),
            jnp.stack(wkvs_s), jnp.stack(vrows))
```

```python
import functools

import jax
import jax.numpy as jnp
from jax import lax
from jax.experimental import pallas as pl
from jax.experimental.pallas import tpu as pltpu
from jax.experimental.pallas import tpu_sc as plsc

F32 = jnp.float32
BF16 = jnp.bfloat16

CHUNK = 128
G_A = 8
HEAD_B = 64
R_W, R_A, R_G = 64, 64, 128
TOP_K = 4
SWIGLU_LIMIT = 7.0
SWIGLU_ALPHA = 1.702
EPS = 1e-5
GN_EPS = HEAD_B * 1e-5

LANES = 128
PAIR = 2 * HEAD_B
VMEM_LIMIT = 56 * 1024 * 1024


def _dot(a, b):
    return jnp.dot(a.astype(BF16), b.astype(BF16), preferred_element_type=F32)


def _split(x):
    hi = x.astype(BF16)
    lo = (x - hi.astype(F32)).astype(BF16)
    return hi, lo


def _dot3(a, b):
    ah, al = _split(a)
    bh, bl = _split(b)
    return (jnp.dot(ah, bh, preferred_element_type=F32)
            + jnp.dot(ah, bl, preferred_element_type=F32)
            + jnp.dot(al, bh, preferred_element_type=F32))


def _dot_exact_rhs(a, b_bf16):
    ah, al = _split(a)
    return (jnp.dot(ah, b_bf16, preferred_element_type=F32)
            + jnp.dot(al, b_bf16, preferred_element_type=F32))


def _stack_rhs3(b):
    bh, bl = _split(b)
    return jnp.concatenate([bh, bh, bl], axis=0)


def _dot3_stacked(a, b_stacked):
    ah, al = _split(a)
    return jnp.dot(jnp.concatenate([ah, al, ah], axis=1), b_stacked, preferred_element_type=F32)


def _head_sums(x, ones_bd):
    w = ones_bd.shape[0]
    return jnp.concatenate([_dot_exact_rhs(x[:, c:c + w], ones_bd)
                            for c in range(0, x.shape[1], w)], axis=1)


def _dot_exact_lhs(a_bf16, b):
    bh, bl = _split(b)
    bm = b - bh.astype(F32) - bl.astype(F32)
    return (jnp.dot(a_bf16, bh, preferred_element_type=F32)
            + jnp.dot(a_bf16, bl, preferred_element_type=F32)
            + jnp.dot(a_bf16, bm.astype(BF16), preferred_element_type=F32))


def _rms(x, g):
    return x * lax.rsqrt(jnp.mean(x * x, axis=-1, keepdims=True) + EPS) * g


def _sigmoid(x):
    return 1.0 / (1.0 + jnp.exp(-x))


def _softplus(z):
    return jnp.maximum(z, 0.0) + jnp.log(1.0 + jnp.exp(-jnp.abs(z)))


def _full_spec(shape):
    nd = len(shape)
    return pl.BlockSpec(shape, lambda *_: (0,) * nd, pipeline_mode=pl.Buffered(1))


def _branch_a_kernel(x_ref, g1_ref, wu_ref, wv_ref, wga_ref, vg_ref, vb_ref, wmix_ref,
                     bias_ref, wao_ref, apart_ref, vn_ref):
    xn = _rms(x_ref[...], g1_ref[...]).astype(BF16)
    v = jnp.dot(xn, wv_ref[...], preferred_element_type=F32)
    mu = jnp.mean(v, axis=-1, keepdims=True)
    d = v - mu
    var = jnp.mean(d * d, axis=-1, keepdims=True)
    vn = d * lax.rsqrt(var + EPS) * vg_ref[...] + vb_ref[...]
    vn_ref[...] = vn
    vnb = vn.astype(BF16)
    c_a = vnb.shape[1] // G_A
    rb = wmix_ref.shape[1]
    mixed = jnp.concatenate(
        [jnp.concatenate(
            [jnp.dot(wmix_ref[g], vnb[r0:r0 + rb, g * c_a:(g + 1) * c_a],
                     preferred_element_type=F32) for g in range(G_A)], axis=1)
         for r0 in range(0, vnb.shape[0], rb)], axis=0) + bias_ref[...]
    u = jnp.dot(xn, wu_ref[...], preferred_element_type=F32)
    ya = (u * mixed).astype(BF16)
    ga = jnp.dot(xn, wga_ref[...], preferred_element_type=F32)
    apart_ref[...] = _sigmoid(ga) * jnp.dot(ya, wao_ref[...], preferred_element_type=F32)


def _branch_a(x2d, tm, g1, wu, wv, wga, vg, vb, wmix, bias_full, wao):
    n, d = x2d.shape
    d_a = wu.shape[1]
    tok = lambda w: pl.BlockSpec((tm, w), lambda i: (i, 0))
    return pl.pallas_call(
        _branch_a_kernel,
        grid=(n // tm,),
        in_specs=[tok(d), _full_spec(g1.shape), _full_spec(wu.shape), _full_spec(wv.shape),
                  _full_spec(wga.shape), _full_spec(vg.shape), _full_spec(vb.shape),
                  _full_spec(wmix.shape), _full_spec(bias_full.shape), _full_spec(wao.shape)],
        out_specs=[tok(d), tok(d_a)],
        out_shape=[jax.ShapeDtypeStruct((n, d), F32), jax.ShapeDtypeStruct((n, d_a), F32)],
        compiler_params=pltpu.CompilerParams(dimension_semantics=("parallel",),
                                             vmem_limit_bytes=VMEM_LIMIT),
        name="branch_a",
    )(x2d, g1, wu, wv, wga, vg, vb, wmix, bias_full, wao)


def _branch_b_kernel(x_ref, g1_ref, wcur_ref, wgb_ref, mu_ref, ext_ref, w0_ref, w2_ref,
                     a0_ref, a2_ref, g2_ref, kk_ref, ka_ref, rk_ref, bd_ref,
                     r_out, k_out, v_out, kk_out, kka_out, lw_out, g_out, bonus_out,
                     sgb_out, cur_out, carry_scr, *, seq_rows, d_b):
    tm = x_ref.shape[0]
    xn = _rms(x_ref[...], g1_ref[...]).astype(BF16)
    cur = jnp.dot(xn, wcur_ref[...], preferred_element_type=F32)
    rolled = pltpu.roll(cur, 1, axis=0)
    row = lax.broadcasted_iota(jnp.int32, (tm, 1), 0)
    if seq_rows is None:
        first_tile = pl.program_id(1) == 0
        carry = jnp.where(first_tile, 0.0, carry_scr[...])
        prev = jnp.where(row == 0, carry, rolled)
        carry_scr[...] = cur[tm - 1:tm, :]
        cur_out[...] = cur[tm - 8:tm, :]
    else:
        prev = jnp.where(row % seq_rows == 0, ext_ref[...], rolled)
        cur_out[...] = cur
    xs = cur + (prev - cur) * mu_ref[...]
    r = xs[:, 0:d_b]
    k = xs[:, d_b:2 * d_b]
    v = xs[:, 2 * d_b:3 * d_b]
    o = 3 * d_b
    xw = xs[:, o:o + R_W]
    xa = xs[:, o + R_W:o + R_W + R_A]
    xg = xs[:, o + R_W + R_A:o + R_W + R_A + R_G]
    wl = w0_ref[...] + _dot3_stacked(jnp.tanh(xw), w2_ref[...])
    lw = -jnp.exp(-_softplus(-wl) - 0.5)
    a = _sigmoid(a0_ref[...] + _dot3_stacked(xa, a2_ref[...]))
    g = _dot3_stacked(_sigmoid(xg), g2_ref[...])
    bd = bd_ref[...]
    kkr = k * kk_ref[...]
    n2 = _head_sums(kkr * kkr, bd)
    kk = kkr / jnp.maximum(jnp.sqrt(n2), 1e-12)
    k2 = k * (1.0 + (a - 1.0) * ka_ref[...])
    bonus = _head_sums(r * k2 * rk_ref[...], bd) * v
    r_out[...] = r
    k_out[...] = k2
    v_out[...] = v
    kk_out[...] = kk
    kka_out[...] = kk * a
    lw_out[...] = lw
    g_out[...] = g
    bonus_out[...] = bonus
    sgb_out[...] = _sigmoid(jnp.dot(xn, wgb_ref[...], preferred_element_type=F32))


def _branch_b(x3d, tm, seq_rows, ext, g1, wcur, wgb, mu, w0, w2, a0, a2, g2, k_k, k_a, r_k, bd):
    nb, t, d = x3d.shape
    d_b = w0.shape[1]
    shift_w = wcur.shape[1]
    nt = t // tm
    tok = lambda w: pl.BlockSpec((None, tm, w), lambda b, i: (b, i, 0))
    cur_rows = 8 if seq_rows is None else tm
    outs = [jax.ShapeDtypeStruct((nb, t, d_b), F32)] * 8 + [
        jax.ShapeDtypeStruct((nb, t, d), F32),
        jax.ShapeDtypeStruct((nb, nt * cur_rows, shift_w), F32)]
    out_specs = [tok(d_b)] * 8 + [tok(d), pl.BlockSpec((None, cur_rows, shift_w),
                                                       lambda b, i: (b, i, 0))]
    weights = (g1, wcur, wgb, mu)
    small = (w0, w2, a0, a2, g2, k_k, k_a, r_k, bd)
    return pl.pallas_call(
        functools.partial(_branch_b_kernel, seq_rows=seq_rows, d_b=d_b),
        grid=(nb, nt),
        in_specs=[tok(d)] + [_full_spec(w.shape) for w in weights]
        + [tok(shift_w) if seq_rows is not None else _full_spec(ext.shape)]
        + [_full_spec(w.shape) for w in small],
        out_specs=out_specs,
        out_shape=outs,
        scratch_shapes=[pltpu.VMEM((1, shift_w), F32)],
        compiler_params=pltpu.CompilerParams(dimension_semantics=("parallel", "arbitrary"),
                                             vmem_limit_bytes=VMEM_LIMIT),
        name="branch_b",
    )(x3d, *weights, ext, *small)


def _scan_kernel(r_ref, k_ref, v_ref, kk_ref, kka_ref, lw_ref, s0_ref, y_ref, sout_ref, s_scr,
                 *, chunk):
    L = chunk
    W2 = 2 * L
    c = pl.program_id(1)

    @pl.when(c == 0)
    def _():
        s_scr[...] = s0_ref[...]

    n_seq, n_pairs = s_scr.shape[:2]
    iota = lambda shape, dim: lax.broadcasted_iota(jnp.int32, shape, dim)
    stack_mask = (iota((W2, 1), 0) >= L) == (iota((1, PAIR), 1) >= HEAD_B)
    bd_mask = (iota((W2, 1), 0) >= L) == (iota((1, W2), 1) >= L)
    assert L & (L - 1) == 0, "chunk length must be a power of two"
    rw = iota((L, W2), 0)
    cw = iota((L, W2), 1) & (L - 1)
    strict_w = cw < rw
    eye_w = (cw == rw).astype(F32)
    incl_w2 = (iota((L, 2 * W2), 1) & (L - 1)) <= iota((L, 2 * W2), 0)
    tri = (iota((L, L), 1) <= iota((L, L), 0)).astype(BF16)
    di = iota((PAIR, PAIR), 0)
    dj = iota((PAIR, PAIR), 1)
    diag = di == dj
    head_diag = (di >= HEAD_B) == (dj >= HEAD_B)

    def stack(x):
        return jnp.where(stack_mask, jnp.concatenate([x, x], axis=0), 0.0)

    def bd(xw):
        return jnp.where(bd_mask, jnp.concatenate([xw, xw], axis=0), 0.0)

    n_double = max(L.bit_length() - 2, 0)
    prep = []
    for s in range(n_seq):
        lw_all = lw_ref[s]
        cum_all = _dot_exact_lhs(tri, lw_all)
        for p in range(n_pairs):
            sl = slice(p * PAIR, (p + 1) * PAIR)
            cum = cum_all[:, sl]
            cum_last = cum[L - 1:L, :]
            e_neg = jnp.exp(-cum)
            e_rel = jnp.exp(cum_last - cum)
            kk = kk_ref[s, :, sl]
            kka = kka_ref[s, :, sl]
            kx = k_ref[s, :, sl]
            prep.append(dict(
                s=s, p=p, sl=sl, ab=-kk * jnp.exp(cum - lw_all[:, sl]),
                rb=r_ref[s, :, sl] * jnp.exp(cum), bb=kka * e_neg, kb=kx * e_neg,
                bt=kka * e_rel, kt=kx * e_rel, v=v_ref[s, :, sl], decay=jnp.exp(cum_last)))
    a_w = [lax.dot_general(
        jnp.concatenate([q["ab"], q["rb"]], axis=0).astype(BF16),
        jnp.concatenate([stack(q["bb"]), stack(q["kb"])], axis=0).astype(BF16),
        (((1,), (1,)), ((), ())), preferred_element_type=F32) for q in prep]
    nw = [jnp.where(strict_w, a[:L, :W2], 0.0) for a in a_w]
    tw = [eye_w + n for n in nw]
    nbd = [bd(n) for n in nw]
    for _ in range(n_double):
        nw = [_dot(n, b) for n, b in zip(nw, nbd)]
        nbd = [bd(n) for n in nw]
        tw = [t + _dot(t, b) for t, b in zip(tw, nbd)]
    akv = [_dot(jnp.where(strict_w, a[:L, W2:], 0.0), stack(q["v"])) for a, q in zip(a_w, prep)]
    tx = [_dot(t, jnp.concatenate([stack(q["ab"]), stack(kv)], axis=1))
          for t, q, kv in zip(tw, prep, akv)]
    ry = [_dot(jnp.where(incl_w2, a[L:], 0.0),
               jnp.concatenate(
                   [jnp.concatenate([stack(x[:, :PAIR]), stack(x[:, PAIR:])], axis=1),
                    jnp.concatenate([jnp.zeros((W2, PAIR), F32), stack(q["v"])], axis=1)], axis=0))
          for a, x, q in zip(a_w, tx, prep)]
    for q, x, y in zip(prep, tx, ry):
        st = s_scr[q["s"], q["p"]]
        us = _dot(jnp.concatenate([x[:, :PAIR], q["rb"] + y[:, :PAIR]], axis=0), st)
        y_ref[q["s"], :, q["sl"]] = us[L:] + y[:, PAIR:]
        u = us[:L] + x[:, PAIR:]
        lhs = jnp.concatenate([jnp.where(diag, q["decay"], 0.0),
                               jnp.concatenate([q["bt"], q["kt"]], axis=0).T], axis=1)
        g = _dot(lhs, jnp.concatenate([st, u, q["v"]], axis=0))
        s_scr[q["s"], q["p"]] = jnp.where(head_diag, g, 0.0)

    @pl.when(c == pl.num_programs(1) - 1)
    def _():
        sout_ref[...] = s_scr[...]


def _scan(r, k, v, kk, kka, lw, s0, chunk, n_seq):
    nb, t, d_b = r.shape
    n_pairs = d_b // PAIR
    tok = pl.BlockSpec((n_seq, chunk, d_b), lambda b, c: (b, c, 0))
    st = pl.BlockSpec((n_seq, n_pairs, PAIR, PAIR), lambda b, c: (b, 0, 0, 0))
    return pl.pallas_call(
        functools.partial(_scan_kernel, chunk=chunk),
        grid=(nb // n_seq, t // chunk),
        in_specs=[tok] * 6 + [st],
        out_specs=[tok, st],
        out_shape=[jax.ShapeDtypeStruct((nb, t, d_b), F32),
                   jax.ShapeDtypeStruct((nb, n_pairs, PAIR, PAIR), F32)],
        scratch_shapes=[pltpu.VMEM((n_seq, n_pairs, PAIR, PAIR), F32)],
        compiler_params=pltpu.CompilerParams(dimension_semantics=("parallel", "arbitrary"),
                                             vmem_limit_bytes=VMEM_LIMIT),
        name="rwkv_scan",
    )(r, k, v, kk, kka, lw, s0)


def _state_to_blockdiag(wkv):
    b, h, n, _ = wkv.shape
    st = jnp.swapaxes(wkv, -1, -2).reshape(b, h // 2, 2, n, n)
    z = jnp.zeros_like(st[:, :, 0])
    top = jnp.concatenate([st[:, :, 0], z], axis=-1)
    bot = jnp.concatenate([z, st[:, :, 1]], axis=-1)
    return jnp.concatenate([top, bot], axis=-2)


def _blockdiag_to_state(s):
    b, hp, _, _ = s.shape
    n = HEAD_B
    blocks = jnp.stack([s[:, :, :n, :n], s[:, :, n:, n:]], axis=2)
    return jnp.swapaxes(blocks, -1, -2).reshape(b, hp * 2, n, n)


def _pack_bf16_pairs(x):
    w = x.shape[1] // 2
    bits = lambda v: lax.bitcast_convert_type(v.astype(BF16).astype(F32), jnp.uint32)
    return (bits(x[:, :w]) >> 16) | (bits(x[:, w:]) & jnp.uint32(0xFFFF0000))


def _unpack_bf16_pairs(u):
    lo = lax.bitcast_convert_type(u << 16, F32)
    hi = lax.bitcast_convert_type(u & jnp.uint32(0xFFFF0000), F32)
    return lo, hi


def _merge_kernel(y_ref, g_ref, bonus_ref, sgb_ref, apart_ref, x_ref, lg_ref, lb_ref, bd_ref,
                  wbo_ref, wo_ref, g2_ref, rw_ref, rb_ref,
                  h_out, xp_out, topi_out, prob_out, rank_out, cnt_out, cnt_scr):
    step = pl.program_id(0)

    @pl.when(step == 0)
    def _():
        cnt_scr[...] = jnp.zeros_like(cnt_scr)

    bd = bd_ref[...]
    y = y_ref[...]
    inv_n = 1.0 / HEAD_B
    mu = _head_sums(y, bd) * inv_n
    d = y - mu
    var = _head_sums(d * d, bd) * inv_n
    yn = d * lax.rsqrt(var + GN_EPS) * lg_ref[...] + lb_ref[...]
    yb = ((yn + bonus_ref[...]) * g_ref[...]).astype(BF16)
    merged = apart_ref[...] + sgb_ref[...] * jnp.dot(yb, wbo_ref[...], preferred_element_type=F32)
    h = x_ref[...] + jnp.dot(merged.astype(BF16), wo_ref[...], preferred_element_type=F32)
    h_out[...] = h
    xn2 = _rms(h, g2_ref[...])
    xp_out[...] = _pack_bf16_pairs(xn2)
    n_e = rb_ref.shape[1]
    xh, xl = _split(xn2)
    rw = rw_ref[...]
    t = jnp.dot(xh, rw, preferred_element_type=F32)
    logits = (t[:, :n_e] + t[:, n_e:] + jnp.dot(xl, rw[:, :n_e], preferred_element_type=F32)
              + rb_ref[...])
    tm = logits.shape[0]
    idx = lax.broadcasted_iota(jnp.int32, logits.shape, 1).astype(F32)
    work = logits
    tops, hots, sels = [], [], []
    for _ in range(TOP_K):
        m = jnp.max(work, axis=-1, keepdims=True)
        sel = jnp.min(jnp.where(work == m, idx, float(n_e)), axis=-1, keepdims=True)
        hot = idx == sel
        tops.append(m)
        hots.append(hot)
        sels.append(sel)
        work = jnp.where(hot, -jnp.inf, work)
    es = [jnp.exp(t - tops[0]) for t in tops]
    denom = es[0] + es[1] + es[2] + es[3]
    topi_out[...] = jnp.concatenate(sels, axis=1).astype(jnp.int32)
    prob_out[...] = jnp.concatenate([e / denom for e in es], axis=1)
    hot_any = jnp.zeros_like(logits)
    for hot in hots:
        hot_any = hot_any + hot.astype(F32)
    ri = lax.broadcasted_iota(jnp.int32, (tm, tm), 0)
    ci = lax.broadcasted_iota(jnp.int32, (tm, tm), 1)
    before = _dot((ci < ri).astype(BF16), hot_any) + cnt_scr[...]
    rank_out[...] = jnp.concatenate(
        [jnp.sum(jnp.where(hot, before, 0.0), axis=-1, keepdims=True) for hot in hots],
        axis=1).astype(jnp.int32)
    total = cnt_scr[...] + jnp.sum(hot_any, axis=0, keepdims=True)
    cnt_scr[...] = total
    cnt_out[...] = total.astype(jnp.int32)


def _merge(y, g, bonus, sgb, apart, x2d, tm, lnx_g, lnx_b, bd, wbo, wo, g2n, rw, rb):
    n, d = x2d.shape
    d_b = y.shape[1]
    n_e = rb.shape[1]
    tok = lambda w: pl.BlockSpec((tm, w), lambda i: (i, 0))
    weights = (lnx_g, lnx_b, bd, wbo, wo, g2n, rw, rb)
    return pl.pallas_call(
        _merge_kernel,
        grid=(n // tm,),
        in_specs=[tok(d_b), tok(d_b), tok(d_b), tok(d), tok(d), tok(d)]
        + [_full_spec(w.shape) for w in weights],
        out_specs=[tok(d), tok(d // 2), tok(TOP_K), tok(TOP_K), tok(TOP_K),
                   pl.BlockSpec((1, n_e), lambda i: (0, 0))],
        out_shape=[jax.ShapeDtypeStruct((n, d), F32),
                   jax.ShapeDtypeStruct((n, d // 2), jnp.uint32),
                   jax.ShapeDtypeStruct((n, TOP_K), jnp.int32),
                   jax.ShapeDtypeStruct((n, TOP_K), F32),
                   jax.ShapeDtypeStruct((n, TOP_K), jnp.int32),
                   jax.ShapeDtypeStruct((1, n_e), jnp.int32)],
        scratch_shapes=[pltpu.VMEM((1, n_e), F32)],
        compiler_params=pltpu.CompilerParams(dimension_semantics=("arbitrary",),
                                             vmem_limit_bytes=VMEM_LIMIT),
        name="merge_router",
    )(y, g, bonus, sgb, apart, x2d, *weights)


MOE_TM = 1024
MOE_SUB = 256
MOE_FF_CHUNK = 512
CAST_ROWS = 256


def _route(topi_p, rank_p, cnt_p, topi_s, rank_s, cnt_s):
    n_e = cnt_p.shape[-1]
    cnt_p, cnt_s = cnt_p.reshape(n_e), cnt_s.reshape(n_e)
    cnt = cnt_p + cnt_s
    padded = (cnt + MOE_TM - 1) // MOE_TM * MOE_TM
    ends = jnp.cumsum(padded)
    base = ends - padded
    lookup = lambda table, idx: jnp.sum(
        jnp.where(idx[..., None] == jnp.arange(n_e), table, 0), axis=-1)
    pos_p = lookup(base, topi_p) + rank_p
    pos_s = lookup(base + cnt_p, topi_s) + rank_s
    return (pos_p.astype(jnp.int32), pos_s.astype(jnp.int32),
            (base // MOE_TM).astype(jnp.int32), cnt.astype(jnp.int32))


def _moe_kernel(tile0_ref, cnt_ref, xs_hbm, wgu_ref, bgu_ref, wd_ref, bdn_ref, ys_hbm,
                wgu_b, wd_b, xbuf, ybuf, sem_in, sem_out):
    e = pl.program_id(0)
    tile0 = tile0_ref[e]
    cnt = cnt_ref[e]
    n_t = (cnt + MOE_TM - 1) // MOE_TM
    d, gu = wgu_ref.shape
    d_ff = wd_ref.shape[0]
    half = d // 2

    def in_copy(tile, slot):
        return pltpu.make_async_copy(xs_hbm.at[pl.ds(tile * MOE_TM, MOE_TM)], xbuf.at[slot],
                                     sem_in.at[slot])

    def out_copy(tile, slot):
        return pltpu.make_async_copy(ybuf.at[slot], ys_hbm.at[pl.ds(tile * MOE_TM, MOE_TM)],
                                     sem_out.at[slot])

    @pl.when(n_t > 0)
    def _():
        in_copy(tile0, 0).start(priority=1)
        for r0 in range(0, d, CAST_ROWS):
            wgu_b[r0:r0 + CAST_ROWS, :] = wgu_ref[r0:r0 + CAST_ROWS, :].astype(BF16)
        for r0 in range(0, d_ff, CAST_ROWS):
            wd_b[r0:r0 + CAST_ROWS, :] = wd_ref[r0:r0 + CAST_ROWS, :].astype(BF16)

    def tile_step(j, carry):
        slot = j % 2
        in_copy(tile0 + j, slot).wait()

        @pl.when(j + 1 < n_t)
        def _():
            in_copy(tile0 + j + 1, 1 - slot).start(priority=1)

        @pl.when(j >= 2)
        def _():
            out_copy(tile0 + j - 2, slot).wait()

        left = cnt - j * MOE_TM

        def ffn(row0, rows):
            lo, hi = _unpack_bf16_pairs(xbuf[slot, pl.ds(row0, rows)])
            valid = lax.broadcasted_iota(jnp.int32, lo.shape, 0) < left - row0
            x_lo = jnp.where(valid, lo, 0.0).astype(BF16)
            x_hi = jnp.where(valid, hi, 0.0).astype(BF16)

            def proj(c0):
                cols = slice(c0, c0 + MOE_FF_CHUNK)
                return (jnp.dot(x_lo, wgu_b[:half, cols], preferred_element_type=F32)
                        + jnp.dot(x_hi, wgu_b[half:, cols], preferred_element_type=F32)
                        + bgu_ref[:, cols])

            y = jnp.zeros((rows, d), F32)
            for f in range(d_ff // MOE_FF_CHUNK):
                c0 = f * MOE_FF_CHUNK
                gate = jnp.minimum(proj(c0), SWIGLU_LIMIT)
                up = jnp.clip(proj(d_ff + c0), -SWIGLU_LIMIT, SWIGLU_LIMIT)
                hh = (up + 1.0) * gate * _sigmoid(gate * SWIGLU_ALPHA)
                y = y + jnp.dot(hh.astype(BF16), wd_b[c0:c0 + MOE_FF_CHUNK, :],
                                preferred_element_type=F32)
            ybuf[slot, pl.ds(row0, rows)] = _pack_bf16_pairs(y + bdn_ref[...])

        n_sub = (jnp.minimum(left, MOE_TM) + MOE_SUB - 1) // MOE_SUB

        @pl.when(n_sub == MOE_TM // MOE_SUB)
        def _():
            ffn(0, MOE_TM)

        @pl.when(n_sub < MOE_TM // MOE_SUB)
        def _():
            def sub_step(i, c):
                ffn(pl.multiple_of(i * MOE_SUB, MOE_SUB), MOE_SUB)
                return c
            lax.fori_loop(0, n_sub, sub_step, 0)

        out_copy(tile0 + j, slot).start()
        return carry

    lax.fori_loop(0, n_t, tile_step, 0)

    @pl.when(n_t >= 2)
    def _():
        out_copy(tile0 + n_t - 2, n_t % 2).wait()

    @pl.when(n_t >= 1)
    def _():
        out_copy(tile0 + n_t - 1, (n_t - 1) % 2).wait()


def _moe(xs, tile0, cnt, wgu, bgu, wd, bdn):
    p_rows, half = xs.shape
    n_e, d, gu = wgu.shape
    d_ff = wd.shape[1]
    w_blk = lambda e, tile0, cnt: (e, 0, 0)
    grid_spec = pltpu.PrefetchScalarGridSpec(
        num_scalar_prefetch=2,
        grid=(n_e,),
        in_specs=[pl.BlockSpec(memory_space=pl.ANY),
                  pl.BlockSpec((None, d, gu), w_blk),
                  pl.BlockSpec((None, 1, gu), w_blk),
                  pl.BlockSpec((None, d_ff, d), w_blk),
                  pl.BlockSpec((None, 1, d), w_blk)],
        out_specs=pl.BlockSpec(memory_space=pl.ANY),
        scratch_shapes=[pltpu.VMEM((d, gu), BF16), pltpu.VMEM((d_ff, d), BF16),
                        pltpu.VMEM((2, MOE_TM, half), jnp.uint32),
                        pltpu.VMEM((2, MOE_TM, half), jnp.uint32),
                        pltpu.SemaphoreType.DMA((2,)), pltpu.SemaphoreType.DMA((2,))])
    return pl.pallas_call(
        _moe_kernel,
        grid_spec=grid_spec,
        out_shape=jax.ShapeDtypeStruct((p_rows, half), jnp.uint32),
        compiler_params=pltpu.CompilerParams(dimension_semantics=("arbitrary",),
                                             vmem_limit_bytes=VMEM_LIMIT),
        name="moe_experts",
    )(tile0, cnt, xs, wgu, bgu, wd, bdn)


def _combine_kernel(h_ref, yg_ref, prob_ref, gf_ref, out_ref):
    half = h_ref.shape[1] // 2
    prob = prob_ref[...]
    acc_lo = jnp.zeros((h_ref.shape[0], half), F32)
    acc_hi = jnp.zeros((h_ref.shape[0], half), F32)
    for k in range(TOP_K):
        lo, hi = _unpack_bf16_pairs(yg_ref[k])
        pk = prob[:, k:k + 1]
        acc_lo = acc_lo + pk * lo
        acc_hi = acc_hi + pk * hi
    z = h_ref[...] + jnp.concatenate([acc_lo, acc_hi], axis=1)
    out_ref[...] = _rms(z, gf_ref[...])


def _combine(h, yg, row0, prob, gf, tm):
    n, d = h.shape
    tok = lambda w: pl.BlockSpec((tm, w), lambda i: (i, 0))
    t0 = row0 // tm
    return pl.pallas_call(
        _combine_kernel,
        grid=(n // tm,),
        in_specs=[tok(d), pl.BlockSpec((TOP_K, tm, d // 2), lambda i: (0, i + t0, 0)),
                  tok(TOP_K), _full_spec(gf.shape)],
        out_specs=tok(d),
        out_shape=jax.ShapeDtypeStruct((n, d), F32),
        compiler_params=pltpu.CompilerParams(dimension_semantics=("parallel",),
                                             vmem_limit_bytes=VMEM_LIMIT),
        name="moe_combine",
    )(h, yg, prob, gf)


SC_CORES = 2
SC_SUBCORES = 16
SC_WORKERS = SC_CORES * SC_SUBCORES
SC_MAX_INDEX = 128


def _sc_chunk(rows_per_worker):
    for c in range(SC_MAX_INDEX, 7, -8):
        if rows_per_worker % c == 0:
            return c
    raise ValueError(f"no 8-aligned chunk divides {rows_per_worker} rows")


def _sc_mesh():
    return plsc.VectorSubcoreMesh(core_axis_name="c", subcore_axis_name="s")


def _sc_worker():
    return lax.axis_index("s") * SC_CORES + lax.axis_index("c")


def _scatter_rows(xs_in, pos_t, p_rows):
    w = xs_in[0].shape[1]
    dtype = xs_in[0].dtype
    n_k, n = pos_t.shape
    pos_flat = pos_t.reshape(n_k * n)
    parts, row0 = [], 0
    for x in xs_in:
        per = x.shape[0] // SC_WORKERS
        parts.append((row0, per, _sc_chunk(per)))
        row0 += x.shape[0]
    max_chunk = max(c for _, _, c in parts)

    def body(*refs):
        x_hbms = refs[:len(xs_in)]
        pos_hbm, out_hbm, idx_v, rows_v, sem = refs[len(xs_in):]
        for x_hbm, (tok0, per, chunk) in zip(x_hbms, parts):
            base = _sc_worker() * per

            @pl.loop(0, per // chunk)
            def _(c):
                off = pl.multiple_of(base + c * chunk, 8)
                rows = rows_v.at[pl.ds(0, chunk)]
                idx = idx_v.at[pl.ds(0, chunk)]
                pltpu.sync_copy(x_hbm.at[pl.ds(off, chunk)], rows)
                for k in range(n_k):
                    src = pl.multiple_of(k * n + tok0 + off, 8)
                    pltpu.sync_copy(pos_hbm.at[pl.ds(src, chunk)], idx)
                    pltpu.async_copy(rows, out_hbm.at[idx], sem).wait()

    return pl.kernel(
        body, out_type=jax.ShapeDtypeStruct((p_rows, w), dtype), mesh=_sc_mesh(),
        scratch_types=[pltpu.VMEM((max_chunk,), jnp.int32), pltpu.VMEM((max_chunk, w), dtype),
                       pltpu.SemaphoreType.DMA],
        name="sc_scatter_rows")(*xs_in, pos_flat)


def _gather_rows(table, idx):
    n = idx.shape[0]
    w = table.shape[1]
    per = n // SC_WORKERS
    chunk = _sc_chunk(per)

    def body(table_hbm, idx_hbm, out_hbm, idx_v, rows_v, sem):
        base = _sc_worker() * per

        @pl.loop(0, per // chunk)
        def _(c):
            off = pl.multiple_of(base + c * chunk, 8)
            pltpu.sync_copy(idx_hbm.at[pl.ds(off, chunk)], idx_v)
            pltpu.async_copy(table_hbm.at[idx_v], rows_v, sem).wait()
            pltpu.sync_copy(rows_v, out_hbm.at[pl.ds(off, chunk)])

    return pl.kernel(
        body, out_type=jax.ShapeDtypeStruct((n, w), table.dtype), mesh=_sc_mesh(),
        scratch_types=[pltpu.VMEM((chunk,), jnp.int32), pltpu.VMEM((chunk, w), table.dtype),
                       pltpu.SemaphoreType.DMA],
        name="sc_gather_rows")(table, idx)


def _mix_matrix(ws, block, seq_rows):
    causal = jnp.tril(jnp.ones((CHUNK, CHUNK), dtype=bool))
    w = jnp.where(causal[None], ws, 0.0)[:, :seq_rows, :seq_rows]
    eye = jnp.eye(block // seq_rows, dtype=ws.dtype)
    return jnp.einsum("ab,gts->gatbs", eye, w).reshape(ws.shape[0], block, block).astype(BF16)


def _stream(x, shift_in, wkv_in, p, tm, seq_rows, scan_chunk, scan_seqs):
    nb, t, d = x.shape
    n = nb * t
    x2d = x.reshape(n, d)
    rows = CHUNK if seq_rows is None else seq_rows
    wmix = _mix_matrix(p["w_spatial"], max(rows, LANES), rows)
    pos = jnp.arange(tm) % rows
    d_a = p["wu"].shape[1]
    bias_full = jnp.repeat(p["b_spatial"].T[pos], d_a // G_A, axis=1)
    apart, vn = _branch_a(x2d, tm, p["norm1_g"], p["wu"], p["wv"], p["wga"], p["vnorm_g"],
                          p["vnorm_b"], wmix, bias_full, p["w_a_out"])
    if seq_rows is None:
        xb, ext = x, jnp.zeros((1, p["wcur"].shape[1]), F32)
    else:
        xb = x2d.reshape(n // tm, tm, d)
        ext = jnp.repeat(shift_in, seq_rows, axis=0).reshape(n // tm, tm, -1)
    outs = _branch_b(xb, tm, seq_rows, ext, p["norm1_g"], p["wcur"], p["wgb"], p["mu_shift"],
                     p["w0"], p["w2"], p["a0"], p["a2"], p["g2"], p["k_k"], p["k_a"], p["r_k"],
                     p["bd"])
    r, k2, v, kk, kka, lw, g, bonus, sgb, cur = outs
    d_b = r.shape[-1]
    if seq_rows is None:
        shift_out = cur[:, -1, :]
        scan_in = [a.reshape(nb, t, d_b) for a in (r, k2, v, kk, kka, lw)]
    else:
        shift_out = cur.reshape(nb, t, -1)[:, -1, :]
        pad = scan_chunk - t
        scan_in = [jnp.pad(a.reshape(nb, t, d_b), ((0, 0), (0, pad), (0, 0)))
                   for a in (r, k2, v, kk, kka, lw)]
    y, s_out = _scan(*scan_in, _state_to_blockdiag(wkv_in), scan_chunk, scan_seqs)
    y2d = y[:, :t].reshape(n, d_b)
    flat = lambda a: a.reshape(n, a.shape[-1])
    routed = _merge(y2d, flat(g), flat(bonus), flat(sgb), apart, x2d, tm, p["lnx_g"],
                    p["lnx_b"], p["bd"], p["w_b_out"], p["w_out"], p["norm2_g"],
                    p["router_w"], p["router_b"])
    return routed, vn, shift_out, _blockdiag_to_state(s_out)


def kernel(x_prompt, x_sample, state_shift, state_wkv, norm1_g, w_in, mu_shift, vnorm_g, vnorm_b, w_spatial, b_spatial, w_a_out, w0, w2, a0, a2, g2, k_k, k_a, r_k, lnx_g, lnx_b, w_b_out, w_out, norm2_g, router_w, router_b, exp_w_gu, exp_b_gu, exp_w_down, exp_b_down, normf_g):
    depth = w_in.shape[0]
    assert depth == 1, "the final norm is fused into the single layer's MoE call"
    d_model = x_prompt.shape[-1]
    d_b = w0.shape[-1]
    shift_w = mu_shift.shape[-1]
    d_a = vnorm_g.shape[-1]
    bp, tp, _ = x_prompt.shape
    bs, ts, _ = x_sample.shape
    head_id = jnp.arange(2 * LANES) // HEAD_B
    bd = (head_id[:, None] == head_id[None, :]).astype(BF16)
    row = lambda a: a.reshape(1, -1)

    hp, hs = x_prompt, x_sample
    shift_p = jnp.zeros((depth, bp, shift_w), state_shift.dtype)
    wkv_p = jnp.zeros((depth, bp) + state_wkv.shape[2:], state_wkv.dtype)
    vrows, shifts_p, wkvs_p, shifts_s, wkvs_s = [], [], [], [], []
    for l in range(depth):
        wi = w_in[l].astype(BF16)
        o = shift_w
        p = dict(
            norm1_g=row(norm1_g[l]), wcur=wi[:, :o], wu=wi[:, o:o + d_a],
            wv=wi[:, o + d_a:o + 2 * d_a], wga=wi[:, o + 2 * d_a:o + 2 * d_a + d_model],
            wgb=wi[:, o + 2 * d_a + d_model:], mu_shift=row(mu_shift[l]),
            vnorm_g=row(vnorm_g[l]), vnorm_b=row(vnorm_b[l]), w_spatial=w_spatial[l],
            b_spatial=b_spatial[l], w_a_out=w_a_out[l].astype(BF16), w0=row(w0[l]),
            w2=_stack_rhs3(w2[l]), a0=row(a0[l]), a2=_stack_rhs3(a2[l]), g2=_stack_rhs3(g2[l]),
            k_k=row(k_k[l]), k_a=row(k_a[l]),
            r_k=row(r_k[l]), lnx_g=row(lnx_g[l]), lnx_b=row(lnx_b[l]),
            w_b_out=w_b_out[l].astype(BF16), w_out=w_out[l].astype(BF16),
            norm2_g=row(norm2_g[l]), router_w=jnp.concatenate(_split(router_w[l]), axis=1),
            router_b=row(router_b[l]), bd=bd)
        routed_p, _, sh_p, s_p = _stream(hp, shift_p[l], wkv_p[l], p, 512, None, 64, 4)
        routed_s, vn_s, sh_s, s_s = _stream(hs, state_shift[l], state_wkv[l], p, 512, ts, 8, 8)
        h_p, xp_p, topi_p, prob_p, rank_p, cnt_p = routed_p
        h_s, xp_s, topi_s, prob_s, rank_s, cnt_s = routed_s
        n_p, n_s = h_p.shape[0], h_s.shape[0]
        n_e = router_w.shape[-1]
        n_tiles = (n_p + n_s) * TOP_K // MOE_TM + n_e
        pos_p, pos_s, tile0, cnt = _route(topi_p, rank_p, cnt_p, topi_s, rank_s, cnt_s)
        pos_t = jnp.concatenate([pos_p, pos_s], axis=0).T
        xs = _scatter_rows([xp_p, xp_s], pos_t, n_tiles * MOE_TM)
        ys = _moe(xs, tile0, cnt, exp_w_gu[l], exp_b_gu[l][:, None, :],
                  exp_w_down[l], exp_b_down[l][:, None, :])
        yg = _gather_rows(ys, pos_t.reshape(-1)).reshape(TOP_K, n_p + n_s, d_model // 2)
        gf = row(normf_g)
        hp = _combine(h_p, yg, 0, prob_p, gf, 512).reshape(bp, tp, d_model)
        hs = _combine(h_s, yg, n_p, prob_s, gf, 512).reshape(bs, ts, d_model)
        vrows.append(vn_s.reshape(bs, ts, d_a))
        shifts_p.append(sh_p)
        wkvs_p.append(s_p)
        shifts_s.append(sh_s)
        wkvs_s.append(s_s)
    return (hp, hs, jnp.stack(shifts_p), jnp.stack(wkvs_p), jnp.stack(shifts_s),
            jnp.stack(wkvs_s), jnp.stack(vrows))
```

```python
import functools

import jax
import jax.numpy as jnp
from jax import lax
from jax.experimental import pallas as pl
from jax.experimental.pallas import tpu as pltpu
from jax.experimental.pallas import tpu_sc as plsc

F32 = jnp.float32
BF16 = jnp.bfloat16

CHUNK = 128
G_A = 8
HEAD_B = 64
R_W, R_A, R_G = 64, 64, 128
TOP_K = 4
SWIGLU_LIMIT = 7.0
SWIGLU_ALPHA = 1.702
EPS = 1e-5
GN_EPS = HEAD_B * 1e-5

LANES = 128
PAIR = 2 * HEAD_B
VMEM_LIMIT = 56 * 1024 * 1024


def _dot(a, b):
    return jnp.dot(a.astype(BF16), b.astype(BF16), preferred_element_type=F32)


def _split(x):
    hi = x.astype(BF16)
    lo = (x - hi.astype(F32)).astype(BF16)
    return hi, lo


def _dot3(a, b):
    ah, al = _split(a)
    bh, bl = _split(b)
    return (jnp.dot(ah, bh, preferred_element_type=F32)
            + jnp.dot(ah, bl, preferred_element_type=F32)
            + jnp.dot(al, bh, preferred_element_type=F32))


def _dot_exact_rhs(a, b_bf16):
    ah, al = _split(a)
    return (jnp.dot(ah, b_bf16, preferred_element_type=F32)
            + jnp.dot(al, b_bf16, preferred_element_type=F32))


def _stack_rhs3(b):
    bh, bl = _split(b)
    return jnp.concatenate([bh, bh, bl], axis=0)


def _dot3_stacked(a, b_stacked):
    ah, al = _split(a)
    return jnp.dot(jnp.concatenate([ah, al, ah], axis=1), b_stacked, preferred_element_type=F32)


def _head_sums(x, ones_bd):
    w = ones_bd.shape[0]
    return jnp.concatenate([_dot_exact_rhs(x[:, c:c + w], ones_bd)
                            for c in range(0, x.shape[1], w)], axis=1)


def _dot_exact_lhs(a_bf16, b):
    bh, bl = _split(b)
    bm = b - bh.astype(F32) - bl.astype(F32)
    return (jnp.dot(a_bf16, bh, preferred_element_type=F32)
            + jnp.dot(a_bf16, bl, preferred_element_type=F32)
            + jnp.dot(a_bf16, bm.astype(BF16), preferred_element_type=F32))


def _rms(x, g):
    return x * lax.rsqrt(jnp.mean(x * x, axis=-1, keepdims=True) + EPS) * g


def _sigmoid(x):
    return 1.0 / (1.0 + jnp.exp(-x))


def _softplus(z):
    return jnp.maximum(z, 0.0) + jnp.log(1.0 + jnp.exp(-jnp.abs(z)))


def _full_spec(shape):
    nd = len(shape)
    return pl.BlockSpec(shape, lambda *_: (0,) * nd, pipeline_mode=pl.Buffered(1))


def _branch_a_kernel(x_ref, g1_ref, wu_ref, wv_ref, wga_ref, vg_ref, vb_ref, wmix_ref,
                     bias_ref, wao_ref, apart_ref, vn_ref):
    xn = _rms(x_ref[...], g1_ref[...]).astype(BF16)
    v = jnp.dot(xn, wv_ref[...], preferred_element_type=F32)
    mu = jnp.mean(v, axis=-1, keepdims=True)
    d = v - mu
    var = jnp.mean(d * d, axis=-1, keepdims=True)
    vn = d * lax.rsqrt(var + EPS) * vg_ref[...] + vb_ref[...]
    vn_ref[...] = vn
    vnb = vn.astype(BF16)
    c_a = vnb.shape[1] // G_A
    rb = wmix_ref.shape[1]
    mixed = jnp.concatenate(
        [jnp.concatenate(
            [jnp.dot(wmix_ref[g], vnb[r0:r0 + rb, g * c_a:(g + 1) * c_a],
                     preferred_element_type=F32) for g in range(G_A)], axis=1)
         for r0 in range(0, vnb.shape[0], rb)], axis=0) + bias_ref[...]
    u = jnp.dot(xn, wu_ref[...], preferred_element_type=F32)
    ya = (u * mixed).astype(BF16)
    ga = jnp.dot(xn, wga_ref[...], preferred_element_type=F32)
    apart_ref[...] = _sigmoid(ga) * jnp.dot(ya, wao_ref[...], preferred_element_type=F32)


def _branch_a(x2d, tm, g1, wu, wv, wga, vg, vb, wmix, bias_full, wao):
    n, d = x2d.shape
    d_a = wu.shape[1]
    tok = lambda w: pl.BlockSpec((tm, w), lambda i: (i, 0))
    return pl.pallas_call(
        _branch_a_kernel,
        grid=(n // tm,),
        in_specs=[tok(d), _full_spec(g1.shape), _full_spec(wu.shape), _full_spec(wv.shape),
                  _full_spec(wga.shape), _full_spec(vg.shape), _full_spec(vb.shape),
                  _full_spec(wmix.shape), _full_spec(bias_full.shape), _full_spec(wao.shape)],
        out_specs=[tok(d), tok(d_a)],
        out_shape=[jax.ShapeDtypeStruct((n, d), F32), jax.ShapeDtypeStruct((n, d_a), F32)],
        compiler_params=pltpu.CompilerParams(dimension_semantics=("parallel",),
                                             vmem_limit_bytes=VMEM_LIMIT),
        name="branch_a",
    )(x2d, g1, wu, wv, wga, vg, vb, wmix, bias_full, wao)


def _branch_b_kernel(x_ref, g1_ref, wcur_ref, wgb_ref, mu_ref, ext_ref, w0_ref, w2_ref,
                     a0_ref, a2_ref, g2_ref, kk_ref, ka_ref, rk_ref, bd_ref,
                     r_out, k_out, v_out, kk_out, kka_out, lw_out, g_out, bonus_out,
                     sgb_out, cur_out, carry_scr, *, seq_rows, d_b):
    tm = x_ref.shape[0]
    xn = _rms(x_ref[...], g1_ref[...]).astype(BF16)
    cur = jnp.dot(xn, wcur_ref[...], preferred_element_type=F32)
    rolled = pltpu.roll(cur, 1, axis=0)
    row = lax.broadcasted_iota(jnp.int32, (tm, 1), 0)
    if seq_rows is None:
        first_tile = pl.program_id(1) == 0
        carry = jnp.where(first_tile, 0.0, carry_scr[...])
        prev = jnp.where(row == 0, carry, rolled)
        carry_scr[...] = cur[tm - 1:tm, :]
        cur_out[...] = cur[tm - 8:tm, :]
    else:
        prev = jnp.where(row % seq_rows == 0, ext_ref[...], rolled)
        cur_out[...] = cur
    xs = cur + (prev - cur) * mu_ref[...]
    r = xs[:, 0:d_b]
    k = xs[:, d_b:2 * d_b]
    v = xs[:, 2 * d_b:3 * d_b]
    o = 3 * d_b
    xw = xs[:, o:o + R_W]
    xa = xs[:, o + R_W:o + R_W + R_A]
    xg = xs[:, o + R_W + R_A:o + R_W + R_A + R_G]
    wl = w0_ref[...] + _dot3_stacked(jnp.tanh(xw), w2_ref[...])
    lw = -jnp.exp(-_softplus(-wl) - 0.5)
    a = _sigmoid(a0_ref[...] + _dot3_stacked(xa, a2_ref[...]))
    g = _dot3_stacked(_sigmoid(xg), g2_ref[...])
    bd = bd_ref[...]
    kkr = k * kk_ref[...]
    n2 = _head_sums(kkr * kkr, bd)
    kk = kkr / jnp.maximum(jnp.sqrt(n2), 1e-12)
    k2 = k * (1.0 + (a - 1.0) * ka_ref[...])
    bonus = _head_sums(r * k2 * rk_ref[...], bd) * v
    r_out[...] = r
    k_out[...] = k2
    v_out[...] = v
    kk_out[...] = kk
    kka_out[...] = kk * a
    lw_out[...] = lw
    g_out[...] = g
    bonus_out[...] = bonus
    sgb_out[...] = _sigmoid(jnp.dot(xn, wgb_ref[...], preferred_element_type=F32))


def _branch_b(x3d, tm, seq_rows, ext, g1, wcur, wgb, mu, w0, w2, a0, a2, g2, k_k, k_a, r_k, bd):
    nb, t, d = x3d.shape
    d_b = w0.shape[1]
    shift_w = wcur.shape[1]
    nt = t // tm
    tok = lambda w: pl.BlockSpec((None, tm, w), lambda b, i: (b, i, 0))
    cur_rows = 8 if seq_rows is None else tm
    outs = [jax.ShapeDtypeStruct((nb, t, d_b), F32)] * 8 + [
        jax.ShapeDtypeStruct((nb, t, d), F32),
        jax.ShapeDtypeStruct((nb, nt * cur_rows, shift_w), F32)]
    out_specs = [tok(d_b)] * 8 + [tok(d), pl.BlockSpec((None, cur_rows, shift_w),
                                                       lambda b, i: (b, i, 0))]
    weights = (g1, wcur, wgb, mu)
    small = (w0, w2, a0, a2, g2, k_k, k_a, r_k, bd)
    return pl.pallas_call(
        functools.partial(_branch_b_kernel, seq_rows=seq_rows, d_b=d_b),
        grid=(nb, nt),
        in_specs=[tok(d)] + [_full_spec(w.shape) for w in weights]
        + [tok(shift_w) if seq_rows is not None else _full_spec(ext.shape)]
        + [_full_spec(w.shape) for w in small],
        out_specs=out_specs,
        out_shape=outs,
        scratch_shapes=[pltpu.VMEM((1, shift_w), F32)],
        compiler_params=pltpu.CompilerParams(dimension_semantics=("parallel", "arbitrary"),
                                             vmem_limit_bytes=VMEM_LIMIT),
        name="branch_b",
    )(x3d, *weights, ext, *small)


def _scan_kernel(r_ref, k_ref, v_ref, kk_ref, kka_ref, lw_ref, s0_ref, y_ref, sout_ref, s_scr,
                 *, chunk):
    L = chunk
    W2 = 2 * L
    c = pl.program_id(1)

    @pl.when(c == 0)
    def _():
        s_scr[...] = s0_ref[...]

    n_seq, n_pairs = s_scr.shape[:2]
    iota = lambda shape, dim: lax.broadcasted_iota(jnp.int32, shape, dim)
    stack_mask = (iota((W2, 1), 0) >= L) == (iota((1, PAIR), 1) >= HEAD_B)
    bd_mask = (iota((W2, 1), 0) >= L) == (iota((1, W2), 1) >= L)
    assert L & (L - 1) == 0, "chunk length must be a power of two"
    rw = iota((L, W2), 0)
    cw = iota((L, W2), 1) & (L - 1)
    strict_w = cw < rw
    eye_w = (cw == rw).astype(F32)
    incl_w2 = (iota((L, 2 * W2), 1) & (L - 1)) <= iota((L, 2 * W2), 0)
    tri = (iota((L, L), 1) <= iota((L, L), 0)).astype(BF16)
    di = iota((PAIR, PAIR), 0)
    dj = iota((PAIR, PAIR), 1)
    diag = di == dj
    head_diag = (di >= HEAD_B) == (dj >= HEAD_B)

    def stack(x):
        return jnp.where(stack_mask, jnp.concatenate([x, x], axis=0), 0.0)

    def bd(xw):
        return jnp.where(bd_mask, jnp.concatenate([xw, xw], axis=0), 0.0)

    n_double = max(L.bit_length() - 2, 0)
    prep = []
    for s in range(n_seq):
        lw_all = lw_ref[s]
        cum_all = _dot_exact_lhs(tri, lw_all)
        for p in range(n_pairs):
            sl = slice(p * PAIR, (p + 1) * PAIR)
            cum = cum_all[:, sl]
            cum_last = cum[L - 1:L, :]
            e_neg = jnp.exp(-cum)
            e_rel = jnp.exp(cum_last - cum)
            kk = kk_ref[s, :, sl]
            kka = kka_ref[s, :, sl]
            kx = k_ref[s, :, sl]
            prep.append(dict(
                s=s, p=p, sl=sl, ab=-kk * jnp.exp(cum - lw_all[:, sl]),
                rb=r_ref[s, :, sl] * jnp.exp(cum), bb=kka * e_neg, kb=kx * e_neg,
                bt=kka * e_rel, kt=kx * e_rel, v=v_ref[s, :, sl], decay=jnp.exp(cum_last)))
    a_w = [lax.dot_general(
        jnp.concatenate([q["ab"], q["rb"]], axis=0).astype(BF16),
        jnp.concatenate([stack(q["bb"]), stack(q["kb"])], axis=0).astype(BF16),
        (((1,), (1,)), ((), ())), preferred_element_type=F32) for q in prep]
    nw = [jnp.where(strict_w, a[:L, :W2], 0.0) for a in a_w]
    tw = [eye_w + n for n in nw]
    nbd = [bd(n) for n in nw]
    for _ in range(n_double):
        nw = [_dot(n, b) for n, b in zip(nw, nbd)]
        nbd = [bd(n) for n in nw]
        tw = [t + _dot(t, b) for t, b in zip(tw, nbd)]
    akv = [_dot(jnp.where(strict_w, a[:L, W2:], 0.0), stack(q["v"])) for a, q in zip(a_w, prep)]
    tx = [_dot(t, jnp.concatenate([stack(q["ab"]), stack(kv)], axis=1))
          for t, q, kv in zip(tw, prep, akv)]
    ry = [_dot(jnp.where(incl_w2, a[L:], 0.0),
               jnp.concatenate(
                   [jnp.concatenate([stack(x[:, :PAIR]), stack(x[:, PAIR:])], axis=1),
                    jnp.concatenate([jnp.zeros((W2, PAIR), F32), stack(q["v"])], axis=1)], axis=0))
          for a, x, q in zip(a_w, tx, prep)]
    for q, x, y in zip(prep, tx, ry):
        st = s_scr[q["s"], q["p"]]
        us = _dot(jnp.concatenate([x[:, :PAIR], q["rb"] + y[:, :PAIR]], axis=0), st)
        y_ref[q["s"], :, q["sl"]] = us[L:] + y[:, PAIR:]
        u = us[:L] + x[:, PAIR:]
        lhs = jnp.concatenate([jnp.where(diag, q["decay"], 0.0),
                               jnp.concatenate([q["bt"], q["kt"]], axis=0).T], axis=1)
        g = _dot(lhs, jnp.concatenate([st, u, q["v"]], axis=0))
        s_scr[q["s"], q["p"]] = jnp.where(head_diag, g, 0.0)

    @pl.when(c == pl.num_programs(1) - 1)
    def _():
        sout_ref[...] = s_scr[...]


def _scan(r, k, v, kk, kka, lw, s0, chunk, n_seq):
    nb, t, d_b = r.shape
    n_pairs = d_b // PAIR
    tok = pl.BlockSpec((n_seq, chunk, d_b), lambda b, c: (b, c, 0))
    st = pl.BlockSpec((n_seq, n_pairs, PAIR, PAIR), lambda b, c: (b, 0, 0, 0))
    return pl.pallas_call(
        functools.partial(_scan_kernel, chunk=chunk),
        grid=(nb // n_seq, t // chunk),
        in_specs=[tok] * 6 + [st],
        out_specs=[tok, st],
        out_shape=[jax.ShapeDtypeStruct((nb, t, d_b), F32),
                   jax.ShapeDtypeStruct((nb, n_pairs, PAIR, PAIR), F32)],
        scratch_shapes=[pltpu.VMEM((n_seq, n_pairs, PAIR, PAIR), F32)],
        compiler_params=pltpu.CompilerParams(dimension_semantics=("parallel", "arbitrary"),
                                             vmem_limit_bytes=VMEM_LIMIT),
        name="rwkv_scan",
    )(r, k, v, kk, kka, lw, s0)


def _state_to_blockdiag(wkv):
    b, h, n, _ = wkv.shape
    st = jnp.swapaxes(wkv, -1, -2).reshape(b, h // 2, 2, n, n)
    z = jnp.zeros_like(st[:, :, 0])
    top = jnp.concatenate([st[:, :, 0], z], axis=-1)
    bot = jnp.concatenate([z, st[:, :, 1]], axis=-1)
    return jnp.concatenate([top, bot], axis=-2)


def _blockdiag_to_state(s):
    b, hp, _, _ = s.shape
    n = HEAD_B
    blocks = jnp.stack([s[:, :, :n, :n], s[:, :, n:, n:]], axis=2)
    return jnp.swapaxes(blocks, -1, -2).reshape(b, hp * 2, n, n)


def _pack_bf16_pairs(x):
    w = x.shape[1] // 2
    bits = lambda v: lax.bitcast_convert_type(v.astype(BF16).astype(F32), jnp.uint32)
    return (bits(x[:, :w]) >> 16) | (bits(x[:, w:]) & jnp.uint32(0xFFFF0000))


def _unpack_bf16_pairs(u):
    lo = lax.bitcast_convert_type(u << 16, F32)
    hi = lax.bitcast_convert_type(u & jnp.uint32(0xFFFF0000), F32)
    return lo, hi


def _merge_kernel(y_ref, g_ref, bonus_ref, sgb_ref, apart_ref, x_ref, lg_ref, lb_ref, bd_ref,
                  wbo_ref, wo_ref, g2_ref, rw_ref, rb_ref,
                  h_out, xp_out, topi_out, prob_out, rank_out, cnt_out, cnt_scr):
    step = pl.program_id(0)

    @pl.when(step == 0)
    def _():
        cnt_scr[...] = jnp.zeros_like(cnt_scr)

    bd = bd_ref[...]
    y = y_ref[...]
    inv_n = 1.0 / HEAD_B
    mu = _head_sums(y, bd) * inv_n
    d = y - mu
    var = _head_sums(d * d, bd) * inv_n
    yn = d * lax.rsqrt(var + GN_EPS) * lg_ref[...] + lb_ref[...]
    yb = ((yn + bonus_ref[...]) * g_ref[...]).astype(BF16)
    merged = apart_ref[...] + sgb_ref[...] * jnp.dot(yb, wbo_ref[...], preferred_element_type=F32)
    h = x_ref[...] + jnp.dot(merged.astype(BF16), wo_ref[...], preferred_element_type=F32)
    h_out[...] = h
    xn2 = _rms(h, g2_ref[...])
    xp_out[...] = _pack_bf16_pairs(xn2)
    n_e = rb_ref.shape[1]
    xh, xl = _split(xn2)
    rw = rw_ref[...]
    t = jnp.dot(xh, rw, preferred_element_type=F32)
    logits = (t[:, :n_e] + t[:, n_e:] + jnp.dot(xl, rw[:, :n_e], preferred_element_type=F32)
              + rb_ref[...])
    tm = logits.shape[0]
    idx = lax.broadcasted_iota(jnp.int32, logits.shape, 1).astype(F32)
    work = logits
    tops, hots, sels = [], [], []
    for _ in range(TOP_K):
        m = jnp.max(work, axis=-1, keepdims=True)
        sel = jnp.min(jnp.where(work == m, idx, float(n_e)), axis=-1, keepdims=True)
        hot = idx == sel
        tops.append(m)
        hots.append(hot)
        sels.append(sel)
        work = jnp.where(hot, -jnp.inf, work)
    es = [jnp.exp(t - tops[0]) for t in tops]
    denom = es[0] + es[1] + es[2] + es[3]
    topi_out[...] = jnp.concatenate(sels, axis=1).astype(jnp.int32)
    prob_out[...] = jnp.concatenate([e / denom for e in es], axis=1)
    hot_any = jnp.zeros_like(logits)
    for hot in hots:
        hot_any = hot_any + hot.astype(F32)
    ri = lax.broadcasted_iota(jnp.int32, (tm, tm), 0)
    ci = lax.broadcasted_iota(jnp.int32, (tm, tm), 1)
    before = _dot((ci < ri).astype(BF16), hot_any) + cnt_scr[...]
    rank_out[...] = jnp.concatenate(
        [jnp.sum(jnp.where(hot, before, 0.0), axis=-1, keepdims=True) for hot in hots],
        axis=1).astype(jnp.int32)
    total = cnt_scr[...] + jnp.sum(hot_any, axis=0, keepdims=True)
    cnt_scr[...] = total
    cnt_out[...] = total.astype(jnp.int32)


def _merge(y, g, bonus, sgb, apart, x2d, tm, lnx_g, lnx_b, bd, wbo, wo, g2n, rw, rb):
    n, d = x2d.shape
    d_b = y.shape[1]
    n_e = rb.shape[1]
    tok = lambda w: pl.BlockSpec((tm, w), lambda i: (i, 0))
    weights = (lnx_g, lnx_b, bd, wbo, wo, g2n, rw, rb)
    return pl.pallas_call(
        _merge_kernel,
        grid=(n // tm,),
        in_specs=[tok(d_b), tok(d_b), tok(d_b), tok(d), tok(d), tok(d)]
        + [_full_spec(w.shape) for w in weights],
        out_specs=[tok(d), tok(d // 2), tok(TOP_K), tok(TOP_K), tok(TOP_K),
                   pl.BlockSpec((1, n_e), lambda i: (0, 0))],
        out_shape=[jax.ShapeDtypeStruct((n, d), F32),
                   jax.ShapeDtypeStruct((n, d // 2), jnp.uint32),
                   jax.ShapeDtypeStruct((n, TOP_K), jnp.int32),
                   jax.ShapeDtypeStruct((n, TOP_K), F32),
                   jax.ShapeDtypeStruct((n, TOP_K), jnp.int32),
                   jax.ShapeDtypeStruct((1, n_e), jnp.int32)],
        scratch_shapes=[pltpu.VMEM((1, n_e), F32)],
        compiler_params=pltpu.CompilerParams(dimension_semantics=("arbitrary",),
                                             vmem_limit_bytes=VMEM_LIMIT),
        name="merge_router",
    )(y, g, bonus, sgb, apart, x2d, *weights)


MOE_TM = 1024
MOE_SUB = 256
MOE_FF_CHUNK = 512
CAST_ROWS = 256


def _route(topi_p, rank_p, cnt_p, topi_s, rank_s, cnt_s):
    n_e = cnt_p.shape[-1]
    cnt_p, cnt_s = cnt_p.reshape(n_e), cnt_s.reshape(n_e)
    cnt = cnt_p + cnt_s
    padded = (cnt + MOE_TM - 1) // MOE_TM * MOE_TM
    ends = jnp.cumsum(padded)
    base = ends - padded
    lookup = lambda table, idx: jnp.sum(
        jnp.where(idx[..., None] == jnp.arange(n_e), table, 0), axis=-1)
    pos_p = lookup(base, topi_p) + rank_p
    pos_s = lookup(base + cnt_p, topi_s) + rank_s
    return (pos_p.astype(jnp.int32), pos_s.astype(jnp.int32),
            (base // MOE_TM).astype(jnp.int32), cnt.astype(jnp.int32))


def _moe_kernel(tile0_ref, cnt_ref, xs_hbm, wgu_ref, bgu_ref, wd_ref, bdn_ref, ys_hbm,
                wgu_b, wd_b, xbuf, ybuf, sem_in, sem_out):
    e = pl.program_id(0)
    tile0 = tile0_ref[e]
    cnt = cnt_ref[e]
    n_t = (cnt + MOE_TM - 1) // MOE_TM
    d, gu = wgu_ref.shape
    d_ff = wd_ref.shape[0]
    half = d // 2

    def in_copy(tile, slot):
        return pltpu.make_async_copy(xs_hbm.at[pl.ds(tile * MOE_TM, MOE_TM)], xbuf.at[slot],
                                     sem_in.at[slot])

    def out_copy(tile, slot):
        return pltpu.make_async_copy(ybuf.at[slot], ys_hbm.at[pl.ds(tile * MOE_TM, MOE_TM)],
                                     sem_out.at[slot])

    @pl.when(jnp.logical_and(e == 0, n_t > 0))
    def _():
        in_copy(tile0, 0).start()

    @pl.when(n_t > 0)
    def _():
        for r0 in range(0, d, CAST_ROWS):
            wgu_b[r0:r0 + CAST_ROWS, :] = wgu_ref[r0:r0 + CAST_ROWS, :].astype(BF16)
        for r0 in range(0, d_ff, CAST_ROWS):
            wd_b[r0:r0 + CAST_ROWS, :] = wd_ref[r0:r0 + CAST_ROWS, :].astype(BF16)

    def tile_step(j, carry):
        slot = j % 2
        in_copy(tile0 + j, slot).wait()

        @pl.when(j + 1 < n_t)
        def _():
            in_copy(tile0 + j + 1, 1 - slot).start()

        @pl.when(j >= 2)
        def _():
            out_copy(tile0 + j - 2, slot).wait()

        left = cnt - j * MOE_TM

        def ffn(row0, rows):
            lo, hi = _unpack_bf16_pairs(xbuf[slot, pl.ds(row0, rows)])
            valid = lax.broadcasted_iota(jnp.int32, lo.shape, 0) < left - row0
            x_lo = jnp.where(valid, lo, 0.0).astype(BF16)
            x_hi = jnp.where(valid, hi, 0.0).astype(BF16)

            def proj(c0):
                cols = slice(c0, c0 + MOE_FF_CHUNK)
                return (jnp.dot(x_lo, wgu_b[:half, cols], preferred_element_type=F32)
                        + jnp.dot(x_hi, wgu_b[half:, cols], preferred_element_type=F32)
                        + bgu_ref[:, cols])

            y = jnp.zeros((rows, d), F32)
            for f in range(d_ff // MOE_FF_CHUNK):
                c0 = f * MOE_FF_CHUNK
                gate = jnp.minimum(proj(c0), SWIGLU_LIMIT)
                up = jnp.clip(proj(d_ff + c0), -SWIGLU_LIMIT, SWIGLU_LIMIT)
                hh = (up + 1.0) * gate * _sigmoid(gate * SWIGLU_ALPHA)
                y = y + jnp.dot(hh.astype(BF16), wd_b[c0:c0 + MOE_FF_CHUNK, :],
                                preferred_element_type=F32)
            ybuf[slot, pl.ds(row0, rows)] = _pack_bf16_pairs(y + bdn_ref[...])

        n_sub = (jnp.minimum(left, MOE_TM) + MOE_SUB - 1) // MOE_SUB

        @pl.when(n_sub == MOE_TM // MOE_SUB)
        def _():
            ffn(0, MOE_TM)

        @pl.when(n_sub < MOE_TM // MOE_SUB)
        def _():
            def sub_step(i, c):
                ffn(pl.multiple_of(i * MOE_SUB, MOE_SUB), MOE_SUB)
                return c
            lax.fori_loop(0, n_sub, sub_step, 0)

        out_copy(tile0 + j, slot).start()
        return carry

    lax.fori_loop(0, n_t, tile_step, 0)

    @pl.when(n_t >= 2)
    def _():
        out_copy(tile0 + n_t - 2, n_t % 2).wait()

    @pl.when(n_t >= 1)
    def _():
        out_copy(tile0 + n_t - 1, (n_t - 1) % 2).wait()

    nxt = jnp.minimum(e + 1, pl.num_programs(0) - 1)

    @pl.when(jnp.logical_and(e + 1 < pl.num_programs(0), cnt_ref[nxt] > 0))
    def _():
        in_copy(tile0_ref[nxt], 0).start()


def _moe(xs, tile0, cnt, wgu, bgu, wd, bdn):
    p_rows, half = xs.shape
    n_e, d, gu = wgu.shape
    d_ff = wd.shape[1]
    w_blk = lambda e, tile0, cnt: (e, 0, 0)
    grid_spec = pltpu.PrefetchScalarGridSpec(
        num_scalar_prefetch=2,
        grid=(n_e,),
        in_specs=[pl.BlockSpec(memory_space=pl.ANY),
                  pl.BlockSpec((None, d, gu), w_blk),
                  pl.BlockSpec((None, 1, gu), w_blk),
                  pl.BlockSpec((None, d_ff, d), w_blk),
                  pl.BlockSpec((None, 1, d), w_blk)],
        out_specs=pl.BlockSpec(memory_space=pl.ANY),
        scratch_shapes=[pltpu.VMEM((d, gu), BF16), pltpu.VMEM((d_ff, d), BF16),
                        pltpu.VMEM((2, MOE_TM, half), jnp.uint32),
                        pltpu.VMEM((2, MOE_TM, half), jnp.uint32),
                        pltpu.SemaphoreType.DMA((2,)), pltpu.SemaphoreType.DMA((2,))])
    return pl.pallas_call(
        _moe_kernel,
        grid_spec=grid_spec,
        out_shape=jax.ShapeDtypeStruct((p_rows, half), jnp.uint32),
        compiler_params=pltpu.CompilerParams(dimension_semantics=("arbitrary",),
                                             vmem_limit_bytes=VMEM_LIMIT),
        name="moe_experts",
    )(tile0, cnt, xs, wgu, bgu, wd, bdn)


def _combine_kernel(h_ref, yg_ref, prob_ref, gf_ref, out_ref):
    half = h_ref.shape[1] // 2
    prob = prob_ref[...]
    acc_lo = jnp.zeros((h_ref.shape[0], half), F32)
    acc_hi = jnp.zeros((h_ref.shape[0], half), F32)
    for k in range(TOP_K):
        lo, hi = _unpack_bf16_pairs(yg_ref[k])
        pk = prob[:, k:k + 1]
        acc_lo = acc_lo + pk * lo
        acc_hi = acc_hi + pk * hi
    z = h_ref[...] + jnp.concatenate([acc_lo, acc_hi], axis=1)
    out_ref[...] = _rms(z, gf_ref[...])


def _combine(h, yg, row0, prob, gf, tm):
    n, d = h.shape
    tok = lambda w: pl.BlockSpec((tm, w), lambda i: (i, 0))
    t0 = row0 // tm
    return pl.pallas_call(
        _combine_kernel,
        grid=(n // tm,),
        in_specs=[tok(d), pl.BlockSpec((TOP_K, tm, d // 2), lambda i: (0, i + t0, 0)),
                  tok(TOP_K), _full_spec(gf.shape)],
        out_specs=tok(d),
        out_shape=jax.ShapeDtypeStruct((n, d), F32),
        compiler_params=pltpu.CompilerParams(dimension_semantics=("parallel",),
                                             vmem_limit_bytes=VMEM_LIMIT),
        name="moe_combine",
    )(h, yg, prob, gf)


SC_CORES = 2
SC_SUBCORES = 16
SC_WORKERS = SC_CORES * SC_SUBCORES
SC_MAX_INDEX = 128


def _sc_chunk(rows_per_worker):
    for c in range(SC_MAX_INDEX, 7, -8):
        if rows_per_worker % c == 0:
            return c
    raise ValueError(f"no 8-aligned chunk divides {rows_per_worker} rows")


def _sc_mesh():
    return plsc.VectorSubcoreMesh(core_axis_name="c", subcore_axis_name="s")


def _sc_worker():
    return lax.axis_index("s") * SC_CORES + lax.axis_index("c")


def _scatter_rows(xs_in, pos_t, p_rows):
    w = xs_in[0].shape[1]
    dtype = xs_in[0].dtype
    n_k, n = pos_t.shape
    pos_flat = pos_t.reshape(n_k * n)
    parts, row0 = [], 0
    for x in xs_in:
        per = x.shape[0] // SC_WORKERS
        parts.append((row0, per, _sc_chunk(per)))
        row0 += x.shape[0]
    max_chunk = max(c for _, _, c in parts)

    def body(*refs):
        x_hbms = refs[:len(xs_in)]
        pos_hbm, out_hbm, idx_v, rows_v, sem = refs[len(xs_in):]
        for x_hbm, (tok0, per, chunk) in zip(x_hbms, parts):
            base = _sc_worker() * per

            @pl.loop(0, per // chunk)
            def _(c):
                off = pl.multiple_of(base + c * chunk, 8)
                rows = rows_v.at[pl.ds(0, chunk)]
                idx = idx_v.at[pl.ds(0, chunk)]
                pltpu.sync_copy(x_hbm.at[pl.ds(off, chunk)], rows)
                for k in range(n_k):
                    src = pl.multiple_of(k * n + tok0 + off, 8)
                    pltpu.sync_copy(pos_hbm.at[pl.ds(src, chunk)], idx)
                    pltpu.async_copy(rows, out_hbm.at[idx], sem).wait()

    return pl.kernel(
        body, out_type=jax.ShapeDtypeStruct((p_rows, w), dtype), mesh=_sc_mesh(),
        scratch_types=[pltpu.VMEM((max_chunk,), jnp.int32), pltpu.VMEM((max_chunk, w), dtype),
                       pltpu.SemaphoreType.DMA],
        name="sc_scatter_rows")(*xs_in, pos_flat)


def _gather_rows(table, idx):
    n = idx.shape[0]
    w = table.shape[1]
    per = n // SC_WORKERS
    chunk = _sc_chunk(per)

    def body(table_hbm, idx_hbm, out_hbm, idx_v, rows_v, sem):
        base = _sc_worker() * per

        @pl.loop(0, per // chunk)
        def _(c):
            off = pl.multiple_of(base + c * chunk, 8)
            pltpu.sync_copy(idx_hbm.at[pl.ds(off, chunk)], idx_v)
            pltpu.async_copy(table_hbm.at[idx_v], rows_v, sem).wait()
            pltpu.sync_copy(rows_v, out_hbm.at[pl.ds(off, chunk)])

    return pl.kernel(
        body, out_type=jax.ShapeDtypeStruct((n, w), table.dtype), mesh=_sc_mesh(),
        scratch_types=[pltpu.VMEM((chunk,), jnp.int32), pltpu.VMEM((chunk, w), table.dtype),
                       pltpu.SemaphoreType.DMA],
        name="sc_gather_rows")(table, idx)


def _mix_matrix(ws, block, seq_rows):
    causal = jnp.tril(jnp.ones((CHUNK, CHUNK), dtype=bool))
    w = jnp.where(causal[None], ws, 0.0)[:, :seq_rows, :seq_rows]
    eye = jnp.eye(block // seq_rows, dtype=ws.dtype)
    return jnp.einsum("ab,gts->gatbs", eye, w).reshape(ws.shape[0], block, block).astype(BF16)


def _stream(x, shift_in, wkv_in, p, tm, seq_rows, scan_chunk, scan_seqs):
    nb, t, d = x.shape
    n = nb * t
    x2d = x.reshape(n, d)
    rows = CHUNK if seq_rows is None else seq_rows
    wmix = _mix_matrix(p["w_spatial"], max(rows, LANES), rows)
    pos = jnp.arange(tm) % rows
    d_a = p["wu"].shape[1]
    bias_full = jnp.repeat(p["b_spatial"].T[pos], d_a // G_A, axis=1)
    apart, vn = _branch_a(x2d, tm, p["norm1_g"], p["wu"], p["wv"], p["wga"], p["vnorm_g"],
                          p["vnorm_b"], wmix, bias_full, p["w_a_out"])
    if seq_rows is None:
        xb, ext = x, jnp.zeros((1, p["wcur"].shape[1]), F32)
    else:
        xb = x2d.reshape(n // tm, tm, d)
        ext = jnp.repeat(shift_in, seq_rows, axis=0).reshape(n // tm, tm, -1)
    outs = _branch_b(xb, tm, seq_rows, ext, p["norm1_g"], p["wcur"], p["wgb"], p["mu_shift"],
                     p["w0"], p["w2"], p["a0"], p["a2"], p["g2"], p["k_k"], p["k_a"], p["r_k"],
                     p["bd"])
    r, k2, v, kk, kka, lw, g, bonus, sgb, cur = outs
    d_b = r.shape[-1]
    if seq_rows is None:
        shift_out = cur[:, -1, :]
        scan_in = [a.reshape(nb, t, d_b) for a in (r, k2, v, kk, kka, lw)]
    else:
        shift_out = cur.reshape(nb, t, -1)[:, -1, :]
        pad = scan_chunk - t
        scan_in = [jnp.pad(a.reshape(nb, t, d_b), ((0, 0), (0, pad), (0, 0)))
                   for a in (r, k2, v, kk, kka, lw)]
    y, s_out = _scan(*scan_in, _state_to_blockdiag(wkv_in), scan_chunk, scan_seqs)
    y2d = y[:, :t].reshape(n, d_b)
    flat = lambda a: a.reshape(n, a.shape[-1])
    routed = _merge(y2d, flat(g), flat(bonus), flat(sgb), apart, x2d, tm, p["lnx_g"],
                    p["lnx_b"], p["bd"], p["w_b_out"], p["w_out"], p["norm2_g"],
                    p["router_w"], p["router_b"])
    return routed, vn, shift_out, _blockdiag_to_state(s_out)


def kernel(x_prompt, x_sample, state_shift, state_wkv, norm1_g, w_in, mu_shift, vnorm_g, vnorm_b, w_spatial, b_spatial, w_a_out, w0, w2, a0, a2, g2, k_k, k_a, r_k, lnx_g, lnx_b, w_b_out, w_out, norm2_g, router_w, router_b, exp_w_gu, exp_b_gu, exp_w_down, exp_b_down, normf_g):
    depth = w_in.shape[0]
    assert depth == 1, "the final norm is fused into the single layer's MoE call"
    d_model = x_prompt.shape[-1]
    d_b = w0.shape[-1]
    shift_w = mu_shift.shape[-1]
    d_a = vnorm_g.shape[-1]
    bp, tp, _ = x_prompt.shape
    bs, ts, _ = x_sample.shape
    head_id = jnp.arange(2 * LANES) // HEAD_B
    bd = (head_id[:, None] == head_id[None, :]).astype(BF16)
    row = lambda a: a.reshape(1, -1)

    hp, hs = x_prompt, x_sample
    shift_p = jnp.zeros((depth, bp, shift_w), state_shift.dtype)
    wkv_p = jnp.zeros((depth, bp) + state_wkv.shape[2:], state_wkv.dtype)
    vrows, shifts_p, wkvs_p, shifts_s, wkvs_s = [], [], [], [], []
    for l in range(depth):
        wi = w_in[l].astype(BF16)
        o = shift_w
        p = dict(
            norm1_g=row(norm1_g[l]), wcur=wi[:, :o], wu=wi[:, o:o + d_a],
            wv=wi[:, o + d_a:o + 2 * d_a], wga=wi[:, o + 2 * d_a:o + 2 * d_a + d_model],
            wgb=wi[:, o + 2 * d_a + d_model:], mu_shift=row(mu_shift[l]),
            vnorm_g=row(vnorm_g[l]), vnorm_b=row(vnorm_b[l]), w_spatial=w_spatial[l],
            b_spatial=b_spatial[l], w_a_out=w_a_out[l].astype(BF16), w0=row(w0[l]),
            w2=_stack_rhs3(w2[l]), a0=row(a0[l]), a2=_stack_rhs3(a2[l]), g2=_stack_rhs3(g2[l]),
            k_k=row(k_k[l]), k_a=row(k_a[l]),
            r_k=row(r_k[l]), lnx_g=row(lnx_g[l]), lnx_b=row(lnx_b[l]),
            w_b_out=w_b_out[l].astype(BF16), w_out=w_out[l].astype(BF16),
            norm2_g=row(norm2_g[l]), router_w=jnp.concatenate(_split(router_w[l]), axis=1),
            router_b=row(router_b[l]), bd=bd)
        routed_p, _, sh_p, s_p = _stream(hp, shift_p[l], wkv_p[l], p, 512, None, 64, 4)
        routed_s, vn_s, sh_s, s_s = _stream(hs, state_shift[l], state_wkv[l], p, 512, ts, 8, 8)
        h_p, xp_p, topi_p, prob_p, rank_p, cnt_p = routed_p
        h_s, xp_s, topi_s, prob_s, rank_s, cnt_s = routed_s
        n_p, n_s = h_p.shape[0], h_s.shape[0]
        n_e = router_w.shape[-1]
        n_tiles = (n_p + n_s) * TOP_K // MOE_TM + n_e
        pos_p, pos_s, tile0, cnt = _route(topi_p, rank_p, cnt_p, topi_s, rank_s, cnt_s)
        pos_t = jnp.concatenate([pos_p, pos_s], axis=0).T
        xs = _scatter_rows([xp_p, xp_s], pos_t, n_tiles * MOE_TM)
        ys = _moe(xs, tile0, cnt, exp_w_gu[l], exp_b_gu[l][:, None, :],
                  exp_w_down[l], exp_b_down[l][:, None, :])
        yg = _gather_rows(ys, pos_t.reshape(-1)).reshape(TOP_K, n_p + n_s, d_model // 2)
        gf = row(normf_g)
        hp = _combine(h_p, yg, 0, prob_p, gf, 512).reshape(bp, tp, d_model)
        hs = _combine(h_s, yg, n_p, prob_s, gf, 512).reshape(bs, ts, d_model)
        vrows.append(vn_s.reshape(bs, ts, d_a))
        shifts_p.append(sh_p)
        wkvs_p.append(s_p)
        shifts_s.append(sh_s)
        wkvs_s.append(s_s)
    return (hp, hs, jnp.stack(shifts_p), jnp.stack(wkvs_p), jnp.stack(shifts_s),
            jnp.stack(wkvs_s), jnp.stack(vrows))
```

```python
import functools

import jax
import jax.numpy as jnp
from jax import lax
from jax.experimental import pallas as pl
from jax.experimental.pallas import tpu as pltpu
from jax.experimental.pallas import tpu_sc as plsc

F32 = jnp.float32
BF16 = jnp.bfloat16

CHUNK = 128
G_A = 8
HEAD_B = 64
R_W, R_A, R_G = 64, 64, 128
TOP_K = 4
SWIGLU_LIMIT = 7.0
SWIGLU_ALPHA = 1.702
EPS = 1e-5
GN_EPS = HEAD_B * 1e-5

LANES = 128
PAIR = 2 * HEAD_B
VMEM_LIMIT = 56 * 1024 * 1024


def _dot(a, b):
    return jnp.dot(a.astype(BF16), b.astype(BF16), preferred_element_type=F32)


def _split(x):
    hi = x.astype(BF16)
    lo = (x - hi.astype(F32)).astype(BF16)
    return hi, lo


def _dot3(a, b):
    ah, al = _split(a)
    bh, bl = _split(b)
    return (jnp.dot(ah, bh, preferred_element_type=F32)
            + jnp.dot(ah, bl, preferred_element_type=F32)
            + jnp.dot(al, bh, preferred_element_type=F32))


def _dot_exact_rhs(a, b_bf16):
    ah, al = _split(a)
    return (jnp.dot(ah, b_bf16, preferred_element_type=F32)
            + jnp.dot(al, b_bf16, preferred_element_type=F32))


def _stack_rhs3(b):
    bh, bl = _split(b)
    return jnp.concatenate([bh, bh, bl], axis=0)


def _dot3_stacked(a, b_stacked):
    ah, al = _split(a)
    return jnp.dot(jnp.concatenate([ah, al, ah], axis=1), b_stacked, preferred_element_type=F32)


def _head_sums(x, ones_bd):
    w = ones_bd.shape[0]
    return jnp.concatenate([_dot_exact_rhs(x[:, c:c + w], ones_bd)
                            for c in range(0, x.shape[1], w)], axis=1)


def _dot_exact_lhs(a_bf16, b):
    bh, bl = _split(b)
    bm = b - bh.astype(F32) - bl.astype(F32)
    return (jnp.dot(a_bf16, bh, preferred_element_type=F32)
            + jnp.dot(a_bf16, bl, preferred_element_type=F32)
            + jnp.dot(a_bf16, bm.astype(BF16), preferred_element_type=F32))


def _rms(x, g):
    return x * lax.rsqrt(jnp.mean(x * x, axis=-1, keepdims=True) + EPS) * g


def _sigmoid(x):
    return 1.0 / (1.0 + jnp.exp(-x))


def _softplus(z):
    return jnp.maximum(z, 0.0) + jnp.log(1.0 + jnp.exp(-jnp.abs(z)))


def _full_spec(shape):
    nd = len(shape)
    return pl.BlockSpec(shape, lambda *_: (0,) * nd, pipeline_mode=pl.Buffered(1))


def _branch_a_kernel(x_ref, g1_ref, wu_ref, wv_ref, wga_ref, vg_ref, vb_ref, wmix_ref,
                     bias_ref, wao_ref, apart_ref, vn_ref):
    xn = _rms(x_ref[...], g1_ref[...]).astype(BF16)
    v = jnp.dot(xn, wv_ref[...], preferred_element_type=F32)
    mu = jnp.mean(v, axis=-1, keepdims=True)
    d = v - mu
    var = jnp.mean(d * d, axis=-1, keepdims=True)
    vn = d * lax.rsqrt(var + EPS) * vg_ref[...] + vb_ref[...]
    vn_ref[...] = vn
    vnb = vn.astype(BF16)
    c_a = vnb.shape[1] // G_A
    rb = wmix_ref.shape[1]
    mixed = jnp.concatenate(
        [jnp.concatenate(
            [jnp.dot(wmix_ref[g], vnb[r0:r0 + rb, g * c_a:(g + 1) * c_a],
                     preferred_element_type=F32) for g in range(G_A)], axis=1)
         for r0 in range(0, vnb.shape[0], rb)], axis=0) + bias_ref[...]
    u = jnp.dot(xn, wu_ref[...], preferred_element_type=F32)
    ya = (u * mixed).astype(BF16)
    ga = jnp.dot(xn, wga_ref[...], preferred_element_type=F32)
    apart_ref[...] = _sigmoid(ga) * jnp.dot(ya, wao_ref[...], preferred_element_type=F32)


def _branch_a(x2d, tm, g1, wu, wv, wga, vg, vb, wmix, bias_full, wao):
    n, d = x2d.shape
    d_a = wu.shape[1]
    tok = lambda w: pl.BlockSpec((tm, w), lambda i: (i, 0))
    return pl.pallas_call(
        _branch_a_kernel,
        grid=(n // tm,),
        in_specs=[tok(d), _full_spec(g1.shape), _full_spec(wu.shape), _full_spec(wv.shape),
                  _full_spec(wga.shape), _full_spec(vg.shape), _full_spec(vb.shape),
                  _full_spec(wmix.shape), _full_spec(bias_full.shape), _full_spec(wao.shape)],
        out_specs=[tok(d), tok(d_a)],
        out_shape=[jax.ShapeDtypeStruct((n, d), F32), jax.ShapeDtypeStruct((n, d_a), F32)],
        compiler_params=pltpu.CompilerParams(dimension_semantics=("parallel",),
                                             vmem_limit_bytes=VMEM_LIMIT),
        name="branch_a",
    )(x2d, g1, wu, wv, wga, vg, vb, wmix, bias_full, wao)


def _branch_b_kernel(x_ref, g1_ref, wcur_ref, wgb_ref, mu_ref, ext_ref, w0_ref, w2_ref,
                     a0_ref, a2_ref, g2_ref, kk_ref, ka_ref, rk_ref, bd_ref,
                     r_out, k_out, v_out, kk_out, kka_out, lw_out, g_out, bonus_out,
                     sgb_out, cur_out, carry_scr, *, seq_rows, d_b):
    tm = x_ref.shape[0]
    xn = _rms(x_ref[...], g1_ref[...]).astype(BF16)
    cur = jnp.dot(xn, wcur_ref[...], preferred_element_type=F32)
    rolled = pltpu.roll(cur, 1, axis=0)
    row = lax.broadcasted_iota(jnp.int32, (tm, 1), 0)
    if seq_rows is None:
        first_tile = pl.program_id(1) == 0
        carry = jnp.where(first_tile, 0.0, carry_scr[...])
        prev = jnp.where(row == 0, carry, rolled)
        carry_scr[...] = cur[tm - 1:tm, :]
        cur_out[...] = cur[tm - 8:tm, :]
    else:
        prev = jnp.where(row % seq_rows == 0, ext_ref[...], rolled)
        cur_out[...] = cur
    xs = cur + (prev - cur) * mu_ref[...]
    r = xs[:, 0:d_b]
    k = xs[:, d_b:2 * d_b]
    v = xs[:, 2 * d_b:3 * d_b]
    o = 3 * d_b
    xw = xs[:, o:o + R_W]
    xa = xs[:, o + R_W:o + R_W + R_A]
    xg = xs[:, o + R_W + R_A:o + R_W + R_A + R_G]
    wl = w0_ref[...] + _dot3_stacked(jnp.tanh(xw), w2_ref[...])
    lw = -jnp.exp(-_softplus(-wl) - 0.5)
    a = _sigmoid(a0_ref[...] + _dot3_stacked(xa, a2_ref[...]))
    g = _dot3_stacked(_sigmoid(xg), g2_ref[...])
    bd = bd_ref[...]
    kkr = k * kk_ref[...]
    n2 = _head_sums(kkr * kkr, bd)
    kk = kkr / jnp.maximum(jnp.sqrt(n2), 1e-12)
    k2 = k * (1.0 + (a - 1.0) * ka_ref[...])
    bonus = _head_sums(r * k2 * rk_ref[...], bd) * v
    r_out[...] = r
    k_out[...] = k2
    v_out[...] = v
    kk_out[...] = kk
    kka_out[...] = kk * a
    lw_out[...] = lw
    g_out[...] = g
    bonus_out[...] = bonus
    sgb_out[...] = _sigmoid(jnp.dot(xn, wgb_ref[...], preferred_element_type=F32))


def _branch_b(x3d, tm, seq_rows, ext, g1, wcur, wgb, mu, w0, w2, a0, a2, g2, k_k, k_a, r_k, bd):
    nb, t, d = x3d.shape
    d_b = w0.shape[1]
    shift_w = wcur.shape[1]
    nt = t // tm
    tok = lambda w: pl.BlockSpec((None, tm, w), lambda b, i: (b, i, 0))
    cur_rows = 8 if seq_rows is None else tm
    outs = [jax.ShapeDtypeStruct((nb, t, d_b), F32)] * 8 + [
        jax.ShapeDtypeStruct((nb, t, d), F32),
        jax.ShapeDtypeStruct((nb, nt * cur_rows, shift_w), F32)]
    out_specs = [tok(d_b)] * 8 + [tok(d), pl.BlockSpec((None, cur_rows, shift_w),
                                                       lambda b, i: (b, i, 0))]
    weights = (g1, wcur, wgb, mu)
    small = (w0, w2, a0, a2, g2, k_k, k_a, r_k, bd)
    return pl.pallas_call(
        functools.partial(_branch_b_kernel, seq_rows=seq_rows, d_b=d_b),
        grid=(nb, nt),
        in_specs=[tok(d)] + [_full_spec(w.shape) for w in weights]
        + [tok(shift_w) if seq_rows is not None else _full_spec(ext.shape)]
        + [_full_spec(w.shape) for w in small],
        out_specs=out_specs,
        out_shape=outs,
        scratch_shapes=[pltpu.VMEM((1, shift_w), F32)],
        compiler_params=pltpu.CompilerParams(dimension_semantics=("parallel", "arbitrary"),
                                             vmem_limit_bytes=VMEM_LIMIT),
        name="branch_b",
    )(x3d, *weights, ext, *small)


def _scan_kernel(*refs, chunk, zero_state):
    r_ref, k_ref, v_ref, kk_ref, kka_ref, lw_ref = refs[:6]
    s0_ref = None if zero_state else refs[6]
    y_ref, sout_ref, s_scr = refs[-3:]
    L = chunk
    W2 = 2 * L
    c = pl.program_id(1)
    n_seq, n_pairs = s_scr.shape[:2]

    @pl.when(c == 0)
    def _():
        if zero_state:
            s_scr[...] = jnp.zeros_like(s_scr)
        else:
            z = jnp.zeros((HEAD_B, HEAD_B), F32)
            for s in range(n_seq):
                for p in range(n_pairs):
                    nat = jnp.concatenate(
                        [jnp.concatenate([s0_ref[s, 2 * p], z], axis=1),
                         jnp.concatenate([z, s0_ref[s, 2 * p + 1]], axis=1)], axis=0)
                    s_scr[s, p] = nat.T

    iota = lambda shape, dim: lax.broadcasted_iota(jnp.int32, shape, dim)
    stack_mask = (iota((W2, 1), 0) >= L) == (iota((1, PAIR), 1) >= HEAD_B)
    bd_mask = (iota((W2, 1), 0) >= L) == (iota((1, W2), 1) >= L)
    assert L & (L - 1) == 0, "chunk length must be a power of two"
    rw = iota((L, W2), 0)
    cw = iota((L, W2), 1) & (L - 1)
    strict_w = cw < rw
    eye_w = (cw == rw).astype(F32)
    incl_w2 = (iota((L, 2 * W2), 1) & (L - 1)) <= iota((L, 2 * W2), 0)
    tri = (iota((L, L), 1) <= iota((L, L), 0)).astype(BF16)
    di = iota((PAIR, PAIR), 0)
    dj = iota((PAIR, PAIR), 1)
    diag = di == dj
    head_diag = (di >= HEAD_B) == (dj >= HEAD_B)

    def stack(x):
        return jnp.where(stack_mask, jnp.concatenate([x, x], axis=0), 0.0)

    def bd(xw):
        return jnp.where(bd_mask, jnp.concatenate([xw, xw], axis=0), 0.0)

    n_double = max(L.bit_length() - 2, 0)
    prep = []
    for s in range(n_seq):
        lw_all = lw_ref[s]
        cum_all = _dot_exact_lhs(tri, lw_all)
        for p in range(n_pairs):
            sl = slice(p * PAIR, (p + 1) * PAIR)
            cum = cum_all[:, sl]
            cum_last = cum[L - 1:L, :]
            e_neg = jnp.exp(-cum)
            e_rel = jnp.exp(cum_last - cum)
            kk = kk_ref[s, :, sl]
            kka = kka_ref[s, :, sl]
            kx = k_ref[s, :, sl]
            prep.append(dict(
                s=s, p=p, sl=sl, ab=-kk * jnp.exp(cum - lw_all[:, sl]),
                rb=r_ref[s, :, sl] * jnp.exp(cum), bb=kka * e_neg, kb=kx * e_neg,
                bt=kka * e_rel, kt=kx * e_rel, v=v_ref[s, :, sl], decay=jnp.exp(cum_last)))
    a_w = [lax.dot_general(
        jnp.concatenate([q["ab"], q["rb"]], axis=0).astype(BF16),
        jnp.concatenate([stack(q["bb"]), stack(q["kb"])], axis=0).astype(BF16),
        (((1,), (1,)), ((), ())), preferred_element_type=F32) for q in prep]
    nw = [jnp.where(strict_w, a[:L, :W2], 0.0) for a in a_w]
    tw = [eye_w + n for n in nw]
    nbd = [bd(n) for n in nw]
    for _ in range(n_double):
        nw = [_dot(n, b) for n, b in zip(nw, nbd)]
        nbd = [bd(n) for n in nw]
        tw = [t + _dot(t, b) for t, b in zip(tw, nbd)]
    akv = [_dot(jnp.where(strict_w, a[:L, W2:], 0.0), stack(q["v"])) for a, q in zip(a_w, prep)]
    tx = [_dot(t, jnp.concatenate([stack(q["ab"]), stack(kv)], axis=1))
          for t, q, kv in zip(tw, prep, akv)]
    ry = [_dot(jnp.where(incl_w2, a[L:], 0.0),
               jnp.concatenate(
                   [jnp.concatenate([stack(x[:, :PAIR]), stack(x[:, PAIR:])], axis=1),
                    jnp.concatenate([jnp.zeros((W2, PAIR), F32), stack(q["v"])], axis=1)], axis=0))
          for a, x, q in zip(a_w, tx, prep)]
    for q, x, y in zip(prep, tx, ry):
        st = s_scr[q["s"], q["p"]]
        us = _dot(jnp.concatenate([x[:, :PAIR], q["rb"] + y[:, :PAIR]], axis=0), st)
        y_ref[q["s"], :, q["sl"]] = us[L:] + y[:, PAIR:]
        u = us[:L] + x[:, PAIR:]
        lhs = jnp.concatenate([jnp.where(diag, q["decay"], 0.0),
                               jnp.concatenate([q["bt"], q["kt"]], axis=0).T], axis=1)
        g = _dot(lhs, jnp.concatenate([st, u, q["v"]], axis=0))
        s_scr[q["s"], q["p"]] = jnp.where(head_diag, g, 0.0)

    @pl.when(c == pl.num_programs(1) - 1)
    def _():
        for s in range(n_seq):
            for p in range(n_pairs):
                nat = s_scr[s, p].T
                sout_ref[s, 2 * p] = nat[:HEAD_B, :HEAD_B]
                sout_ref[s, 2 * p + 1] = nat[HEAD_B:, HEAD_B:]


def _scan(r, k, v, kk, kka, lw, s0, chunk, n_seq):
    nb, t, d_b = r.shape
    n_pairs = d_b // PAIR
    n_heads = d_b // HEAD_B
    tok = pl.BlockSpec((n_seq, chunk, d_b), lambda b, c: (b, c, 0))
    st = pl.BlockSpec((n_seq, n_heads, HEAD_B, HEAD_B), lambda b, c: (b, 0, 0, 0))
    zero_state = s0 is None
    return pl.pallas_call(
        functools.partial(_scan_kernel, chunk=chunk, zero_state=zero_state),
        grid=(nb // n_seq, t // chunk),
        in_specs=[tok] * 6 + ([] if zero_state else [st]),
        out_specs=[tok, st],
        out_shape=[jax.ShapeDtypeStruct((nb, t, d_b), F32),
                   jax.ShapeDtypeStruct((nb, n_heads, HEAD_B, HEAD_B), F32)],
        scratch_shapes=[pltpu.VMEM((n_seq, n_pairs, PAIR, PAIR), F32)],
        compiler_params=pltpu.CompilerParams(dimension_semantics=("parallel", "arbitrary"),
                                             vmem_limit_bytes=VMEM_LIMIT),
        name="rwkv_scan",
    )(r, k, v, kk, kka, lw, *(() if zero_state else (s0,)))


def _pack_bf16_pairs(x):
    w = x.shape[1] // 2
    bits = lambda v: lax.bitcast_convert_type(v.astype(BF16).astype(F32), jnp.uint32)
    return (bits(x[:, :w]) >> 16) | (bits(x[:, w:]) & jnp.uint32(0xFFFF0000))


def _unpack_bf16_pairs(u):
    lo = lax.bitcast_convert_type(u << 16, F32)
    hi = lax.bitcast_convert_type(u & jnp.uint32(0xFFFF0000), F32)
    return lo, hi


def _merge_kernel(y_ref, g_ref, bonus_ref, sgb_ref, apart_ref, x_ref, lg_ref, lb_ref, bd_ref,
                  wbo_ref, wo_ref, g2_ref, rw_ref, rb_ref,
                  h_out, xp_out, topi_out, prob_out, rank_out, cnt_out, cnt_scr):
    step = pl.program_id(0)

    @pl.when(step == 0)
    def _():
        cnt_scr[...] = jnp.zeros_like(cnt_scr)

    bd = bd_ref[...]
    y = y_ref[...]
    inv_n = 1.0 / HEAD_B
    mu = _head_sums(y, bd) * inv_n
    d = y - mu
    var = _head_sums(d * d, bd) * inv_n
    yn = d * lax.rsqrt(var + GN_EPS) * lg_ref[...] + lb_ref[...]
    yb = ((yn + bonus_ref[...]) * g_ref[...]).astype(BF16)
    merged = apart_ref[...] + sgb_ref[...] * jnp.dot(yb, wbo_ref[...], preferred_element_type=F32)
    h = x_ref[...] + jnp.dot(merged.astype(BF16), wo_ref[...], preferred_element_type=F32)
    h_out[...] = h
    xn2 = _rms(h, g2_ref[...])
    xp_out[...] = _pack_bf16_pairs(xn2)
    n_e = rb_ref.shape[1]
    xh, xl = _split(xn2)
    rw = rw_ref[...]
    t = jnp.dot(xh, rw, preferred_element_type=F32)
    logits = (t[:, :n_e] + t[:, n_e:] + jnp.dot(xl, rw[:, :n_e], preferred_element_type=F32)
              + rb_ref[...])
    tm = logits.shape[0]
    idx = lax.broadcasted_iota(jnp.int32, logits.shape, 1).astype(F32)
    work = logits
    tops, hots, sels = [], [], []
    for _ in range(TOP_K):
        m = jnp.max(work, axis=-1, keepdims=True)
        sel = jnp.min(jnp.where(work == m, idx, float(n_e)), axis=-1, keepdims=True)
        hot = idx == sel
        tops.append(m)
        hots.append(hot)
        sels.append(sel)
        work = jnp.where(hot, -jnp.inf, work)
    es = [jnp.exp(t - tops[0]) for t in tops]
    denom = es[0] + es[1] + es[2] + es[3]
    topi_out[...] = jnp.concatenate(sels, axis=1).astype(jnp.int32)
    prob_out[...] = jnp.concatenate([e / denom for e in es], axis=1)
    hot_any = jnp.zeros_like(logits)
    for hot in hots:
        hot_any = hot_any + hot.astype(F32)
    ri = lax.broadcasted_iota(jnp.int32, (tm, tm), 0)
    ci = lax.broadcasted_iota(jnp.int32, (tm, tm), 1)
    before = _dot((ci < ri).astype(BF16), hot_any) + cnt_scr[...]
    rank_out[...] = jnp.concatenate(
        [jnp.sum(jnp.where(hot, before, 0.0), axis=-1, keepdims=True) for hot in hots],
        axis=1).astype(jnp.int32)
    total = cnt_scr[...] + jnp.sum(hot_any, axis=0, keepdims=True)
    cnt_scr[...] = total
    cnt_out[...] = total.astype(jnp.int32)


def _merge(y, g, bonus, sgb, apart, x2d, tm, lnx_g, lnx_b, bd, wbo, wo, g2n, rw, rb):
    n, d = x2d.shape
    d_b = y.shape[1]
    n_e = rb.shape[1]
    tok = lambda w: pl.BlockSpec((tm, w), lambda i: (i, 0))
    weights = (lnx_g, lnx_b, bd, wbo, wo, g2n, rw, rb)
    return pl.pallas_call(
        _merge_kernel,
        grid=(n // tm,),
        in_specs=[tok(d_b), tok(d_b), tok(d_b), tok(d), tok(d), tok(d)]
        + [_full_spec(w.shape) for w in weights],
        out_specs=[tok(d), tok(d // 2), tok(TOP_K), tok(TOP_K), tok(TOP_K),
                   pl.BlockSpec((1, n_e), lambda i: (0, 0))],
        out_shape=[jax.ShapeDtypeStruct((n, d), F32),
                   jax.ShapeDtypeStruct((n, d // 2), jnp.uint32),
                   jax.ShapeDtypeStruct((n, TOP_K), jnp.int32),
                   jax.ShapeDtypeStruct((n, TOP_K), F32),
                   jax.ShapeDtypeStruct((n, TOP_K), jnp.int32),
                   jax.ShapeDtypeStruct((1, n_e), jnp.int32)],
        scratch_shapes=[pltpu.VMEM((1, n_e), F32)],
        compiler_params=pltpu.CompilerParams(dimension_semantics=("arbitrary",),
                                             vmem_limit_bytes=VMEM_LIMIT),
        name="merge_router",
    )(y, g, bonus, sgb, apart, x2d, *weights)


MOE_TM = 1024
MOE_SUB = 256
MOE_FF_CHUNK = 512
CAST_ROWS = 256


def _route(topi_p, rank_p, cnt_p, topi_s, rank_s, cnt_s):
    n_e = cnt_p.shape[-1]
    cnt_p, cnt_s = cnt_p.reshape(n_e), cnt_s.reshape(n_e)
    cnt = cnt_p + cnt_s
    padded = (cnt + MOE_TM - 1) // MOE_TM * MOE_TM
    ends = jnp.cumsum(padded)
    base = ends - padded
    lookup = lambda table, idx: jnp.sum(
        jnp.where(idx[..., None] == jnp.arange(n_e), table, 0), axis=-1)
    pos_p = lookup(base, topi_p) + rank_p
    pos_s = lookup(base + cnt_p, topi_s) + rank_s
    return (pos_p.astype(jnp.int32), pos_s.astype(jnp.int32),
            (base // MOE_TM).astype(jnp.int32), cnt.astype(jnp.int32))


def _moe_kernel(tile0_ref, cnt_ref, xs_hbm, wgu_ref, bgu_ref, wd_ref, bdn_ref, ys_hbm,
                wgu_b, wd_b, xbuf, ybuf, sem_in, sem_out, done_ref):
    e = pl.program_id(0)
    n_e = pl.num_programs(0)

    @pl.when(e == 0)
    def _():
        done_ref[0] = 0

    g0 = done_ref[0]
    tile0 = tile0_ref[e]
    cnt = cnt_ref[e]
    n_t = (cnt + MOE_TM - 1) // MOE_TM
    d, gu = wgu_ref.shape
    d_ff = wd_ref.shape[0]
    half = d // 2

    def in_copy(tile, slot):
        return pltpu.make_async_copy(xs_hbm.at[pl.ds(tile * MOE_TM, MOE_TM)], xbuf.at[slot],
                                     sem_in.at[slot])

    def out_copy(tile, slot):
        return pltpu.make_async_copy(ybuf.at[slot], ys_hbm.at[pl.ds(tile * MOE_TM, MOE_TM)],
                                     sem_out.at[slot])

    @pl.when(jnp.logical_and(e == 0, n_t > 0))
    def _():
        in_copy(tile0, 0).start()

    @pl.when(n_t > 0)
    def _():
        for r0 in range(0, d, CAST_ROWS):
            wgu_b[r0:r0 + CAST_ROWS, :] = wgu_ref[r0:r0 + CAST_ROWS, :].astype(BF16)
        for r0 in range(0, d_ff, CAST_ROWS):
            wd_b[r0:r0 + CAST_ROWS, :] = wd_ref[r0:r0 + CAST_ROWS, :].astype(BF16)

    def tile_step(j, carry):
        slot = (g0 + j) % 2
        in_copy(tile0 + j, slot).wait()

        @pl.when(j + 1 < n_t)
        def _():
            in_copy(tile0 + j + 1, 1 - slot).start()

        @pl.when(g0 + j >= 2)
        def _():
            out_copy(0, slot).wait()

        left = cnt - j * MOE_TM

        def ffn(row0, rows):
            lo, hi = _unpack_bf16_pairs(xbuf[slot, pl.ds(row0, rows)])
            valid = lax.broadcasted_iota(jnp.int32, lo.shape, 0) < left - row0
            x_lo = jnp.where(valid, lo, 0.0).astype(BF16)
            x_hi = jnp.where(valid, hi, 0.0).astype(BF16)

            def proj(c0):
                cols = slice(c0, c0 + MOE_FF_CHUNK)
                return (jnp.dot(x_lo, wgu_b[:half, cols], preferred_element_type=F32)
                        + jnp.dot(x_hi, wgu_b[half:, cols], preferred_element_type=F32)
                        + bgu_ref[:, cols])

            y = jnp.zeros((rows, d), F32)
            for f in range(d_ff // MOE_FF_CHUNK):
                c0 = f * MOE_FF_CHUNK
                gate = jnp.minimum(proj(c0), SWIGLU_LIMIT)
                up = jnp.clip(proj(d_ff + c0), -SWIGLU_LIMIT, SWIGLU_LIMIT)
                hh = (up + 1.0) * gate * _sigmoid(gate * SWIGLU_ALPHA)
                y = y + jnp.dot(hh.astype(BF16), wd_b[c0:c0 + MOE_FF_CHUNK, :],
                                preferred_element_type=F32)
            ybuf[slot, pl.ds(row0, rows)] = _pack_bf16_pairs(y + bdn_ref[...])

        n_sub = (jnp.minimum(left, MOE_TM) + MOE_SUB - 1) // MOE_SUB

        @pl.when(n_sub == MOE_TM // MOE_SUB)
        def _():
            ffn(0, MOE_TM)

        @pl.when(n_sub < MOE_TM // MOE_SUB)
        def _():
            def sub_step(i, c):
                ffn(pl.multiple_of(i * MOE_SUB, MOE_SUB), MOE_SUB)
                return c
            lax.fori_loop(0, n_sub, sub_step, 0)

        out_copy(tile0 + j, slot).start()
        return carry

    lax.fori_loop(0, n_t, tile_step, 0)

    g1 = g0 + n_t
    done_ref[0] = g1
    nxt = jnp.minimum(e + 1, n_e - 1)

    @pl.when(jnp.logical_and(e + 1 < n_e, cnt_ref[nxt] > 0))
    def _():
        in_copy(tile0_ref[nxt], g1 % 2).start()

    @pl.when(jnp.logical_and(e == n_e - 1, g1 >= 2))
    def _():
        out_copy(0, g1 % 2).wait()

    @pl.when(jnp.logical_and(e == n_e - 1, g1 >= 1))
    def _():
        out_copy(0, (g1 - 1) % 2).wait()


def _moe(xs, tile0, cnt, wgu, bgu, wd, bdn):
    p_rows, half = xs.shape
    n_e, d, gu = wgu.shape
    d_ff = wd.shape[1]
    w_blk = lambda e, tile0, cnt: (e, 0, 0)
    grid_spec = pltpu.PrefetchScalarGridSpec(
        num_scalar_prefetch=2,
        grid=(n_e,),
        in_specs=[pl.BlockSpec(memory_space=pl.ANY),
                  pl.BlockSpec((None, d, gu), w_blk),
                  pl.BlockSpec((None, 1, gu), w_blk),
                  pl.BlockSpec((None, d_ff, d), w_blk),
                  pl.BlockSpec((None, 1, d), w_blk)],
        out_specs=pl.BlockSpec(memory_space=pl.ANY),
        scratch_shapes=[pltpu.VMEM((d, gu), BF16), pltpu.VMEM((d_ff, d), BF16),
                        pltpu.VMEM((2, MOE_TM, half), jnp.uint32),
                        pltpu.VMEM((2, MOE_TM, half), jnp.uint32),
                        pltpu.SemaphoreType.DMA((2,)), pltpu.SemaphoreType.DMA((2,)),
                        pltpu.SMEM((1,), jnp.int32)])
    return pl.pallas_call(
        _moe_kernel,
        grid_spec=grid_spec,
        out_shape=jax.ShapeDtypeStruct((p_rows, half), jnp.uint32),
        compiler_params=pltpu.CompilerParams(dimension_semantics=("arbitrary",),
                                             vmem_limit_bytes=VMEM_LIMIT),
        name="moe_experts",
    )(tile0, cnt, xs, wgu, bgu, wd, bdn)


def _combine_kernel(h_ref, yg_ref, prob_ref, gf_ref, out_ref):
    half = h_ref.shape[1] // 2
    prob = prob_ref[...]
    acc_lo = jnp.zeros((h_ref.shape[0], half), F32)
    acc_hi = jnp.zeros((h_ref.shape[0], half), F32)
    for k in range(TOP_K):
        lo, hi = _unpack_bf16_pairs(yg_ref[k])
        pk = prob[:, k:k + 1]
        acc_lo = acc_lo + pk * lo
        acc_hi = acc_hi + pk * hi
    z = h_ref[...] + jnp.concatenate([acc_lo, acc_hi], axis=1)
    out_ref[...] = _rms(z, gf_ref[...])


def _combine(h, yg, row0, prob, gf, tm):
    n, d = h.shape
    tok = lambda w: pl.BlockSpec((tm, w), lambda i: (i, 0))
    t0 = row0 // tm
    return pl.pallas_call(
        _combine_kernel,
        grid=(n // tm,),
        in_specs=[tok(d), pl.BlockSpec((TOP_K, tm, d // 2), lambda i: (0, i + t0, 0)),
                  tok(TOP_K), _full_spec(gf.shape)],
        out_specs=tok(d),
        out_shape=jax.ShapeDtypeStruct((n, d), F32),
        compiler_params=pltpu.CompilerParams(dimension_semantics=("parallel",),
                                             vmem_limit_bytes=VMEM_LIMIT),
        name="moe_combine",
    )(h, yg, prob, gf)


SC_CORES = 2
SC_SUBCORES = 16
SC_WORKERS = SC_CORES * SC_SUBCORES
SC_MAX_INDEX = 128


def _sc_chunk(rows_per_worker):
    for c in range(SC_MAX_INDEX, 7, -8):
        if rows_per_worker % c == 0:
            return c
    raise ValueError(f"no 8-aligned chunk divides {rows_per_worker} rows")


def _sc_mesh():
    return plsc.VectorSubcoreMesh(core_axis_name="c", subcore_axis_name="s")


def _sc_worker():
    return lax.axis_index("s") * SC_CORES + lax.axis_index("c")


def _scatter_rows(xs_in, pos_t, p_rows):
    w = xs_in[0].shape[1]
    dtype = xs_in[0].dtype
    n_k, n = pos_t.shape
    pos_flat = pos_t.reshape(n_k * n)
    parts, row0 = [], 0
    for x in xs_in:
        per = x.shape[0] // SC_WORKERS
        parts.append((row0, per, _sc_chunk(per)))
        row0 += x.shape[0]
    max_chunk = max(c for _, _, c in parts)

    def body(*refs):
        x_hbms = refs[:len(xs_in)]
        pos_hbm, out_hbm, idx_v, rows_v, sem = refs[len(xs_in):]
        for x_hbm, (tok0, per, chunk) in zip(x_hbms, parts):
            base = _sc_worker() * per

            @pl.loop(0, per // chunk)
            def _(c):
                off = pl.multiple_of(base + c * chunk, 8)
                rows = rows_v.at[pl.ds(0, chunk)]
                idx = idx_v.at[pl.ds(0, chunk)]
                pltpu.sync_copy(x_hbm.at[pl.ds(off, chunk)], rows)
                for k in range(n_k):
                    src = pl.multiple_of(k * n + tok0 + off, 8)
                    pltpu.sync_copy(pos_hbm.at[pl.ds(src, chunk)], idx)
                    pltpu.async_copy(rows, out_hbm.at[idx], sem).wait()

    return pl.kernel(
        body, out_type=jax.ShapeDtypeStruct((p_rows, w), dtype), mesh=_sc_mesh(),
        scratch_types=[pltpu.VMEM((max_chunk,), jnp.int32), pltpu.VMEM((max_chunk, w), dtype),
                       pltpu.SemaphoreType.DMA],
        name="sc_scatter_rows")(*xs_in, pos_flat)


def _gather_rows(table, idx):
    n = idx.shape[0]
    w = table.shape[1]
    per = n // SC_WORKERS
    chunk = _sc_chunk(per)

    def body(table_hbm, idx_hbm, out_hbm, idx_v, rows_v, sem):
        base = _sc_worker() * per

        @pl.loop(0, per // chunk)
        def _(c):
            off = pl.multiple_of(base + c * chunk, 8)
            pltpu.sync_copy(idx_hbm.at[pl.ds(off, chunk)], idx_v)
            pltpu.async_copy(table_hbm.at[idx_v], rows_v, sem).wait()
            pltpu.sync_copy(rows_v, out_hbm.at[pl.ds(off, chunk)])

    return pl.kernel(
        body, out_type=jax.ShapeDtypeStruct((n, w), table.dtype), mesh=_sc_mesh(),
        scratch_types=[pltpu.VMEM((chunk,), jnp.int32), pltpu.VMEM((chunk, w), table.dtype),
                       pltpu.SemaphoreType.DMA],
        name="sc_gather_rows")(table, idx)


def _mix_matrix(ws, block, seq_rows):
    causal = jnp.tril(jnp.ones((CHUNK, CHUNK), dtype=bool))
    w = jnp.where(causal[None], ws, 0.0)[:, :seq_rows, :seq_rows]
    eye = jnp.eye(block // seq_rows, dtype=ws.dtype)
    return jnp.einsum("ab,gts->gatbs", eye, w).reshape(ws.shape[0], block, block).astype(BF16)


def _stream(x, shift_in, wkv_in, p, tm, seq_rows, scan_chunk, scan_seqs):
    nb, t, d = x.shape
    n = nb * t
    x2d = x.reshape(n, d)
    rows = CHUNK if seq_rows is None else seq_rows
    wmix = _mix_matrix(p["w_spatial"], max(rows, LANES), rows)
    pos = jnp.arange(tm) % rows
    d_a = p["wu"].shape[1]
    bias_full = jnp.repeat(p["b_spatial"].T[pos], d_a // G_A, axis=1)
    apart, vn = _branch_a(x2d, tm, p["norm1_g"], p["wu"], p["wv"], p["wga"], p["vnorm_g"],
                          p["vnorm_b"], wmix, bias_full, p["w_a_out"])
    if seq_rows is None:
        xb, ext = x, jnp.zeros((1, p["wcur"].shape[1]), F32)
    else:
        xb = x2d.reshape(n // tm, tm, d)
        ext = jnp.repeat(shift_in, seq_rows, axis=0).reshape(n // tm, tm, -1)
    outs = _branch_b(xb, tm, seq_rows, ext, p["norm1_g"], p["wcur"], p["wgb"], p["mu_shift"],
                     p["w0"], p["w2"], p["a0"], p["a2"], p["g2"], p["k_k"], p["k_a"], p["r_k"],
                     p["bd"])
    r, k2, v, kk, kka, lw, g, bonus, sgb, cur = outs
    d_b = r.shape[-1]
    if seq_rows is None:
        shift_out = cur[:, -1, :]
        scan_in = [a.reshape(nb, t, d_b) for a in (r, k2, v, kk, kka, lw)]
    else:
        shift_out = cur.reshape(nb, t, -1)[:, -1, :]
        pad = scan_chunk - t
        scan_in = [jnp.pad(a.reshape(nb, t, d_b), ((0, 0), (0, pad), (0, 0)))
                   for a in (r, k2, v, kk, kka, lw)]
    y, s_out = _scan(*scan_in, wkv_in, scan_chunk, scan_seqs)
    y2d = y[:, :t].reshape(n, d_b)
    flat = lambda a: a.reshape(n, a.shape[-1])
    routed = _merge(y2d, flat(g), flat(bonus), flat(sgb), apart, x2d, tm, p["lnx_g"],
                    p["lnx_b"], p["bd"], p["w_b_out"], p["w_out"], p["norm2_g"],
                    p["router_w"], p["router_b"])
    return routed, vn, shift_out, s_out


def kernel(x_prompt, x_sample, state_shift, state_wkv, norm1_g, w_in, mu_shift, vnorm_g, vnorm_b, w_spatial, b_spatial, w_a_out, w0, w2, a0, a2, g2, k_k, k_a, r_k, lnx_g, lnx_b, w_b_out, w_out, norm2_g, router_w, router_b, exp_w_gu, exp_b_gu, exp_w_down, exp_b_down, normf_g):
    depth = w_in.shape[0]
    assert depth == 1, "the final norm is fused into the single layer's MoE call"
    d_model = x_prompt.shape[-1]
    d_b = w0.shape[-1]
    shift_w = mu_shift.shape[-1]
    d_a = vnorm_g.shape[-1]
    bp, tp, _ = x_prompt.shape
    bs, ts, _ = x_sample.shape
    head_id = jnp.arange(2 * LANES) // HEAD_B
    bd = (head_id[:, None] == head_id[None, :]).astype(BF16)
    row = lambda a: a.reshape(1, -1)

    hp, hs = x_prompt, x_sample
    vrows, shifts_p, wkvs_p, shifts_s, wkvs_s = [], [], [], [], []
    for l in range(depth):
        wi = w_in[l].astype(BF16)
        o = shift_w
        p = dict(
            norm1_g=row(norm1_g[l]), wcur=wi[:, :o], wu=wi[:, o:o + d_a],
            wv=wi[:, o + d_a:o + 2 * d_a], wga=wi[:, o + 2 * d_a:o + 2 * d_a + d_model],
            wgb=wi[:, o + 2 * d_a + d_model:], mu_shift=row(mu_shift[l]),
            vnorm_g=row(vnorm_g[l]), vnorm_b=row(vnorm_b[l]), w_spatial=w_spatial[l],
            b_spatial=b_spatial[l], w_a_out=w_a_out[l].astype(BF16), w0=row(w0[l]),
            w2=_stack_rhs3(w2[l]), a0=row(a0[l]), a2=_stack_rhs3(a2[l]), g2=_stack_rhs3(g2[l]),
            k_k=row(k_k[l]), k_a=row(k_a[l]),
            r_k=row(r_k[l]), lnx_g=row(lnx_g[l]), lnx_b=row(lnx_b[l]),
            w_b_out=w_b_out[l].astype(BF16), w_out=w_out[l].astype(BF16),
            norm2_g=row(norm2_g[l]), router_w=jnp.concatenate(_split(router_w[l]), axis=1),
            router_b=row(router_b[l]), bd=bd)
        routed_p, _, sh_p, s_p = _stream(hp, None, None, p, 512, None, 64, 4)
        routed_s, vn_s, sh_s, s_s = _stream(hs, state_shift[l], state_wkv[l], p, 512, ts, 8, 8)
        h_p, xp_p, topi_p, prob_p, rank_p, cnt_p = routed_p
        h_s, xp_s, topi_s, prob_s, rank_s, cnt_s = routed_s
        n_p, n_s = h_p.shape[0], h_s.shape[0]
        n_e = router_w.shape[-1]
        n_tiles = (n_p + n_s) * TOP_K // MOE_TM + n_e
        pos_p, pos_s, tile0, cnt = _route(topi_p, rank_p, cnt_p, topi_s, rank_s, cnt_s)
        pos_t = jnp.concatenate([pos_p, pos_s], axis=0).T
        xs = _scatter_rows([xp_p, xp_s], pos_t, n_tiles * MOE_TM)
        ys = _moe(xs, tile0, cnt, exp_w_gu[l], exp_b_gu[l][:, None, :],
                  exp_w_down[l], exp_b_down[l][:, None, :])
        yg = _gather_rows(ys, pos_t.reshape(-1)).reshape(TOP_K, n_p + n_s, d_model // 2)
        gf = row(normf_g)
        hp = _combine(h_p, yg, 0, prob_p, gf, 512).reshape(bp, tp, d_model)
        hs = _combine(h_s, yg, n_p, prob_s, gf, 512).reshape(bs, ts, d_model)
        vrows.append(vn_s.reshape(bs, ts, d_a))
        shifts_p.append(sh_p)
        wkvs_p.append(s_p)
        shifts_s.append(sh_s)
        wkvs_s.append(s_s)
    return (hp, hs, jnp.stack(shifts_p), jnp.stack(wkvs_p), jnp.stack(shifts_s),
            jnp.stack(wkvs_s), jnp.stack(vrows))
```

```python
import functools

import jax
import jax.numpy as jnp
from jax import lax
from jax.experimental import pallas as pl
from jax.experimental.pallas import tpu as pltpu
from jax.experimental.pallas import tpu_sc as plsc

F32 = jnp.float32
BF16 = jnp.bfloat16

CHUNK = 128
G_A = 8
HEAD_B = 64
R_W, R_A, R_G = 64, 64, 128
TOP_K = 4
SWIGLU_LIMIT = 7.0
SWIGLU_ALPHA = 1.702
EPS = 1e-5
GN_EPS = HEAD_B * 1e-5

LANES = 128
PAIR = 2 * HEAD_B
VMEM_LIMIT = 56 * 1024 * 1024


def _dot(a, b):
    return jnp.dot(a.astype(BF16), b.astype(BF16), preferred_element_type=F32)


def _split(x):
    hi = x.astype(BF16)
    lo = (x - hi.astype(F32)).astype(BF16)
    return hi, lo


def _dot3(a, b):
    ah, al = _split(a)
    bh, bl = _split(b)
    return (jnp.dot(ah, bh, preferred_element_type=F32)
            + jnp.dot(ah, bl, preferred_element_type=F32)
            + jnp.dot(al, bh, preferred_element_type=F32))


def _dot_exact_rhs(a, b_bf16):
    ah, al = _split(a)
    return (jnp.dot(ah, b_bf16, preferred_element_type=F32)
            + jnp.dot(al, b_bf16, preferred_element_type=F32))


def _stack_rhs3(b):
    bh, bl = _split(b)
    return jnp.concatenate([bh, bh, bl], axis=0)


def _dot3_stacked(a, b_stacked):
    ah, al = _split(a)
    return jnp.dot(jnp.concatenate([ah, al, ah], axis=1), b_stacked, preferred_element_type=F32)


def _head_sums(x, ones_bd):
    w = ones_bd.shape[0]
    return jnp.concatenate([_dot_exact_rhs(x[:, c:c + w], ones_bd)
                            for c in range(0, x.shape[1], w)], axis=1)


def _dot_exact_lhs(a_bf16, b):
    bh, bl = _split(b)
    bm = b - bh.astype(F32) - bl.astype(F32)
    return (jnp.dot(a_bf16, bh, preferred_element_type=F32)
            + jnp.dot(a_bf16, bl, preferred_element_type=F32)
            + jnp.dot(a_bf16, bm.astype(BF16), preferred_element_type=F32))


def _rms(x, g):
    return x * lax.rsqrt(jnp.mean(x * x, axis=-1, keepdims=True) + EPS) * g


def _sigmoid(x):
    return 1.0 / (1.0 + jnp.exp(-x))


def _softplus(z):
    return jnp.maximum(z, 0.0) + jnp.log(1.0 + jnp.exp(-jnp.abs(z)))


def _full_spec(shape):
    nd = len(shape)
    return pl.BlockSpec(shape, lambda *_: (0,) * nd, pipeline_mode=pl.Buffered(1))


def _branch_a_kernel(x_ref, g1_ref, wu_ref, wv_ref, wga_ref, vg_ref, vb_ref, wmix_ref,
                     bias_ref, wao_ref, apart_ref, vn_ref):
    xn = _rms(x_ref[...], g1_ref[...]).astype(BF16)
    v = jnp.dot(xn, wv_ref[...], preferred_element_type=F32)
    mu = jnp.mean(v, axis=-1, keepdims=True)
    d = v - mu
    var = jnp.mean(d * d, axis=-1, keepdims=True)
    vn = d * lax.rsqrt(var + EPS) * vg_ref[...] + vb_ref[...]
    vn_ref[...] = vn
    vnb = vn.astype(BF16)
    c_a = vnb.shape[1] // G_A
    rb = wmix_ref.shape[1]
    mixed = jnp.concatenate(
        [jnp.concatenate(
            [jnp.dot(wmix_ref[g], vnb[r0:r0 + rb, g * c_a:(g + 1) * c_a],
                     preferred_element_type=F32) for g in range(G_A)], axis=1)
         for r0 in range(0, vnb.shape[0], rb)], axis=0) + bias_ref[...]
    u = jnp.dot(xn, wu_ref[...], preferred_element_type=F32)
    ya = (u * mixed).astype(BF16)
    ga = jnp.dot(xn, wga_ref[...], preferred_element_type=F32)
    apart_ref[...] = _sigmoid(ga) * jnp.dot(ya, wao_ref[...], preferred_element_type=F32)


def _branch_a(x2d, tm, g1, wu, wv, wga, vg, vb, wmix, bias_full, wao):
    n, d = x2d.shape
    d_a = wu.shape[1]
    tok = lambda w: pl.BlockSpec((tm, w), lambda i: (i, 0))
    return pl.pallas_call(
        _branch_a_kernel,
        grid=(n // tm,),
        in_specs=[tok(d), _full_spec(g1.shape), _full_spec(wu.shape), _full_spec(wv.shape),
                  _full_spec(wga.shape), _full_spec(vg.shape), _full_spec(vb.shape),
                  _full_spec(wmix.shape), _full_spec(bias_full.shape), _full_spec(wao.shape)],
        out_specs=[tok(d), tok(d_a)],
        out_shape=[jax.ShapeDtypeStruct((n, d), F32), jax.ShapeDtypeStruct((n, d_a), F32)],
        compiler_params=pltpu.CompilerParams(dimension_semantics=("parallel",),
                                             vmem_limit_bytes=VMEM_LIMIT),
        name="branch_a",
    )(x2d, g1, wu, wv, wga, vg, vb, wmix, bias_full, wao)


def _branch_b_kernel(x_ref, g1_ref, wcur_ref, wgb_ref, mu_ref, ext_ref, w0_ref, w2_ref,
                     a0_ref, a2_ref, g2_ref, kk_ref, ka_ref, rk_ref, bd_ref,
                     r_out, k_out, v_out, kk_out, kka_out, lw_out, g_out, bonus_out,
                     sgb_out, cur_out, carry_scr, *, seq_rows, d_b):
    tm = x_ref.shape[0]
    xn = _rms(x_ref[...], g1_ref[...]).astype(BF16)
    cur = jnp.dot(xn, wcur_ref[...], preferred_element_type=F32)
    rolled = pltpu.roll(cur, 1, axis=0)
    row = lax.broadcasted_iota(jnp.int32, (tm, 1), 0)
    if seq_rows is None:
        first_tile = pl.program_id(1) == 0
        carry = jnp.where(first_tile, 0.0, carry_scr[...])
        prev = jnp.where(row == 0, carry, rolled)
        carry_scr[...] = cur[tm - 1:tm, :]
        cur_out[...] = cur[tm - 8:tm, :]
    else:
        prev = jnp.where(row % seq_rows == 0, ext_ref[...], rolled)
        cur_out[...] = cur
    xs = cur + (prev - cur) * mu_ref[...]
    r = xs[:, 0:d_b]
    k = xs[:, d_b:2 * d_b]
    v = xs[:, 2 * d_b:3 * d_b]
    o = 3 * d_b
    xw = xs[:, o:o + R_W]
    xa = xs[:, o + R_W:o + R_W + R_A]
    xg = xs[:, o + R_W + R_A:o + R_W + R_A + R_G]
    wl = w0_ref[...] + _dot3_stacked(jnp.tanh(xw), w2_ref[...])
    lw = -jnp.exp(-_softplus(-wl) - 0.5)
    a = _sigmoid(a0_ref[...] + _dot3_stacked(xa, a2_ref[...]))
    g = _dot3_stacked(_sigmoid(xg), g2_ref[...])
    bd = bd_ref[...]
    kkr = k * kk_ref[...]
    n2 = _head_sums(kkr * kkr, bd)
    kk = kkr / jnp.maximum(jnp.sqrt(n2), 1e-12)
    k2 = k * (1.0 + (a - 1.0) * ka_ref[...])
    bonus = _head_sums(r * k2 * rk_ref[...], bd) * v
    r_out[...] = r
    k_out[...] = k2
    v_out[...] = v
    kk_out[...] = kk
    kka_out[...] = kk * a
    lw_out[...] = lw
    g_out[...] = g
    bonus_out[...] = bonus
    sgb_out[...] = _sigmoid(jnp.dot(xn, wgb_ref[...], preferred_element_type=F32))


def _branch_b(x3d, tm, seq_rows, ext, g1, wcur, wgb, mu, w0, w2, a0, a2, g2, k_k, k_a, r_k, bd):
    nb, t, d = x3d.shape
    d_b = w0.shape[1]
    shift_w = wcur.shape[1]
    nt = t // tm
    tok = lambda w: pl.BlockSpec((None, tm, w), lambda b, i: (b, i, 0))
    cur_rows = 8 if seq_rows is None else tm
    outs = [jax.ShapeDtypeStruct((nb, t, d_b), F32)] * 8 + [
        jax.ShapeDtypeStruct((nb, t, d), F32),
        jax.ShapeDtypeStruct((nb, nt * cur_rows, shift_w), F32)]
    out_specs = [tok(d_b)] * 8 + [tok(d), pl.BlockSpec((None, cur_rows, shift_w),
                                                       lambda b, i: (b, i, 0))]
    weights = (g1, wcur, wgb, mu)
    small = (w0, w2, a0, a2, g2, k_k, k_a, r_k, bd)
    return pl.pallas_call(
        functools.partial(_branch_b_kernel, seq_rows=seq_rows, d_b=d_b),
        grid=(nb, nt),
        in_specs=[tok(d)] + [_full_spec(w.shape) for w in weights]
        + [tok(shift_w) if seq_rows is not None else _full_spec(ext.shape)]
        + [_full_spec(w.shape) for w in small],
        out_specs=out_specs,
        out_shape=outs,
        scratch_shapes=[pltpu.VMEM((1, shift_w), F32)],
        compiler_params=pltpu.CompilerParams(dimension_semantics=("parallel", "arbitrary"),
                                             vmem_limit_bytes=VMEM_LIMIT),
        name="branch_b",
    )(x3d, *weights, ext, *small)


def _scan_kernel(*refs, chunk, zero_state):
    r_ref, k_ref, v_ref, kk_ref, kka_ref, lw_ref = refs[:6]
    s0_ref = None if zero_state else refs[6]
    y_ref, sout_ref, s_scr = refs[-3:]
    L = chunk
    W2 = 2 * L
    c = pl.program_id(1)
    n_seq, n_pairs = s_scr.shape[:2]

    @pl.when(c == 0)
    def _():
        if zero_state:
            s_scr[...] = jnp.zeros_like(s_scr)
        else:
            z = jnp.zeros((HEAD_B, HEAD_B), F32)
            for s in range(n_seq):
                for p in range(n_pairs):
                    nat = jnp.concatenate(
                        [jnp.concatenate([s0_ref[s, 2 * p], z], axis=1),
                         jnp.concatenate([z, s0_ref[s, 2 * p + 1]], axis=1)], axis=0)
                    s_scr[s, p] = nat.T

    iota = lambda shape, dim: lax.broadcasted_iota(jnp.int32, shape, dim)
    stack_mask = (iota((W2, 1), 0) >= L) == (iota((1, PAIR), 1) >= HEAD_B)
    bd_mask = (iota((W2, 1), 0) >= L) == (iota((1, W2), 1) >= L)
    assert L & (L - 1) == 0, "chunk length must be a power of two"
    rw = iota((L, W2), 0)
    cw = iota((L, W2), 1) & (L - 1)
    strict_w = cw < rw
    eye_w = (cw == rw).astype(F32)
    incl_w2 = (iota((L, 2 * W2), 1) & (L - 1)) <= iota((L, 2 * W2), 0)
    tri = (iota((L, L), 1) <= iota((L, L), 0)).astype(BF16)
    di = iota((PAIR, PAIR), 0)
    dj = iota((PAIR, PAIR), 1)
    diag = di == dj
    head_diag = (di >= HEAD_B) == (dj >= HEAD_B)

    def stack(x):
        return jnp.where(stack_mask, jnp.concatenate([x, x], axis=0), 0.0)

    def bd(xw):
        return jnp.where(bd_mask, jnp.concatenate([xw, xw], axis=0), 0.0)

    n_double = max(L.bit_length() - 2, 0)
    prep = []
    for s in range(n_seq):
        lw_all = lw_ref[s]
        cum_all = _dot_exact_lhs(tri, lw_all)
        for p in range(n_pairs):
            sl = slice(p * PAIR, (p + 1) * PAIR)
            cum = cum_all[:, sl]
            cum_last = cum[L - 1:L, :]
            e_neg = jnp.exp(-cum)
            e_rel = jnp.exp(cum_last - cum)
            kk = kk_ref[s, :, sl]
            kka = kka_ref[s, :, sl]
            kx = k_ref[s, :, sl]
            prep.append(dict(
                s=s, p=p, sl=sl, ab=-kk * jnp.exp(cum - lw_all[:, sl]),
                rb=r_ref[s, :, sl] * jnp.exp(cum), bb=kka * e_neg, kb=kx * e_neg,
                bt=kka * e_rel, kt=kx * e_rel, v=v_ref[s, :, sl], decay=jnp.exp(cum_last)))
    a_w = [lax.dot_general(
        jnp.concatenate([q["ab"], q["rb"]], axis=0).astype(BF16),
        jnp.concatenate([stack(q["bb"]), stack(q["kb"])], axis=0).astype(BF16),
        (((1,), (1,)), ((), ())), preferred_element_type=F32) for q in prep]
    nw = [jnp.where(strict_w, a[:L, :W2], 0.0) for a in a_w]
    tw = [eye_w + n for n in nw]
    nbd = [bd(n) for n in nw]
    for _ in range(n_double):
        nw = [_dot(n, b) for n, b in zip(nw, nbd)]
        nbd = [bd(n) for n in nw]
        tw = [t + _dot(t, b) for t, b in zip(tw, nbd)]
    akv = [_dot(jnp.where(strict_w, a[:L, W2:], 0.0), stack(q["v"])) for a, q in zip(a_w, prep)]
    tx = [_dot(t, jnp.concatenate([stack(q["ab"]), stack(kv)], axis=1))
          for t, q, kv in zip(tw, prep, akv)]
    ry = [_dot(jnp.where(incl_w2, a[L:], 0.0),
               jnp.concatenate(
                   [jnp.concatenate([stack(x[:, :PAIR]), stack(x[:, PAIR:])], axis=1),
                    jnp.concatenate([jnp.zeros((W2, PAIR), F32), stack(q["v"])], axis=1)], axis=0))
          for a, x, q in zip(a_w, tx, prep)]
    for q, x, y in zip(prep, tx, ry):
        st = s_scr[q["s"], q["p"]]
        us = _dot(jnp.concatenate([x[:, :PAIR], q["rb"] + y[:, :PAIR]], axis=0), st)
        y_ref[q["s"], :, q["sl"]] = us[L:] + y[:, PAIR:]
        u = us[:L] + x[:, PAIR:]
        lhs = jnp.concatenate([jnp.where(diag, q["decay"], 0.0),
                               jnp.concatenate([q["bt"], q["kt"]], axis=0).T], axis=1)
        g = _dot(lhs, jnp.concatenate([st, u, q["v"]], axis=0))
        s_scr[q["s"], q["p"]] = jnp.where(head_diag, g, 0.0)

    @pl.when(c == pl.num_programs(1) - 1)
    def _():
        for s in range(n_seq):
            for p in range(n_pairs):
                nat = s_scr[s, p].T
                sout_ref[s, 2 * p] = nat[:HEAD_B, :HEAD_B]
                sout_ref[s, 2 * p + 1] = nat[HEAD_B:, HEAD_B:]


def _scan(r, k, v, kk, kka, lw, s0, chunk, n_seq):
    nb, t, d_b = r.shape
    n_pairs = d_b // PAIR
    n_heads = d_b // HEAD_B
    tok = pl.BlockSpec((n_seq, chunk, d_b), lambda b, c: (b, c, 0))
    st = pl.BlockSpec((n_seq, n_heads, HEAD_B, HEAD_B), lambda b, c: (b, 0, 0, 0))
    zero_state = s0 is None
    return pl.pallas_call(
        functools.partial(_scan_kernel, chunk=chunk, zero_state=zero_state),
        grid=(nb // n_seq, t // chunk),
        in_specs=[tok] * 6 + ([] if zero_state else [st]),
        out_specs=[tok, st],
        out_shape=[jax.ShapeDtypeStruct((nb, t, d_b), F32),
                   jax.ShapeDtypeStruct((nb, n_heads, HEAD_B, HEAD_B), F32)],
        scratch_shapes=[pltpu.VMEM((n_seq, n_pairs, PAIR, PAIR), F32)],
        compiler_params=pltpu.CompilerParams(dimension_semantics=("parallel", "arbitrary"),
                                             vmem_limit_bytes=VMEM_LIMIT),
        name="rwkv_scan",
    )(r, k, v, kk, kka, lw, *(() if zero_state else (s0,)))


def _pack_bf16_pairs(x):
    w = x.shape[1] // 2
    bits = lambda v: lax.bitcast_convert_type(v.astype(BF16).astype(F32), jnp.uint32)
    return (bits(x[:, :w]) >> 16) | (bits(x[:, w:]) & jnp.uint32(0xFFFF0000))


def _unpack_bf16_pairs(u):
    lo = lax.bitcast_convert_type(u << 16, F32)
    hi = lax.bitcast_convert_type(u & jnp.uint32(0xFFFF0000), F32)
    return lo, hi


def _merge_kernel(y_ref, g_ref, bonus_ref, sgb_ref, apart_ref, x_ref, lg_ref, lb_ref, bd_ref,
                  wbo_ref, wo_ref, g2_ref, rw_ref, rb_ref,
                  h_out, xp_out, topi_out, prob_out, rank_out, cnt_out, cnt_scr):
    step = pl.program_id(0)

    @pl.when(step == 0)
    def _():
        cnt_scr[...] = jnp.zeros_like(cnt_scr)

    bd = bd_ref[...]
    y = y_ref[...]
    inv_n = 1.0 / HEAD_B
    mu = _head_sums(y, bd) * inv_n
    d = y - mu
    var = _head_sums(d * d, bd) * inv_n
    yn = d * lax.rsqrt(var + GN_EPS) * lg_ref[...] + lb_ref[...]
    yb = ((yn + bonus_ref[...]) * g_ref[...]).astype(BF16)
    merged = apart_ref[...] + sgb_ref[...] * jnp.dot(yb, wbo_ref[...], preferred_element_type=F32)
    h = x_ref[...] + jnp.dot(merged.astype(BF16), wo_ref[...], preferred_element_type=F32)
    h_out[...] = h
    xn2 = _rms(h, g2_ref[...])
    xp_out[...] = _pack_bf16_pairs(xn2)
    n_e = rb_ref.shape[1]
    xh, xl = _split(xn2)
    rw = rw_ref[...]
    t = jnp.dot(xh, rw, preferred_element_type=F32)
    logits = (t[:, :n_e] + t[:, n_e:] + jnp.dot(xl, rw[:, :n_e], preferred_element_type=F32)
              + rb_ref[...])
    tm = logits.shape[0]
    idx = lax.broadcasted_iota(jnp.int32, logits.shape, 1).astype(F32)
    work = logits
    tops, hots, sels = [], [], []
    for _ in range(TOP_K):
        m = jnp.max(work, axis=-1, keepdims=True)
        sel = jnp.min(jnp.where(work == m, idx, float(n_e)), axis=-1, keepdims=True)
        hot = idx == sel
        tops.append(m)
        hots.append(hot)
        sels.append(sel)
        work = jnp.where(hot, -jnp.inf, work)
    es = [jnp.exp(t - tops[0]) for t in tops]
    denom = es[0] + es[1] + es[2] + es[3]
    topi_out[...] = jnp.concatenate(sels, axis=1).astype(jnp.int32)
    prob_out[...] = jnp.concatenate([e / denom for e in es], axis=1)
    hot_any = jnp.zeros_like(logits)
    for hot in hots:
        hot_any = hot_any + hot.astype(F32)
    ri = lax.broadcasted_iota(jnp.int32, (tm, tm), 0)
    ci = lax.broadcasted_iota(jnp.int32, (tm, tm), 1)
    before = _dot((ci < ri).astype(BF16), hot_any) + cnt_scr[...]
    rank_out[...] = jnp.concatenate(
        [jnp.sum(jnp.where(hot, before, 0.0), axis=-1, keepdims=True) for hot in hots],
        axis=1).astype(jnp.int32)
    total = cnt_scr[...] + jnp.sum(hot_any, axis=0, keepdims=True)
    cnt_scr[...] = total
    cnt_out[...] = total.astype(jnp.int32)


def _merge(y, g, bonus, sgb, apart, x2d, tm, lnx_g, lnx_b, bd, wbo, wo, g2n, rw, rb):
    n, d = x2d.shape
    d_b = y.shape[1]
    n_e = rb.shape[1]
    tok = lambda w: pl.BlockSpec((tm, w), lambda i: (i, 0))
    weights = (lnx_g, lnx_b, bd, wbo, wo, g2n, rw, rb)
    return pl.pallas_call(
        _merge_kernel,
        grid=(n // tm,),
        in_specs=[tok(d_b), tok(d_b), tok(d_b), tok(d), tok(d), tok(d)]
        + [_full_spec(w.shape) for w in weights],
        out_specs=[tok(d), tok(d // 2), tok(TOP_K), tok(TOP_K), tok(TOP_K),
                   pl.BlockSpec((1, n_e), lambda i: (0, 0))],
        out_shape=[jax.ShapeDtypeStruct((n, d), F32),
                   jax.ShapeDtypeStruct((n, d // 2), jnp.uint32),
                   jax.ShapeDtypeStruct((n, TOP_K), jnp.int32),
                   jax.ShapeDtypeStruct((n, TOP_K), F32),
                   jax.ShapeDtypeStruct((n, TOP_K), jnp.int32),
                   jax.ShapeDtypeStruct((1, n_e), jnp.int32)],
        scratch_shapes=[pltpu.VMEM((1, n_e), F32)],
        compiler_params=pltpu.CompilerParams(dimension_semantics=("arbitrary",),
                                             vmem_limit_bytes=VMEM_LIMIT),
        name="merge_router",
    )(y, g, bonus, sgb, apart, x2d, *weights)


MOE_TM = 1024
MOE_SUB = 256
MOE_FF_CHUNK = 512
CAST_ROWS = 256


def _route(topi_p, rank_p, cnt_p, topi_s, rank_s, cnt_s):
    n_e = cnt_p.shape[-1]
    cnt_p, cnt_s = cnt_p.reshape(n_e), cnt_s.reshape(n_e)
    cnt = cnt_p + cnt_s
    padded = (cnt + MOE_TM - 1) // MOE_TM * MOE_TM
    ends = jnp.cumsum(padded)
    base = ends - padded
    lookup = lambda table, idx: jnp.sum(
        jnp.where(idx[..., None] == jnp.arange(n_e), table, 0), axis=-1)
    pos_p = lookup(base, topi_p) + rank_p
    pos_s = lookup(base + cnt_p, topi_s) + rank_s
    return (pos_p.astype(jnp.int32), pos_s.astype(jnp.int32),
            (base // MOE_TM).astype(jnp.int32), cnt.astype(jnp.int32))


def _moe_kernel(tile0_ref, cnt_ref, xs_hbm, wgu_ref, bgu_ref, wd_ref, bdn_ref, ys_hbm,
                wgu_b, wd_b, xbuf, ybuf, sem_in, sem_out, done_ref):
    e = pl.program_id(0)
    n_e = pl.num_programs(0)

    @pl.when(e == 0)
    def _():
        done_ref[0] = 0

    g0 = done_ref[0]
    tile0 = tile0_ref[e]
    cnt = cnt_ref[e]
    n_t = (cnt + MOE_TM - 1) // MOE_TM
    d, gu = wgu_ref.shape
    d_ff = wd_ref.shape[0]
    half = d // 2

    def in_copy(tile, slot):
        return pltpu.make_async_copy(xs_hbm.at[pl.ds(tile * MOE_TM, MOE_TM)], xbuf.at[slot],
                                     sem_in.at[slot])

    def out_copy(tile, slot):
        return pltpu.make_async_copy(ybuf.at[slot], ys_hbm.at[pl.ds(tile * MOE_TM, MOE_TM)],
                                     sem_out.at[slot])

    @pl.when(jnp.logical_and(e == 0, n_t > 0))
    def _():
        in_copy(tile0, 0).start()

    @pl.when(n_t > 0)
    def _():
        for r0 in range(0, d, CAST_ROWS):
            wgu_b[r0:r0 + CAST_ROWS, :] = wgu_ref[r0:r0 + CAST_ROWS, :].astype(BF16)
        for r0 in range(0, d_ff, CAST_ROWS):
            wd_b[r0:r0 + CAST_ROWS, :] = wd_ref[r0:r0 + CAST_ROWS, :].astype(BF16)

    def tile_step(j, carry):
        slot = (g0 + j) % 2
        in_copy(tile0 + j, slot).wait()

        @pl.when(j + 1 < n_t)
        def _():
            in_copy(tile0 + j + 1, 1 - slot).start()

        @pl.when(g0 + j >= 2)
        def _():
            out_copy(0, slot).wait()

        left = cnt - j * MOE_TM

        def ffn(row0, rows):
            lo, hi = _unpack_bf16_pairs(xbuf[slot, pl.ds(row0, rows)])
            valid = lax.broadcasted_iota(jnp.int32, lo.shape, 0) < left - row0
            x_lo = jnp.where(valid, lo, 0.0).astype(BF16)
            x_hi = jnp.where(valid, hi, 0.0).astype(BF16)

            def proj(c0):
                cols = slice(c0, c0 + MOE_FF_CHUNK)
                return (jnp.dot(x_lo, wgu_b[:half, cols], preferred_element_type=F32)
                        + jnp.dot(x_hi, wgu_b[half:, cols], preferred_element_type=F32)
                        + bgu_ref[:, cols])

            y = jnp.zeros((rows, d), F32)
            for f in range(d_ff // MOE_FF_CHUNK):
                c0 = f * MOE_FF_CHUNK
                gate = jnp.minimum(proj(c0), SWIGLU_LIMIT)
                up = jnp.clip(proj(d_ff + c0), -SWIGLU_LIMIT, SWIGLU_LIMIT)
                hh = (up + 1.0) * gate * _sigmoid(gate * SWIGLU_ALPHA)
                y = y + jnp.dot(hh.astype(BF16), wd_b[c0:c0 + MOE_FF_CHUNK, :],
                                preferred_element_type=F32)
            ybuf[slot, pl.ds(row0, rows)] = _pack_bf16_pairs(y + bdn_ref[...])

        n_sub = (jnp.minimum(left, MOE_TM) + MOE_SUB - 1) // MOE_SUB

        @pl.when(n_sub == MOE_TM // MOE_SUB)
        def _():
            ffn(0, MOE_TM)

        @pl.when(n_sub < MOE_TM // MOE_SUB)
        def _():
            def sub_step(i, c):
                ffn(pl.multiple_of(i * MOE_SUB, MOE_SUB), MOE_SUB)
                return c
            lax.fori_loop(0, n_sub, sub_step, 0)

        out_copy(tile0 + j, slot).start()
        return carry

    lax.fori_loop(0, n_t, tile_step, 0)

    g1 = g0 + n_t
    done_ref[0] = g1
    nxt = jnp.minimum(e + 1, n_e - 1)

    @pl.when(jnp.logical_and(e + 1 < n_e, cnt_ref[nxt] > 0))
    def _():
        in_copy(tile0_ref[nxt], g1 % 2).start()

    @pl.when(jnp.logical_and(e == n_e - 1, g1 >= 2))
    def _():
        out_copy(0, g1 % 2).wait()

    @pl.when(jnp.logical_and(e == n_e - 1, g1 >= 1))
    def _():
        out_copy(0, (g1 - 1) % 2).wait()


def _moe(xs, tile0, cnt, wgu, bgu, wd, bdn):
    p_rows, half = xs.shape
    n_e, d, gu = wgu.shape
    d_ff = wd.shape[1]
    w_blk = lambda e, tile0, cnt: (e, 0, 0)
    grid_spec = pltpu.PrefetchScalarGridSpec(
        num_scalar_prefetch=2,
        grid=(n_e,),
        in_specs=[pl.BlockSpec(memory_space=pl.ANY),
                  pl.BlockSpec((None, d, gu), w_blk),
                  pl.BlockSpec((None, 1, gu), w_blk),
                  pl.BlockSpec((None, d_ff, d), w_blk),
                  pl.BlockSpec((None, 1, d), w_blk)],
        out_specs=pl.BlockSpec(memory_space=pl.ANY),
        scratch_shapes=[pltpu.VMEM((d, gu), BF16), pltpu.VMEM((d_ff, d), BF16),
                        pltpu.VMEM((2, MOE_TM, half), jnp.uint32),
                        pltpu.VMEM((2, MOE_TM, half), jnp.uint32),
                        pltpu.SemaphoreType.DMA((2,)), pltpu.SemaphoreType.DMA((2,)),
                        pltpu.SMEM((1,), jnp.int32)])
    return pl.pallas_call(
        _moe_kernel,
        grid_spec=grid_spec,
        out_shape=jax.ShapeDtypeStruct((p_rows, half), jnp.uint32),
        compiler_params=pltpu.CompilerParams(dimension_semantics=("arbitrary",),
                                             vmem_limit_bytes=VMEM_LIMIT),
        name="moe_experts",
    )(tile0, cnt, xs, wgu, bgu, wd, bdn)


def _combine_kernel(h_ref, yg_ref, prob_ref, gf_ref, out_ref):
    half = h_ref.shape[1] // 2
    prob = prob_ref[...]
    acc_lo = jnp.zeros((h_ref.shape[0], half), F32)
    acc_hi = jnp.zeros((h_ref.shape[0], half), F32)
    for k in range(TOP_K):
        lo, hi = _unpack_bf16_pairs(yg_ref[k])
        pk = prob[:, k:k + 1]
        acc_lo = acc_lo + pk * lo
        acc_hi = acc_hi + pk * hi
    z = h_ref[...] + jnp.concatenate([acc_lo, acc_hi], axis=1)
    out_ref[...] = _rms(z, gf_ref[...])


def _combine(h, yg, row0, prob, gf, tm):
    n, d = h.shape
    tok = lambda w: pl.BlockSpec((tm, w), lambda i: (i, 0))
    t0 = row0 // tm
    return pl.pallas_call(
        _combine_kernel,
        grid=(n // tm,),
        in_specs=[tok(d), pl.BlockSpec((TOP_K, tm, d // 2), lambda i: (0, i + t0, 0)),
                  tok(TOP_K), _full_spec(gf.shape)],
        out_specs=tok(d),
        out_shape=jax.ShapeDtypeStruct((n, d), F32),
        compiler_params=pltpu.CompilerParams(dimension_semantics=("parallel",),
                                             vmem_limit_bytes=VMEM_LIMIT),
        name="moe_combine",
    )(h, yg, prob, gf)


SC_CORES = 2
SC_SUBCORES = 16
SC_WORKERS = SC_CORES * SC_SUBCORES
SC_MAX_INDEX = 128


def _sc_chunk(rows_per_worker, limit=SC_MAX_INDEX):
    for c in range(limit, 7, -8):
        if rows_per_worker % c == 0:
            return c
    raise ValueError(f"no 8-aligned chunk divides {rows_per_worker} rows")


def _sc_mesh():
    return plsc.VectorSubcoreMesh(core_axis_name="c", subcore_axis_name="s")


def _sc_worker():
    return lax.axis_index("s") * SC_CORES + lax.axis_index("c")


def _scatter_rows(xs_in, pos_t, p_rows):
    w = xs_in[0].shape[1]
    dtype = xs_in[0].dtype
    n_k, n = pos_t.shape
    pos_flat = pos_t.reshape(n_k * n)
    parts, row0 = [], 0
    for x in xs_in:
        per = x.shape[0] // SC_WORKERS
        parts.append((row0, per, _sc_chunk(per, SC_MAX_INDEX // 2)))
        row0 += x.shape[0]
    max_chunk = max(c for _, _, c in parts)
    max_per = max(p for _, p, _ in parts)

    def body(*refs):
        x_hbms = refs[:len(xs_in)]
        pos_hbm, out_hbm, idx_v, rows_v, sem_l, sem_s = refs[len(xs_in):]
        for x_hbm, (tok0, per, chunk) in zip(x_hbms, parts):
            n_chunks = per // chunk
            base = pl.multiple_of(_sc_worker() * per, 8)
            for k in range(n_k):
                src = pl.multiple_of(k * n + tok0 + base, 8)
                pltpu.sync_copy(pos_hbm.at[pl.ds(src, per)], idx_v.at[pl.ds(k * per, per)])

            def load(c, slot):
                src = x_hbm.at[pl.ds(pl.multiple_of(base + c * chunk, 8), chunk)]
                return pltpu.make_async_copy(src, rows_v.at[slot, pl.ds(0, chunk)], sem_l.at[slot])

            def scatter(c, k, slot):
                rows = idx_v.at[pl.ds(pl.multiple_of(k * per + c * chunk, 8), chunk)]
                return pltpu.make_async_copy(rows_v.at[slot, pl.ds(0, chunk)], out_hbm.at[rows],
                                             sem_s.at[slot])

            load(0, 0).start()

            @pl.loop(0, n_chunks)
            def _(c):
                slot = c % 2
                load(c, slot).wait()

                @pl.when(c + 1 < n_chunks)
                def _():
                    @pl.when(c >= 1)
                    def _():
                        for k in range(n_k):
                            scatter(c - 1, k, 1 - slot).wait()
                    load(c + 1, 1 - slot).start()

                for k in range(n_k):
                    scatter(c, k, slot).start()

            if n_chunks >= 2:
                for k in range(n_k):
                    scatter(n_chunks - 2, k, n_chunks % 2).wait()
            for k in range(n_k):
                scatter(n_chunks - 1, k, (n_chunks - 1) % 2).wait()

    return pl.kernel(
        body, out_type=jax.ShapeDtypeStruct((p_rows, w), dtype), mesh=_sc_mesh(),
        scratch_types=[pltpu.VMEM((n_k * max_per,), jnp.int32),
                       pltpu.VMEM((2, max_chunk, w), dtype),
                       pltpu.SemaphoreType.DMA((2,)), pltpu.SemaphoreType.DMA((2,))],
        name="sc_scatter_rows")(*xs_in, pos_flat)


def _gather_rows(table, idx):
    n = idx.shape[0]
    w = table.shape[1]
    per = n // SC_WORKERS
    chunk = _sc_chunk(per)

    n_chunks = per // chunk

    def body(table_hbm, idx_hbm, out_hbm, idx_v, rows_v, sem_g, sem_w):
        base = pl.multiple_of(_sc_worker() * per, 8)
        pltpu.sync_copy(idx_hbm.at[pl.ds(base, per)], idx_v)

        def gather(c, slot):
            rows = idx_v.at[pl.ds(pl.multiple_of(c * chunk, 8), chunk)]
            return pltpu.make_async_copy(table_hbm.at[rows], rows_v.at[slot], sem_g.at[slot])

        def write(c, slot):
            dst = out_hbm.at[pl.ds(pl.multiple_of(base + c * chunk, 8), chunk)]
            return pltpu.make_async_copy(rows_v.at[slot], dst, sem_w.at[slot])

        gather(0, 0).start()

        @pl.loop(0, n_chunks)
        def _(c):
            slot = c % 2
            gather(c, slot).wait()

            @pl.when(c + 1 < n_chunks)
            def _():
                @pl.when(c >= 1)
                def _():
                    write(c - 1, 1 - slot).wait()
                gather(c + 1, 1 - slot).start()

            write(c, slot).start()

        if n_chunks >= 2:
            write(n_chunks - 2, n_chunks % 2).wait()
        write(n_chunks - 1, (n_chunks - 1) % 2).wait()

    return pl.kernel(
        body, out_type=jax.ShapeDtypeStruct((n, w), table.dtype), mesh=_sc_mesh(),
        scratch_types=[pltpu.VMEM((per,), jnp.int32), pltpu.VMEM((2, chunk, w), table.dtype),
                       pltpu.SemaphoreType.DMA((2,)), pltpu.SemaphoreType.DMA((2,))],
        name="sc_gather_rows")(table, idx)


def _mix_matrix(ws, block, seq_rows):
    causal = jnp.tril(jnp.ones((CHUNK, CHUNK), dtype=bool))
    w = jnp.where(causal[None], ws, 0.0)[:, :seq_rows, :seq_rows]
    eye = jnp.eye(block // seq_rows, dtype=ws.dtype)
    return jnp.einsum("ab,gts->gatbs", eye, w).reshape(ws.shape[0], block, block).astype(BF16)


def _stream(x, shift_in, wkv_in, p, tm, seq_rows, scan_chunk, scan_seqs):
    nb, t, d = x.shape
    n = nb * t
    x2d = x.reshape(n, d)
    rows = CHUNK if seq_rows is None else seq_rows
    wmix = _mix_matrix(p["w_spatial"], max(rows, LANES), rows)
    pos = jnp.arange(tm) % rows
    d_a = p["wu"].shape[1]
    bias_full = jnp.repeat(p["b_spatial"].T[pos], d_a // G_A, axis=1)
    apart, vn = _branch_a(x2d, tm, p["norm1_g"], p["wu"], p["wv"], p["wga"], p["vnorm_g"],
                          p["vnorm_b"], wmix, bias_full, p["w_a_out"])
    if seq_rows is None:
        xb, ext = x, jnp.zeros((1, p["wcur"].shape[1]), F32)
    else:
        xb = x2d.reshape(n // tm, tm, d)
        ext = jnp.repeat(shift_in, seq_rows, axis=0).reshape(n // tm, tm, -1)
    outs = _branch_b(xb, tm, seq_rows, ext, p["norm1_g"], p["wcur"], p["wgb"], p["mu_shift"],
                     p["w0"], p["w2"], p["a0"], p["a2"], p["g2"], p["k_k"], p["k_a"], p["r_k"],
                     p["bd"])
    r, k2, v, kk, kka, lw, g, bonus, sgb, cur = outs
    d_b = r.shape[-1]
    if seq_rows is None:
        shift_out = cur[:, -1, :]
        scan_in = [a.reshape(nb, t, d_b) for a in (r, k2, v, kk, kka, lw)]
    else:
        shift_out = cur.reshape(nb, t, -1)[:, -1, :]
        pad = scan_chunk - t
        scan_in = [jnp.pad(a.reshape(nb, t, d_b), ((0, 0), (0, pad), (0, 0)))
                   for a in (r, k2, v, kk, kka, lw)]
    y, s_out = _scan(*scan_in, wkv_in, scan_chunk, scan_seqs)
    y2d = y[:, :t].reshape(n, d_b)
    flat = lambda a: a.reshape(n, a.shape[-1])
    routed = _merge(y2d, flat(g), flat(bonus), flat(sgb), apart, x2d, tm, p["lnx_g"],
                    p["lnx_b"], p["bd"], p["w_b_out"], p["w_out"], p["norm2_g"],
                    p["router_w"], p["router_b"])
    return routed, vn, shift_out, s_out


def kernel(x_prompt, x_sample, state_shift, state_wkv, norm1_g, w_in, mu_shift, vnorm_g, vnorm_b, w_spatial, b_spatial, w_a_out, w0, w2, a0, a2, g2, k_k, k_a, r_k, lnx_g, lnx_b, w_b_out, w_out, norm2_g, router_w, router_b, exp_w_gu, exp_b_gu, exp_w_down, exp_b_down, normf_g):
    depth = w_in.shape[0]
    assert depth == 1, "the final norm is fused into the single layer's MoE call"
    d_model = x_prompt.shape[-1]
    d_b = w0.shape[-1]
    shift_w = mu_shift.shape[-1]
    d_a = vnorm_g.shape[-1]
    bp, tp, _ = x_prompt.shape
    bs, ts, _ = x_sample.shape
    head_id = jnp.arange(2 * LANES) // HEAD_B
    bd = (head_id[:, None] == head_id[None, :]).astype(BF16)
    row = lambda a: a.reshape(1, -1)

    hp, hs = x_prompt, x_sample
    vrows, shifts_p, wkvs_p, shifts_s, wkvs_s = [], [], [], [], []
    for l in range(depth):
        wi = w_in[l].astype(BF16)
        o = shift_w
        p = dict(
            norm1_g=row(norm1_g[l]), wcur=wi[:, :o], wu=wi[:, o:o + d_a],
            wv=wi[:, o + d_a:o + 2 * d_a], wga=wi[:, o + 2 * d_a:o + 2 * d_a + d_model],
            wgb=wi[:, o + 2 * d_a + d_model:], mu_shift=row(mu_shift[l]),
            vnorm_g=row(vnorm_g[l]), vnorm_b=row(vnorm_b[l]), w_spatial=w_spatial[l],
            b_spatial=b_spatial[l], w_a_out=w_a_out[l].astype(BF16), w0=row(w0[l]),
            w2=_stack_rhs3(w2[l]), a0=row(a0[l]), a2=_stack_rhs3(a2[l]), g2=_stack_rhs3(g2[l]),
            k_k=row(k_k[l]), k_a=row(k_a[l]),
            r_k=row(r_k[l]), lnx_g=row(lnx_g[l]), lnx_b=row(lnx_b[l]),
            w_b_out=w_b_out[l].astype(BF16), w_out=w_out[l].astype(BF16),
            norm2_g=row(norm2_g[l]), router_w=jnp.concatenate(_split(router_w[l]), axis=1),
            router_b=row(router_b[l]), bd=bd)
        routed_p, _, sh_p, s_p = _stream(hp, None, None, p, 512, None, 64, 4)
        routed_s, vn_s, sh_s, s_s = _stream(hs, state_shift[l], state_wkv[l], p, 512, ts, 8, 8)
        h_p, xp_p, topi_p, prob_p, rank_p, cnt_p = routed_p
        h_s, xp_s, topi_s, prob_s, rank_s, cnt_s = routed_s
        n_p, n_s = h_p.shape[0], h_s.shape[0]
        n_e = router_w.shape[-1]
        n_tiles = (n_p + n_s) * TOP_K // MOE_TM + n_e
        pos_p, pos_s, tile0, cnt = _route(topi_p, rank_p, cnt_p, topi_s, rank_s, cnt_s)
        pos_t = jnp.concatenate([pos_p, pos_s], axis=0).T
        xs = _scatter_rows([xp_p, xp_s], pos_t, n_tiles * MOE_TM)
        ys = _moe(xs, tile0, cnt, exp_w_gu[l], exp_b_gu[l][:, None, :],
                  exp_w_down[l], exp_b_down[l][:, None, :])
        yg = _gather_rows(ys, pos_t.reshape(-1)).reshape(TOP_K, n_p + n_s, d_model // 2)
        gf = row(normf_g)
        hp = _combine(h_p, yg, 0, prob_p, gf, 512).reshape(bp, tp, d_model)
        hs = _combine(h_s, yg, n_p, prob_s, gf, 512).reshape(bs, ts, d_model)
        vrows.append(vn_s.reshape(bs, ts, d_a))
        shifts_p.append(sh_p)
        wkvs_p.append(s_p)
        shifts_s.append(sh_s)
        wkvs_s.append(s_s)
    return (hp, hs, jnp.stack(shifts_p), jnp.stack(wkvs_p), jnp.stack(shifts_s),
            jnp.stack(wkvs_s), jnp.stack(vrows))
```

```python
import functools

import jax
import jax.numpy as jnp
from jax import lax
from jax.experimental import pallas as pl
from jax.experimental.pallas import tpu as pltpu
from jax.experimental.pallas import tpu_sc as plsc

F32 = jnp.float32
BF16 = jnp.bfloat16

CHUNK = 128
G_A = 8
HEAD_B = 64
R_W, R_A, R_G = 64, 64, 128
TOP_K = 4
SWIGLU_LIMIT = 7.0
SWIGLU_ALPHA = 1.702
EPS = 1e-5
GN_EPS = HEAD_B * 1e-5

LANES = 128
PAIR = 2 * HEAD_B
VMEM_LIMIT = 56 * 1024 * 1024


def _dot(a, b):
    return jnp.dot(a.astype(BF16), b.astype(BF16), preferred_element_type=F32)


def _split(x):
    hi = x.astype(BF16)
    lo = (x - hi.astype(F32)).astype(BF16)
    return hi, lo


def _stack_rhs3(b):
    bh, bl = _split(b)
    return jnp.concatenate([bh, bh, bl], axis=0)


def _dot3_stacked(a, b_stacked):
    ah, al = _split(a)
    return jnp.dot(jnp.concatenate([ah, al, ah], axis=1), b_stacked, preferred_element_type=F32)


def _dot2_stacked(a, b_stacked):
    ah, al = _split(a)
    return jnp.dot(jnp.concatenate([ah, al], axis=1), b_stacked, preferred_element_type=F32)


def _head_sums(x, ones_bd):
    w = ones_bd.shape[0]
    return jnp.concatenate([jnp.dot(x[:, c:c + w].astype(BF16), ones_bd,
                                    preferred_element_type=F32)
                            for c in range(0, x.shape[1], w)], axis=1)


def _dot_exact_lhs(a_bf16, b):
    bh, bl = _split(b)
    bm = b - bh.astype(F32) - bl.astype(F32)
    return (jnp.dot(a_bf16, bh, preferred_element_type=F32)
            + jnp.dot(a_bf16, bl, preferred_element_type=F32)
            + jnp.dot(a_bf16, bm.astype(BF16), preferred_element_type=F32))


def _rms(x, g):
    return x * lax.rsqrt(jnp.mean(x * x, axis=-1, keepdims=True) + EPS) * g


def _sigmoid(x):
    return 1.0 / (1.0 + jnp.exp(-x))


def _softplus(z):
    return jnp.maximum(z, 0.0) + jnp.log(1.0 + jnp.exp(-jnp.abs(z)))


def _full_spec(shape):
    nd = len(shape)
    return pl.BlockSpec(shape, lambda *_: (0,) * nd, pipeline_mode=pl.Buffered(1))


def _branch_a_kernel(x_ref, g1_ref, wu_ref, wv_ref, wga_ref, vg_ref, vb_ref, wmix_ref,
                     bias_ref, wao_ref, apart_ref, vn_ref):
    xn = _rms(x_ref[...], g1_ref[...]).astype(BF16)
    v = jnp.dot(xn, wv_ref[...], preferred_element_type=F32)
    mu = jnp.mean(v, axis=-1, keepdims=True)
    d = v - mu
    var = jnp.mean(d * d, axis=-1, keepdims=True)
    vn = d * lax.rsqrt(var + EPS) * vg_ref[...] + vb_ref[...]
    vn_ref[...] = vn
    vnb = vn.astype(BF16)
    c_a = vnb.shape[1] // G_A
    rb = wmix_ref.shape[1]
    mixed = jnp.concatenate(
        [jnp.concatenate(
            [jnp.dot(wmix_ref[g], vnb[r0:r0 + rb, g * c_a:(g + 1) * c_a],
                     preferred_element_type=F32) for g in range(G_A)], axis=1)
         for r0 in range(0, vnb.shape[0], rb)], axis=0) + bias_ref[...]
    u = jnp.dot(xn, wu_ref[...], preferred_element_type=F32)
    ya = (u * mixed).astype(BF16)
    ga = jnp.dot(xn, wga_ref[...], preferred_element_type=F32)
    apart_ref[...] = _sigmoid(ga) * jnp.dot(ya, wao_ref[...], preferred_element_type=F32)


def _branch_a(x2d, tm, g1, wu, wv, wga, vg, vb, wmix, bias_full, wao):
    n, d = x2d.shape
    d_a = wu.shape[1]
    tok = lambda w: pl.BlockSpec((tm, w), lambda i: (i, 0))
    return pl.pallas_call(
        _branch_a_kernel,
        grid=(n // tm,),
        in_specs=[tok(d), _full_spec(g1.shape), _full_spec(wu.shape), _full_spec(wv.shape),
                  _full_spec(wga.shape), _full_spec(vg.shape), _full_spec(vb.shape),
                  _full_spec(wmix.shape), _full_spec(bias_full.shape), _full_spec(wao.shape)],
        out_specs=[tok(d), tok(d_a)],
        out_shape=[jax.ShapeDtypeStruct((n, d), F32), jax.ShapeDtypeStruct((n, d_a), F32)],
        compiler_params=pltpu.CompilerParams(dimension_semantics=("parallel",),
                                             vmem_limit_bytes=VMEM_LIMIT),
        name="branch_a",
    )(x2d, g1, wu, wv, wga, vg, vb, wmix, bias_full, wao)


def _branch_b_kernel(x_ref, g1_ref, wcur_ref, wgb_ref, mu_ref, ext_ref, w0_ref, w2_ref,
                     a0_ref, a2_ref, g2_ref, kk_ref, ka_ref, rk_ref, bd_ref,
                     r_out, k_out, v_out, kk_out, kka_out, lw_out, g_out, bonus_out,
                     sgb_out, cur_out, carry_scr, *, seq_rows, d_b):
    tm = x_ref.shape[0]
    xn = _rms(x_ref[...], g1_ref[...]).astype(BF16)
    cur = jnp.dot(xn, wcur_ref[...], preferred_element_type=F32)
    rolled = pltpu.roll(cur, 1, axis=0)
    row = lax.broadcasted_iota(jnp.int32, (tm, 1), 0)
    if seq_rows is None:
        first_tile = pl.program_id(1) == 0
        carry = jnp.where(first_tile, 0.0, carry_scr[...])
        prev = jnp.where(row == 0, carry, rolled)
        carry_scr[...] = cur[tm - 1:tm, :]
        cur_out[...] = cur[tm - 8:tm, :]
    else:
        prev = jnp.where(row % seq_rows == 0, ext_ref[...], rolled)
        cur_out[...] = cur
    xs = cur + (prev - cur) * mu_ref[...]
    r = xs[:, 0:d_b]
    k = xs[:, d_b:2 * d_b]
    v = xs[:, 2 * d_b:3 * d_b]
    o = 3 * d_b
    xw = xs[:, o:o + R_W]
    xa = xs[:, o + R_W:o + R_W + R_A]
    xg = xs[:, o + R_W + R_A:o + R_W + R_A + R_G]
    wl = w0_ref[...] + _dot3_stacked(jnp.tanh(xw), w2_ref[...])
    lw = -jnp.exp(-_softplus(-wl) - 0.5)
    a = _sigmoid(a0_ref[...] + _dot3_stacked(xa, a2_ref[...]))
    g = _dot2_stacked(_sigmoid(xg), g2_ref[...])
    bd = bd_ref[...]
    kkr = k * kk_ref[...]
    n2 = _head_sums(kkr * kkr, bd)
    kk = kkr / jnp.maximum(jnp.sqrt(n2), 1e-12)
    k2 = k * (1.0 + (a - 1.0) * ka_ref[...])
    bonus = _head_sums(r * k2 * rk_ref[...], bd) * v
    r_out[...] = r
    k_out[...] = k2
    v_out[...] = v
    kk_out[...] = kk
    kka_out[...] = kk * a
    lw_out[...] = lw
    g_out[...] = g
    bonus_out[...] = bonus
    sgb_out[...] = _sigmoid(jnp.dot(xn, wgb_ref[...], preferred_element_type=F32))


def _branch_b(x3d, tm, seq_rows, ext, g1, wcur, wgb, mu, w0, w2, a0, a2, g2, k_k, k_a, r_k, bd):
    nb, t, d = x3d.shape
    d_b = w0.shape[1]
    shift_w = wcur.shape[1]
    nt = t // tm
    tok = lambda w: pl.BlockSpec((None, tm, w), lambda b, i: (b, i, 0))
    cur_rows = 8 if seq_rows is None else tm
    outs = [jax.ShapeDtypeStruct((nb, t, d_b), F32)] * 8 + [
        jax.ShapeDtypeStruct((nb, t, d), F32),
        jax.ShapeDtypeStruct((nb, nt * cur_rows, shift_w), F32)]
    out_specs = [tok(d_b)] * 8 + [tok(d), pl.BlockSpec((None, cur_rows, shift_w),
                                                       lambda b, i: (b, i, 0))]
    weights = (g1, wcur, wgb, mu)
    small = (w0, w2, a0, a2, g2, k_k, k_a, r_k, bd)
    return pl.pallas_call(
        functools.partial(_branch_b_kernel, seq_rows=seq_rows, d_b=d_b),
        grid=(nb, nt),
        in_specs=[tok(d)] + [_full_spec(w.shape) for w in weights]
        + [tok(shift_w) if seq_rows is not None else _full_spec(ext.shape)]
        + [_full_spec(w.shape) for w in small],
        out_specs=out_specs,
        out_shape=outs,
        scratch_shapes=[pltpu.VMEM((1, shift_w), F32)],
        compiler_params=pltpu.CompilerParams(dimension_semantics=("parallel", "arbitrary"),
                                             vmem_limit_bytes=VMEM_LIMIT),
        name="branch_b",
    )(x3d, *weights, ext, *small)


def _scan_kernel(*refs, chunk, zero_state):
    r_ref, k_ref, v_ref, kk_ref, kka_ref, lw_ref = refs[:6]
    s0_ref = None if zero_state else refs[6]
    y_ref, sout_ref, s_scr = refs[-3:]
    L = chunk
    W2 = 2 * L
    c = pl.program_id(1)
    n_seq, n_pairs = s_scr.shape[:2]

    @pl.when(c == 0)
    def _():
        if zero_state:
            s_scr[...] = jnp.zeros_like(s_scr)
        else:
            z = jnp.zeros((HEAD_B, HEAD_B), F32)
            for s in range(n_seq):
                for p in range(n_pairs):
                    nat = jnp.concatenate(
                        [jnp.concatenate([s0_ref[s, 2 * p], z], axis=1),
                         jnp.concatenate([z, s0_ref[s, 2 * p + 1]], axis=1)], axis=0)
                    s_scr[s, p] = nat.T

    iota = lambda shape, dim: lax.broadcasted_iota(jnp.int32, shape, dim)
    stack_mask = (iota((W2, 1), 0) >= L) == (iota((1, PAIR), 1) >= HEAD_B)
    bd_mask = (iota((W2, 1), 0) >= L) == (iota((1, W2), 1) >= L)
    assert L & (L - 1) == 0, "chunk length must be a power of two"
    rw = iota((L, W2), 0)
    cw = iota((L, W2), 1) & (L - 1)
    strict_w = cw < rw
    eye_w = (cw == rw).astype(F32)
    incl_w2 = (iota((L, 2 * W2), 1) & (L - 1)) <= iota((L, 2 * W2), 0)
    tri = (iota((L, L), 1) <= iota((L, L), 0)).astype(BF16)
    di = iota((PAIR, PAIR), 0)
    dj = iota((PAIR, PAIR), 1)
    diag = di == dj
    head_diag = (di >= HEAD_B) == (dj >= HEAD_B)

    def stack(x):
        return jnp.where(stack_mask, jnp.concatenate([x, x], axis=0), 0.0)

    def bd(xw):
        return jnp.where(bd_mask, jnp.concatenate([xw, xw], axis=0), 0.0)

    n_double = max(L.bit_length() - 2, 0)
    prep = []
    for s in range(n_seq):
        lw_all = lw_ref[s]
        cum_all = _dot_exact_lhs(tri, lw_all)
        for p in range(n_pairs):
            sl = slice(p * PAIR, (p + 1) * PAIR)
            cum = cum_all[:, sl]
            cum_last = cum[L - 1:L, :]
            e_neg = jnp.exp(-cum)
            e_rel = jnp.exp(cum_last - cum)
            kk = kk_ref[s, :, sl]
            kka = kka_ref[s, :, sl]
            kx = k_ref[s, :, sl]
            prep.append(dict(
                s=s, p=p, sl=sl, ab=-kk * jnp.exp(cum - lw_all[:, sl]),
                rb=r_ref[s, :, sl] * jnp.exp(cum), bb=kka * e_neg, kb=kx * e_neg,
                bt=kka * e_rel, kt=kx * e_rel, v=v_ref[s, :, sl], decay=jnp.exp(cum_last)))
    a_w = [lax.dot_general(
        jnp.concatenate([q["ab"], q["rb"]], axis=0).astype(BF16),
        jnp.concatenate([stack(q["bb"]), stack(q["kb"])], axis=0).astype(BF16),
        (((1,), (1,)), ((), ())), preferred_element_type=F32) for q in prep]
    nw = [jnp.where(strict_w, a[:L, :W2], 0.0) for a in a_w]
    tw = [eye_w + n for n in nw]
    nbd = [bd(n) for n in nw]
    for _ in range(n_double):
        nw = [_dot(n, b) for n, b in zip(nw, nbd)]
        nbd = [bd(n) for n in nw]
        tw = [t + _dot(t, b) for t, b in zip(tw, nbd)]
    akv = [_dot(jnp.where(strict_w, a[:L, W2:], 0.0), stack(q["v"])) for a, q in zip(a_w, prep)]
    tx = [_dot(t, jnp.concatenate([stack(q["ab"]), stack(kv)], axis=1))
          for t, q, kv in zip(tw, prep, akv)]
    ry = [_dot(jnp.where(incl_w2, a[L:], 0.0),
               jnp.concatenate(
                   [jnp.concatenate([stack(x[:, :PAIR]), stack(x[:, PAIR:])], axis=1),
                    jnp.concatenate([jnp.zeros((W2, PAIR), F32), stack(q["v"])], axis=1)], axis=0))
          for a, x, q in zip(a_w, tx, prep)]
    for q, x, y in zip(prep, tx, ry):
        st = s_scr[q["s"], q["p"]]
        us = _dot(jnp.concatenate([x[:, :PAIR], q["rb"] + y[:, :PAIR]], axis=0), st)
        y_ref[q["s"], :, q["sl"]] = us[L:] + y[:, PAIR:]
        u = us[:L] + x[:, PAIR:]
        lhs = jnp.concatenate([jnp.where(diag, q["decay"], 0.0),
                               jnp.concatenate([q["bt"], q["kt"]], axis=0).T], axis=1)
        g = _dot(lhs, jnp.concatenate([st, u, q["v"]], axis=0))
        s_scr[q["s"], q["p"]] = jnp.where(head_diag, g, 0.0)

    @pl.when(c == pl.num_programs(1) - 1)
    def _():
        for s in range(n_seq):
            for p in range(n_pairs):
                nat = s_scr[s, p].T
                sout_ref[s, 2 * p] = nat[:HEAD_B, :HEAD_B]
                sout_ref[s, 2 * p + 1] = nat[HEAD_B:, HEAD_B:]


def _scan(r, k, v, kk, kka, lw, s0, chunk, n_seq):
    nb, t, d_b = r.shape
    n_pairs = d_b // PAIR
    n_heads = d_b // HEAD_B
    tok = pl.BlockSpec((n_seq, chunk, d_b), lambda b, c: (b, c, 0))
    st = pl.BlockSpec((n_seq, n_heads, HEAD_B, HEAD_B), lambda b, c: (b, 0, 0, 0))
    zero_state = s0 is None
    return pl.pallas_call(
        functools.partial(_scan_kernel, chunk=chunk, zero_state=zero_state),
        grid=(nb // n_seq, t // chunk),
        in_specs=[tok] * 6 + ([] if zero_state else [st]),
        out_specs=[tok, st],
        out_shape=[jax.ShapeDtypeStruct((nb, t, d_b), F32),
                   jax.ShapeDtypeStruct((nb, n_heads, HEAD_B, HEAD_B), F32)],
        scratch_shapes=[pltpu.VMEM((n_seq, n_pairs, PAIR, PAIR), F32)],
        compiler_params=pltpu.CompilerParams(dimension_semantics=("parallel", "arbitrary"),
                                             vmem_limit_bytes=VMEM_LIMIT),
        name="rwkv_scan",
    )(r, k, v, kk, kka, lw, *(() if zero_state else (s0,)))


def _pack_bf16_pairs(x):
    w = x.shape[1] // 2
    bits = lambda v: lax.bitcast_convert_type(v.astype(BF16).astype(F32), jnp.uint32)
    return (bits(x[:, :w]) >> 16) | (bits(x[:, w:]) & jnp.uint32(0xFFFF0000))


def _unpack_bf16_pairs(u):
    lo = lax.bitcast_convert_type(u << 16, F32)
    hi = lax.bitcast_convert_type(u & jnp.uint32(0xFFFF0000), F32)
    return lo, hi


def _merge_kernel(y_ref, g_ref, bonus_ref, sgb_ref, apart_ref, x_ref, lg_ref, lb_ref, bd_ref,
                  wbo_ref, wo_ref, g2_ref, rw_ref, rb_ref,
                  h_out, xp_out, topi_out, prob_out, rank_out, cnt_out, cnt_scr):
    step = pl.program_id(0)

    @pl.when(step == 0)
    def _():
        cnt_scr[...] = jnp.zeros_like(cnt_scr)

    bd = bd_ref[...]
    y = y_ref[...]
    inv_n = 1.0 / HEAD_B
    mu = _head_sums(y, bd) * inv_n
    d = y - mu
    var = _head_sums(d * d, bd) * inv_n
    yn = d * lax.rsqrt(var + GN_EPS) * lg_ref[...] + lb_ref[...]
    yb = ((yn + bonus_ref[...]) * g_ref[...]).astype(BF16)
    merged = apart_ref[...] + sgb_ref[...] * jnp.dot(yb, wbo_ref[...], preferred_element_type=F32)
    h = x_ref[...] + jnp.dot(merged.astype(BF16), wo_ref[...], preferred_element_type=F32)
    h_out[...] = h
    xn2 = _rms(h, g2_ref[...])
    xp_out[...] = _pack_bf16_pairs(xn2)
    n_e = rb_ref.shape[1]
    xh, xl = _split(xn2)
    rw = rw_ref[...]
    t = jnp.dot(xh, rw, preferred_element_type=F32)
    logits = (t[:, :n_e] + t[:, n_e:] + jnp.dot(xl, rw[:, :n_e], preferred_element_type=F32)
              + rb_ref[...])
    tm = logits.shape[0]
    idx = lax.broadcasted_iota(jnp.int32, logits.shape, 1).astype(F32)
    work = logits
    tops, hots, sels = [], [], []
    for _ in range(TOP_K):
        m = jnp.max(work, axis=-1, keepdims=True)
        sel = jnp.min(jnp.where(work == m, idx, float(n_e)), axis=-1, keepdims=True)
        hot = idx == sel
        tops.append(m)
        hots.append(hot)
        sels.append(sel)
        work = jnp.where(hot, -jnp.inf, work)
    es = [jnp.exp(t - tops[0]) for t in tops]
    denom = es[0] + es[1] + es[2] + es[3]
    topi_out[...] = jnp.concatenate(sels, axis=1).astype(jnp.int32)
    prob_out[...] = jnp.concatenate([e / denom for e in es], axis=1)
    hot_any = jnp.zeros_like(logits)
    for hot in hots:
        hot_any = hot_any + hot.astype(F32)
    ri = lax.broadcasted_iota(jnp.int32, (tm, tm), 0)
    ci = lax.broadcasted_iota(jnp.int32, (tm, tm), 1)
    before = _dot((ci < ri).astype(BF16), hot_any) + cnt_scr[...]
    rank_out[...] = jnp.concatenate(
        [jnp.sum(jnp.where(hot, before, 0.0), axis=-1, keepdims=True) for hot in hots],
        axis=1).astype(jnp.int32)
    total = cnt_scr[...] + jnp.sum(hot_any, axis=0, keepdims=True)
    cnt_scr[...] = total
    cnt_out[...] = total.astype(jnp.int32)


def _merge(y, g, bonus, sgb, apart, x2d, tm, lnx_g, lnx_b, bd, wbo, wo, g2n, rw, rb):
    n, d = x2d.shape
    d_b = y.shape[1]
    n_e = rb.shape[1]
    tok = lambda w: pl.BlockSpec((tm, w), lambda i: (i, 0))
    weights = (lnx_g, lnx_b, bd, wbo, wo, g2n, rw, rb)
    return pl.pallas_call(
        _merge_kernel,
        grid=(n // tm,),
        in_specs=[tok(d_b), tok(d_b), tok(d_b), tok(d), tok(d), tok(d)]
        + [_full_spec(w.shape) for w in weights],
        out_specs=[tok(d), tok(d // 2), tok(TOP_K), tok(TOP_K), tok(TOP_K),
                   pl.BlockSpec((1, n_e), lambda i: (0, 0))],
        out_shape=[jax.ShapeDtypeStruct((n, d), F32),
                   jax.ShapeDtypeStruct((n, d // 2), jnp.uint32),
                   jax.ShapeDtypeStruct((n, TOP_K), jnp.int32),
                   jax.ShapeDtypeStruct((n, TOP_K), F32),
                   jax.ShapeDtypeStruct((n, TOP_K), jnp.int32),
                   jax.ShapeDtypeStruct((1, n_e), jnp.int32)],
        scratch_shapes=[pltpu.VMEM((1, n_e), F32)],
        compiler_params=pltpu.CompilerParams(dimension_semantics=("arbitrary",),
                                             vmem_limit_bytes=VMEM_LIMIT),
        name="merge_router",
    )(y, g, bonus, sgb, apart, x2d, *weights)


MOE_TM = 1024
MOE_SUB = 256
MOE_FF_CHUNK = 512
CAST_ROWS = 256


def _route(topi_p, rank_p, cnt_p, topi_s, rank_s, cnt_s):
    n_e = cnt_p.shape[-1]
    cnt_p, cnt_s = cnt_p.reshape(n_e), cnt_s.reshape(n_e)
    cnt = cnt_p + cnt_s
    padded = (cnt + MOE_TM - 1) // MOE_TM * MOE_TM
    ends = jnp.cumsum(padded)
    base = ends - padded
    lookup = lambda table, idx: jnp.sum(
        jnp.where(idx[..., None] == jnp.arange(n_e), table, 0), axis=-1)
    pos_p = lookup(base, topi_p) + rank_p
    pos_s = lookup(base + cnt_p, topi_s) + rank_s
    return (pos_p.astype(jnp.int32), pos_s.astype(jnp.int32),
            (base // MOE_TM).astype(jnp.int32), cnt.astype(jnp.int32))


def _moe_kernel(tile0_ref, cnt_ref, xs_hbm, wgu_ref, bgu_ref, wd_ref, bdn_ref, ys_hbm,
                wgu_b, wd_b, xbuf, ybuf, sem_in, sem_out, done_ref):
    e = pl.program_id(0)
    n_e = pl.num_programs(0)

    @pl.when(e == 0)
    def _():
        done_ref[0] = 0

    g0 = done_ref[0]
    tile0 = tile0_ref[e]
    cnt = cnt_ref[e]
    n_t = (cnt + MOE_TM - 1) // MOE_TM
    d, gu = wgu_ref.shape
    d_ff = wd_ref.shape[0]
    half = d // 2

    def in_copy(tile, slot):
        return pltpu.make_async_copy(xs_hbm.at[pl.ds(tile * MOE_TM, MOE_TM)], xbuf.at[slot],
                                     sem_in.at[slot])

    def out_copy(tile, slot):
        return pltpu.make_async_copy(ybuf.at[slot], ys_hbm.at[pl.ds(tile * MOE_TM, MOE_TM)],
                                     sem_out.at[slot])

    @pl.when(jnp.logical_and(e == 0, n_t > 0))
    def _():
        in_copy(tile0, 0).start()

    @pl.when(n_t > 0)
    def _():
        for r0 in range(0, d, CAST_ROWS):
            wgu_b[r0:r0 + CAST_ROWS, :] = wgu_ref[r0:r0 + CAST_ROWS, :].astype(BF16)
        for r0 in range(0, d_ff, CAST_ROWS):
            wd_b[r0:r0 + CAST_ROWS, :] = wd_ref[r0:r0 + CAST_ROWS, :].astype(BF16)

    def tile_step(j, carry):
        slot = (g0 + j) % 2
        in_copy(tile0 + j, slot).wait()

        @pl.when(j + 1 < n_t)
        def _():
            in_copy(tile0 + j + 1, 1 - slot).start()

        @pl.when(g0 + j >= 2)
        def _():
            out_copy(0, slot).wait()

        left = cnt - j * MOE_TM

        def ffn(row0, rows):
            lo, hi = _unpack_bf16_pairs(xbuf[slot, pl.ds(row0, rows)])
            valid = lax.broadcasted_iota(jnp.int32, lo.shape, 0) < left - row0
            x_lo = jnp.where(valid, lo, 0.0).astype(BF16)
            x_hi = jnp.where(valid, hi, 0.0).astype(BF16)

            def proj(c0):
                cols = slice(c0, c0 + MOE_FF_CHUNK)
                return (jnp.dot(x_lo, wgu_b[:half, cols], preferred_element_type=F32)
                        + jnp.dot(x_hi, wgu_b[half:, cols], preferred_element_type=F32)
                        + bgu_ref[:, cols])

            y = jnp.zeros((rows, d), F32)
            for f in range(d_ff // MOE_FF_CHUNK):
                c0 = f * MOE_FF_CHUNK
                gate = jnp.minimum(proj(c0), SWIGLU_LIMIT)
                up = jnp.clip(proj(d_ff + c0), -SWIGLU_LIMIT, SWIGLU_LIMIT)
                hh = (up + 1.0) * gate * _sigmoid(gate * SWIGLU_ALPHA)
                y = y + jnp.dot(hh.astype(BF16), wd_b[c0:c0 + MOE_FF_CHUNK, :],
                                preferred_element_type=F32)
            ybuf[slot, pl.ds(row0, rows)] = _pack_bf16_pairs(y + bdn_ref[...])

        n_sub = (jnp.minimum(left, MOE_TM) + MOE_SUB - 1) // MOE_SUB

        @pl.when(n_sub == MOE_TM // MOE_SUB)
        def _():
            ffn(0, MOE_TM)

        @pl.when(n_sub < MOE_TM // MOE_SUB)
        def _():
            def sub_step(i, c):
                ffn(pl.multiple_of(i * MOE_SUB, MOE_SUB), MOE_SUB)
                return c
            lax.fori_loop(0, n_sub, sub_step, 0)

        out_copy(tile0 + j, slot).start()
        return carry

    lax.fori_loop(0, n_t, tile_step, 0)

    g1 = g0 + n_t
    done_ref[0] = g1
    nxt = jnp.minimum(e + 1, n_e - 1)

    @pl.when(jnp.logical_and(e + 1 < n_e, cnt_ref[nxt] > 0))
    def _():
        in_copy(tile0_ref[nxt], g1 % 2).start()

    @pl.when(jnp.logical_and(e == n_e - 1, g1 >= 2))
    def _():
        out_copy(0, g1 % 2).wait()

    @pl.when(jnp.logical_and(e == n_e - 1, g1 >= 1))
    def _():
        out_copy(0, (g1 - 1) % 2).wait()


def _moe(xs, tile0, cnt, wgu, bgu, wd, bdn):
    p_rows, half = xs.shape
    n_e, d, gu = wgu.shape
    d_ff = wd.shape[1]
    w_blk = lambda e, tile0, cnt: (e, 0, 0)
    grid_spec = pltpu.PrefetchScalarGridSpec(
        num_scalar_prefetch=2,
        grid=(n_e,),
        in_specs=[pl.BlockSpec(memory_space=pl.ANY),
                  pl.BlockSpec((None, d, gu), w_blk),
                  pl.BlockSpec((None, 1, gu), w_blk),
                  pl.BlockSpec((None, d_ff, d), w_blk),
                  pl.BlockSpec((None, 1, d), w_blk)],
        out_specs=pl.BlockSpec(memory_space=pl.ANY),
        scratch_shapes=[pltpu.VMEM((d, gu), BF16), pltpu.VMEM((d_ff, d), BF16),
                        pltpu.VMEM((2, MOE_TM, half), jnp.uint32),
                        pltpu.VMEM((2, MOE_TM, half), jnp.uint32),
                        pltpu.SemaphoreType.DMA((2,)), pltpu.SemaphoreType.DMA((2,)),
                        pltpu.SMEM((1,), jnp.int32)])
    return pl.pallas_call(
        _moe_kernel,
        grid_spec=grid_spec,
        out_shape=jax.ShapeDtypeStruct((p_rows, half), jnp.uint32),
        compiler_params=pltpu.CompilerParams(dimension_semantics=("arbitrary",),
                                             vmem_limit_bytes=VMEM_LIMIT),
        name="moe_experts",
    )(tile0, cnt, xs, wgu, bgu, wd, bdn)


def _combine_kernel(h_ref, yg_ref, prob_ref, gf_ref, out_ref):
    half = h_ref.shape[1] // 2
    prob = prob_ref[...]
    acc_lo = jnp.zeros((h_ref.shape[0], half), F32)
    acc_hi = jnp.zeros((h_ref.shape[0], half), F32)
    for k in range(TOP_K):
        lo, hi = _unpack_bf16_pairs(yg_ref[k])
        pk = prob[:, k:k + 1]
        acc_lo = acc_lo + pk * lo
        acc_hi = acc_hi + pk * hi
    z = h_ref[...] + jnp.concatenate([acc_lo, acc_hi], axis=1)
    out_ref[...] = _rms(z, gf_ref[...])


def _combine(h, yg, row0, prob, gf, tm):
    n, d = h.shape
    tok = lambda w: pl.BlockSpec((tm, w), lambda i: (i, 0))
    t0 = row0 // tm
    return pl.pallas_call(
        _combine_kernel,
        grid=(n // tm,),
        in_specs=[tok(d), pl.BlockSpec((TOP_K, tm, d // 2), lambda i: (0, i + t0, 0)),
                  tok(TOP_K), _full_spec(gf.shape)],
        out_specs=tok(d),
        out_shape=jax.ShapeDtypeStruct((n, d), F32),
        compiler_params=pltpu.CompilerParams(dimension_semantics=("parallel",),
                                             vmem_limit_bytes=VMEM_LIMIT),
        name="moe_combine",
    )(h, yg, prob, gf)


SC_CORES = 2
SC_SUBCORES = 16
SC_WORKERS = SC_CORES * SC_SUBCORES
SC_MAX_INDEX = 128


def _sc_chunk(rows_per_worker, limit=SC_MAX_INDEX):
    for c in range(limit, 7, -8):
        if rows_per_worker % c == 0:
            return c
    raise ValueError(f"no 8-aligned chunk divides {rows_per_worker} rows")


def _sc_mesh():
    return plsc.VectorSubcoreMesh(core_axis_name="c", subcore_axis_name="s")


def _sc_worker():
    return lax.axis_index("s") * SC_CORES + lax.axis_index("c")


def _scatter_rows(xs_in, pos_t, p_rows):
    w = xs_in[0].shape[1]
    dtype = xs_in[0].dtype
    n_k, n = pos_t.shape
    pos_flat = pos_t.reshape(n_k * n)
    parts, row0 = [], 0
    for x in xs_in:
        per = x.shape[0] // SC_WORKERS
        parts.append((row0, per, _sc_chunk(per, SC_MAX_INDEX // 2)))
        row0 += x.shape[0]
    max_chunk = max(c for _, _, c in parts)
    max_per = max(p for _, p, _ in parts)

    def body(*refs):
        x_hbms = refs[:len(xs_in)]
        pos_hbm, out_hbm, idx_v, rows_v, sem_l, sem_s = refs[len(xs_in):]
        for x_hbm, (tok0, per, chunk) in zip(x_hbms, parts):
            n_chunks = per // chunk
            base = pl.multiple_of(_sc_worker() * per, 8)
            for k in range(n_k):
                src = pl.multiple_of(k * n + tok0 + base, 8)
                pltpu.sync_copy(pos_hbm.at[pl.ds(src, per)], idx_v.at[pl.ds(k * per, per)])

            def load(c, slot):
                src = x_hbm.at[pl.ds(pl.multiple_of(base + c * chunk, 8), chunk)]
                return pltpu.make_async_copy(src, rows_v.at[slot, pl.ds(0, chunk)], sem_l.at[slot])

            def scatter(c, k, slot):
                rows = idx_v.at[pl.ds(pl.multiple_of(k * per + c * chunk, 8), chunk)]
                return pltpu.make_async_copy(rows_v.at[slot, pl.ds(0, chunk)], out_hbm.at[rows],
                                             sem_s.at[slot])

            load(0, 0).start()

            @pl.loop(0, n_chunks)
            def _(c):
                slot = c % 2
                load(c, slot).wait()

                @pl.when(c + 1 < n_chunks)
                def _():
                    @pl.when(c >= 1)
                    def _():
                        for k in range(n_k):
                            scatter(c - 1, k, 1 - slot).wait()
                    load(c + 1, 1 - slot).start()

                for k in range(n_k):
                    scatter(c, k, slot).start()

            if n_chunks >= 2:
                for k in range(n_k):
                    scatter(n_chunks - 2, k, n_chunks % 2).wait()
            for k in range(n_k):
                scatter(n_chunks - 1, k, (n_chunks - 1) % 2).wait()

    return pl.kernel(
        body, out_type=jax.ShapeDtypeStruct((p_rows, w), dtype), mesh=_sc_mesh(),
        scratch_types=[pltpu.VMEM((n_k * max_per,), jnp.int32),
                       pltpu.VMEM((2, max_chunk, w), dtype),
                       pltpu.SemaphoreType.DMA((2,)), pltpu.SemaphoreType.DMA((2,))],
        name="sc_scatter_rows")(*xs_in, pos_flat)


def _gather_rows(table, idx):
    n = idx.shape[0]
    w = table.shape[1]
    per = n // SC_WORKERS
    chunk = _sc_chunk(per)

    n_chunks = per // chunk

    def body(table_hbm, idx_hbm, out_hbm, idx_v, rows_v, sem_g, sem_w):
        base = pl.multiple_of(_sc_worker() * per, 8)
        pltpu.sync_copy(idx_hbm.at[pl.ds(base, per)], idx_v)

        def gather(c, slot):
            rows = idx_v.at[pl.ds(pl.multiple_of(c * chunk, 8), chunk)]
            return pltpu.make_async_copy(table_hbm.at[rows], rows_v.at[slot], sem_g.at[slot])

        def write(c, slot):
            dst = out_hbm.at[pl.ds(pl.multiple_of(base + c * chunk, 8), chunk)]
            return pltpu.make_async_copy(rows_v.at[slot], dst, sem_w.at[slot])

        gather(0, 0).start()

        @pl.loop(0, n_chunks)
        def _(c):
            slot = c % 2
            gather(c, slot).wait()

            @pl.when(c + 1 < n_chunks)
            def _():
                @pl.when(c >= 1)
                def _():
                    write(c - 1, 1 - slot).wait()
                gather(c + 1, 1 - slot).start()

            write(c, slot).start()

        if n_chunks >= 2:
            write(n_chunks - 2, n_chunks % 2).wait()
        write(n_chunks - 1, (n_chunks - 1) % 2).wait()

    return pl.kernel(
        body, out_type=jax.ShapeDtypeStruct((n, w), table.dtype), mesh=_sc_mesh(),
        scratch_types=[pltpu.VMEM((per,), jnp.int32), pltpu.VMEM((2, chunk, w), table.dtype),
                       pltpu.SemaphoreType.DMA((2,)), pltpu.SemaphoreType.DMA((2,))],
        name="sc_gather_rows")(table, idx)


def _mix_matrix(ws, block, seq_rows):
    causal = jnp.tril(jnp.ones((CHUNK, CHUNK), dtype=bool))
    w = jnp.where(causal[None], ws, 0.0)[:, :seq_rows, :seq_rows]
    eye = jnp.eye(block // seq_rows, dtype=ws.dtype)
    return jnp.einsum("ab,gts->gatbs", eye, w).reshape(ws.shape[0], block, block).astype(BF16)


def _stream(x, shift_in, wkv_in, p, tm, seq_rows, scan_chunk, scan_seqs):
    nb, t, d = x.shape
    n = nb * t
    x2d = x.reshape(n, d)
    rows = CHUNK if seq_rows is None else seq_rows
    wmix = _mix_matrix(p["w_spatial"], max(rows, LANES), rows)
    pos = jnp.arange(tm) % rows
    d_a = p["wu"].shape[1]
    bias_full = jnp.repeat(p["b_spatial"].T[pos], d_a // G_A, axis=1)
    apart, vn = _branch_a(x2d, tm, p["norm1_g"], p["wu"], p["wv"], p["wga"], p["vnorm_g"],
                          p["vnorm_b"], wmix, bias_full, p["w_a_out"])
    if seq_rows is None:
        xb, ext = x, jnp.zeros((1, p["wcur"].shape[1]), F32)
    else:
        xb = x2d.reshape(n // tm, tm, d)
        ext = jnp.repeat(shift_in, seq_rows, axis=0).reshape(n // tm, tm, -1)
    outs = _branch_b(xb, tm, seq_rows, ext, p["norm1_g"], p["wcur"], p["wgb"], p["mu_shift"],
                     p["w0"], p["w2"], p["a0"], p["a2"], p["g2"], p["k_k"], p["k_a"], p["r_k"],
                     p["bd"])
    r, k2, v, kk, kka, lw, g, bonus, sgb, cur = outs
    d_b = r.shape[-1]
    if seq_rows is None:
        shift_out = cur[:, -1, :]
        scan_in = [a.reshape(nb, t, d_b) for a in (r, k2, v, kk, kka, lw)]
    else:
        shift_out = cur.reshape(nb, t, -1)[:, -1, :]
        pad = scan_chunk - t
        scan_in = [jnp.pad(a.reshape(nb, t, d_b), ((0, 0), (0, pad), (0, 0)))
                   for a in (r, k2, v, kk, kka, lw)]
    y, s_out = _scan(*scan_in, wkv_in, scan_chunk, scan_seqs)
    y2d = y[:, :t].reshape(n, d_b)
    flat = lambda a: a.reshape(n, a.shape[-1])
    routed = _merge(y2d, flat(g), flat(bonus), flat(sgb), apart, x2d, tm, p["lnx_g"],
                    p["lnx_b"], p["bd"], p["w_b_out"], p["w_out"], p["norm2_g"],
                    p["router_w"], p["router_b"])
    return routed, vn, shift_out, s_out


def kernel(x_prompt, x_sample, state_shift, state_wkv, norm1_g, w_in, mu_shift, vnorm_g, vnorm_b, w_spatial, b_spatial, w_a_out, w0, w2, a0, a2, g2, k_k, k_a, r_k, lnx_g, lnx_b, w_b_out, w_out, norm2_g, router_w, router_b, exp_w_gu, exp_b_gu, exp_w_down, exp_b_down, normf_g):
    depth = w_in.shape[0]
    assert depth == 1, "the final norm is fused into the single layer's MoE call"
    d_model = x_prompt.shape[-1]
    d_b = w0.shape[-1]
    shift_w = mu_shift.shape[-1]
    d_a = vnorm_g.shape[-1]
    bp, tp, _ = x_prompt.shape
    bs, ts, _ = x_sample.shape
    head_id = jnp.arange(2 * LANES) // HEAD_B
    bd = (head_id[:, None] == head_id[None, :]).astype(BF16)
    row = lambda a: a.reshape(1, -1)

    hp, hs = x_prompt, x_sample
    vrows, shifts_p, wkvs_p, shifts_s, wkvs_s = [], [], [], [], []
    for l in range(depth):
        wi = w_in[l].astype(BF16)
        o = shift_w
        p = dict(
            norm1_g=row(norm1_g[l]), wcur=wi[:, :o], wu=wi[:, o:o + d_a],
            wv=wi[:, o + d_a:o + 2 * d_a], wga=wi[:, o + 2 * d_a:o + 2 * d_a + d_model],
            wgb=wi[:, o + 2 * d_a + d_model:], mu_shift=row(mu_shift[l]),
            vnorm_g=row(vnorm_g[l]), vnorm_b=row(vnorm_b[l]), w_spatial=w_spatial[l],
            b_spatial=b_spatial[l], w_a_out=w_a_out[l].astype(BF16), w0=row(w0[l]),
            w2=_stack_rhs3(w2[l]), a0=row(a0[l]), a2=_stack_rhs3(a2[l]),
            g2=jnp.concatenate([g2[l].astype(BF16)] * 2, axis=0),
            k_k=row(k_k[l]), k_a=row(k_a[l]),
            r_k=row(r_k[l]), lnx_g=row(lnx_g[l]), lnx_b=row(lnx_b[l]),
            w_b_out=w_b_out[l].astype(BF16), w_out=w_out[l].astype(BF16),
            norm2_g=row(norm2_g[l]), router_w=jnp.concatenate(_split(router_w[l]), axis=1),
            router_b=row(router_b[l]), bd=bd)
        routed_p, _, sh_p, s_p = _stream(hp, None, None, p, 512, None, 64, 4)
        routed_s, vn_s, sh_s, s_s = _stream(hs, state_shift[l], state_wkv[l], p, 512, ts, 8, 8)
        h_p, xp_p, topi_p, prob_p, rank_p, cnt_p = routed_p
        h_s, xp_s, topi_s, prob_s, rank_s, cnt_s = routed_s
        n_p, n_s = h_p.shape[0], h_s.shape[0]
        n_e = router_w.shape[-1]
        n_tiles = (n_p + n_s) * TOP_K // MOE_TM + n_e
        pos_p, pos_s, tile0, cnt = _route(topi_p, rank_p, cnt_p, topi_s, rank_s, cnt_s)
        pos_t = jnp.concatenate([pos_p, pos_s], axis=0).T
        xs = _scatter_rows([xp_p, xp_s], pos_t, n_tiles * MOE_TM)
        ys = _moe(xs, tile0, cnt, exp_w_gu[l], exp_b_gu[l][:, None, :],
                  exp_w_down[l], exp_b_down[l][:, None, :])
        yg = _gather_rows(ys, pos_t.reshape(-1)).reshape(TOP_K, n_p + n_s, d_model // 2)
        gf = row(normf_g)
        hp = _combine(h_p, yg, 0, prob_p, gf, 512).reshape(bp, tp, d_model)
        hs = _combine(h_s, yg, n_p, prob_s, gf, 512).reshape(bs, ts, d_model)
        vrows.append(vn_s.reshape(bs, ts, d_a))
        shifts_p.append(sh_p)
        wkvs_p.append(s_p)
        shifts_s.append(sh_s)
        wkvs_s.append(s_s)
    return (hp, hs, jnp.stack(shifts_p), jnp.stack(wkvs_p), jnp.stack(shifts_s),
            jnp.stack(wkvs_s), jnp.stack(vrows))
```

```python
import functools

import jax
import jax.numpy as jnp
from jax import lax
from jax.experimental import pallas as pl
from jax.experimental.pallas import tpu as pltpu
from jax.experimental.pallas import tpu_sc as plsc

F32 = jnp.float32
BF16 = jnp.bfloat16

CHUNK = 128
G_A = 8
HEAD_B = 64
R_W, R_A, R_G = 64, 64, 128
TOP_K = 4
SWIGLU_LIMIT = 7.0
SWIGLU_ALPHA = 1.702
EPS = 1e-5
GN_EPS = HEAD_B * 1e-5

LANES = 128
PAIR = 2 * HEAD_B
VMEM_LIMIT = 56 * 1024 * 1024


def _dot(a, b):
    return jnp.dot(a.astype(BF16), b.astype(BF16), preferred_element_type=F32)


def _split(x):
    hi = x.astype(BF16)
    lo = (x - hi.astype(F32)).astype(BF16)
    return hi, lo


def _stack_rhs3(b):
    bh, bl = _split(b)
    return jnp.concatenate([bh, bh, bl], axis=0)


def _dot3_stacked(a, b_stacked):
    ah, al = _split(a)
    return jnp.dot(jnp.concatenate([ah, al, ah], axis=1), b_stacked, preferred_element_type=F32)


def _dot2_stacked(a, b_stacked):
    ah, al = _split(a)
    return jnp.dot(jnp.concatenate([ah, al], axis=1), b_stacked, preferred_element_type=F32)


def _head_sums(x, ones_bd):
    w = ones_bd.shape[0]
    return jnp.concatenate([jnp.dot(x[:, c:c + w].astype(BF16), ones_bd,
                                    preferred_element_type=F32)
                            for c in range(0, x.shape[1], w)], axis=1)


def _dot_exact_lhs(a_bf16, b):
    bh, bl = _split(b)
    bm = b - bh.astype(F32) - bl.astype(F32)
    return (jnp.dot(a_bf16, bh, preferred_element_type=F32)
            + jnp.dot(a_bf16, bl, preferred_element_type=F32)
            + jnp.dot(a_bf16, bm.astype(BF16), preferred_element_type=F32))


def _rms(x, g):
    return x * lax.rsqrt(jnp.mean(x * x, axis=-1, keepdims=True) + EPS) * g


def _sigmoid(x):
    return 1.0 / (1.0 + jnp.exp(-x))


def _softplus(z):
    return jnp.maximum(z, 0.0) + jnp.log(1.0 + jnp.exp(-jnp.abs(z)))


def _full_spec(shape):
    nd = len(shape)
    return pl.BlockSpec(shape, lambda *_: (0,) * nd, pipeline_mode=pl.Buffered(1))


def _branch_a_kernel(x_ref, g1_ref, wu_ref, wv_ref, wga_ref, vg_ref, vb_ref, wmix_ref,
                     bias_ref, wao_ref, apart_ref, vn_ref):
    xn = _rms(x_ref[...], g1_ref[...]).astype(BF16)
    v = jnp.dot(xn, wv_ref[...], preferred_element_type=F32)
    mu = jnp.mean(v, axis=-1, keepdims=True)
    d = v - mu
    var = jnp.mean(d * d, axis=-1, keepdims=True)
    vn = d * lax.rsqrt(var + EPS) * vg_ref[...] + vb_ref[...]
    vn_ref[...] = vn
    vnb = vn.astype(BF16)
    c_a = vnb.shape[1] // G_A
    rb = wmix_ref.shape[1]
    mixed = jnp.concatenate(
        [jnp.concatenate(
            [jnp.dot(wmix_ref[g], vnb[r0:r0 + rb, g * c_a:(g + 1) * c_a],
                     preferred_element_type=F32) for g in range(G_A)], axis=1)
         for r0 in range(0, vnb.shape[0], rb)], axis=0) + bias_ref[...]
    u = jnp.dot(xn, wu_ref[...], preferred_element_type=F32)
    ya = (u * mixed).astype(BF16)
    ga = jnp.dot(xn, wga_ref[...], preferred_element_type=F32)
    apart_ref[...] = _sigmoid(ga) * jnp.dot(ya, wao_ref[...], preferred_element_type=F32)


def _branch_a(x2d, tm, g1, wu, wv, wga, vg, vb, wmix, bias_full, wao):
    n, d = x2d.shape
    d_a = wu.shape[1]
    tok = lambda w: pl.BlockSpec((tm, w), lambda i: (i, 0))
    return pl.pallas_call(
        _branch_a_kernel,
        grid=(n // tm,),
        in_specs=[tok(d), _full_spec(g1.shape), _full_spec(wu.shape), _full_spec(wv.shape),
                  _full_spec(wga.shape), _full_spec(vg.shape), _full_spec(vb.shape),
                  _full_spec(wmix.shape), _full_spec(bias_full.shape), _full_spec(wao.shape)],
        out_specs=[tok(d), tok(d_a)],
        out_shape=[jax.ShapeDtypeStruct((n, d), F32), jax.ShapeDtypeStruct((n, d_a), F32)],
        compiler_params=pltpu.CompilerParams(dimension_semantics=("parallel",),
                                             vmem_limit_bytes=VMEM_LIMIT),
        name="branch_a",
    )(x2d, g1, wu, wv, wga, vg, vb, wmix, bias_full, wao)


def _branch_b_kernel(x_ref, g1_ref, wcur_ref, wgb_ref, mu_ref, ext_ref, w0_ref, w2_ref,
                     a0_ref, a2_ref, g2_ref, kk_ref, ka_ref, rk_ref, bd_ref,
                     r_out, k_out, v_out, kk_out, kka_out, lw_out, g_out, bonus_out,
                     sgb_out, cur_out, carry_scr, *, seq_rows, d_b):
    tm = x_ref.shape[0]
    xn = _rms(x_ref[...], g1_ref[...]).astype(BF16)
    cur = jnp.dot(xn, wcur_ref[...], preferred_element_type=F32)
    rolled = pltpu.roll(cur, 1, axis=0)
    row = lax.broadcasted_iota(jnp.int32, (tm, 1), 0)
    if seq_rows is None:
        first_tile = pl.program_id(1) == 0
        carry = jnp.where(first_tile, 0.0, carry_scr[...])
        prev = jnp.where(row == 0, carry, rolled)
        carry_scr[...] = cur[tm - 1:tm, :]
        cur_out[...] = cur[tm - 8:tm, :]
    else:
        prev = jnp.where(row % seq_rows == 0, ext_ref[...], rolled)
        cur_out[...] = cur
    xs = cur + (prev - cur) * mu_ref[...]
    r = xs[:, 0:d_b]
    k = xs[:, d_b:2 * d_b]
    v = xs[:, 2 * d_b:3 * d_b]
    o = 3 * d_b
    xw = xs[:, o:o + R_W]
    xa = xs[:, o + R_W:o + R_W + R_A]
    xg = xs[:, o + R_W + R_A:o + R_W + R_A + R_G]
    wl = w0_ref[...] + _dot3_stacked(jnp.tanh(xw), w2_ref[...])
    lw = -jnp.exp(-_softplus(-wl) - 0.5)
    a = _sigmoid(a0_ref[...] + _dot3_stacked(xa, a2_ref[...]))
    g = _dot2_stacked(_sigmoid(xg), g2_ref[...])
    bd = bd_ref[...]
    kkr = k * kk_ref[...]
    n2 = _head_sums(kkr * kkr, bd)
    kk = kkr / jnp.maximum(jnp.sqrt(n2), 1e-12)
    k2 = k * (1.0 + (a - 1.0) * ka_ref[...])
    bonus = _head_sums(r * k2 * rk_ref[...], bd) * v
    r_out[...] = r
    k_out[...] = k2
    v_out[...] = v
    kk_out[...] = kk
    kka_out[...] = kk * a
    lw_out[...] = lw
    g_out[...] = g
    bonus_out[...] = bonus
    sgb_out[...] = _sigmoid(jnp.dot(xn, wgb_ref[...], preferred_element_type=F32))


def _branch_b(x3d, tm, seq_rows, ext, g1, wcur, wgb, mu, w0, w2, a0, a2, g2, k_k, k_a, r_k, bd):
    nb, t, d = x3d.shape
    d_b = w0.shape[1]
    shift_w = wcur.shape[1]
    nt = t // tm
    tok = lambda w: pl.BlockSpec((None, tm, w), lambda b, i: (b, i, 0))
    cur_rows = 8 if seq_rows is None else tm
    outs = [jax.ShapeDtypeStruct((nb, t, d_b), F32)] * 8 + [
        jax.ShapeDtypeStruct((nb, t, d), F32),
        jax.ShapeDtypeStruct((nb, nt * cur_rows, shift_w), F32)]
    out_specs = [tok(d_b)] * 8 + [tok(d), pl.BlockSpec((None, cur_rows, shift_w),
                                                       lambda b, i: (b, i, 0))]
    weights = (g1, wcur, wgb, mu)
    small = (w0, w2, a0, a2, g2, k_k, k_a, r_k, bd)
    return pl.pallas_call(
        functools.partial(_branch_b_kernel, seq_rows=seq_rows, d_b=d_b),
        grid=(nb, nt),
        in_specs=[tok(d)] + [_full_spec(w.shape) for w in weights]
        + [tok(shift_w) if seq_rows is not None else _full_spec(ext.shape)]
        + [_full_spec(w.shape) for w in small],
        out_specs=out_specs,
        out_shape=outs,
        scratch_shapes=[pltpu.VMEM((1, shift_w), F32)],
        compiler_params=pltpu.CompilerParams(dimension_semantics=("parallel", "arbitrary"),
                                             vmem_limit_bytes=VMEM_LIMIT),
        name="branch_b",
    )(x3d, *weights, ext, *small)


def _scan_kernel(*refs, chunk, zero_state):
    r_ref, k_ref, v_ref, kk_ref, kka_ref, lw_ref = refs[:6]
    s0_ref = None if zero_state else refs[6]
    y_ref, sout_ref, s_scr = refs[-3:]
    L = chunk
    W2 = 2 * L
    c = pl.program_id(1)
    n_seq, n_pairs = s_scr.shape[:2]

    @pl.when(c == 0)
    def _():
        if zero_state:
            s_scr[...] = jnp.zeros_like(s_scr)
        else:
            z = jnp.zeros((HEAD_B, HEAD_B), F32)
            for s in range(n_seq):
                for p in range(n_pairs):
                    nat = jnp.concatenate(
                        [jnp.concatenate([s0_ref[s, 2 * p], z], axis=1),
                         jnp.concatenate([z, s0_ref[s, 2 * p + 1]], axis=1)], axis=0)
                    s_scr[s, p] = nat.T

    iota = lambda shape, dim: lax.broadcasted_iota(jnp.int32, shape, dim)
    stack_mask = (iota((W2, 1), 0) >= L) == (iota((1, PAIR), 1) >= HEAD_B)
    bd_mask = (iota((W2, 1), 0) >= L) == (iota((1, W2), 1) >= L)
    assert L & (L - 1) == 0, "chunk length must be a power of two"
    rw = iota((L, W2), 0)
    cw = iota((L, W2), 1) & (L - 1)
    strict_w = cw < rw
    eye_w = (cw == rw).astype(F32)
    incl_w2 = (iota((L, 2 * W2), 1) & (L - 1)) <= iota((L, 2 * W2), 0)
    tri = (iota((L, L), 1) <= iota((L, L), 0)).astype(BF16)
    di = iota((PAIR, PAIR), 0)
    dj = iota((PAIR, PAIR), 1)
    diag = di == dj
    head_diag = (di >= HEAD_B) == (dj >= HEAD_B)

    def stack(x):
        return jnp.where(stack_mask, jnp.concatenate([x, x], axis=0), 0.0)

    def bd(xw):
        return jnp.where(bd_mask, jnp.concatenate([xw, xw], axis=0), 0.0)

    n_double = max(L.bit_length() - 2, 0)
    prep = []
    for s in range(n_seq):
        lw_all = lw_ref[s]
        cum_all = _dot_exact_lhs(tri, lw_all)
        for p in range(n_pairs):
            sl = slice(p * PAIR, (p + 1) * PAIR)
            cum = cum_all[:, sl]
            cum_last = cum[L - 1:L, :]
            e_neg = jnp.exp(-cum)
            e_rel = jnp.exp(cum_last - cum)
            kk = kk_ref[s, :, sl]
            kka = kka_ref[s, :, sl]
            kx = k_ref[s, :, sl]
            prep.append(dict(
                s=s, p=p, sl=sl, ab=-kk * jnp.exp(cum - lw_all[:, sl]),
                rb=r_ref[s, :, sl] * jnp.exp(cum), bb=kka * e_neg, kb=kx * e_neg,
                bt=kka * e_rel, kt=kx * e_rel, v=v_ref[s, :, sl], decay=jnp.exp(cum_last)))
    a_w = [lax.dot_general(
        jnp.concatenate([q["ab"], q["rb"]], axis=0).astype(BF16),
        jnp.concatenate([stack(q["bb"]), stack(q["kb"])], axis=0).astype(BF16),
        (((1,), (1,)), ((), ())), preferred_element_type=F32) for q in prep]
    nw = [jnp.where(strict_w, a[:L, :W2], 0.0) for a in a_w]
    tw = [eye_w + n for n in nw]
    nbd = [bd(n) for n in nw]
    for _ in range(n_double):
        nw = [_dot(n, b) for n, b in zip(nw, nbd)]
        nbd = [bd(n) for n in nw]
        tw = [t + _dot(t, b) for t, b in zip(tw, nbd)]
    akv = [_dot(jnp.where(strict_w, a[:L, W2:], 0.0), stack(q["v"])) for a, q in zip(a_w, prep)]
    tx = [_dot(t, jnp.concatenate([stack(q["ab"]), stack(kv)], axis=1))
          for t, q, kv in zip(tw, prep, akv)]
    ry = [_dot(jnp.where(incl_w2, a[L:], 0.0),
               jnp.concatenate(
                   [jnp.concatenate([stack(x[:, :PAIR]), stack(x[:, PAIR:])], axis=1),
                    jnp.concatenate([jnp.zeros((W2, PAIR), F32), stack(q["v"])], axis=1)], axis=0))
          for a, x, q in zip(a_w, tx, prep)]
    for q, x, y in zip(prep, tx, ry):
        st = s_scr[q["s"], q["p"]]
        us = _dot(jnp.concatenate([x[:, :PAIR], q["rb"] + y[:, :PAIR]], axis=0), st)
        y_ref[q["s"], :, q["sl"]] = us[L:] + y[:, PAIR:]
        u = us[:L] + x[:, PAIR:]
        lhs = jnp.concatenate([jnp.where(diag, q["decay"], 0.0),
                               jnp.concatenate([q["bt"], q["kt"]], axis=0).T], axis=1)
        g = _dot(lhs, jnp.concatenate([st, u, q["v"]], axis=0))
        s_scr[q["s"], q["p"]] = jnp.where(head_diag, g, 0.0)

    @pl.when(c == pl.num_programs(1) - 1)
    def _():
        for s in range(n_seq):
            for p in range(n_pairs):
                nat = s_scr[s, p].T
                sout_ref[s, 2 * p] = nat[:HEAD_B, :HEAD_B]
                sout_ref[s, 2 * p + 1] = nat[HEAD_B:, HEAD_B:]


def _scan(r, k, v, kk, kka, lw, s0, chunk, n_seq):
    nb, t, d_b = r.shape
    n_pairs = d_b // PAIR
    n_heads = d_b // HEAD_B
    tok = pl.BlockSpec((n_seq, chunk, d_b), lambda b, c: (b, c, 0))
    st = pl.BlockSpec((n_seq, n_heads, HEAD_B, HEAD_B), lambda b, c: (b, 0, 0, 0))
    zero_state = s0 is None
    return pl.pallas_call(
        functools.partial(_scan_kernel, chunk=chunk, zero_state=zero_state),
        grid=(nb // n_seq, t // chunk),
        in_specs=[tok] * 6 + ([] if zero_state else [st]),
        out_specs=[tok, st],
        out_shape=[jax.ShapeDtypeStruct((nb, t, d_b), F32),
                   jax.ShapeDtypeStruct((nb, n_heads, HEAD_B, HEAD_B), F32)],
        scratch_shapes=[pltpu.VMEM((n_seq, n_pairs, PAIR, PAIR), F32)],
        compiler_params=pltpu.CompilerParams(dimension_semantics=("parallel", "arbitrary"),
                                             vmem_limit_bytes=VMEM_LIMIT),
        name="rwkv_scan",
    )(r, k, v, kk, kka, lw, *(() if zero_state else (s0,)))


def _pack_bf16_pairs(x):
    w = x.shape[1] // 2
    bits = lambda v: lax.bitcast_convert_type(v.astype(BF16).astype(F32), jnp.uint32)
    return (bits(x[:, :w]) >> 16) | (bits(x[:, w:]) & jnp.uint32(0xFFFF0000))


def _unpack_bf16_pairs(u):
    lo = lax.bitcast_convert_type(u << 16, F32)
    hi = lax.bitcast_convert_type(u & jnp.uint32(0xFFFF0000), F32)
    return lo, hi


def _merge_kernel(y_ref, g_ref, bonus_ref, sgb_ref, apart_ref, x_ref, lg_ref, lb_ref, bd_ref,
                  wbo_ref, wo_ref, g2_ref, rwt_ref, rb_ref,
                  h_out, xp_out, topi_out, prob_out, rank_out, cnt_out, cnt_scr):
    step = pl.program_id(0)

    @pl.when(step == 0)
    def _():
        cnt_scr[...] = jnp.zeros_like(cnt_scr)

    bd = bd_ref[...]
    y = y_ref[...]
    inv_n = 1.0 / HEAD_B
    mu = _head_sums(y, bd) * inv_n
    d = y - mu
    var = _head_sums(d * d, bd) * inv_n
    yn = d * lax.rsqrt(var + GN_EPS) * lg_ref[...] + lb_ref[...]
    yb = ((yn + bonus_ref[...]) * g_ref[...]).astype(BF16)
    merged = apart_ref[...] + sgb_ref[...] * jnp.dot(yb, wbo_ref[...], preferred_element_type=F32)
    h = x_ref[...] + jnp.dot(merged.astype(BF16), wo_ref[...], preferred_element_type=F32)
    h_out[...] = h
    xn2 = _rms(h, g2_ref[...])
    xp_out[...] = _pack_bf16_pairs(xn2)
    n_e = rb_ref.shape[0]
    tm = xn2.shape[0]
    xh, xl = _split(xn2)
    rwt = rwt_ref[...]
    nt = (((1,), (1,)), ((), ()))
    t = lax.dot_general(rwt, xh, nt, preferred_element_type=F32)
    logits = (t[:n_e] + t[n_e:] + lax.dot_general(rwt[:n_e], xl, nt, preferred_element_type=F32)
              + rb_ref[...])
    idx = lax.broadcasted_iota(jnp.int32, logits.shape, 0).astype(F32)
    work = logits
    tops, hots, sels = [], [], []
    for _ in range(TOP_K):
        m = jnp.max(work, axis=0, keepdims=True)
        sel = jnp.min(jnp.where(work == m, idx, float(n_e)), axis=0, keepdims=True)
        hot = idx == sel
        tops.append(m)
        hots.append(hot)
        sels.append(sel)
        work = jnp.where(hot, -jnp.inf, work)
    es = [jnp.exp(t - tops[0]) for t in tops]
    denom = es[0] + es[1] + es[2] + es[3]
    topi_out[...] = jnp.concatenate(sels, axis=0).astype(jnp.int32)
    prob_out[...] = jnp.concatenate([e / denom for e in es], axis=0)
    hot_any = jnp.zeros_like(logits)
    for hot in hots:
        hot_any = hot_any + hot.astype(F32)
    ri = lax.broadcasted_iota(jnp.int32, (tm, tm), 0)
    ci = lax.broadcasted_iota(jnp.int32, (tm, tm), 1)
    before = _dot(hot_any, (ri < ci).astype(BF16)) + cnt_scr[...]
    rank_out[...] = jnp.concatenate(
        [jnp.sum(jnp.where(hot, before, 0.0), axis=0, keepdims=True) for hot in hots],
        axis=0).astype(jnp.int32)
    total = cnt_scr[...] + jnp.sum(hot_any, axis=1, keepdims=True)
    cnt_scr[...] = total
    cnt_out[...] = total.astype(jnp.int32)


def _merge(y, g, bonus, sgb, apart, x2d, tm, lnx_g, lnx_b, bd, wbo, wo, g2n, rw, rb):
    n, d = x2d.shape
    d_b = y.shape[1]
    n_e = rb.shape[0]
    tok = lambda w: pl.BlockSpec((tm, w), lambda i: (i, 0))
    per_k = pl.BlockSpec((TOP_K, tm), lambda i: (0, i))
    weights = (lnx_g, lnx_b, bd, wbo, wo, g2n, rw, rb)
    return pl.pallas_call(
        _merge_kernel,
        grid=(n // tm,),
        in_specs=[tok(d_b), tok(d_b), tok(d_b), tok(d), tok(d), tok(d)]
        + [_full_spec(w.shape) for w in weights],
        out_specs=[tok(d), tok(d // 2), per_k, per_k, per_k,
                   pl.BlockSpec((n_e, 1), lambda i: (0, 0))],
        out_shape=[jax.ShapeDtypeStruct((n, d), F32),
                   jax.ShapeDtypeStruct((n, d // 2), jnp.uint32),
                   jax.ShapeDtypeStruct((TOP_K, n), jnp.int32),
                   jax.ShapeDtypeStruct((TOP_K, n), F32),
                   jax.ShapeDtypeStruct((TOP_K, n), jnp.int32),
                   jax.ShapeDtypeStruct((n_e, 1), jnp.int32)],
        scratch_shapes=[pltpu.VMEM((n_e, 1), F32)],
        compiler_params=pltpu.CompilerParams(dimension_semantics=("arbitrary",),
                                             vmem_limit_bytes=VMEM_LIMIT),
        name="merge_router",
    )(y, g, bonus, sgb, apart, x2d, *weights)


MOE_TM = 1024
MOE_SUB = 256
MOE_FF_CHUNK = 512
CAST_ROWS = 256


def _route(topi_p, rank_p, cnt_p, topi_s, rank_s, cnt_s):
    n_e = cnt_p.size
    cnt_p, cnt_s = cnt_p.reshape(n_e), cnt_s.reshape(n_e)
    cnt = cnt_p + cnt_s
    padded = (cnt + MOE_TM - 1) // MOE_TM * MOE_TM
    ends = jnp.cumsum(padded)
    base = ends - padded
    lookup = lambda table, idx: jnp.sum(
        jnp.where(idx[..., None] == jnp.arange(n_e), table, 0), axis=-1)
    pos_t = jnp.concatenate([lookup(base, topi_p) + rank_p,
                             lookup(base + cnt_p, topi_s) + rank_s], axis=1)
    return (pos_t.astype(jnp.int32), (base // MOE_TM).astype(jnp.int32),
            cnt.astype(jnp.int32))


def _moe_kernel(tile0_ref, cnt_ref, xs_hbm, wgu_ref, bgu_ref, wd_ref, bdn_ref, ys_hbm,
                wgu_b, wd_b, xbuf, ybuf, sem_in, sem_out, done_ref):
    e = pl.program_id(0)
    n_e = pl.num_programs(0)

    @pl.when(e == 0)
    def _():
        done_ref[0] = 0

    g0 = done_ref[0]
    tile0 = tile0_ref[e]
    cnt = cnt_ref[e]
    n_t = (cnt + MOE_TM - 1) // MOE_TM
    d, gu = wgu_ref.shape
    d_ff = wd_ref.shape[0]
    half = d // 2

    def in_copy(tile, slot):
        return pltpu.make_async_copy(xs_hbm.at[pl.ds(tile * MOE_TM, MOE_TM)], xbuf.at[slot],
                                     sem_in.at[slot])

    def out_copy(tile, slot):
        return pltpu.make_async_copy(ybuf.at[slot], ys_hbm.at[pl.ds(tile * MOE_TM, MOE_TM)],
                                     sem_out.at[slot])

    @pl.when(jnp.logical_and(e == 0, n_t > 0))
    def _():
        in_copy(tile0, 0).start()

    @pl.when(n_t > 0)
    def _():
        for r0 in range(0, d, CAST_ROWS):
            wgu_b[r0:r0 + CAST_ROWS, :] = wgu_ref[r0:r0 + CAST_ROWS, :].astype(BF16)
        for r0 in range(0, d_ff, CAST_ROWS):
            wd_b[r0:r0 + CAST_ROWS, :] = wd_ref[r0:r0 + CAST_ROWS, :].astype(BF16)

    def tile_step(j, carry):
        slot = (g0 + j) % 2
        in_copy(tile0 + j, slot).wait()

        @pl.when(j + 1 < n_t)
        def _():
            in_copy(tile0 + j + 1, 1 - slot).start()

        @pl.when(g0 + j >= 2)
        def _():
            out_copy(0, slot).wait()

        left = cnt - j * MOE_TM

        def ffn(row0, rows):
            lo, hi = _unpack_bf16_pairs(xbuf[slot, pl.ds(row0, rows)])
            valid = lax.broadcasted_iota(jnp.int32, lo.shape, 0) < left - row0
            x_lo = jnp.where(valid, lo, 0.0).astype(BF16)
            x_hi = jnp.where(valid, hi, 0.0).astype(BF16)

            def proj(c0):
                cols = slice(c0, c0 + MOE_FF_CHUNK)
                return (jnp.dot(x_lo, wgu_b[:half, cols], preferred_element_type=F32)
                        + jnp.dot(x_hi, wgu_b[half:, cols], preferred_element_type=F32)
                        + bgu_ref[:, cols])

            y = jnp.zeros((rows, d), F32)
            for f in range(d_ff // MOE_FF_CHUNK):
                c0 = f * MOE_FF_CHUNK
                gate = jnp.minimum(proj(c0), SWIGLU_LIMIT)
                up = jnp.clip(proj(d_ff + c0), -SWIGLU_LIMIT, SWIGLU_LIMIT)
                hh = (up + 1.0) * gate * _sigmoid(gate * SWIGLU_ALPHA)
                y = y + jnp.dot(hh.astype(BF16), wd_b[c0:c0 + MOE_FF_CHUNK, :],
                                preferred_element_type=F32)
            ybuf[slot, pl.ds(row0, rows)] = _pack_bf16_pairs(y + bdn_ref[...])

        n_sub = (jnp.minimum(left, MOE_TM) + MOE_SUB - 1) // MOE_SUB

        @pl.when(n_sub == MOE_TM // MOE_SUB)
        def _():
            ffn(0, MOE_TM)

        @pl.when(n_sub < MOE_TM // MOE_SUB)
        def _():
            def sub_step(i, c):
                ffn(pl.multiple_of(i * MOE_SUB, MOE_SUB), MOE_SUB)
                return c
            lax.fori_loop(0, n_sub, sub_step, 0)

        out_copy(tile0 + j, slot).start()
        return carry

    lax.fori_loop(0, n_t, tile_step, 0)

    g1 = g0 + n_t
    done_ref[0] = g1
    nxt = jnp.minimum(e + 1, n_e - 1)

    @pl.when(jnp.logical_and(e + 1 < n_e, cnt_ref[nxt] > 0))
    def _():
        in_copy(tile0_ref[nxt], g1 % 2).start()

    @pl.when(jnp.logical_and(e == n_e - 1, g1 >= 2))
    def _():
        out_copy(0, g1 % 2).wait()

    @pl.when(jnp.logical_and(e == n_e - 1, g1 >= 1))
    def _():
        out_copy(0, (g1 - 1) % 2).wait()


def _moe(xs, tile0, cnt, wgu, bgu, wd, bdn):
    p_rows, half = xs.shape
    n_e, d, gu = wgu.shape
    d_ff = wd.shape[1]
    w_blk = lambda e, tile0, cnt: (e, 0, 0)
    grid_spec = pltpu.PrefetchScalarGridSpec(
        num_scalar_prefetch=2,
        grid=(n_e,),
        in_specs=[pl.BlockSpec(memory_space=pl.ANY),
                  pl.BlockSpec((None, d, gu), w_blk),
                  pl.BlockSpec((None, 1, gu), w_blk),
                  pl.BlockSpec((None, d_ff, d), w_blk),
                  pl.BlockSpec((None, 1, d), w_blk)],
        out_specs=pl.BlockSpec(memory_space=pl.ANY),
        scratch_shapes=[pltpu.VMEM((d, gu), BF16), pltpu.VMEM((d_ff, d), BF16),
                        pltpu.VMEM((2, MOE_TM, half), jnp.uint32),
                        pltpu.VMEM((2, MOE_TM, half), jnp.uint32),
                        pltpu.SemaphoreType.DMA((2,)), pltpu.SemaphoreType.DMA((2,)),
                        pltpu.SMEM((1,), jnp.int32)])
    return pl.pallas_call(
        _moe_kernel,
        grid_spec=grid_spec,
        out_shape=jax.ShapeDtypeStruct((p_rows, half), jnp.uint32),
        compiler_params=pltpu.CompilerParams(dimension_semantics=("arbitrary",),
                                             vmem_limit_bytes=VMEM_LIMIT),
        name="moe_experts",
    )(tile0, cnt, xs, wgu, bgu, wd, bdn)


def _combine_kernel(h_ref, yg_ref, prob_ref, gf_ref, out_ref):
    tm = h_ref.shape[0]
    half = h_ref.shape[1] // 2
    p_rows = jnp.concatenate([prob_ref[...], jnp.zeros((8 - TOP_K, tm), F32)], axis=0)
    ph = p_rows.astype(BF16)
    pm, pl_ = _split(p_rows - ph.astype(F32))
    eye = (lax.broadcasted_iota(jnp.int32, (tm, tm), 0)
           == lax.broadcasted_iota(jnp.int32, (tm, tm), 1)).astype(BF16)
    nt = (((1,), (1,)), ((), ()))
    prob = sum(lax.dot_general(eye, piece, nt, preferred_element_type=F32)
               for piece in (ph, pm, pl_))
    acc_lo = jnp.zeros((tm, half), F32)
    acc_hi = jnp.zeros((tm, half), F32)
    for k in range(TOP_K):
        lo, hi = _unpack_bf16_pairs(yg_ref[k])
        pk = prob[:, k:k + 1]
        acc_lo = acc_lo + pk * lo
        acc_hi = acc_hi + pk * hi
    z = h_ref[...] + jnp.concatenate([acc_lo, acc_hi], axis=1)
    out_ref[...] = _rms(z, gf_ref[...])


def _combine(h, yg, row0, prob, gf, tm):
    n, d = h.shape
    tok = lambda w: pl.BlockSpec((tm, w), lambda i: (i, 0))
    t0 = row0 // tm
    return pl.pallas_call(
        _combine_kernel,
        grid=(n // tm,),
        in_specs=[tok(d), pl.BlockSpec((TOP_K, tm, d // 2), lambda i: (0, i + t0, 0)),
                  pl.BlockSpec((TOP_K, tm), lambda i: (0, i)), _full_spec(gf.shape)],
        out_specs=tok(d),
        out_shape=jax.ShapeDtypeStruct((n, d), F32),
        compiler_params=pltpu.CompilerParams(dimension_semantics=("parallel",),
                                             vmem_limit_bytes=VMEM_LIMIT),
        name="moe_combine",
    )(h, yg, prob, gf)


SC_CORES = 2
SC_SUBCORES = 16
SC_WORKERS = SC_CORES * SC_SUBCORES
SC_MAX_INDEX = 128


def _sc_chunk(rows_per_worker, limit=SC_MAX_INDEX):
    for c in range(limit, 7, -8):
        if rows_per_worker % c == 0:
            return c
    raise ValueError(f"no 8-aligned chunk divides {rows_per_worker} rows")


def _sc_mesh():
    return plsc.VectorSubcoreMesh(core_axis_name="c", subcore_axis_name="s")


def _sc_worker():
    return lax.axis_index("s") * SC_CORES + lax.axis_index("c")


def _scatter_rows(xs_in, pos_t, p_rows):
    w = xs_in[0].shape[1]
    dtype = xs_in[0].dtype
    n_k, n = pos_t.shape
    pos_flat = pos_t.reshape(n_k * n)
    parts, row0 = [], 0
    for x in xs_in:
        per = x.shape[0] // SC_WORKERS
        parts.append((row0, per, _sc_chunk(per, SC_MAX_INDEX // 2)))
        row0 += x.shape[0]
    max_chunk = max(c for _, _, c in parts)
    max_per = max(p for _, p, _ in parts)

    def body(*refs):
        x_hbms = refs[:len(xs_in)]
        pos_hbm, out_hbm, idx_v, rows_v, sem_l, sem_s = refs[len(xs_in):]
        for x_hbm, (tok0, per, chunk) in zip(x_hbms, parts):
            n_chunks = per // chunk
            base = pl.multiple_of(_sc_worker() * per, 8)
            for k in range(n_k):
                src = pl.multiple_of(k * n + tok0 + base, 8)
                pltpu.sync_copy(pos_hbm.at[pl.ds(src, per)], idx_v.at[pl.ds(k * per, per)])

            def load(c, slot):
                src = x_hbm.at[pl.ds(pl.multiple_of(base + c * chunk, 8), chunk)]
                return pltpu.make_async_copy(src, rows_v.at[slot, pl.ds(0, chunk)], sem_l.at[slot])

            def scatter(c, k, slot):
                rows = idx_v.at[pl.ds(pl.multiple_of(k * per + c * chunk, 8), chunk)]
                return pltpu.make_async_copy(rows_v.at[slot, pl.ds(0, chunk)], out_hbm.at[rows],
                                             sem_s.at[slot])

            load(0, 0).start()

            @pl.loop(0, n_chunks)
            def _(c):
                slot = c % 2
                load(c, slot).wait()

                @pl.when(c + 1 < n_chunks)
                def _():
                    @pl.when(c >= 1)
                    def _():
                        for k in range(n_k):
                            scatter(c - 1, k, 1 - slot).wait()
                    load(c + 1, 1 - slot).start()

                for k in range(n_k):
                    scatter(c, k, slot).start()

            if n_chunks >= 2:
                for k in range(n_k):
                    scatter(n_chunks - 2, k, n_chunks % 2).wait()
            for k in range(n_k):
                scatter(n_chunks - 1, k, (n_chunks - 1) % 2).wait()

    return pl.kernel(
        body, out_type=jax.ShapeDtypeStruct((p_rows, w), dtype), mesh=_sc_mesh(),
        scratch_types=[pltpu.VMEM((n_k * max_per,), jnp.int32),
                       pltpu.VMEM((2, max_chunk, w), dtype),
                       pltpu.SemaphoreType.DMA((2,)), pltpu.SemaphoreType.DMA((2,))],
        name="sc_scatter_rows")(*xs_in, pos_flat)


def _gather_rows(table, idx):
    n = idx.shape[0]
    w = table.shape[1]
    per = n // SC_WORKERS
    chunk = _sc_chunk(per)

    n_chunks = per // chunk

    def body(table_hbm, idx_hbm, out_hbm, idx_v, rows_v, sem_g, sem_w):
        base = pl.multiple_of(_sc_worker() * per, 8)
        pltpu.sync_copy(idx_hbm.at[pl.ds(base, per)], idx_v)

        def gather(c, slot):
            rows = idx_v.at[pl.ds(pl.multiple_of(c * chunk, 8), chunk)]
            return pltpu.make_async_copy(table_hbm.at[rows], rows_v.at[slot], sem_g.at[slot])

        def write(c, slot):
            dst = out_hbm.at[pl.ds(pl.multiple_of(base + c * chunk, 8), chunk)]
            return pltpu.make_async_copy(rows_v.at[slot], dst, sem_w.at[slot])

        gather(0, 0).start()

        @pl.loop(0, n_chunks)
        def _(c):
            slot = c % 2
            gather(c, slot).wait()

            @pl.when(c + 1 < n_chunks)
            def _():
                @pl.when(c >= 1)
                def _():
                    write(c - 1, 1 - slot).wait()
                gather(c + 1, 1 - slot).start()

            write(c, slot).start()

        if n_chunks >= 2:
            write(n_chunks - 2, n_chunks % 2).wait()
        write(n_chunks - 1, (n_chunks - 1) % 2).wait()

    return pl.kernel(
        body, out_type=jax.ShapeDtypeStruct((n, w), table.dtype), mesh=_sc_mesh(),
        scratch_types=[pltpu.VMEM((per,), jnp.int32), pltpu.VMEM((2, chunk, w), table.dtype),
                       pltpu.SemaphoreType.DMA((2,)), pltpu.SemaphoreType.DMA((2,))],
        name="sc_gather_rows")(table, idx)


def _mix_matrix(ws, block, seq_rows):
    causal = jnp.tril(jnp.ones((CHUNK, CHUNK), dtype=bool))
    w = jnp.where(causal[None], ws, 0.0)[:, :seq_rows, :seq_rows]
    eye = jnp.eye(block // seq_rows, dtype=ws.dtype)
    return jnp.einsum("ab,gts->gatbs", eye, w).reshape(ws.shape[0], block, block).astype(BF16)


def _stream(x, shift_in, wkv_in, p, tm, seq_rows, scan_chunk, scan_seqs):
    nb, t, d = x.shape
    n = nb * t
    x2d = x.reshape(n, d)
    rows = CHUNK if seq_rows is None else seq_rows
    wmix = _mix_matrix(p["w_spatial"], max(rows, LANES), rows)
    pos = jnp.arange(tm) % rows
    d_a = p["wu"].shape[1]
    bias_full = jnp.repeat(p["b_spatial"].T[pos], d_a // G_A, axis=1)
    apart, vn = _branch_a(x2d, tm, p["norm1_g"], p["wu"], p["wv"], p["wga"], p["vnorm_g"],
                          p["vnorm_b"], wmix, bias_full, p["w_a_out"])
    if seq_rows is None:
        xb, ext = x, jnp.zeros((1, p["wcur"].shape[1]), F32)
    else:
        xb = x2d.reshape(n // tm, tm, d)
        ext = jnp.repeat(shift_in, seq_rows, axis=0).reshape(n // tm, tm, -1)
    outs = _branch_b(xb, tm, seq_rows, ext, p["norm1_g"], p["wcur"], p["wgb"], p["mu_shift"],
                     p["w0"], p["w2"], p["a0"], p["a2"], p["g2"], p["k_k"], p["k_a"], p["r_k"],
                     p["bd"])
    r, k2, v, kk, kka, lw, g, bonus, sgb, cur = outs
    d_b = r.shape[-1]
    if seq_rows is None:
        shift_out = cur[:, -1, :]
        scan_in = [a.reshape(nb, t, d_b) for a in (r, k2, v, kk, kka, lw)]
    else:
        shift_out = cur.reshape(nb, t, -1)[:, -1, :]
        pad = scan_chunk - t
        scan_in = [jnp.pad(a.reshape(nb, t, d_b), ((0, 0), (0, pad), (0, 0)))
                   for a in (r, k2, v, kk, kka, lw)]
    y, s_out = _scan(*scan_in, wkv_in, scan_chunk, scan_seqs)
    y2d = y[:, :t].reshape(n, d_b)
    flat = lambda a: a.reshape(n, a.shape[-1])
    routed = _merge(y2d, flat(g), flat(bonus), flat(sgb), apart, x2d, tm, p["lnx_g"],
                    p["lnx_b"], p["bd"], p["w_b_out"], p["w_out"], p["norm2_g"],
                    p["router_w"], p["router_b"])
    return routed, vn, shift_out, s_out


def kernel(x_prompt, x_sample, state_shift, state_wkv, norm1_g, w_in, mu_shift, vnorm_g, vnorm_b, w_spatial, b_spatial, w_a_out, w0, w2, a0, a2, g2, k_k, k_a, r_k, lnx_g, lnx_b, w_b_out, w_out, norm2_g, router_w, router_b, exp_w_gu, exp_b_gu, exp_w_down, exp_b_down, normf_g):
    depth = w_in.shape[0]
    assert depth == 1, "the final norm is fused into the single layer's MoE call"
    d_model = x_prompt.shape[-1]
    d_b = w0.shape[-1]
    shift_w = mu_shift.shape[-1]
    d_a = vnorm_g.shape[-1]
    bp, tp, _ = x_prompt.shape
    bs, ts, _ = x_sample.shape
    head_id = jnp.arange(2 * LANES) // HEAD_B
    bd = (head_id[:, None] == head_id[None, :]).astype(BF16)
    row = lambda a: a.reshape(1, -1)

    hp, hs = x_prompt, x_sample
    vrows, shifts_p, wkvs_p, shifts_s, wkvs_s = [], [], [], [], []
    for l in range(depth):
        wi = w_in[l].astype(BF16)
        o = shift_w
        p = dict(
            norm1_g=row(norm1_g[l]), wcur=wi[:, :o], wu=wi[:, o:o + d_a],
            wv=wi[:, o + d_a:o + 2 * d_a], wga=wi[:, o + 2 * d_a:o + 2 * d_a + d_model],
            wgb=wi[:, o + 2 * d_a + d_model:], mu_shift=row(mu_shift[l]),
            vnorm_g=row(vnorm_g[l]), vnorm_b=row(vnorm_b[l]), w_spatial=w_spatial[l],
            b_spatial=b_spatial[l], w_a_out=w_a_out[l].astype(BF16), w0=row(w0[l]),
            w2=_stack_rhs3(w2[l]), a0=row(a0[l]), a2=_stack_rhs3(a2[l]),
            g2=jnp.concatenate([g2[l].astype(BF16)] * 2, axis=0),
            k_k=row(k_k[l]), k_a=row(k_a[l]),
            r_k=row(r_k[l]), lnx_g=row(lnx_g[l]), lnx_b=row(lnx_b[l]),
            w_b_out=w_b_out[l].astype(BF16), w_out=w_out[l].astype(BF16),
            norm2_g=row(norm2_g[l]), router_w=jnp.concatenate(_split(router_w[l].T), axis=0),
            router_b=router_b[l].reshape(-1, 1), bd=bd)
        routed_p, _, sh_p, s_p = _stream(hp, None, None, p, 512, None, 64, 4)
        routed_s, vn_s, sh_s, s_s = _stream(hs, state_shift[l], state_wkv[l], p, 512, ts, 8, 8)
        h_p, xp_p, topi_p, prob_p, rank_p, cnt_p = routed_p
        h_s, xp_s, topi_s, prob_s, rank_s, cnt_s = routed_s
        n_p, n_s = h_p.shape[0], h_s.shape[0]
        n_e = router_w.shape[-1]
        n_tiles = (n_p + n_s) * TOP_K // MOE_TM + n_e
        pos_t, tile0, cnt = _route(topi_p, rank_p, cnt_p, topi_s, rank_s, cnt_s)
        xs = _scatter_rows([xp_p, xp_s], pos_t, n_tiles * MOE_TM)
        ys = _moe(xs, tile0, cnt, exp_w_gu[l], exp_b_gu[l][:, None, :],
                  exp_w_down[l], exp_b_down[l][:, None, :])
        yg = _gather_rows(ys, pos_t.reshape(-1)).reshape(TOP_K, n_p + n_s, d_model // 2)
        gf = row(normf_g)
        hp = _combine(h_p, yg, 0, prob_p, gf, 512).reshape(bp, tp, d_model)
        hs = _combine(h_s, yg, n_p, prob_s, gf, 512).reshape(bs, ts, d_model)
        vrows.append(vn_s.reshape(bs, ts, d_a))
        shifts_p.append(sh_p)
        wkvs_p.append(s_p)
        shifts_s.append(sh_s)
        wkvs_s.append(s_s)
    return (hp, hs, jnp.stack(shifts_p), jnp.stack(wkvs_p), jnp.stack(shifts_s),
            jnp.stack(wkvs_s), jnp.stack(vrows))
```

```python
import functools

import jax
import jax.numpy as jnp
from jax import lax
from jax.experimental import pallas as pl
from jax.experimental.pallas import tpu as pltpu
from jax.experimental.pallas import tpu_sc as plsc

F32 = jnp.float32
BF16 = jnp.bfloat16

CHUNK = 128
G_A = 8
HEAD_B = 64
R_W, R_A, R_G = 64, 64, 128
TOP_K = 4
SWIGLU_LIMIT = 7.0
SWIGLU_ALPHA = 1.702
EPS = 1e-5
GN_EPS = HEAD_B * 1e-5

LANES = 128
PAIR = 2 * HEAD_B
VMEM_LIMIT = 56 * 1024 * 1024


def _dot(a, b):
    return jnp.dot(a.astype(BF16), b.astype(BF16), preferred_element_type=F32)


def _split(x):
    hi = x.astype(BF16)
    lo = (x - hi.astype(F32)).astype(BF16)
    return hi, lo


def _stack_rhs3(b):
    bh, bl = _split(b)
    return jnp.concatenate([bh, bh, bl], axis=0)


def _dot3_stacked(a, b_stacked):
    ah, al = _split(a)
    return jnp.dot(jnp.concatenate([ah, al, ah], axis=1), b_stacked, preferred_element_type=F32)


def _dot2_stacked(a, b_stacked):
    ah, al = _split(a)
    return jnp.dot(jnp.concatenate([ah, al], axis=1), b_stacked, preferred_element_type=F32)


def _head_sums(x, ones_bd):
    w = ones_bd.shape[0]
    return jnp.concatenate([jnp.dot(x[:, c:c + w].astype(BF16), ones_bd,
                                    preferred_element_type=F32)
                            for c in range(0, x.shape[1], w)], axis=1)


def _dot_exact_lhs(a_bf16, b):
    bh, bl = _split(b)
    bm = b - bh.astype(F32) - bl.astype(F32)
    return (jnp.dot(a_bf16, bh, preferred_element_type=F32)
            + jnp.dot(a_bf16, bl, preferred_element_type=F32)
            + jnp.dot(a_bf16, bm.astype(BF16), preferred_element_type=F32))


def _rms(x, g):
    return x * lax.rsqrt(jnp.mean(x * x, axis=-1, keepdims=True) + EPS) * g


def _sigmoid(x):
    return 1.0 / (1.0 + jnp.exp(-x))


def _softplus(z):
    return jnp.maximum(z, 0.0) + jnp.log(1.0 + jnp.exp(-jnp.abs(z)))


def _full_spec(shape):
    nd = len(shape)
    return pl.BlockSpec(shape, lambda *_: (0,) * nd, pipeline_mode=pl.Buffered(1))


def _branch_a_kernel(x_ref, g1_ref, wu_ref, wv_ref, wga_ref, vg_ref, vb_ref, wmix_ref,
                     bias_ref, wao_ref, apart_ref, vn_ref):
    xn = _rms(x_ref[...], g1_ref[...]).astype(BF16)
    v = jnp.dot(xn, wv_ref[...], preferred_element_type=F32)
    mu = jnp.mean(v, axis=-1, keepdims=True)
    d = v - mu
    var = jnp.mean(d * d, axis=-1, keepdims=True)
    vn = d * lax.rsqrt(var + EPS) * vg_ref[...] + vb_ref[...]
    vn_ref[...] = vn
    vnb = vn.astype(BF16)
    c_a = vnb.shape[1] // G_A
    rb = wmix_ref.shape[1]
    mixed = jnp.concatenate(
        [jnp.concatenate(
            [jnp.dot(wmix_ref[g], vnb[r0:r0 + rb, g * c_a:(g + 1) * c_a],
                     preferred_element_type=F32) for g in range(G_A)], axis=1)
         for r0 in range(0, vnb.shape[0], rb)], axis=0) + bias_ref[...]
    u = jnp.dot(xn, wu_ref[...], preferred_element_type=F32)
    ya = (u * mixed).astype(BF16)
    ga = jnp.dot(xn, wga_ref[...], preferred_element_type=F32)
    apart_ref[...] = _sigmoid(ga) * jnp.dot(ya, wao_ref[...], preferred_element_type=F32)


def _branch_a(x2d, tm, g1, wu, wv, wga, vg, vb, wmix, bias_full, wao):
    n, d = x2d.shape
    d_a = wu.shape[1]
    tok = lambda w: pl.BlockSpec((tm, w), lambda i: (i, 0))
    return pl.pallas_call(
        _branch_a_kernel,
        grid=(n // tm,),
        in_specs=[tok(d), _full_spec(g1.shape), _full_spec(wu.shape), _full_spec(wv.shape),
                  _full_spec(wga.shape), _full_spec(vg.shape), _full_spec(vb.shape),
                  _full_spec(wmix.shape), _full_spec(bias_full.shape), _full_spec(wao.shape)],
        out_specs=[tok(d), tok(d_a)],
        out_shape=[jax.ShapeDtypeStruct((n, d), F32), jax.ShapeDtypeStruct((n, d_a), F32)],
        compiler_params=pltpu.CompilerParams(dimension_semantics=("parallel",),
                                             vmem_limit_bytes=VMEM_LIMIT),
        name="branch_a",
    )(x2d, g1, wu, wv, wga, vg, vb, wmix, bias_full, wao)


def _branch_b_kernel(x_ref, g1_ref, wcur_ref, wgb_ref, mu_ref, ext_ref, w0_ref, w2_ref,
                     a0_ref, a2_ref, g2_ref, kk_ref, ka_ref, rk_ref, bd_ref,
                     r_out, k_out, v_out, kk_out, kka_out, lw_out, g_out, bonus_out,
                     sgb_out, cur_out, carry_scr, *, seq_rows, d_b):
    tm = x_ref.shape[0]
    xn = _rms(x_ref[...], g1_ref[...]).astype(BF16)
    cur = jnp.dot(xn, wcur_ref[...], preferred_element_type=F32)
    rolled = pltpu.roll(cur, 1, axis=0)
    row = lax.broadcasted_iota(jnp.int32, (tm, 1), 0)
    if seq_rows is None:
        first_tile = pl.program_id(1) == 0
        carry = jnp.where(first_tile, 0.0, carry_scr[...])
        prev = jnp.where(row == 0, carry, rolled)
        carry_scr[...] = cur[tm - 1:tm, :]
        cur_out[...] = cur[tm - 8:tm, :]
    else:
        prev = jnp.where(row % seq_rows == 0, ext_ref[...], rolled)
        cur_out[...] = cur
    xs = cur + (prev - cur) * mu_ref[...]
    r = xs[:, 0:d_b]
    k = xs[:, d_b:2 * d_b]
    v = xs[:, 2 * d_b:3 * d_b]
    o = 3 * d_b
    xw = xs[:, o:o + R_W]
    xa = xs[:, o + R_W:o + R_W + R_A]
    xg = xs[:, o + R_W + R_A:o + R_W + R_A + R_G]
    wl = w0_ref[...] + _dot3_stacked(jnp.tanh(xw), w2_ref[...])
    lw = -jnp.exp(-_softplus(-wl) - 0.5)
    a = _sigmoid(a0_ref[...] + _dot3_stacked(xa, a2_ref[...]))
    g = _dot2_stacked(_sigmoid(xg), g2_ref[...])
    bd = bd_ref[...]
    kkr = k * kk_ref[...]
    n2 = _head_sums(kkr * kkr, bd)
    kk = kkr / jnp.maximum(jnp.sqrt(n2), 1e-12)
    k2 = k * (1.0 + (a - 1.0) * ka_ref[...])
    bonus = _head_sums(r * k2 * rk_ref[...], bd) * v
    r_out[...] = r
    k_out[...] = k2
    v_out[...] = v
    kk_out[...] = kk
    kka_out[...] = kk * a
    lw_out[...] = lw
    g_out[...] = g
    bonus_out[...] = bonus
    sgb_out[...] = _sigmoid(jnp.dot(xn, wgb_ref[...], preferred_element_type=F32))


def _branch_b(x3d, tm, seq_rows, ext, g1, wcur, wgb, mu, w0, w2, a0, a2, g2, k_k, k_a, r_k, bd):
    nb, t, d = x3d.shape
    d_b = w0.shape[1]
    shift_w = wcur.shape[1]
    nt = t // tm
    tok = lambda w: pl.BlockSpec((None, tm, w), lambda b, i: (b, i, 0))
    cur_rows = 8 if seq_rows is None else tm
    outs = [jax.ShapeDtypeStruct((nb, t, d_b), F32)] * 8 + [
        jax.ShapeDtypeStruct((nb, t, d), F32),
        jax.ShapeDtypeStruct((nb, nt * cur_rows, shift_w), F32)]
    out_specs = [tok(d_b)] * 8 + [tok(d), pl.BlockSpec((None, cur_rows, shift_w),
                                                       lambda b, i: (b, i, 0))]
    weights = (g1, wcur, wgb, mu)
    small = (w0, w2, a0, a2, g2, k_k, k_a, r_k, bd)
    return pl.pallas_call(
        functools.partial(_branch_b_kernel, seq_rows=seq_rows, d_b=d_b),
        grid=(nb, nt),
        in_specs=[tok(d)] + [_full_spec(w.shape) for w in weights]
        + [tok(shift_w) if seq_rows is not None else _full_spec(ext.shape)]
        + [_full_spec(w.shape) for w in small],
        out_specs=out_specs,
        out_shape=outs,
        scratch_shapes=[pltpu.VMEM((1, shift_w), F32)],
        compiler_params=pltpu.CompilerParams(dimension_semantics=("parallel", "arbitrary"),
                                             vmem_limit_bytes=VMEM_LIMIT),
        name="branch_b",
    )(x3d, *weights, ext, *small)


def _scan_kernel(*refs, chunk, zero_state):
    r_ref, k_ref, v_ref, kk_ref, kka_ref, lw_ref = refs[:6]
    s0_ref = None if zero_state else refs[6]
    y_ref, sout_ref, s_scr = refs[-3:]
    L = chunk
    W2 = 2 * L
    c = pl.program_id(1)
    n_seq, n_pairs = s_scr.shape[:2]

    @pl.when(c == 0)
    def _():
        if zero_state:
            s_scr[...] = jnp.zeros_like(s_scr)
        else:
            z = jnp.zeros((HEAD_B, HEAD_B), F32)
            for s in range(n_seq):
                for p in range(n_pairs):
                    nat = jnp.concatenate(
                        [jnp.concatenate([s0_ref[s, 2 * p], z], axis=1),
                         jnp.concatenate([z, s0_ref[s, 2 * p + 1]], axis=1)], axis=0)
                    s_scr[s, p] = nat.T

    iota = lambda shape, dim: lax.broadcasted_iota(jnp.int32, shape, dim)
    stack_mask = (iota((W2, 1), 0) >= L) == (iota((1, PAIR), 1) >= HEAD_B)
    bd_mask = (iota((W2, 1), 0) >= L) == (iota((1, W2), 1) >= L)
    assert L & (L - 1) == 0, "chunk length must be a power of two"
    rw = iota((L, W2), 0)
    cw = iota((L, W2), 1) & (L - 1)
    strict_w = cw < rw
    eye_w = (cw == rw).astype(F32)
    incl_w2 = (iota((L, 2 * W2), 1) & (L - 1)) <= iota((L, 2 * W2), 0)
    tri = (iota((L, L), 1) <= iota((L, L), 0)).astype(BF16)
    di = iota((PAIR, PAIR), 0)
    dj = iota((PAIR, PAIR), 1)
    diag = di == dj
    head_diag = (di >= HEAD_B) == (dj >= HEAD_B)

    def stack(x):
        return jnp.where(stack_mask, jnp.concatenate([x, x], axis=0), 0.0)

    def bd(xw):
        return jnp.where(bd_mask, jnp.concatenate([xw, xw], axis=0), 0.0)

    n_double = max(L.bit_length() - 2, 0)
    prep = []
    for s in range(n_seq):
        lw_all = lw_ref[s]
        cum_all = _dot_exact_lhs(tri, lw_all)
        for p in range(n_pairs):
            sl = slice(p * PAIR, (p + 1) * PAIR)
            cum = cum_all[:, sl]
            cum_last = cum[L - 1:L, :]
            e_neg = jnp.exp(-cum)
            e_rel = jnp.exp(cum_last - cum)
            kk = kk_ref[s, :, sl]
            kka = kka_ref[s, :, sl]
            kx = k_ref[s, :, sl]
            prep.append(dict(
                s=s, p=p, sl=sl, ab=-kk * jnp.exp(cum - lw_all[:, sl]),
                rb=r_ref[s, :, sl] * jnp.exp(cum), bb=kka * e_neg, kb=kx * e_neg,
                bt=kka * e_rel, kt=kx * e_rel, v=v_ref[s, :, sl], decay=jnp.exp(cum_last)))
    a_w = [lax.dot_general(
        jnp.concatenate([q["ab"], q["rb"]], axis=0).astype(BF16),
        jnp.concatenate([stack(q["bb"]), stack(q["kb"])], axis=0).astype(BF16),
        (((1,), (1,)), ((), ())), preferred_element_type=F32) for q in prep]
    nw = [jnp.where(strict_w, a[:L, :W2], 0.0) for a in a_w]
    tw = [eye_w + n for n in nw]
    nbd = [bd(n) for n in nw]
    for _ in range(n_double):
        nw = [_dot(n, b) for n, b in zip(nw, nbd)]
        nbd = [bd(n) for n in nw]
        tw = [t + _dot(t, b) for t, b in zip(tw, nbd)]
    akv = [_dot(jnp.where(strict_w, a[:L, W2:], 0.0), stack(q["v"])) for a, q in zip(a_w, prep)]
    tx = [_dot(t, jnp.concatenate([stack(q["ab"]), stack(kv)], axis=1))
          for t, q, kv in zip(tw, prep, akv)]
    ry = [_dot(jnp.where(incl_w2, a[L:], 0.0),
               jnp.concatenate(
                   [jnp.concatenate([stack(x[:, :PAIR]), stack(x[:, PAIR:])], axis=1),
                    jnp.concatenate([jnp.zeros((W2, PAIR), F32), stack(q["v"])], axis=1)], axis=0))
          for a, x, q in zip(a_w, tx, prep)]
    for q, x, y in zip(prep, tx, ry):
        st = s_scr[q["s"], q["p"]]
        us = _dot(jnp.concatenate([x[:, :PAIR], q["rb"] + y[:, :PAIR]], axis=0), st)
        y_ref[q["s"], :, q["sl"]] = us[L:] + y[:, PAIR:]
        u = us[:L] + x[:, PAIR:]
        lhs = jnp.concatenate([jnp.where(diag, q["decay"], 0.0),
                               jnp.concatenate([q["bt"], q["kt"]], axis=0).T], axis=1)
        g = _dot(lhs, jnp.concatenate([st, u, q["v"]], axis=0))
        s_scr[q["s"], q["p"]] = jnp.where(head_diag, g, 0.0)

    @pl.when(c == pl.num_programs(1) - 1)
    def _():
        for s in range(n_seq):
            for p in range(n_pairs):
                nat = s_scr[s, p].T
                sout_ref[s, 2 * p] = nat[:HEAD_B, :HEAD_B]
                sout_ref[s, 2 * p + 1] = nat[HEAD_B:, HEAD_B:]


def _scan(r, k, v, kk, kka, lw, s0, chunk, n_seq):
    nb, t, d_b = r.shape
    n_pairs = d_b // PAIR
    n_heads = d_b // HEAD_B
    tok = pl.BlockSpec((n_seq, chunk, d_b), lambda b, c: (b, c, 0))
    st = pl.BlockSpec((n_seq, n_heads, HEAD_B, HEAD_B), lambda b, c: (b, 0, 0, 0))
    zero_state = s0 is None
    return pl.pallas_call(
        functools.partial(_scan_kernel, chunk=chunk, zero_state=zero_state),
        grid=(nb // n_seq, t // chunk),
        in_specs=[tok] * 6 + ([] if zero_state else [st]),
        out_specs=[tok, st],
        out_shape=[jax.ShapeDtypeStruct((nb, t, d_b), F32),
                   jax.ShapeDtypeStruct((nb, n_heads, HEAD_B, HEAD_B), F32)],
        scratch_shapes=[pltpu.VMEM((n_seq, n_pairs, PAIR, PAIR), F32)],
        compiler_params=pltpu.CompilerParams(dimension_semantics=("parallel", "arbitrary"),
                                             vmem_limit_bytes=VMEM_LIMIT),
        name="rwkv_scan",
    )(r, k, v, kk, kka, lw, *(() if zero_state else (s0,)))


STEP_ROWS = 8


def _steps_kernel(r_ref, k_ref, v_ref, kk_ref, kka_ref, lw_ref, s0_ref, y_ref, s_ref,
                  vec_scr, yt_scr, *, n_steps):
    n_seq = s_ref.shape[-1]
    s_ref[...] = s0_ref[...]
    for t in range(n_steps):
        rows = pl.ds(t, n_seq, stride=n_steps)
        vecs = {}
        for slot, (name, ref) in enumerate((("r", r_ref), ("k", k_ref), ("v", v_ref),
                                            ("kk", kk_ref), ("kka", kka_ref), ("lw", lw_ref))):
            vecs[name] = ref[rows, :].T
            vec_scr[slot] = vecs[name]
        for q in range(PAIR // HEAD_B):
            hs = slice(q * HEAD_B, (q + 1) * HEAD_B)
            r_q, k_q, kka_q = vecs["r"][hs], vecs["k"][hs], vecs["kka"][hs]
            nkk_q = -vecs["kk"][hs]
            w_q = jnp.exp(vecs["lw"][hs])

            def rows_step(i0, carry):
                for u in range(STEP_ROWS):
                    i = i0 * STEP_ROWS + u
                    s_row = s_ref[q, i]
                    sa = jnp.sum(s_row * nkk_q, axis=0, keepdims=True)
                    v_i = vec_scr[2, pl.ds(q * HEAD_B + i, 1), :]
                    s_new = s_row * w_q + sa * kka_q + v_i * k_q
                    s_ref[q, i] = s_new
                    yt_scr[pl.ds(q * HEAD_B + i, 1), :] = jnp.sum(s_new * r_q, axis=0,
                                                                  keepdims=True)
                return carry

            lax.fori_loop(0, HEAD_B // STEP_ROWS, rows_step, 0)
        y_ref[rows, :] = yt_scr[...].T


def _scan_steps(r, k, v, kk, kka, lw, s0, n_steps):
    n, d_b = r.shape
    n_heads, _, _, n_seq = s0.shape
    tok = pl.BlockSpec((n, PAIR), lambda p: (0, p))
    st = pl.BlockSpec((PAIR // HEAD_B, HEAD_B, HEAD_B, n_seq), lambda p: (p, 0, 0, 0))
    return pl.pallas_call(
        functools.partial(_steps_kernel, n_steps=n_steps),
        grid=(d_b // PAIR,),
        in_specs=[tok] * 6 + [st],
        out_specs=[tok, st],
        out_shape=[jax.ShapeDtypeStruct((n, d_b), F32), jax.ShapeDtypeStruct(s0.shape, F32)],
        scratch_shapes=[pltpu.VMEM((6, PAIR, n_seq), F32), pltpu.VMEM((PAIR, n_seq), F32)],
        compiler_params=pltpu.CompilerParams(dimension_semantics=("parallel",),
                                             vmem_limit_bytes=VMEM_LIMIT),
        name="rwkv_steps",
    )(r, k, v, kk, kka, lw, s0)


def _pack_bf16_pairs(x):
    w = x.shape[1] // 2
    bits = lambda v: lax.bitcast_convert_type(v.astype(BF16).astype(F32), jnp.uint32)
    return (bits(x[:, :w]) >> 16) | (bits(x[:, w:]) & jnp.uint32(0xFFFF0000))


def _unpack_bf16_pairs(u):
    lo = lax.bitcast_convert_type(u << 16, F32)
    hi = lax.bitcast_convert_type(u & jnp.uint32(0xFFFF0000), F32)
    return lo, hi


def _merge_kernel(y_ref, g_ref, bonus_ref, sgb_ref, apart_ref, x_ref, lg_ref, lb_ref, bd_ref,
                  wbo_ref, wo_ref, g2_ref, rwt_ref, rb_ref,
                  h_out, xp_out, topi_out, prob_out, rank_out, cnt_out, cnt_scr):
    step = pl.program_id(0)

    @pl.when(step == 0)
    def _():
        cnt_scr[...] = jnp.zeros_like(cnt_scr)

    bd = bd_ref[...]
    y = y_ref[...]
    inv_n = 1.0 / HEAD_B
    mu = _head_sums(y, bd) * inv_n
    d = y - mu
    var = _head_sums(d * d, bd) * inv_n
    yn = d * lax.rsqrt(var + GN_EPS) * lg_ref[...] + lb_ref[...]
    yb = ((yn + bonus_ref[...]) * g_ref[...]).astype(BF16)
    merged = apart_ref[...] + sgb_ref[...] * jnp.dot(yb, wbo_ref[...], preferred_element_type=F32)
    h = x_ref[...] + jnp.dot(merged.astype(BF16), wo_ref[...], preferred_element_type=F32)
    h_out[...] = h
    xn2 = _rms(h, g2_ref[...])
    xp_out[...] = _pack_bf16_pairs(xn2)
    n_e = rb_ref.shape[0]
    tm = xn2.shape[0]
    xh, xl = _split(xn2)
    rwt = rwt_ref[...]
    nt = (((1,), (1,)), ((), ()))
    t = lax.dot_general(rwt, xh, nt, preferred_element_type=F32)
    logits = (t[:n_e] + t[n_e:] + lax.dot_general(rwt[:n_e], xl, nt, preferred_element_type=F32)
              + rb_ref[...])
    idx = lax.broadcasted_iota(jnp.int32, logits.shape, 0).astype(F32)
    work = logits
    tops, hots, sels = [], [], []
    for _ in range(TOP_K):
        m = jnp.max(work, axis=0, keepdims=True)
        sel = jnp.min(jnp.where(work == m, idx, float(n_e)), axis=0, keepdims=True)
        hot = idx == sel
        tops.append(m)
        hots.append(hot)
        sels.append(sel)
        work = jnp.where(hot, -jnp.inf, work)
    es = [jnp.exp(t - tops[0]) for t in tops]
    denom = es[0] + es[1] + es[2] + es[3]
    topi_out[...] = jnp.concatenate(sels, axis=0).astype(jnp.int32)
    prob_out[...] = jnp.concatenate([e / denom for e in es], axis=0)
    hot_any = jnp.zeros_like(logits)
    for hot in hots:
        hot_any = hot_any + hot.astype(F32)
    ri = lax.broadcasted_iota(jnp.int32, (tm, tm), 0)
    ci = lax.broadcasted_iota(jnp.int32, (tm, tm), 1)
    before = _dot(hot_any, (ri < ci).astype(BF16)) + cnt_scr[...]
    rank_out[...] = jnp.concatenate(
        [jnp.sum(jnp.where(hot, before, 0.0), axis=0, keepdims=True) for hot in hots],
        axis=0).astype(jnp.int32)
    total = cnt_scr[...] + jnp.sum(hot_any, axis=1, keepdims=True)
    cnt_scr[...] = total
    cnt_out[...] = total.astype(jnp.int32)


def _merge(y, g, bonus, sgb, apart, x2d, tm, lnx_g, lnx_b, bd, wbo, wo, g2n, rw, rb):
    n, d = x2d.shape
    d_b = y.shape[1]
    n_e = rb.shape[0]
    tok = lambda w: pl.BlockSpec((tm, w), lambda i: (i, 0))
    per_k = pl.BlockSpec((TOP_K, tm), lambda i: (0, i))
    weights = (lnx_g, lnx_b, bd, wbo, wo, g2n, rw, rb)
    return pl.pallas_call(
        _merge_kernel,
        grid=(n // tm,),
        in_specs=[tok(d_b), tok(d_b), tok(d_b), tok(d), tok(d), tok(d)]
        + [_full_spec(w.shape) for w in weights],
        out_specs=[tok(d), tok(d // 2), per_k, per_k, per_k,
                   pl.BlockSpec((n_e, 1), lambda i: (0, 0))],
        out_shape=[jax.ShapeDtypeStruct((n, d), F32),
                   jax.ShapeDtypeStruct((n, d // 2), jnp.uint32),
                   jax.ShapeDtypeStruct((TOP_K, n), jnp.int32),
                   jax.ShapeDtypeStruct((TOP_K, n), F32),
                   jax.ShapeDtypeStruct((TOP_K, n), jnp.int32),
                   jax.ShapeDtypeStruct((n_e, 1), jnp.int32)],
        scratch_shapes=[pltpu.VMEM((n_e, 1), F32)],
        compiler_params=pltpu.CompilerParams(dimension_semantics=("arbitrary",),
                                             vmem_limit_bytes=VMEM_LIMIT),
        name="merge_router",
    )(y, g, bonus, sgb, apart, x2d, *weights)


MOE_TM = 1024
MOE_SUB = 256
MOE_FF_CHUNK = 512
CAST_ROWS = 256


def _route(topi_p, rank_p, cnt_p, topi_s, rank_s, cnt_s):
    n_e = cnt_p.size
    cnt_p, cnt_s = cnt_p.reshape(n_e), cnt_s.reshape(n_e)
    cnt = cnt_p + cnt_s
    padded = (cnt + MOE_TM - 1) // MOE_TM * MOE_TM
    ends = jnp.cumsum(padded)
    base = ends - padded
    lookup = lambda table, idx: jnp.sum(
        jnp.where(idx[..., None] == jnp.arange(n_e), table, 0), axis=-1)
    pos_t = jnp.concatenate([lookup(base, topi_p) + rank_p,
                             lookup(base + cnt_p, topi_s) + rank_s], axis=1)
    return (pos_t.astype(jnp.int32), (base // MOE_TM).astype(jnp.int32),
            cnt.astype(jnp.int32))


def _moe_kernel(tile0_ref, cnt_ref, xs_hbm, wgu_ref, bgu_ref, wd_ref, bdn_ref, ys_hbm,
                wgu_b, wd_b, xbuf, ybuf, sem_in, sem_out, done_ref):
    e = pl.program_id(0)
    n_e = pl.num_programs(0)

    @pl.when(e == 0)
    def _():
        done_ref[0] = 0

    g0 = done_ref[0]
    tile0 = tile0_ref[e]
    cnt = cnt_ref[e]
    n_t = (cnt + MOE_TM - 1) // MOE_TM
    d, gu = wgu_ref.shape
    d_ff = wd_ref.shape[0]
    half = d // 2

    def in_copy(tile, slot):
        return pltpu.make_async_copy(xs_hbm.at[pl.ds(tile * MOE_TM, MOE_TM)], xbuf.at[slot],
                                     sem_in.at[slot])

    def out_copy(tile, slot):
        return pltpu.make_async_copy(ybuf.at[slot], ys_hbm.at[pl.ds(tile * MOE_TM, MOE_TM)],
                                     sem_out.at[slot])

    @pl.when(jnp.logical_and(e == 0, n_t > 0))
    def _():
        in_copy(tile0, 0).start()

    @pl.when(n_t > 0)
    def _():
        for r0 in range(0, d, CAST_ROWS):
            wgu_b[r0:r0 + CAST_ROWS, :] = wgu_ref[r0:r0 + CAST_ROWS, :].astype(BF16)
        for r0 in range(0, d_ff, CAST_ROWS):
            wd_b[r0:r0 + CAST_ROWS, :] = wd_ref[r0:r0 + CAST_ROWS, :].astype(BF16)

    def tile_step(j, carry):
        slot = (g0 + j) % 2
        in_copy(tile0 + j, slot).wait()

        @pl.when(j + 1 < n_t)
        def _():
            in_copy(tile0 + j + 1, 1 - slot).start()

        @pl.when(g0 + j >= 2)
        def _():
            out_copy(0, slot).wait()

        left = cnt - j * MOE_TM

        def ffn(row0, rows):
            lo, hi = _unpack_bf16_pairs(xbuf[slot, pl.ds(row0, rows)])
            valid = lax.broadcasted_iota(jnp.int32, lo.shape, 0) < left - row0
            x_lo = jnp.where(valid, lo, 0.0).astype(BF16)
            x_hi = jnp.where(valid, hi, 0.0).astype(BF16)

            def proj(c0):
                cols = slice(c0, c0 + MOE_FF_CHUNK)
                return (jnp.dot(x_lo, wgu_b[:half, cols], preferred_element_type=F32)
                        + jnp.dot(x_hi, wgu_b[half:, cols], preferred_element_type=F32)
                        + bgu_ref[:, cols])

            y = jnp.zeros((rows, d), F32)
            for f in range(d_ff // MOE_FF_CHUNK):
                c0 = f * MOE_FF_CHUNK
                gate = jnp.minimum(proj(c0), SWIGLU_LIMIT)
                up = jnp.clip(proj(d_ff + c0), -SWIGLU_LIMIT, SWIGLU_LIMIT)
                hh = (up + 1.0) * gate * _sigmoid(gate * SWIGLU_ALPHA)
                y = y + jnp.dot(hh.astype(BF16), wd_b[c0:c0 + MOE_FF_CHUNK, :],
                                preferred_element_type=F32)
            ybuf[slot, pl.ds(row0, rows)] = _pack_bf16_pairs(y + bdn_ref[...])

        n_sub = (jnp.minimum(left, MOE_TM) + MOE_SUB - 1) // MOE_SUB

        @pl.when(n_sub == MOE_TM // MOE_SUB)
        def _():
            ffn(0, MOE_TM)

        @pl.when(n_sub < MOE_TM // MOE_SUB)
        def _():
            def sub_step(i, c):
                ffn(pl.multiple_of(i * MOE_SUB, MOE_SUB), MOE_SUB)
                return c
            lax.fori_loop(0, n_sub, sub_step, 0)

        out_copy(tile0 + j, slot).start()
        return carry

    lax.fori_loop(0, n_t, tile_step, 0)

    g1 = g0 + n_t
    done_ref[0] = g1
    nxt = jnp.minimum(e + 1, n_e - 1)

    @pl.when(jnp.logical_and(e + 1 < n_e, cnt_ref[nxt] > 0))
    def _():
        in_copy(tile0_ref[nxt], g1 % 2).start()

    @pl.when(jnp.logical_and(e == n_e - 1, g1 >= 2))
    def _():
        out_copy(0, g1 % 2).wait()

    @pl.when(jnp.logical_and(e == n_e - 1, g1 >= 1))
    def _():
        out_copy(0, (g1 - 1) % 2).wait()


def _moe(xs, tile0, cnt, wgu, bgu, wd, bdn):
    p_rows, half = xs.shape
    n_e, d, gu = wgu.shape
    d_ff = wd.shape[1]
    w_blk = lambda e, tile0, cnt: (e, 0, 0)
    grid_spec = pltpu.PrefetchScalarGridSpec(
        num_scalar_prefetch=2,
        grid=(n_e,),
        in_specs=[pl.BlockSpec(memory_space=pl.ANY),
                  pl.BlockSpec((None, d, gu), w_blk),
                  pl.BlockSpec((None, 1, gu), w_blk),
                  pl.BlockSpec((None, d_ff, d), w_blk),
                  pl.BlockSpec((None, 1, d), w_blk)],
        out_specs=pl.BlockSpec(memory_space=pl.ANY),
        scratch_shapes=[pltpu.VMEM((d, gu), BF16), pltpu.VMEM((d_ff, d), BF16),
                        pltpu.VMEM((2, MOE_TM, half), jnp.uint32),
                        pltpu.VMEM((2, MOE_TM, half), jnp.uint32),
                        pltpu.SemaphoreType.DMA((2,)), pltpu.SemaphoreType.DMA((2,)),
                        pltpu.SMEM((1,), jnp.int32)])
    return pl.pallas_call(
        _moe_kernel,
        grid_spec=grid_spec,
        out_shape=jax.ShapeDtypeStruct((p_rows, half), jnp.uint32),
        compiler_params=pltpu.CompilerParams(dimension_semantics=("arbitrary",),
                                             vmem_limit_bytes=VMEM_LIMIT),
        name="moe_experts",
    )(tile0, cnt, xs, wgu, bgu, wd, bdn)


def _combine_kernel(h_ref, yg_ref, prob_ref, gf_ref, out_ref):
    tm = h_ref.shape[0]
    half = h_ref.shape[1] // 2
    p_rows = jnp.concatenate([prob_ref[...], jnp.zeros((8 - TOP_K, tm), F32)], axis=0)
    ph = p_rows.astype(BF16)
    pm, pl_ = _split(p_rows - ph.astype(F32))
    eye = (lax.broadcasted_iota(jnp.int32, (tm, tm), 0)
           == lax.broadcasted_iota(jnp.int32, (tm, tm), 1)).astype(BF16)
    nt = (((1,), (1,)), ((), ()))
    prob = sum(lax.dot_general(eye, piece, nt, preferred_element_type=F32)
               for piece in (ph, pm, pl_))
    acc_lo = jnp.zeros((tm, half), F32)
    acc_hi = jnp.zeros((tm, half), F32)
    for k in range(TOP_K):
        lo, hi = _unpack_bf16_pairs(yg_ref[k])
        pk = prob[:, k:k + 1]
        acc_lo = acc_lo + pk * lo
        acc_hi = acc_hi + pk * hi
    z = h_ref[...] + jnp.concatenate([acc_lo, acc_hi], axis=1)
    out_ref[...] = _rms(z, gf_ref[...])


def _combine(h, yg, row0, prob, gf, tm):
    n, d = h.shape
    tok = lambda w: pl.BlockSpec((tm, w), lambda i: (i, 0))
    t0 = row0 // tm
    return pl.pallas_call(
        _combine_kernel,
        grid=(n // tm,),
        in_specs=[tok(d), pl.BlockSpec((TOP_K, tm, d // 2), lambda i: (0, i + t0, 0)),
                  pl.BlockSpec((TOP_K, tm), lambda i: (0, i)), _full_spec(gf.shape)],
        out_specs=tok(d),
        out_shape=jax.ShapeDtypeStruct((n, d), F32),
        compiler_params=pltpu.CompilerParams(dimension_semantics=("parallel",),
                                             vmem_limit_bytes=VMEM_LIMIT),
        name="moe_combine",
    )(h, yg, prob, gf)


SC_CORES = 2
SC_SUBCORES = 16
SC_WORKERS = SC_CORES * SC_SUBCORES
SC_MAX_INDEX = 128


def _sc_chunk(rows_per_worker, limit=SC_MAX_INDEX):
    for c in range(limit, 7, -8):
        if rows_per_worker % c == 0:
            return c
    raise ValueError(f"no 8-aligned chunk divides {rows_per_worker} rows")


def _sc_mesh():
    return plsc.VectorSubcoreMesh(core_axis_name="c", subcore_axis_name="s")


def _sc_worker():
    return lax.axis_index("s") * SC_CORES + lax.axis_index("c")


def _scatter_rows(xs_in, pos_t, p_rows):
    w = xs_in[0].shape[1]
    dtype = xs_in[0].dtype
    n_k, n = pos_t.shape
    pos_flat = pos_t.reshape(n_k * n)
    parts, row0 = [], 0
    for x in xs_in:
        per = x.shape[0] // SC_WORKERS
        parts.append((row0, per, _sc_chunk(per, SC_MAX_INDEX // 2)))
        row0 += x.shape[0]
    max_chunk = max(c for _, _, c in parts)
    max_per = max(p for _, p, _ in parts)

    def body(*refs):
        x_hbms = refs[:len(xs_in)]
        pos_hbm, out_hbm, idx_v, rows_v, sem_l, sem_s = refs[len(xs_in):]
        for x_hbm, (tok0, per, chunk) in zip(x_hbms, parts):
            n_chunks = per // chunk
            base = pl.multiple_of(_sc_worker() * per, 8)
            for k in range(n_k):
                src = pl.multiple_of(k * n + tok0 + base, 8)
                pltpu.sync_copy(pos_hbm.at[pl.ds(src, per)], idx_v.at[pl.ds(k * per, per)])

            def load(c, slot):
                src = x_hbm.at[pl.ds(pl.multiple_of(base + c * chunk, 8), chunk)]
                return pltpu.make_async_copy(src, rows_v.at[slot, pl.ds(0, chunk)], sem_l.at[slot])

            def scatter(c, k, slot):
                rows = idx_v.at[pl.ds(pl.multiple_of(k * per + c * chunk, 8), chunk)]
                return pltpu.make_async_copy(rows_v.at[slot, pl.ds(0, chunk)], out_hbm.at[rows],
                                             sem_s.at[slot])

            load(0, 0).start()

            @pl.loop(0, n_chunks)
            def _(c):
                slot = c % 2
                load(c, slot).wait()

                @pl.when(c + 1 < n_chunks)
                def _():
                    @pl.when(c >= 1)
                    def _():
                        for k in range(n_k):
                            scatter(c - 1, k, 1 - slot).wait()
                    load(c + 1, 1 - slot).start()

                for k in range(n_k):
                    scatter(c, k, slot).start()

            if n_chunks >= 2:
                for k in range(n_k):
                    scatter(n_chunks - 2, k, n_chunks % 2).wait()
            for k in range(n_k):
                scatter(n_chunks - 1, k, (n_chunks - 1) % 2).wait()

    return pl.kernel(
        body, out_type=jax.ShapeDtypeStruct((p_rows, w), dtype), mesh=_sc_mesh(),
        scratch_types=[pltpu.VMEM((n_k * max_per,), jnp.int32),
                       pltpu.VMEM((2, max_chunk, w), dtype),
                       pltpu.SemaphoreType.DMA((2,)), pltpu.SemaphoreType.DMA((2,))],
        name="sc_scatter_rows")(*xs_in, pos_flat)


def _gather_rows(table, idx):
    n = idx.shape[0]
    w = table.shape[1]
    per = n // SC_WORKERS
    chunk = _sc_chunk(per)

    n_chunks = per // chunk

    def body(table_hbm, idx_hbm, out_hbm, idx_v, rows_v, sem_g, sem_w):
        base = pl.multiple_of(_sc_worker() * per, 8)
        pltpu.sync_copy(idx_hbm.at[pl.ds(base, per)], idx_v)

        def gather(c, slot):
            rows = idx_v.at[pl.ds(pl.multiple_of(c * chunk, 8), chunk)]
            return pltpu.make_async_copy(table_hbm.at[rows], rows_v.at[slot], sem_g.at[slot])

        def write(c, slot):
            dst = out_hbm.at[pl.ds(pl.multiple_of(base + c * chunk, 8), chunk)]
            return pltpu.make_async_copy(rows_v.at[slot], dst, sem_w.at[slot])

        gather(0, 0).start()

        @pl.loop(0, n_chunks)
        def _(c):
            slot = c % 2
            gather(c, slot).wait()

            @pl.when(c + 1 < n_chunks)
            def _():
                @pl.when(c >= 1)
                def _():
                    write(c - 1, 1 - slot).wait()
                gather(c + 1, 1 - slot).start()

            write(c, slot).start()

        if n_chunks >= 2:
            write(n_chunks - 2, n_chunks % 2).wait()
        write(n_chunks - 1, (n_chunks - 1) % 2).wait()

    return pl.kernel(
        body, out_type=jax.ShapeDtypeStruct((n, w), table.dtype), mesh=_sc_mesh(),
        scratch_types=[pltpu.VMEM((per,), jnp.int32), pltpu.VMEM((2, chunk, w), table.dtype),
                       pltpu.SemaphoreType.DMA((2,)), pltpu.SemaphoreType.DMA((2,))],
        name="sc_gather_rows")(table, idx)


def _mix_matrix(ws, block, seq_rows):
    causal = jnp.tril(jnp.ones((CHUNK, CHUNK), dtype=bool))
    w = jnp.where(causal[None], ws, 0.0)[:, :seq_rows, :seq_rows]
    eye = jnp.eye(block // seq_rows, dtype=ws.dtype)
    return jnp.einsum("ab,gts->gatbs", eye, w).reshape(ws.shape[0], block, block).astype(BF16)


def _stream(x, shift_in, wkv_in, p, tm, seq_rows, scan_chunk, scan_seqs):
    nb, t, d = x.shape
    n = nb * t
    x2d = x.reshape(n, d)
    rows = CHUNK if seq_rows is None else seq_rows
    wmix = _mix_matrix(p["w_spatial"], max(rows, LANES), rows)
    pos = jnp.arange(tm) % rows
    d_a = p["wu"].shape[1]
    bias_full = jnp.repeat(p["b_spatial"].T[pos], d_a // G_A, axis=1)
    apart, vn = _branch_a(x2d, tm, p["norm1_g"], p["wu"], p["wv"], p["wga"], p["vnorm_g"],
                          p["vnorm_b"], wmix, bias_full, p["w_a_out"])
    if seq_rows is None:
        xb, ext = x, jnp.zeros((1, p["wcur"].shape[1]), F32)
    else:
        xb = x2d.reshape(n // tm, tm, d)
        ext = jnp.repeat(shift_in, seq_rows, axis=0).reshape(n // tm, tm, -1)
    outs = _branch_b(xb, tm, seq_rows, ext, p["norm1_g"], p["wcur"], p["wgb"], p["mu_shift"],
                     p["w0"], p["w2"], p["a0"], p["a2"], p["g2"], p["k_k"], p["k_a"], p["r_k"],
                     p["bd"])
    r, k2, v, kk, kka, lw, g, bonus, sgb, cur = outs
    d_b = r.shape[-1]
    if seq_rows is None:
        shift_out = cur[:, -1, :]
        y, s_out = _scan(r, k2, v, kk, kka, lw, wkv_in, scan_chunk, scan_seqs)
        y2d = y.reshape(n, d_b)
    else:
        shift_out = cur.reshape(nb, t, -1)[:, -1, :]
        y2d, s_lanes = _scan_steps(*[a.reshape(n, d_b) for a in (r, k2, v, kk, kka, lw)],
                                   jnp.transpose(wkv_in, (1, 2, 3, 0)), t)
        s_out = jnp.transpose(s_lanes, (3, 0, 1, 2))
    flat = lambda a: a.reshape(n, a.shape[-1])
    routed = _merge(y2d, flat(g), flat(bonus), flat(sgb), apart, x2d, tm, p["lnx_g"],
                    p["lnx_b"], p["bd"], p["w_b_out"], p["w_out"], p["norm2_g"],
                    p["router_w"], p["router_b"])
    return routed, vn, shift_out, s_out


def kernel(x_prompt, x_sample, state_shift, state_wkv, norm1_g, w_in, mu_shift, vnorm_g, vnorm_b, w_spatial, b_spatial, w_a_out, w0, w2, a0, a2, g2, k_k, k_a, r_k, lnx_g, lnx_b, w_b_out, w_out, norm2_g, router_w, router_b, exp_w_gu, exp_b_gu, exp_w_down, exp_b_down, normf_g):
    depth = w_in.shape[0]
    assert depth == 1, "the final norm is fused into the single layer's MoE call"
    d_model = x_prompt.shape[-1]
    d_b = w0.shape[-1]
    shift_w = mu_shift.shape[-1]
    d_a = vnorm_g.shape[-1]
    bp, tp, _ = x_prompt.shape
    bs, ts, _ = x_sample.shape
    head_id = jnp.arange(2 * LANES) // HEAD_B
    bd = (head_id[:, None] == head_id[None, :]).astype(BF16)
    row = lambda a: a.reshape(1, -1)

    hp, hs = x_prompt, x_sample
    vrows, shifts_p, wkvs_p, shifts_s, wkvs_s = [], [], [], [], []
    for l in range(depth):
        wi = w_in[l].astype(BF16)
        o = shift_w
        p = dict(
            norm1_g=row(norm1_g[l]), wcur=wi[:, :o], wu=wi[:, o:o + d_a],
            wv=wi[:, o + d_a:o + 2 * d_a], wga=wi[:, o + 2 * d_a:o + 2 * d_a + d_model],
            wgb=wi[:, o + 2 * d_a + d_model:], mu_shift=row(mu_shift[l]),
            vnorm_g=row(vnorm_g[l]), vnorm_b=row(vnorm_b[l]), w_spatial=w_spatial[l],
            b_spatial=b_spatial[l], w_a_out=w_a_out[l].astype(BF16), w0=row(w0[l]),
            w2=_stack_rhs3(w2[l]), a0=row(a0[l]), a2=_stack_rhs3(a2[l]),
            g2=jnp.concatenate([g2[l].astype(BF16)] * 2, axis=0),
            k_k=row(k_k[l]), k_a=row(k_a[l]),
            r_k=row(r_k[l]), lnx_g=row(lnx_g[l]), lnx_b=row(lnx_b[l]),
            w_b_out=w_b_out[l].astype(BF16), w_out=w_out[l].astype(BF16),
            norm2_g=row(norm2_g[l]), router_w=jnp.concatenate(_split(router_w[l].T), axis=0),
            router_b=router_b[l].reshape(-1, 1), bd=bd)
        routed_p, _, sh_p, s_p = _stream(hp, None, None, p, 512, None, 64, 4)
        routed_s, vn_s, sh_s, s_s = _stream(hs, state_shift[l], state_wkv[l], p, 512, ts, 8, 8)
        h_p, xp_p, topi_p, prob_p, rank_p, cnt_p = routed_p
        h_s, xp_s, topi_s, prob_s, rank_s, cnt_s = routed_s
        n_p, n_s = h_p.shape[0], h_s.shape[0]
        n_e = router_w.shape[-1]
        n_tiles = (n_p + n_s) * TOP_K // MOE_TM + n_e
        pos_t, tile0, cnt = _route(topi_p, rank_p, cnt_p, topi_s, rank_s, cnt_s)
        xs = _scatter_rows([xp_p, xp_s], pos_t, n_tiles * MOE_TM)
        ys = _moe(xs, tile0, cnt, exp_w_gu[l], exp_b_gu[l][:, None, :],
                  exp_w_down[l], exp_b_down[l][:, None, :])
        yg = _gather_rows(ys, pos_t.reshape(-1)).reshape(TOP_K, n_p + n_s, d_model // 2)
        gf = row(normf_g)
        hp = _combine(h_p, yg, 0, prob_p, gf, 512).reshape(bp, tp, d_model)
        hs = _combine(h_s, yg, n_p, prob_s, gf, 512).reshape(bs, ts, d_model)
        vrows.append(vn_s.reshape(bs, ts, d_a))
        shifts_p.append(sh_p)
        wkvs_p.append(s_p)
        shifts_s.append(sh_s)
        wkvs_s.append(s_s)
    return (hp, hs, jnp.stack(shifts_p), jnp.stack(wkvs_p), jnp.stack(shifts_s),
            jnp.stack(wkvs_s), jnp.stack(vrows))
```

```python
import functools

import jax
import jax.numpy as jnp
from jax import lax
from jax.experimental import pallas as pl
from jax.experimental.pallas import tpu as pltpu
from jax.experimental.pallas import tpu_sc as plsc

F32 = jnp.float32
BF16 = jnp.bfloat16

CHUNK = 128
G_A = 8
HEAD_B = 64
R_W, R_A, R_G = 64, 64, 128
TOP_K = 4
SWIGLU_LIMIT = 7.0
SWIGLU_ALPHA = 1.702
EPS = 1e-5
GN_EPS = HEAD_B * 1e-5

LANES = 128
PAIR = 2 * HEAD_B
VMEM_LIMIT = 56 * 1024 * 1024


def _dot(a, b):
    return jnp.dot(a.astype(BF16), b.astype(BF16), preferred_element_type=F32)


def _split(x):
    hi = x.astype(BF16)
    lo = (x - hi.astype(F32)).astype(BF16)
    return hi, lo


def _stack_rhs3(b):
    bh, bl = _split(b)
    return jnp.concatenate([bh, bh, bl], axis=0)


def _dot3_stacked(a, b_stacked):
    ah, al = _split(a)
    return jnp.dot(jnp.concatenate([ah, al, ah], axis=1), b_stacked, preferred_element_type=F32)


def _dot2_stacked(a, b_stacked):
    ah, al = _split(a)
    return jnp.dot(jnp.concatenate([ah, al], axis=1), b_stacked, preferred_element_type=F32)


def _head_sums(x, ones_bd):
    w = ones_bd.shape[0]
    return jnp.concatenate([jnp.dot(x[:, c:c + w].astype(BF16), ones_bd,
                                    preferred_element_type=F32)
                            for c in range(0, x.shape[1], w)], axis=1)


def _dot_exact_lhs(a_bf16, b):
    bh, bl = _split(b)
    bm = b - bh.astype(F32) - bl.astype(F32)
    return (jnp.dot(a_bf16, bh, preferred_element_type=F32)
            + jnp.dot(a_bf16, bl, preferred_element_type=F32)
            + jnp.dot(a_bf16, bm.astype(BF16), preferred_element_type=F32))


def _rms(x, g):
    return x * lax.rsqrt(jnp.mean(x * x, axis=-1, keepdims=True) + EPS) * g


def _sigmoid(x):
    return 1.0 / (1.0 + jnp.exp(-x))


def _softplus(z):
    return jnp.maximum(z, 0.0) + jnp.log(1.0 + jnp.exp(-jnp.abs(z)))


def _full_spec(shape):
    nd = len(shape)
    return pl.BlockSpec(shape, lambda *_: (0,) * nd, pipeline_mode=pl.Buffered(1))


def _branch_a_kernel(x_ref, g1_ref, wu_ref, wv_ref, wga_ref, vg_ref, vb_ref, wmix_ref,
                     bias_ref, wao_ref, apart_ref, vn_ref):
    xn = _rms(x_ref[...], g1_ref[...]).astype(BF16)
    v = jnp.dot(xn, wv_ref[...], preferred_element_type=F32)
    mu = jnp.mean(v, axis=-1, keepdims=True)
    d = v - mu
    var = jnp.mean(d * d, axis=-1, keepdims=True)
    vn = d * lax.rsqrt(var + EPS) * vg_ref[...] + vb_ref[...]
    vn_ref[...] = vn
    vnb = vn.astype(BF16)
    c_a = vnb.shape[1] // G_A
    rb = wmix_ref.shape[1]
    mixed = jnp.concatenate(
        [jnp.concatenate(
            [jnp.dot(wmix_ref[g], vnb[r0:r0 + rb, g * c_a:(g + 1) * c_a],
                     preferred_element_type=F32) for g in range(G_A)], axis=1)
         for r0 in range(0, vnb.shape[0], rb)], axis=0) + bias_ref[...]
    u = jnp.dot(xn, wu_ref[...], preferred_element_type=F32)
    ya = (u * mixed).astype(BF16)
    ga = jnp.dot(xn, wga_ref[...], preferred_element_type=F32)
    apart_ref[...] = _sigmoid(ga) * jnp.dot(ya, wao_ref[...], preferred_element_type=F32)


def _branch_a(x2d, tm, g1, wu, wv, wga, vg, vb, wmix, bias_full, wao):
    n, d = x2d.shape
    d_a = wu.shape[1]
    tok = lambda w: pl.BlockSpec((tm, w), lambda i: (i, 0))
    return pl.pallas_call(
        _branch_a_kernel,
        grid=(n // tm,),
        in_specs=[tok(d), _full_spec(g1.shape), _full_spec(wu.shape), _full_spec(wv.shape),
                  _full_spec(wga.shape), _full_spec(vg.shape), _full_spec(vb.shape),
                  _full_spec(wmix.shape), _full_spec(bias_full.shape), _full_spec(wao.shape)],
        out_specs=[tok(d), tok(d_a)],
        out_shape=[jax.ShapeDtypeStruct((n, d), F32), jax.ShapeDtypeStruct((n, d_a), F32)],
        compiler_params=pltpu.CompilerParams(dimension_semantics=("parallel",),
                                             vmem_limit_bytes=VMEM_LIMIT),
        name="branch_a",
    )(x2d, g1, wu, wv, wga, vg, vb, wmix, bias_full, wao)


def _branch_b_kernel(x_ref, g1_ref, wcur_ref, wgb_ref, mu_ref, ext_ref, w0_ref, w2_ref,
                     a0_ref, a2_ref, g2_ref, kk_ref, ka_ref, rk_ref, bd_ref,
                     r_out, k_out, v_out, kk_out, kka_out, lw_out, g_out, bonus_out,
                     sgb_out, cur_out, carry_scr, *, seq_rows, d_b):
    tm = x_ref.shape[0]
    xn = _rms(x_ref[...], g1_ref[...]).astype(BF16)
    cur = jnp.dot(xn, wcur_ref[...], preferred_element_type=F32)
    rolled = pltpu.roll(cur, 1, axis=0)
    row = lax.broadcasted_iota(jnp.int32, (tm, 1), 0)
    if seq_rows is None:
        first_tile = pl.program_id(1) == 0
        carry = jnp.where(first_tile, 0.0, carry_scr[...])
        prev = jnp.where(row == 0, carry, rolled)
        carry_scr[...] = cur[tm - 1:tm, :]
        cur_out[...] = cur[tm - 8:tm, :]
    else:
        prev = jnp.where(row % seq_rows == 0, ext_ref[...], rolled)
        cur_out[...] = cur
    xs = cur + (prev - cur) * mu_ref[...]
    r = xs[:, 0:d_b]
    k = xs[:, d_b:2 * d_b]
    v = xs[:, 2 * d_b:3 * d_b]
    o = 3 * d_b
    xw = xs[:, o:o + R_W]
    xa = xs[:, o + R_W:o + R_W + R_A]
    xg = xs[:, o + R_W + R_A:o + R_W + R_A + R_G]
    wl = w0_ref[...] + _dot3_stacked(jnp.tanh(xw), w2_ref[...])
    lw = -jnp.exp(-_softplus(-wl) - 0.5)
    a = _sigmoid(a0_ref[...] + _dot3_stacked(xa, a2_ref[...]))
    g = _dot2_stacked(_sigmoid(xg), g2_ref[...])
    bd = bd_ref[...]
    kkr = k * kk_ref[...]
    n2 = _head_sums(kkr * kkr, bd)
    kk = kkr / jnp.maximum(jnp.sqrt(n2), 1e-12)
    k2 = k * (1.0 + (a - 1.0) * ka_ref[...])
    bonus = _head_sums(r * k2 * rk_ref[...], bd) * v
    r_out[...] = r
    k_out[...] = k2
    v_out[...] = v
    kk_out[...] = kk
    kka_out[...] = kk * a
    lw_out[...] = lw
    g_out[...] = g
    bonus_out[...] = bonus
    sgb_out[...] = _sigmoid(jnp.dot(xn, wgb_ref[...], preferred_element_type=F32))


def _branch_b(x3d, tm, seq_rows, ext, g1, wcur, wgb, mu, w0, w2, a0, a2, g2, k_k, k_a, r_k, bd):
    nb, t, d = x3d.shape
    d_b = w0.shape[1]
    shift_w = wcur.shape[1]
    nt = t // tm
    tok = lambda w: pl.BlockSpec((None, tm, w), lambda b, i: (b, i, 0))
    cur_rows = 8 if seq_rows is None else tm
    outs = [jax.ShapeDtypeStruct((nb, t, d_b), F32)] * 8 + [
        jax.ShapeDtypeStruct((nb, t, d), F32),
        jax.ShapeDtypeStruct((nb, nt * cur_rows, shift_w), F32)]
    out_specs = [tok(d_b)] * 8 + [tok(d), pl.BlockSpec((None, cur_rows, shift_w),
                                                       lambda b, i: (b, i, 0))]
    weights = (g1, wcur, wgb, mu)
    small = (w0, w2, a0, a2, g2, k_k, k_a, r_k, bd)
    return pl.pallas_call(
        functools.partial(_branch_b_kernel, seq_rows=seq_rows, d_b=d_b),
        grid=(nb, nt),
        in_specs=[tok(d)] + [_full_spec(w.shape) for w in weights]
        + [tok(shift_w) if seq_rows is not None else _full_spec(ext.shape)]
        + [_full_spec(w.shape) for w in small],
        out_specs=out_specs,
        out_shape=outs,
        scratch_shapes=[pltpu.VMEM((1, shift_w), F32)],
        compiler_params=pltpu.CompilerParams(dimension_semantics=("parallel", "arbitrary"),
                                             vmem_limit_bytes=VMEM_LIMIT),
        name="branch_b",
    )(x3d, *weights, ext, *small)


def _scan_kernel(r_ref, k_ref, v_ref, kk_ref, kka_ref, lw_ref, y_ref, sout_ref, s_scr, *, chunk):
    L = chunk
    W2 = 2 * L
    c = pl.program_id(1)
    n_seq, n_pairs = s_scr.shape[:2]

    @pl.when(c == 0)
    def _():
        s_scr[...] = jnp.zeros_like(s_scr)

    iota = lambda shape, dim: lax.broadcasted_iota(jnp.int32, shape, dim)
    stack_mask = (iota((W2, 1), 0) >= L) == (iota((1, PAIR), 1) >= HEAD_B)
    bd_mask = (iota((W2, 1), 0) >= L) == (iota((1, W2), 1) >= L)
    assert L & (L - 1) == 0, "chunk length must be a power of two"
    rw = iota((L, W2), 0)
    cw = iota((L, W2), 1) & (L - 1)
    strict_w = cw < rw
    eye_w = (cw == rw).astype(F32)
    incl_w2 = (iota((L, 2 * W2), 1) & (L - 1)) <= iota((L, 2 * W2), 0)
    tri = (iota((L, L), 1) <= iota((L, L), 0)).astype(BF16)
    di = iota((PAIR, PAIR), 0)
    dj = iota((PAIR, PAIR), 1)
    diag = di == dj
    head_diag = (di >= HEAD_B) == (dj >= HEAD_B)

    def stack(x):
        return jnp.where(stack_mask, jnp.concatenate([x, x], axis=0), 0.0)

    def bd(xw):
        return jnp.where(bd_mask, jnp.concatenate([xw, xw], axis=0), 0.0)

    n_double = max(L.bit_length() - 2, 0)
    prep = []
    for s in range(n_seq):
        lw_all = lw_ref[s]
        cum_all = _dot_exact_lhs(tri, lw_all)
        for p in range(n_pairs):
            sl = slice(p * PAIR, (p + 1) * PAIR)
            cum = cum_all[:, sl]
            cum_last = cum[L - 1:L, :]
            e_neg = jnp.exp(-cum)
            e_rel = jnp.exp(cum_last - cum)
            kk = kk_ref[s, :, sl]
            kka = kka_ref[s, :, sl]
            kx = k_ref[s, :, sl]
            prep.append(dict(
                s=s, p=p, sl=sl, ab=-kk * jnp.exp(cum - lw_all[:, sl]),
                rb=r_ref[s, :, sl] * jnp.exp(cum), bb=kka * e_neg, kb=kx * e_neg,
                bt=kka * e_rel, kt=kx * e_rel, v=v_ref[s, :, sl], decay=jnp.exp(cum_last)))
    a_w = [lax.dot_general(
        jnp.concatenate([q["ab"], q["rb"]], axis=0).astype(BF16),
        jnp.concatenate([stack(q["bb"]), stack(q["kb"])], axis=0).astype(BF16),
        (((1,), (1,)), ((), ())), preferred_element_type=F32) for q in prep]
    nw = [jnp.where(strict_w, a[:L, :W2], 0.0) for a in a_w]
    tw = [eye_w + n for n in nw]
    nbd = [bd(n) for n in nw]
    for _ in range(n_double):
        nw = [_dot(n, b) for n, b in zip(nw, nbd)]
        nbd = [bd(n) for n in nw]
        tw = [t + _dot(t, b) for t, b in zip(tw, nbd)]
    akv = [_dot(jnp.where(strict_w, a[:L, W2:], 0.0), stack(q["v"])) for a, q in zip(a_w, prep)]
    tx = [_dot(t, jnp.concatenate([stack(q["ab"]), stack(kv)], axis=1))
          for t, q, kv in zip(tw, prep, akv)]
    ry = [_dot(jnp.where(incl_w2, a[L:], 0.0),
               jnp.concatenate(
                   [jnp.concatenate([stack(x[:, :PAIR]), stack(x[:, PAIR:])], axis=1),
                    jnp.concatenate([jnp.zeros((W2, PAIR), F32), stack(q["v"])], axis=1)], axis=0))
          for a, x, q in zip(a_w, tx, prep)]
    for q, x, y in zip(prep, tx, ry):
        st = s_scr[q["s"], q["p"]]
        us = _dot(jnp.concatenate([x[:, :PAIR], q["rb"] + y[:, :PAIR]], axis=0), st)
        y_ref[q["s"], :, q["sl"]] = us[L:] + y[:, PAIR:]
        u = us[:L] + x[:, PAIR:]
        lhs = jnp.concatenate([jnp.where(diag, q["decay"], 0.0),
                               jnp.concatenate([q["bt"], q["kt"]], axis=0).T], axis=1)
        g = _dot(lhs, jnp.concatenate([st, u, q["v"]], axis=0))
        s_scr[q["s"], q["p"]] = jnp.where(head_diag, g, 0.0)

    @pl.when(c == pl.num_programs(1) - 1)
    def _():
        for s in range(n_seq):
            for p in range(n_pairs):
                nat = s_scr[s, p].T
                sout_ref[s, 2 * p] = nat[:HEAD_B, :HEAD_B]
                sout_ref[s, 2 * p + 1] = nat[HEAD_B:, HEAD_B:]


SCAN_CHUNK = 64
SCAN_SEQS = 4


def _scan(r, k, v, kk, kka, lw):
    nb, t, d_b = r.shape
    n_pairs = d_b // PAIR
    n_heads = d_b // HEAD_B
    tok = pl.BlockSpec((SCAN_SEQS, SCAN_CHUNK, d_b), lambda b, c: (b, c, 0))
    st = pl.BlockSpec((SCAN_SEQS, n_heads, HEAD_B, HEAD_B), lambda b, c: (b, 0, 0, 0))
    return pl.pallas_call(
        functools.partial(_scan_kernel, chunk=SCAN_CHUNK),
        grid=(nb // SCAN_SEQS, t // SCAN_CHUNK),
        in_specs=[tok] * 6,
        out_specs=[tok, st],
        out_shape=[jax.ShapeDtypeStruct((nb, t, d_b), F32),
                   jax.ShapeDtypeStruct((nb, n_heads, HEAD_B, HEAD_B), F32)],
        scratch_shapes=[pltpu.VMEM((SCAN_SEQS, n_pairs, PAIR, PAIR), F32)],
        compiler_params=pltpu.CompilerParams(dimension_semantics=("parallel", "arbitrary"),
                                             vmem_limit_bytes=VMEM_LIMIT),
        name="rwkv_scan",
    )(r, k, v, kk, kka, lw)


STEP_ROWS = 8


def _steps_kernel(r_ref, k_ref, v_ref, kk_ref, kka_ref, lw_ref, s0_ref, y_ref, s_ref,
                  vec_scr, yt_scr, *, n_steps):
    n_seq = s_ref.shape[-1]
    s_ref[...] = s0_ref[...]
    for t in range(n_steps):
        rows = pl.ds(t, n_seq, stride=n_steps)
        vecs = {}
        for slot, (name, ref) in enumerate((("r", r_ref), ("k", k_ref), ("v", v_ref),
                                            ("kk", kk_ref), ("kka", kka_ref), ("lw", lw_ref))):
            vecs[name] = ref[rows, :].T
            vec_scr[slot] = vecs[name]
        for q in range(PAIR // HEAD_B):
            hs = slice(q * HEAD_B, (q + 1) * HEAD_B)
            r_q, k_q, kka_q = vecs["r"][hs], vecs["k"][hs], vecs["kka"][hs]
            nkk_q = -vecs["kk"][hs]
            w_q = jnp.exp(vecs["lw"][hs])

            def rows_step(i0, carry):
                for u in range(STEP_ROWS):
                    i = i0 * STEP_ROWS + u
                    s_row = s_ref[q, i]
                    sa = jnp.sum(s_row * nkk_q, axis=0, keepdims=True)
                    v_i = vec_scr[2, pl.ds(q * HEAD_B + i, 1), :]
                    s_new = s_row * w_q + sa * kka_q + v_i * k_q
                    s_ref[q, i] = s_new
                    yt_scr[pl.ds(q * HEAD_B + i, 1), :] = jnp.sum(s_new * r_q, axis=0,
                                                                  keepdims=True)
                return carry

            lax.fori_loop(0, HEAD_B // STEP_ROWS, rows_step, 0)
        y_ref[rows, :] = yt_scr[...].T


def _scan_steps(r, k, v, kk, kka, lw, s0, n_steps):
    n, d_b = r.shape
    n_heads, _, _, n_seq = s0.shape
    tok = pl.BlockSpec((n, PAIR), lambda p: (0, p))
    st = pl.BlockSpec((PAIR // HEAD_B, HEAD_B, HEAD_B, n_seq), lambda p: (p, 0, 0, 0))
    return pl.pallas_call(
        functools.partial(_steps_kernel, n_steps=n_steps),
        grid=(d_b // PAIR,),
        in_specs=[tok] * 6 + [st],
        out_specs=[tok, st],
        out_shape=[jax.ShapeDtypeStruct((n, d_b), F32), jax.ShapeDtypeStruct(s0.shape, F32)],
        scratch_shapes=[pltpu.VMEM((6, PAIR, n_seq), F32), pltpu.VMEM((PAIR, n_seq), F32)],
        compiler_params=pltpu.CompilerParams(dimension_semantics=("parallel",),
                                             vmem_limit_bytes=VMEM_LIMIT),
        name="rwkv_steps",
    )(r, k, v, kk, kka, lw, s0)


def _pack_bf16_pairs(x):
    w = x.shape[1] // 2
    bits = lambda v: lax.bitcast_convert_type(v.astype(BF16).astype(F32), jnp.uint32)
    return (bits(x[:, :w]) >> 16) | (bits(x[:, w:]) & jnp.uint32(0xFFFF0000))


def _unpack_bf16_pairs(u):
    lo = lax.bitcast_convert_type(u << 16, F32)
    hi = lax.bitcast_convert_type(u & jnp.uint32(0xFFFF0000), F32)
    return lo, hi


def _merge_kernel(y_ref, g_ref, bonus_ref, sgb_ref, apart_ref, x_ref, lg_ref, lb_ref, bd_ref,
                  wbo_ref, wo_ref, g2_ref, rwt_ref, rb_ref,
                  h_out, xp_out, topi_out, prob_out, rank_out, cnt_out, cnt_scr):
    step = pl.program_id(0)

    @pl.when(step == 0)
    def _():
        cnt_scr[...] = jnp.zeros_like(cnt_scr)

    bd = bd_ref[...]
    y = y_ref[...]
    inv_n = 1.0 / HEAD_B
    mu = _head_sums(y, bd) * inv_n
    d = y - mu
    var = _head_sums(d * d, bd) * inv_n
    yn = d * lax.rsqrt(var + GN_EPS) * lg_ref[...] + lb_ref[...]
    yb = ((yn + bonus_ref[...]) * g_ref[...]).astype(BF16)
    merged = apart_ref[...] + sgb_ref[...] * jnp.dot(yb, wbo_ref[...], preferred_element_type=F32)
    h = x_ref[...] + jnp.dot(merged.astype(BF16), wo_ref[...], preferred_element_type=F32)
    h_out[...] = h
    xn2 = _rms(h, g2_ref[...])
    xp_out[...] = _pack_bf16_pairs(xn2)
    n_e = rb_ref.shape[0]
    tm = xn2.shape[0]
    xh, xl = _split(xn2)
    rwt = rwt_ref[...]
    nt = (((1,), (1,)), ((), ()))
    t = lax.dot_general(rwt, xh, nt, preferred_element_type=F32)
    logits = (t[:n_e] + t[n_e:] + lax.dot_general(rwt[:n_e], xl, nt, preferred_element_type=F32)
              + rb_ref[...])
    idx = lax.broadcasted_iota(jnp.int32, logits.shape, 0).astype(F32)
    work = logits
    tops, hots, sels = [], [], []
    for _ in range(TOP_K):
        m = jnp.max(work, axis=0, keepdims=True)
        sel = jnp.min(jnp.where(work == m, idx, float(n_e)), axis=0, keepdims=True)
        hot = idx == sel
        tops.append(m)
        hots.append(hot)
        sels.append(sel)
        work = jnp.where(hot, -jnp.inf, work)
    es = [jnp.exp(t - tops[0]) for t in tops]
    denom = es[0] + es[1] + es[2] + es[3]
    topi_out[...] = jnp.concatenate(sels, axis=0).astype(jnp.int32)
    prob_out[...] = jnp.concatenate([e / denom for e in es], axis=0)
    hot_any = jnp.zeros_like(logits)
    for hot in hots:
        hot_any = hot_any + hot.astype(F32)
    ri = lax.broadcasted_iota(jnp.int32, (tm, tm), 0)
    ci = lax.broadcasted_iota(jnp.int32, (tm, tm), 1)
    before = _dot(hot_any, (ri < ci).astype(BF16)) + cnt_scr[...]
    rank_out[...] = jnp.concatenate(
        [jnp.sum(jnp.where(hot, before, 0.0), axis=0, keepdims=True) for hot in hots],
        axis=0).astype(jnp.int32)
    total = cnt_scr[...] + jnp.sum(hot_any, axis=1, keepdims=True)
    cnt_scr[...] = total
    cnt_out[...] = total.astype(jnp.int32)


def _merge(y, g, bonus, sgb, apart, x2d, tm, lnx_g, lnx_b, bd, wbo, wo, g2n, rw, rb):
    n, d = x2d.shape
    d_b = y.shape[1]
    n_e = rb.shape[0]
    tok = lambda w: pl.BlockSpec((tm, w), lambda i: (i, 0))
    per_k = pl.BlockSpec((TOP_K, tm), lambda i: (0, i))
    weights = (lnx_g, lnx_b, bd, wbo, wo, g2n, rw, rb)
    return pl.pallas_call(
        _merge_kernel,
        grid=(n // tm,),
        in_specs=[tok(d_b), tok(d_b), tok(d_b), tok(d), tok(d), tok(d)]
        + [_full_spec(w.shape) for w in weights],
        out_specs=[tok(d), tok(d // 2), per_k, per_k, per_k,
                   pl.BlockSpec((n_e, 1), lambda i: (0, 0))],
        out_shape=[jax.ShapeDtypeStruct((n, d), F32),
                   jax.ShapeDtypeStruct((n, d // 2), jnp.uint32),
                   jax.ShapeDtypeStruct((TOP_K, n), jnp.int32),
                   jax.ShapeDtypeStruct((TOP_K, n), F32),
                   jax.ShapeDtypeStruct((TOP_K, n), jnp.int32),
                   jax.ShapeDtypeStruct((n_e, 1), jnp.int32)],
        scratch_shapes=[pltpu.VMEM((n_e, 1), F32)],
        compiler_params=pltpu.CompilerParams(dimension_semantics=("arbitrary",),
                                             vmem_limit_bytes=VMEM_LIMIT),
        name="merge_router",
    )(y, g, bonus, sgb, apart, x2d, *weights)


MOE_TM = 1024
MOE_SUB = 128
MOE_FF_CHUNK = 512
CAST_ROWS = 256


def _route(topi_p, rank_p, cnt_p, topi_s, rank_s, cnt_s):
    n_e = cnt_p.size
    cnt_p, cnt_s = cnt_p.reshape(n_e), cnt_s.reshape(n_e)
    cnt = cnt_p + cnt_s
    padded = (cnt + MOE_TM - 1) // MOE_TM * MOE_TM
    ends = jnp.cumsum(padded)
    base = ends - padded
    lookup = lambda table, idx: jnp.sum(
        jnp.where(idx[..., None] == jnp.arange(n_e), table, 0), axis=-1)
    pos_t = jnp.concatenate([lookup(base, topi_p) + rank_p,
                             lookup(base + cnt_p, topi_s) + rank_s], axis=1)
    return (pos_t.astype(jnp.int32), (base // MOE_TM).astype(jnp.int32),
            cnt.astype(jnp.int32))


def _moe_kernel(tile0_ref, cnt_ref, xs_hbm, wgu_ref, bgu_ref, wd_ref, bdn_ref, ys_hbm,
                wgu_b, wd_b, xbuf, ybuf, sem_in, sem_out, done_ref):
    e = pl.program_id(0)
    n_e = pl.num_programs(0)

    @pl.when(e == 0)
    def _():
        done_ref[0] = 0

    g0 = done_ref[0]
    tile0 = tile0_ref[e]
    cnt = cnt_ref[e]
    n_t = (cnt + MOE_TM - 1) // MOE_TM
    d, gu = wgu_ref.shape
    d_ff = wd_ref.shape[0]
    half = d // 2

    def in_copy(tile, slot):
        return pltpu.make_async_copy(xs_hbm.at[pl.ds(tile * MOE_TM, MOE_TM)], xbuf.at[slot],
                                     sem_in.at[slot])

    def out_copy(tile, slot):
        return pltpu.make_async_copy(ybuf.at[slot], ys_hbm.at[pl.ds(tile * MOE_TM, MOE_TM)],
                                     sem_out.at[slot])

    @pl.when(jnp.logical_and(e == 0, n_t > 0))
    def _():
        in_copy(tile0, 0).start()

    @pl.when(n_t > 0)
    def _():
        for r0 in range(0, d, CAST_ROWS):
            wgu_b[r0:r0 + CAST_ROWS, :] = wgu_ref[r0:r0 + CAST_ROWS, :].astype(BF16)
        for r0 in range(0, d_ff, CAST_ROWS):
            wd_b[r0:r0 + CAST_ROWS, :] = wd_ref[r0:r0 + CAST_ROWS, :].astype(BF16)

    def tile_step(j, carry):
        slot = (g0 + j) % 2
        in_copy(tile0 + j, slot).wait()

        @pl.when(j + 1 < n_t)
        def _():
            in_copy(tile0 + j + 1, 1 - slot).start()

        @pl.when(g0 + j >= 2)
        def _():
            out_copy(0, slot).wait()

        left = cnt - j * MOE_TM

        def ffn(row0, rows):
            lo, hi = _unpack_bf16_pairs(xbuf[slot, pl.ds(row0, rows)])
            valid = lax.broadcasted_iota(jnp.int32, lo.shape, 0) < left - row0
            x_lo = jnp.where(valid, lo, 0.0).astype(BF16)
            x_hi = jnp.where(valid, hi, 0.0).astype(BF16)

            def proj(c0):
                cols = slice(c0, c0 + MOE_FF_CHUNK)
                return (jnp.dot(x_lo, wgu_b[:half, cols], preferred_element_type=F32)
                        + jnp.dot(x_hi, wgu_b[half:, cols], preferred_element_type=F32)
                        + bgu_ref[:, cols])

            y = jnp.zeros((rows, d), F32)
            for f in range(d_ff // MOE_FF_CHUNK):
                c0 = f * MOE_FF_CHUNK
                gate = jnp.minimum(proj(c0), SWIGLU_LIMIT)
                up = jnp.clip(proj(d_ff + c0), -SWIGLU_LIMIT, SWIGLU_LIMIT)
                hh = (up + 1.0) * gate * _sigmoid(gate * SWIGLU_ALPHA)
                y = y + jnp.dot(hh.astype(BF16), wd_b[c0:c0 + MOE_FF_CHUNK, :],
                                preferred_element_type=F32)
            ybuf[slot, pl.ds(row0, rows)] = _pack_bf16_pairs(y + bdn_ref[...])

        n_sub = (jnp.minimum(left, MOE_TM) + MOE_SUB - 1) // MOE_SUB

        @pl.when(n_sub == MOE_TM // MOE_SUB)
        def _():
            ffn(0, MOE_TM)

        @pl.when(n_sub < MOE_TM // MOE_SUB)
        def _():
            def sub_step(i, c):
                ffn(pl.multiple_of(i * 2 * MOE_SUB, 2 * MOE_SUB), 2 * MOE_SUB)
                return c
            lax.fori_loop(0, n_sub // 2, sub_step, 0)

            @pl.when(n_sub % 2 == 1)
            def _():
                ffn(pl.multiple_of((n_sub // 2) * 2 * MOE_SUB, 2 * MOE_SUB), MOE_SUB)

        out_copy(tile0 + j, slot).start()
        return carry

    lax.fori_loop(0, n_t, tile_step, 0)

    g1 = g0 + n_t
    done_ref[0] = g1
    nxt = jnp.minimum(e + 1, n_e - 1)

    @pl.when(jnp.logical_and(e + 1 < n_e, cnt_ref[nxt] > 0))
    def _():
        in_copy(tile0_ref[nxt], g1 % 2).start()

    @pl.when(jnp.logical_and(e == n_e - 1, g1 >= 2))
    def _():
        out_copy(0, g1 % 2).wait()

    @pl.when(jnp.logical_and(e == n_e - 1, g1 >= 1))
    def _():
        out_copy(0, (g1 - 1) % 2).wait()


def _moe(xs, tile0, cnt, wgu, bgu, wd, bdn):
    p_rows, half = xs.shape
    n_e, d, gu = wgu.shape
    d_ff = wd.shape[1]
    w_blk = lambda e, tile0, cnt: (e, 0, 0)
    grid_spec = pltpu.PrefetchScalarGridSpec(
        num_scalar_prefetch=2,
        grid=(n_e,),
        in_specs=[pl.BlockSpec(memory_space=pl.ANY),
                  pl.BlockSpec((None, d, gu), w_blk),
                  pl.BlockSpec((None, 1, gu), w_blk),
                  pl.BlockSpec((None, d_ff, d), w_blk),
                  pl.BlockSpec((None, 1, d), w_blk)],
        out_specs=pl.BlockSpec(memory_space=pl.ANY),
        scratch_shapes=[pltpu.VMEM((d, gu), BF16), pltpu.VMEM((d_ff, d), BF16),
                        pltpu.VMEM((2, MOE_TM, half), jnp.uint32),
                        pltpu.VMEM((2, MOE_TM, half), jnp.uint32),
                        pltpu.SemaphoreType.DMA((2,)), pltpu.SemaphoreType.DMA((2,)),
                        pltpu.SMEM((1,), jnp.int32)])
    return pl.pallas_call(
        _moe_kernel,
        grid_spec=grid_spec,
        out_shape=jax.ShapeDtypeStruct((p_rows, half), jnp.uint32),
        compiler_params=pltpu.CompilerParams(dimension_semantics=("arbitrary",),
                                             vmem_limit_bytes=VMEM_LIMIT),
        name="moe_experts",
    )(tile0, cnt, xs, wgu, bgu, wd, bdn)


def _combine_kernel(h_ref, yg_ref, prob_ref, gf_ref, out_ref):
    tm = h_ref.shape[0]
    half = h_ref.shape[1] // 2
    p_rows = jnp.concatenate([prob_ref[...], jnp.zeros((8 - TOP_K, tm), F32)], axis=0)
    ph = p_rows.astype(BF16)
    pm, pl_ = _split(p_rows - ph.astype(F32))
    eye = (lax.broadcasted_iota(jnp.int32, (tm, tm), 0)
           == lax.broadcasted_iota(jnp.int32, (tm, tm), 1)).astype(BF16)
    nt = (((1,), (1,)), ((), ()))
    prob = sum(lax.dot_general(eye, piece, nt, preferred_element_type=F32)
               for piece in (ph, pm, pl_))
    acc_lo = jnp.zeros((tm, half), F32)
    acc_hi = jnp.zeros((tm, half), F32)
    for k in range(TOP_K):
        lo, hi = _unpack_bf16_pairs(yg_ref[k])
        pk = prob[:, k:k + 1]
        acc_lo = acc_lo + pk * lo
        acc_hi = acc_hi + pk * hi
    z = h_ref[...] + jnp.concatenate([acc_lo, acc_hi], axis=1)
    out_ref[...] = _rms(z, gf_ref[...])


def _combine(h, yg, row0, prob, gf, tm):
    n, d = h.shape
    tok = lambda w: pl.BlockSpec((tm, w), lambda i: (i, 0))
    t0 = row0 // tm
    return pl.pallas_call(
        _combine_kernel,
        grid=(n // tm,),
        in_specs=[tok(d), pl.BlockSpec((TOP_K, tm, d // 2), lambda i: (0, i + t0, 0)),
                  pl.BlockSpec((TOP_K, tm), lambda i: (0, i)), _full_spec(gf.shape)],
        out_specs=tok(d),
        out_shape=jax.ShapeDtypeStruct((n, d), F32),
        compiler_params=pltpu.CompilerParams(dimension_semantics=("parallel",),
                                             vmem_limit_bytes=VMEM_LIMIT),
        name="moe_combine",
    )(h, yg, prob, gf)


SC_CORES = 2
SC_SUBCORES = 16
SC_WORKERS = SC_CORES * SC_SUBCORES
SC_MAX_INDEX = 128


def _sc_chunk(rows_per_worker, limit=SC_MAX_INDEX):
    for c in range(limit, 7, -8):
        if rows_per_worker % c == 0:
            return c
    raise ValueError(f"no 8-aligned chunk divides {rows_per_worker} rows")


def _sc_mesh():
    return plsc.VectorSubcoreMesh(core_axis_name="c", subcore_axis_name="s")


def _sc_worker():
    return lax.axis_index("s") * SC_CORES + lax.axis_index("c")


def _scatter_rows(xs_in, pos_t, p_rows):
    w = xs_in[0].shape[1]
    dtype = xs_in[0].dtype
    n_k, n = pos_t.shape
    pos_flat = pos_t.reshape(n_k * n)
    parts, row0 = [], 0
    for x in xs_in:
        per = x.shape[0] // SC_WORKERS
        parts.append((row0, per, _sc_chunk(per, SC_MAX_INDEX // 2)))
        row0 += x.shape[0]
    max_chunk = max(c for _, _, c in parts)
    max_per = max(p for _, p, _ in parts)

    def body(*refs):
        x_hbms = refs[:len(xs_in)]
        pos_hbm, out_hbm, idx_v, rows_v, sem_l, sem_s = refs[len(xs_in):]
        for x_hbm, (tok0, per, chunk) in zip(x_hbms, parts):
            n_chunks = per // chunk
            base = pl.multiple_of(_sc_worker() * per, 8)
            for k in range(n_k):
                src = pl.multiple_of(k * n + tok0 + base, 8)
                pltpu.sync_copy(pos_hbm.at[pl.ds(src, per)], idx_v.at[pl.ds(k * per, per)])

            def load(c, slot):
                src = x_hbm.at[pl.ds(pl.multiple_of(base + c * chunk, 8), chunk)]
                return pltpu.make_async_copy(src, rows_v.at[slot, pl.ds(0, chunk)], sem_l.at[slot])

            def scatter(c, k, slot):
                rows = idx_v.at[pl.ds(pl.multiple_of(k * per + c * chunk, 8), chunk)]
                return pltpu.make_async_copy(rows_v.at[slot, pl.ds(0, chunk)], out_hbm.at[rows],
                                             sem_s.at[slot])

            load(0, 0).start()

            @pl.loop(0, n_chunks)
            def _(c):
                slot = c % 2
                load(c, slot).wait()

                @pl.when(c + 1 < n_chunks)
                def _():
                    @pl.when(c >= 1)
                    def _():
                        for k in range(n_k):
                            scatter(c - 1, k, 1 - slot).wait()
                    load(c + 1, 1 - slot).start()

                for k in range(n_k):
                    scatter(c, k, slot).start()

            if n_chunks >= 2:
                for k in range(n_k):
                    scatter(n_chunks - 2, k, n_chunks % 2).wait()
            for k in range(n_k):
                scatter(n_chunks - 1, k, (n_chunks - 1) % 2).wait()

    return pl.kernel(
        body, out_type=jax.ShapeDtypeStruct((p_rows, w), dtype), mesh=_sc_mesh(),
        scratch_types=[pltpu.VMEM((n_k * max_per,), jnp.int32),
                       pltpu.VMEM((2, max_chunk, w), dtype),
                       pltpu.SemaphoreType.DMA((2,)), pltpu.SemaphoreType.DMA((2,))],
        name="sc_scatter_rows")(*xs_in, pos_flat)


def _gather_rows(table, idx):
    n = idx.shape[0]
    w = table.shape[1]
    per = n // SC_WORKERS
    chunk = _sc_chunk(per)

    n_chunks = per // chunk

    def body(table_hbm, idx_hbm, out_hbm, idx_v, rows_v, sem_g, sem_w):
        base = pl.multiple_of(_sc_worker() * per, 8)
        pltpu.sync_copy(idx_hbm.at[pl.ds(base, per)], idx_v)

        def gather(c, slot):
            rows = idx_v.at[pl.ds(pl.multiple_of(c * chunk, 8), chunk)]
            return pltpu.make_async_copy(table_hbm.at[rows], rows_v.at[slot], sem_g.at[slot])

        def write(c, slot):
            dst = out_hbm.at[pl.ds(pl.multiple_of(base + c * chunk, 8), chunk)]
            return pltpu.make_async_copy(rows_v.at[slot], dst, sem_w.at[slot])

        gather(0, 0).start()

        @pl.loop(0, n_chunks)
        def _(c):
            slot = c % 2
            gather(c, slot).wait()

            @pl.when(c + 1 < n_chunks)
            def _():
                @pl.when(c >= 1)
                def _():
                    write(c - 1, 1 - slot).wait()
                gather(c + 1, 1 - slot).start()

            write(c, slot).start()

        if n_chunks >= 2:
            write(n_chunks - 2, n_chunks % 2).wait()
        write(n_chunks - 1, (n_chunks - 1) % 2).wait()

    return pl.kernel(
        body, out_type=jax.ShapeDtypeStruct((n, w), table.dtype), mesh=_sc_mesh(),
        scratch_types=[pltpu.VMEM((per,), jnp.int32), pltpu.VMEM((2, chunk, w), table.dtype),
                       pltpu.SemaphoreType.DMA((2,)), pltpu.SemaphoreType.DMA((2,))],
        name="sc_gather_rows")(table, idx)


def _mix_matrix(ws, block, seq_rows):
    causal = jnp.tril(jnp.ones((CHUNK, CHUNK), dtype=bool))
    w = jnp.where(causal[None], ws, 0.0)[:, :seq_rows, :seq_rows]
    eye = jnp.eye(block // seq_rows, dtype=ws.dtype)
    return jnp.einsum("ab,gts->gatbs", eye, w).reshape(ws.shape[0], block, block).astype(BF16)


TOKEN_TILE = 512


def _stream(x, shift_in, wkv_in, p, seq_rows):
    nb, t, d = x.shape
    n = nb * t
    tm = TOKEN_TILE
    x2d = x.reshape(n, d)
    rows = CHUNK if seq_rows is None else seq_rows
    wmix = _mix_matrix(p["w_spatial"], max(rows, LANES), rows)
    pos = jnp.arange(tm) % rows
    d_a = p["wu"].shape[1]
    bias_full = jnp.repeat(p["b_spatial"].T[pos], d_a // G_A, axis=1)
    apart, vn = _branch_a(x2d, tm, p["norm1_g"], p["wu"], p["wv"], p["wga"], p["vnorm_g"],
                          p["vnorm_b"], wmix, bias_full, p["w_a_out"])
    if seq_rows is None:
        xb, ext = x, jnp.zeros((1, p["wcur"].shape[1]), F32)
    else:
        xb = x2d.reshape(n // tm, tm, d)
        ext = jnp.repeat(shift_in, seq_rows, axis=0).reshape(n // tm, tm, -1)
    outs = _branch_b(xb, tm, seq_rows, ext, p["norm1_g"], p["wcur"], p["wgb"], p["mu_shift"],
                     p["w0"], p["w2"], p["a0"], p["a2"], p["g2"], p["k_k"], p["k_a"], p["r_k"],
                     p["bd"])
    r, k2, v, kk, kka, lw, g, bonus, sgb, cur = outs
    d_b = r.shape[-1]
    if seq_rows is None:
        shift_out = cur[:, -1, :]
        y, s_out = _scan(r, k2, v, kk, kka, lw)
        y2d = y.reshape(n, d_b)
    else:
        shift_out = cur.reshape(nb, t, -1)[:, -1, :]
        y2d, s_lanes = _scan_steps(*[a.reshape(n, d_b) for a in (r, k2, v, kk, kka, lw)],
                                   jnp.transpose(wkv_in, (1, 2, 3, 0)), t)
        s_out = jnp.transpose(s_lanes, (3, 0, 1, 2))
    flat = lambda a: a.reshape(n, a.shape[-1])
    routed = _merge(y2d, flat(g), flat(bonus), flat(sgb), apart, x2d, tm, p["lnx_g"],
                    p["lnx_b"], p["bd"], p["w_b_out"], p["w_out"], p["norm2_g"],
                    p["router_w"], p["router_b"])
    return routed, vn, shift_out, s_out


def kernel(x_prompt, x_sample, state_shift, state_wkv, norm1_g, w_in, mu_shift, vnorm_g, vnorm_b, w_spatial, b_spatial, w_a_out, w0, w2, a0, a2, g2, k_k, k_a, r_k, lnx_g, lnx_b, w_b_out, w_out, norm2_g, router_w, router_b, exp_w_gu, exp_b_gu, exp_w_down, exp_b_down, normf_g):
    assert w_in.shape[0] == 1, "one layer: the final norm is fused into its MoE combine"
    l = 0
    d_model = x_prompt.shape[-1]
    shift_w = mu_shift.shape[-1]
    d_a = vnorm_g.shape[-1]
    bp, tp, _ = x_prompt.shape
    bs, ts, _ = x_sample.shape
    head_id = jnp.arange(2 * LANES) // HEAD_B
    bd = (head_id[:, None] == head_id[None, :]).astype(BF16)
    row = lambda a: a.reshape(1, -1)

    wi = w_in[l].astype(BF16)
    o = shift_w
    p = dict(
        norm1_g=row(norm1_g[l]), wcur=wi[:, :o], wu=wi[:, o:o + d_a],
        wv=wi[:, o + d_a:o + 2 * d_a], wga=wi[:, o + 2 * d_a:o + 2 * d_a + d_model],
        wgb=wi[:, o + 2 * d_a + d_model:], mu_shift=row(mu_shift[l]),
        vnorm_g=row(vnorm_g[l]), vnorm_b=row(vnorm_b[l]), w_spatial=w_spatial[l],
        b_spatial=b_spatial[l], w_a_out=w_a_out[l].astype(BF16), w0=row(w0[l]),
        w2=_stack_rhs3(w2[l]), a0=row(a0[l]), a2=_stack_rhs3(a2[l]),
        g2=jnp.concatenate([g2[l].astype(BF16)] * 2, axis=0),
        k_k=row(k_k[l]), k_a=row(k_a[l]),
        r_k=row(r_k[l]), lnx_g=row(lnx_g[l]), lnx_b=row(lnx_b[l]),
        w_b_out=w_b_out[l].astype(BF16), w_out=w_out[l].astype(BF16),
        norm2_g=row(norm2_g[l]), router_w=jnp.concatenate(_split(router_w[l].T), axis=0),
        router_b=router_b[l].reshape(-1, 1), bd=bd)

    routed_p, _, sh_p, s_p = _stream(x_prompt, None, None, p, None)
    routed_s, vn_s, sh_s, s_s = _stream(x_sample, state_shift[l], state_wkv[l], p, ts)
    h_p, xp_p, topi_p, prob_p, rank_p, cnt_p = routed_p
    h_s, xp_s, topi_s, prob_s, rank_s, cnt_s = routed_s

    n_p, n_s = h_p.shape[0], h_s.shape[0]
    n_e = router_w.shape[-1]
    n_tiles = (n_p + n_s) * TOP_K // MOE_TM + n_e
    pos_t, tile0, cnt = _route(topi_p, rank_p, cnt_p, topi_s, rank_s, cnt_s)
    xs = _scatter_rows([xp_p, xp_s], pos_t, n_tiles * MOE_TM)
    ys = _moe(xs, tile0, cnt, exp_w_gu[l], exp_b_gu[l][:, None, :],
              exp_w_down[l], exp_b_down[l][:, None, :])
    yg = _gather_rows(ys, pos_t.reshape(-1)).reshape(TOP_K, n_p + n_s, d_model // 2)
    gf = row(normf_g)
    y_p = _combine(h_p, yg, 0, prob_p, gf, TOKEN_TILE).reshape(bp, tp, d_model)
    y_s = _combine(h_s, yg, n_p, prob_s, gf, TOKEN_TILE).reshape(bs, ts, d_model)
    return (y_p, y_s, sh_p[None], s_p[None], sh_s[None], s_s[None],
            vn_s.reshape(1, bs, ts, d_a))
```

```python
import functools

import jax
import jax.numpy as jnp
from jax import lax
from jax.experimental import pallas as pl
from jax.experimental.pallas import tpu as pltpu
from jax.experimental.pallas import tpu_sc as plsc

F32 = jnp.float32
BF16 = jnp.bfloat16

CHUNK = 128
G_A = 8
HEAD_B = 64
R_W, R_A, R_G = 64, 64, 128
TOP_K = 4
SWIGLU_LIMIT = 7.0
SWIGLU_ALPHA = 1.702
EPS = 1e-5
GN_EPS = HEAD_B * 1e-5

LANES = 128
PAIR = 2 * HEAD_B
VMEM_LIMIT = 56 * 1024 * 1024


def _dot(a, b):
    return jnp.dot(a.astype(BF16), b.astype(BF16), preferred_element_type=F32)


def _split(x):
    hi = x.astype(BF16)
    lo = (x - hi.astype(F32)).astype(BF16)
    return hi, lo


def _stack_rhs3(b):
    bh, bl = _split(b)
    return jnp.concatenate([bh, bh, bl], axis=0)


def _dot3_stacked(a, b_stacked):
    ah, al = _split(a)
    return jnp.dot(jnp.concatenate([ah, al, ah], axis=1), b_stacked, preferred_element_type=F32)


def _dot2_stacked(a, b_stacked):
    ah, al = _split(a)
    return jnp.dot(jnp.concatenate([ah, al], axis=1), b_stacked, preferred_element_type=F32)


def _head_sums(x, ones_bd):
    w = ones_bd.shape[0]
    return jnp.concatenate([jnp.dot(x[:, c:c + w].astype(BF16), ones_bd,
                                    preferred_element_type=F32)
                            for c in range(0, x.shape[1], w)], axis=1)


def _dot_exact_lhs(a_bf16, b):
    bh, bl = _split(b)
    bm = b - bh.astype(F32) - bl.astype(F32)
    return (jnp.dot(a_bf16, bh, preferred_element_type=F32)
            + jnp.dot(a_bf16, bl, preferred_element_type=F32)
            + jnp.dot(a_bf16, bm.astype(BF16), preferred_element_type=F32))


def _rms(x, g):
    return x * lax.rsqrt(jnp.mean(x * x, axis=-1, keepdims=True) + EPS) * g


def _sigmoid(x):
    return 1.0 / (1.0 + jnp.exp(-x))


def _softplus(z):
    return jnp.maximum(z, 0.0) + jnp.log(1.0 + jnp.exp(-jnp.abs(z)))


def _full_spec(shape):
    nd = len(shape)
    return pl.BlockSpec(shape, lambda *_: (0,) * nd, pipeline_mode=pl.Buffered(1))


def _branch_a_kernel(x_ref, g1_ref, wu_ref, wv_ref, wga_ref, vg_ref, vb_ref, wmix_ref,
                     bias_ref, wao_ref, apart_ref, vn_ref):
    xn = _rms(x_ref[...], g1_ref[...]).astype(BF16)
    v = jnp.dot(xn, wv_ref[...], preferred_element_type=F32)
    mu = jnp.mean(v, axis=-1, keepdims=True)
    d = v - mu
    var = jnp.mean(d * d, axis=-1, keepdims=True)
    vn = d * lax.rsqrt(var + EPS) * vg_ref[...] + vb_ref[...]
    vn_ref[...] = vn
    vnb = vn.astype(BF16)
    c_a = vnb.shape[1] // G_A
    rb = wmix_ref.shape[1]
    mixed = jnp.concatenate(
        [jnp.concatenate(
            [jnp.dot(wmix_ref[g], vnb[r0:r0 + rb, g * c_a:(g + 1) * c_a],
                     preferred_element_type=F32) for g in range(G_A)], axis=1)
         for r0 in range(0, vnb.shape[0], rb)], axis=0) + bias_ref[...]
    u = jnp.dot(xn, wu_ref[...], preferred_element_type=F32)
    ya = (u * mixed).astype(BF16)
    ga = jnp.dot(xn, wga_ref[...], preferred_element_type=F32)
    apart_ref[...] = _sigmoid(ga) * jnp.dot(ya, wao_ref[...], preferred_element_type=F32)


def _branch_a(x2d, tm, g1, wu, wv, wga, vg, vb, wmix, bias_full, wao):
    n, d = x2d.shape
    d_a = wu.shape[1]
    tok = lambda w: pl.BlockSpec((tm, w), lambda i: (i, 0))
    return pl.pallas_call(
        _branch_a_kernel,
        grid=(n // tm,),
        in_specs=[tok(d), _full_spec(g1.shape), _full_spec(wu.shape), _full_spec(wv.shape),
                  _full_spec(wga.shape), _full_spec(vg.shape), _full_spec(vb.shape),
                  _full_spec(wmix.shape), _full_spec(bias_full.shape), _full_spec(wao.shape)],
        out_specs=[tok(d), tok(d_a)],
        out_shape=[jax.ShapeDtypeStruct((n, d), F32), jax.ShapeDtypeStruct((n, d_a), F32)],
        compiler_params=pltpu.CompilerParams(dimension_semantics=("parallel",),
                                             vmem_limit_bytes=VMEM_LIMIT),
        name="branch_a",
    )(x2d, g1, wu, wv, wga, vg, vb, wmix, bias_full, wao)


def _branch_b_kernel(x_ref, g1_ref, wcur_ref, wgb_ref, mu_ref, ext_ref, w0_ref, w2_ref,
                     a0_ref, a2_ref, g2_ref, kk_ref, ka_ref, rk_ref, bd_ref,
                     r_out, k_out, v_out, kk_out, kka_out, lw_out, g_out, bonus_out,
                     sgb_out, cur_out, carry_scr, *, seq_rows, d_b):
    tm = x_ref.shape[0]
    xn = _rms(x_ref[...], g1_ref[...]).astype(BF16)
    cur = jnp.dot(xn, wcur_ref[...], preferred_element_type=F32)
    rolled = pltpu.roll(cur, 1, axis=0)
    row = lax.broadcasted_iota(jnp.int32, (tm, 1), 0)
    if seq_rows is None:
        first_tile = pl.program_id(1) == 0
        carry = jnp.where(first_tile, 0.0, carry_scr[...])
        prev = jnp.where(row == 0, carry, rolled)
        carry_scr[...] = cur[tm - 1:tm, :]
        cur_out[...] = cur[tm - 8:tm, :]
    else:
        prev = jnp.where(row % seq_rows == 0, ext_ref[...], rolled)
        cur_out[...] = cur
    xs = cur + (prev - cur) * mu_ref[...]
    r = xs[:, 0:d_b]
    k = xs[:, d_b:2 * d_b]
    v = xs[:, 2 * d_b:3 * d_b]
    o = 3 * d_b
    xw = xs[:, o:o + R_W]
    xa = xs[:, o + R_W:o + R_W + R_A]
    xg = xs[:, o + R_W + R_A:o + R_W + R_A + R_G]
    wl = w0_ref[...] + _dot3_stacked(jnp.tanh(xw), w2_ref[...])
    lw = -jnp.exp(-_softplus(-wl) - 0.5)
    a = _sigmoid(a0_ref[...] + _dot3_stacked(xa, a2_ref[...]))
    g = _dot2_stacked(_sigmoid(xg), g2_ref[...])
    bd = bd_ref[...]
    kkr = k * kk_ref[...]
    n2 = _head_sums(kkr * kkr, bd)
    kk = kkr * lax.rsqrt(jnp.maximum(n2, 1e-24))
    k2 = k * (1.0 + (a - 1.0) * ka_ref[...])
    bonus = _head_sums(r * k2 * rk_ref[...], bd) * v
    r_out[...] = r
    k_out[...] = k2
    v_out[...] = v
    kk_out[...] = kk
    kka_out[...] = kk * a
    lw_out[...] = lw
    g_out[...] = g
    bonus_out[...] = bonus
    sgb_out[...] = _sigmoid(jnp.dot(xn, wgb_ref[...], preferred_element_type=F32))


def _branch_b(x3d, tm, seq_rows, ext, g1, wcur, wgb, mu, w0, w2, a0, a2, g2, k_k, k_a, r_k, bd):
    nb, t, d = x3d.shape
    d_b = w0.shape[1]
    shift_w = wcur.shape[1]
    nt = t // tm
    tok = lambda w: pl.BlockSpec((None, tm, w), lambda b, i: (b, i, 0))
    cur_rows = 8 if seq_rows is None else tm
    outs = [jax.ShapeDtypeStruct((nb, t, d_b), F32)] * 8 + [
        jax.ShapeDtypeStruct((nb, t, d), F32),
        jax.ShapeDtypeStruct((nb, nt * cur_rows, shift_w), F32)]
    out_specs = [tok(d_b)] * 8 + [tok(d), pl.BlockSpec((None, cur_rows, shift_w),
                                                       lambda b, i: (b, i, 0))]
    weights = (g1, wcur, wgb, mu)
    small = (w0, w2, a0, a2, g2, k_k, k_a, r_k, bd)
    return pl.pallas_call(
        functools.partial(_branch_b_kernel, seq_rows=seq_rows, d_b=d_b),
        grid=(nb, nt),
        in_specs=[tok(d)] + [_full_spec(w.shape) for w in weights]
        + [tok(shift_w) if seq_rows is not None else _full_spec(ext.shape)]
        + [_full_spec(w.shape) for w in small],
        out_specs=out_specs,
        out_shape=outs,
        scratch_shapes=[pltpu.VMEM((1, shift_w), F32)],
        compiler_params=pltpu.CompilerParams(dimension_semantics=("parallel", "arbitrary"),
                                             vmem_limit_bytes=VMEM_LIMIT),
        name="branch_b",
    )(x3d, *weights, ext, *small)


def _scan_kernel(r_ref, k_ref, v_ref, kk_ref, kka_ref, lw_ref, y_ref, sout_ref, s_scr, *, chunk):
    L = chunk
    W2 = 2 * L
    c = pl.program_id(1)
    n_seq, n_pairs = s_scr.shape[:2]

    @pl.when(c == 0)
    def _():
        s_scr[...] = jnp.zeros_like(s_scr)

    iota = lambda shape, dim: lax.broadcasted_iota(jnp.int32, shape, dim)
    stack_mask = (iota((W2, 1), 0) >= L) == (iota((1, PAIR), 1) >= HEAD_B)
    bd_mask = (iota((W2, 1), 0) >= L) == (iota((1, W2), 1) >= L)
    assert L & (L - 1) == 0, "chunk length must be a power of two"
    rw = iota((L, W2), 0)
    cw = iota((L, W2), 1) & (L - 1)
    strict_w = cw < rw
    eye_w = (cw == rw).astype(F32)
    incl_w2 = (iota((L, 2 * W2), 1) & (L - 1)) <= iota((L, 2 * W2), 0)
    tri = (iota((L, L), 1) <= iota((L, L), 0)).astype(BF16)
    di = iota((PAIR, PAIR), 0)
    dj = iota((PAIR, PAIR), 1)
    diag = di == dj
    head_diag = (di >= HEAD_B) == (dj >= HEAD_B)

    def stack(x):
        return jnp.where(stack_mask, jnp.concatenate([x, x], axis=0), 0.0)

    def bd(xw):
        return jnp.where(bd_mask, jnp.concatenate([xw, xw], axis=0), 0.0)

    n_double = max(L.bit_length() - 2, 0)
    prep = []
    for s in range(n_seq):
        lw_all = lw_ref[s]
        cum_all = _dot_exact_lhs(tri, lw_all)
        for p in range(n_pairs):
            sl = slice(p * PAIR, (p + 1) * PAIR)
            cum = cum_all[:, sl]
            cum_last = cum[L - 1:L, :]
            e_neg = jnp.exp(-cum)
            e_rel = jnp.exp(cum_last - cum)
            kk = kk_ref[s, :, sl]
            kka = kka_ref[s, :, sl]
            kx = k_ref[s, :, sl]
            prep.append(dict(
                s=s, p=p, sl=sl, ab=-kk * jnp.exp(cum - lw_all[:, sl]),
                rb=r_ref[s, :, sl] * jnp.exp(cum), bb=kka * e_neg, kb=kx * e_neg,
                bt=kka * e_rel, kt=kx * e_rel, v=v_ref[s, :, sl], decay=jnp.exp(cum_last)))
    a_w = [lax.dot_general(
        jnp.concatenate([q["ab"], q["rb"]], axis=0).astype(BF16),
        jnp.concatenate([stack(q["bb"]), stack(q["kb"])], axis=0).astype(BF16),
        (((1,), (1,)), ((), ())), preferred_element_type=F32) for q in prep]
    nw = [jnp.where(strict_w, a[:L, :W2], 0.0) for a in a_w]
    tw = [eye_w + n for n in nw]
    nbd = [bd(n) for n in nw]
    for _ in range(n_double):
        nw = [_dot(n, b) for n, b in zip(nw, nbd)]
        nbd = [bd(n) for n in nw]
        tw = [t + _dot(t, b) for t, b in zip(tw, nbd)]
    akv = [_dot(jnp.where(strict_w, a[:L, W2:], 0.0), stack(q["v"])) for a, q in zip(a_w, prep)]
    tx = [_dot(t, jnp.concatenate([stack(q["ab"]), stack(kv)], axis=1))
          for t, q, kv in zip(tw, prep, akv)]
    ry = [_dot(jnp.where(incl_w2, a[L:], 0.0),
               jnp.concatenate(
                   [jnp.concatenate([stack(x[:, :PAIR]), stack(x[:, PAIR:])], axis=1),
                    jnp.concatenate([jnp.zeros((W2, PAIR), F32), stack(q["v"])], axis=1)], axis=0))
          for a, x, q in zip(a_w, tx, prep)]
    for q, x, y in zip(prep, tx, ry):
        st = s_scr[q["s"], q["p"]]
        us = _dot(jnp.concatenate([x[:, :PAIR], q["rb"] + y[:, :PAIR]], axis=0), st)
        y_ref[q["s"], :, q["sl"]] = us[L:] + y[:, PAIR:]
        u = us[:L] + x[:, PAIR:]
        lhs = jnp.concatenate([jnp.where(diag, q["decay"], 0.0),
                               jnp.concatenate([q["bt"], q["kt"]], axis=0).T], axis=1)
        g = _dot(lhs, jnp.concatenate([st, u, q["v"]], axis=0))
        s_scr[q["s"], q["p"]] = jnp.where(head_diag, g, 0.0)

    @pl.when(c == pl.num_programs(1) - 1)
    def _():
        for s in range(n_seq):
            for p in range(n_pairs):
                nat = s_scr[s, p].T
                sout_ref[s, 2 * p] = nat[:HEAD_B, :HEAD_B]
                sout_ref[s, 2 * p + 1] = nat[HEAD_B:, HEAD_B:]


SCAN_CHUNK = 64
SCAN_SEQS = 4


def _scan(r, k, v, kk, kka, lw):
    nb, t, d_b = r.shape
    n_pairs = d_b // PAIR
    n_heads = d_b // HEAD_B
    tok = pl.BlockSpec((SCAN_SEQS, SCAN_CHUNK, d_b), lambda b, c: (b, c, 0))
    st = pl.BlockSpec((SCAN_SEQS, n_heads, HEAD_B, HEAD_B), lambda b, c: (b, 0, 0, 0))
    return pl.pallas_call(
        functools.partial(_scan_kernel, chunk=SCAN_CHUNK),
        grid=(nb // SCAN_SEQS, t // SCAN_CHUNK),
        in_specs=[tok] * 6,
        out_specs=[tok, st],
        out_shape=[jax.ShapeDtypeStruct((nb, t, d_b), F32),
                   jax.ShapeDtypeStruct((nb, n_heads, HEAD_B, HEAD_B), F32)],
        scratch_shapes=[pltpu.VMEM((SCAN_SEQS, n_pairs, PAIR, PAIR), F32)],
        compiler_params=pltpu.CompilerParams(dimension_semantics=("parallel", "arbitrary"),
                                             vmem_limit_bytes=VMEM_LIMIT),
        name="rwkv_scan",
    )(r, k, v, kk, kka, lw)


STEP_ROWS = 8


def _steps_kernel(r_ref, k_ref, v_ref, kk_ref, kka_ref, lw_ref, s0_ref, y_ref, s_ref,
                  vec_scr, yt_scr, *, n_steps):
    n_seq = s_ref.shape[-1]
    s_ref[...] = s0_ref[...]
    for t in range(n_steps):
        rows = pl.ds(t, n_seq, stride=n_steps)
        vecs = {}
        for slot, (name, ref) in enumerate((("r", r_ref), ("k", k_ref), ("v", v_ref),
                                            ("kk", kk_ref), ("kka", kka_ref), ("lw", lw_ref))):
            vecs[name] = ref[rows, :].T
            vec_scr[slot] = vecs[name]
        for q in range(PAIR // HEAD_B):
            hs = slice(q * HEAD_B, (q + 1) * HEAD_B)
            r_q, k_q, kka_q = vecs["r"][hs], vecs["k"][hs], vecs["kka"][hs]
            nkk_q = -vecs["kk"][hs]
            w_q = jnp.exp(vecs["lw"][hs])

            def rows_step(i0, carry):
                for u in range(STEP_ROWS):
                    i = i0 * STEP_ROWS + u
                    s_row = s_ref[q, i]
                    sa = jnp.sum(s_row * nkk_q, axis=0, keepdims=True)
                    v_i = vec_scr[2, pl.ds(q * HEAD_B + i, 1), :]
                    s_new = s_row * w_q + sa * kka_q + v_i * k_q
                    s_ref[q, i] = s_new
                    yt_scr[pl.ds(q * HEAD_B + i, 1), :] = jnp.sum(s_new * r_q, axis=0,
                                                                  keepdims=True)
                return carry

            lax.fori_loop(0, HEAD_B // STEP_ROWS, rows_step, 0)
        y_ref[rows, :] = yt_scr[...].T


def _scan_steps(r, k, v, kk, kka, lw, s0, n_steps):
    n, d_b = r.shape
    n_heads, _, _, n_seq = s0.shape
    tok = pl.BlockSpec((n, PAIR), lambda p: (0, p))
    st = pl.BlockSpec((PAIR // HEAD_B, HEAD_B, HEAD_B, n_seq), lambda p: (p, 0, 0, 0))
    return pl.pallas_call(
        functools.partial(_steps_kernel, n_steps=n_steps),
        grid=(d_b // PAIR,),
        in_specs=[tok] * 6 + [st],
        out_specs=[tok, st],
        out_shape=[jax.ShapeDtypeStruct((n, d_b), F32), jax.ShapeDtypeStruct(s0.shape, F32)],
        scratch_shapes=[pltpu.VMEM((6, PAIR, n_seq), F32), pltpu.VMEM((PAIR, n_seq), F32)],
        compiler_params=pltpu.CompilerParams(dimension_semantics=("parallel",),
                                             vmem_limit_bytes=VMEM_LIMIT),
        name="rwkv_steps",
    )(r, k, v, kk, kka, lw, s0)


def _pack_bf16_pairs(x):
    w = x.shape[1] // 2
    bits = lambda v: lax.bitcast_convert_type(v.astype(BF16).astype(F32), jnp.uint32)
    return (bits(x[:, :w]) >> 16) | (bits(x[:, w:]) & jnp.uint32(0xFFFF0000))


def _unpack_bf16_pairs(u):
    lo = lax.bitcast_convert_type(u << 16, F32)
    hi = lax.bitcast_convert_type(u & jnp.uint32(0xFFFF0000), F32)
    return lo, hi


def _merge_kernel(y_ref, g_ref, bonus_ref, sgb_ref, apart_ref, x_ref, lg_ref, lb_ref, bd_ref,
                  wbo_ref, wo_ref, g2_ref, rwt_ref, rb_ref,
                  h_out, xp_out, topi_out, prob_out, rank_out, cnt_out, cnt_scr):
    step = pl.program_id(0)

    @pl.when(step == 0)
    def _():
        cnt_scr[...] = jnp.zeros_like(cnt_scr)

    bd = bd_ref[...]
    y = y_ref[...]
    inv_n = 1.0 / HEAD_B
    mu = _head_sums(y, bd) * inv_n
    d = y - mu
    var = _head_sums(d * d, bd) * inv_n
    yn = d * lax.rsqrt(var + GN_EPS) * lg_ref[...] + lb_ref[...]
    yb = ((yn + bonus_ref[...]) * g_ref[...]).astype(BF16)
    merged = apart_ref[...] + sgb_ref[...] * jnp.dot(yb, wbo_ref[...], preferred_element_type=F32)
    h = x_ref[...] + jnp.dot(merged.astype(BF16), wo_ref[...], preferred_element_type=F32)
    h_out[...] = h
    xn2 = _rms(h, g2_ref[...])
    xp_out[...] = _pack_bf16_pairs(xn2)
    n_e = rb_ref.shape[0]
    tm = xn2.shape[0]
    xh, xl = _split(xn2)
    rwt = rwt_ref[...]
    nt = (((1,), (1,)), ((), ()))
    t = lax.dot_general(rwt, xh, nt, preferred_element_type=F32)
    logits = (t[:n_e] + t[n_e:] + lax.dot_general(rwt[:n_e], xl, nt, preferred_element_type=F32)
              + rb_ref[...])
    idx = lax.broadcasted_iota(jnp.int32, logits.shape, 0).astype(F32)
    work = logits
    tops, hots, sels = [], [], []
    for _ in range(TOP_K):
        m = jnp.max(work, axis=0, keepdims=True)
        sel = jnp.min(jnp.where(work == m, idx, float(n_e)), axis=0, keepdims=True)
        hot = idx == sel
        tops.append(m)
        hots.append(hot)
        sels.append(sel)
        work = jnp.where(hot, -jnp.inf, work)
    es = [jnp.exp(t - tops[0]) for t in tops]
    denom = es[0] + es[1] + es[2] + es[3]
    topi_out[...] = jnp.concatenate(sels, axis=0).astype(jnp.int32)
    prob_out[...] = jnp.concatenate([e / denom for e in es], axis=0)
    hot_any = jnp.zeros_like(logits)
    for hot in hots:
        hot_any = hot_any + hot.astype(F32)
    ri = lax.broadcasted_iota(jnp.int32, (tm, tm), 0)
    ci = lax.broadcasted_iota(jnp.int32, (tm, tm), 1)
    before = _dot(hot_any, (ri < ci).astype(BF16)) + cnt_scr[...]
    rank_out[...] = jnp.concatenate(
        [jnp.sum(jnp.where(hot, before, 0.0), axis=0, keepdims=True) for hot in hots],
        axis=0).astype(jnp.int32)
    total = cnt_scr[...] + jnp.sum(hot_any, axis=1, keepdims=True)
    cnt_scr[...] = total
    cnt_out[...] = total.astype(jnp.int32)


def _merge(y, g, bonus, sgb, apart, x2d, tm, lnx_g, lnx_b, bd, wbo, wo, g2n, rw, rb):
    n, d = x2d.shape
    d_b = y.shape[1]
    n_e = rb.shape[0]
    tok = lambda w: pl.BlockSpec((tm, w), lambda i: (i, 0))
    per_k = pl.BlockSpec((TOP_K, tm), lambda i: (0, i))
    weights = (lnx_g, lnx_b, bd, wbo, wo, g2n, rw, rb)
    return pl.pallas_call(
        _merge_kernel,
        grid=(n // tm,),
        in_specs=[tok(d_b), tok(d_b), tok(d_b), tok(d), tok(d), tok(d)]
        + [_full_spec(w.shape) for w in weights],
        out_specs=[tok(d), tok(d // 2), per_k, per_k, per_k,
                   pl.BlockSpec((n_e, 1), lambda i: (0, 0))],
        out_shape=[jax.ShapeDtypeStruct((n, d), F32),
                   jax.ShapeDtypeStruct((n, d // 2), jnp.uint32),
                   jax.ShapeDtypeStruct((TOP_K, n), jnp.int32),
                   jax.ShapeDtypeStruct((TOP_K, n), F32),
                   jax.ShapeDtypeStruct((TOP_K, n), jnp.int32),
                   jax.ShapeDtypeStruct((n_e, 1), jnp.int32)],
        scratch_shapes=[pltpu.VMEM((n_e, 1), F32)],
        compiler_params=pltpu.CompilerParams(dimension_semantics=("arbitrary",),
                                             vmem_limit_bytes=VMEM_LIMIT),
        name="merge_router",
    )(y, g, bonus, sgb, apart, x2d, *weights)


MOE_TM = 1024
MOE_SUB = 128
MOE_FF_CHUNK = 512
CAST_ROWS = 256


def _route(topi_p, rank_p, cnt_p, topi_s, rank_s, cnt_s):
    n_e = cnt_p.size
    cnt_p, cnt_s = cnt_p.reshape(n_e), cnt_s.reshape(n_e)
    cnt = cnt_p + cnt_s
    padded = (cnt + MOE_TM - 1) // MOE_TM * MOE_TM
    ends = jnp.cumsum(padded)
    base = ends - padded
    lookup = lambda table, idx: jnp.sum(
        jnp.where(idx[..., None] == jnp.arange(n_e), table, 0), axis=-1)
    pos_t = jnp.concatenate([lookup(base, topi_p) + rank_p,
                             lookup(base + cnt_p, topi_s) + rank_s], axis=1)
    return (pos_t.astype(jnp.int32), (base // MOE_TM).astype(jnp.int32),
            cnt.astype(jnp.int32))


def _moe_kernel(tile0_ref, cnt_ref, xs_hbm, wgu_ref, bgu_ref, wd_ref, bdn_ref, ys_hbm,
                wgu_b, wd_b, xbuf, ybuf, sem_in, sem_out, done_ref):
    e = pl.program_id(0)
    n_e = pl.num_programs(0)

    @pl.when(e == 0)
    def _():
        done_ref[0] = 0

    g0 = done_ref[0]
    tile0 = tile0_ref[e]
    cnt = cnt_ref[e]
    n_t = (cnt + MOE_TM - 1) // MOE_TM
    d, gu = wgu_ref.shape
    d_ff = wd_ref.shape[0]
    half = d // 2

    def in_copy(tile, slot):
        return pltpu.make_async_copy(xs_hbm.at[pl.ds(tile * MOE_TM, MOE_TM)], xbuf.at[slot],
                                     sem_in.at[slot])

    def out_copy(tile, slot):
        return pltpu.make_async_copy(ybuf.at[slot], ys_hbm.at[pl.ds(tile * MOE_TM, MOE_TM)],
                                     sem_out.at[slot])

    @pl.when(jnp.logical_and(e == 0, n_t > 0))
    def _():
        in_copy(tile0, 0).start()

    @pl.when(n_t > 0)
    def _():
        for r0 in range(0, d, CAST_ROWS):
            wgu_b[r0:r0 + CAST_ROWS, :] = wgu_ref[r0:r0 + CAST_ROWS, :].astype(BF16)
        for r0 in range(0, d_ff, CAST_ROWS):
            wd_b[r0:r0 + CAST_ROWS, :] = wd_ref[r0:r0 + CAST_ROWS, :].astype(BF16)

    def tile_step(j, carry):
        slot = (g0 + j) % 2
        in_copy(tile0 + j, slot).wait()

        @pl.when(j + 1 < n_t)
        def _():
            in_copy(tile0 + j + 1, 1 - slot).start()

        @pl.when(g0 + j >= 2)
        def _():
            out_copy(0, slot).wait()

        left = cnt - j * MOE_TM

        def ffn(row0, rows):
            lo, hi = _unpack_bf16_pairs(xbuf[slot, pl.ds(row0, rows)])
            valid = lax.broadcasted_iota(jnp.int32, lo.shape, 0) < left - row0
            x_lo = jnp.where(valid, lo, 0.0).astype(BF16)
            x_hi = jnp.where(valid, hi, 0.0).astype(BF16)

            def proj(c0):
                cols = slice(c0, c0 + MOE_FF_CHUNK)
                return (jnp.dot(x_lo, wgu_b[:half, cols], preferred_element_type=F32)
                        + jnp.dot(x_hi, wgu_b[half:, cols], preferred_element_type=F32)
                        + bgu_ref[:, cols])

            y = jnp.zeros((rows, d), F32)
            for f in range(d_ff // MOE_FF_CHUNK):
                c0 = f * MOE_FF_CHUNK
                gate = jnp.minimum(proj(c0), SWIGLU_LIMIT)
                up = jnp.clip(proj(d_ff + c0), -SWIGLU_LIMIT, SWIGLU_LIMIT)
                hh = (up + 1.0) * gate * _sigmoid(gate * SWIGLU_ALPHA)
                y = y + jnp.dot(hh.astype(BF16), wd_b[c0:c0 + MOE_FF_CHUNK, :],
                                preferred_element_type=F32)
            ybuf[slot, pl.ds(row0, rows)] = _pack_bf16_pairs(y + bdn_ref[...])

        n_sub = (jnp.minimum(left, MOE_TM) + MOE_SUB - 1) // MOE_SUB

        @pl.when(n_sub == MOE_TM // MOE_SUB)
        def _():
            ffn(0, MOE_TM)

        @pl.when(n_sub < MOE_TM // MOE_SUB)
        def _():
            def sub_step(i, c):
                ffn(pl.multiple_of(i * 2 * MOE_SUB, 2 * MOE_SUB), 2 * MOE_SUB)
                return c
            lax.fori_loop(0, n_sub // 2, sub_step, 0)

            @pl.when(n_sub % 2 == 1)
            def _():
                ffn(pl.multiple_of((n_sub // 2) * 2 * MOE_SUB, 2 * MOE_SUB), MOE_SUB)

        out_copy(tile0 + j, slot).start()
        return carry

    lax.fori_loop(0, n_t, tile_step, 0)

    g1 = g0 + n_t
    done_ref[0] = g1
    nxt = jnp.minimum(e + 1, n_e - 1)

    @pl.when(jnp.logical_and(e + 1 < n_e, cnt_ref[nxt] > 0))
    def _():
        in_copy(tile0_ref[nxt], g1 % 2).start()

    @pl.when(jnp.logical_and(e == n_e - 1, g1 >= 2))
    def _():
        out_copy(0, g1 % 2).wait()

    @pl.when(jnp.logical_and(e == n_e - 1, g1 >= 1))
    def _():
        out_copy(0, (g1 - 1) % 2).wait()


def _moe(xs, tile0, cnt, wgu, bgu, wd, bdn):
    p_rows, half = xs.shape
    n_e, d, gu = wgu.shape
    d_ff = wd.shape[1]
    w_blk = lambda e, tile0, cnt: (e, 0, 0)
    grid_spec = pltpu.PrefetchScalarGridSpec(
        num_scalar_prefetch=2,
        grid=(n_e,),
        in_specs=[pl.BlockSpec(memory_space=pl.ANY),
                  pl.BlockSpec((None, d, gu), w_blk),
                  pl.BlockSpec((None, 1, gu), w_blk),
                  pl.BlockSpec((None, d_ff, d), w_blk),
                  pl.BlockSpec((None, 1, d), w_blk)],
        out_specs=pl.BlockSpec(memory_space=pl.ANY),
        scratch_shapes=[pltpu.VMEM((d, gu), BF16), pltpu.VMEM((d_ff, d), BF16),
                        pltpu.VMEM((2, MOE_TM, half), jnp.uint32),
                        pltpu.VMEM((2, MOE_TM, half), jnp.uint32),
                        pltpu.SemaphoreType.DMA((2,)), pltpu.SemaphoreType.DMA((2,)),
                        pltpu.SMEM((1,), jnp.int32)])
    return pl.pallas_call(
        _moe_kernel,
        grid_spec=grid_spec,
        out_shape=jax.ShapeDtypeStruct((p_rows, half), jnp.uint32),
        compiler_params=pltpu.CompilerParams(dimension_semantics=("arbitrary",),
                                             vmem_limit_bytes=VMEM_LIMIT),
        name="moe_experts",
    )(tile0, cnt, xs, wgu, bgu, wd, bdn)


def _combine_kernel(h_ref, yg_ref, prob_ref, gf_ref, out_ref):
    tm = h_ref.shape[0]
    half = h_ref.shape[1] // 2
    p_rows = jnp.concatenate([prob_ref[...], jnp.zeros((8 - TOP_K, tm), F32)], axis=0)
    ph = p_rows.astype(BF16)
    pm, pl_ = _split(p_rows - ph.astype(F32))
    eye = (lax.broadcasted_iota(jnp.int32, (tm, tm), 0)
           == lax.broadcasted_iota(jnp.int32, (tm, tm), 1)).astype(BF16)
    nt = (((1,), (1,)), ((), ()))
    prob = sum(lax.dot_general(eye, piece, nt, preferred_element_type=F32)
               for piece in (ph, pm, pl_))
    acc_lo = jnp.zeros((tm, half), F32)
    acc_hi = jnp.zeros((tm, half), F32)
    for k in range(TOP_K):
        lo, hi = _unpack_bf16_pairs(yg_ref[k])
        pk = prob[:, k:k + 1]
        acc_lo = acc_lo + pk * lo
        acc_hi = acc_hi + pk * hi
    z = h_ref[...] + jnp.concatenate([acc_lo, acc_hi], axis=1)
    out_ref[...] = _rms(z, gf_ref[...])


def _combine(h, yg, row0, prob, gf, tm):
    n, d = h.shape
    tok = lambda w: pl.BlockSpec((tm, w), lambda i: (i, 0))
    t0 = row0 // tm
    return pl.pallas_call(
        _combine_kernel,
        grid=(n // tm,),
        in_specs=[tok(d), pl.BlockSpec((TOP_K, tm, d // 2), lambda i: (0, i + t0, 0)),
                  pl.BlockSpec((TOP_K, tm), lambda i: (0, i)), _full_spec(gf.shape)],
        out_specs=tok(d),
        out_shape=jax.ShapeDtypeStruct((n, d), F32),
        compiler_params=pltpu.CompilerParams(dimension_semantics=("parallel",),
                                             vmem_limit_bytes=VMEM_LIMIT),
        name="moe_combine",
    )(h, yg, prob, gf)


SC_CORES = 2
SC_SUBCORES = 16
SC_WORKERS = SC_CORES * SC_SUBCORES
SC_MAX_INDEX = 128


def _sc_chunk(rows_per_worker, limit=SC_MAX_INDEX):
    for c in range(limit, 7, -8):
        if rows_per_worker % c == 0:
            return c
    raise ValueError(f"no 8-aligned chunk divides {rows_per_worker} rows")


def _sc_mesh():
    return plsc.VectorSubcoreMesh(core_axis_name="c", subcore_axis_name="s")


def _sc_worker():
    return lax.axis_index("s") * SC_CORES + lax.axis_index("c")


def _scatter_rows(xs_in, pos_t, p_rows):
    w = xs_in[0].shape[1]
    dtype = xs_in[0].dtype
    n_k, n = pos_t.shape
    pos_flat = pos_t.reshape(n_k * n)
    parts, row0 = [], 0
    for x in xs_in:
        per = x.shape[0] // SC_WORKERS
        parts.append((row0, per, _sc_chunk(per, SC_MAX_INDEX // 2)))
        row0 += x.shape[0]
    max_chunk = max(c for _, _, c in parts)
    max_per = max(p for _, p, _ in parts)

    def body(*refs):
        x_hbms = refs[:len(xs_in)]
        pos_hbm, out_hbm, idx_v, rows_v, sem_l, sem_s = refs[len(xs_in):]
        for x_hbm, (tok0, per, chunk) in zip(x_hbms, parts):
            n_chunks = per // chunk
            base = pl.multiple_of(_sc_worker() * per, 8)
            for k in range(n_k):
                src = pl.multiple_of(k * n + tok0 + base, 8)
                pltpu.sync_copy(pos_hbm.at[pl.ds(src, per)], idx_v.at[pl.ds(k * per, per)])

            def load(c, slot):
                src = x_hbm.at[pl.ds(pl.multiple_of(base + c * chunk, 8), chunk)]
                return pltpu.make_async_copy(src, rows_v.at[slot, pl.ds(0, chunk)], sem_l.at[slot])

            def scatter(c, k, slot):
                rows = idx_v.at[pl.ds(pl.multiple_of(k * per + c * chunk, 8), chunk)]
                return pltpu.make_async_copy(rows_v.at[slot, pl.ds(0, chunk)], out_hbm.at[rows],
                                             sem_s.at[slot])

            load(0, 0).start()

            @pl.loop(0, n_chunks)
            def _(c):
                slot = c % 2
                load(c, slot).wait()

                @pl.when(c + 1 < n_chunks)
                def _():
                    @pl.when(c >= 1)
                    def _():
                        for k in range(n_k):
                            scatter(c - 1, k, 1 - slot).wait()
                    load(c + 1, 1 - slot).start()

                for k in range(n_k):
                    scatter(c, k, slot).start()

            if n_chunks >= 2:
                for k in range(n_k):
                    scatter(n_chunks - 2, k, n_chunks % 2).wait()
            for k in range(n_k):
                scatter(n_chunks - 1, k, (n_chunks - 1) % 2).wait()

    return pl.kernel(
        body, out_type=jax.ShapeDtypeStruct((p_rows, w), dtype), mesh=_sc_mesh(),
        scratch_types=[pltpu.VMEM((n_k * max_per,), jnp.int32),
                       pltpu.VMEM((2, max_chunk, w), dtype),
                       pltpu.SemaphoreType.DMA((2,)), pltpu.SemaphoreType.DMA((2,))],
        name="sc_scatter_rows")(*xs_in, pos_flat)


def _gather_rows(table, idx):
    n = idx.shape[0]
    w = table.shape[1]
    per = n // SC_WORKERS
    chunk = _sc_chunk(per)

    n_chunks = per // chunk

    def body(table_hbm, idx_hbm, out_hbm, idx_v, rows_v, sem_g, sem_w):
        base = pl.multiple_of(_sc_worker() * per, 8)
        pltpu.sync_copy(idx_hbm.at[pl.ds(base, per)], idx_v)

        def gather(c, slot):
            rows = idx_v.at[pl.ds(pl.multiple_of(c * chunk, 8), chunk)]
            return pltpu.make_async_copy(table_hbm.at[rows], rows_v.at[slot], sem_g.at[slot])

        def write(c, slot):
            dst = out_hbm.at[pl.ds(pl.multiple_of(base + c * chunk, 8), chunk)]
            return pltpu.make_async_copy(rows_v.at[slot], dst, sem_w.at[slot])

        gather(0, 0).start()

        @pl.loop(0, n_chunks)
        def _(c):
            slot = c % 2
            gather(c, slot).wait()

            @pl.when(c + 1 < n_chunks)
            def _():
                @pl.when(c >= 1)
                def _():
                    write(c - 1, 1 - slot).wait()
                gather(c + 1, 1 - slot).start()

            write(c, slot).start()

        if n_chunks >= 2:
            write(n_chunks - 2, n_chunks % 2).wait()
        write(n_chunks - 1, (n_chunks - 1) % 2).wait()

    return pl.kernel(
        body, out_type=jax.ShapeDtypeStruct((n, w), table.dtype), mesh=_sc_mesh(),
        scratch_types=[pltpu.VMEM((per,), jnp.int32), pltpu.VMEM((2, chunk, w), table.dtype),
                       pltpu.SemaphoreType.DMA((2,)), pltpu.SemaphoreType.DMA((2,))],
        name="sc_gather_rows")(table, idx)


def _mix_matrix(ws, block, seq_rows):
    causal = jnp.tril(jnp.ones((CHUNK, CHUNK), dtype=bool))
    w = jnp.where(causal[None], ws, 0.0)[:, :seq_rows, :seq_rows]
    eye = jnp.eye(block // seq_rows, dtype=ws.dtype)
    return jnp.einsum("ab,gts->gatbs", eye, w).reshape(ws.shape[0], block, block).astype(BF16)


TOKEN_TILE = 512


def _stream(x, shift_in, wkv_in, p, seq_rows):
    nb, t, d = x.shape
    n = nb * t
    tm = TOKEN_TILE
    x2d = x.reshape(n, d)
    rows = CHUNK if seq_rows is None else seq_rows
    wmix = _mix_matrix(p["w_spatial"], max(rows, LANES), rows)
    pos = jnp.arange(tm) % rows
    d_a = p["wu"].shape[1]
    bias_full = jnp.repeat(p["b_spatial"].T[pos], d_a // G_A, axis=1)
    apart, vn = _branch_a(x2d, tm, p["norm1_g"], p["wu"], p["wv"], p["wga"], p["vnorm_g"],
                          p["vnorm_b"], wmix, bias_full, p["w_a_out"])
    if seq_rows is None:
        xb, ext = x, jnp.zeros((1, p["wcur"].shape[1]), F32)
    else:
        xb = x2d.reshape(n // tm, tm, d)
        ext = jnp.repeat(shift_in, seq_rows, axis=0).reshape(n // tm, tm, -1)
    outs = _branch_b(xb, tm, seq_rows, ext, p["norm1_g"], p["wcur"], p["wgb"], p["mu_shift"],
                     p["w0"], p["w2"], p["a0"], p["a2"], p["g2"], p["k_k"], p["k_a"], p["r_k"],
                     p["bd"])
    r, k2, v, kk, kka, lw, g, bonus, sgb, cur = outs
    d_b = r.shape[-1]
    if seq_rows is None:
        shift_out = cur[:, -1, :]
        y, s_out = _scan(r, k2, v, kk, kka, lw)
        y2d = y.reshape(n, d_b)
    else:
        shift_out = cur.reshape(nb, t, -1)[:, -1, :]
        y2d, s_lanes = _scan_steps(*[a.reshape(n, d_b) for a in (r, k2, v, kk, kka, lw)],
                                   jnp.transpose(wkv_in, (1, 2, 3, 0)), t)
        s_out = jnp.transpose(s_lanes, (3, 0, 1, 2))
    flat = lambda a: a.reshape(n, a.shape[-1])
    routed = _merge(y2d, flat(g), flat(bonus), flat(sgb), apart, x2d, tm, p["lnx_g"],
                    p["lnx_b"], p["bd"], p["w_b_out"], p["w_out"], p["norm2_g"],
                    p["router_w"], p["router_b"])
    return routed, vn, shift_out, s_out


def kernel(x_prompt, x_sample, state_shift, state_wkv, norm1_g, w_in, mu_shift, vnorm_g, vnorm_b, w_spatial, b_spatial, w_a_out, w0, w2, a0, a2, g2, k_k, k_a, r_k, lnx_g, lnx_b, w_b_out, w_out, norm2_g, router_w, router_b, exp_w_gu, exp_b_gu, exp_w_down, exp_b_down, normf_g):
    assert w_in.shape[0] == 1, "one layer: the final norm is fused into its MoE combine"
    l = 0
    d_model = x_prompt.shape[-1]
    shift_w = mu_shift.shape[-1]
    d_a = vnorm_g.shape[-1]
    bp, tp, _ = x_prompt.shape
    bs, ts, _ = x_sample.shape
    head_id = jnp.arange(2 * LANES) // HEAD_B
    bd = (head_id[:, None] == head_id[None, :]).astype(BF16)
    row = lambda a: a.reshape(1, -1)

    wi = w_in[l].astype(BF16)
    o = shift_w
    p = dict(
        norm1_g=row(norm1_g[l]), wcur=wi[:, :o], wu=wi[:, o:o + d_a],
        wv=wi[:, o + d_a:o + 2 * d_a], wga=wi[:, o + 2 * d_a:o + 2 * d_a + d_model],
        wgb=wi[:, o + 2 * d_a + d_model:], mu_shift=row(mu_shift[l]),
        vnorm_g=row(vnorm_g[l]), vnorm_b=row(vnorm_b[l]), w_spatial=w_spatial[l],
        b_spatial=b_spatial[l], w_a_out=w_a_out[l].astype(BF16), w0=row(w0[l]),
        w2=_stack_rhs3(w2[l]), a0=row(a0[l]), a2=_stack_rhs3(a2[l]),
        g2=jnp.concatenate([g2[l].astype(BF16)] * 2, axis=0),
        k_k=row(k_k[l]), k_a=row(k_a[l]),
        r_k=row(r_k[l]), lnx_g=row(lnx_g[l]), lnx_b=row(lnx_b[l]),
        w_b_out=w_b_out[l].astype(BF16), w_out=w_out[l].astype(BF16),
        norm2_g=row(norm2_g[l]), router_w=jnp.concatenate(_split(router_w[l].T), axis=0),
        router_b=router_b[l].reshape(-1, 1), bd=bd)

    routed_p, _, sh_p, s_p = _stream(x_prompt, None, None, p, None)
    routed_s, vn_s, sh_s, s_s = _stream(x_sample, state_shift[l], state_wkv[l], p, ts)
    h_p, xp_p, topi_p, prob_p, rank_p, cnt_p = routed_p
    h_s, xp_s, topi_s, prob_s, rank_s, cnt_s = routed_s

    n_p, n_s = h_p.shape[0], h_s.shape[0]
    n_e = router_w.shape[-1]
    n_tiles = (n_p + n_s) * TOP_K // MOE_TM + n_e
    pos_t, tile0, cnt = _route(topi_p, rank_p, cnt_p, topi_s, rank_s, cnt_s)
    xs = _scatter_rows([xp_p, xp_s], pos_t, n_tiles * MOE_TM)
    ys = _moe(xs, tile0, cnt, exp_w_gu[l], exp_b_gu[l][:, None, :],
              exp_w_down[l], exp_b_down[l][:, None, :])
    yg = _gather_rows(ys, pos_t.reshape(-1)).reshape(TOP_K, n_p + n_s, d_model // 2)
    gf = row(normf_g)
    y_p = _combine(h_p, yg, 0, prob_p, gf, TOKEN_TILE).reshape(bp, tp, d_model)
    y_s = _combine(h_s, yg, n_p, prob_s, gf, TOKEN_TILE).reshape(bs, ts, d_model)
    return (y_p, y_s, sh_p[None], s_p[None], sh_s[None], s_s[None],
            vn_s.reshape(1, bs, ts, d_a))
```

```python
import functools
import math

import jax
import jax.numpy as jnp
from jax import lax
from jax.experimental import pallas as pl
from jax.experimental.pallas import tpu as pltpu
from jax.experimental.pallas import tpu_sc as plsc

F32 = jnp.float32
BF16 = jnp.bfloat16

CHUNK = 128
G_A = 8
HEAD_B = 64
R_W, R_A, R_G = 64, 64, 128
TOP_K = 4
SWIGLU_LIMIT = 7.0
SWIGLU_ALPHA = 1.702
EPS = 1e-5
GN_EPS = HEAD_B * 1e-5

LANES = 128
PAIR = 2 * HEAD_B
VMEM_LIMIT = 56 * 1024 * 1024


def _dot(a, b):
    return jnp.dot(a.astype(BF16), b.astype(BF16), preferred_element_type=F32)


def _split(x):
    hi = x.astype(BF16)
    lo = (x - hi.astype(F32)).astype(BF16)
    return hi, lo


def _stack_rhs3(b):
    bh, bl = _split(b)
    return jnp.concatenate([bh, bh, bl], axis=0)


def _dot3_stacked(a, b_stacked):
    ah, al = _split(a)
    return jnp.dot(jnp.concatenate([ah, al, ah], axis=1), b_stacked, preferred_element_type=F32)


def _dot2_stacked(a, b_stacked):
    ah, al = _split(a)
    return jnp.dot(jnp.concatenate([ah, al], axis=1), b_stacked, preferred_element_type=F32)


def _head_sums(x, ones_bd):
    w = ones_bd.shape[0]
    return jnp.concatenate([jnp.dot(x[:, c:c + w].astype(BF16), ones_bd,
                                    preferred_element_type=F32)
                            for c in range(0, x.shape[1], w)], axis=1)


def _dot_exact_lhs(a_bf16, b):
    bh, bl = _split(b)
    bm = b - bh.astype(F32) - bl.astype(F32)
    return (jnp.dot(a_bf16, bh, preferred_element_type=F32)
            + jnp.dot(a_bf16, bl, preferred_element_type=F32)
            + jnp.dot(a_bf16, bm.astype(BF16), preferred_element_type=F32))


def _rms(x, g):
    return x * lax.rsqrt(jnp.mean(x * x, axis=-1, keepdims=True) + EPS) * g


def _sigmoid(x):
    return 1.0 / (1.0 + jnp.exp(-x))


def _full_spec(shape):
    nd = len(shape)
    return pl.BlockSpec(shape, lambda *_: (0,) * nd, pipeline_mode=pl.Buffered(1))


def _branch_a_kernel(x_ref, g1_ref, wu_ref, wv_ref, wga_ref, vg_ref, vb_ref, wmix_ref,
                     bias_ref, wao_ref, apart_ref, vn_ref):
    xn = _rms(x_ref[...], g1_ref[...]).astype(BF16)
    v = jnp.dot(xn, wv_ref[...], preferred_element_type=F32)
    mu = jnp.mean(v, axis=-1, keepdims=True)
    d = v - mu
    var = jnp.mean(d * d, axis=-1, keepdims=True)
    vn = d * lax.rsqrt(var + EPS) * vg_ref[...] + vb_ref[...]
    vn_ref[...] = vn
    vnb = vn.astype(BF16)
    c_a = vnb.shape[1] // G_A
    rb = wmix_ref.shape[1]
    mixed = jnp.concatenate(
        [jnp.concatenate(
            [jnp.dot(wmix_ref[g], vnb[r0:r0 + rb, g * c_a:(g + 1) * c_a],
                     preferred_element_type=F32) for g in range(G_A)], axis=1)
         for r0 in range(0, vnb.shape[0], rb)], axis=0) + bias_ref[...]
    u = jnp.dot(xn, wu_ref[...], preferred_element_type=F32)
    ya = (u * mixed).astype(BF16)
    ga = jnp.dot(xn, wga_ref[...], preferred_element_type=F32)
    apart_ref[...] = _sigmoid(ga) * jnp.dot(ya, wao_ref[...], preferred_element_type=F32)


def _branch_a(x2d, tm, g1, wu, wv, wga, vg, vb, wmix, bias_full, wao):
    n, d = x2d.shape
    d_a = wu.shape[1]
    tok = lambda w: pl.BlockSpec((tm, w), lambda i: (i, 0))
    return pl.pallas_call(
        _branch_a_kernel,
        grid=(n // tm,),
        in_specs=[tok(d), _full_spec(g1.shape), _full_spec(wu.shape), _full_spec(wv.shape),
                  _full_spec(wga.shape), _full_spec(vg.shape), _full_spec(vb.shape),
                  _full_spec(wmix.shape), _full_spec(bias_full.shape), _full_spec(wao.shape)],
        out_specs=[tok(d), tok(d_a)],
        out_shape=[jax.ShapeDtypeStruct((n, d), F32), jax.ShapeDtypeStruct((n, d_a), F32)],
        compiler_params=pltpu.CompilerParams(dimension_semantics=("parallel",),
                                             vmem_limit_bytes=VMEM_LIMIT),
        name="branch_a",
    )(x2d, g1, wu, wv, wga, vg, vb, wmix, bias_full, wao)


def _branch_b_kernel(x_ref, g1_ref, wcur_ref, wgb_ref, mu_ref, ext_ref, w0_ref, w2_ref,
                     a0_ref, a2_ref, g2_ref, kk_ref, ka_ref, rk_ref, bd_ref,
                     r_out, k_out, v_out, kk_out, kka_out, lw_out, g_out, bonus_out,
                     sgb_out, cur_out, carry_scr, *, seq_rows, d_b):
    tm = x_ref.shape[0]
    xn = _rms(x_ref[...], g1_ref[...]).astype(BF16)
    cur = jnp.dot(xn, wcur_ref[...], preferred_element_type=F32)
    rolled = pltpu.roll(cur, 1, axis=0)
    row = lax.broadcasted_iota(jnp.int32, (tm, 1), 0)
    if seq_rows is None:
        first_tile = pl.program_id(1) == 0
        carry = jnp.where(first_tile, 0.0, carry_scr[...])
        prev = jnp.where(row == 0, carry, rolled)
        carry_scr[...] = cur[tm - 1:tm, :]
        cur_out[...] = cur[tm - 8:tm, :]
    else:
        prev = jnp.where(row % seq_rows == 0, ext_ref[...], rolled)
        cur_out[...] = cur
    xs = cur + (prev - cur) * mu_ref[...]
    r = xs[:, 0:d_b]
    k = xs[:, d_b:2 * d_b]
    v = xs[:, 2 * d_b:3 * d_b]
    o = 3 * d_b
    xw = xs[:, o:o + R_W]
    xa = xs[:, o + R_W:o + R_W + R_A]
    xg = xs[:, o + R_W + R_A:o + R_W + R_A + R_G]
    wl = w0_ref[...] + _dot3_stacked(jnp.tanh(xw), w2_ref[...])
    lw = -math.exp(-0.5) * _sigmoid(wl)
    a = _sigmoid(a0_ref[...] + _dot3_stacked(xa, a2_ref[...]))
    g = _dot2_stacked(_sigmoid(xg), g2_ref[...])
    bd = bd_ref[...]
    kkr = k * kk_ref[...]
    n2 = _head_sums(kkr * kkr, bd)
    kk = kkr * lax.rsqrt(jnp.maximum(n2, 1e-24))
    k2 = k * (1.0 + (a - 1.0) * ka_ref[...])
    bonus = _head_sums(r * k2 * rk_ref[...], bd) * v
    r_out[...] = r
    k_out[...] = k2
    v_out[...] = v
    kk_out[...] = kk
    kka_out[...] = kk * a
    lw_out[...] = lw
    g_out[...] = g
    bonus_out[...] = bonus
    sgb_out[...] = _sigmoid(jnp.dot(xn, wgb_ref[...], preferred_element_type=F32))


def _branch_b(x3d, tm, seq_rows, ext, g1, wcur, wgb, mu, w0, w2, a0, a2, g2, k_k, k_a, r_k, bd):
    nb, t, d = x3d.shape
    d_b = w0.shape[1]
    shift_w = wcur.shape[1]
    nt = t // tm
    tok = lambda w: pl.BlockSpec((None, tm, w), lambda b, i: (b, i, 0))
    cur_rows = 8 if seq_rows is None else tm
    outs = [jax.ShapeDtypeStruct((nb, t, d_b), F32)] * 8 + [
        jax.ShapeDtypeStruct((nb, t, d), F32),
        jax.ShapeDtypeStruct((nb, nt * cur_rows, shift_w), F32)]
    out_specs = [tok(d_b)] * 8 + [tok(d), pl.BlockSpec((None, cur_rows, shift_w),
                                                       lambda b, i: (b, i, 0))]
    weights = (g1, wcur, wgb, mu)
    small = (w0, w2, a0, a2, g2, k_k, k_a, r_k, bd)
    return pl.pallas_call(
        functools.partial(_branch_b_kernel, seq_rows=seq_rows, d_b=d_b),
        grid=(nb, nt),
        in_specs=[tok(d)] + [_full_spec(w.shape) for w in weights]
        + [tok(shift_w) if seq_rows is not None else _full_spec(ext.shape)]
        + [_full_spec(w.shape) for w in small],
        out_specs=out_specs,
        out_shape=outs,
        scratch_shapes=[pltpu.VMEM((1, shift_w), F32)],
        compiler_params=pltpu.CompilerParams(dimension_semantics=("parallel", "arbitrary"),
                                             vmem_limit_bytes=VMEM_LIMIT),
        name="branch_b",
    )(x3d, *weights, ext, *small)


def _scan_kernel(r_ref, k_ref, v_ref, kk_ref, kka_ref, lw_ref, y_ref, sout_ref, s_scr, *, chunk):
    L = chunk
    W2 = 2 * L
    c = pl.program_id(1)
    n_seq, n_pairs = s_scr.shape[:2]

    @pl.when(c == 0)
    def _():
        s_scr[...] = jnp.zeros_like(s_scr)

    iota = lambda shape, dim: lax.broadcasted_iota(jnp.int32, shape, dim)
    stack_mask = (iota((W2, 1), 0) >= L) == (iota((1, PAIR), 1) >= HEAD_B)
    bd_mask = (iota((W2, 1), 0) >= L) == (iota((1, W2), 1) >= L)
    assert L & (L - 1) == 0, "chunk length must be a power of two"
    rw = iota((L, W2), 0)
    cw = iota((L, W2), 1) & (L - 1)
    strict_w = cw < rw
    eye_w = (cw == rw).astype(F32)
    incl_w2 = (iota((L, 2 * W2), 1) & (L - 1)) <= iota((L, 2 * W2), 0)
    tri = (iota((L, L), 1) <= iota((L, L), 0)).astype(BF16)
    di = iota((PAIR, PAIR), 0)
    dj = iota((PAIR, PAIR), 1)
    diag = di == dj
    head_diag = (di >= HEAD_B) == (dj >= HEAD_B)

    def stack(x):
        return jnp.where(stack_mask, jnp.concatenate([x, x], axis=0), 0.0)

    def bd(xw):
        return jnp.where(bd_mask, jnp.concatenate([xw, xw], axis=0), 0.0)

    n_double = max(L.bit_length() - 2, 0)
    prep = []
    for s in range(n_seq):
        lw_all = lw_ref[s]
        cum_all = _dot_exact_lhs(tri, lw_all)
        for p in range(n_pairs):
            sl = slice(p * PAIR, (p + 1) * PAIR)
            cum = cum_all[:, sl]
            cum_last = cum[L - 1:L, :]
            e_neg = jnp.exp(-cum)
            e_rel = jnp.exp(cum_last - cum)
            kk = kk_ref[s, :, sl]
            kka = kka_ref[s, :, sl]
            kx = k_ref[s, :, sl]
            prep.append(dict(
                s=s, p=p, sl=sl, ab=-kk * jnp.exp(cum - lw_all[:, sl]),
                rb=r_ref[s, :, sl] * jnp.exp(cum), bb=kka * e_neg, kb=kx * e_neg,
                bt=kka * e_rel, kt=kx * e_rel, v=v_ref[s, :, sl], decay=jnp.exp(cum_last)))
    a_w = [lax.dot_general(
        jnp.concatenate([q["ab"], q["rb"]], axis=0).astype(BF16),
        jnp.concatenate([stack(q["bb"]), stack(q["kb"])], axis=0).astype(BF16),
        (((1,), (1,)), ((), ())), preferred_element_type=F32) for q in prep]
    nw = [jnp.where(strict_w, a[:L, :W2], 0.0) for a in a_w]
    tw = [eye_w + n for n in nw]
    nbd = [bd(n) for n in nw]
    for _ in range(n_double):
        nw = [_dot(n, b) for n, b in zip(nw, nbd)]
        nbd = [bd(n) for n in nw]
        tw = [t + _dot(t, b) for t, b in zip(tw, nbd)]
    akv = [_dot(jnp.where(strict_w, a[:L, W2:], 0.0), stack(q["v"])) for a, q in zip(a_w, prep)]
    tx = [_dot(t, jnp.concatenate([stack(q["ab"]), stack(kv)], axis=1))
          for t, q, kv in zip(tw, prep, akv)]
    ry = [_dot(jnp.where(incl_w2, a[L:], 0.0),
               jnp.concatenate(
                   [jnp.concatenate([stack(x[:, :PAIR]), stack(x[:, PAIR:])], axis=1),
                    jnp.concatenate([jnp.zeros((W2, PAIR), F32), stack(q["v"])], axis=1)], axis=0))
          for a, x, q in zip(a_w, tx, prep)]
    for q, x, y in zip(prep, tx, ry):
        st = s_scr[q["s"], q["p"]]
        us = _dot(jnp.concatenate([x[:, :PAIR], q["rb"] + y[:, :PAIR]], axis=0), st)
        y_ref[q["s"], :, q["sl"]] = us[L:] + y[:, PAIR:]
        u = us[:L] + x[:, PAIR:]
        lhs = jnp.concatenate([jnp.where(diag, q["decay"], 0.0),
                               jnp.concatenate([q["bt"], q["kt"]], axis=0).T], axis=1)
        g = _dot(lhs, jnp.concatenate([st, u, q["v"]], axis=0))
        s_scr[q["s"], q["p"]] = jnp.where(head_diag, g, 0.0)

    @pl.when(c == pl.num_programs(1) - 1)
    def _():
        for s in range(n_seq):
            for p in range(n_pairs):
                nat = s_scr[s, p].T
                sout_ref[s, 2 * p] = nat[:HEAD_B, :HEAD_B]
                sout_ref[s, 2 * p + 1] = nat[HEAD_B:, HEAD_B:]


SCAN_CHUNK = 64
SCAN_SEQS = 4


def _scan(r, k, v, kk, kka, lw):
    nb, t, d_b = r.shape
    n_pairs = d_b // PAIR
    n_heads = d_b // HEAD_B
    tok = pl.BlockSpec((SCAN_SEQS, SCAN_CHUNK, d_b), lambda b, c: (b, c, 0))
    st = pl.BlockSpec((SCAN_SEQS, n_heads, HEAD_B, HEAD_B), lambda b, c: (b, 0, 0, 0))
    return pl.pallas_call(
        functools.partial(_scan_kernel, chunk=SCAN_CHUNK),
        grid=(nb // SCAN_SEQS, t // SCAN_CHUNK),
        in_specs=[tok] * 6,
        out_specs=[tok, st],
        out_shape=[jax.ShapeDtypeStruct((nb, t, d_b), F32),
                   jax.ShapeDtypeStruct((nb, n_heads, HEAD_B, HEAD_B), F32)],
        scratch_shapes=[pltpu.VMEM((SCAN_SEQS, n_pairs, PAIR, PAIR), F32)],
        compiler_params=pltpu.CompilerParams(dimension_semantics=("parallel", "arbitrary"),
                                             vmem_limit_bytes=VMEM_LIMIT),
        name="rwkv_scan",
    )(r, k, v, kk, kka, lw)


STEP_ROWS = 8


def _steps_kernel(r_ref, k_ref, v_ref, kk_ref, kka_ref, lw_ref, s0_ref, y_ref, s_ref,
                  vec_scr, yt_scr, *, n_steps):
    n_seq = s_ref.shape[-1]
    s_ref[...] = s0_ref[...]
    for t in range(n_steps):
        rows = pl.ds(t, n_seq, stride=n_steps)
        vecs = {}
        for slot, (name, ref) in enumerate((("r", r_ref), ("k", k_ref), ("v", v_ref),
                                            ("kk", kk_ref), ("kka", kka_ref), ("lw", lw_ref))):
            vecs[name] = ref[rows, :].T
            vec_scr[slot] = vecs[name]
        for q in range(PAIR // HEAD_B):
            hs = slice(q * HEAD_B, (q + 1) * HEAD_B)
            r_q, k_q, kka_q = vecs["r"][hs], vecs["k"][hs], vecs["kka"][hs]
            nkk_q = -vecs["kk"][hs]
            w_q = jnp.exp(vecs["lw"][hs])

            def rows_step(i0, carry):
                for u in range(STEP_ROWS):
                    i = i0 * STEP_ROWS + u
                    s_row = s_ref[q, i]
                    sa = jnp.sum(s_row * nkk_q, axis=0, keepdims=True)
                    v_i = vec_scr[2, pl.ds(q * HEAD_B + i, 1), :]
                    s_new = s_row * w_q + sa * kka_q + v_i * k_q
                    s_ref[q, i] = s_new
                    yt_scr[pl.ds(q * HEAD_B + i, 1), :] = jnp.sum(s_new * r_q, axis=0,
                                                                  keepdims=True)
                return carry

            lax.fori_loop(0, HEAD_B // STEP_ROWS, rows_step, 0)
        y_ref[rows, :] = yt_scr[...].T


def _scan_steps(r, k, v, kk, kka, lw, s0, n_steps):
    n, d_b = r.shape
    n_heads, _, _, n_seq = s0.shape
    tok = pl.BlockSpec((n, PAIR), lambda p: (0, p))
    st = pl.BlockSpec((PAIR // HEAD_B, HEAD_B, HEAD_B, n_seq), lambda p: (p, 0, 0, 0))
    return pl.pallas_call(
        functools.partial(_steps_kernel, n_steps=n_steps),
        grid=(d_b // PAIR,),
        in_specs=[tok] * 6 + [st],
        out_specs=[tok, st],
        out_shape=[jax.ShapeDtypeStruct((n, d_b), F32), jax.ShapeDtypeStruct(s0.shape, F32)],
        scratch_shapes=[pltpu.VMEM((6, PAIR, n_seq), F32), pltpu.VMEM((PAIR, n_seq), F32)],
        compiler_params=pltpu.CompilerParams(dimension_semantics=("parallel",),
                                             vmem_limit_bytes=VMEM_LIMIT),
        name="rwkv_steps",
    )(r, k, v, kk, kka, lw, s0)


def _pack_bf16_pairs(x):
    w = x.shape[1] // 2
    bits = lambda v: lax.bitcast_convert_type(v.astype(BF16).astype(F32), jnp.uint32)
    return (bits(x[:, :w]) >> 16) | (bits(x[:, w:]) & jnp.uint32(0xFFFF0000))


def _unpack_bf16_pairs(u):
    lo = lax.bitcast_convert_type(u << 16, F32)
    hi = lax.bitcast_convert_type(u & jnp.uint32(0xFFFF0000), F32)
    return lo, hi


def _merge_kernel(y_ref, g_ref, bonus_ref, sgb_ref, apart_ref, x_ref, lg_ref, lb_ref, bd_ref,
                  wbo_ref, wo_ref, g2_ref, rwt_ref, rb_ref,
                  h_out, xp_out, topi_out, prob_out, rank_out, cnt_out, cnt_scr):
    step = pl.program_id(0)

    @pl.when(step == 0)
    def _():
        cnt_scr[...] = jnp.zeros_like(cnt_scr)

    bd = bd_ref[...]
    y = y_ref[...]
    inv_n = 1.0 / HEAD_B
    mu = _head_sums(y, bd) * inv_n
    d = y - mu
    var = _head_sums(d * d, bd) * inv_n
    yn = d * lax.rsqrt(var + GN_EPS) * lg_ref[...] + lb_ref[...]
    yb = ((yn + bonus_ref[...]) * g_ref[...]).astype(BF16)
    merged = apart_ref[...] + sgb_ref[...] * jnp.dot(yb, wbo_ref[...], preferred_element_type=F32)
    h = x_ref[...] + jnp.dot(merged.astype(BF16), wo_ref[...], preferred_element_type=F32)
    h_out[...] = h
    xn2 = _rms(h, g2_ref[...])
    xp_out[...] = _pack_bf16_pairs(xn2)
    n_e = rb_ref.shape[0]
    tm = xn2.shape[0]
    xh, xl = _split(xn2)
    rwt = rwt_ref[...]
    nt = (((1,), (1,)), ((), ()))
    t = lax.dot_general(rwt, xh, nt, preferred_element_type=F32)
    logits = (t[:n_e] + t[n_e:] + lax.dot_general(rwt[:n_e], xl, nt, preferred_element_type=F32)
              + rb_ref[...])
    idx = lax.broadcasted_iota(jnp.int32, logits.shape, 0).astype(F32)
    work = logits
    tops, hots, sels = [], [], []
    for _ in range(TOP_K):
        m = jnp.max(work, axis=0, keepdims=True)
        sel = jnp.min(jnp.where(work == m, idx, float(n_e)), axis=0, keepdims=True)
        hot = idx == sel
        tops.append(m)
        hots.append(hot)
        sels.append(sel)
        work = jnp.where(hot, -jnp.inf, work)
    es = [jnp.exp(t - tops[0]) for t in tops]
    denom = es[0] + es[1] + es[2] + es[3]
    topi_out[...] = jnp.concatenate(sels, axis=0).astype(jnp.int32)
    prob_out[...] = jnp.concatenate([e / denom for e in es], axis=0)
    hot_any = jnp.zeros_like(logits)
    for hot in hots:
        hot_any = hot_any + hot.astype(F32)
    ri = lax.broadcasted_iota(jnp.int32, (tm, tm), 0)
    ci = lax.broadcasted_iota(jnp.int32, (tm, tm), 1)
    before = _dot(hot_any, (ri < ci).astype(BF16)) + cnt_scr[...]
    rank_out[...] = jnp.concatenate(
        [jnp.sum(jnp.where(hot, before, 0.0), axis=0, keepdims=True) for hot in hots],
        axis=0).astype(jnp.int32)
    total = cnt_scr[...] + jnp.sum(hot_any, axis=1, keepdims=True)
    cnt_scr[...] = total
    cnt_out[...] = total.astype(jnp.int32)


def _merge(y, g, bonus, sgb, apart, x2d, tm, lnx_g, lnx_b, bd, wbo, wo, g2n, rw, rb):
    n, d = x2d.shape
    d_b = y.shape[1]
    n_e = rb.shape[0]
    tok = lambda w: pl.BlockSpec((tm, w), lambda i: (i, 0))
    per_k = pl.BlockSpec((TOP_K, tm), lambda i: (0, i))
    weights = (lnx_g, lnx_b, bd, wbo, wo, g2n, rw, rb)
    return pl.pallas_call(
        _merge_kernel,
        grid=(n // tm,),
        in_specs=[tok(d_b), tok(d_b), tok(d_b), tok(d), tok(d), tok(d)]
        + [_full_spec(w.shape) for w in weights],
        out_specs=[tok(d), tok(d // 2), per_k, per_k, per_k,
                   pl.BlockSpec((n_e, 1), lambda i: (0, 0))],
        out_shape=[jax.ShapeDtypeStruct((n, d), F32),
                   jax.ShapeDtypeStruct((n, d // 2), jnp.uint32),
                   jax.ShapeDtypeStruct((TOP_K, n), jnp.int32),
                   jax.ShapeDtypeStruct((TOP_K, n), F32),
                   jax.ShapeDtypeStruct((TOP_K, n), jnp.int32),
                   jax.ShapeDtypeStruct((n_e, 1), jnp.int32)],
        scratch_shapes=[pltpu.VMEM((n_e, 1), F32)],
        compiler_params=pltpu.CompilerParams(dimension_semantics=("arbitrary",),
                                             vmem_limit_bytes=VMEM_LIMIT),
        name="merge_router",
    )(y, g, bonus, sgb, apart, x2d, *weights)


MOE_TM = 1024
MOE_SUB = 128
MOE_FF_CHUNK = 512
CAST_ROWS = 256


def _route(topi_p, rank_p, cnt_p, topi_s, rank_s, cnt_s):
    n_e = cnt_p.size
    cnt_p, cnt_s = cnt_p.reshape(n_e), cnt_s.reshape(n_e)
    cnt = cnt_p + cnt_s
    padded = (cnt + MOE_TM - 1) // MOE_TM * MOE_TM
    ends = jnp.cumsum(padded)
    base = ends - padded
    lookup = lambda table, idx: jnp.sum(
        jnp.where(idx[..., None] == jnp.arange(n_e), table, 0), axis=-1)
    pos_t = jnp.concatenate([lookup(base, topi_p) + rank_p,
                             lookup(base + cnt_p, topi_s) + rank_s], axis=1)
    return (pos_t.astype(jnp.int32), (base // MOE_TM).astype(jnp.int32),
            cnt.astype(jnp.int32))


def _moe_kernel(tile0_ref, cnt_ref, xs_hbm, wgu_ref, bgu_ref, wd_ref, bdn_ref, ys_hbm,
                wgu_b, wd_b, xbuf, ybuf, sem_in, sem_out, done_ref):
    e = pl.program_id(0)
    n_e = pl.num_programs(0)

    @pl.when(e == 0)
    def _():
        done_ref[0] = 0

    g0 = done_ref[0]
    tile0 = tile0_ref[e]
    cnt = cnt_ref[e]
    n_t = (cnt + MOE_TM - 1) // MOE_TM
    d, gu = wgu_ref.shape
    d_ff = wd_ref.shape[0]
    half = d // 2

    def in_copy(tile, slot):
        return pltpu.make_async_copy(xs_hbm.at[pl.ds(tile * MOE_TM, MOE_TM)], xbuf.at[slot],
                                     sem_in.at[slot])

    def out_copy(tile, slot):
        return pltpu.make_async_copy(ybuf.at[slot], ys_hbm.at[pl.ds(tile * MOE_TM, MOE_TM)],
                                     sem_out.at[slot])

    @pl.when(jnp.logical_and(e == 0, n_t > 0))
    def _():
        in_copy(tile0, 0).start()

    @pl.when(n_t > 0)
    def _():
        for r0 in range(0, d, CAST_ROWS):
            wgu_b[r0:r0 + CAST_ROWS, :] = wgu_ref[r0:r0 + CAST_ROWS, :].astype(BF16)
        for r0 in range(0, d_ff, CAST_ROWS):
            wd_b[r0:r0 + CAST_ROWS, :] = wd_ref[r0:r0 + CAST_ROWS, :].astype(BF16)

    def tile_step(j, carry):
        slot = (g0 + j) % 2
        in_copy(tile0 + j, slot).wait()

        @pl.when(j + 1 < n_t)
        def _():
            in_copy(tile0 + j + 1, 1 - slot).start()

        @pl.when(g0 + j >= 2)
        def _():
            out_copy(0, slot).wait()

        left = cnt - j * MOE_TM

        def ffn(row0, rows):
            lo, hi = _unpack_bf16_pairs(xbuf[slot, pl.ds(row0, rows)])
            valid = lax.broadcasted_iota(jnp.int32, lo.shape, 0) < left - row0
            x_lo = jnp.where(valid, lo, 0.0).astype(BF16)
            x_hi = jnp.where(valid, hi, 0.0).astype(BF16)

            def proj(c0):
                cols = slice(c0, c0 + MOE_FF_CHUNK)
                return (jnp.dot(x_lo, wgu_b[:half, cols], preferred_element_type=F32)
                        + jnp.dot(x_hi, wgu_b[half:, cols], preferred_element_type=F32)
                        + bgu_ref[:, cols])

            y = jnp.zeros((rows, d), F32)
            for f in range(d_ff // MOE_FF_CHUNK):
                c0 = f * MOE_FF_CHUNK
                gate = jnp.minimum(proj(c0), SWIGLU_LIMIT)
                up = jnp.clip(proj(d_ff + c0), -SWIGLU_LIMIT, SWIGLU_LIMIT)
                hh = (up + 1.0) * gate * _sigmoid(gate * SWIGLU_ALPHA)
                y = y + jnp.dot(hh.astype(BF16), wd_b[c0:c0 + MOE_FF_CHUNK, :],
                                preferred_element_type=F32)
            ybuf[slot, pl.ds(row0, rows)] = _pack_bf16_pairs(y + bdn_ref[...])

        n_sub = (jnp.minimum(left, MOE_TM) + MOE_SUB - 1) // MOE_SUB

        @pl.when(n_sub == MOE_TM // MOE_SUB)
        def _():
            ffn(0, MOE_TM)

        @pl.when(n_sub < MOE_TM // MOE_SUB)
        def _():
            def sub_step(i, c):
                ffn(pl.multiple_of(i * 2 * MOE_SUB, 2 * MOE_SUB), 2 * MOE_SUB)
                return c
            lax.fori_loop(0, n_sub // 2, sub_step, 0)

            @pl.when(n_sub % 2 == 1)
            def _():
                ffn(pl.multiple_of((n_sub // 2) * 2 * MOE_SUB, 2 * MOE_SUB), MOE_SUB)

        out_copy(tile0 + j, slot).start()
        return carry

    lax.fori_loop(0, n_t, tile_step, 0)

    g1 = g0 + n_t
    done_ref[0] = g1
    nxt = jnp.minimum(e + 1, n_e - 1)

    @pl.when(jnp.logical_and(e + 1 < n_e, cnt_ref[nxt] > 0))
    def _():
        in_copy(tile0_ref[nxt], g1 % 2).start()

    @pl.when(jnp.logical_and(e == n_e - 1, g1 >= 2))
    def _():
        out_copy(0, g1 % 2).wait()

    @pl.when(jnp.logical_and(e == n_e - 1, g1 >= 1))
    def _():
        out_copy(0, (g1 - 1) % 2).wait()


def _moe(xs, tile0, cnt, wgu, bgu, wd, bdn):
    p_rows, half = xs.shape
    n_e, d, gu = wgu.shape
    d_ff = wd.shape[1]
    w_blk = lambda e, tile0, cnt: (e, 0, 0)
    grid_spec = pltpu.PrefetchScalarGridSpec(
        num_scalar_prefetch=2,
        grid=(n_e,),
        in_specs=[pl.BlockSpec(memory_space=pl.ANY),
                  pl.BlockSpec((None, d, gu), w_blk),
                  pl.BlockSpec((None, 1, gu), w_blk),
                  pl.BlockSpec((None, d_ff, d), w_blk),
                  pl.BlockSpec((None, 1, d), w_blk)],
        out_specs=pl.BlockSpec(memory_space=pl.ANY),
        scratch_shapes=[pltpu.VMEM((d, gu), BF16), pltpu.VMEM((d_ff, d), BF16),
                        pltpu.VMEM((2, MOE_TM, half), jnp.uint32),
                        pltpu.VMEM((2, MOE_TM, half), jnp.uint32),
                        pltpu.SemaphoreType.DMA((2,)), pltpu.SemaphoreType.DMA((2,)),
                        pltpu.SMEM((1,), jnp.int32)])
    return pl.pallas_call(
        _moe_kernel,
        grid_spec=grid_spec,
        out_shape=jax.ShapeDtypeStruct((p_rows, half), jnp.uint32),
        compiler_params=pltpu.CompilerParams(dimension_semantics=("arbitrary",),
                                             vmem_limit_bytes=VMEM_LIMIT),
        name="moe_experts",
    )(tile0, cnt, xs, wgu, bgu, wd, bdn)


def _combine_kernel(h_ref, yg_ref, prob_ref, gf_ref, out_ref):
    tm = h_ref.shape[0]
    half = h_ref.shape[1] // 2
    p_rows = jnp.concatenate([prob_ref[...], jnp.zeros((8 - TOP_K, tm), F32)], axis=0)
    ph = p_rows.astype(BF16)
    pm, pl_ = _split(p_rows - ph.astype(F32))
    eye = (lax.broadcasted_iota(jnp.int32, (tm, tm), 0)
           == lax.broadcasted_iota(jnp.int32, (tm, tm), 1)).astype(BF16)
    nt = (((1,), (1,)), ((), ()))
    prob = sum(lax.dot_general(eye, piece, nt, preferred_element_type=F32)
               for piece in (ph, pm, pl_))
    acc_lo = jnp.zeros((tm, half), F32)
    acc_hi = jnp.zeros((tm, half), F32)
    for k in range(TOP_K):
        lo, hi = _unpack_bf16_pairs(yg_ref[k])
        pk = prob[:, k:k + 1]
        acc_lo = acc_lo + pk * lo
        acc_hi = acc_hi + pk * hi
    z = h_ref[...] + jnp.concatenate([acc_lo, acc_hi], axis=1)
    out_ref[...] = _rms(z, gf_ref[...])


def _combine(h, yg, row0, prob, gf, tm):
    n, d = h.shape
    tok = lambda w: pl.BlockSpec((tm, w), lambda i: (i, 0))
    t0 = row0 // tm
    return pl.pallas_call(
        _combine_kernel,
        grid=(n // tm,),
        in_specs=[tok(d), pl.BlockSpec((TOP_K, tm, d // 2), lambda i: (0, i + t0, 0)),
                  pl.BlockSpec((TOP_K, tm), lambda i: (0, i)), _full_spec(gf.shape)],
        out_specs=tok(d),
        out_shape=jax.ShapeDtypeStruct((n, d), F32),
        compiler_params=pltpu.CompilerParams(dimension_semantics=("parallel",),
                                             vmem_limit_bytes=VMEM_LIMIT),
        name="moe_combine",
    )(h, yg, prob, gf)


SC_CORES = 2
SC_SUBCORES = 16
SC_WORKERS = SC_CORES * SC_SUBCORES
SC_MAX_INDEX = 128


def _sc_chunk(rows_per_worker, limit=SC_MAX_INDEX):
    for c in range(limit, 7, -8):
        if rows_per_worker % c == 0:
            return c
    raise ValueError(f"no 8-aligned chunk divides {rows_per_worker} rows")


def _sc_mesh():
    return plsc.VectorSubcoreMesh(core_axis_name="c", subcore_axis_name="s")


def _sc_worker():
    return lax.axis_index("s") * SC_CORES + lax.axis_index("c")


def _scatter_rows(xs_in, pos_t, p_rows):
    w = xs_in[0].shape[1]
    dtype = xs_in[0].dtype
    n_k, n = pos_t.shape
    pos_flat = pos_t.reshape(n_k * n)
    parts, row0 = [], 0
    for x in xs_in:
        per = x.shape[0] // SC_WORKERS
        parts.append((row0, per, _sc_chunk(per, SC_MAX_INDEX // 2)))
        row0 += x.shape[0]
    max_chunk = max(c for _, _, c in parts)
    max_per = max(p for _, p, _ in parts)

    def body(*refs):
        x_hbms = refs[:len(xs_in)]
        pos_hbm, out_hbm, idx_v, rows_v, sem_l, sem_s = refs[len(xs_in):]
        for x_hbm, (tok0, per, chunk) in zip(x_hbms, parts):
            n_chunks = per // chunk
            base = pl.multiple_of(_sc_worker() * per, 8)
            for k in range(n_k):
                src = pl.multiple_of(k * n + tok0 + base, 8)
                pltpu.sync_copy(pos_hbm.at[pl.ds(src, per)], idx_v.at[pl.ds(k * per, per)])

            def load(c, slot):
                src = x_hbm.at[pl.ds(pl.multiple_of(base + c * chunk, 8), chunk)]
                return pltpu.make_async_copy(src, rows_v.at[slot, pl.ds(0, chunk)], sem_l.at[slot])

            def scatter(c, k, slot):
                rows = idx_v.at[pl.ds(pl.multiple_of(k * per + c * chunk, 8), chunk)]
                return pltpu.make_async_copy(rows_v.at[slot, pl.ds(0, chunk)], out_hbm.at[rows],
                                             sem_s.at[slot])

            load(0, 0).start()

            @pl.loop(0, n_chunks)
            def _(c):
                slot = c % 2
                load(c, slot).wait()

                @pl.when(c + 1 < n_chunks)
                def _():
                    @pl.when(c >= 1)
                    def _():
                        for k in range(n_k):
                            scatter(c - 1, k, 1 - slot).wait()
                    load(c + 1, 1 - slot).start()

                for k in range(n_k):
                    scatter(c, k, slot).start()

            if n_chunks >= 2:
                for k in range(n_k):
                    scatter(n_chunks - 2, k, n_chunks % 2).wait()
            for k in range(n_k):
                scatter(n_chunks - 1, k, (n_chunks - 1) % 2).wait()

    return pl.kernel(
        body, out_type=jax.ShapeDtypeStruct((p_rows, w), dtype), mesh=_sc_mesh(),
        scratch_types=[pltpu.VMEM((n_k * max_per,), jnp.int32),
                       pltpu.VMEM((2, max_chunk, w), dtype),
                       pltpu.SemaphoreType.DMA((2,)), pltpu.SemaphoreType.DMA((2,))],
        name="sc_scatter_rows")(*xs_in, pos_flat)


def _gather_rows(table, idx):
    n = idx.shape[0]
    w = table.shape[1]
    per = n // SC_WORKERS
    chunk = _sc_chunk(per)

    n_chunks = per // chunk

    def body(table_hbm, idx_hbm, out_hbm, idx_v, rows_v, sem_g, sem_w):
        base = pl.multiple_of(_sc_worker() * per, 8)
        pltpu.sync_copy(idx_hbm.at[pl.ds(base, per)], idx_v)

        def gather(c, slot):
            rows = idx_v.at[pl.ds(pl.multiple_of(c * chunk, 8), chunk)]
            return pltpu.make_async_copy(table_hbm.at[rows], rows_v.at[slot], sem_g.at[slot])

        def write(c, slot):
            dst = out_hbm.at[pl.ds(pl.multiple_of(base + c * chunk, 8), chunk)]
            return pltpu.make_async_copy(rows_v.at[slot], dst, sem_w.at[slot])

        gather(0, 0).start()

        @pl.loop(0, n_chunks)
        def _(c):
            slot = c % 2
            gather(c, slot).wait()

            @pl.when(c + 1 < n_chunks)
            def _():
                @pl.when(c >= 1)
                def _():
                    write(c - 1, 1 - slot).wait()
                gather(c + 1, 1 - slot).start()

            write(c, slot).start()

        if n_chunks >= 2:
            write(n_chunks - 2, n_chunks % 2).wait()
        write(n_chunks - 1, (n_chunks - 1) % 2).wait()

    return pl.kernel(
        body, out_type=jax.ShapeDtypeStruct((n, w), table.dtype), mesh=_sc_mesh(),
        scratch_types=[pltpu.VMEM((per,), jnp.int32), pltpu.VMEM((2, chunk, w), table.dtype),
                       pltpu.SemaphoreType.DMA((2,)), pltpu.SemaphoreType.DMA((2,))],
        name="sc_gather_rows")(table, idx)


def _mix_matrix(ws, block, seq_rows):
    causal = jnp.tril(jnp.ones((CHUNK, CHUNK), dtype=bool))
    w = jnp.where(causal[None], ws, 0.0)[:, :seq_rows, :seq_rows]
    eye = jnp.eye(block // seq_rows, dtype=ws.dtype)
    return jnp.einsum("ab,gts->gatbs", eye, w).reshape(ws.shape[0], block, block).astype(BF16)


TOKEN_TILE = 512


def _stream(x, shift_in, wkv_in, p, seq_rows):
    nb, t, d = x.shape
    n = nb * t
    tm = TOKEN_TILE
    x2d = x.reshape(n, d)
    rows = CHUNK if seq_rows is None else seq_rows
    wmix = _mix_matrix(p["w_spatial"], max(rows, LANES), rows)
    pos = jnp.arange(tm) % rows
    d_a = p["wu"].shape[1]
    bias_full = jnp.repeat(p["b_spatial"].T[pos], d_a // G_A, axis=1)
    apart, vn = _branch_a(x2d, tm, p["norm1_g"], p["wu"], p["wv"], p["wga"], p["vnorm_g"],
                          p["vnorm_b"], wmix, bias_full, p["w_a_out"])
    if seq_rows is None:
        xb, ext = x, jnp.zeros((1, p["wcur"].shape[1]), F32)
    else:
        xb = x2d.reshape(n // tm, tm, d)
        ext = jnp.repeat(shift_in, seq_rows, axis=0).reshape(n // tm, tm, -1)
    outs = _branch_b(xb, tm, seq_rows, ext, p["norm1_g"], p["wcur"], p["wgb"], p["mu_shift"],
                     p["w0"], p["w2"], p["a0"], p["a2"], p["g2"], p["k_k"], p["k_a"], p["r_k"],
                     p["bd"])
    r, k2, v, kk, kka, lw, g, bonus, sgb, cur = outs
    d_b = r.shape[-1]
    if seq_rows is None:
        shift_out = cur[:, -1, :]
        y, s_out = _scan(r, k2, v, kk, kka, lw)
        y2d = y.reshape(n, d_b)
    else:
        shift_out = cur.reshape(nb, t, -1)[:, -1, :]
        y2d, s_lanes = _scan_steps(*[a.reshape(n, d_b) for a in (r, k2, v, kk, kka, lw)],
                                   jnp.transpose(wkv_in, (1, 2, 3, 0)), t)
        s_out = jnp.transpose(s_lanes, (3, 0, 1, 2))
    flat = lambda a: a.reshape(n, a.shape[-1])
    routed = _merge(y2d, flat(g), flat(bonus), flat(sgb), apart, x2d, tm, p["lnx_g"],
                    p["lnx_b"], p["bd"], p["w_b_out"], p["w_out"], p["norm2_g"],
                    p["router_w"], p["router_b"])
    return routed, vn, shift_out, s_out


def kernel(x_prompt, x_sample, state_shift, state_wkv, norm1_g, w_in, mu_shift, vnorm_g, vnorm_b, w_spatial, b_spatial, w_a_out, w0, w2, a0, a2, g2, k_k, k_a, r_k, lnx_g, lnx_b, w_b_out, w_out, norm2_g, router_w, router_b, exp_w_gu, exp_b_gu, exp_w_down, exp_b_down, normf_g):
    assert w_in.shape[0] == 1, "one layer: the final norm is fused into its MoE combine"
    l = 0
    d_model = x_prompt.shape[-1]
    shift_w = mu_shift.shape[-1]
    d_a = vnorm_g.shape[-1]
    bp, tp, _ = x_prompt.shape
    bs, ts, _ = x_sample.shape
    head_id = jnp.arange(2 * LANES) // HEAD_B
    bd = (head_id[:, None] == head_id[None, :]).astype(BF16)
    row = lambda a: a.reshape(1, -1)

    wi = w_in[l].astype(BF16)
    o = shift_w
    p = dict(
        norm1_g=row(norm1_g[l]), wcur=wi[:, :o], wu=wi[:, o:o + d_a],
        wv=wi[:, o + d_a:o + 2 * d_a], wga=wi[:, o + 2 * d_a:o + 2 * d_a + d_model],
        wgb=wi[:, o + 2 * d_a + d_model:], mu_shift=row(mu_shift[l]),
        vnorm_g=row(vnorm_g[l]), vnorm_b=row(vnorm_b[l]), w_spatial=w_spatial[l],
        b_spatial=b_spatial[l], w_a_out=w_a_out[l].astype(BF16), w0=row(w0[l]),
        w2=_stack_rhs3(w2[l]), a0=row(a0[l]), a2=_stack_rhs3(a2[l]),
        g2=jnp.concatenate([g2[l].astype(BF16)] * 2, axis=0),
        k_k=row(k_k[l]), k_a=row(k_a[l]),
        r_k=row(r_k[l]), lnx_g=row(lnx_g[l]), lnx_b=row(lnx_b[l]),
        w_b_out=w_b_out[l].astype(BF16), w_out=w_out[l].astype(BF16),
        norm2_g=row(norm2_g[l]), router_w=jnp.concatenate(_split(router_w[l].T), axis=0),
        router_b=router_b[l].reshape(-1, 1), bd=bd)

    routed_p, _, sh_p, s_p = _stream(x_prompt, None, None, p, None)
    routed_s, vn_s, sh_s, s_s = _stream(x_sample, state_shift[l], state_wkv[l], p, ts)
    h_p, xp_p, topi_p, prob_p, rank_p, cnt_p = routed_p
    h_s, xp_s, topi_s, prob_s, rank_s, cnt_s = routed_s

    n_p, n_s = h_p.shape[0], h_s.shape[0]
    n_e = router_w.shape[-1]
    n_tiles = (n_p + n_s) * TOP_K // MOE_TM + n_e
    pos_t, tile0, cnt = _route(topi_p, rank_p, cnt_p, topi_s, rank_s, cnt_s)
    xs = _scatter_rows([xp_p, xp_s], pos_t, n_tiles * MOE_TM)
    ys = _moe(xs, tile0, cnt, exp_w_gu[l], exp_b_gu[l][:, None, :],
              exp_w_down[l], exp_b_down[l][:, None, :])
    yg = _gather_rows(ys, pos_t.reshape(-1)).reshape(TOP_K, n_p + n_s, d_model // 2)
    gf = row(normf_g)
    y_p = _combine(h_p, yg, 0, prob_p, gf, TOKEN_TILE).reshape(bp, tp, d_model)
    y_s = _combine(h_s, yg, n_p, prob_s, gf, TOKEN_TILE).reshape(bs, ts, d_model)
    return (y_p, y_s, sh_p[None], s_p[None], sh_s[None], s_s[None],
            vn_s.reshape(1, bs, ts, d_a))
```

```python
import functools
import math

import jax
import jax.numpy as jnp
from jax import lax
from jax.experimental import pallas as pl
from jax.experimental.pallas import tpu as pltpu
from jax.experimental.pallas import tpu_sc as plsc

F32 = jnp.float32
BF16 = jnp.bfloat16

CHUNK = 128
G_A = 8
HEAD_B = 64
R_W, R_A, R_G = 64, 64, 128
TOP_K = 4
SWIGLU_LIMIT = 7.0
SWIGLU_ALPHA = 1.702
EPS = 1e-5
GN_EPS = HEAD_B * 1e-5

LANES = 128
PAIR = 2 * HEAD_B
VMEM_LIMIT = 56 * 1024 * 1024


def _dot(a, b):
    return jnp.dot(a.astype(BF16), b.astype(BF16), preferred_element_type=F32)


def _split(x):
    hi = x.astype(BF16)
    lo = (x - hi.astype(F32)).astype(BF16)
    return hi, lo


def _stack_rhs3(b):
    bh, bl = _split(b)
    return jnp.concatenate([bh, bh, bl], axis=0)


def _dot3_stacked(a, b_stacked):
    ah, al = _split(a)
    return jnp.dot(jnp.concatenate([ah, al, ah], axis=1), b_stacked, preferred_element_type=F32)


def _dot2_stacked(a, b_stacked):
    ah, al = _split(a)
    return jnp.dot(jnp.concatenate([ah, al], axis=1), b_stacked, preferred_element_type=F32)


def _head_sums(x, ones_bd):
    w = ones_bd.shape[0]
    return jnp.concatenate([jnp.dot(x[:, c:c + w].astype(BF16), ones_bd,
                                    preferred_element_type=F32)
                            for c in range(0, x.shape[1], w)], axis=1)


def _dot_exact_lhs(a_bf16, b):
    bh, bl = _split(b)
    bm = b - bh.astype(F32) - bl.astype(F32)
    return (jnp.dot(a_bf16, bh, preferred_element_type=F32)
            + jnp.dot(a_bf16, bl, preferred_element_type=F32)
            + jnp.dot(a_bf16, bm.astype(BF16), preferred_element_type=F32))


def _rms(x, g):
    return x * lax.rsqrt(jnp.mean(x * x, axis=-1, keepdims=True) + EPS) * g


def _sigmoid(x):
    return 1.0 / (1.0 + jnp.exp(-x))


def _full_spec(shape):
    nd = len(shape)
    return pl.BlockSpec(shape, lambda *_: (0,) * nd, pipeline_mode=pl.Buffered(1))


def _branch_a_kernel(x_ref, g1_ref, wu_ref, wv_ref, wga_ref, vg_ref, vb_ref, wmix_ref,
                     bias_ref, wao_ref, apart_ref, vn_ref):
    xn = _rms(x_ref[...], g1_ref[...]).astype(BF16)
    v = jnp.dot(xn, wv_ref[...], preferred_element_type=F32)
    mu = jnp.mean(v, axis=-1, keepdims=True)
    d = v - mu
    var = jnp.mean(d * d, axis=-1, keepdims=True)
    vn = d * lax.rsqrt(var + EPS) * vg_ref[...] + vb_ref[...]
    vn_ref[...] = vn
    vnb = vn.astype(BF16)
    c_a = vnb.shape[1] // G_A
    rb = wmix_ref.shape[1]
    mixed = jnp.concatenate(
        [jnp.concatenate(
            [jnp.dot(wmix_ref[g], vnb[r0:r0 + rb, g * c_a:(g + 1) * c_a],
                     preferred_element_type=F32) for g in range(G_A)], axis=1)
         for r0 in range(0, vnb.shape[0], rb)], axis=0) + bias_ref[...]
    u = jnp.dot(xn, wu_ref[...], preferred_element_type=F32)
    ya = (u * mixed).astype(BF16)
    ga = jnp.dot(xn, wga_ref[...], preferred_element_type=F32)
    apart_ref[...] = _sigmoid(ga) * jnp.dot(ya, wao_ref[...], preferred_element_type=F32)


def _branch_a(x2d, tm, g1, wu, wv, wga, vg, vb, wmix, bias_full, wao):
    n, d = x2d.shape
    d_a = wu.shape[1]
    tok = lambda w: pl.BlockSpec((tm, w), lambda i: (i, 0))
    return pl.pallas_call(
        _branch_a_kernel,
        grid=(n // tm,),
        in_specs=[tok(d), _full_spec(g1.shape), _full_spec(wu.shape), _full_spec(wv.shape),
                  _full_spec(wga.shape), _full_spec(vg.shape), _full_spec(vb.shape),
                  _full_spec(wmix.shape), _full_spec(bias_full.shape), _full_spec(wao.shape)],
        out_specs=[tok(d), tok(d_a)],
        out_shape=[jax.ShapeDtypeStruct((n, d), F32), jax.ShapeDtypeStruct((n, d_a), F32)],
        compiler_params=pltpu.CompilerParams(dimension_semantics=("parallel",),
                                             vmem_limit_bytes=VMEM_LIMIT),
        name="branch_a",
    )(x2d, g1, wu, wv, wga, vg, vb, wmix, bias_full, wao)


def _branch_b_kernel(x_ref, g1_ref, wcur_ref, wgb_ref, mu_ref, ext_ref, w0_ref, w2_ref,
                     a0_ref, a2_ref, g2_ref, kk_ref, ka_ref, rk_ref, bd_ref,
                     r_out, k_out, v_out, kk_out, kka_out, lw_out, g_out, bonus_out,
                     sgb_out, cur_out, carry_scr, *, seq_rows, d_b):
    tm = x_ref.shape[0]
    xn = _rms(x_ref[...], g1_ref[...]).astype(BF16)
    cur = jnp.dot(xn, wcur_ref[...], preferred_element_type=F32)
    rolled = pltpu.roll(cur, 1, axis=0)
    row = lax.broadcasted_iota(jnp.int32, (tm, 1), 0)
    if seq_rows is None:
        first_tile = pl.program_id(1) == 0
        carry = jnp.where(first_tile, 0.0, carry_scr[...])
        prev = jnp.where(row == 0, carry, rolled)
        carry_scr[...] = cur[tm - 1:tm, :]
        cur_out[...] = cur[tm - 8:tm, :]
    else:
        prev = jnp.where(row % seq_rows == 0, ext_ref[...], rolled)
        cur_out[...] = cur
    xs = cur + (prev - cur) * mu_ref[...]
    r = xs[:, 0:d_b]
    k = xs[:, d_b:2 * d_b]
    v = xs[:, 2 * d_b:3 * d_b]
    o = 3 * d_b
    xw = xs[:, o:o + R_W]
    xa = xs[:, o + R_W:o + R_W + R_A]
    xg = xs[:, o + R_W + R_A:o + R_W + R_A + R_G]
    wl = w0_ref[...] + _dot3_stacked(jnp.tanh(xw), w2_ref[...])
    lw = -math.exp(-0.5) * _sigmoid(wl)
    a = _sigmoid(a0_ref[...] + _dot3_stacked(xa, a2_ref[...]))
    g = _dot2_stacked(_sigmoid(xg), g2_ref[...])
    bd = bd_ref[...]
    kkr = k * kk_ref[...]
    n2 = _head_sums(kkr * kkr, bd)
    kk = kkr * lax.rsqrt(jnp.maximum(n2, 1e-24))
    k2 = k * (1.0 + (a - 1.0) * ka_ref[...])
    bonus = _head_sums(r * k2 * rk_ref[...], bd) * v
    r_out[...] = r
    k_out[...] = k2
    v_out[...] = v
    kk_out[...] = kk
    kka_out[...] = kk * a
    lw_out[...] = lw
    g_out[...] = g
    bonus_out[...] = bonus
    sgb_out[...] = _sigmoid(jnp.dot(xn, wgb_ref[...], preferred_element_type=F32))


def _branch_b(x3d, tm, seq_rows, ext, g1, wcur, wgb, mu, w0, w2, a0, a2, g2, k_k, k_a, r_k, bd):
    nb, t, d = x3d.shape
    d_b = w0.shape[1]
    shift_w = wcur.shape[1]
    nt = t // tm
    tok = lambda w: pl.BlockSpec((None, tm, w), lambda b, i: (b, i, 0))
    cur_rows = 8 if seq_rows is None else tm
    outs = [jax.ShapeDtypeStruct((nb, t, d_b), F32)] * 8 + [
        jax.ShapeDtypeStruct((nb, t, d), F32),
        jax.ShapeDtypeStruct((nb, nt * cur_rows, shift_w), F32)]
    out_specs = [tok(d_b)] * 8 + [tok(d), pl.BlockSpec((None, cur_rows, shift_w),
                                                       lambda b, i: (b, i, 0))]
    weights = (g1, wcur, wgb, mu)
    small = (w0, w2, a0, a2, g2, k_k, k_a, r_k, bd)
    return pl.pallas_call(
        functools.partial(_branch_b_kernel, seq_rows=seq_rows, d_b=d_b),
        grid=(nb, nt),
        in_specs=[tok(d)] + [_full_spec(w.shape) for w in weights]
        + [tok(shift_w) if seq_rows is not None else _full_spec(ext.shape)]
        + [_full_spec(w.shape) for w in small],
        out_specs=out_specs,
        out_shape=outs,
        scratch_shapes=[pltpu.VMEM((1, shift_w), F32)],
        compiler_params=pltpu.CompilerParams(dimension_semantics=("parallel", "arbitrary"),
                                             vmem_limit_bytes=VMEM_LIMIT),
        name="branch_b",
    )(x3d, *weights, ext, *small)


def _scan_kernel(r_ref, k_ref, v_ref, kk_ref, kka_ref, lw_ref, y_ref, sout_ref, s_scr, *, chunk):
    L = chunk
    W2 = 2 * L
    c = pl.program_id(1)
    n_seq, n_pairs = s_scr.shape[:2]

    @pl.when(c == 0)
    def _():
        s_scr[...] = jnp.zeros_like(s_scr)

    iota = lambda shape, dim: lax.broadcasted_iota(jnp.int32, shape, dim)
    stack_mask = (iota((W2, 1), 0) >= L) == (iota((1, PAIR), 1) >= HEAD_B)
    bd_mask = (iota((W2, 1), 0) >= L) == (iota((1, W2), 1) >= L)
    assert L & (L - 1) == 0, "chunk length must be a power of two"
    rw = iota((L, W2), 0)
    cw = iota((L, W2), 1) & (L - 1)
    strict_w = cw < rw
    eye_w = (cw == rw).astype(F32)
    incl_w2 = (iota((L, 2 * W2), 1) & (L - 1)) <= iota((L, 2 * W2), 0)
    tri = (iota((L, L), 1) <= iota((L, L), 0)).astype(BF16)
    di = iota((PAIR, PAIR), 0)
    dj = iota((PAIR, PAIR), 1)
    diag = di == dj
    head_diag = (di >= HEAD_B) == (dj >= HEAD_B)

    def stack(x):
        return jnp.where(stack_mask, jnp.concatenate([x, x], axis=0), 0.0)

    def bd(xw):
        return jnp.where(bd_mask, jnp.concatenate([xw, xw], axis=0), 0.0)

    n_double = max(L.bit_length() - 2, 0)
    prep = []
    for s in range(n_seq):
        lw_all = lw_ref[s]
        cum_all = _dot_exact_lhs(tri, lw_all)
        for p in range(n_pairs):
            sl = slice(p * PAIR, (p + 1) * PAIR)
            cum = cum_all[:, sl]
            cum_last = cum[L - 1:L, :]
            e_neg = jnp.exp(-cum)
            e_rel = jnp.exp(cum_last - cum)
            kk = kk_ref[s, :, sl]
            kka = kka_ref[s, :, sl]
            kx = k_ref[s, :, sl]
            prep.append(dict(
                s=s, p=p, sl=sl, ab=-kk * jnp.exp(cum - lw_all[:, sl]),
                rb=r_ref[s, :, sl] * jnp.exp(cum), bb=kka * e_neg, kb=kx * e_neg,
                bt=kka * e_rel, kt=kx * e_rel, v=v_ref[s, :, sl], decay=jnp.exp(cum_last)))
    a_w = [lax.dot_general(
        jnp.concatenate([q["ab"], q["rb"]], axis=0).astype(BF16),
        jnp.concatenate([stack(q["bb"]), stack(q["kb"])], axis=0).astype(BF16),
        (((1,), (1,)), ((), ())), preferred_element_type=F32) for q in prep]
    nw = [jnp.where(strict_w, a[:L, :W2], 0.0) for a in a_w]
    tw = [eye_w + n for n in nw]
    if n_double:
        sw = [_dot(n, bd(n)) for n in nw]
        for _ in range(n_double - 1):
            prod = [_dot(jnp.concatenate([t, s], axis=0), bd(s)) for t, s in zip(tw, sw)]
            tw = [t + p[:L] for t, p in zip(tw, prod)]
            sw = [p[L:] for p in prod]
        tw = [t + _dot(t, bd(s)) for t, s in zip(tw, sw)]
    akv = [_dot(jnp.where(strict_w, a[:L, W2:], 0.0), stack(q["v"])) for a, q in zip(a_w, prep)]
    tx = [_dot(t, jnp.concatenate([stack(q["ab"]), stack(kv)], axis=1))
          for t, q, kv in zip(tw, prep, akv)]
    ry = [_dot(jnp.where(incl_w2, a[L:], 0.0),
               jnp.concatenate(
                   [jnp.concatenate([stack(x[:, :PAIR]), stack(x[:, PAIR:])], axis=1),
                    jnp.concatenate([jnp.zeros((W2, PAIR), F32), stack(q["v"])], axis=1)], axis=0))
          for a, x, q in zip(a_w, tx, prep)]
    for q, x, y in zip(prep, tx, ry):
        st = s_scr[q["s"], q["p"]]
        us = _dot(jnp.concatenate([x[:, :PAIR], q["rb"] + y[:, :PAIR]], axis=0), st)
        y_ref[q["s"], :, q["sl"]] = us[L:] + y[:, PAIR:]
        u = us[:L] + x[:, PAIR:]
        lhs = jnp.concatenate([jnp.where(diag, q["decay"], 0.0),
                               jnp.concatenate([q["bt"], q["kt"]], axis=0).T], axis=1)
        g = _dot(lhs, jnp.concatenate([st, u, q["v"]], axis=0))
        s_scr[q["s"], q["p"]] = jnp.where(head_diag, g, 0.0)

    @pl.when(c == pl.num_programs(1) - 1)
    def _():
        for s in range(n_seq):
            for p in range(n_pairs):
                nat = s_scr[s, p].T
                sout_ref[s, 2 * p] = nat[:HEAD_B, :HEAD_B]
                sout_ref[s, 2 * p + 1] = nat[HEAD_B:, HEAD_B:]


SCAN_CHUNK = 64
SCAN_SEQS = 4


def _scan(r, k, v, kk, kka, lw):
    nb, t, d_b = r.shape
    n_pairs = d_b // PAIR
    n_heads = d_b // HEAD_B
    tok = pl.BlockSpec((SCAN_SEQS, SCAN_CHUNK, d_b), lambda b, c: (b, c, 0))
    st = pl.BlockSpec((SCAN_SEQS, n_heads, HEAD_B, HEAD_B), lambda b, c: (b, 0, 0, 0))
    return pl.pallas_call(
        functools.partial(_scan_kernel, chunk=SCAN_CHUNK),
        grid=(nb // SCAN_SEQS, t // SCAN_CHUNK),
        in_specs=[tok] * 6,
        out_specs=[tok, st],
        out_shape=[jax.ShapeDtypeStruct((nb, t, d_b), F32),
                   jax.ShapeDtypeStruct((nb, n_heads, HEAD_B, HEAD_B), F32)],
        scratch_shapes=[pltpu.VMEM((SCAN_SEQS, n_pairs, PAIR, PAIR), F32)],
        compiler_params=pltpu.CompilerParams(dimension_semantics=("parallel", "arbitrary"),
                                             vmem_limit_bytes=VMEM_LIMIT),
        name="rwkv_scan",
    )(r, k, v, kk, kka, lw)


STEP_ROWS = 8


def _steps_kernel(r_ref, k_ref, v_ref, kk_ref, kka_ref, lw_ref, s0_ref, y_ref, s_ref,
                  vec_scr, yt_scr, *, n_steps):
    n_seq = s_ref.shape[-1]
    s_ref[...] = s0_ref[...]
    for t in range(n_steps):
        rows = pl.ds(t, n_seq, stride=n_steps)
        vecs = {}
        for slot, (name, ref) in enumerate((("r", r_ref), ("k", k_ref), ("v", v_ref),
                                            ("kk", kk_ref), ("kka", kka_ref), ("lw", lw_ref))):
            vecs[name] = ref[rows, :].T
            vec_scr[slot] = vecs[name]
        for q in range(PAIR // HEAD_B):
            hs = slice(q * HEAD_B, (q + 1) * HEAD_B)
            r_q, k_q, kka_q = vecs["r"][hs], vecs["k"][hs], vecs["kka"][hs]
            nkk_q = -vecs["kk"][hs]
            w_q = jnp.exp(vecs["lw"][hs])

            def rows_step(i0, carry):
                for u in range(STEP_ROWS):
                    i = i0 * STEP_ROWS + u
                    s_row = s_ref[q, i]
                    sa = jnp.sum(s_row * nkk_q, axis=0, keepdims=True)
                    v_i = vec_scr[2, pl.ds(q * HEAD_B + i, 1), :]
                    s_new = s_row * w_q + sa * kka_q + v_i * k_q
                    s_ref[q, i] = s_new
                    yt_scr[pl.ds(q * HEAD_B + i, 1), :] = jnp.sum(s_new * r_q, axis=0,
                                                                  keepdims=True)
                return carry

            lax.fori_loop(0, HEAD_B // STEP_ROWS, rows_step, 0)
        y_ref[rows, :] = yt_scr[...].T


def _scan_steps(r, k, v, kk, kka, lw, s0, n_steps):
    n, d_b = r.shape
    n_heads, _, _, n_seq = s0.shape
    tok = pl.BlockSpec((n, PAIR), lambda p: (0, p))
    st = pl.BlockSpec((PAIR // HEAD_B, HEAD_B, HEAD_B, n_seq), lambda p: (p, 0, 0, 0))
    return pl.pallas_call(
        functools.partial(_steps_kernel, n_steps=n_steps),
        grid=(d_b // PAIR,),
        in_specs=[tok] * 6 + [st],
        out_specs=[tok, st],
        out_shape=[jax.ShapeDtypeStruct((n, d_b), F32), jax.ShapeDtypeStruct(s0.shape, F32)],
        scratch_shapes=[pltpu.VMEM((6, PAIR, n_seq), F32), pltpu.VMEM((PAIR, n_seq), F32)],
        compiler_params=pltpu.CompilerParams(dimension_semantics=("parallel",),
                                             vmem_limit_bytes=VMEM_LIMIT),
        name="rwkv_steps",
    )(r, k, v, kk, kka, lw, s0)


def _pack_bf16_pairs(x):
    w = x.shape[1] // 2
    bits = lambda v: lax.bitcast_convert_type(v.astype(BF16).astype(F32), jnp.uint32)
    return (bits(x[:, :w]) >> 16) | (bits(x[:, w:]) & jnp.uint32(0xFFFF0000))


def _unpack_bf16_pairs(u):
    lo = lax.bitcast_convert_type(u << 16, F32)
    hi = lax.bitcast_convert_type(u & jnp.uint32(0xFFFF0000), F32)
    return lo, hi


def _merge_kernel(y_ref, g_ref, bonus_ref, sgb_ref, apart_ref, x_ref, lg_ref, lb_ref, bd_ref,
                  wbo_ref, wo_ref, g2_ref, rwt_ref, rb_ref,
                  h_out, xp_out, topi_out, prob_out, rank_out, cnt_out, cnt_scr):
    step = pl.program_id(0)

    @pl.when(step == 0)
    def _():
        cnt_scr[...] = jnp.zeros_like(cnt_scr)

    bd = bd_ref[...]
    y = y_ref[...]
    inv_n = 1.0 / HEAD_B
    mu = _head_sums(y, bd) * inv_n
    d = y - mu
    var = _head_sums(d * d, bd) * inv_n
    yn = d * lax.rsqrt(var + GN_EPS) * lg_ref[...] + lb_ref[...]
    yb = ((yn + bonus_ref[...]) * g_ref[...]).astype(BF16)
    merged = apart_ref[...] + sgb_ref[...] * jnp.dot(yb, wbo_ref[...], preferred_element_type=F32)
    h = x_ref[...] + jnp.dot(merged.astype(BF16), wo_ref[...], preferred_element_type=F32)
    h_out[...] = h
    xn2 = _rms(h, g2_ref[...])
    xp_out[...] = _pack_bf16_pairs(xn2)
    n_e = rb_ref.shape[0]
    tm = xn2.shape[0]
    xh, xl = _split(xn2)
    rwt = rwt_ref[...]
    nt = (((1,), (1,)), ((), ()))
    t = lax.dot_general(rwt, xh, nt, preferred_element_type=F32)
    logits = (t[:n_e] + t[n_e:] + lax.dot_general(rwt[:n_e], xl, nt, preferred_element_type=F32)
              + rb_ref[...])
    idx = lax.broadcasted_iota(jnp.int32, logits.shape, 0).astype(F32)
    work = logits
    tops, hots, sels = [], [], []
    for _ in range(TOP_K):
        m = jnp.max(work, axis=0, keepdims=True)
        sel = jnp.min(jnp.where(work == m, idx, float(n_e)), axis=0, keepdims=True)
        hot = idx == sel
        tops.append(m)
        hots.append(hot)
        sels.append(sel)
        work = jnp.where(hot, -jnp.inf, work)
    es = [jnp.exp(t - tops[0]) for t in tops]
    denom = es[0] + es[1] + es[2] + es[3]
    topi_out[...] = jnp.concatenate(sels, axis=0).astype(jnp.int32)
    prob_out[...] = jnp.concatenate([e / denom for e in es], axis=0)
    hot_any = jnp.zeros_like(logits)
    for hot in hots:
        hot_any = hot_any + hot.astype(F32)
    ri = lax.broadcasted_iota(jnp.int32, (tm, tm), 0)
    ci = lax.broadcasted_iota(jnp.int32, (tm, tm), 1)
    before = _dot(hot_any, (ri < ci).astype(BF16)) + cnt_scr[...]
    rank_out[...] = jnp.concatenate(
        [jnp.sum(jnp.where(hot, before, 0.0), axis=0, keepdims=True) for hot in hots],
        axis=0).astype(jnp.int32)
    total = cnt_scr[...] + jnp.sum(hot_any, axis=1, keepdims=True)
    cnt_scr[...] = total
    cnt_out[...] = total.astype(jnp.int32)


def _merge(y, g, bonus, sgb, apart, x2d, tm, lnx_g, lnx_b, bd, wbo, wo, g2n, rw, rb):
    n, d = x2d.shape
    d_b = y.shape[1]
    n_e = rb.shape[0]
    tok = lambda w: pl.BlockSpec((tm, w), lambda i: (i, 0))
    per_k = pl.BlockSpec((TOP_K, tm), lambda i: (0, i))
    weights = (lnx_g, lnx_b, bd, wbo, wo, g2n, rw, rb)
    return pl.pallas_call(
        _merge_kernel,
        grid=(n // tm,),
        in_specs=[tok(d_b), tok(d_b), tok(d_b), tok(d), tok(d), tok(d)]
        + [_full_spec(w.shape) for w in weights],
        out_specs=[tok(d), tok(d // 2), per_k, per_k, per_k,
                   pl.BlockSpec((n_e, 1), lambda i: (0, 0))],
        out_shape=[jax.ShapeDtypeStruct((n, d), F32),
                   jax.ShapeDtypeStruct((n, d // 2), jnp.uint32),
                   jax.ShapeDtypeStruct((TOP_K, n), jnp.int32),
                   jax.ShapeDtypeStruct((TOP_K, n), F32),
                   jax.ShapeDtypeStruct((TOP_K, n), jnp.int32),
                   jax.ShapeDtypeStruct((n_e, 1), jnp.int32)],
        scratch_shapes=[pltpu.VMEM((n_e, 1), F32)],
        compiler_params=pltpu.CompilerParams(dimension_semantics=("arbitrary",),
                                             vmem_limit_bytes=VMEM_LIMIT),
        name="merge_router",
    )(y, g, bonus, sgb, apart, x2d, *weights)


MOE_TM = 1024
MOE_SUB = 128
MOE_FF_CHUNK = 512
CAST_ROWS = 256


def _route(topi_p, rank_p, cnt_p, topi_s, rank_s, cnt_s):
    n_e = cnt_p.size
    cnt_p, cnt_s = cnt_p.reshape(n_e), cnt_s.reshape(n_e)
    cnt = cnt_p + cnt_s
    padded = (cnt + MOE_TM - 1) // MOE_TM * MOE_TM
    ends = jnp.cumsum(padded)
    base = ends - padded
    lookup = lambda table, idx: jnp.sum(
        jnp.where(idx[..., None] == jnp.arange(n_e), table, 0), axis=-1)
    pos_t = jnp.concatenate([lookup(base, topi_p) + rank_p,
                             lookup(base + cnt_p, topi_s) + rank_s], axis=1)
    return (pos_t.astype(jnp.int32), (base // MOE_TM).astype(jnp.int32),
            cnt.astype(jnp.int32))


def _moe_kernel(tile0_ref, cnt_ref, xs_hbm, wgu_ref, bgu_ref, wd_ref, bdn_ref, ys_hbm,
                wgu_b, wd_b, xbuf, ybuf, sem_in, sem_out, done_ref):
    e = pl.program_id(0)
    n_e = pl.num_programs(0)

    @pl.when(e == 0)
    def _():
        done_ref[0] = 0

    g0 = done_ref[0]
    tile0 = tile0_ref[e]
    cnt = cnt_ref[e]
    n_t = (cnt + MOE_TM - 1) // MOE_TM
    d, gu = wgu_ref.shape
    d_ff = wd_ref.shape[0]
    half = d // 2

    def in_copy(tile, slot):
        return pltpu.make_async_copy(xs_hbm.at[pl.ds(tile * MOE_TM, MOE_TM)], xbuf.at[slot],
                                     sem_in.at[slot])

    def out_copy(tile, slot):
        return pltpu.make_async_copy(ybuf.at[slot], ys_hbm.at[pl.ds(tile * MOE_TM, MOE_TM)],
                                     sem_out.at[slot])

    @pl.when(jnp.logical_and(e == 0, n_t > 0))
    def _():
        in_copy(tile0, 0).start()

    @pl.when(n_t > 0)
    def _():
        for r0 in range(0, d, CAST_ROWS):
            wgu_b[r0:r0 + CAST_ROWS, :] = wgu_ref[r0:r0 + CAST_ROWS, :].astype(BF16)
        for r0 in range(0, d_ff, CAST_ROWS):
            wd_b[r0:r0 + CAST_ROWS, :] = wd_ref[r0:r0 + CAST_ROWS, :].astype(BF16)

    def tile_step(j, carry):
        slot = (g0 + j) % 2
        in_copy(tile0 + j, slot).wait()

        @pl.when(j + 1 < n_t)
        def _():
            in_copy(tile0 + j + 1, 1 - slot).start()

        @pl.when(g0 + j >= 2)
        def _():
            out_copy(0, slot).wait()

        left = cnt - j * MOE_TM

        def ffn(row0, rows):
            lo, hi = _unpack_bf16_pairs(xbuf[slot, pl.ds(row0, rows)])
            valid = lax.broadcasted_iota(jnp.int32, lo.shape, 0) < left - row0
            x_lo = jnp.where(valid, lo, 0.0).astype(BF16)
            x_hi = jnp.where(valid, hi, 0.0).astype(BF16)

            def proj(c0):
                cols = slice(c0, c0 + MOE_FF_CHUNK)
                return (jnp.dot(x_lo, wgu_b[:half, cols], preferred_element_type=F32)
                        + jnp.dot(x_hi, wgu_b[half:, cols], preferred_element_type=F32)
                        + bgu_ref[:, cols])

            y = jnp.zeros((rows, d), F32)
            for f in range(d_ff // MOE_FF_CHUNK):
                c0 = f * MOE_FF_CHUNK
                gate = jnp.minimum(proj(c0), SWIGLU_LIMIT)
                up = jnp.clip(proj(d_ff + c0), -SWIGLU_LIMIT, SWIGLU_LIMIT)
                hh = (up + 1.0) * gate * _sigmoid(gate * SWIGLU_ALPHA)
                y = y + jnp.dot(hh.astype(BF16), wd_b[c0:c0 + MOE_FF_CHUNK, :],
                                preferred_element_type=F32)
            ybuf[slot, pl.ds(row0, rows)] = _pack_bf16_pairs(y + bdn_ref[...])

        n_sub = (jnp.minimum(left, MOE_TM) + MOE_SUB - 1) // MOE_SUB

        @pl.when(n_sub == MOE_TM // MOE_SUB)
        def _():
            ffn(0, MOE_TM)

        @pl.when(n_sub < MOE_TM // MOE_SUB)
        def _():
            def sub_step(i, c):
                ffn(pl.multiple_of(i * 2 * MOE_SUB, 2 * MOE_SUB), 2 * MOE_SUB)
                return c
            lax.fori_loop(0, n_sub // 2, sub_step, 0)

            @pl.when(n_sub % 2 == 1)
            def _():
                ffn(pl.multiple_of((n_sub // 2) * 2 * MOE_SUB, 2 * MOE_SUB), MOE_SUB)

        out_copy(tile0 + j, slot).start()
        return carry

    lax.fori_loop(0, n_t, tile_step, 0)

    g1 = g0 + n_t
    done_ref[0] = g1
    nxt = jnp.minimum(e + 1, n_e - 1)

    @pl.when(jnp.logical_and(e + 1 < n_e, cnt_ref[nxt] > 0))
    def _():
        in_copy(tile0_ref[nxt], g1 % 2).start()

    @pl.when(jnp.logical_and(e == n_e - 1, g1 >= 2))
    def _():
        out_copy(0, g1 % 2).wait()

    @pl.when(jnp.logical_and(e == n_e - 1, g1 >= 1))
    def _():
        out_copy(0, (g1 - 1) % 2).wait()


def _moe(xs, tile0, cnt, wgu, bgu, wd, bdn):
    p_rows, half = xs.shape
    n_e, d, gu = wgu.shape
    d_ff = wd.shape[1]
    w_blk = lambda e, tile0, cnt: (e, 0, 0)
    grid_spec = pltpu.PrefetchScalarGridSpec(
        num_scalar_prefetch=2,
        grid=(n_e,),
        in_specs=[pl.BlockSpec(memory_space=pl.ANY),
                  pl.BlockSpec((None, d, gu), w_blk),
                  pl.BlockSpec((None, 1, gu), w_blk),
                  pl.BlockSpec((None, d_ff, d), w_blk),
                  pl.BlockSpec((None, 1, d), w_blk)],
        out_specs=pl.BlockSpec(memory_space=pl.ANY),
        scratch_shapes=[pltpu.VMEM((d, gu), BF16), pltpu.VMEM((d_ff, d), BF16),
                        pltpu.VMEM((2, MOE_TM, half), jnp.uint32),
                        pltpu.VMEM((2, MOE_TM, half), jnp.uint32),
                        pltpu.SemaphoreType.DMA((2,)), pltpu.SemaphoreType.DMA((2,)),
                        pltpu.SMEM((1,), jnp.int32)])
    return pl.pallas_call(
        _moe_kernel,
        grid_spec=grid_spec,
        out_shape=jax.ShapeDtypeStruct((p_rows, half), jnp.uint32),
        compiler_params=pltpu.CompilerParams(dimension_semantics=("arbitrary",),
                                             vmem_limit_bytes=VMEM_LIMIT),
        name="moe_experts",
    )(tile0, cnt, xs, wgu, bgu, wd, bdn)


def _combine_kernel(h_ref, yg_ref, prob_ref, gf_ref, out_ref):
    tm = h_ref.shape[0]
    half = h_ref.shape[1] // 2
    p_rows = jnp.concatenate([prob_ref[...], jnp.zeros((8 - TOP_K, tm), F32)], axis=0)
    ph = p_rows.astype(BF16)
    pm, pl_ = _split(p_rows - ph.astype(F32))
    eye = (lax.broadcasted_iota(jnp.int32, (tm, tm), 0)
           == lax.broadcasted_iota(jnp.int32, (tm, tm), 1)).astype(BF16)
    nt = (((1,), (1,)), ((), ()))
    prob = sum(lax.dot_general(eye, piece, nt, preferred_element_type=F32)
               for piece in (ph, pm, pl_))
    acc_lo = jnp.zeros((tm, half), F32)
    acc_hi = jnp.zeros((tm, half), F32)
    for k in range(TOP_K):
        lo, hi = _unpack_bf16_pairs(yg_ref[k])
        pk = prob[:, k:k + 1]
        acc_lo = acc_lo + pk * lo
        acc_hi = acc_hi + pk * hi
    z = h_ref[...] + jnp.concatenate([acc_lo, acc_hi], axis=1)
    out_ref[...] = _rms(z, gf_ref[...])


def _combine(h, yg, row0, prob, gf, tm):
    n, d = h.shape
    tok = lambda w: pl.BlockSpec((tm, w), lambda i: (i, 0))
    t0 = row0 // tm
    return pl.pallas_call(
        _combine_kernel,
        grid=(n // tm,),
        in_specs=[tok(d), pl.BlockSpec((TOP_K, tm, d // 2), lambda i: (0, i + t0, 0)),
                  pl.BlockSpec((TOP_K, tm), lambda i: (0, i)), _full_spec(gf.shape)],
        out_specs=tok(d),
        out_shape=jax.ShapeDtypeStruct((n, d), F32),
        compiler_params=pltpu.CompilerParams(dimension_semantics=("parallel",),
                                             vmem_limit_bytes=VMEM_LIMIT),
        name="moe_combine",
    )(h, yg, prob, gf)


SC_CORES = 2
SC_SUBCORES = 16
SC_WORKERS = SC_CORES * SC_SUBCORES
SC_MAX_INDEX = 128


def _sc_chunk(rows_per_worker, limit=SC_MAX_INDEX):
    for c in range(limit, 7, -8):
        if rows_per_worker % c == 0:
            return c
    raise ValueError(f"no 8-aligned chunk divides {rows_per_worker} rows")


def _sc_mesh():
    return plsc.VectorSubcoreMesh(core_axis_name="c", subcore_axis_name="s")


def _sc_worker():
    return lax.axis_index("s") * SC_CORES + lax.axis_index("c")


def _scatter_rows(xs_in, pos_t, p_rows):
    w = xs_in[0].shape[1]
    dtype = xs_in[0].dtype
    n_k, n = pos_t.shape
    pos_flat = pos_t.reshape(n_k * n)
    parts, row0 = [], 0
    for x in xs_in:
        per = x.shape[0] // SC_WORKERS
        parts.append((row0, per, _sc_chunk(per, SC_MAX_INDEX // 2)))
        row0 += x.shape[0]
    max_chunk = max(c for _, _, c in parts)
    max_per = max(p for _, p, _ in parts)

    def body(*refs):
        x_hbms = refs[:len(xs_in)]
        pos_hbm, out_hbm, idx_v, rows_v, sem_l, sem_s = refs[len(xs_in):]
        for x_hbm, (tok0, per, chunk) in zip(x_hbms, parts):
            n_chunks = per // chunk
            base = pl.multiple_of(_sc_worker() * per, 8)
            for k in range(n_k):
                src = pl.multiple_of(k * n + tok0 + base, 8)
                pltpu.sync_copy(pos_hbm.at[pl.ds(src, per)], idx_v.at[pl.ds(k * per, per)])

            def load(c, slot):
                src = x_hbm.at[pl.ds(pl.multiple_of(base + c * chunk, 8), chunk)]
                return pltpu.make_async_copy(src, rows_v.at[slot, pl.ds(0, chunk)], sem_l.at[slot])

            def scatter(c, k, slot):
                rows = idx_v.at[pl.ds(pl.multiple_of(k * per + c * chunk, 8), chunk)]
                return pltpu.make_async_copy(rows_v.at[slot, pl.ds(0, chunk)], out_hbm.at[rows],
                                             sem_s.at[slot])

            load(0, 0).start()

            @pl.loop(0, n_chunks)
            def _(c):
                slot = c % 2
                load(c, slot).wait()

                @pl.when(c + 1 < n_chunks)
                def _():
                    @pl.when(c >= 1)
                    def _():
                        for k in range(n_k):
                            scatter(c - 1, k, 1 - slot).wait()
                    load(c + 1, 1 - slot).start()

                for k in range(n_k):
                    scatter(c, k, slot).start()

            if n_chunks >= 2:
                for k in range(n_k):
                    scatter(n_chunks - 2, k, n_chunks % 2).wait()
            for k in range(n_k):
                scatter(n_chunks - 1, k, (n_chunks - 1) % 2).wait()

    return pl.kernel(
        body, out_type=jax.ShapeDtypeStruct((p_rows, w), dtype), mesh=_sc_mesh(),
        scratch_types=[pltpu.VMEM((n_k * max_per,), jnp.int32),
                       pltpu.VMEM((2, max_chunk, w), dtype),
                       pltpu.SemaphoreType.DMA((2,)), pltpu.SemaphoreType.DMA((2,))],
        name="sc_scatter_rows")(*xs_in, pos_flat)


def _gather_rows(table, idx):
    n = idx.shape[0]
    w = table.shape[1]
    per = n // SC_WORKERS
    chunk = _sc_chunk(per)

    n_chunks = per // chunk

    def body(table_hbm, idx_hbm, out_hbm, idx_v, rows_v, sem_g, sem_w):
        base = pl.multiple_of(_sc_worker() * per, 8)
        pltpu.sync_copy(idx_hbm.at[pl.ds(base, per)], idx_v)

        def gather(c, slot):
            rows = idx_v.at[pl.ds(pl.multiple_of(c * chunk, 8), chunk)]
            return pltpu.make_async_copy(table_hbm.at[rows], rows_v.at[slot], sem_g.at[slot])

        def write(c, slot):
            dst = out_hbm.at[pl.ds(pl.multiple_of(base + c * chunk, 8), chunk)]
            return pltpu.make_async_copy(rows_v.at[slot], dst, sem_w.at[slot])

        gather(0, 0).start()

        @pl.loop(0, n_chunks)
        def _(c):
            slot = c % 2
            gather(c, slot).wait()

            @pl.when(c + 1 < n_chunks)
            def _():
                @pl.when(c >= 1)
                def _():
                    write(c - 1, 1 - slot).wait()
                gather(c + 1, 1 - slot).start()

            write(c, slot).start()

        if n_chunks >= 2:
            write(n_chunks - 2, n_chunks % 2).wait()
        write(n_chunks - 1, (n_chunks - 1) % 2).wait()

    return pl.kernel(
        body, out_type=jax.ShapeDtypeStruct((n, w), table.dtype), mesh=_sc_mesh(),
        scratch_types=[pltpu.VMEM((per,), jnp.int32), pltpu.VMEM((2, chunk, w), table.dtype),
                       pltpu.SemaphoreType.DMA((2,)), pltpu.SemaphoreType.DMA((2,))],
        name="sc_gather_rows")(table, idx)


def _mix_matrix(ws, block, seq_rows):
    causal = jnp.tril(jnp.ones((CHUNK, CHUNK), dtype=bool))
    w = jnp.where(causal[None], ws, 0.0)[:, :seq_rows, :seq_rows]
    eye = jnp.eye(block // seq_rows, dtype=ws.dtype)
    return jnp.einsum("ab,gts->gatbs", eye, w).reshape(ws.shape[0], block, block).astype(BF16)


TOKEN_TILE = 512


def _stream(x, shift_in, wkv_in, p, seq_rows):
    nb, t, d = x.shape
    n = nb * t
    tm = TOKEN_TILE
    x2d = x.reshape(n, d)
    rows = CHUNK if seq_rows is None else seq_rows
    wmix = _mix_matrix(p["w_spatial"], max(rows, LANES), rows)
    pos = jnp.arange(tm) % rows
    d_a = p["wu"].shape[1]
    bias_full = jnp.repeat(p["b_spatial"].T[pos], d_a // G_A, axis=1)
    apart, vn = _branch_a(x2d, tm, p["norm1_g"], p["wu"], p["wv"], p["wga"], p["vnorm_g"],
                          p["vnorm_b"], wmix, bias_full, p["w_a_out"])
    if seq_rows is None:
        xb, ext = x, jnp.zeros((1, p["wcur"].shape[1]), F32)
    else:
        xb = x2d.reshape(n // tm, tm, d)
        ext = jnp.repeat(shift_in, seq_rows, axis=0).reshape(n // tm, tm, -1)
    outs = _branch_b(xb, tm, seq_rows, ext, p["norm1_g"], p["wcur"], p["wgb"], p["mu_shift"],
                     p["w0"], p["w2"], p["a0"], p["a2"], p["g2"], p["k_k"], p["k_a"], p["r_k"],
                     p["bd"])
    r, k2, v, kk, kka, lw, g, bonus, sgb, cur = outs
    d_b = r.shape[-1]
    if seq_rows is None:
        shift_out = cur[:, -1, :]
        y, s_out = _scan(r, k2, v, kk, kka, lw)
        y2d = y.reshape(n, d_b)
    else:
        shift_out = cur.reshape(nb, t, -1)[:, -1, :]
        y2d, s_lanes = _scan_steps(*[a.reshape(n, d_b) for a in (r, k2, v, kk, kka, lw)],
                                   jnp.transpose(wkv_in, (1, 2, 3, 0)), t)
        s_out = jnp.transpose(s_lanes, (3, 0, 1, 2))
    flat = lambda a: a.reshape(n, a.shape[-1])
    routed = _merge(y2d, flat(g), flat(bonus), flat(sgb), apart, x2d, tm, p["lnx_g"],
                    p["lnx_b"], p["bd"], p["w_b_out"], p["w_out"], p["norm2_g"],
                    p["router_w"], p["router_b"])
    return routed, vn, shift_out, s_out


def kernel(x_prompt, x_sample, state_shift, state_wkv, norm1_g, w_in, mu_shift, vnorm_g, vnorm_b, w_spatial, b_spatial, w_a_out, w0, w2, a0, a2, g2, k_k, k_a, r_k, lnx_g, lnx_b, w_b_out, w_out, norm2_g, router_w, router_b, exp_w_gu, exp_b_gu, exp_w_down, exp_b_down, normf_g):
    assert w_in.shape[0] == 1, "one layer: the final norm is fused into its MoE combine"
    l = 0
    d_model = x_prompt.shape[-1]
    shift_w = mu_shift.shape[-1]
    d_a = vnorm_g.shape[-1]
    bp, tp, _ = x_prompt.shape
    bs, ts, _ = x_sample.shape
    head_id = jnp.arange(2 * LANES) // HEAD_B
    bd = (head_id[:, None] == head_id[None, :]).astype(BF16)
    row = lambda a: a.reshape(1, -1)

    wi = w_in[l].astype(BF16)
    o = shift_w
    p = dict(
        norm1_g=row(norm1_g[l]), wcur=wi[:, :o], wu=wi[:, o:o + d_a],
        wv=wi[:, o + d_a:o + 2 * d_a], wga=wi[:, o + 2 * d_a:o + 2 * d_a + d_model],
        wgb=wi[:, o + 2 * d_a + d_model:], mu_shift=row(mu_shift[l]),
        vnorm_g=row(vnorm_g[l]), vnorm_b=row(vnorm_b[l]), w_spatial=w_spatial[l],
        b_spatial=b_spatial[l], w_a_out=w_a_out[l].astype(BF16), w0=row(w0[l]),
        w2=_stack_rhs3(w2[l]), a0=row(a0[l]), a2=_stack_rhs3(a2[l]),
        g2=jnp.concatenate([g2[l].astype(BF16)] * 2, axis=0),
        k_k=row(k_k[l]), k_a=row(k_a[l]),
        r_k=row(r_k[l]), lnx_g=row(lnx_g[l]), lnx_b=row(lnx_b[l]),
        w_b_out=w_b_out[l].astype(BF16), w_out=w_out[l].astype(BF16),
        norm2_g=row(norm2_g[l]), router_w=jnp.concatenate(_split(router_w[l].T), axis=0),
        router_b=router_b[l].reshape(-1, 1), bd=bd)

    routed_p, _, sh_p, s_p = _stream(x_prompt, None, None, p, None)
    routed_s, vn_s, sh_s, s_s = _stream(x_sample, state_shift[l], state_wkv[l], p, ts)
    h_p, xp_p, topi_p, prob_p, rank_p, cnt_p = routed_p
    h_s, xp_s, topi_s, prob_s, rank_s, cnt_s = routed_s

    n_p, n_s = h_p.shape[0], h_s.shape[0]
    n_e = router_w.shape[-1]
    n_tiles = (n_p + n_s) * TOP_K // MOE_TM + n_e
    pos_t, tile0, cnt = _route(topi_p, rank_p, cnt_p, topi_s, rank_s, cnt_s)
    xs = _scatter_rows([xp_p, xp_s], pos_t, n_tiles * MOE_TM)
    ys = _moe(xs, tile0, cnt, exp_w_gu[l], exp_b_gu[l][:, None, :],
              exp_w_down[l], exp_b_down[l][:, None, :])
    yg = _gather_rows(ys, pos_t.reshape(-1)).reshape(TOP_K, n_p + n_s, d_model // 2)
    gf = row(normf_g)
    y_p = _combine(h_p, yg, 0, prob_p, gf, TOKEN_TILE).reshape(bp, tp, d_model)
    y_s = _combine(h_s, yg, n_p, prob_s, gf, TOKEN_TILE).reshape(bs, ts, d_model)
    return (y_p, y_s, sh_p[None], s_p[None], sh_s[None], s_s[None],
            vn_s.reshape(1, bs, ts, d_a))
```

```python
import functools
import math

import jax
import jax.numpy as jnp
from jax import lax
from jax.experimental import pallas as pl
from jax.experimental.pallas import tpu as pltpu
from jax.experimental.pallas import tpu_sc as plsc

F32 = jnp.float32
BF16 = jnp.bfloat16

CHUNK = 128
G_A = 8
HEAD_B = 64
R_W, R_A, R_G = 64, 64, 128
TOP_K = 4
SWIGLU_LIMIT = 7.0
SWIGLU_ALPHA = 1.702
EPS = 1e-5
GN_EPS = HEAD_B * 1e-5

LANES = 128
PAIR = 2 * HEAD_B
VMEM_LIMIT = 56 * 1024 * 1024


def _dot(a, b):
    return jnp.dot(a.astype(BF16), b.astype(BF16), preferred_element_type=F32)


def _split(x):
    hi = x.astype(BF16)
    lo = (x - hi.astype(F32)).astype(BF16)
    return hi, lo


def _stack_rhs3(b):
    bh, bl = _split(b)
    return jnp.concatenate([bh, bh, bl], axis=0)


def _dot3_stacked(a, b_stacked):
    ah, al = _split(a)
    return jnp.dot(jnp.concatenate([ah, al, ah], axis=1), b_stacked, preferred_element_type=F32)


def _dot2_stacked(a, b_stacked):
    ah, al = _split(a)
    return jnp.dot(jnp.concatenate([ah, al], axis=1), b_stacked, preferred_element_type=F32)


def _head_sums(x, ones_bd):
    w = ones_bd.shape[0]
    return jnp.concatenate([jnp.dot(x[:, c:c + w].astype(BF16), ones_bd,
                                    preferred_element_type=F32)
                            for c in range(0, x.shape[1], w)], axis=1)


def _dot_exact_lhs(a_bf16, b):
    bh, bl = _split(b)
    bm = b - bh.astype(F32) - bl.astype(F32)
    return (jnp.dot(a_bf16, bh, preferred_element_type=F32)
            + jnp.dot(a_bf16, bl, preferred_element_type=F32)
            + jnp.dot(a_bf16, bm.astype(BF16), preferred_element_type=F32))


def _rms(x, g):
    return x * lax.rsqrt(jnp.mean(x * x, axis=-1, keepdims=True) + EPS) * g


def _sigmoid(x):
    return 1.0 / (1.0 + jnp.exp(-x))


def _full_spec(shape):
    nd = len(shape)
    return pl.BlockSpec(shape, lambda *_: (0,) * nd, pipeline_mode=pl.Buffered(1))


def _branch_a_kernel(x_ref, g1_ref, wu_ref, wv_ref, wga_ref, vg_ref, vb_ref, wmix_ref,
                     bias_ref, wao_ref, apart_ref, vn_ref):
    xn = _rms(x_ref[...], g1_ref[...]).astype(BF16)
    v = jnp.dot(xn, wv_ref[...], preferred_element_type=F32)
    mu = jnp.mean(v, axis=-1, keepdims=True)
    d = v - mu
    var = jnp.mean(d * d, axis=-1, keepdims=True)
    vn = d * lax.rsqrt(var + EPS) * vg_ref[...] + vb_ref[...]
    vn_ref[...] = vn
    vnb = vn.astype(BF16)
    c_a = vnb.shape[1] // G_A
    rb = wmix_ref.shape[1]
    mixed = jnp.concatenate(
        [jnp.concatenate(
            [jnp.dot(wmix_ref[g], vnb[r0:r0 + rb, g * c_a:(g + 1) * c_a],
                     preferred_element_type=F32) for g in range(G_A)], axis=1)
         for r0 in range(0, vnb.shape[0], rb)], axis=0) + bias_ref[...]
    u = jnp.dot(xn, wu_ref[...], preferred_element_type=F32)
    ya = (u * mixed).astype(BF16)
    ga = jnp.dot(xn, wga_ref[...], preferred_element_type=F32)
    apart_ref[...] = _sigmoid(ga) * jnp.dot(ya, wao_ref[...], preferred_element_type=F32)


def _branch_a(x2d, tm, g1, wu, wv, wga, vg, vb, wmix, bias_full, wao):
    n, d = x2d.shape
    d_a = wu.shape[1]
    tok = lambda w: pl.BlockSpec((tm, w), lambda i: (i, 0))
    return pl.pallas_call(
        _branch_a_kernel,
        grid=(n // tm,),
        in_specs=[tok(d), _full_spec(g1.shape), _full_spec(wu.shape), _full_spec(wv.shape),
                  _full_spec(wga.shape), _full_spec(vg.shape), _full_spec(vb.shape),
                  _full_spec(wmix.shape), _full_spec(bias_full.shape), _full_spec(wao.shape)],
        out_specs=[tok(d), tok(d_a)],
        out_shape=[jax.ShapeDtypeStruct((n, d), F32), jax.ShapeDtypeStruct((n, d_a), F32)],
        compiler_params=pltpu.CompilerParams(dimension_semantics=("parallel",),
                                             vmem_limit_bytes=VMEM_LIMIT),
        name="branch_a",
    )(x2d, g1, wu, wv, wga, vg, vb, wmix, bias_full, wao)


def _branch_b_kernel(x_ref, g1_ref, wcur_ref, wgb_ref, mu_ref, ext_ref, w0_ref, w2_ref,
                     a0_ref, a2_ref, g2_ref, kk_ref, ka_ref, rk_ref, bd_ref,
                     r_out, k_out, v_out, kk_out, kka_out, lw_out, g_out, bonus_out,
                     sgb_out, cur_out, carry_scr, *, seq_rows, d_b):
    tm = x_ref.shape[0]
    xn = _rms(x_ref[...], g1_ref[...]).astype(BF16)
    cur = jnp.dot(xn, wcur_ref[...], preferred_element_type=F32)
    rolled = pltpu.roll(cur, 1, axis=0)
    row = lax.broadcasted_iota(jnp.int32, (tm, 1), 0)
    if seq_rows is None:
        first_tile = pl.program_id(1) == 0
        carry = jnp.where(first_tile, 0.0, carry_scr[...])
        prev = jnp.where(row == 0, carry, rolled)
        carry_scr[...] = cur[tm - 1:tm, :]
        cur_out[...] = cur[tm - 8:tm, :]
    else:
        prev = jnp.where(row % seq_rows == 0, ext_ref[...], rolled)
        cur_out[...] = cur
    xs = cur + (prev - cur) * mu_ref[...]
    r = xs[:, 0:d_b]
    k = xs[:, d_b:2 * d_b]
    v = xs[:, 2 * d_b:3 * d_b]
    o = 3 * d_b
    xw = xs[:, o:o + R_W]
    xa = xs[:, o + R_W:o + R_W + R_A]
    xg = xs[:, o + R_W + R_A:o + R_W + R_A + R_G]
    wl = w0_ref[...] + _dot3_stacked(jnp.tanh(xw), w2_ref[...])
    lw = -math.exp(-0.5) * _sigmoid(wl)
    a = _sigmoid(a0_ref[...] + _dot3_stacked(xa, a2_ref[...]))
    g = _dot2_stacked(_sigmoid(xg), g2_ref[...])
    bd = bd_ref[...]
    kkr = k * kk_ref[...]
    n2 = _head_sums(kkr * kkr, bd)
    kk = kkr * lax.rsqrt(jnp.maximum(n2, 1e-24))
    k2 = k * (1.0 + (a - 1.0) * ka_ref[...])
    bonus = _head_sums(r * k2 * rk_ref[...], bd) * v
    r_out[...] = r
    k_out[...] = k2
    v_out[...] = v
    kk_out[...] = kk
    kka_out[...] = kk * a
    lw_out[...] = lw
    g_out[...] = g
    bonus_out[...] = bonus
    sgb_out[...] = _sigmoid(jnp.dot(xn, wgb_ref[...], preferred_element_type=F32))


def _branch_b(x3d, tm, seq_rows, ext, g1, wcur, wgb, mu, w0, w2, a0, a2, g2, k_k, k_a, r_k, bd):
    nb, t, d = x3d.shape
    d_b = w0.shape[1]
    shift_w = wcur.shape[1]
    nt = t // tm
    tok = lambda w: pl.BlockSpec((None, tm, w), lambda b, i: (b, i, 0))
    cur_rows = 8 if seq_rows is None else tm
    outs = [jax.ShapeDtypeStruct((nb, t, d_b), F32)] * 8 + [
        jax.ShapeDtypeStruct((nb, t, d), F32),
        jax.ShapeDtypeStruct((nb, nt * cur_rows, shift_w), F32)]
    out_specs = [tok(d_b)] * 8 + [tok(d), pl.BlockSpec((None, cur_rows, shift_w),
                                                       lambda b, i: (b, i, 0))]
    weights = (g1, wcur, wgb, mu)
    small = (w0, w2, a0, a2, g2, k_k, k_a, r_k, bd)
    return pl.pallas_call(
        functools.partial(_branch_b_kernel, seq_rows=seq_rows, d_b=d_b),
        grid=(nb, nt),
        in_specs=[tok(d)] + [_full_spec(w.shape) for w in weights]
        + [tok(shift_w) if seq_rows is not None else _full_spec(ext.shape)]
        + [_full_spec(w.shape) for w in small],
        out_specs=out_specs,
        out_shape=outs,
        scratch_shapes=[pltpu.VMEM((1, shift_w), F32)],
        compiler_params=pltpu.CompilerParams(dimension_semantics=("parallel", "arbitrary"),
                                             vmem_limit_bytes=VMEM_LIMIT),
        name="branch_b",
    )(x3d, *weights, ext, *small)


def _scan_kernel(r_ref, k_ref, v_ref, kk_ref, kka_ref, lw_ref, y_ref, sout_ref, s_scr, *, chunk):
    L = chunk
    W2 = 2 * L
    c = pl.program_id(1)
    n_seq, n_pairs = s_scr.shape[:2]

    @pl.when(c == 0)
    def _():
        s_scr[...] = jnp.zeros_like(s_scr)

    iota = lambda shape, dim: lax.broadcasted_iota(jnp.int32, shape, dim)
    stack_mask = (iota((W2, 1), 0) >= L) == (iota((1, PAIR), 1) >= HEAD_B)
    bd_mask = (iota((W2, 1), 0) >= L) == (iota((1, W2), 1) >= L)
    assert L & (L - 1) == 0, "chunk length must be a power of two"
    rw = iota((L, W2), 0)
    cw = iota((L, W2), 1) & (L - 1)
    strict_w = cw < rw
    eye_w = (cw == rw).astype(F32)
    incl_w2 = (iota((L, 2 * W2), 1) & (L - 1)) <= iota((L, 2 * W2), 0)
    tri = (iota((L, L), 1) <= iota((L, L), 0)).astype(BF16)
    di = iota((PAIR, PAIR), 0)
    dj = iota((PAIR, PAIR), 1)
    diag = di == dj
    head_diag = (di >= HEAD_B) == (dj >= HEAD_B)

    def stack(x):
        return jnp.where(stack_mask, jnp.concatenate([x, x], axis=0), 0.0)

    def bd(xw):
        return jnp.where(bd_mask, jnp.concatenate([xw, xw], axis=0), 0.0)

    n_double = max(L.bit_length() - 2, 0)
    prep = []
    for s in range(n_seq):
        lw_all = lw_ref[s]
        cum_all = _dot_exact_lhs(tri, lw_all)
        for p in range(n_pairs):
            sl = slice(p * PAIR, (p + 1) * PAIR)
            cum = cum_all[:, sl]
            cum_last = cum[L - 1:L, :]
            e_neg = jnp.exp(-cum)
            e_rel = jnp.exp(cum_last - cum)
            kk = kk_ref[s, :, sl]
            kka = kka_ref[s, :, sl]
            kx = k_ref[s, :, sl]
            prep.append(dict(
                s=s, p=p, sl=sl, ab=-kk * jnp.exp(cum - lw_all[:, sl]),
                rb=r_ref[s, :, sl] * jnp.exp(cum), bb=kka * e_neg, kb=kx * e_neg,
                bt=kka * e_rel, kt=kx * e_rel, v=v_ref[s, :, sl], decay=jnp.exp(cum_last)))
    a_w = [lax.dot_general(
        jnp.concatenate([q["ab"], q["rb"]], axis=0).astype(BF16),
        jnp.concatenate([stack(q["bb"]), stack(q["kb"])], axis=0).astype(BF16),
        (((1,), (1,)), ((), ())), preferred_element_type=F32) for q in prep]
    nw = [jnp.where(strict_w, a[:L, :W2], 0.0) for a in a_w]
    tw = [eye_w + n for n in nw]
    if n_double:
        sw = [_dot(n, bd(n)) for n in nw]
        for _ in range(n_double - 1):
            prod = [_dot(jnp.concatenate([t, s], axis=0), bd(s)) for t, s in zip(tw, sw)]
            tw = [t + p[:L] for t, p in zip(tw, prod)]
            sw = [p[L:] for p in prod]
        tw = [t + _dot(t, bd(s)) for t, s in zip(tw, sw)]
    akv = [_dot(jnp.where(strict_w, a[:L, W2:], 0.0), stack(q["v"])) for a, q in zip(a_w, prep)]
    tx = [_dot(t, jnp.concatenate([stack(q["ab"]), stack(kv)], axis=1))
          for t, q, kv in zip(tw, prep, akv)]
    ry = [_dot(jnp.where(incl_w2, a[L:], 0.0),
               jnp.concatenate(
                   [jnp.concatenate([stack(x[:, :PAIR]), stack(x[:, PAIR:])], axis=1),
                    jnp.concatenate([jnp.zeros((W2, PAIR), F32), stack(q["v"])], axis=1)], axis=0))
          for a, x, q in zip(a_w, tx, prep)]
    for q, x, y in zip(prep, tx, ry):
        st = s_scr[q["s"], q["p"]]
        us = _dot(jnp.concatenate([x[:, :PAIR], q["rb"] + y[:, :PAIR]], axis=0), st)
        y_ref[q["s"], :, q["sl"]] = us[L:] + y[:, PAIR:]
        u = us[:L] + x[:, PAIR:]
        lhs = jnp.concatenate([jnp.where(diag, q["decay"], 0.0),
                               jnp.concatenate([q["bt"], q["kt"]], axis=0).T], axis=1)
        g = _dot(lhs, jnp.concatenate([st, u, q["v"]], axis=0))
        s_scr[q["s"], q["p"]] = jnp.where(head_diag, g, 0.0)

    @pl.when(c == pl.num_programs(1) - 1)
    def _():
        for s in range(n_seq):
            for p in range(n_pairs):
                nat = s_scr[s, p].T
                sout_ref[s, 2 * p] = nat[:HEAD_B, :HEAD_B]
                sout_ref[s, 2 * p + 1] = nat[HEAD_B:, HEAD_B:]


SCAN_CHUNK = 64
SCAN_SEQS = 8


def _scan(r, k, v, kk, kka, lw):
    nb, t, d_b = r.shape
    n_pairs = d_b // PAIR
    n_heads = d_b // HEAD_B
    tok = pl.BlockSpec((SCAN_SEQS, SCAN_CHUNK, d_b), lambda b, c: (b, c, 0))
    st = pl.BlockSpec((SCAN_SEQS, n_heads, HEAD_B, HEAD_B), lambda b, c: (b, 0, 0, 0))
    return pl.pallas_call(
        functools.partial(_scan_kernel, chunk=SCAN_CHUNK),
        grid=(nb // SCAN_SEQS, t // SCAN_CHUNK),
        in_specs=[tok] * 6,
        out_specs=[tok, st],
        out_shape=[jax.ShapeDtypeStruct((nb, t, d_b), F32),
                   jax.ShapeDtypeStruct((nb, n_heads, HEAD_B, HEAD_B), F32)],
        scratch_shapes=[pltpu.VMEM((SCAN_SEQS, n_pairs, PAIR, PAIR), F32)],
        compiler_params=pltpu.CompilerParams(dimension_semantics=("parallel", "arbitrary"),
                                             vmem_limit_bytes=VMEM_LIMIT),
        name="rwkv_scan",
    )(r, k, v, kk, kka, lw)


STEP_ROWS = 8


def _steps_kernel(r_ref, k_ref, v_ref, kk_ref, kka_ref, lw_ref, s0_ref, y_ref, s_ref,
                  vec_scr, yt_scr, *, n_steps):
    n_seq = s_ref.shape[-1]
    s_ref[...] = s0_ref[...]
    for t in range(n_steps):
        rows = pl.ds(t, n_seq, stride=n_steps)
        vecs = {}
        for slot, (name, ref) in enumerate((("r", r_ref), ("k", k_ref), ("v", v_ref),
                                            ("kk", kk_ref), ("kka", kka_ref), ("lw", lw_ref))):
            vecs[name] = ref[rows, :].T
            vec_scr[slot] = vecs[name]
        for q in range(PAIR // HEAD_B):
            hs = slice(q * HEAD_B, (q + 1) * HEAD_B)
            r_q, k_q, kka_q = vecs["r"][hs], vecs["k"][hs], vecs["kka"][hs]
            nkk_q = -vecs["kk"][hs]
            w_q = jnp.exp(vecs["lw"][hs])

            def rows_step(i0, carry):
                for u in range(STEP_ROWS):
                    i = i0 * STEP_ROWS + u
                    s_row = s_ref[q, i]
                    sa = jnp.sum(s_row * nkk_q, axis=0, keepdims=True)
                    v_i = vec_scr[2, pl.ds(q * HEAD_B + i, 1), :]
                    s_new = s_row * w_q + sa * kka_q + v_i * k_q
                    s_ref[q, i] = s_new
                    yt_scr[pl.ds(q * HEAD_B + i, 1), :] = jnp.sum(s_new * r_q, axis=0,
                                                                  keepdims=True)
                return carry

            lax.fori_loop(0, HEAD_B // STEP_ROWS, rows_step, 0)
        y_ref[rows, :] = yt_scr[...].T


def _scan_steps(r, k, v, kk, kka, lw, s0, n_steps):
    n, d_b = r.shape
    n_heads, _, _, n_seq = s0.shape
    tok = pl.BlockSpec((n, PAIR), lambda p: (0, p))
    st = pl.BlockSpec((PAIR // HEAD_B, HEAD_B, HEAD_B, n_seq), lambda p: (p, 0, 0, 0))
    return pl.pallas_call(
        functools.partial(_steps_kernel, n_steps=n_steps),
        grid=(d_b // PAIR,),
        in_specs=[tok] * 6 + [st],
        out_specs=[tok, st],
        out_shape=[jax.ShapeDtypeStruct((n, d_b), F32), jax.ShapeDtypeStruct(s0.shape, F32)],
        scratch_shapes=[pltpu.VMEM((6, PAIR, n_seq), F32), pltpu.VMEM((PAIR, n_seq), F32)],
        compiler_params=pltpu.CompilerParams(dimension_semantics=("parallel",),
                                             vmem_limit_bytes=VMEM_LIMIT),
        name="rwkv_steps",
    )(r, k, v, kk, kka, lw, s0)


def _pack_bf16_pairs(x):
    w = x.shape[1] // 2
    bits = lambda v: lax.bitcast_convert_type(v.astype(BF16).astype(F32), jnp.uint32)
    return (bits(x[:, :w]) >> 16) | (bits(x[:, w:]) & jnp.uint32(0xFFFF0000))


def _unpack_bf16_pairs(u):
    lo = lax.bitcast_convert_type(u << 16, F32)
    hi = lax.bitcast_convert_type(u & jnp.uint32(0xFFFF0000), F32)
    return lo, hi


def _merge_kernel(y_ref, g_ref, bonus_ref, sgb_ref, apart_ref, x_ref, lg_ref, lb_ref, bd_ref,
                  wbo_ref, wo_ref, g2_ref, rwt_ref, rb_ref,
                  h_out, xp_out, topi_out, prob_out, rank_out, cnt_out, cnt_scr):
    step = pl.program_id(0)

    @pl.when(step == 0)
    def _():
        cnt_scr[...] = jnp.zeros_like(cnt_scr)

    bd = bd_ref[...]
    y = y_ref[...]
    inv_n = 1.0 / HEAD_B
    mu = _head_sums(y, bd) * inv_n
    d = y - mu
    var = _head_sums(d * d, bd) * inv_n
    yn = d * lax.rsqrt(var + GN_EPS) * lg_ref[...] + lb_ref[...]
    yb = ((yn + bonus_ref[...]) * g_ref[...]).astype(BF16)
    merged = apart_ref[...] + sgb_ref[...] * jnp.dot(yb, wbo_ref[...], preferred_element_type=F32)
    h = x_ref[...] + jnp.dot(merged.astype(BF16), wo_ref[...], preferred_element_type=F32)
    h_out[...] = h
    xn2 = _rms(h, g2_ref[...])
    xp_out[...] = _pack_bf16_pairs(xn2)
    n_e = rb_ref.shape[0]
    tm = xn2.shape[0]
    xh, xl = _split(xn2)
    rwt = rwt_ref[...]
    nt = (((1,), (1,)), ((), ()))
    t = lax.dot_general(rwt, xh, nt, preferred_element_type=F32)
    logits = (t[:n_e] + t[n_e:] + lax.dot_general(rwt[:n_e], xl, nt, preferred_element_type=F32)
              + rb_ref[...])
    idx = lax.broadcasted_iota(jnp.int32, logits.shape, 0).astype(F32)
    work = logits
    tops, hots, sels = [], [], []
    for _ in range(TOP_K):
        m = jnp.max(work, axis=0, keepdims=True)
        sel = jnp.min(jnp.where(work == m, idx, float(n_e)), axis=0, keepdims=True)
        hot = idx == sel
        tops.append(m)
        hots.append(hot)
        sels.append(sel)
        work = jnp.where(hot, -jnp.inf, work)
    es = [jnp.exp(t - tops[0]) for t in tops]
    denom = es[0] + es[1] + es[2] + es[3]
    topi_out[...] = jnp.concatenate(sels, axis=0).astype(jnp.int32)
    prob_out[...] = jnp.concatenate([e / denom for e in es], axis=0)
    hot_any = jnp.zeros_like(logits)
    for hot in hots:
        hot_any = hot_any + hot.astype(F32)
    ri = lax.broadcasted_iota(jnp.int32, (tm, tm), 0)
    ci = lax.broadcasted_iota(jnp.int32, (tm, tm), 1)
    before = _dot(hot_any, (ri < ci).astype(BF16)) + cnt_scr[...]
    rank_out[...] = jnp.concatenate(
        [jnp.sum(jnp.where(hot, before, 0.0), axis=0, keepdims=True) for hot in hots],
        axis=0).astype(jnp.int32)
    total = cnt_scr[...] + jnp.sum(hot_any, axis=1, keepdims=True)
    cnt_scr[...] = total
    cnt_out[...] = total.astype(jnp.int32)


def _merge(y, g, bonus, sgb, apart, x2d, tm, lnx_g, lnx_b, bd, wbo, wo, g2n, rw, rb):
    n, d = x2d.shape
    d_b = y.shape[1]
    n_e = rb.shape[0]
    tok = lambda w: pl.BlockSpec((tm, w), lambda i: (i, 0))
    per_k = pl.BlockSpec((TOP_K, tm), lambda i: (0, i))
    weights = (lnx_g, lnx_b, bd, wbo, wo, g2n, rw, rb)
    return pl.pallas_call(
        _merge_kernel,
        grid=(n // tm,),
        in_specs=[tok(d_b), tok(d_b), tok(d_b), tok(d), tok(d), tok(d)]
        + [_full_spec(w.shape) for w in weights],
        out_specs=[tok(d), tok(d // 2), per_k, per_k, per_k,
                   pl.BlockSpec((n_e, 1), lambda i: (0, 0))],
        out_shape=[jax.ShapeDtypeStruct((n, d), F32),
                   jax.ShapeDtypeStruct((n, d // 2), jnp.uint32),
                   jax.ShapeDtypeStruct((TOP_K, n), jnp.int32),
                   jax.ShapeDtypeStruct((TOP_K, n), F32),
                   jax.ShapeDtypeStruct((TOP_K, n), jnp.int32),
                   jax.ShapeDtypeStruct((n_e, 1), jnp.int32)],
        scratch_shapes=[pltpu.VMEM((n_e, 1), F32)],
        compiler_params=pltpu.CompilerParams(dimension_semantics=("arbitrary",),
                                             vmem_limit_bytes=VMEM_LIMIT),
        name="merge_router",
    )(y, g, bonus, sgb, apart, x2d, *weights)


MOE_TM = 1024
MOE_SUB = 128
MOE_FF_CHUNK = 512
CAST_ROWS = 256


def _route(topi_p, rank_p, cnt_p, topi_s, rank_s, cnt_s):
    n_e = cnt_p.size
    cnt_p, cnt_s = cnt_p.reshape(n_e), cnt_s.reshape(n_e)
    cnt = cnt_p + cnt_s
    padded = (cnt + MOE_TM - 1) // MOE_TM * MOE_TM
    ends = jnp.cumsum(padded)
    base = ends - padded
    lookup = lambda table, idx: jnp.sum(
        jnp.where(idx[..., None] == jnp.arange(n_e), table, 0), axis=-1)
    pos_t = jnp.concatenate([lookup(base, topi_p) + rank_p,
                             lookup(base + cnt_p, topi_s) + rank_s], axis=1)
    return (pos_t.astype(jnp.int32), (base // MOE_TM).astype(jnp.int32),
            cnt.astype(jnp.int32))


def _moe_kernel(tile0_ref, cnt_ref, xs_hbm, wgu_ref, bgu_ref, wd_ref, bdn_ref, ys_hbm,
                wgu_b, wd_b, xbuf, ybuf, sem_in, sem_out, done_ref):
    e = pl.program_id(0)
    n_e = pl.num_programs(0)

    @pl.when(e == 0)
    def _():
        done_ref[0] = 0

    g0 = done_ref[0]
    tile0 = tile0_ref[e]
    cnt = cnt_ref[e]
    n_t = (cnt + MOE_TM - 1) // MOE_TM
    d, gu = wgu_ref.shape
    d_ff = wd_ref.shape[0]
    half = d // 2

    def in_copy(tile, slot):
        return pltpu.make_async_copy(xs_hbm.at[pl.ds(tile * MOE_TM, MOE_TM)], xbuf.at[slot],
                                     sem_in.at[slot])

    def out_copy(tile, slot):
        return pltpu.make_async_copy(ybuf.at[slot], ys_hbm.at[pl.ds(tile * MOE_TM, MOE_TM)],
                                     sem_out.at[slot])

    @pl.when(jnp.logical_and(e == 0, n_t > 0))
    def _():
        in_copy(tile0, 0).start()

    @pl.when(n_t > 0)
    def _():
        for r0 in range(0, d, CAST_ROWS):
            wgu_b[r0:r0 + CAST_ROWS, :] = wgu_ref[r0:r0 + CAST_ROWS, :].astype(BF16)
        for r0 in range(0, d_ff, CAST_ROWS):
            wd_b[r0:r0 + CAST_ROWS, :] = wd_ref[r0:r0 + CAST_ROWS, :].astype(BF16)

    def tile_step(j, carry):
        slot = (g0 + j) % 2
        in_copy(tile0 + j, slot).wait()

        @pl.when(j + 1 < n_t)
        def _():
            in_copy(tile0 + j + 1, 1 - slot).start()

        @pl.when(g0 + j >= 2)
        def _():
            out_copy(0, slot).wait()

        left = cnt - j * MOE_TM

        def ffn(row0, rows):
            lo, hi = _unpack_bf16_pairs(xbuf[slot, pl.ds(row0, rows)])
            valid = lax.broadcasted_iota(jnp.int32, lo.shape, 0) < left - row0
            x_lo = jnp.where(valid, lo, 0.0).astype(BF16)
            x_hi = jnp.where(valid, hi, 0.0).astype(BF16)

            def proj(c0):
                cols = slice(c0, c0 + MOE_FF_CHUNK)
                return (jnp.dot(x_lo, wgu_b[:half, cols], preferred_element_type=F32)
                        + jnp.dot(x_hi, wgu_b[half:, cols], preferred_element_type=F32)
                        + bgu_ref[:, cols])

            y = jnp.zeros((rows, d), F32)
            for f in range(d_ff // MOE_FF_CHUNK):
                c0 = f * MOE_FF_CHUNK
                gate = jnp.minimum(proj(c0), SWIGLU_LIMIT)
                up = jnp.clip(proj(d_ff + c0), -SWIGLU_LIMIT, SWIGLU_LIMIT)
                hh = (up + 1.0) * gate * _sigmoid(gate * SWIGLU_ALPHA)
                y = y + jnp.dot(hh.astype(BF16), wd_b[c0:c0 + MOE_FF_CHUNK, :],
                                preferred_element_type=F32)
            ybuf[slot, pl.ds(row0, rows)] = _pack_bf16_pairs(y + bdn_ref[...])

        n_sub = (jnp.minimum(left, MOE_TM) + MOE_SUB - 1) // MOE_SUB

        @pl.when(n_sub == MOE_TM // MOE_SUB)
        def _():
            ffn(0, MOE_TM)

        @pl.when(n_sub < MOE_TM // MOE_SUB)
        def _():
            def sub_step(i, c):
                ffn(pl.multiple_of(i * 2 * MOE_SUB, 2 * MOE_SUB), 2 * MOE_SUB)
                return c
            lax.fori_loop(0, n_sub // 2, sub_step, 0)

            @pl.when(n_sub % 2 == 1)
            def _():
                ffn(pl.multiple_of((n_sub // 2) * 2 * MOE_SUB, 2 * MOE_SUB), MOE_SUB)

        out_copy(tile0 + j, slot).start()
        return carry

    lax.fori_loop(0, n_t, tile_step, 0)

    g1 = g0 + n_t
    done_ref[0] = g1
    nxt = jnp.minimum(e + 1, n_e - 1)

    @pl.when(jnp.logical_and(e + 1 < n_e, cnt_ref[nxt] > 0))
    def _():
        in_copy(tile0_ref[nxt], g1 % 2).start()

    @pl.when(jnp.logical_and(e == n_e - 1, g1 >= 2))
    def _():
        out_copy(0, g1 % 2).wait()

    @pl.when(jnp.logical_and(e == n_e - 1, g1 >= 1))
    def _():
        out_copy(0, (g1 - 1) % 2).wait()


def _moe(xs, tile0, cnt, wgu, bgu, wd, bdn):
    p_rows, half = xs.shape
    n_e, d, gu = wgu.shape
    d_ff = wd.shape[1]
    w_blk = lambda e, tile0, cnt: (e, 0, 0)
    grid_spec = pltpu.PrefetchScalarGridSpec(
        num_scalar_prefetch=2,
        grid=(n_e,),
        in_specs=[pl.BlockSpec(memory_space=pl.ANY),
                  pl.BlockSpec((None, d, gu), w_blk),
                  pl.BlockSpec((None, 1, gu), w_blk),
                  pl.BlockSpec((None, d_ff, d), w_blk),
                  pl.BlockSpec((None, 1, d), w_blk)],
        out_specs=pl.BlockSpec(memory_space=pl.ANY),
        scratch_shapes=[pltpu.VMEM((d, gu), BF16), pltpu.VMEM((d_ff, d), BF16),
                        pltpu.VMEM((2, MOE_TM, half), jnp.uint32),
                        pltpu.VMEM((2, MOE_TM, half), jnp.uint32),
                        pltpu.SemaphoreType.DMA((2,)), pltpu.SemaphoreType.DMA((2,)),
                        pltpu.SMEM((1,), jnp.int32)])
    return pl.pallas_call(
        _moe_kernel,
        grid_spec=grid_spec,
        out_shape=jax.ShapeDtypeStruct((p_rows, half), jnp.uint32),
        compiler_params=pltpu.CompilerParams(dimension_semantics=("arbitrary",),
                                             vmem_limit_bytes=VMEM_LIMIT),
        name="moe_experts",
    )(tile0, cnt, xs, wgu, bgu, wd, bdn)


def _combine_kernel(h_ref, yg_ref, prob_ref, gf_ref, out_ref):
    tm = h_ref.shape[0]
    half = h_ref.shape[1] // 2
    p_rows = jnp.concatenate([prob_ref[...], jnp.zeros((8 - TOP_K, tm), F32)], axis=0)
    ph = p_rows.astype(BF16)
    pm, pl_ = _split(p_rows - ph.astype(F32))
    eye = (lax.broadcasted_iota(jnp.int32, (tm, tm), 0)
           == lax.broadcasted_iota(jnp.int32, (tm, tm), 1)).astype(BF16)
    nt = (((1,), (1,)), ((), ()))
    prob = sum(lax.dot_general(eye, piece, nt, preferred_element_type=F32)
               for piece in (ph, pm, pl_))
    acc_lo = jnp.zeros((tm, half), F32)
    acc_hi = jnp.zeros((tm, half), F32)
    for k in range(TOP_K):
        lo, hi = _unpack_bf16_pairs(yg_ref[k])
        pk = prob[:, k:k + 1]
        acc_lo = acc_lo + pk * lo
        acc_hi = acc_hi + pk * hi
    z = h_ref[...] + jnp.concatenate([acc_lo, acc_hi], axis=1)
    out_ref[...] = _rms(z, gf_ref[...])


def _combine(h, yg, row0, prob, gf, tm):
    n, d = h.shape
    tok = lambda w: pl.BlockSpec((tm, w), lambda i: (i, 0))
    t0 = row0 // tm
    return pl.pallas_call(
        _combine_kernel,
        grid=(n // tm,),
        in_specs=[tok(d), pl.BlockSpec((TOP_K, tm, d // 2), lambda i: (0, i + t0, 0)),
                  pl.BlockSpec((TOP_K, tm), lambda i: (0, i)), _full_spec(gf.shape)],
        out_specs=tok(d),
        out_shape=jax.ShapeDtypeStruct((n, d), F32),
        compiler_params=pltpu.CompilerParams(dimension_semantics=("parallel",),
                                             vmem_limit_bytes=VMEM_LIMIT),
        name="moe_combine",
    )(h, yg, prob, gf)


SC_CORES = 2
SC_SUBCORES = 16
SC_WORKERS = SC_CORES * SC_SUBCORES
SC_MAX_INDEX = 128


def _sc_chunk(rows_per_worker, limit=SC_MAX_INDEX):
    for c in range(limit, 7, -8):
        if rows_per_worker % c == 0:
            return c
    raise ValueError(f"no 8-aligned chunk divides {rows_per_worker} rows")


def _sc_mesh():
    return plsc.VectorSubcoreMesh(core_axis_name="c", subcore_axis_name="s")


def _sc_worker():
    return lax.axis_index("s") * SC_CORES + lax.axis_index("c")


def _scatter_rows(xs_in, pos_t, p_rows):
    w = xs_in[0].shape[1]
    dtype = xs_in[0].dtype
    n_k, n = pos_t.shape
    pos_flat = pos_t.reshape(n_k * n)
    parts, row0 = [], 0
    for x in xs_in:
        per = x.shape[0] // SC_WORKERS
        parts.append((row0, per, _sc_chunk(per, SC_MAX_INDEX // 2)))
        row0 += x.shape[0]
    max_chunk = max(c for _, _, c in parts)
    max_per = max(p for _, p, _ in parts)

    def body(*refs):
        x_hbms = refs[:len(xs_in)]
        pos_hbm, out_hbm, idx_v, rows_v, sem_l, sem_s = refs[len(xs_in):]
        for x_hbm, (tok0, per, chunk) in zip(x_hbms, parts):
            n_chunks = per // chunk
            base = pl.multiple_of(_sc_worker() * per, 8)
            for k in range(n_k):
                src = pl.multiple_of(k * n + tok0 + base, 8)
                pltpu.sync_copy(pos_hbm.at[pl.ds(src, per)], idx_v.at[pl.ds(k * per, per)])

            def load(c, slot):
                src = x_hbm.at[pl.ds(pl.multiple_of(base + c * chunk, 8), chunk)]
                return pltpu.make_async_copy(src, rows_v.at[slot, pl.ds(0, chunk)], sem_l.at[slot])

            def scatter(c, k, slot):
                rows = idx_v.at[pl.ds(pl.multiple_of(k * per + c * chunk, 8), chunk)]
                return pltpu.make_async_copy(rows_v.at[slot, pl.ds(0, chunk)], out_hbm.at[rows],
                                             sem_s.at[slot])

            load(0, 0).start()

            @pl.loop(0, n_chunks)
            def _(c):
                slot = c % 2
                load(c, slot).wait()

                @pl.when(c + 1 < n_chunks)
                def _():
                    @pl.when(c >= 1)
                    def _():
                        for k in range(n_k):
                            scatter(c - 1, k, 1 - slot).wait()
                    load(c + 1, 1 - slot).start()

                for k in range(n_k):
                    scatter(c, k, slot).start()

            if n_chunks >= 2:
                for k in range(n_k):
                    scatter(n_chunks - 2, k, n_chunks % 2).wait()
            for k in range(n_k):
                scatter(n_chunks - 1, k, (n_chunks - 1) % 2).wait()

    return pl.kernel(
        body, out_type=jax.ShapeDtypeStruct((p_rows, w), dtype), mesh=_sc_mesh(),
        scratch_types=[pltpu.VMEM((n_k * max_per,), jnp.int32),
                       pltpu.VMEM((2, max_chunk, w), dtype),
                       pltpu.SemaphoreType.DMA((2,)), pltpu.SemaphoreType.DMA((2,))],
        name="sc_scatter_rows")(*xs_in, pos_flat)


def _gather_rows(table, idx):
    n = idx.shape[0]
    w = table.shape[1]
    per = n // SC_WORKERS
    chunk = _sc_chunk(per)

    n_chunks = per // chunk

    def body(table_hbm, idx_hbm, out_hbm, idx_v, rows_v, sem_g, sem_w):
        base = pl.multiple_of(_sc_worker() * per, 8)
        pltpu.sync_copy(idx_hbm.at[pl.ds(base, per)], idx_v)

        def gather(c, slot):
            rows = idx_v.at[pl.ds(pl.multiple_of(c * chunk, 8), chunk)]
            return pltpu.make_async_copy(table_hbm.at[rows], rows_v.at[slot], sem_g.at[slot])

        def write(c, slot):
            dst = out_hbm.at[pl.ds(pl.multiple_of(base + c * chunk, 8), chunk)]
            return pltpu.make_async_copy(rows_v.at[slot], dst, sem_w.at[slot])

        gather(0, 0).start()

        @pl.loop(0, n_chunks)
        def _(c):
            slot = c % 2
            gather(c, slot).wait()

            @pl.when(c + 1 < n_chunks)
            def _():
                @pl.when(c >= 1)
                def _():
                    write(c - 1, 1 - slot).wait()
                gather(c + 1, 1 - slot).start()

            write(c, slot).start()

        if n_chunks >= 2:
            write(n_chunks - 2, n_chunks % 2).wait()
        write(n_chunks - 1, (n_chunks - 1) % 2).wait()

    return pl.kernel(
        body, out_type=jax.ShapeDtypeStruct((n, w), table.dtype), mesh=_sc_mesh(),
        scratch_types=[pltpu.VMEM((per,), jnp.int32), pltpu.VMEM((2, chunk, w), table.dtype),
                       pltpu.SemaphoreType.DMA((2,)), pltpu.SemaphoreType.DMA((2,))],
        name="sc_gather_rows")(table, idx)


def _mix_matrix(ws, block, seq_rows):
    causal = jnp.tril(jnp.ones((CHUNK, CHUNK), dtype=bool))
    w = jnp.where(causal[None], ws, 0.0)[:, :seq_rows, :seq_rows]
    eye = jnp.eye(block // seq_rows, dtype=ws.dtype)
    return jnp.einsum("ab,gts->gatbs", eye, w).reshape(ws.shape[0], block, block).astype(BF16)


TOKEN_TILE = 512


def _stream(x, shift_in, wkv_in, p, seq_rows):
    nb, t, d = x.shape
    n = nb * t
    tm = TOKEN_TILE
    x2d = x.reshape(n, d)
    rows = CHUNK if seq_rows is None else seq_rows
    wmix = _mix_matrix(p["w_spatial"], max(rows, LANES), rows)
    pos = jnp.arange(tm) % rows
    d_a = p["wu"].shape[1]
    bias_full = jnp.repeat(p["b_spatial"].T[pos], d_a // G_A, axis=1)
    apart, vn = _branch_a(x2d, tm, p["norm1_g"], p["wu"], p["wv"], p["wga"], p["vnorm_g"],
                          p["vnorm_b"], wmix, bias_full, p["w_a_out"])
    if seq_rows is None:
        xb, ext = x, jnp.zeros((1, p["wcur"].shape[1]), F32)
    else:
        xb = x2d.reshape(n // tm, tm, d)
        ext = jnp.repeat(shift_in, seq_rows, axis=0).reshape(n // tm, tm, -1)
    outs = _branch_b(xb, tm, seq_rows, ext, p["norm1_g"], p["wcur"], p["wgb"], p["mu_shift"],
                     p["w0"], p["w2"], p["a0"], p["a2"], p["g2"], p["k_k"], p["k_a"], p["r_k"],
                     p["bd"])
    r, k2, v, kk, kka, lw, g, bonus, sgb, cur = outs
    d_b = r.shape[-1]
    if seq_rows is None:
        shift_out = cur[:, -1, :]
        y, s_out = _scan(r, k2, v, kk, kka, lw)
        y2d = y.reshape(n, d_b)
    else:
        shift_out = cur.reshape(nb, t, -1)[:, -1, :]
        y2d, s_lanes = _scan_steps(*[a.reshape(n, d_b) for a in (r, k2, v, kk, kka, lw)],
                                   jnp.transpose(wkv_in, (1, 2, 3, 0)), t)
        s_out = jnp.transpose(s_lanes, (3, 0, 1, 2))
    flat = lambda a: a.reshape(n, a.shape[-1])
    routed = _merge(y2d, flat(g), flat(bonus), flat(sgb), apart, x2d, tm, p["lnx_g"],
                    p["lnx_b"], p["bd"], p["w_b_out"], p["w_out"], p["norm2_g"],
                    p["router_w"], p["router_b"])
    return routed, vn, shift_out, s_out


def kernel(x_prompt, x_sample, state_shift, state_wkv, norm1_g, w_in, mu_shift, vnorm_g, vnorm_b, w_spatial, b_spatial, w_a_out, w0, w2, a0, a2, g2, k_k, k_a, r_k, lnx_g, lnx_b, w_b_out, w_out, norm2_g, router_w, router_b, exp_w_gu, exp_b_gu, exp_w_down, exp_b_down, normf_g):
    assert w_in.shape[0] == 1, "one layer: the final norm is fused into its MoE combine"
    l = 0
    d_model = x_prompt.shape[-1]
    shift_w = mu_shift.shape[-1]
    d_a = vnorm_g.shape[-1]
    bp, tp, _ = x_prompt.shape
    bs, ts, _ = x_sample.shape
    head_id = jnp.arange(2 * LANES) // HEAD_B
    bd = (head_id[:, None] == head_id[None, :]).astype(BF16)
    row = lambda a: a.reshape(1, -1)

    wi = w_in[l].astype(BF16)
    o = shift_w
    p = dict(
        norm1_g=row(norm1_g[l]), wcur=wi[:, :o], wu=wi[:, o:o + d_a],
        wv=wi[:, o + d_a:o + 2 * d_a], wga=wi[:, o + 2 * d_a:o + 2 * d_a + d_model],
        wgb=wi[:, o + 2 * d_a + d_model:], mu_shift=row(mu_shift[l]),
        vnorm_g=row(vnorm_g[l]), vnorm_b=row(vnorm_b[l]), w_spatial=w_spatial[l],
        b_spatial=b_spatial[l], w_a_out=w_a_out[l].astype(BF16), w0=row(w0[l]),
        w2=_stack_rhs3(w2[l]), a0=row(a0[l]), a2=_stack_rhs3(a2[l]),
        g2=jnp.concatenate([g2[l].astype(BF16)] * 2, axis=0),
        k_k=row(k_k[l]), k_a=row(k_a[l]),
        r_k=row(r_k[l]), lnx_g=row(lnx_g[l]), lnx_b=row(lnx_b[l]),
        w_b_out=w_b_out[l].astype(BF16), w_out=w_out[l].astype(BF16),
        norm2_g=row(norm2_g[l]), router_w=jnp.concatenate(_split(router_w[l].T), axis=0),
        router_b=router_b[l].reshape(-1, 1), bd=bd)

    routed_p, _, sh_p, s_p = _stream(x_prompt, None, None, p, None)
    routed_s, vn_s, sh_s, s_s = _stream(x_sample, state_shift[l], state_wkv[l], p, ts)
    h_p, xp_p, topi_p, prob_p, rank_p, cnt_p = routed_p
    h_s, xp_s, topi_s, prob_s, rank_s, cnt_s = routed_s

    n_p, n_s = h_p.shape[0], h_s.shape[0]
    n_e = router_w.shape[-1]
    n_tiles = (n_p + n_s) * TOP_K // MOE_TM + n_e
    pos_t, tile0, cnt = _route(topi_p, rank_p, cnt_p, topi_s, rank_s, cnt_s)
    xs = _scatter_rows([xp_p, xp_s], pos_t, n_tiles * MOE_TM)
    ys = _moe(xs, tile0, cnt, exp_w_gu[l], exp_b_gu[l][:, None, :],
              exp_w_down[l], exp_b_down[l][:, None, :])
    yg = _gather_rows(ys, pos_t.reshape(-1)).reshape(TOP_K, n_p + n_s, d_model // 2)
    gf = row(normf_g)
    y_p = _combine(h_p, yg, 0, prob_p, gf, TOKEN_TILE).reshape(bp, tp, d_model)
    y_s = _combine(h_s, yg, n_p, prob_s, gf, TOKEN_TILE).reshape(bs, ts, d_model)
    return (y_p, y_s, sh_p[None], s_p[None], sh_s[None], s_s[None],
            vn_s.reshape(1, bs, ts, d_a))
```

```python
import functools
import math

import jax
import jax.numpy as jnp
from jax import lax
from jax.experimental import pallas as pl
from jax.experimental.pallas import tpu as pltpu
from jax.experimental.pallas import tpu_sc as plsc

F32 = jnp.float32
BF16 = jnp.bfloat16

CHUNK = 128
G_A = 8
HEAD_B = 64
R_W, R_A, R_G = 64, 64, 128
TOP_K = 4
SWIGLU_LIMIT = 7.0
SWIGLU_ALPHA = 1.702
EPS = 1e-5
GN_EPS = HEAD_B * 1e-5

LANES = 128
PAIR = 2 * HEAD_B
VMEM_LIMIT = 56 * 1024 * 1024


def _dot(a, b):
    return jnp.dot(a.astype(BF16), b.astype(BF16), preferred_element_type=F32)


def _split(x):
    hi = x.astype(BF16)
    lo = (x - hi.astype(F32)).astype(BF16)
    return hi, lo


def _stack_rhs3(b):
    bh, bl = _split(b)
    return jnp.concatenate([bh, bh, bl], axis=0)


def _dot3_stacked(a, b_stacked):
    ah, al = _split(a)
    return jnp.dot(jnp.concatenate([ah, al, ah], axis=1), b_stacked, preferred_element_type=F32)


def _dot2_stacked(a, b_stacked):
    ah, al = _split(a)
    return jnp.dot(jnp.concatenate([ah, al], axis=1), b_stacked, preferred_element_type=F32)


def _head_sums(x, ones_bd):
    w = ones_bd.shape[0]
    return jnp.concatenate([jnp.dot(x[:, c:c + w].astype(BF16), ones_bd,
                                    preferred_element_type=F32)
                            for c in range(0, x.shape[1], w)], axis=1)


def _dot_exact_lhs(a_bf16, b):
    bh, bl = _split(b)
    bm = b - bh.astype(F32) - bl.astype(F32)
    return (jnp.dot(a_bf16, bh, preferred_element_type=F32)
            + jnp.dot(a_bf16, bl, preferred_element_type=F32)
            + jnp.dot(a_bf16, bm.astype(BF16), preferred_element_type=F32))


def _rms(x, g):
    return x * lax.rsqrt(jnp.mean(x * x, axis=-1, keepdims=True) + EPS) * g


def _sigmoid(x):
    return 1.0 / (1.0 + jnp.exp(-x))


def _full_spec(shape):
    nd = len(shape)
    return pl.BlockSpec(shape, lambda *_: (0,) * nd, pipeline_mode=pl.Buffered(1))


def _branch_a_kernel(x_ref, g1_ref, wu_ref, wv_ref, wga_ref, vg_ref, vb_ref, wmix_ref,
                     bias_ref, wao_ref, apart_ref, *vn_ref):
    xn = _rms(x_ref[...], g1_ref[...]).astype(BF16)
    v = jnp.dot(xn, wv_ref[...], preferred_element_type=F32)
    mu = jnp.mean(v, axis=-1, keepdims=True)
    d = v - mu
    var = jnp.mean(d * d, axis=-1, keepdims=True)
    vn = d * lax.rsqrt(var + EPS) * vg_ref[...] + vb_ref[...]
    if vn_ref:
        vn_ref[0][...] = vn
    vnb = vn.astype(BF16)
    c_a = vnb.shape[1] // G_A
    rb = wmix_ref.shape[1]
    mixed = jnp.concatenate(
        [jnp.concatenate(
            [jnp.dot(wmix_ref[g], vnb[r0:r0 + rb, g * c_a:(g + 1) * c_a],
                     preferred_element_type=F32) for g in range(G_A)], axis=1)
         for r0 in range(0, vnb.shape[0], rb)], axis=0) + bias_ref[...]
    u = jnp.dot(xn, wu_ref[...], preferred_element_type=F32)
    ya = (u * mixed).astype(BF16)
    ga = jnp.dot(xn, wga_ref[...], preferred_element_type=F32)
    apart_ref[...] = _sigmoid(ga) * jnp.dot(ya, wao_ref[...], preferred_element_type=F32)


def _branch_a(x2d, tm, g1, wu, wv, wga, vg, vb, wmix, bias_full, wao, emit_vn):
    n, d = x2d.shape
    d_a = wu.shape[1]
    tok = lambda w: pl.BlockSpec((tm, w), lambda i: (i, 0))
    n_out = 2 if emit_vn else 1
    return pl.pallas_call(
        _branch_a_kernel,
        grid=(n // tm,),
        in_specs=[tok(d), _full_spec(g1.shape), _full_spec(wu.shape), _full_spec(wv.shape),
                  _full_spec(wga.shape), _full_spec(vg.shape), _full_spec(vb.shape),
                  _full_spec(wmix.shape), _full_spec(bias_full.shape), _full_spec(wao.shape)],
        out_specs=[tok(d), tok(d_a)][:n_out],
        out_shape=[jax.ShapeDtypeStruct((n, d), F32),
                   jax.ShapeDtypeStruct((n, d_a), F32)][:n_out],
        compiler_params=pltpu.CompilerParams(dimension_semantics=("parallel",),
                                             vmem_limit_bytes=VMEM_LIMIT),
        name="branch_a",
    )(x2d, g1, wu, wv, wga, vg, vb, wmix, bias_full, wao)


def _branch_b_kernel(x_ref, g1_ref, wcur_ref, wgb_ref, mu_ref, ext_ref, w0_ref, w2_ref,
                     a0_ref, a2_ref, g2_ref, kk_ref, ka_ref, rk_ref, bd_ref,
                     r_out, k_out, v_out, kk_out, kka_out, lw_out, g_out, bonus_out,
                     sgb_out, cur_out, carry_scr, *, seq_rows, d_b):
    tm = x_ref.shape[0]
    xn = _rms(x_ref[...], g1_ref[...]).astype(BF16)
    cur = jnp.dot(xn, wcur_ref[...], preferred_element_type=F32)
    rolled = pltpu.roll(cur, 1, axis=0)
    row = lax.broadcasted_iota(jnp.int32, (tm, 1), 0)
    if seq_rows is None:
        first_tile = pl.program_id(1) == 0
        carry = jnp.where(first_tile, 0.0, carry_scr[...])
        prev = jnp.where(row == 0, carry, rolled)
        carry_scr[...] = cur[tm - 1:tm, :]
        cur_out[...] = cur[tm - 8:tm, :]
    else:
        prev = jnp.where(row % seq_rows == 0, ext_ref[...], rolled)
        cur_out[...] = cur
    xs = cur + (prev - cur) * mu_ref[...]
    r = xs[:, 0:d_b]
    k = xs[:, d_b:2 * d_b]
    v = xs[:, 2 * d_b:3 * d_b]
    o = 3 * d_b
    xw = xs[:, o:o + R_W]
    xa = xs[:, o + R_W:o + R_W + R_A]
    xg = xs[:, o + R_W + R_A:o + R_W + R_A + R_G]
    wl = w0_ref[...] + _dot3_stacked(jnp.tanh(xw), w2_ref[...])
    lw = -math.exp(-0.5) * _sigmoid(wl)
    a = _sigmoid(a0_ref[...] + _dot3_stacked(xa, a2_ref[...]))
    g = _dot2_stacked(_sigmoid(xg), g2_ref[...])
    bd = bd_ref[...]
    kkr = k * kk_ref[...]
    n2 = _head_sums(kkr * kkr, bd)
    kk = kkr * lax.rsqrt(jnp.maximum(n2, 1e-24))
    k2 = k * (1.0 + (a - 1.0) * ka_ref[...])
    bonus = _head_sums(r * k2 * rk_ref[...], bd) * v
    r_out[...] = r
    k_out[...] = k2
    v_out[...] = v
    kk_out[...] = kk
    kka_out[...] = kk * a
    lw_out[...] = lw
    g_out[...] = g
    bonus_out[...] = bonus
    sgb_out[...] = _sigmoid(jnp.dot(xn, wgb_ref[...], preferred_element_type=F32))


def _branch_b(x3d, tm, seq_rows, ext, g1, wcur, wgb, mu, w0, w2, a0, a2, g2, k_k, k_a, r_k, bd):
    nb, t, d = x3d.shape
    d_b = w0.shape[1]
    shift_w = wcur.shape[1]
    nt = t // tm
    tok = lambda w: pl.BlockSpec((None, tm, w), lambda b, i: (b, i, 0))
    cur_rows = 8 if seq_rows is None else tm
    outs = [jax.ShapeDtypeStruct((nb, t, d_b), F32)] * 8 + [
        jax.ShapeDtypeStruct((nb, t, d), F32),
        jax.ShapeDtypeStruct((nb, nt * cur_rows, shift_w), F32)]
    out_specs = [tok(d_b)] * 8 + [tok(d), pl.BlockSpec((None, cur_rows, shift_w),
                                                       lambda b, i: (b, i, 0))]
    weights = (g1, wcur, wgb, mu)
    small = (w0, w2, a0, a2, g2, k_k, k_a, r_k, bd)
    return pl.pallas_call(
        functools.partial(_branch_b_kernel, seq_rows=seq_rows, d_b=d_b),
        grid=(nb, nt),
        in_specs=[tok(d)] + [_full_spec(w.shape) for w in weights]
        + [tok(shift_w) if seq_rows is not None else _full_spec(ext.shape)]
        + [_full_spec(w.shape) for w in small],
        out_specs=out_specs,
        out_shape=outs,
        scratch_shapes=[pltpu.VMEM((1, shift_w), F32)],
        compiler_params=pltpu.CompilerParams(dimension_semantics=("parallel", "arbitrary"),
                                             vmem_limit_bytes=VMEM_LIMIT),
        name="branch_b",
    )(x3d, *weights, ext, *small)


def _scan_kernel(r_ref, k_ref, v_ref, kk_ref, kka_ref, lw_ref, y_ref, sout_ref, s_scr, *, chunk):
    L = chunk
    W2 = 2 * L
    c = pl.program_id(1)
    n_seq, n_pairs = s_scr.shape[:2]

    @pl.when(c == 0)
    def _():
        s_scr[...] = jnp.zeros_like(s_scr)

    iota = lambda shape, dim: lax.broadcasted_iota(jnp.int32, shape, dim)
    stack_mask = (iota((W2, 1), 0) >= L) == (iota((1, PAIR), 1) >= HEAD_B)
    bd_mask = (iota((W2, 1), 0) >= L) == (iota((1, W2), 1) >= L)
    assert L & (L - 1) == 0, "chunk length must be a power of two"
    rw = iota((L, W2), 0)
    cw = iota((L, W2), 1) & (L - 1)
    strict_w = cw < rw
    eye_w = (cw == rw).astype(F32)
    incl_w2 = (iota((L, 2 * W2), 1) & (L - 1)) <= iota((L, 2 * W2), 0)
    tri = (iota((L, L), 1) <= iota((L, L), 0)).astype(BF16)
    di = iota((PAIR, PAIR), 0)
    dj = iota((PAIR, PAIR), 1)
    diag = di == dj
    head_diag = (di >= HEAD_B) == (dj >= HEAD_B)

    def stack(x):
        return jnp.where(stack_mask, jnp.concatenate([x, x], axis=0), 0.0)

    def bd(xw):
        return jnp.where(bd_mask, jnp.concatenate([xw, xw], axis=0), 0.0)

    n_double = max(L.bit_length() - 2, 0)
    prep = []
    for s in range(n_seq):
        lw_all = lw_ref[s]
        cum_all = _dot_exact_lhs(tri, lw_all)
        for p in range(n_pairs):
            sl = slice(p * PAIR, (p + 1) * PAIR)
            cum = cum_all[:, sl]
            cum_last = cum[L - 1:L, :]
            e_neg = jnp.exp(-cum)
            e_rel = jnp.exp(cum_last - cum)
            kk = kk_ref[s, :, sl]
            kka = kka_ref[s, :, sl]
            kx = k_ref[s, :, sl]
            prep.append(dict(
                s=s, p=p, sl=sl, ab=-kk * jnp.exp(cum - lw_all[:, sl]),
                rb=r_ref[s, :, sl] * jnp.exp(cum), bb=kka * e_neg, kb=kx * e_neg,
                bt=kka * e_rel, kt=kx * e_rel, v=v_ref[s, :, sl], decay=jnp.exp(cum_last)))
    a_w = [lax.dot_general(
        jnp.concatenate([q["ab"], q["rb"]], axis=0).astype(BF16),
        jnp.concatenate([stack(q["bb"]), stack(q["kb"])], axis=0).astype(BF16),
        (((1,), (1,)), ((), ())), preferred_element_type=F32) for q in prep]
    nw = [jnp.where(strict_w, a[:L, :W2], 0.0) for a in a_w]
    tw = [eye_w + n for n in nw]
    if n_double:
        sw = [_dot(n, bd(n)) for n in nw]
        for _ in range(n_double - 1):
            prod = [_dot(jnp.concatenate([t, s], axis=0), bd(s)) for t, s in zip(tw, sw)]
            tw = [t + p[:L] for t, p in zip(tw, prod)]
            sw = [p[L:] for p in prod]
        tw = [t + _dot(t, bd(s)) for t, s in zip(tw, sw)]
    akv = [_dot(jnp.where(strict_w, a[:L, W2:], 0.0), stack(q["v"])) for a, q in zip(a_w, prep)]
    tx = [_dot(t, jnp.concatenate([stack(q["ab"]), stack(kv)], axis=1))
          for t, q, kv in zip(tw, prep, akv)]
    ry = [_dot(jnp.where(incl_w2, a[L:], 0.0),
               jnp.concatenate(
                   [jnp.concatenate([stack(x[:, :PAIR]), stack(x[:, PAIR:])], axis=1),
                    jnp.concatenate([jnp.zeros((W2, PAIR), F32), stack(q["v"])], axis=1)], axis=0))
          for a, x, q in zip(a_w, tx, prep)]
    for q, x, y in zip(prep, tx, ry):
        st = s_scr[q["s"], q["p"]]
        us = _dot(jnp.concatenate([x[:, :PAIR], q["rb"] + y[:, :PAIR]], axis=0), st)
        y_ref[q["s"], :, q["sl"]] = us[L:] + y[:, PAIR:]
        u = us[:L] + x[:, PAIR:]
        lhs = jnp.concatenate([jnp.where(diag, q["decay"], 0.0),
                               jnp.concatenate([q["bt"], q["kt"]], axis=0).T], axis=1)
        g = _dot(lhs, jnp.concatenate([st, u, q["v"]], axis=0))
        s_scr[q["s"], q["p"]] = jnp.where(head_diag, g, 0.0)

    @pl.when(c == pl.num_programs(1) - 1)
    def _():
        for s in range(n_seq):
            for p in range(n_pairs):
                nat = s_scr[s, p].T
                sout_ref[s, 2 * p] = nat[:HEAD_B, :HEAD_B]
                sout_ref[s, 2 * p + 1] = nat[HEAD_B:, HEAD_B:]


SCAN_CHUNK = 64
SCAN_SEQS = 8


def _scan(r, k, v, kk, kka, lw):
    nb, t, d_b = r.shape
    n_pairs = d_b // PAIR
    n_heads = d_b // HEAD_B
    tok = pl.BlockSpec((SCAN_SEQS, SCAN_CHUNK, d_b), lambda b, c: (b, c, 0))
    st = pl.BlockSpec((SCAN_SEQS, n_heads, HEAD_B, HEAD_B), lambda b, c: (b, 0, 0, 0))
    return pl.pallas_call(
        functools.partial(_scan_kernel, chunk=SCAN_CHUNK),
        grid=(nb // SCAN_SEQS, t // SCAN_CHUNK),
        in_specs=[tok] * 6,
        out_specs=[tok, st],
        out_shape=[jax.ShapeDtypeStruct((nb, t, d_b), F32),
                   jax.ShapeDtypeStruct((nb, n_heads, HEAD_B, HEAD_B), F32)],
        scratch_shapes=[pltpu.VMEM((SCAN_SEQS, n_pairs, PAIR, PAIR), F32)],
        compiler_params=pltpu.CompilerParams(dimension_semantics=("parallel", "arbitrary"),
                                             vmem_limit_bytes=VMEM_LIMIT),
        name="rwkv_scan",
    )(r, k, v, kk, kka, lw)


STEP_ROWS = 8


def _steps_kernel(r_ref, k_ref, v_ref, kk_ref, kka_ref, lw_ref, s0_ref, y_ref, s_ref,
                  vec_scr, yt_scr, *, n_steps):
    n_seq = s_ref.shape[-1]
    s_ref[...] = s0_ref[...]
    for t in range(n_steps):
        rows = pl.ds(t, n_seq, stride=n_steps)
        vecs = {}
        for slot, (name, ref) in enumerate((("r", r_ref), ("k", k_ref), ("v", v_ref),
                                            ("kk", kk_ref), ("kka", kka_ref), ("lw", lw_ref))):
            vecs[name] = ref[rows, :].T
            vec_scr[slot] = vecs[name]
        for q in range(PAIR // HEAD_B):
            hs = slice(q * HEAD_B, (q + 1) * HEAD_B)
            r_q, k_q, kka_q = vecs["r"][hs], vecs["k"][hs], vecs["kka"][hs]
            nkk_q = -vecs["kk"][hs]
            w_q = jnp.exp(vecs["lw"][hs])

            def rows_step(i0, carry):
                for u in range(STEP_ROWS):
                    i = i0 * STEP_ROWS + u
                    s_row = s_ref[q, i]
                    sa = jnp.sum(s_row * nkk_q, axis=0, keepdims=True)
                    v_i = vec_scr[2, pl.ds(q * HEAD_B + i, 1), :]
                    s_new = s_row * w_q + sa * kka_q + v_i * k_q
                    s_ref[q, i] = s_new
                    yt_scr[pl.ds(q * HEAD_B + i, 1), :] = jnp.sum(s_new * r_q, axis=0,
                                                                  keepdims=True)
                return carry

            lax.fori_loop(0, HEAD_B // STEP_ROWS, rows_step, 0)
        y_ref[rows, :] = yt_scr[...].T


def _scan_steps(r, k, v, kk, kka, lw, s0, n_steps):
    n, d_b = r.shape
    n_heads, _, _, n_seq = s0.shape
    tok = pl.BlockSpec((n, PAIR), lambda p: (0, p))
    st = pl.BlockSpec((PAIR // HEAD_B, HEAD_B, HEAD_B, n_seq), lambda p: (p, 0, 0, 0))
    return pl.pallas_call(
        functools.partial(_steps_kernel, n_steps=n_steps),
        grid=(d_b // PAIR,),
        in_specs=[tok] * 6 + [st],
        out_specs=[tok, st],
        out_shape=[jax.ShapeDtypeStruct((n, d_b), F32), jax.ShapeDtypeStruct(s0.shape, F32)],
        scratch_shapes=[pltpu.VMEM((6, PAIR, n_seq), F32), pltpu.VMEM((PAIR, n_seq), F32)],
        compiler_params=pltpu.CompilerParams(dimension_semantics=("parallel",),
                                             vmem_limit_bytes=VMEM_LIMIT),
        name="rwkv_steps",
    )(r, k, v, kk, kka, lw, s0)


def _pack_bf16_pairs(x):
    w = x.shape[1] // 2
    bits = lambda v: lax.bitcast_convert_type(v.astype(BF16).astype(F32), jnp.uint32)
    return (bits(x[:, :w]) >> 16) | (bits(x[:, w:]) & jnp.uint32(0xFFFF0000))


def _unpack_bf16_pairs(u):
    lo = lax.bitcast_convert_type(u << 16, F32)
    hi = lax.bitcast_convert_type(u & jnp.uint32(0xFFFF0000), F32)
    return lo, hi


def _merge_kernel(y_ref, g_ref, bonus_ref, sgb_ref, apart_ref, x_ref, lg_ref, lb_ref, bd_ref,
                  wbo_ref, wo_ref, g2_ref, rwt_ref, rb_ref,
                  h_out, xp_out, topi_out, prob_out, rank_out, cnt_out, cnt_scr):
    step = pl.program_id(0)

    @pl.when(step == 0)
    def _():
        cnt_scr[...] = jnp.zeros_like(cnt_scr)

    bd = bd_ref[...]
    y = y_ref[...]
    inv_n = 1.0 / HEAD_B
    mu = _head_sums(y, bd) * inv_n
    d = y - mu
    var = _head_sums(d * d, bd) * inv_n
    yn = d * lax.rsqrt(var + GN_EPS) * lg_ref[...] + lb_ref[...]
    yb = ((yn + bonus_ref[...]) * g_ref[...]).astype(BF16)
    merged = apart_ref[...] + sgb_ref[...] * jnp.dot(yb, wbo_ref[...], preferred_element_type=F32)
    h = x_ref[...] + jnp.dot(merged.astype(BF16), wo_ref[...], preferred_element_type=F32)
    h_out[...] = h
    xn2 = _rms(h, g2_ref[...])
    xp_out[...] = _pack_bf16_pairs(xn2)
    n_e = rb_ref.shape[0]
    tm = xn2.shape[0]
    xh, xl = _split(xn2)
    rwt = rwt_ref[...]
    nt = (((1,), (1,)), ((), ()))
    t = lax.dot_general(rwt, xh, nt, preferred_element_type=F32)
    logits = (t[:n_e] + t[n_e:] + lax.dot_general(rwt[:n_e], xl, nt, preferred_element_type=F32)
              + rb_ref[...])
    idx = lax.broadcasted_iota(jnp.int32, logits.shape, 0).astype(F32)
    work = logits
    tops, hots, sels = [], [], []
    for _ in range(TOP_K):
        m = jnp.max(work, axis=0, keepdims=True)
        sel = jnp.min(jnp.where(work == m, idx, float(n_e)), axis=0, keepdims=True)
        hot = idx == sel
        tops.append(m)
        hots.append(hot)
        sels.append(sel)
        work = jnp.where(hot, -jnp.inf, work)
    es = [jnp.exp(t - tops[0]) for t in tops]
    denom = es[0] + es[1] + es[2] + es[3]
    topi_out[...] = jnp.concatenate(sels, axis=0).astype(jnp.int32)
    prob_out[...] = jnp.concatenate([e / denom for e in es], axis=0)
    hot_any = jnp.zeros_like(logits)
    for hot in hots:
        hot_any = hot_any + hot.astype(F32)
    ri = lax.broadcasted_iota(jnp.int32, (tm, tm), 0)
    ci = lax.broadcasted_iota(jnp.int32, (tm, tm), 1)
    before = _dot(hot_any, (ri < ci).astype(BF16)) + cnt_scr[...]
    rank_out[...] = jnp.concatenate(
        [jnp.sum(jnp.where(hot, before, 0.0), axis=0, keepdims=True) for hot in hots],
        axis=0).astype(jnp.int32)
    total = cnt_scr[...] + jnp.sum(hot_any, axis=1, keepdims=True)
    cnt_scr[...] = total
    cnt_out[...] = total.astype(jnp.int32)


def _merge(y, g, bonus, sgb, apart, x2d, tm, lnx_g, lnx_b, bd, wbo, wo, g2n, rw, rb):
    n, d = x2d.shape
    d_b = y.shape[1]
    n_e = rb.shape[0]
    tok = lambda w: pl.BlockSpec((tm, w), lambda i: (i, 0))
    per_k = pl.BlockSpec((TOP_K, tm), lambda i: (0, i))
    weights = (lnx_g, lnx_b, bd, wbo, wo, g2n, rw, rb)
    return pl.pallas_call(
        _merge_kernel,
        grid=(n // tm,),
        in_specs=[tok(d_b), tok(d_b), tok(d_b), tok(d), tok(d), tok(d)]
        + [_full_spec(w.shape) for w in weights],
        out_specs=[tok(d), tok(d // 2), per_k, per_k, per_k,
                   pl.BlockSpec((n_e, 1), lambda i: (0, 0))],
        out_shape=[jax.ShapeDtypeStruct((n, d), F32),
                   jax.ShapeDtypeStruct((n, d // 2), jnp.uint32),
                   jax.ShapeDtypeStruct((TOP_K, n), jnp.int32),
                   jax.ShapeDtypeStruct((TOP_K, n), F32),
                   jax.ShapeDtypeStruct((TOP_K, n), jnp.int32),
                   jax.ShapeDtypeStruct((n_e, 1), jnp.int32)],
        scratch_shapes=[pltpu.VMEM((n_e, 1), F32)],
        compiler_params=pltpu.CompilerParams(dimension_semantics=("arbitrary",),
                                             vmem_limit_bytes=VMEM_LIMIT),
        name="merge_router",
    )(y, g, bonus, sgb, apart, x2d, *weights)


MOE_TM = 1024
MOE_SUB = 128
MOE_FF_CHUNK = 512
CAST_ROWS = 256


def _route(topi_p, rank_p, cnt_p, topi_s, rank_s, cnt_s):
    n_e = cnt_p.size
    cnt_p, cnt_s = cnt_p.reshape(n_e), cnt_s.reshape(n_e)
    cnt = cnt_p + cnt_s
    padded = (cnt + MOE_TM - 1) // MOE_TM * MOE_TM
    ends = jnp.cumsum(padded)
    base = ends - padded
    lookup = lambda table, idx: jnp.sum(
        jnp.where(idx[..., None] == jnp.arange(n_e), table, 0), axis=-1)
    pos_t = jnp.concatenate([lookup(base, topi_p) + rank_p,
                             lookup(base + cnt_p, topi_s) + rank_s], axis=1)
    return (pos_t.astype(jnp.int32), (base // MOE_TM).astype(jnp.int32),
            cnt.astype(jnp.int32))


def _moe_kernel(tile0_ref, cnt_ref, xs_hbm, wgu_ref, bgu_ref, wd_ref, bdn_ref, ys_hbm,
                wgu_b, wd_b, xbuf, ybuf, sem_in, sem_out, done_ref):
    e = pl.program_id(0)
    n_e = pl.num_programs(0)

    @pl.when(e == 0)
    def _():
        done_ref[0] = 0

    g0 = done_ref[0]
    tile0 = tile0_ref[e]
    cnt = cnt_ref[e]
    n_t = (cnt + MOE_TM - 1) // MOE_TM
    d, gu = wgu_ref.shape
    d_ff = wd_ref.shape[0]
    half = d // 2

    def in_copy(tile, slot):
        return pltpu.make_async_copy(xs_hbm.at[pl.ds(tile * MOE_TM, MOE_TM)], xbuf.at[slot],
                                     sem_in.at[slot])

    def out_copy(tile, slot):
        return pltpu.make_async_copy(ybuf.at[slot], ys_hbm.at[pl.ds(tile * MOE_TM, MOE_TM)],
                                     sem_out.at[slot])

    @pl.when(jnp.logical_and(e == 0, n_t > 0))
    def _():
        in_copy(tile0, 0).start()

    @pl.when(n_t > 0)
    def _():
        for r0 in range(0, d, CAST_ROWS):
            wgu_b[r0:r0 + CAST_ROWS, :] = wgu_ref[r0:r0 + CAST_ROWS, :].astype(BF16)
        for r0 in range(0, d_ff, CAST_ROWS):
            wd_b[r0:r0 + CAST_ROWS, :] = wd_ref[r0:r0 + CAST_ROWS, :].astype(BF16)

    def tile_step(j, carry):
        slot = (g0 + j) % 2
        in_copy(tile0 + j, slot).wait()

        @pl.when(j + 1 < n_t)
        def _():
            in_copy(tile0 + j + 1, 1 - slot).start()

        @pl.when(g0 + j >= 2)
        def _():
            out_copy(0, slot).wait()

        left = cnt - j * MOE_TM

        def ffn(row0, rows):
            lo, hi = _unpack_bf16_pairs(xbuf[slot, pl.ds(row0, rows)])
            valid = lax.broadcasted_iota(jnp.int32, lo.shape, 0) < left - row0
            x_lo = jnp.where(valid, lo, 0.0).astype(BF16)
            x_hi = jnp.where(valid, hi, 0.0).astype(BF16)

            def proj(c0):
                cols = slice(c0, c0 + MOE_FF_CHUNK)
                return (jnp.dot(x_lo, wgu_b[:half, cols], preferred_element_type=F32)
                        + jnp.dot(x_hi, wgu_b[half:, cols], preferred_element_type=F32)
                        + bgu_ref[:, cols])

            y = jnp.zeros((rows, d), F32)
            for f in range(d_ff // MOE_FF_CHUNK):
                c0 = f * MOE_FF_CHUNK
                gate = jnp.minimum(proj(c0), SWIGLU_LIMIT)
                up = jnp.clip(proj(d_ff + c0), -SWIGLU_LIMIT, SWIGLU_LIMIT)
                hh = (up + 1.0) * gate * _sigmoid(gate * SWIGLU_ALPHA)
                y = y + jnp.dot(hh.astype(BF16), wd_b[c0:c0 + MOE_FF_CHUNK, :],
                                preferred_element_type=F32)
            ybuf[slot, pl.ds(row0, rows)] = _pack_bf16_pairs(y + bdn_ref[...])

        n_sub = (jnp.minimum(left, MOE_TM) + MOE_SUB - 1) // MOE_SUB

        @pl.when(n_sub == MOE_TM // MOE_SUB)
        def _():
            ffn(0, MOE_TM)

        @pl.when(n_sub < MOE_TM // MOE_SUB)
        def _():
            def sub_step(i, c):
                ffn(pl.multiple_of(i * 2 * MOE_SUB, 2 * MOE_SUB), 2 * MOE_SUB)
                return c
            lax.fori_loop(0, n_sub // 2, sub_step, 0)

            @pl.when(n_sub % 2 == 1)
            def _():
                ffn(pl.multiple_of((n_sub // 2) * 2 * MOE_SUB, 2 * MOE_SUB), MOE_SUB)

        out_copy(tile0 + j, slot).start()
        return carry

    lax.fori_loop(0, n_t, tile_step, 0)

    g1 = g0 + n_t
    done_ref[0] = g1
    nxt = jnp.minimum(e + 1, n_e - 1)

    @pl.when(jnp.logical_and(e + 1 < n_e, cnt_ref[nxt] > 0))
    def _():
        in_copy(tile0_ref[nxt], g1 % 2).start()

    @pl.when(jnp.logical_and(e == n_e - 1, g1 >= 2))
    def _():
        out_copy(0, g1 % 2).wait()

    @pl.when(jnp.logical_and(e == n_e - 1, g1 >= 1))
    def _():
        out_copy(0, (g1 - 1) % 2).wait()


def _moe(xs, tile0, cnt, wgu, bgu, wd, bdn):
    p_rows, half = xs.shape
    n_e, d, gu = wgu.shape
    d_ff = wd.shape[1]
    w_blk = lambda e, tile0, cnt: (e, 0, 0)
    grid_spec = pltpu.PrefetchScalarGridSpec(
        num_scalar_prefetch=2,
        grid=(n_e,),
        in_specs=[pl.BlockSpec(memory_space=pl.ANY),
                  pl.BlockSpec((None, d, gu), w_blk),
                  pl.BlockSpec((None, 1, gu), w_blk),
                  pl.BlockSpec((None, d_ff, d), w_blk),
                  pl.BlockSpec((None, 1, d), w_blk)],
        out_specs=pl.BlockSpec(memory_space=pl.ANY),
        scratch_shapes=[pltpu.VMEM((d, gu), BF16), pltpu.VMEM((d_ff, d), BF16),
                        pltpu.VMEM((2, MOE_TM, half), jnp.uint32),
                        pltpu.VMEM((2, MOE_TM, half), jnp.uint32),
                        pltpu.SemaphoreType.DMA((2,)), pltpu.SemaphoreType.DMA((2,)),
                        pltpu.SMEM((1,), jnp.int32)])
    return pl.pallas_call(
        _moe_kernel,
        grid_spec=grid_spec,
        out_shape=jax.ShapeDtypeStruct((p_rows, half), jnp.uint32),
        compiler_params=pltpu.CompilerParams(dimension_semantics=("arbitrary",),
                                             vmem_limit_bytes=VMEM_LIMIT),
        name="moe_experts",
    )(tile0, cnt, xs, wgu, bgu, wd, bdn)


def _combine_kernel(h_ref, yg_ref, prob_ref, gf_ref, out_ref):
    tm = h_ref.shape[0]
    half = h_ref.shape[1] // 2
    p_rows = jnp.concatenate([prob_ref[...], jnp.zeros((8 - TOP_K, tm), F32)], axis=0)
    ph = p_rows.astype(BF16)
    pm, pl_ = _split(p_rows - ph.astype(F32))
    eye = (lax.broadcasted_iota(jnp.int32, (tm, tm), 0)
           == lax.broadcasted_iota(jnp.int32, (tm, tm), 1)).astype(BF16)
    nt = (((1,), (1,)), ((), ()))
    prob = sum(lax.dot_general(eye, piece, nt, preferred_element_type=F32)
               for piece in (ph, pm, pl_))
    acc_lo = jnp.zeros((tm, half), F32)
    acc_hi = jnp.zeros((tm, half), F32)
    for k in range(TOP_K):
        lo, hi = _unpack_bf16_pairs(yg_ref[k])
        pk = prob[:, k:k + 1]
        acc_lo = acc_lo + pk * lo
        acc_hi = acc_hi + pk * hi
    z = h_ref[...] + jnp.concatenate([acc_lo, acc_hi], axis=1)
    out_ref[...] = _rms(z, gf_ref[...])


def _combine(h, yg, row0, prob, gf, tm):
    n, d = h.shape
    tok = lambda w: pl.BlockSpec((tm, w), lambda i: (i, 0))
    t0 = row0 // tm
    return pl.pallas_call(
        _combine_kernel,
        grid=(n // tm,),
        in_specs=[tok(d), pl.BlockSpec((TOP_K, tm, d // 2), lambda i: (0, i + t0, 0)),
                  pl.BlockSpec((TOP_K, tm), lambda i: (0, i)), _full_spec(gf.shape)],
        out_specs=tok(d),
        out_shape=jax.ShapeDtypeStruct((n, d), F32),
        compiler_params=pltpu.CompilerParams(dimension_semantics=("parallel",),
                                             vmem_limit_bytes=VMEM_LIMIT),
        name="moe_combine",
    )(h, yg, prob, gf)


SC_CORES = 2
SC_SUBCORES = 16
SC_WORKERS = SC_CORES * SC_SUBCORES
SC_MAX_INDEX = 128


def _sc_chunk(rows_per_worker, limit=SC_MAX_INDEX):
    for c in range(limit, 7, -8):
        if rows_per_worker % c == 0:
            return c
    raise ValueError(f"no 8-aligned chunk divides {rows_per_worker} rows")


def _sc_mesh():
    return plsc.VectorSubcoreMesh(core_axis_name="c", subcore_axis_name="s")


def _sc_worker():
    return lax.axis_index("s") * SC_CORES + lax.axis_index("c")


def _scatter_rows(xs_in, pos_t, p_rows):
    w = xs_in[0].shape[1]
    dtype = xs_in[0].dtype
    n_k, n = pos_t.shape
    pos_flat = pos_t.reshape(n_k * n)
    parts, row0 = [], 0
    for x in xs_in:
        per = x.shape[0] // SC_WORKERS
        parts.append((row0, per, _sc_chunk(per, SC_MAX_INDEX // 2)))
        row0 += x.shape[0]
    max_chunk = max(c for _, _, c in parts)
    max_per = max(p for _, p, _ in parts)

    def body(*refs):
        x_hbms = refs[:len(xs_in)]
        pos_hbm, out_hbm, idx_v, rows_v, sem_l, sem_s = refs[len(xs_in):]
        for x_hbm, (tok0, per, chunk) in zip(x_hbms, parts):
            n_chunks = per // chunk
            base = pl.multiple_of(_sc_worker() * per, 8)
            for k in range(n_k):
                src = pl.multiple_of(k * n + tok0 + base, 8)
                pltpu.sync_copy(pos_hbm.at[pl.ds(src, per)], idx_v.at[pl.ds(k * per, per)])

            def load(c, slot):
                src = x_hbm.at[pl.ds(pl.multiple_of(base + c * chunk, 8), chunk)]
                return pltpu.make_async_copy(src, rows_v.at[slot, pl.ds(0, chunk)], sem_l.at[slot])

            def scatter(c, k, slot):
                rows = idx_v.at[pl.ds(pl.multiple_of(k * per + c * chunk, 8), chunk)]
                return pltpu.make_async_copy(rows_v.at[slot, pl.ds(0, chunk)], out_hbm.at[rows],
                                             sem_s.at[slot])

            load(0, 0).start()

            @pl.loop(0, n_chunks)
            def _(c):
                slot = c % 2
                load(c, slot).wait()

                @pl.when(c + 1 < n_chunks)
                def _():
                    @pl.when(c >= 1)
                    def _():
                        for k in range(n_k):
                            scatter(c - 1, k, 1 - slot).wait()
                    load(c + 1, 1 - slot).start()

                for k in range(n_k):
                    scatter(c, k, slot).start()

            if n_chunks >= 2:
                for k in range(n_k):
                    scatter(n_chunks - 2, k, n_chunks % 2).wait()
            for k in range(n_k):
                scatter(n_chunks - 1, k, (n_chunks - 1) % 2).wait()

    return pl.kernel(
        body, out_type=jax.ShapeDtypeStruct((p_rows, w), dtype), mesh=_sc_mesh(),
        scratch_types=[pltpu.VMEM((n_k * max_per,), jnp.int32),
                       pltpu.VMEM((2, max_chunk, w), dtype),
                       pltpu.SemaphoreType.DMA((2,)), pltpu.SemaphoreType.DMA((2,))],
        name="sc_scatter_rows")(*xs_in, pos_flat)


def _gather_rows(table, idx):
    n = idx.shape[0]
    w = table.shape[1]
    per = n // SC_WORKERS
    chunk = _sc_chunk(per)

    n_chunks = per // chunk

    def body(table_hbm, idx_hbm, out_hbm, idx_v, rows_v, sem_g, sem_w):
        base = pl.multiple_of(_sc_worker() * per, 8)
        pltpu.sync_copy(idx_hbm.at[pl.ds(base, per)], idx_v)

        def gather(c, slot):
            rows = idx_v.at[pl.ds(pl.multiple_of(c * chunk, 8), chunk)]
            return pltpu.make_async_copy(table_hbm.at[rows], rows_v.at[slot], sem_g.at[slot])

        def write(c, slot):
            dst = out_hbm.at[pl.ds(pl.multiple_of(base + c * chunk, 8), chunk)]
            return pltpu.make_async_copy(rows_v.at[slot], dst, sem_w.at[slot])

        gather(0, 0).start()

        @pl.loop(0, n_chunks)
        def _(c):
            slot = c % 2
            gather(c, slot).wait()

            @pl.when(c + 1 < n_chunks)
            def _():
                @pl.when(c >= 1)
                def _():
                    write(c - 1, 1 - slot).wait()
                gather(c + 1, 1 - slot).start()

            write(c, slot).start()

        if n_chunks >= 2:
            write(n_chunks - 2, n_chunks % 2).wait()
        write(n_chunks - 1, (n_chunks - 1) % 2).wait()

    return pl.kernel(
        body, out_type=jax.ShapeDtypeStruct((n, w), table.dtype), mesh=_sc_mesh(),
        scratch_types=[pltpu.VMEM((per,), jnp.int32), pltpu.VMEM((2, chunk, w), table.dtype),
                       pltpu.SemaphoreType.DMA((2,)), pltpu.SemaphoreType.DMA((2,))],
        name="sc_gather_rows")(table, idx)


def _mix_matrix(ws, block, seq_rows):
    causal = jnp.tril(jnp.ones((CHUNK, CHUNK), dtype=bool))
    w = jnp.where(causal[None], ws, 0.0)[:, :seq_rows, :seq_rows]
    eye = jnp.eye(block // seq_rows, dtype=ws.dtype)
    return jnp.einsum("ab,gts->gatbs", eye, w).reshape(ws.shape[0], block, block).astype(BF16)


TOKEN_TILE = 512


def _stream(x, shift_in, wkv_in, p, seq_rows):
    nb, t, d = x.shape
    n = nb * t
    tm = TOKEN_TILE
    x2d = x.reshape(n, d)
    rows = CHUNK if seq_rows is None else seq_rows
    wmix = _mix_matrix(p["w_spatial"], max(rows, LANES), rows)
    pos = jnp.arange(tm) % rows
    d_a = p["wu"].shape[1]
    bias_full = jnp.repeat(p["b_spatial"].T[pos], d_a // G_A, axis=1)
    apart, *vn = _branch_a(x2d, tm, p["norm1_g"], p["wu"], p["wv"], p["wga"], p["vnorm_g"],
                           p["vnorm_b"], wmix, bias_full, p["w_a_out"], seq_rows is not None)
    vn = vn[0] if vn else None
    if seq_rows is None:
        xb, ext = x, jnp.zeros((1, p["wcur"].shape[1]), F32)
    else:
        xb = x2d.reshape(n // tm, tm, d)
        ext = jnp.repeat(shift_in, seq_rows, axis=0).reshape(n // tm, tm, -1)
    outs = _branch_b(xb, tm, seq_rows, ext, p["norm1_g"], p["wcur"], p["wgb"], p["mu_shift"],
                     p["w0"], p["w2"], p["a0"], p["a2"], p["g2"], p["k_k"], p["k_a"], p["r_k"],
                     p["bd"])
    r, k2, v, kk, kka, lw, g, bonus, sgb, cur = outs
    d_b = r.shape[-1]
    if seq_rows is None:
        shift_out = cur[:, -1, :]
        y, s_out = _scan(r, k2, v, kk, kka, lw)
        y2d = y.reshape(n, d_b)
    else:
        shift_out = cur.reshape(nb, t, -1)[:, -1, :]
        y2d, s_lanes = _scan_steps(*[a.reshape(n, d_b) for a in (r, k2, v, kk, kka, lw)],
                                   jnp.transpose(wkv_in, (1, 2, 3, 0)), t)
        s_out = jnp.transpose(s_lanes, (3, 0, 1, 2))
    flat = lambda a: a.reshape(n, a.shape[-1])
    routed = _merge(y2d, flat(g), flat(bonus), flat(sgb), apart, x2d, tm, p["lnx_g"],
                    p["lnx_b"], p["bd"], p["w_b_out"], p["w_out"], p["norm2_g"],
                    p["router_w"], p["router_b"])
    return routed, vn, shift_out, s_out


def kernel(x_prompt, x_sample, state_shift, state_wkv, norm1_g, w_in, mu_shift, vnorm_g, vnorm_b, w_spatial, b_spatial, w_a_out, w0, w2, a0, a2, g2, k_k, k_a, r_k, lnx_g, lnx_b, w_b_out, w_out, norm2_g, router_w, router_b, exp_w_gu, exp_b_gu, exp_w_down, exp_b_down, normf_g):
    assert w_in.shape[0] == 1, "one layer: the final norm is fused into its MoE combine"
    l = 0
    d_model = x_prompt.shape[-1]
    shift_w = mu_shift.shape[-1]
    d_a = vnorm_g.shape[-1]
    bp, tp, _ = x_prompt.shape
    bs, ts, _ = x_sample.shape
    head_id = jnp.arange(2 * LANES) // HEAD_B
    bd = (head_id[:, None] == head_id[None, :]).astype(BF16)
    row = lambda a: a.reshape(1, -1)

    wi = w_in[l].astype(BF16)
    o = shift_w
    p = dict(
        norm1_g=row(norm1_g[l]), wcur=wi[:, :o], wu=wi[:, o:o + d_a],
        wv=wi[:, o + d_a:o + 2 * d_a], wga=wi[:, o + 2 * d_a:o + 2 * d_a + d_model],
        wgb=wi[:, o + 2 * d_a + d_model:], mu_shift=row(mu_shift[l]),
        vnorm_g=row(vnorm_g[l]), vnorm_b=row(vnorm_b[l]), w_spatial=w_spatial[l],
        b_spatial=b_spatial[l], w_a_out=w_a_out[l].astype(BF16), w0=row(w0[l]),
        w2=_stack_rhs3(w2[l]), a0=row(a0[l]), a2=_stack_rhs3(a2[l]),
        g2=jnp.concatenate([g2[l].astype(BF16)] * 2, axis=0),
        k_k=row(k_k[l]), k_a=row(k_a[l]),
        r_k=row(r_k[l]), lnx_g=row(lnx_g[l]), lnx_b=row(lnx_b[l]),
        w_b_out=w_b_out[l].astype(BF16), w_out=w_out[l].astype(BF16),
        norm2_g=row(norm2_g[l]), router_w=jnp.concatenate(_split(router_w[l].T), axis=0),
        router_b=router_b[l].reshape(-1, 1), bd=bd)

    routed_p, _, sh_p, s_p = _stream(x_prompt, None, None, p, None)
    routed_s, vn_s, sh_s, s_s = _stream(x_sample, state_shift[l], state_wkv[l], p, ts)
    h_p, xp_p, topi_p, prob_p, rank_p, cnt_p = routed_p
    h_s, xp_s, topi_s, prob_s, rank_s, cnt_s = routed_s

    n_p, n_s = h_p.shape[0], h_s.shape[0]
    n_e = router_w.shape[-1]
    n_tiles = (n_p + n_s) * TOP_K // MOE_TM + n_e
    pos_t, tile0, cnt = _route(topi_p, rank_p, cnt_p, topi_s, rank_s, cnt_s)
    xs = _scatter_rows([xp_p, xp_s], pos_t, n_tiles * MOE_TM)
    ys = _moe(xs, tile0, cnt, exp_w_gu[l], exp_b_gu[l][:, None, :],
              exp_w_down[l], exp_b_down[l][:, None, :])
    yg = _gather_rows(ys, pos_t.reshape(-1)).reshape(TOP_K, n_p + n_s, d_model // 2)
    gf = row(normf_g)
    y_p = _combine(h_p, yg, 0, prob_p, gf, TOKEN_TILE).reshape(bp, tp, d_model)
    y_s = _combine(h_s, yg, n_p, prob_s, gf, TOKEN_TILE).reshape(bs, ts, d_model)
    return (y_p, y_s, sh_p[None], s_p[None], sh_s[None], s_s[None],
            vn_s.reshape(1, bs, ts, d_a))
```

```python
import functools
import math

import jax
import jax.numpy as jnp
from jax import lax
from jax.experimental import pallas as pl
from jax.experimental.pallas import tpu as pltpu
from jax.experimental.pallas import tpu_sc as plsc

F32 = jnp.float32
BF16 = jnp.bfloat16

CHUNK = 128
G_A = 8
HEAD_B = 64
R_W, R_A, R_G = 64, 64, 128
TOP_K = 4
SWIGLU_LIMIT = 7.0
SWIGLU_ALPHA = 1.702
EPS = 1e-5
GN_EPS = HEAD_B * 1e-5

LANES = 128
PAIR = 2 * HEAD_B
VMEM_LIMIT = 56 * 1024 * 1024


def _dot(a, b):
    return jnp.dot(a.astype(BF16), b.astype(BF16), preferred_element_type=F32)


def _split(x):
    hi = x.astype(BF16)
    lo = (x - hi.astype(F32)).astype(BF16)
    return hi, lo


def _stack_rhs3(b):
    bh, bl = _split(b)
    return jnp.concatenate([bh, bh, bl], axis=0)


def _dot3_stacked(a, b_stacked):
    ah, al = _split(a)
    return jnp.dot(jnp.concatenate([ah, al, ah], axis=1), b_stacked, preferred_element_type=F32)


def _dot2_stacked(a, b_stacked):
    ah, al = _split(a)
    return jnp.dot(jnp.concatenate([ah, al], axis=1), b_stacked, preferred_element_type=F32)


def _head_sums(x, ones_bd):
    w = ones_bd.shape[0]
    return jnp.concatenate([jnp.dot(x[:, c:c + w].astype(BF16), ones_bd,
                                    preferred_element_type=F32)
                            for c in range(0, x.shape[1], w)], axis=1)


def _dot_exact_lhs(a_bf16, b):
    bh, bl = _split(b)
    bm = b - bh.astype(F32) - bl.astype(F32)
    return (jnp.dot(a_bf16, bh, preferred_element_type=F32)
            + jnp.dot(a_bf16, bl, preferred_element_type=F32)
            + jnp.dot(a_bf16, bm.astype(BF16), preferred_element_type=F32))


def _rms(x, g):
    return x * lax.rsqrt(jnp.mean(x * x, axis=-1, keepdims=True) + EPS) * g


def _sigmoid(x):
    return 1.0 / (1.0 + jnp.exp(-x))


def _full_spec(shape):
    nd = len(shape)
    return pl.BlockSpec(shape, lambda *_: (0,) * nd, pipeline_mode=pl.Buffered(1))


def _branch_a_kernel(x_ref, g1_ref, wu_ref, wv_ref, wga_ref, vg_ref, vb_ref, wmix_ref,
                     bias_ref, wao_ref, apart_ref, vn_ref):
    xn = _rms(x_ref[...], g1_ref[...]).astype(BF16)
    v = jnp.dot(xn, wv_ref[...], preferred_element_type=F32)
    mu = jnp.mean(v, axis=-1, keepdims=True)
    d = v - mu
    var = jnp.mean(d * d, axis=-1, keepdims=True)
    vn = d * lax.rsqrt(var + EPS) * vg_ref[...] + vb_ref[...]
    vn_ref[...] = vn
    vnb = vn.astype(BF16)
    c_a = vnb.shape[1] // G_A
    rb = wmix_ref.shape[1]
    mixed = jnp.concatenate(
        [jnp.concatenate(
            [jnp.dot(wmix_ref[g], vnb[r0:r0 + rb, g * c_a:(g + 1) * c_a],
                     preferred_element_type=F32) for g in range(G_A)], axis=1)
         for r0 in range(0, vnb.shape[0], rb)], axis=0) + bias_ref[...]
    u = jnp.dot(xn, wu_ref[...], preferred_element_type=F32)
    ya = (u * mixed).astype(BF16)
    ga = jnp.dot(xn, wga_ref[...], preferred_element_type=F32)
    apart_ref[...] = _sigmoid(ga) * jnp.dot(ya, wao_ref[...], preferred_element_type=F32)


def _branch_a(x2d, tm, g1, wu, wv, wga, vg, vb, wmix, bias_full, wao):
    n, d = x2d.shape
    d_a = wu.shape[1]
    tok = lambda w: pl.BlockSpec((tm, w), lambda i: (i, 0))
    return pl.pallas_call(
        _branch_a_kernel,
        grid=(n // tm,),
        in_specs=[tok(d), _full_spec(g1.shape), _full_spec(wu.shape), _full_spec(wv.shape),
                  _full_spec(wga.shape), _full_spec(vg.shape), _full_spec(vb.shape),
                  _full_spec(wmix.shape), _full_spec(bias_full.shape), _full_spec(wao.shape)],
        out_specs=[tok(d), tok(d_a)],
        out_shape=[jax.ShapeDtypeStruct((n, d), F32), jax.ShapeDtypeStruct((n, d_a), F32)],
        compiler_params=pltpu.CompilerParams(dimension_semantics=("parallel",),
                                             vmem_limit_bytes=VMEM_LIMIT),
        name="branch_a",
    )(x2d, g1, wu, wv, wga, vg, vb, wmix, bias_full, wao)


def _branch_b_kernel(x_ref, g1_ref, wcur_ref, wgb_ref, mu_ref, ext_ref, w0_ref, w2_ref,
                     a0_ref, a2_ref, g2_ref, kk_ref, ka_ref, rk_ref, bd_ref,
                     r_out, k_out, v_out, kk_out, kka_out, lw_out, g_out, bonus_out,
                     sgb_out, cur_out, carry_scr, *, seq_rows, d_b):
    tm = x_ref.shape[0]
    xn = _rms(x_ref[...], g1_ref[...]).astype(BF16)
    cur = jnp.dot(xn, wcur_ref[...], preferred_element_type=F32)
    rolled = pltpu.roll(cur, 1, axis=0)
    row = lax.broadcasted_iota(jnp.int32, (tm, 1), 0)
    if seq_rows is None:
        first_tile = pl.program_id(1) == 0
        carry = jnp.where(first_tile, 0.0, carry_scr[...])
        prev = jnp.where(row == 0, carry, rolled)
        carry_scr[...] = cur[tm - 1:tm, :]
        cur_out[...] = cur[tm - 8:tm, :]
    else:
        prev = jnp.where(row % seq_rows == 0, ext_ref[...], rolled)
        cur_out[...] = cur
    xs = cur + (prev - cur) * mu_ref[...]
    r = xs[:, 0:d_b]
    k = xs[:, d_b:2 * d_b]
    v = xs[:, 2 * d_b:3 * d_b]
    o = 3 * d_b
    xw = xs[:, o:o + R_W]
    xa = xs[:, o + R_W:o + R_W + R_A]
    xg = xs[:, o + R_W + R_A:o + R_W + R_A + R_G]
    wl = w0_ref[...] + _dot3_stacked(jnp.tanh(xw), w2_ref[...])
    lw = -math.exp(-0.5) * _sigmoid(wl)
    a = _sigmoid(a0_ref[...] + _dot3_stacked(xa, a2_ref[...]))
    g = _dot2_stacked(_sigmoid(xg), g2_ref[...])
    bd = bd_ref[...]
    kkr = k * kk_ref[...]
    n2 = _head_sums(kkr * kkr, bd)
    kk = kkr * lax.rsqrt(jnp.maximum(n2, 1e-24))
    k2 = k * (1.0 + (a - 1.0) * ka_ref[...])
    bonus = _head_sums(r * k2 * rk_ref[...], bd) * v
    r_out[...] = r
    k_out[...] = k2
    v_out[...] = v
    kk_out[...] = kk
    kka_out[...] = kk * a
    lw_out[...] = lw
    g_out[...] = g
    bonus_out[...] = bonus
    sgb_out[...] = _sigmoid(jnp.dot(xn, wgb_ref[...], preferred_element_type=F32))


def _branch_b(x3d, tm, seq_rows, ext, g1, wcur, wgb, mu, w0, w2, a0, a2, g2, k_k, k_a, r_k, bd):
    nb, t, d = x3d.shape
    d_b = w0.shape[1]
    shift_w = wcur.shape[1]
    nt = t // tm
    tok = lambda w: pl.BlockSpec((None, tm, w), lambda b, i: (b, i, 0))
    cur_rows = 8 if seq_rows is None else tm
    outs = [jax.ShapeDtypeStruct((nb, t, d_b), F32)] * 8 + [
        jax.ShapeDtypeStruct((nb, t, d), F32),
        jax.ShapeDtypeStruct((nb, nt * cur_rows, shift_w), F32)]
    out_specs = [tok(d_b)] * 8 + [tok(d), pl.BlockSpec((None, cur_rows, shift_w),
                                                       lambda b, i: (b, i, 0))]
    weights = (g1, wcur, wgb, mu)
    small = (w0, w2, a0, a2, g2, k_k, k_a, r_k, bd)
    return pl.pallas_call(
        functools.partial(_branch_b_kernel, seq_rows=seq_rows, d_b=d_b),
        grid=(nb, nt),
        in_specs=[tok(d)] + [_full_spec(w.shape) for w in weights]
        + [tok(shift_w) if seq_rows is not None else _full_spec(ext.shape)]
        + [_full_spec(w.shape) for w in small],
        out_specs=out_specs,
        out_shape=outs,
        scratch_shapes=[pltpu.VMEM((1, shift_w), F32)],
        compiler_params=pltpu.CompilerParams(dimension_semantics=("parallel", "arbitrary"),
                                             vmem_limit_bytes=VMEM_LIMIT),
        name="branch_b",
    )(x3d, *weights, ext, *small)


def _scan_kernel(r_ref, k_ref, v_ref, kk_ref, kka_ref, lw_ref, y_ref, sout_ref, s_scr, *, chunk):
    L = chunk
    W2 = 2 * L
    c = pl.program_id(1)
    n_seq, n_pairs = s_scr.shape[:2]

    @pl.when(c == 0)
    def _():
        s_scr[...] = jnp.zeros_like(s_scr)

    iota = lambda shape, dim: lax.broadcasted_iota(jnp.int32, shape, dim)
    stack_mask = (iota((W2, 1), 0) >= L) == (iota((1, PAIR), 1) >= HEAD_B)
    bd_mask = (iota((W2, 1), 0) >= L) == (iota((1, W2), 1) >= L)
    assert L & (L - 1) == 0, "chunk length must be a power of two"
    rw = iota((L, W2), 0)
    cw = iota((L, W2), 1) & (L - 1)
    strict_w = cw < rw
    eye_w = (cw == rw).astype(F32)
    incl_w2 = (iota((L, 2 * W2), 1) & (L - 1)) <= iota((L, 2 * W2), 0)
    tri = (iota((L, L), 1) <= iota((L, L), 0)).astype(BF16)
    di = iota((PAIR, PAIR), 0)
    dj = iota((PAIR, PAIR), 1)
    diag = di == dj
    head_diag = (di >= HEAD_B) == (dj >= HEAD_B)

    def stack(x):
        return jnp.where(stack_mask, jnp.concatenate([x, x], axis=0), 0.0)

    def bd(xw):
        return jnp.where(bd_mask, jnp.concatenate([xw, xw], axis=0), 0.0)

    n_double = max(L.bit_length() - 2, 0)
    prep = []
    for s in range(n_seq):
        lw_all = lw_ref[s]
        cum_all = _dot_exact_lhs(tri, lw_all)
        for p in range(n_pairs):
            sl = slice(p * PAIR, (p + 1) * PAIR)
            cum = cum_all[:, sl]
            cum_last = cum[L - 1:L, :]
            e_neg = jnp.exp(-cum)
            e_rel = jnp.exp(cum_last - cum)
            kk = kk_ref[s, :, sl]
            kka = kka_ref[s, :, sl]
            kx = k_ref[s, :, sl]
            prep.append(dict(
                s=s, p=p, sl=sl, ab=-kk * jnp.exp(cum - lw_all[:, sl]),
                rb=r_ref[s, :, sl] * jnp.exp(cum), bb=kka * e_neg, kb=kx * e_neg,
                bt=kka * e_rel, kt=kx * e_rel, v=v_ref[s, :, sl], decay=jnp.exp(cum_last)))
    a_w = [lax.dot_general(
        jnp.concatenate([q["ab"], q["rb"]], axis=0).astype(BF16),
        jnp.concatenate([stack(q["bb"]), stack(q["kb"])], axis=0).astype(BF16),
        (((1,), (1,)), ((), ())), preferred_element_type=F32) for q in prep]
    nw = [jnp.where(strict_w, a[:L, :W2], 0.0) for a in a_w]
    tw = [eye_w + n for n in nw]
    if n_double:
        sw = [_dot(n, bd(n)) for n in nw]
        for _ in range(n_double - 1):
            prod = [_dot(jnp.concatenate([t, s], axis=0), bd(s)) for t, s in zip(tw, sw)]
            tw = [t + p[:L] for t, p in zip(tw, prod)]
            sw = [p[L:] for p in prod]
        tw = [t + _dot(t, bd(s)) for t, s in zip(tw, sw)]
    akv = [_dot(jnp.where(strict_w, a[:L, W2:], 0.0), stack(q["v"])) for a, q in zip(a_w, prep)]
    tx = [_dot(t, jnp.concatenate([stack(q["ab"]), stack(kv)], axis=1))
          for t, q, kv in zip(tw, prep, akv)]
    ry = [_dot(jnp.where(incl_w2, a[L:], 0.0),
               jnp.concatenate(
                   [jnp.concatenate([stack(x[:, :PAIR]), stack(x[:, PAIR:])], axis=1),
                    jnp.concatenate([jnp.zeros((W2, PAIR), F32), stack(q["v"])], axis=1)], axis=0))
          for a, x, q in zip(a_w, tx, prep)]
    for q, x, y in zip(prep, tx, ry):
        st = s_scr[q["s"], q["p"]]
        us = _dot(jnp.concatenate([x[:, :PAIR], q["rb"] + y[:, :PAIR]], axis=0), st)
        y_ref[q["s"], :, q["sl"]] = us[L:] + y[:, PAIR:]
        u = us[:L] + x[:, PAIR:]
        lhs = jnp.concatenate([jnp.where(diag, q["decay"], 0.0),
                               jnp.concatenate([q["bt"], q["kt"]], axis=0).T], axis=1)
        g = _dot(lhs, jnp.concatenate([st, u, q["v"]], axis=0))
        s_scr[q["s"], q["p"]] = jnp.where(head_diag, g, 0.0)

    @pl.when(c == pl.num_programs(1) - 1)
    def _():
        for s in range(n_seq):
            for p in range(n_pairs):
                nat = s_scr[s, p].T
                sout_ref[s, 2 * p] = nat[:HEAD_B, :HEAD_B]
                sout_ref[s, 2 * p + 1] = nat[HEAD_B:, HEAD_B:]


SCAN_CHUNK = 64
SCAN_SEQS = 8


def _scan(r, k, v, kk, kka, lw):
    nb, t, d_b = r.shape
    n_pairs = d_b // PAIR
    n_heads = d_b // HEAD_B
    tok = pl.BlockSpec((SCAN_SEQS, SCAN_CHUNK, d_b), lambda b, c: (b, c, 0))
    st = pl.BlockSpec((SCAN_SEQS, n_heads, HEAD_B, HEAD_B), lambda b, c: (b, 0, 0, 0))
    return pl.pallas_call(
        functools.partial(_scan_kernel, chunk=SCAN_CHUNK),
        grid=(nb // SCAN_SEQS, t // SCAN_CHUNK),
        in_specs=[tok] * 6,
        out_specs=[tok, st],
        out_shape=[jax.ShapeDtypeStruct((nb, t, d_b), F32),
                   jax.ShapeDtypeStruct((nb, n_heads, HEAD_B, HEAD_B), F32)],
        scratch_shapes=[pltpu.VMEM((SCAN_SEQS, n_pairs, PAIR, PAIR), F32)],
        compiler_params=pltpu.CompilerParams(dimension_semantics=("parallel", "arbitrary"),
                                             vmem_limit_bytes=VMEM_LIMIT),
        name="rwkv_scan",
    )(r, k, v, kk, kka, lw)


STEP_ROWS = 8


def _steps_kernel(r_ref, k_ref, v_ref, kk_ref, kka_ref, lw_ref, s0_ref, y_ref, s_ref,
                  vec_scr, yt_scr, *, n_steps):
    n_seq = s_ref.shape[-1]
    s_ref[...] = s0_ref[...]
    for t in range(n_steps):
        rows = pl.ds(t, n_seq, stride=n_steps)
        vecs = {}
        for slot, (name, ref) in enumerate((("r", r_ref), ("k", k_ref), ("v", v_ref),
                                            ("kk", kk_ref), ("kka", kka_ref), ("lw", lw_ref))):
            vecs[name] = ref[rows, :].T
            vec_scr[slot] = vecs[name]
        for q in range(PAIR // HEAD_B):
            hs = slice(q * HEAD_B, (q + 1) * HEAD_B)
            r_q, k_q, kka_q = vecs["r"][hs], vecs["k"][hs], vecs["kka"][hs]
            nkk_q = -vecs["kk"][hs]
            w_q = jnp.exp(vecs["lw"][hs])

            def rows_step(i0, carry):
                for u in range(STEP_ROWS):
                    i = i0 * STEP_ROWS + u
                    s_row = s_ref[q, i]
                    sa = jnp.sum(s_row * nkk_q, axis=0, keepdims=True)
                    v_i = vec_scr[2, pl.ds(q * HEAD_B + i, 1), :]
                    s_new = s_row * w_q + sa * kka_q + v_i * k_q
                    s_ref[q, i] = s_new
                    yt_scr[pl.ds(q * HEAD_B + i, 1), :] = jnp.sum(s_new * r_q, axis=0,
                                                                  keepdims=True)
                return carry

            lax.fori_loop(0, HEAD_B // STEP_ROWS, rows_step, 0)
        y_ref[rows, :] = yt_scr[...].T


def _scan_steps(r, k, v, kk, kka, lw, s0, n_steps):
    n, d_b = r.shape
    n_heads, _, _, n_seq = s0.shape
    tok = pl.BlockSpec((n, PAIR), lambda p: (0, p))
    st = pl.BlockSpec((PAIR // HEAD_B, HEAD_B, HEAD_B, n_seq), lambda p: (p, 0, 0, 0))
    return pl.pallas_call(
        functools.partial(_steps_kernel, n_steps=n_steps),
        grid=(d_b // PAIR,),
        in_specs=[tok] * 6 + [st],
        out_specs=[tok, st],
        out_shape=[jax.ShapeDtypeStruct((n, d_b), F32), jax.ShapeDtypeStruct(s0.shape, F32)],
        scratch_shapes=[pltpu.VMEM((6, PAIR, n_seq), F32), pltpu.VMEM((PAIR, n_seq), F32)],
        compiler_params=pltpu.CompilerParams(dimension_semantics=("parallel",),
                                             vmem_limit_bytes=VMEM_LIMIT),
        name="rwkv_steps",
    )(r, k, v, kk, kka, lw, s0)


def _pack_bf16_pairs(x):
    w = x.shape[1] // 2
    bits = lambda v: lax.bitcast_convert_type(v.astype(BF16).astype(F32), jnp.uint32)
    return (bits(x[:, :w]) >> 16) | (bits(x[:, w:]) & jnp.uint32(0xFFFF0000))


def _unpack_bf16_pairs(u):
    lo = lax.bitcast_convert_type(u << 16, F32)
    hi = lax.bitcast_convert_type(u & jnp.uint32(0xFFFF0000), F32)
    return lo, hi


def _merge_kernel(y_ref, g_ref, bonus_ref, sgb_ref, apart_ref, x_ref, lg_ref, lb_ref, bd_ref,
                  wbo_ref, wo_ref, g2_ref, rwt_ref, rb_ref,
                  h_out, xp_out, topi_out, prob_out, rank_out, cnt_out, cnt_scr):
    step = pl.program_id(0)

    @pl.when(step == 0)
    def _():
        cnt_scr[...] = jnp.zeros_like(cnt_scr)

    bd = bd_ref[...]
    y = y_ref[...]
    inv_n = 1.0 / HEAD_B
    mu = _head_sums(y, bd) * inv_n
    d = y - mu
    var = _head_sums(d * d, bd) * inv_n
    yn = d * lax.rsqrt(var + GN_EPS) * lg_ref[...] + lb_ref[...]
    yb = ((yn + bonus_ref[...]) * g_ref[...]).astype(BF16)
    merged = apart_ref[...] + sgb_ref[...] * jnp.dot(yb, wbo_ref[...], preferred_element_type=F32)
    h = x_ref[...] + jnp.dot(merged.astype(BF16), wo_ref[...], preferred_element_type=F32)
    h_out[...] = h
    xn2 = _rms(h, g2_ref[...])
    xp_out[...] = _pack_bf16_pairs(xn2)
    n_e = rb_ref.shape[0]
    tm = xn2.shape[0]
    xh, xl = _split(xn2)
    rwt = rwt_ref[...]
    nt = (((1,), (1,)), ((), ()))
    t = lax.dot_general(rwt, xh, nt, preferred_element_type=F32)
    logits = (t[:n_e] + t[n_e:] + lax.dot_general(rwt[:n_e], xl, nt, preferred_element_type=F32)
              + rb_ref[...])
    idx = lax.broadcasted_iota(jnp.int32, logits.shape, 0).astype(F32)
    work = logits
    tops, hots, sels = [], [], []
    for _ in range(TOP_K):
        m = jnp.max(work, axis=0, keepdims=True)
        sel = jnp.min(jnp.where(work == m, idx, float(n_e)), axis=0, keepdims=True)
        hot = idx == sel
        tops.append(m)
        hots.append(hot)
        sels.append(sel)
        work = jnp.where(hot, -jnp.inf, work)
    es = [jnp.exp(t - tops[0]) for t in tops]
    denom = es[0] + es[1] + es[2] + es[3]
    topi_out[...] = jnp.concatenate(sels, axis=0).astype(jnp.int32)
    prob_out[...] = jnp.concatenate([e / denom for e in es], axis=0)
    hot_any = jnp.zeros_like(logits)
    for hot in hots:
        hot_any = hot_any + hot.astype(F32)
    ri = lax.broadcasted_iota(jnp.int32, (tm, tm), 0)
    ci = lax.broadcasted_iota(jnp.int32, (tm, tm), 1)
    before = _dot(hot_any, (ri < ci).astype(BF16)) + cnt_scr[...]
    rank_out[...] = jnp.concatenate(
        [jnp.sum(jnp.where(hot, before, 0.0), axis=0, keepdims=True) for hot in hots],
        axis=0).astype(jnp.int32)
    total = cnt_scr[...] + jnp.sum(hot_any, axis=1, keepdims=True)
    cnt_scr[...] = total
    cnt_out[...] = total.astype(jnp.int32)


def _merge(y, g, bonus, sgb, apart, x2d, tm, lnx_g, lnx_b, bd, wbo, wo, g2n, rw, rb):
    n, d = x2d.shape
    d_b = y.shape[1]
    n_e = rb.shape[0]
    tok = lambda w: pl.BlockSpec((tm, w), lambda i: (i, 0))
    per_k = pl.BlockSpec((TOP_K, tm), lambda i: (0, i))
    weights = (lnx_g, lnx_b, bd, wbo, wo, g2n, rw, rb)
    return pl.pallas_call(
        _merge_kernel,
        grid=(n // tm,),
        in_specs=[tok(d_b), tok(d_b), tok(d_b), tok(d), tok(d), tok(d)]
        + [_full_spec(w.shape) for w in weights],
        out_specs=[tok(d), tok(d // 2), per_k, per_k, per_k,
                   pl.BlockSpec((n_e, 1), lambda i: (0, 0))],
        out_shape=[jax.ShapeDtypeStruct((n, d), F32),
                   jax.ShapeDtypeStruct((n, d // 2), jnp.uint32),
                   jax.ShapeDtypeStruct((TOP_K, n), jnp.int32),
                   jax.ShapeDtypeStruct((TOP_K, n), F32),
                   jax.ShapeDtypeStruct((TOP_K, n), jnp.int32),
                   jax.ShapeDtypeStruct((n_e, 1), jnp.int32)],
        scratch_shapes=[pltpu.VMEM((n_e, 1), F32)],
        compiler_params=pltpu.CompilerParams(dimension_semantics=("arbitrary",),
                                             vmem_limit_bytes=VMEM_LIMIT),
        name="merge_router",
    )(y, g, bonus, sgb, apart, x2d, *weights)


MOE_TM = 1024
MOE_SUB = 128
MOE_FF_CHUNK = 512
CAST_ROWS = 256


def _route(topi_p, rank_p, cnt_p, topi_s, rank_s, cnt_s):
    n_e = cnt_p.size
    cnt_p, cnt_s = cnt_p.reshape(n_e), cnt_s.reshape(n_e)
    cnt = cnt_p + cnt_s
    padded = (cnt + MOE_TM - 1) // MOE_TM * MOE_TM
    ends = jnp.cumsum(padded)
    base = ends - padded
    lookup = lambda table, idx: jnp.sum(
        jnp.where(idx[..., None] == jnp.arange(n_e), table, 0), axis=-1)
    pos_t = jnp.concatenate([lookup(base, topi_p) + rank_p,
                             lookup(base + cnt_p, topi_s) + rank_s], axis=1)
    return (pos_t.astype(jnp.int32), (base // MOE_TM).astype(jnp.int32),
            cnt.astype(jnp.int32))


def _moe_kernel(tile0_ref, cnt_ref, xs_hbm, wgu_ref, bgu_ref, wd_ref, bdn_ref, ys_hbm,
                wgu_b, wd_b, xbuf, ybuf, sem_in, sem_out, done_ref):
    e = pl.program_id(0)
    n_e = pl.num_programs(0)

    @pl.when(e == 0)
    def _():
        done_ref[0] = 0

    g0 = done_ref[0]
    tile0 = tile0_ref[e]
    cnt = cnt_ref[e]
    n_t = (cnt + MOE_TM - 1) // MOE_TM
    d, gu = wgu_ref.shape
    d_ff = wd_ref.shape[0]
    half = d // 2

    def in_copy(tile, slot):
        return pltpu.make_async_copy(xs_hbm.at[pl.ds(tile * MOE_TM, MOE_TM)], xbuf.at[slot],
                                     sem_in.at[slot])

    def out_copy(tile, slot):
        return pltpu.make_async_copy(ybuf.at[slot], ys_hbm.at[pl.ds(tile * MOE_TM, MOE_TM)],
                                     sem_out.at[slot])

    @pl.when(jnp.logical_and(e == 0, n_t > 0))
    def _():
        in_copy(tile0, 0).start()

    @pl.when(n_t > 0)
    def _():
        for r0 in range(0, d, CAST_ROWS):
            wgu_b[r0:r0 + CAST_ROWS, :] = wgu_ref[r0:r0 + CAST_ROWS, :].astype(BF16)
        for r0 in range(0, d_ff, CAST_ROWS):
            wd_b[r0:r0 + CAST_ROWS, :] = wd_ref[r0:r0 + CAST_ROWS, :].astype(BF16)

    def tile_step(j, carry):
        slot = (g0 + j) % 2
        in_copy(tile0 + j, slot).wait()

        @pl.when(j + 1 < n_t)
        def _():
            in_copy(tile0 + j + 1, 1 - slot).start()

        @pl.when(g0 + j >= 2)
        def _():
            out_copy(0, slot).wait()

        left = cnt - j * MOE_TM

        def ffn(row0, rows):
            lo, hi = _unpack_bf16_pairs(xbuf[slot, pl.ds(row0, rows)])
            valid = lax.broadcasted_iota(jnp.int32, lo.shape, 0) < left - row0
            x_lo = jnp.where(valid, lo, 0.0).astype(BF16)
            x_hi = jnp.where(valid, hi, 0.0).astype(BF16)

            def proj(c0):
                cols = slice(c0, c0 + MOE_FF_CHUNK)
                return (jnp.dot(x_lo, wgu_b[:half, cols], preferred_element_type=F32)
                        + jnp.dot(x_hi, wgu_b[half:, cols], preferred_element_type=F32)
                        + bgu_ref[:, cols])

            y = jnp.zeros((rows, d), F32)
            for f in range(d_ff // MOE_FF_CHUNK):
                c0 = f * MOE_FF_CHUNK
                gate = jnp.minimum(proj(c0), SWIGLU_LIMIT)
                up = jnp.clip(proj(d_ff + c0), -SWIGLU_LIMIT, SWIGLU_LIMIT)
                hh = (up + 1.0) * gate * _sigmoid(gate * SWIGLU_ALPHA)
                y = y + jnp.dot(hh.astype(BF16), wd_b[c0:c0 + MOE_FF_CHUNK, :],
                                preferred_element_type=F32)
            ybuf[slot, pl.ds(row0, rows)] = _pack_bf16_pairs(y + bdn_ref[...])

        n_sub = (jnp.minimum(left, MOE_TM) + MOE_SUB - 1) // MOE_SUB

        @pl.when(n_sub == MOE_TM // MOE_SUB)
        def _():
            ffn(0, MOE_TM)

        @pl.when(n_sub < MOE_TM // MOE_SUB)
        def _():
            def sub_step(i, c):
                ffn(pl.multiple_of(i * 2 * MOE_SUB, 2 * MOE_SUB), 2 * MOE_SUB)
                return c
            lax.fori_loop(0, n_sub // 2, sub_step, 0)

            @pl.when(n_sub % 2 == 1)
            def _():
                ffn(pl.multiple_of((n_sub // 2) * 2 * MOE_SUB, 2 * MOE_SUB), MOE_SUB)

        out_copy(tile0 + j, slot).start()
        return carry

    lax.fori_loop(0, n_t, tile_step, 0)

    g1 = g0 + n_t
    done_ref[0] = g1
    nxt = jnp.minimum(e + 1, n_e - 1)

    @pl.when(jnp.logical_and(e + 1 < n_e, cnt_ref[nxt] > 0))
    def _():
        in_copy(tile0_ref[nxt], g1 % 2).start()

    @pl.when(jnp.logical_and(e == n_e - 1, g1 >= 2))
    def _():
        out_copy(0, g1 % 2).wait()

    @pl.when(jnp.logical_and(e == n_e - 1, g1 >= 1))
    def _():
        out_copy(0, (g1 - 1) % 2).wait()


def _moe(xs, tile0, cnt, wgu, bgu, wd, bdn):
    p_rows, half = xs.shape
    n_e, d, gu = wgu.shape
    d_ff = wd.shape[1]
    w_blk = lambda e, tile0, cnt: (e, 0, 0)
    grid_spec = pltpu.PrefetchScalarGridSpec(
        num_scalar_prefetch=2,
        grid=(n_e,),
        in_specs=[pl.BlockSpec(memory_space=pl.ANY),
                  pl.BlockSpec((None, d, gu), w_blk),
                  pl.BlockSpec((None, 1, gu), w_blk),
                  pl.BlockSpec((None, d_ff, d), w_blk),
                  pl.BlockSpec((None, 1, d), w_blk)],
        out_specs=pl.BlockSpec(memory_space=pl.ANY),
        scratch_shapes=[pltpu.VMEM((d, gu), BF16), pltpu.VMEM((d_ff, d), BF16),
                        pltpu.VMEM((2, MOE_TM, half), jnp.uint32),
                        pltpu.VMEM((2, MOE_TM, half), jnp.uint32),
                        pltpu.SemaphoreType.DMA((2,)), pltpu.SemaphoreType.DMA((2,)),
                        pltpu.SMEM((1,), jnp.int32)])
    return pl.pallas_call(
        _moe_kernel,
        grid_spec=grid_spec,
        out_shape=jax.ShapeDtypeStruct((p_rows, half), jnp.uint32),
        compiler_params=pltpu.CompilerParams(dimension_semantics=("arbitrary",),
                                             vmem_limit_bytes=VMEM_LIMIT),
        name="moe_experts",
    )(tile0, cnt, xs, wgu, bgu, wd, bdn)


def _combine_kernel(h_ref, yg_ref, prob_ref, gf_ref, out_ref):
    tm = h_ref.shape[0]
    half = h_ref.shape[1] // 2
    p_rows = jnp.concatenate([prob_ref[...], jnp.zeros((8 - TOP_K, tm), F32)], axis=0)
    ph = p_rows.astype(BF16)
    pm, pl_ = _split(p_rows - ph.astype(F32))
    eye = (lax.broadcasted_iota(jnp.int32, (tm, tm), 0)
           == lax.broadcasted_iota(jnp.int32, (tm, tm), 1)).astype(BF16)
    nt = (((1,), (1,)), ((), ()))
    prob = sum(lax.dot_general(eye, piece, nt, preferred_element_type=F32)
               for piece in (ph, pm, pl_))
    acc_lo = jnp.zeros((tm, half), F32)
    acc_hi = jnp.zeros((tm, half), F32)
    for k in range(TOP_K):
        lo, hi = _unpack_bf16_pairs(yg_ref[k])
        pk = prob[:, k:k + 1]
        acc_lo = acc_lo + pk * lo
        acc_hi = acc_hi + pk * hi
    z = h_ref[...] + jnp.concatenate([acc_lo, acc_hi], axis=1)
    out_ref[...] = _rms(z, gf_ref[...])


def _combine(h, yg, row0, prob, gf, tm):
    n, d = h.shape
    tok = lambda w: pl.BlockSpec((tm, w), lambda i: (i, 0))
    t0 = row0 // tm
    return pl.pallas_call(
        _combine_kernel,
        grid=(n // tm,),
        in_specs=[tok(d), pl.BlockSpec((TOP_K, tm, d // 2), lambda i: (0, i + t0, 0)),
                  pl.BlockSpec((TOP_K, tm), lambda i: (0, i)), _full_spec(gf.shape)],
        out_specs=tok(d),
        out_shape=jax.ShapeDtypeStruct((n, d), F32),
        compiler_params=pltpu.CompilerParams(dimension_semantics=("parallel",),
                                             vmem_limit_bytes=VMEM_LIMIT),
        name="moe_combine",
    )(h, yg, prob, gf)


SC_CORES = 2
SC_SUBCORES = 16
SC_WORKERS = SC_CORES * SC_SUBCORES
SC_MAX_INDEX = 128


def _sc_chunk(rows_per_worker, limit=SC_MAX_INDEX):
    for c in range(limit, 7, -8):
        if rows_per_worker % c == 0:
            return c
    raise ValueError(f"no 8-aligned chunk divides {rows_per_worker} rows")


def _sc_mesh():
    return plsc.VectorSubcoreMesh(core_axis_name="c", subcore_axis_name="s")


def _sc_worker():
    return lax.axis_index("s") * SC_CORES + lax.axis_index("c")


def _scatter_rows(xs_in, pos_t, p_rows):
    w = xs_in[0].shape[1]
    dtype = xs_in[0].dtype
    n_k, n = pos_t.shape
    pos_flat = pos_t.reshape(n_k * n)
    parts, row0 = [], 0
    for x in xs_in:
        per = x.shape[0] // SC_WORKERS
        parts.append((row0, per, _sc_chunk(per, SC_MAX_INDEX // 2)))
        row0 += x.shape[0]
    max_chunk = max(c for _, _, c in parts)
    max_per = max(p for _, p, _ in parts)

    def body(*refs):
        x_hbms = refs[:len(xs_in)]
        pos_hbm, out_hbm, idx_v, rows_v, sem_l, sem_s = refs[len(xs_in):]
        for x_hbm, (tok0, per, chunk) in zip(x_hbms, parts):
            n_chunks = per // chunk
            base = pl.multiple_of(_sc_worker() * per, 8)
            for k in range(n_k):
                src = pl.multiple_of(k * n + tok0 + base, 8)
                pltpu.sync_copy(pos_hbm.at[pl.ds(src, per)], idx_v.at[pl.ds(k * per, per)])

            def load(c, slot):
                src = x_hbm.at[pl.ds(pl.multiple_of(base + c * chunk, 8), chunk)]
                return pltpu.make_async_copy(src, rows_v.at[slot, pl.ds(0, chunk)], sem_l.at[slot])

            def scatter(c, k, slot):
                rows = idx_v.at[pl.ds(pl.multiple_of(k * per + c * chunk, 8), chunk)]
                return pltpu.make_async_copy(rows_v.at[slot, pl.ds(0, chunk)], out_hbm.at[rows],
                                             sem_s.at[slot])

            load(0, 0).start()

            @pl.loop(0, n_chunks)
            def _(c):
                slot = c % 2
                load(c, slot).wait()

                @pl.when(c + 1 < n_chunks)
                def _():
                    @pl.when(c >= 1)
                    def _():
                        for k in range(n_k):
                            scatter(c - 1, k, 1 - slot).wait()
                    load(c + 1, 1 - slot).start()

                for k in range(n_k):
                    scatter(c, k, slot).start()

            if n_chunks >= 2:
                for k in range(n_k):
                    scatter(n_chunks - 2, k, n_chunks % 2).wait()
            for k in range(n_k):
                scatter(n_chunks - 1, k, (n_chunks - 1) % 2).wait()

    return pl.kernel(
        body, out_type=jax.ShapeDtypeStruct((p_rows, w), dtype), mesh=_sc_mesh(),
        scratch_types=[pltpu.VMEM((n_k * max_per,), jnp.int32),
                       pltpu.VMEM((2, max_chunk, w), dtype),
                       pltpu.SemaphoreType.DMA((2,)), pltpu.SemaphoreType.DMA((2,))],
        name="sc_scatter_rows")(*xs_in, pos_flat)


def _gather_rows(table, idx):
    n = idx.shape[0]
    w = table.shape[1]
    per = n // SC_WORKERS
    chunk = _sc_chunk(per)

    n_chunks = per // chunk

    def body(table_hbm, idx_hbm, out_hbm, idx_v, rows_v, sem_g, sem_w):
        base = pl.multiple_of(_sc_worker() * per, 8)
        pltpu.sync_copy(idx_hbm.at[pl.ds(base, per)], idx_v)

        def gather(c, slot):
            rows = idx_v.at[pl.ds(pl.multiple_of(c * chunk, 8), chunk)]
            return pltpu.make_async_copy(table_hbm.at[rows], rows_v.at[slot], sem_g.at[slot])

        def write(c, slot):
            dst = out_hbm.at[pl.ds(pl.multiple_of(base + c * chunk, 8), chunk)]
            return pltpu.make_async_copy(rows_v.at[slot], dst, sem_w.at[slot])

        gather(0, 0).start()

        @pl.loop(0, n_chunks)
        def _(c):
            slot = c % 2
            gather(c, slot).wait()

            @pl.when(c + 1 < n_chunks)
            def _():
                @pl.when(c >= 1)
                def _():
                    write(c - 1, 1 - slot).wait()
                gather(c + 1, 1 - slot).start()

            write(c, slot).start()

        if n_chunks >= 2:
            write(n_chunks - 2, n_chunks % 2).wait()
        write(n_chunks - 1, (n_chunks - 1) % 2).wait()

    return pl.kernel(
        body, out_type=jax.ShapeDtypeStruct((n, w), table.dtype), mesh=_sc_mesh(),
        scratch_types=[pltpu.VMEM((per,), jnp.int32), pltpu.VMEM((2, chunk, w), table.dtype),
                       pltpu.SemaphoreType.DMA((2,)), pltpu.SemaphoreType.DMA((2,))],
        name="sc_gather_rows")(table, idx)


def _mix_matrix(ws, block, seq_rows):
    causal = jnp.tril(jnp.ones((CHUNK, CHUNK), dtype=bool))
    w = jnp.where(causal[None], ws, 0.0)[:, :seq_rows, :seq_rows]
    eye = jnp.eye(block // seq_rows, dtype=ws.dtype)
    return jnp.einsum("ab,gts->gatbs", eye, w).reshape(ws.shape[0], block, block).astype(BF16)


TOKEN_TILE = 512


def _stream(x, shift_in, wkv_in, p, seq_rows):
    nb, t, d = x.shape
    n = nb * t
    tm = TOKEN_TILE
    x2d = x.reshape(n, d)
    rows = CHUNK if seq_rows is None else seq_rows
    wmix = _mix_matrix(p["w_spatial"], max(rows, LANES), rows)
    pos = jnp.arange(tm) % rows
    d_a = p["wu"].shape[1]
    bias_full = jnp.repeat(p["b_spatial"].T[pos], d_a // G_A, axis=1)
    apart, vn = _branch_a(x2d, tm, p["norm1_g"], p["wu"], p["wv"], p["wga"], p["vnorm_g"],
                          p["vnorm_b"], wmix, bias_full, p["w_a_out"])
    if seq_rows is None:
        xb, ext = x, jnp.zeros((1, p["wcur"].shape[1]), F32)
    else:
        xb = x2d.reshape(n // tm, tm, d)
        ext = jnp.repeat(shift_in, seq_rows, axis=0).reshape(n // tm, tm, -1)
    outs = _branch_b(xb, tm, seq_rows, ext, p["norm1_g"], p["wcur"], p["wgb"], p["mu_shift"],
                     p["w0"], p["w2"], p["a0"], p["a2"], p["g2"], p["k_k"], p["k_a"], p["r_k"],
                     p["bd"])
    r, k2, v, kk, kka, lw, g, bonus, sgb, cur = outs
    d_b = r.shape[-1]
    if seq_rows is None:
        shift_out = cur[:, -1, :]
        y, s_out = _scan(r, k2, v, kk, kka, lw)
        y2d = y.reshape(n, d_b)
    else:
        shift_out = cur.reshape(nb, t, -1)[:, -1, :]
        y2d, s_lanes = _scan_steps(*[a.reshape(n, d_b) for a in (r, k2, v, kk, kka, lw)],
                                   jnp.transpose(wkv_in, (1, 2, 3, 0)), t)
        s_out = jnp.transpose(s_lanes, (3, 0, 1, 2))
    flat = lambda a: a.reshape(n, a.shape[-1])
    routed = _merge(y2d, flat(g), flat(bonus), flat(sgb), apart, x2d, tm, p["lnx_g"],
                    p["lnx_b"], p["bd"], p["w_b_out"], p["w_out"], p["norm2_g"],
                    p["router_w"], p["router_b"])
    return routed, vn, shift_out, s_out


def kernel(x_prompt, x_sample, state_shift, state_wkv, norm1_g, w_in, mu_shift, vnorm_g, vnorm_b, w_spatial, b_spatial, w_a_out, w0, w2, a0, a2, g2, k_k, k_a, r_k, lnx_g, lnx_b, w_b_out, w_out, norm2_g, router_w, router_b, exp_w_gu, exp_b_gu, exp_w_down, exp_b_down, normf_g):
    assert w_in.shape[0] == 1, "one layer: the final norm is fused into its MoE combine"
    l = 0
    d_model = x_prompt.shape[-1]
    shift_w = mu_shift.shape[-1]
    d_a = vnorm_g.shape[-1]
    bp, tp, _ = x_prompt.shape
    bs, ts, _ = x_sample.shape
    head_id = jnp.arange(2 * LANES) // HEAD_B
    bd = (head_id[:, None] == head_id[None, :]).astype(BF16)
    row = lambda a: a.reshape(1, -1)

    wi = w_in[l].astype(BF16)
    o = shift_w
    p = dict(
        norm1_g=row(norm1_g[l]), wcur=wi[:, :o], wu=wi[:, o:o + d_a],
        wv=wi[:, o + d_a:o + 2 * d_a], wga=wi[:, o + 2 * d_a:o + 2 * d_a + d_model],
        wgb=wi[:, o + 2 * d_a + d_model:], mu_shift=row(mu_shift[l]),
        vnorm_g=row(vnorm_g[l]), vnorm_b=row(vnorm_b[l]), w_spatial=w_spatial[l],
        b_spatial=b_spatial[l], w_a_out=w_a_out[l].astype(BF16), w0=row(w0[l]),
        w2=_stack_rhs3(w2[l]), a0=row(a0[l]), a2=_stack_rhs3(a2[l]),
        g2=jnp.concatenate([g2[l].astype(BF16)] * 2, axis=0),
        k_k=row(k_k[l]), k_a=row(k_a[l]),
        r_k=row(r_k[l]), lnx_g=row(lnx_g[l]), lnx_b=row(lnx_b[l]),
        w_b_out=w_b_out[l].astype(BF16), w_out=w_out[l].astype(BF16),
        norm2_g=row(norm2_g[l]), router_w=jnp.concatenate(_split(router_w[l].T), axis=0),
        router_b=router_b[l].reshape(-1, 1), bd=bd)

    routed_p, _, sh_p, s_p = _stream(x_prompt, None, None, p, None)
    routed_s, vn_s, sh_s, s_s = _stream(x_sample, state_shift[l], state_wkv[l], p, ts)
    h_p, xp_p, topi_p, prob_p, rank_p, cnt_p = routed_p
    h_s, xp_s, topi_s, prob_s, rank_s, cnt_s = routed_s

    n_p, n_s = h_p.shape[0], h_s.shape[0]
    n_e = router_w.shape[-1]
    n_tiles = (n_p + n_s) * TOP_K // MOE_TM + n_e
    pos_t, tile0, cnt = _route(topi_p, rank_p, cnt_p, topi_s, rank_s, cnt_s)
    xs = _scatter_rows([xp_p, xp_s], pos_t, n_tiles * MOE_TM)
    ys = _moe(xs, tile0, cnt, exp_w_gu[l], exp_b_gu[l][:, None, :],
              exp_w_down[l], exp_b_down[l][:, None, :])
    yg = _gather_rows(ys, pos_t.reshape(-1)).reshape(TOP_K, n_p + n_s, d_model // 2)
    gf = row(normf_g)
    y_p = _combine(h_p, yg, 0, prob_p, gf, 2 * TOKEN_TILE).reshape(bp, tp, d_model)
    y_s = _combine(h_s, yg, n_p, prob_s, gf, TOKEN_TILE).reshape(bs, ts, d_model)
    return (y_p, y_s, sh_p[None], s_p[None], sh_s[None], s_s[None],
            vn_s.reshape(1, bs, ts, d_a))
```
